```python
import jax, jax.numpy as jnp
from jax import lax
import numpy as np

D_MODEL = 2048
BATCH = 8
SEQ = 8192
DEPTH = 1

PLE_DIM = 256
EPS = 1e-6

N_HEADS_MLA = 16
Q_LORA = 512
KV_LORA = 512
QK_NOPE = 128
QK_ROPE = 64
V_DIM = 128
QK_DIM = QK_NOPE + QK_ROPE
ROPE_THETA = 10000.0
Q_BLOCK = 128

SSM_EXPAND = 2
D_INNER = SSM_EXPAND * D_MODEL
SSM_HEADDIM = 64
N_HEADS_SSM = D_INNER // SSM_HEADDIM
SSM_GROUPS = 8
HEADS_PER_GROUP = N_HEADS_SSM // SSM_GROUPS
D_STATE = 128
CONV_WIDTH = 4
CHUNK = 256
CONV_DIM = D_INNER + 2 * SSM_GROUPS * D_STATE

D_FF = ((8 * D_MODEL // 3 + 255) // 256) * 256

IN_SPLITS = (Q_LORA, KV_LORA, QK_ROPE, D_INNER, CONV_DIM, N_HEADS_SSM, D_MODEL, D_MODEL)
D_IN_PROJ = Q_LORA + KV_LORA + QK_ROPE + D_INNER + CONV_DIM + N_HEADS_SSM + 2 * D_MODEL

kernel_name = "hybrid_mla_ssd_gated_block"


def rms_norm(x, w):
    xf = x.astype(jnp.float32)
    y = xf * lax.rsqrt(jnp.mean(xf * xf, axis=-1, keepdims=True) + EPS)
    return (y * w.astype(jnp.float32)).astype(x.dtype)


def split_cols(t, sizes):
    outs, off = [], 0
    for s in sizes:
        outs.append(t[..., off:off + s])
        off += s
    return outs


def rotary_tables(positions, dim):
    inv_freq = ROPE_THETA ** (-jnp.arange(0, dim, 2, dtype=jnp.float32) / dim)
    ang = positions.astype(jnp.float32)[..., None] * inv_freq
    return jnp.cos(ang)[:, :, None, :], jnp.sin(ang)[:, :, None, :]


def apply_rope(x, cos, sin):
    xf = x.astype(jnp.float32)
    x1, x2 = jnp.split(xf, 2, axis=-1)
    out = jnp.concatenate([x1 * cos - x2 * sin, x2 * cos + x1 * sin], axis=-1)
    return out.astype(x.dtype)


def causal_block_attention(q, k, v):
    b, s, h, dk = q.shape
    nblk = s // Q_BLOCK
    scale = dk ** -0.5
    qb = jnp.moveaxis(q.reshape(b, nblk, Q_BLOCK, h, dk), 1, 0)
    key_idx = jnp.arange(s)

    def one_block(args):
        q_blk, blk = args
        sc = jnp.einsum('bqhd,bkhd->bhqk', q_blk, k, preferred_element_type=jnp.float32) * scale
        q_idx = blk * Q_BLOCK + jnp.arange(Q_BLOCK)
        mask = key_idx[None, :] <= q_idx[:, None]
        sc = jnp.where(mask[None, None], sc, -jnp.inf)
        prob = jax.nn.softmax(sc, axis=-1).astype(v.dtype)
        return jnp.einsum('bhqk,bkhd->bqhd', prob, v)

    o = lax.map(one_block, (qb, jnp.arange(nblk)))
    return jnp.moveaxis(o, 0, 1).reshape(b, s, h * v.shape[-1])


def mla_branch(c_q, c_kv, k_r, cos, sin, q_norm, w_uq, kv_norm, w_ukv):
    b, s, _ = c_q.shape
    q = (rms_norm(c_q, q_norm) @ w_uq).reshape(b, s, N_HEADS_MLA, QK_DIM)
    q_nope, q_pe = q[..., :QK_NOPE], apply_rope(q[..., QK_NOPE:], cos, sin)
    kv = (rms_norm(c_kv, kv_norm) @ w_ukv).reshape(b, s, N_HEADS_MLA, QK_NOPE + V_DIM)
    k_nope, v = kv[..., :QK_NOPE], kv[..., QK_NOPE:]
    k_pe = apply_rope(k_r[:, :, None, :], cos, sin)
    k = jnp.concatenate([k_nope, jnp.broadcast_to(k_pe, (b, s, N_HEADS_MLA, QK_ROPE))], axis=-1)
    q = jnp.concatenate([q_nope, q_pe], axis=-1)
    return causal_block_attention(q, k, v)


def causal_depthwise_conv(t, w, bias):
    out = lax.conv_general_dilated(
        t, w[:, None, :].astype(t.dtype), window_strides=(1,),
        padding=[(CONV_WIDTH - 1, 0)], dimension_numbers=('NWC', 'WIO', 'NWC'),
        feature_group_count=t.shape[-1])
    return out + bias


def ssd_chunked(x, dt, a, bm, cm):
    b, s, g, r, p = x.shape
    n = bm.shape[-1]
    nc = -(-s // CHUNK)
    pad = nc * CHUNK - s
    padw = lambda t: jnp.pad(t, [(0, 0), (0, pad)] + [(0, 0)] * (t.ndim - 2))
    xdt = padw(x * dt[..., None])
    da = padw(dt * a)
    bm, cm = padw(bm), padw(cm)
    chunks = lambda t: jnp.moveaxis(t.reshape((b, nc, CHUNK) + t.shape[2:]), 1, 0)
    causal = jnp.tril(jnp.ones((CHUNK, CHUNK), dtype=bool))[None, :, :, None, None]

    def step(state, inp):
        xc, ac, bc, cc = inp
        cum = jnp.cumsum(ac, axis=1)
        seg = cum[:, :, None] - cum[:, None, :]
        decay = jnp.exp(jnp.where(causal, seg, -jnp.inf))
        cb = jnp.einsum('bign,bjgn->bijg', cc, bc)
        y_diag = jnp.einsum('bijgr,bjgrp->bigrp', cb[..., None] * decay, xc)
        y_off = jnp.einsum('bign,bgrpn->bigrp', cc, state) * jnp.exp(cum)[..., None]
        last = cum[:, -1]
        w_end = jnp.exp(last[:, None] - cum)
        state = state * jnp.exp(last)[..., None, None] + jnp.einsum(
            'bjgn,bjgrp->bgrpn', bc, xc * w_end[..., None])
        return state, y_diag + y_off

    state0 = jnp.zeros((b, g, r, p, n), jnp.float32)
    _, ys = lax.scan(step, state0, (chunks(xdt), chunks(da), chunks(bm), chunks(cm)))
    ys = jnp.moveaxis(ys, 0, 1).reshape(b, nc * CHUNK, g, r, p)
    return ys[:, :s]


def mamba2_branch(z, xbc, dt_raw, conv_w, conv_b, dt_bias, a_log, d_skip, ssm_norm):
    b, s, _ = z.shape
    xbc = jax.nn.silu(causal_depthwise_conv(xbc, conv_w, conv_b))
    xs, bm, cm = split_cols(xbc, (D_INNER, SSM_GROUPS * D_STATE, SSM_GROUPS * D_STATE))
    xh = xs.reshape(b, s, SSM_GROUPS, HEADS_PER_GROUP, SSM_HEADDIM).astype(jnp.float32)
    dt = jax.nn.softplus(dt_raw.astype(jnp.float32) + dt_bias.astype(jnp.float32))
    dt = dt.reshape(b, s, SSM_GROUPS, HEADS_PER_GROUP)
    a = -jnp.exp(a_log.astype(jnp.float32)).reshape(SSM_GROUPS, HEADS_PER_GROUP)
    y = ssd_chunked(xh, dt, a,
                    bm.reshape(b, s, SSM_GROUPS, D_STATE).astype(jnp.float32),
                    cm.reshape(b, s, SSM_GROUPS, D_STATE).astype(jnp.float32))
    y = y + d_skip.astype(jnp.float32).reshape(SSM_GROUPS, HEADS_PER_GROUP)[..., None] * xh
    y = y.reshape(b, s, D_INNER).astype(z.dtype) * jax.nn.silu(z)
    y = rms_norm(y.reshape(b, s, SSM_GROUPS, D_INNER // SSM_GROUPS),
                 ssm_norm.reshape(SSM_GROUPS, D_INNER // SSM_GROUPS))
    return y.reshape(b, s, D_INNER)


def _fwd_setup_inputs(seed: int = 0) -> dict:
    key = jax.random.key(seed)
    ks = jax.random.split(key, 32)
    f32 = jnp.float32
    nrm = lambda k, shape, fan_in: jax.random.normal(k, shape, f32) * fan_in ** -0.5
    gain = lambda k, dim: 1.0 + 0.02 * jax.random.normal(k, (DEPTH, dim), f32)
    dt0 = jnp.exp(jax.random.uniform(ks[12], (DEPTH, N_HEADS_SSM), f32)
                  * (np.log(0.1) - np.log(0.001)) + np.log(0.001))
    return {
        "x": jax.random.normal(ks[0], (BATCH, SEQ, D_MODEL), f32),
        "p": jax.random.normal(ks[1], (DEPTH, BATCH, SEQ, PLE_DIM), f32),
        "positions": jnp.broadcast_to(jnp.arange(SEQ, dtype=jnp.int32), (BATCH, SEQ)),
        "mix_norm_pre": gain(ks[2], D_MODEL),
        "mix_norm_post": gain(ks[3], D_MODEL),
        "w_in": nrm(ks[4], (DEPTH, D_MODEL, D_IN_PROJ), D_MODEL),
        "q_norm": gain(ks[5], Q_LORA),
        "w_uq": nrm(ks[6], (DEPTH, Q_LORA, N_HEADS_MLA * QK_DIM), Q_LORA),
        "kv_norm": gain(ks[7], KV_LORA),
        "w_ukv": nrm(ks[8], (DEPTH, KV_LORA, N_HEADS_MLA * (QK_NOPE + V_DIM)), KV_LORA),
        "conv_w": nrm(ks[9], (DEPTH, CONV_WIDTH, CONV_DIM), CONV_WIDTH),
        "conv_b": 0.01 * jax.random.normal(ks[10], (DEPTH, CONV_DIM), f32),
        "dt_bias": dt0 + jnp.log(-jnp.expm1(-dt0)),
        "a_log": jnp.log(jax.random.uniform(ks[11], (DEPTH, N_HEADS_SSM), f32, 1.0, 16.0)),
        "d_skip": 1.0 + 0.02 * jax.random.normal(ks[13], (DEPTH, N_HEADS_SSM), f32),
        "ssm_norm": gain(ks[14], D_INNER),
        "w_attn_o": nrm(ks[15], (DEPTH, N_HEADS_MLA * V_DIM, D_MODEL), N_HEADS_MLA * V_DIM),
        "w_ssm_o": nrm(ks[16], (DEPTH, D_INNER, D_MODEL), D_INNER),
        "w_out": nrm(ks[17], (DEPTH, D_MODEL, D_MODEL), D_MODEL),
        "ffn_norm_pre": gain(ks[18], D_MODEL),
        "ffn_norm_post": gain(ks[19], D_MODEL),
        "w_gate": nrm(ks[20], (DEPTH, D_MODEL, D_FF), D_MODEL),
        "w_up": nrm(ks[21], (DEPTH, D_MODEL, D_FF), D_MODEL),
        "w_down": nrm(ks[22], (DEPTH, D_FF, D_MODEL), D_FF),
        "ple_norm_pre": gain(ks[23], D_MODEL),
        "ple_norm_post": gain(ks[24], D_MODEL),
        "w_ple_gate": nrm(ks[25], (DEPTH, D_MODEL, D_MODEL), D_MODEL),
        "w_ple": nrm(ks[26], (DEPTH, PLE_DIM, D_MODEL), PLE_DIM),
    }


def _fwd_reference(x, p, positions, mix_norm_pre, mix_norm_post, w_in, q_norm, w_uq, kv_norm, w_ukv,
              conv_w, conv_b, dt_bias, a_log, d_skip, ssm_norm, w_attn_o, w_ssm_o, w_out,
              ffn_norm_pre, ffn_norm_post, w_gate, w_up, w_down,
              ple_norm_pre, ple_norm_post, w_ple_gate, w_ple):
    cos, sin = rotary_tables(positions, QK_ROPE)
    h = x
    for i in range(DEPTH):
        u = rms_norm(h, mix_norm_pre[i])
        proj = u @ w_in[i]
        c_q, c_kv, k_r, z, xbc, dt_raw, g_attn, g_ssm = split_cols(proj, IN_SPLITS)
        attn = mla_branch(c_q, c_kv, k_r, cos, sin, q_norm[i], w_uq[i], kv_norm[i], w_ukv[i])
        ssm = mamba2_branch(z, xbc, dt_raw, conv_w[i], conv_b[i], dt_bias[i], a_log[i],
                            d_skip[i], ssm_norm[i])
        mixed = jax.nn.sigmoid(g_attn) * (attn @ w_attn_o[i]) + jax.nn.sigmoid(g_ssm) * (ssm @ w_ssm_o[i])
        h = h + rms_norm(mixed @ w_out[i], mix_norm_post[i])
        f = rms_norm(h, ffn_norm_pre[i])
        f = (jax.nn.silu(f @ w_gate[i]) * (f @ w_up[i])) @ w_down[i]
        h = h + rms_norm(f, ffn_norm_post[i])
        gate = jax.nn.sigmoid(rms_norm(h, ple_norm_pre[i]) @ w_ple_gate[i])
        e = (p[i].astype(h.dtype) @ w_ple[i]) * gate
        h = h + rms_norm(e, ple_norm_post[i])
    return h


import jax as _jax
import jax.numpy as _jnp

TWIN_FORMAT = 'train_step'
FWD_PARAMS = ['x', 'p', 'positions', 'mix_norm_pre', 'mix_norm_post', 'w_in', 'q_norm', 'w_uq', 'kv_norm', 'w_ukv', 'conv_w', 'conv_b', 'dt_bias', 'a_log', 'd_skip', 'ssm_norm', 'w_attn_o', 'w_ssm_o', 'w_out', 'ffn_norm_pre', 'ffn_norm_post', 'w_gate', 'w_up', 'w_down', 'ple_norm_pre', 'ple_norm_post', 'w_ple_gate', 'w_ple']
TWIN_WEIGHTS = ['mix_norm_pre', 'mix_norm_post', 'w_in', 'q_norm', 'w_uq', 'kv_norm', 'w_ukv', 'conv_w', 'conv_b', 'dt_bias', 'a_log', 'd_skip', 'ssm_norm', 'w_attn_o', 'w_ssm_o', 'w_out', 'ffn_norm_pre', 'ffn_norm_post', 'w_gate', 'w_up', 'w_down', 'ple_norm_pre', 'ple_norm_post', 'w_ple_gate', 'w_ple']
TWIN_DIFF_INPUT = 'x'
TWIN_INPUTS = ['x', 'p', 'positions', 'mix_norm_pre', 'mix_norm_post', 'w_in', 'q_norm', 'w_uq', 'kv_norm', 'w_ukv', 'conv_w', 'conv_b', 'dt_bias', 'a_log', 'd_skip', 'ssm_norm', 'w_attn_o', 'w_ssm_o', 'w_out', 'ffn_norm_pre', 'ffn_norm_post', 'w_gate', 'w_up', 'w_down', 'ple_norm_pre', 'ple_norm_post', 'w_ple_gate', 'w_ple', 'loss_target', 'm_mix_norm_pre', 'm_mix_norm_post', 'm_w_in', 'm_q_norm', 'm_w_uq', 'm_kv_norm', 'm_w_ukv', 'm_conv_w', 'm_conv_b', 'm_dt_bias', 'm_a_log', 'm_d_skip', 'm_ssm_norm', 'm_w_attn_o', 'm_w_ssm_o', 'm_w_out', 'm_ffn_norm_pre', 'm_ffn_norm_post', 'm_w_gate', 'm_w_up', 'm_w_down', 'm_ple_norm_pre', 'm_ple_norm_post', 'm_w_ple_gate', 'm_w_ple', 'v_mix_norm_pre', 'v_mix_norm_post', 'v_w_in', 'v_q_norm', 'v_w_uq', 'v_kv_norm', 'v_w_ukv', 'v_conv_w', 'v_conv_b', 'v_dt_bias', 'v_a_log', 'v_d_skip', 'v_ssm_norm', 'v_w_attn_o', 'v_w_ssm_o', 'v_w_out', 'v_ffn_norm_pre', 'v_ffn_norm_post', 'v_w_gate', 'v_w_up', 'v_w_down', 'v_ple_norm_pre', 'v_ple_norm_post', 'v_w_ple_gate', 'v_w_ple']
TWIN_OUTPUTS = ['loss', 'grad_x', 'grad_mix_norm_pre', 'grad_mix_norm_post', 'grad_w_in', 'grad_q_norm', 'grad_w_uq', 'grad_kv_norm', 'grad_w_ukv', 'grad_conv_w', 'grad_conv_b', 'grad_dt_bias', 'grad_a_log', 'grad_d_skip', 'grad_ssm_norm', 'grad_w_attn_o', 'grad_w_ssm_o', 'grad_w_out', 'grad_ffn_norm_pre', 'grad_ffn_norm_post', 'grad_w_gate', 'grad_w_up', 'grad_w_down', 'grad_ple_norm_pre', 'grad_ple_norm_post', 'grad_w_ple_gate', 'grad_w_ple', 'delta_mix_norm_pre', 'delta_mix_norm_post', 'delta_w_in', 'delta_q_norm', 'delta_w_uq', 'delta_kv_norm', 'delta_w_ukv', 'delta_conv_w', 'delta_conv_b', 'delta_dt_bias', 'delta_a_log', 'delta_d_skip', 'delta_ssm_norm', 'delta_w_attn_o', 'delta_w_ssm_o', 'delta_w_out', 'delta_ffn_norm_pre', 'delta_ffn_norm_post', 'delta_w_gate', 'delta_w_up', 'delta_w_down', 'delta_ple_norm_pre', 'delta_ple_norm_post', 'delta_w_ple_gate', 'delta_w_ple', 'new_m_mix_norm_pre', 'new_m_mix_norm_post', 'new_m_w_in', 'new_m_q_norm', 'new_m_w_uq', 'new_m_kv_norm', 'new_m_w_ukv', 'new_m_conv_w', 'new_m_conv_b', 'new_m_dt_bias', 'new_m_a_log', 'new_m_d_skip', 'new_m_ssm_norm', 'new_m_w_attn_o', 'new_m_w_ssm_o', 'new_m_w_out', 'new_m_ffn_norm_pre', 'new_m_ffn_norm_post', 'new_m_w_gate', 'new_m_w_up', 'new_m_w_down', 'new_m_ple_norm_pre', 'new_m_ple_norm_post', 'new_m_w_ple_gate', 'new_m_w_ple', 'new_v_mix_norm_pre', 'new_v_mix_norm_post', 'new_v_w_in', 'new_v_q_norm', 'new_v_w_uq', 'new_v_kv_norm', 'new_v_w_ukv', 'new_v_conv_w', 'new_v_conv_b', 'new_v_dt_bias', 'new_v_a_log', 'new_v_d_skip', 'new_v_ssm_norm', 'new_v_w_attn_o', 'new_v_w_ssm_o', 'new_v_w_out', 'new_v_ffn_norm_pre', 'new_v_ffn_norm_post', 'new_v_w_gate', 'new_v_w_up', 'new_v_w_down', 'new_v_ple_norm_pre', 'new_v_ple_norm_post', 'new_v_w_ple_gate', 'new_v_w_ple']
TWIN_LEAF_KINDS = {'loss': 'loss', 'grad_x': 'grad_x', 'grad_mix_norm_pre': 'grad_w', 'grad_mix_norm_post': 'grad_w', 'grad_w_in': 'grad_w', 'grad_q_norm': 'grad_w', 'grad_w_uq': 'grad_w', 'grad_kv_norm': 'grad_w', 'grad_w_ukv': 'grad_w', 'grad_conv_w': 'grad_w', 'grad_conv_b': 'grad_w', 'grad_dt_bias': 'grad_w', 'grad_a_log': 'grad_w', 'grad_d_skip': 'grad_w', 'grad_ssm_norm': 'grad_w', 'grad_w_attn_o': 'grad_w', 'grad_w_ssm_o': 'grad_w', 'grad_w_out': 'grad_w', 'grad_ffn_norm_pre': 'grad_w', 'grad_ffn_norm_post': 'grad_w', 'grad_w_gate': 'grad_w', 'grad_w_up': 'grad_w', 'grad_w_down': 'grad_w', 'grad_ple_norm_pre': 'grad_w', 'grad_ple_norm_post': 'grad_w', 'grad_w_ple_gate': 'grad_w', 'grad_w_ple': 'grad_w', 'delta_mix_norm_pre': 'delta_w', 'delta_mix_norm_post': 'delta_w', 'delta_w_in': 'delta_w', 'delta_q_norm': 'delta_w', 'delta_w_uq': 'delta_w', 'delta_kv_norm': 'delta_w', 'delta_w_ukv': 'delta_w', 'delta_conv_w': 'delta_w', 'delta_conv_b': 'delta_w', 'delta_dt_bias': 'delta_w', 'delta_a_log': 'delta_w', 'delta_d_skip': 'delta_w', 'delta_ssm_norm': 'delta_w', 'delta_w_attn_o': 'delta_w', 'delta_w_ssm_o': 'delta_w', 'delta_w_out': 'delta_w', 'delta_ffn_norm_pre': 'delta_w', 'delta_ffn_norm_post': 'delta_w', 'delta_w_gate': 'delta_w', 'delta_w_up': 'delta_w', 'delta_w_down': 'delta_w', 'delta_ple_norm_pre': 'delta_w', 'delta_ple_norm_post': 'delta_w', 'delta_w_ple_gate': 'delta_w', 'delta_w_ple': 'delta_w', 'new_m_mix_norm_pre': 'new_m', 'new_m_mix_norm_post': 'new_m', 'new_m_w_in': 'new_m', 'new_m_q_norm': 'new_m', 'new_m_w_uq': 'new_m', 'new_m_kv_norm': 'new_m', 'new_m_w_ukv': 'new_m', 'new_m_conv_w': 'new_m', 'new_m_conv_b': 'new_m', 'new_m_dt_bias': 'new_m', 'new_m_a_log': 'new_m', 'new_m_d_skip': 'new_m', 'new_m_ssm_norm': 'new_m', 'new_m_w_attn_o': 'new_m', 'new_m_w_ssm_o': 'new_m', 'new_m_w_out': 'new_m', 'new_m_ffn_norm_pre': 'new_m', 'new_m_ffn_norm_post': 'new_m', 'new_m_w_gate': 'new_m', 'new_m_w_up': 'new_m', 'new_m_w_down': 'new_m', 'new_m_ple_norm_pre': 'new_m', 'new_m_ple_norm_post': 'new_m', 'new_m_w_ple_gate': 'new_m', 'new_m_w_ple': 'new_m', 'new_v_mix_norm_pre': 'new_v', 'new_v_mix_norm_post': 'new_v', 'new_v_w_in': 'new_v', 'new_v_q_norm': 'new_v', 'new_v_w_uq': 'new_v', 'new_v_kv_norm': 'new_v', 'new_v_w_ukv': 'new_v', 'new_v_conv_w': 'new_v', 'new_v_conv_b': 'new_v', 'new_v_dt_bias': 'new_v', 'new_v_a_log': 'new_v', 'new_v_d_skip': 'new_v', 'new_v_ssm_norm': 'new_v', 'new_v_w_attn_o': 'new_v', 'new_v_w_ssm_o': 'new_v', 'new_v_w_out': 'new_v', 'new_v_ffn_norm_pre': 'new_v', 'new_v_ffn_norm_post': 'new_v', 'new_v_w_gate': 'new_v', 'new_v_w_up': 'new_v', 'new_v_w_down': 'new_v', 'new_v_ple_norm_pre': 'new_v', 'new_v_ple_norm_post': 'new_v', 'new_v_w_ple_gate': 'new_v', 'new_v_w_ple': 'new_v'}


def _forward(args):
    return _fwd_reference(*[args[k] for k in FWD_PARAMS])


def _output_shape():
    def fwd():
        inp = _fwd_setup_inputs(0)
        return _fwd_reference(*[inp[k] for k in FWD_PARAMS])
    out = _jax.eval_shape(fwd)
    return out.shape, out.dtype

N_MICROBATCH = 1
ADAM_LR = 0.001
ADAM_B1 = 0.9
ADAM_B2 = 0.999
ADAM_EPS = 1e-08
ADAM_WD = 0.01
ADAM_STEP = 10
PER_EXAMPLE_BATCH_AXIS = {'x': 0, 'p': 1, 'positions': 0, 'loss_target': 0}
SHARED_INPUTS = []
_WEIGHT_DTYPES = {'mix_norm_pre': _jnp.float32, 'mix_norm_post': _jnp.float32, 'w_in': _jnp.float32, 'q_norm': _jnp.float32, 'w_uq': _jnp.float32, 'kv_norm': _jnp.float32, 'w_ukv': _jnp.float32, 'conv_w': _jnp.float32, 'conv_b': _jnp.float32, 'dt_bias': _jnp.float32, 'a_log': _jnp.float32, 'd_skip': _jnp.float32, 'ssm_norm': _jnp.float32, 'w_attn_o': _jnp.float32, 'w_ssm_o': _jnp.float32, 'w_out': _jnp.float32, 'ffn_norm_pre': _jnp.float32, 'ffn_norm_post': _jnp.float32, 'w_gate': _jnp.float32, 'w_up': _jnp.float32, 'w_down': _jnp.float32, 'ple_norm_pre': _jnp.float32, 'ple_norm_post': _jnp.float32, 'w_ple_gate': _jnp.float32, 'w_ple': _jnp.float32}
MOMENT_SCALE = {'mix_norm_pre': 6.205157e-01, 'mix_norm_post': 3.199440e+01, 'w_in': 2.144063e-01, 'q_norm': 1.233516e-01, 'w_uq': 5.121376e-02, 'kv_norm': 1.840255e-01, 'w_ukv': 6.450968e-02, 'conv_w': 2.603692e-01, 'conv_b': 5.817460e-01, 'dt_bias': 5.760667e-01, 'a_log': 1.099380e+00, 'd_skip': 1.660726e+00, 'ssm_norm': 3.845523e-01, 'w_attn_o': 7.485330e-02, 'w_ssm_o': 5.173816e-01, 'w_out': 5.361519e-01, 'ffn_norm_pre': 5.091808e-01, 'ffn_norm_post': 3.198131e+01, 'w_gate': 1.785058e-01, 'w_up': 2.502319e-01, 'w_down': 4.157383e-01, 'ple_norm_pre': 1.074879e-01, 'ple_norm_post': 3.217916e+01, 'w_ple_gate': 1.013519e-01, 'w_ple': 2.535256e-01}


def _to_microbatches(a, axis):
    t = _jnp.moveaxis(a, axis, 0)
    t = t.reshape((N_MICROBATCH, t.shape[0] // N_MICROBATCH) + t.shape[1:])
    return _jnp.moveaxis(t, 1, axis + 1)


def setup_inputs(seed: int = 0) -> dict:
    inp = _fwd_setup_inputs(seed)
    key = _jax.random.fold_in(_jax.random.key(seed), 7919)
    shape, _ = _output_shape()
    out = dict(inp)
    out["loss_target"] = _jax.random.normal(_jax.random.fold_in(key, 0), shape, _jnp.float32)
    for i, name in enumerate(TWIN_WEIGHTS):
        w = inp[name].astype(_jnp.float32)
        if MOMENT_SCALE is None:
            s = _jnp.sqrt(_jnp.mean(_jnp.square(w)) + 1e-30)
        else:
            s = MOMENT_SCALE[name]
        km, kv = _jax.random.split(_jax.random.fold_in(key, i + 1))
        out[name] = w
        out["m_" + name] = s * _jax.random.normal(km, w.shape, _jnp.float32)
        out["v_" + name] = (s * s) * _jax.random.uniform(kv, w.shape, _jnp.float32, 0.5, 1.5)
    if N_MICROBATCH > 1:
        for name, axis in PER_EXAMPLE_BATCH_AXIS.items():
            out[name] = _to_microbatches(out[name], axis)
    return {'x': out['x'], 'p': out['p'], 'positions': out['positions'], 'mix_norm_pre': out['mix_norm_pre'], 'mix_norm_post': out['mix_norm_post'], 'w_in': out['w_in'], 'q_norm': out['q_norm'], 'w_uq': out['w_uq'], 'kv_norm': out['kv_norm'], 'w_ukv': out['w_ukv'], 'conv_w': out['conv_w'], 'conv_b': out['conv_b'], 'dt_bias': out['dt_bias'], 'a_log': out['a_log'], 'd_skip': out['d_skip'], 'ssm_norm': out['ssm_norm'], 'w_attn_o': out['w_attn_o'], 'w_ssm_o': out['w_ssm_o'], 'w_out': out['w_out'], 'ffn_norm_pre': out['ffn_norm_pre'], 'ffn_norm_post': out['ffn_norm_post'], 'w_gate': out['w_gate'], 'w_up': out['w_up'], 'w_down': out['w_down'], 'ple_norm_pre': out['ple_norm_pre'], 'ple_norm_post': out['ple_norm_post'], 'w_ple_gate': out['w_ple_gate'], 'w_ple': out['w_ple'], 'loss_target': out['loss_target'], 'm_mix_norm_pre': out['m_mix_norm_pre'], 'm_mix_norm_post': out['m_mix_norm_post'], 'm_w_in': out['m_w_in'], 'm_q_norm': out['m_q_norm'], 'm_w_uq': out['m_w_uq'], 'm_kv_norm': out['m_kv_norm'], 'm_w_ukv': out['m_w_ukv'], 'm_conv_w': out['m_conv_w'], 'm_conv_b': out['m_conv_b'], 'm_dt_bias': out['m_dt_bias'], 'm_a_log': out['m_a_log'], 'm_d_skip': out['m_d_skip'], 'm_ssm_norm': out['m_ssm_norm'], 'm_w_attn_o': out['m_w_attn_o'], 'm_w_ssm_o': out['m_w_ssm_o'], 'm_w_out': out['m_w_out'], 'm_ffn_norm_pre': out['m_ffn_norm_pre'], 'm_ffn_norm_post': out['m_ffn_norm_post'], 'm_w_gate': out['m_w_gate'], 'm_w_up': out['m_w_up'], 'm_w_down': out['m_w_down'], 'm_ple_norm_pre': out['m_ple_norm_pre'], 'm_ple_norm_post': out['m_ple_norm_post'], 'm_w_ple_gate': out['m_w_ple_gate'], 'm_w_ple': out['m_w_ple'], 'v_mix_norm_pre': out['v_mix_norm_pre'], 'v_mix_norm_post': out['v_mix_norm_post'], 'v_w_in': out['v_w_in'], 'v_q_norm': out['v_q_norm'], 'v_w_uq': out['v_w_uq'], 'v_kv_norm': out['v_kv_norm'], 'v_w_ukv': out['v_w_ukv'], 'v_conv_w': out['v_conv_w'], 'v_conv_b': out['v_conv_b'], 'v_dt_bias': out['v_dt_bias'], 'v_a_log': out['v_a_log'], 'v_d_skip': out['v_d_skip'], 'v_ssm_norm': out['v_ssm_norm'], 'v_w_attn_o': out['v_w_attn_o'], 'v_w_ssm_o': out['v_w_ssm_o'], 'v_w_out': out['v_w_out'], 'v_ffn_norm_pre': out['v_ffn_norm_pre'], 'v_ffn_norm_post': out['v_ffn_norm_post'], 'v_w_gate': out['v_w_gate'], 'v_w_up': out['v_w_up'], 'v_w_down': out['v_w_down'], 'v_ple_norm_pre': out['v_ple_norm_pre'], 'v_ple_norm_post': out['v_ple_norm_post'], 'v_w_ple_gate': out['v_w_ple_gate'], 'v_w_ple': out['v_w_ple']}


def _loss(weights, diff, rest, loss_target):
    with _jax.named_scope("forward"):
        args = {**rest, TWIN_DIFF_INPUT: diff, **{k: w.astype(_WEIGHT_DTYPES[k]) for k, w in weights.items()}}
        y = _forward(args)
    with _jax.named_scope("loss_head"):
        err = _jnp.square(y.astype(_jnp.float32) - loss_target)
        return 0.5 * _jnp.sum(_jnp.mean(err, axis=-1)) if err.ndim else 0.5 * err


def _adamw(w, g, m, v):
    m = ADAM_B1 * m + (1.0 - ADAM_B1) * g
    v = ADAM_B2 * v + (1.0 - ADAM_B2) * _jnp.square(g)
    m_hat = m / (1.0 - ADAM_B1 ** ADAM_STEP)
    v_hat = v / (1.0 - ADAM_B2 ** ADAM_STEP)
    delta = -ADAM_LR * (m_hat / (_jnp.sqrt(v_hat) + ADAM_EPS) + ADAM_WD * w)
    return delta, m, v


def reference(x, p, positions, mix_norm_pre, mix_norm_post, w_in, q_norm, w_uq, kv_norm, w_ukv, conv_w, conv_b, dt_bias, a_log, d_skip, ssm_norm, w_attn_o, w_ssm_o, w_out, ffn_norm_pre, ffn_norm_post, w_gate, w_up, w_down, ple_norm_pre, ple_norm_post, w_ple_gate, w_ple, loss_target, m_mix_norm_pre, m_mix_norm_post, m_w_in, m_q_norm, m_w_uq, m_kv_norm, m_w_ukv, m_conv_w, m_conv_b, m_dt_bias, m_a_log, m_d_skip, m_ssm_norm, m_w_attn_o, m_w_ssm_o, m_w_out, m_ffn_norm_pre, m_ffn_norm_post, m_w_gate, m_w_up, m_w_down, m_ple_norm_pre, m_ple_norm_post, m_w_ple_gate, m_w_ple, v_mix_norm_pre, v_mix_norm_post, v_w_in, v_q_norm, v_w_uq, v_kv_norm, v_w_ukv, v_conv_w, v_conv_b, v_dt_bias, v_a_log, v_d_skip, v_ssm_norm, v_w_attn_o, v_w_ssm_o, v_w_out, v_ffn_norm_pre, v_ffn_norm_post, v_w_gate, v_w_up, v_w_down, v_ple_norm_pre, v_ple_norm_post, v_w_ple_gate, v_w_ple):
    given = dict(x=x, p=p, positions=positions, mix_norm_pre=mix_norm_pre, mix_norm_post=mix_norm_post, w_in=w_in, q_norm=q_norm, w_uq=w_uq, kv_norm=kv_norm, w_ukv=w_ukv, conv_w=conv_w, conv_b=conv_b, dt_bias=dt_bias, a_log=a_log, d_skip=d_skip, ssm_norm=ssm_norm, w_attn_o=w_attn_o, w_ssm_o=w_ssm_o, w_out=w_out, ffn_norm_pre=ffn_norm_pre, ffn_norm_post=ffn_norm_post, w_gate=w_gate, w_up=w_up, w_down=w_down, ple_norm_pre=ple_norm_pre, ple_norm_post=ple_norm_post, w_ple_gate=w_ple_gate, w_ple=w_ple, loss_target=loss_target, m_mix_norm_pre=m_mix_norm_pre, m_mix_norm_post=m_mix_norm_post, m_w_in=m_w_in, m_q_norm=m_q_norm, m_w_uq=m_w_uq, m_kv_norm=m_kv_norm, m_w_ukv=m_w_ukv, m_conv_w=m_conv_w, m_conv_b=m_conv_b, m_dt_bias=m_dt_bias, m_a_log=m_a_log, m_d_skip=m_d_skip, m_ssm_norm=m_ssm_norm, m_w_attn_o=m_w_attn_o, m_w_ssm_o=m_w_ssm_o, m_w_out=m_w_out, m_ffn_norm_pre=m_ffn_norm_pre, m_ffn_norm_post=m_ffn_norm_post, m_w_gate=m_w_gate, m_w_up=m_w_up, m_w_down=m_w_down, m_ple_norm_pre=m_ple_norm_pre, m_ple_norm_post=m_ple_norm_post, m_w_ple_gate=m_w_ple_gate, m_w_ple=m_w_ple, v_mix_norm_pre=v_mix_norm_pre, v_mix_norm_post=v_mix_norm_post, v_w_in=v_w_in, v_q_norm=v_q_norm, v_w_uq=v_w_uq, v_kv_norm=v_kv_norm, v_w_ukv=v_w_ukv, v_conv_w=v_conv_w, v_conv_b=v_conv_b, v_dt_bias=v_dt_bias, v_a_log=v_a_log, v_d_skip=v_d_skip, v_ssm_norm=v_ssm_norm, v_w_attn_o=v_w_attn_o, v_w_ssm_o=v_w_ssm_o, v_w_out=v_w_out, v_ffn_norm_pre=v_ffn_norm_pre, v_ffn_norm_post=v_ffn_norm_post, v_w_gate=v_w_gate, v_w_up=v_w_up, v_w_down=v_w_down, v_ple_norm_pre=v_ple_norm_pre, v_ple_norm_post=v_ple_norm_post, v_w_ple_gate=v_w_ple_gate, v_w_ple=v_w_ple)
    weights = {n: given[n] for n in TWIN_WEIGHTS}
    shared = {n: given[n] for n in SHARED_INPUTS}
    per_example = {n: given[n] for n in ['x', 'p', 'positions']}
    grad_fn = _jax.value_and_grad(_loss, argnums=(0, 1))

    def one_microbatch(ex, loss_target):
        ex = dict(ex)
        diff = ex.pop(TWIN_DIFF_INPUT)
        return grad_fn(weights, diff, {**shared, **ex}, loss_target)

    if N_MICROBATCH == 1:
        loss, (grad_w, grad_x) = one_microbatch(per_example, given["loss_target"])
    else:
        def body(carry, xs):
            loss_sum, grad_sum = carry
            l_k, (gw_k, gx_k) = one_microbatch(xs[0], xs[1])
            with _jax.named_scope("update"):
                return (loss_sum + l_k, _jax.tree.map(_jnp.add, grad_sum, gw_k)), gx_k

        init = (_jnp.zeros((), _jnp.float32), _jax.tree.map(_jnp.zeros_like, weights))
        (loss, grad_w), grad_x = _jax.lax.scan(body, init, (per_example, given["loss_target"]))
    with _jax.named_scope("update"):
        delta_w, new_m, new_v = {}, {}, {}
        for n in TWIN_WEIGHTS:
            delta_w[n], new_m[n], new_v[n] = _adamw(weights[n], grad_w[n], given["m_" + n], given["v_" + n])
    return (loss, grad_x, *[grad_w[n] for n in TWIN_WEIGHTS], *[delta_w[n] for n in TWIN_WEIGHTS],
            *[new_m[n] for n in TWIN_WEIGHTS], *[new_v[n] for n in TWIN_WEIGHTS])
```

```python
import functools
import math

import numpy as np
import jax
import jax.numpy as jnp
from jax import lax
from jax.experimental import pallas as pl
from jax.experimental.pallas import tpu as pltpu

F32 = jnp.float32
BF16 = jnp.bfloat16

D_MODEL = 2048
N_HEADS_MLA = 16
Q_LORA = 512
KV_LORA = 512
QK_NOPE = 128
QK_ROPE = 64
V_DIM = 128
QK_DIM = QK_NOPE + QK_ROPE
ROPE_THETA = 10000.0
D_INNER = 4096
SSM_HEADDIM = 64
N_HEADS_SSM = 64
SSM_GROUPS = 8
HEADS_PER_GROUP = 8
D_STATE = 128
CONV_WIDTH = 4
CHUNK = 256
CONV_DIM = D_INNER + 2 * SSM_GROUPS * D_STATE
D_FF = 5632
PLE_DIM = 256
EPS = 1e-6
IN_SPLITS = (Q_LORA, KV_LORA, QK_ROPE, D_INNER, CONV_DIM, N_HEADS_SSM, D_MODEL, D_MODEL)

ADAM_LR = 0.001
ADAM_B1 = 0.9
ADAM_B2 = 0.999
ADAM_EPS = 1e-08
ADAM_WD = 0.01
ADAM_STEP = 10

LANE = 128
SUBLANE = 8
HEAD_PAD = 256
VMEM_LIMIT = 56 * 1024 * 1024
ATTN_TILE = 512
NEG = -1e30

MESH_ID = pl.DeviceIdType.MESH
N_CHIPS = 4
N_DEV = 8


def _tile(n, pref, mult=LANE):
    if n <= pref:
        return n
    t = (pref // mult) * mult
    while t >= mult:
        if n % t == 0:
            return t
        t -= mult
    return n


def _params(sem, vmem=VMEM_LIMIT):
    return pltpu.CompilerParams(dimension_semantics=sem, vmem_limit_bytes=vmem)


def _mm(a, b, *, ta=False, tb=False, add=None, out_dtype=F32, name, tm=1024, tn=1024, tk=512):
    if ta:
        K, M = a.shape
    else:
        M, K = a.shape
    N = b.shape[0] if tb else b.shape[1]
    assert (b.shape[1] if tb else b.shape[0]) == K, (a.shape, b.shape, ta, tb)
    tm, tn, tk = _tile(M, tm), _tile(N, tn), _tile(K, tk)
    nk = K // tk
    dn = (((0 if ta else 1,), (1 if tb else 0,)), ((), ()))
    has_add = add is not None

    def body(*refs):
        if has_add:
            a_ref, b_ref, c_ref, o_ref, acc = refs
        else:
            a_ref, b_ref, o_ref, acc = refs
        k = pl.program_id(2)

        @pl.when(k == 0)
        def _():
            acc[...] = c_ref[...] if has_add else jnp.zeros_like(acc)

        acc[...] += lax.dot_general(a_ref[...].astype(BF16), b_ref[...].astype(BF16), dn,
                                    preferred_element_type=F32)

        @pl.when(k == nk - 1)
        def _():
            o_ref[...] = acc[...].astype(out_dtype)

    a_spec = pl.BlockSpec((tk, tm), lambda i, j, k: (k, i)) if ta else pl.BlockSpec((tm, tk), lambda i, j, k: (i, k))
    b_spec = pl.BlockSpec((tn, tk), lambda i, j, k: (j, k)) if tb else pl.BlockSpec((tk, tn), lambda i, j, k: (k, j))
    in_specs = [a_spec, b_spec]
    args = [a, b]
    if has_add:
        in_specs.append(pl.BlockSpec((tm, tn), lambda i, j, k: (i, j)))
        args.append(add)
    return pl.pallas_call(
        body, name=name,
        out_shape=jax.ShapeDtypeStruct((M, N), out_dtype),
        grid=(M // tm, N // tn, nk),
        in_specs=in_specs,
        out_specs=pl.BlockSpec((tm, tn), lambda i, j, k: (i, j)),
        scratch_shapes=[pltpu.VMEM((tm, tn), F32)],
        compiler_params=_params(("parallel", "parallel", "arbitrary")),
    )(*args)


def _row(arr, width=None, cblk=0):
    return ("row", arr, arr.shape[1] if width is None else width, cblk)


def _full(arr):
    return ("full", arr)


def _prev8(arr):
    return ("prev8", arr)


def _next8(arr):
    return ("next8", arr)


def _rows(fn, n_rows, tm, ins, outs, name):
    tm = min(tm, n_rows)
    assert n_rows % tm == 0 and tm % SUBLANE == 0
    n = n_rows // tm
    in_specs, args = [], []
    for spec in ins:
        kind, arr = spec[0], spec[1]
        if kind == "row":
            _, _, w, cb = spec
            in_specs.append(pl.BlockSpec((tm, w), lambda i, cb=cb: (i, cb)))
        elif kind == "full":
            in_specs.append(pl.BlockSpec(arr.shape, lambda i, nd=arr.ndim: (0,) * nd))
        elif kind == "prev8":
            in_specs.append(pl.BlockSpec((SUBLANE, arr.shape[1]),
                                         lambda i: (jnp.maximum(i * (tm // SUBLANE) - 1, 0), 0)))
        elif kind == "next8":
            last = n_rows // SUBLANE - 1
            in_specs.append(pl.BlockSpec((SUBLANE, arr.shape[1]),
                                         lambda i: (jnp.minimum((i + 1) * (tm // SUBLANE), last), 0)))
        else:
            raise ValueError(kind)
        args.append(arr)
    out_shapes, out_specs = [], []
    any_acc = False
    for spec in outs:
        if spec[0] == "row":
            _, w, dt = spec
            out_shapes.append(jax.ShapeDtypeStruct((n_rows, w), dt))
            out_specs.append(pl.BlockSpec((tm, w), lambda i: (i, 0)))
        else:
            _, shp, dt = spec
            any_acc = True
            out_shapes.append(jax.ShapeDtypeStruct(shp, dt))
            out_specs.append(pl.BlockSpec(shp, lambda i, nd=len(shp): (0,) * nd))
    nin = len(ins)

    def body(*refs):
        i = pl.program_id(0)
        vals = fn(i, n, *[r[...] for r in refs[:nin]])
        for o_ref, spec, v in zip(refs[nin:], outs, vals):
            if spec[0] == "acc":
                @pl.when(i == 0)
                def _(o_ref=o_ref):
                    o_ref[...] = jnp.zeros_like(o_ref)

                o_ref[...] += v.astype(o_ref.dtype)
            else:
                o_ref[...] = v.astype(o_ref.dtype)

    res = pl.pallas_call(
        body, name=name,
        out_shape=tuple(out_shapes),
        grid=(n,),
        in_specs=in_specs,
        out_specs=tuple(out_specs),
        compiler_params=_params(("arbitrary",) if any_acc else ("parallel",)),
    )(*args)
    return res


def _rstd(x):
    return lax.rsqrt(jnp.mean(x * x, axis=-1, keepdims=True) + EPS)


def _norm_bwd(x, r, g, dy):
    xh = x * r
    dyg = dy * g
    dx = r * (dyg - xh * jnp.mean(dyg * xh, axis=-1, keepdims=True))
    return dx, dy * xh


def _sigmoid(x):
    return 1.0 / (1.0 + jnp.exp(-x))


def _colsum(v):
    return jnp.sum(v, axis=0, keepdims=True)


def _rope_tables(pos, invf):
    ang = pos.astype(F32) * invf
    lane = lax.broadcasted_iota(jnp.int32, ang.shape, 1)
    cos, sin = jnp.cos(ang), jnp.sin(ang)
    ct = jnp.where(lane < QK_ROPE, cos, 0.0)
    sa = jnp.where(lane < QK_ROPE // 2, -sin, 0.0)
    sb = jnp.where((lane >= QK_ROPE // 2) & (lane < QK_ROPE), sin, 0.0)
    return ct, sa, sb


def _rope(b, ct, sa, sb):
    return ct * b + sa * pltpu.roll(b, LANE - QK_ROPE // 2, 1) + sb * pltpu.roll(b, QK_ROPE // 2, 1)


def _rope_t(d, ct, sa, sb):
    return ct * d + pltpu.roll(sa * d, QK_ROPE // 2, 1) + pltpu.roll(sb * d, LANE - QK_ROPE // 2, 1)


def _rope_fwd(q_raw, kr_pad, pos_col, invf):
    S = q_raw.shape[0]

    def fn(i, n, q, kr, pos, invf):
        ct, sa, sb = _rope_tables(pos, invf)
        parts = []
        for h in range(N_HEADS_MLA):
            parts.append(q[:, h * HEAD_PAD:h * HEAD_PAD + LANE])
            parts.append(_rope(q[:, h * HEAD_PAD + LANE:(h + 1) * HEAD_PAD], ct, sa, sb))
        return jnp.concatenate(parts, axis=1), _rope(kr, ct, sa, sb)

    return _rows(fn, S, 256, [_row(q_raw), _row(kr_pad), _row(pos_col), _full(invf)],
                 [("row", N_HEADS_MLA * HEAD_PAD, BF16), ("row", LANE, BF16)], "rope_fwd")


def _rope_bwd(dq, dkp, pos_col, invf):
    S = dq.shape[0]
    tm = 256

    def body(dq_ref, dkp_ref, pos_ref, invf_ref, dqo_ref, dkr_ref):
        ct, sa, sb = _rope_tables(pos_ref[...], invf_ref[...])
        for h in range(N_HEADS_MLA):
            dqo_ref[:, h * HEAD_PAD:h * HEAD_PAD + LANE] = dq_ref[:, h * HEAD_PAD:h * HEAD_PAD + LANE].astype(BF16)
            dqo_ref[:, h * HEAD_PAD + LANE:(h + 1) * HEAD_PAD] = _rope_t(
                dq_ref[:, h * HEAD_PAD + LANE:(h + 1) * HEAD_PAD], ct, sa, sb).astype(BF16)
        tot = dkp_ref[0]
        for h in range(1, N_HEADS_MLA):
            tot = tot + dkp_ref[h]
        dkr_ref[...] = _rope_t(tot, ct, sa, sb).astype(BF16)

    return pl.pallas_call(
        body, name="rope_bwd",
        out_shape=(jax.ShapeDtypeStruct(dq.shape, BF16), jax.ShapeDtypeStruct((S, LANE), BF16)),
        grid=(S // tm,),
        in_specs=[pl.BlockSpec((tm, dq.shape[1]), lambda i: (i, 0)),
                  pl.BlockSpec((N_HEADS_MLA, tm, LANE), lambda i: (0, i, 0)),
                  pl.BlockSpec((tm, 1), lambda i: (i, 0)),
                  pl.BlockSpec((1, LANE), lambda i: (0, 0))],
        out_specs=(pl.BlockSpec((tm, dq.shape[1]), lambda i: (i, 0)), pl.BlockSpec((tm, LANE), lambda i: (i, 0))),
        compiler_params=_params(("parallel",)),
    )(dq, dkp, pos_col, invf)


def _row_of(col, n):
    eye = lax.broadcasted_iota(jnp.int32, (n, n), 0) == lax.broadcasted_iota(jnp.int32, (n, n), 1)
    return jnp.sum(jnp.where(eye, col, 0.0), axis=0, keepdims=True)


def _attn_fwd(q, kv, kp, tile):
    S = q.shape[0]
    nq = S // tile
    scale = QK_DIM ** -0.5
    nt = (((1,), (1,)), ((), ()))

    def body(q_ref, kv_ref, kp_ref, o_ref, lse_ref, m_s, l_s, acc_s):
        qi = pl.program_id(1)
        qv = q_ref[...]
        m_s[...] = jnp.full_like(m_s, NEG)
        l_s[...] = jnp.zeros_like(l_s)
        acc_s[...] = jnp.zeros_like(acc_s)

        def step(start, masked):
            k = jnp.concatenate([kv_ref[pl.ds(start, tile), 0:LANE], kp_ref[pl.ds(start, tile), :]], axis=1)
            v = kv_ref[pl.ds(start, tile), LANE:2 * LANE]
            s = lax.dot_general(qv, k, nt, preferred_element_type=F32) * scale
            if masked:
                row = lax.broadcasted_iota(jnp.int32, s.shape, 0)
                col = lax.broadcasted_iota(jnp.int32, s.shape, 1)
                s = jnp.where(row >= col, s, NEG)
            m_old = m_s[...]
            m_new = jnp.maximum(m_old, jnp.max(s, axis=1, keepdims=True))
            alpha = jnp.exp(m_old - m_new)
            p = jnp.exp(s - m_new)
            l_s[...] = alpha * l_s[...] + jnp.sum(p, axis=1, keepdims=True)
            acc_s[...] = alpha * acc_s[...] + jnp.dot(p.astype(BF16), v, preferred_element_type=F32)
            m_s[...] = m_new

        def loop_body(j, carry):
            step(pl.multiple_of(j * tile, tile), False)
            return carry

        lax.fori_loop(0, qi, loop_body, 0)
        step(pl.multiple_of(qi * tile, tile), True)
        l = l_s[...]
        o_ref[...] = (acc_s[...] / l).astype(o_ref.dtype)
        lse_ref[0, 0] = _row_of(m_s[...] + jnp.log(l), tile)

    return pl.pallas_call(
        body, name="attn_fwd",
        out_shape=(jax.ShapeDtypeStruct((S, N_HEADS_MLA * V_DIM), BF16),
                   jax.ShapeDtypeStruct((N_HEADS_MLA, nq, 1, tile), F32)),
        grid=(N_HEADS_MLA, nq),
        in_specs=[pl.BlockSpec((tile, HEAD_PAD), lambda h, i: (i, h)),
                  pl.BlockSpec((S, HEAD_PAD), lambda h, i: (0, h)),
                  pl.BlockSpec((S, LANE), lambda h, i: (0, 0))],
        out_specs=(pl.BlockSpec((tile, V_DIM), lambda h, i: (i, h)),
                   pl.BlockSpec((1, 1, 1, tile), lambda h, i: (h, i, 0, 0))),
        scratch_shapes=[pltpu.VMEM((tile, 1), F32), pltpu.VMEM((tile, 1), F32), pltpu.VMEM((tile, V_DIM), F32)],
        compiler_params=_params(("parallel", "arbitrary")),
    )(q, kv, kp)


def _attn_delta(o, do, tile):
    S = o.shape[0]
    nq = S // tile

    def body(o_ref, do_ref, d_ref, dob_ref):
        dov = do_ref[...]
        prod = o_ref[...].astype(F32) * dov
        dob_ref[...] = dov.astype(BF16)
        for h in range(N_HEADS_MLA):
            col = jnp.sum(prod[:, h * V_DIM:(h + 1) * V_DIM], axis=1, keepdims=True)
            d_ref[h, 0] = _row_of(col, tile)

    return pl.pallas_call(
        body, name="attn_delta",
        out_shape=(jax.ShapeDtypeStruct((N_HEADS_MLA, nq, 1, tile), F32), jax.ShapeDtypeStruct(o.shape, BF16)),
        grid=(nq,),
        in_specs=[pl.BlockSpec((tile, o.shape[1]), lambda i: (i, 0)), pl.BlockSpec((tile, o.shape[1]), lambda i: (i, 0))],
        out_specs=(pl.BlockSpec((N_HEADS_MLA, 1, 1, tile), lambda i: (0, i, 0, 0)),
                   pl.BlockSpec((tile, o.shape[1]), lambda i: (i, 0))),
        compiler_params=_params(("parallel",)),
    )(o, do)


def _attn_bwd(q, kv, kp, do, lse, delta, tile):
    S = q.shape[0]
    nq = S // tile
    scale = QK_DIM ** -0.5
    nt = (((1,), (1,)), ((), ()))
    tn = (((0,), (0,)), ((), ()))

    def body(kv_ref, kp_ref, q_ref, do_ref, lse_ref, d_ref, dq_ref, dkv_ref, dkp_ref, dk_s, dv_s):
        ki = pl.program_id(1)
        k = jnp.concatenate([kv_ref[:, 0:LANE], kp_ref[...]], axis=1)
        v = kv_ref[:, LANE:2 * LANE]

        @pl.when(ki == 0)
        def _():
            dq_ref[...] = jnp.zeros_like(dq_ref)

        dk_s[...] = jnp.zeros_like(dk_s)
        dv_s[...] = jnp.zeros_like(dv_s)

        def step(qi, masked):
            start = pl.multiple_of(qi * tile, tile)
            qv = q_ref[pl.ds(start, tile), :]
            dov = do_ref[pl.ds(start, tile), :]
            st = lax.dot_general(k, qv, nt, preferred_element_type=F32) * scale
            pt = jnp.exp(st - lse_ref[0, qi])
            if masked:
                krow = lax.broadcasted_iota(jnp.int32, pt.shape, 0)
                qcol = lax.broadcasted_iota(jnp.int32, pt.shape, 1)
                pt = jnp.where(krow <= qcol, pt, 0.0)
            dv_s[...] += jnp.dot(pt.astype(BF16), dov, preferred_element_type=F32)
            dpt = lax.dot_general(v, dov, nt, preferred_element_type=F32)
            dst = (pt * (dpt - d_ref[0, qi]) * scale).astype(BF16)
            dk_s[...] += jnp.dot(dst, qv, preferred_element_type=F32)
            dq_ref[pl.ds(start, tile), :] += lax.dot_general(dst, k, tn, preferred_element_type=F32)

        step(ki, True)

        def loop_body(qi, carry):
            step(qi, False)
            return carry

        lax.fori_loop(ki + 1, nq, loop_body, 0)
        dkv_ref[...] = jnp.concatenate([dk_s[:, 0:LANE], dv_s[...]], axis=1).astype(dkv_ref.dtype)
        dkp_ref[0] = dk_s[:, LANE:2 * LANE]

    return pl.pallas_call(
        body, name="attn_bwd",
        out_shape=(jax.ShapeDtypeStruct((S, N_HEADS_MLA * HEAD_PAD), F32),
                   jax.ShapeDtypeStruct((S, N_HEADS_MLA * HEAD_PAD), BF16),
                   jax.ShapeDtypeStruct((N_HEADS_MLA, S, LANE), F32)),
        grid=(N_HEADS_MLA, nq),
        in_specs=[pl.BlockSpec((tile, HEAD_PAD), lambda h, i: (i, h)),
                  pl.BlockSpec((tile, LANE), lambda h, i: (i, 0)),
                  pl.BlockSpec((S, HEAD_PAD), lambda h, i: (0, h)),
                  pl.BlockSpec((S, V_DIM), lambda h, i: (0, h)),
                  pl.BlockSpec((1, nq, 1, tile), lambda h, i: (h, 0, 0, 0)),
                  pl.BlockSpec((1, nq, 1, tile), lambda h, i: (h, 0, 0, 0))],
        out_specs=(pl.BlockSpec((S, HEAD_PAD), lambda h, i: (0, h)),
                   pl.BlockSpec((tile, HEAD_PAD), lambda h, i: (i, h)),
                   pl.BlockSpec((1, tile, LANE), lambda h, i: (h, i, 0))),
        scratch_shapes=[pltpu.VMEM((tile, HEAD_PAD), F32), pltpu.VMEM((tile, V_DIM), F32)],
        compiler_params=_params(("parallel", "arbitrary")),
    )(kv, kp, q, do, lse, delta)


def _shift_down(cur, halo, k):
    sh = pltpu.roll(cur, k, 0)
    hs = pltpu.roll(halo, k, 0)
    rows = lax.broadcasted_iota(jnp.int32, hs.shape, 0)
    first = jnp.where(rows < k, hs, sh[0:SUBLANE])
    if cur.shape[0] == SUBLANE:
        return first
    return jnp.concatenate([first, sh[SUBLANE:]], axis=0)


def _shift_up(cur, nxt, k):
    n = cur.shape[0]
    sh = pltpu.roll(cur, n - k, 0)
    ns = pltpu.roll(nxt, SUBLANE - k, 0)
    rows = lax.broadcasted_iota(jnp.int32, ns.shape, 0)
    last = jnp.where(rows >= SUBLANE - k, ns, sh[n - SUBLANE:])
    if n == SUBLANE:
        return last
    return jnp.concatenate([sh[:n - SUBLANE], last], axis=0)


def _conv_pre(cur, halo, w, b):
    out = b + w[3:4] * cur
    for k in range(1, CONV_WIDTH):
        out = out + w[3 - k:4 - k] * _shift_down(cur, halo, k)
    return out


def _conv_fwd(xbc, w, b):
    S = xbc.shape[0]

    def fn(i, n, cur, prev, w, b):
        halo = jnp.where(i > 0, prev, 0.0)
        pre = _conv_pre(cur, halo, w, b)
        return (pre * _sigmoid(pre),)

    return _rows(fn, S, 256, [_row(xbc), _prev8(xbc), _full(w), _full(b)], [("row", xbc.shape[1], F32)], "conv_fwd")[0]


def _conv_bwd(xbc, dact, w, b):
    S, C = xbc.shape

    def dsilu(pre):
        s = _sigmoid(pre)
        return s * (1.0 + pre * (1.0 - s))

    def fn(i, n, cur, prev, nxt, dcur, dnxt, w, b):
        halo = jnp.where(i > 0, prev, 0.0)
        pre = _conv_pre(cur, halo, w, b)
        dpre = dcur * dsilu(pre)
        pre_n = _conv_pre(nxt, cur[cur.shape[0] - SUBLANE:], w, b)
        dpre_n = jnp.where(i < n - 1, dnxt * dsilu(pre_n), 0.0)
        dx = w[3:4] * dpre
        rows = lax.broadcasted_iota(jnp.int32, (SUBLANE, C), 0)
        dw = jnp.where(rows == 3, _colsum(dpre * cur), 0.0)
        for k in range(1, CONV_WIDTH):
            dx = dx + w[3 - k:4 - k] * _shift_up(dpre, dpre_n, k)
            dw = dw + jnp.where(rows == 3 - k, _colsum(dpre * _shift_down(cur, halo, k)), 0.0)
        return dx, dw, _colsum(dpre)

    return _rows(fn, S, 256, [_row(xbc), _prev8(xbc), _next8(xbc), _row(dact), _next8(dact), _full(w), _full(b)],
                 [("row", C, BF16), ("acc", (SUBLANE, C), F32), ("acc", (1, C), F32)], "conv_bwd")


def _softplus(x):
    return jnp.maximum(x, 0.0) + jnp.log1p(jnp.exp(-jnp.abs(x)))


def _cumsum_rows(x):
    rows = lax.broadcasted_iota(jnp.int32, x.shape, 0)
    s = 1
    while s < x.shape[0]:
        x = x + jnp.where(rows >= s, pltpu.roll(x, s, 0), 0.0)
        s *= 2
    return x


def _revcumsum_rows(x):
    n = x.shape[0]
    rows = lax.broadcasted_iota(jnp.int32, x.shape, 0)
    s = 1
    while s < n:
        x = x + jnp.where(rows < n - s, pltpu.roll(x, n - s, 0), 0.0)
        s *= 2
    return x


def _dt_prep(dt_raw, dt_bias, a_log):
    S = dt_raw.shape[0]

    def body(raw_ref, bias_ref, alog_ref, dt_ref, cum_ref, cumt_ref):
        dt = _softplus(raw_ref[...] + bias_ref[...])
        cum = _cumsum_rows(dt * (-jnp.exp(alog_ref[...])))
        dt_ref[...] = dt
        cum_ref[...] = cum
        cumt_ref[...] = cum.T

    return pl.pallas_call(
        body, name="dt_prep",
        out_shape=(jax.ShapeDtypeStruct((S, LANE), F32), jax.ShapeDtypeStruct((S, LANE), F32),
                   jax.ShapeDtypeStruct((LANE, S), F32)),
        grid=(S // CHUNK,),
        in_specs=[pl.BlockSpec((CHUNK, LANE), lambda i: (i, 0)), pl.BlockSpec((1, LANE), lambda i: (0, 0)),
                  pl.BlockSpec((1, LANE), lambda i: (0, 0))],
        out_specs=(pl.BlockSpec((CHUNK, LANE), lambda i: (i, 0)), pl.BlockSpec((CHUNK, LANE), lambda i: (i, 0)),
                   pl.BlockSpec((LANE, CHUNK), lambda i: (0, i))),
        compiler_params=_params(("parallel",)),
    )(dt_raw, dt_bias, a_log)


def _group_cols(t):
    S = t.shape[0]
    return jnp.transpose(t[:, :N_HEADS_SSM].reshape(S, SSM_GROUPS, HEADS_PER_GROUP), (1, 0, 2))


def _ungroup_cols(t):
    S = t.shape[1]
    flat = jnp.transpose(t, (1, 0, 2)).reshape(S, N_HEADS_SSM)
    return jnp.pad(flat, ((0, 0), (0, LANE - N_HEADS_SSM)))


_NT = (((1,), (1,)), ((), ()))
_TN = (((0,), (0,)), ((), ()))
P = SSM_HEADDIM
GW = HEADS_PER_GROUP * SSM_HEADDIM


def _decay(cc, cr):
    L = cc.shape[0]
    i = lax.broadcasted_iota(jnp.int32, (L, L), 0)
    j = lax.broadcasted_iota(jnp.int32, (L, L), 1)
    return jnp.exp(jnp.where(i >= j, cc - cr, NEG))


def _ssd_fwd(xbc_c, dt_g, cum_g, cumt_g):
    S = xbc_c.shape[0]
    nc = S // CHUNK
    L = CHUNK
    boff = D_INNER // D_STATE

    def body(x_ref, b_ref, c_ref, dt_ref, cum_ref, cumt_ref, y_ref, st_ref, state):
        c = pl.program_id(1)

        @pl.when(c == 0)
        def _():
            state[...] = jnp.zeros_like(state)

        bm = b_ref[...].astype(BF16)
        cm = c_ref[...].astype(BF16)
        cb = lax.dot_general(cm, bm, _NT, preferred_element_type=F32)
        for r in range(HEADS_PER_GROUP):
            cc = cum_ref[0, :, r:r + 1]
            cr = cumt_ref[0, r:r + 1, :]
            m = (cb * _decay(cc, cr)).astype(BF16)
            xdt = x_ref[:, r * P:(r + 1) * P] * dt_ref[0, :, r:r + 1]
            st = state[r * P:(r + 1) * P, :]
            st_ref[0, 0, r * P:(r + 1) * P, :] = st
            y = jnp.dot(m, xdt.astype(BF16), preferred_element_type=F32)
            y = y + lax.dot_general(cm, st.astype(BF16), _NT, preferred_element_type=F32) * jnp.exp(cc)
            y_ref[:, r * P:(r + 1) * P] = y
            cl = cum_ref[0, L - 1:L, r:r + 1]
            wend = jnp.exp(cl - cc)
            state[r * P:(r + 1) * P, :] = st * jnp.exp(cl) + lax.dot_general(
                (xdt * wend).astype(BF16), bm, _TN, preferred_element_type=F32)

    return pl.pallas_call(
        body, name="ssd_fwd",
        out_shape=(jax.ShapeDtypeStruct((S, D_INNER), F32), jax.ShapeDtypeStruct((SSM_GROUPS, nc, GW, D_STATE), F32)),
        grid=(SSM_GROUPS, nc),
        in_specs=[pl.BlockSpec((L, GW), lambda g, c: (c, g)),
                  pl.BlockSpec((L, D_STATE), lambda g, c: (c, boff + g)),
                  pl.BlockSpec((L, D_STATE), lambda g, c: (c, boff + SSM_GROUPS + g)),
                  pl.BlockSpec((1, L, HEADS_PER_GROUP), lambda g, c: (g, c, 0)),
                  pl.BlockSpec((1, L, HEADS_PER_GROUP), lambda g, c: (g, c, 0)),
                  pl.BlockSpec((1, HEADS_PER_GROUP, L), lambda g, c: (g, 0, c))],
        out_specs=(pl.BlockSpec((L, GW), lambda g, c: (c, g)),
                   pl.BlockSpec((1, 1, GW, D_STATE), lambda g, c: (g, c, 0, 0))),
        scratch_shapes=[pltpu.VMEM((GW, D_STATE), F32)],
        compiler_params=_params(("parallel", "arbitrary")),
    )(xbc_c, xbc_c, xbc_c, dt_g, cum_g, cumt_g)


def _ssd_bwd(xbc_c, dt_g, cum_g, cumt_g, states, dy, dx_skip):
    S = xbc_c.shape[0]
    nc = S // CHUNK
    L = CHUNK
    boff = D_INNER // D_STATE
    rev = lambda c: nc - 1 - c

    def body(x_ref, b_ref, c_ref, dt_ref, cum_ref, cumt_ref, st_ref, dy_ref, skip_ref,
             dx_ref, db_ref, dc_ref, ddt_ref, dcum_ref, dstate):
        c = pl.program_id(1)

        @pl.when(c == 0)
        def _():
            dstate[...] = jnp.zeros_like(dstate)

        bf = b_ref[...]
        bm = bf.astype(BF16)
        cm = c_ref[...].astype(BF16)
        cb = lax.dot_general(cm, bm, _NT, preferred_element_type=F32)
        dcb = jnp.zeros((L, L), F32)
        dbs = jnp.zeros((L, D_STATE), F32)
        dcs = jnp.zeros((L, D_STATE), F32)
        rowid = lax.broadcasted_iota(jnp.int32, (L, 1), 0)
        for r in range(HEADS_PER_GROUP):
            sl = slice(r * P, (r + 1) * P)
            cc = cum_ref[0, :, r:r + 1]
            cr = cumt_ref[0, r:r + 1, :]
            dtc = dt_ref[0, :, r:r + 1]
            decay = _decay(cc, cr)
            m = cb * decay
            xr = x_ref[:, sl]
            xdt = xr * dtc
            xdb = xdt.astype(BF16)
            dyr = dy_ref[:, sl]
            dyb = dyr.astype(BF16)
            st = st_ref[0, 0, sl, :]
            stb = st.astype(BF16)
            ds = dstate[sl, :]
            dsb = ds.astype(BF16)
            ecc = jnp.exp(cc)
            cl = cum_ref[0, L - 1:L, r:r + 1]
            ecl = jnp.exp(cl)
            wend = jnp.exp(cl - cc)

            g = lax.dot_general(dyb, xdb, _NT, preferred_element_type=F32)
            q = g * m
            dcb = dcb + g * decay
            dcum = jnp.sum(q, axis=1, keepdims=True) - _row_of_t(jnp.sum(q, axis=0, keepdims=True), L)
            dxd = lax.dot_general(m.astype(BF16), dyb, _TN, preferred_element_type=F32)
            dxd = dxd + lax.dot_general(bm, dsb, _NT, preferred_element_type=F32) * wend
            yoff = lax.dot_general(cm, stb, _NT, preferred_element_type=F32) * ecc
            dcum = dcum + jnp.sum(dyr * yoff, axis=1, keepdims=True)
            dcs = dcs + jnp.dot(dyb, stb, preferred_element_type=F32) * ecc
            t = jnp.dot(xdb, dsb, preferred_element_type=F32)
            dbs = dbs + t * wend
            vj = jnp.sum(t * bf, axis=1, keepdims=True) * wend
            dcum = dcum - vj
            dlast = jnp.sum(vj) + ecl * jnp.sum(ds * st)
            dcum = dcum + jnp.where(rowid == L - 1, dlast, 0.0)
            dstate[sl, :] = ecl * ds + lax.dot_general((dyr * ecc).astype(BF16), cm, _TN, preferred_element_type=F32)

            dx_ref[:, sl] = dxd * dtc + skip_ref[:, sl]
            ddt_ref[0, :, r:r + 1] = jnp.sum(dxd * xr, axis=1, keepdims=True)
            dcum_ref[0, :, r:r + 1] = dcum
        dcbb = dcb.astype(BF16)
        dc_ref[...] = dcs + jnp.dot(dcbb, bm, preferred_element_type=F32)
        db_ref[...] = dbs + lax.dot_general(dcbb, cm, _TN, preferred_element_type=F32)

    return pl.pallas_call(
        body, name="ssd_bwd",
        out_shape=(jax.ShapeDtypeStruct((S, D_INNER), F32),
                   jax.ShapeDtypeStruct((S, SSM_GROUPS * D_STATE), F32),
                   jax.ShapeDtypeStruct((S, SSM_GROUPS * D_STATE), F32),
                   jax.ShapeDtypeStruct((SSM_GROUPS, S, HEADS_PER_GROUP), F32),
                   jax.ShapeDtypeStruct((SSM_GROUPS, S, HEADS_PER_GROUP), F32)),
        grid=(SSM_GROUPS, nc),
        in_specs=[pl.BlockSpec((L, GW), lambda g, c: (rev(c), g)),
                  pl.BlockSpec((L, D_STATE), lambda g, c: (rev(c), boff + g)),
                  pl.BlockSpec((L, D_STATE), lambda g, c: (rev(c), boff + SSM_GROUPS + g)),
                  pl.BlockSpec((1, L, HEADS_PER_GROUP), lambda g, c: (g, rev(c), 0)),
                  pl.BlockSpec((1, L, HEADS_PER_GROUP), lambda g, c: (g, rev(c), 0)),
                  pl.BlockSpec((1, HEADS_PER_GROUP, L), lambda g, c: (g, 0, rev(c))),
                  pl.BlockSpec((1, 1, GW, D_STATE), lambda g, c: (g, rev(c), 0, 0)),
                  pl.BlockSpec((L, GW), lambda g, c: (rev(c), g)),
                  pl.BlockSpec((L, GW), lambda g, c: (rev(c), g))],
        out_specs=(pl.BlockSpec((L, GW), lambda g, c: (rev(c), g)),
                   pl.BlockSpec((L, D_STATE), lambda g, c: (rev(c), g)),
                   pl.BlockSpec((L, D_STATE), lambda g, c: (rev(c), g)),
                   pl.BlockSpec((1, L, HEADS_PER_GROUP), lambda g, c: (g, rev(c), 0)),
                   pl.BlockSpec((1, L, HEADS_PER_GROUP), lambda g, c: (g, rev(c), 0))),
        scratch_shapes=[pltpu.VMEM((GW, D_STATE), F32)],
        compiler_params=_params(("parallel", "arbitrary")),
    )(xbc_c, xbc_c, xbc_c, dt_g, cum_g, cumt_g, states, dy, dx_skip)


def _row_of_t(row, n):
    eye = lax.broadcasted_iota(jnp.int32, (n, n), 0) == lax.broadcasted_iota(jnp.int32, (n, n), 1)
    return jnp.sum(jnp.where(eye, row, 0.0), axis=1, keepdims=True)


def _dt_bwd(dt_raw, dt_bias, a_log, ddt_x, dcum):
    S = dt_raw.shape[0]

    def fn(i, n, raw, ddx, dcu, bias, alog):
        xx = raw + bias
        dt = _softplus(xx)
        a = -jnp.exp(alog)
        dda = _revcumsum_rows(dcu)
        ddt = ddx + dda * a
        lane = lax.broadcasted_iota(jnp.int32, raw.shape, 1)
        draw = jnp.where(lane < N_HEADS_SSM, ddt * _sigmoid(xx), 0.0)
        return draw, _colsum(draw), _colsum(dda * dt) * a

    return _rows(fn, S, CHUNK, [_row(dt_raw), _row(ddt_x), _row(dcum), _full(dt_bias), _full(a_log)],
                 [("row", LANE, BF16), ("acc", (1, LANE), F32), ("acc", (1, LANE), F32)], "dt_bwd")


def _adamw(w, g, m, v, name):
    shape = w.shape
    cols = shape[-1]
    rows = int(np.prod(shape[:-1]))
    w2, g2, m2, v2 = (t.reshape(rows, cols) for t in (w, g, m, v))
    tr = rows if rows * cols <= 512 * 1024 else _tile(rows, max(SUBLANE, (512 * 1024 // cols) // SUBLANE * SUBLANE), SUBLANE)
    c1 = 1.0 - ADAM_B1 ** ADAM_STEP
    c2 = 1.0 - ADAM_B2 ** ADAM_STEP

    def body(w_ref, g_ref, m_ref, v_ref, d_ref, mo_ref, vo_ref):
        gv = g_ref[...]
        mn = ADAM_B1 * m_ref[...] + (1.0 - ADAM_B1) * gv
        vn = ADAM_B2 * v_ref[...] + (1.0 - ADAM_B2) * (gv * gv)
        d_ref[...] = -ADAM_LR * ((mn / c1) / (jnp.sqrt(vn / c2) + ADAM_EPS) + ADAM_WD * w_ref[...])
        mo_ref[...] = mn
        vo_ref[...] = vn

    spec = pl.BlockSpec((tr, cols), lambda i: (i, 0))
    outs = pl.pallas_call(
        body, name=name,
        out_shape=tuple(jax.ShapeDtypeStruct((rows, cols), F32) for _ in range(3)),
        grid=(rows // tr,),
        in_specs=[spec] * 4, out_specs=(spec,) * 3,
        compiler_params=_params(("parallel",)),
    )(w2, g2, m2, v2)
    return tuple(o.reshape(shape) for o in outs)


def _prep_weights(w_in, w_uq):
    offs = np.cumsum((0,) + IN_SPLITS)
    pad = lambda t: jnp.pad(t, ((0, 0), (0, LANE - t.shape[1])))
    pieces = dict(
        qkv=w_in[:, offs[0]:offs[2]],
        kr=pad(w_in[:, offs[2]:offs[3]]),
        z=w_in[:, offs[3]:offs[4]],
        xbc=w_in[:, offs[4]:offs[5]],
        dt=pad(w_in[:, offs[5]:offs[6]]),
        g=w_in[:, offs[6]:offs[8]],
    )
    uq = w_uq.reshape(Q_LORA, N_HEADS_MLA, QK_DIM)
    uq = jnp.pad(uq, ((0, 0), (0, 0), (0, HEAD_PAD - QK_DIM))).reshape(Q_LORA, N_HEADS_MLA * HEAD_PAD)
    return pieces, uq


def _local_step(x, p, positions, W, sp, target):
    S = x.shape[0]
    tile = min(ATTN_TILE, S)
    pos_col = positions.reshape(S, 1)
    invf = ROPE_THETA ** (-jnp.arange(0, QK_ROPE, 2, dtype=F32) / QK_ROPE)
    invf = jnp.pad(jnp.concatenate([invf, invf]), (0, LANE - QK_ROPE)).reshape(1, LANE)
    wp, w_uq_p = _prep_weights(W["w_in"], W["w_uq"])
    padl = lambda t: jnp.pad(t, ((0, 0), (0, LANE - t.shape[1])))
    dt_bias_p, a_log_p = padl(sp["dt_bias"]), padl(sp["a_log"])
    dskip_ch = jnp.repeat(sp["d_skip"], SSM_HEADDIM, axis=1)
    p_bf = p.astype(BF16)
    RW = 256

    (u_bf,) = _rows(lambda i, n, x, g: (x * _rstd(x) * g,), S, RW, [_row(x), _full(sp["mix_norm_pre"])],
                    [("row", D_MODEL, BF16)], "norm_pre")
    cqkv = _mm(u_bf, wp["qkv"], name="mm_qkv")
    z = _mm(u_bf, wp["z"], name="mm_z")
    xbc = _mm(u_bf, wp["xbc"], name="mm_xbc")
    gates = _mm(u_bf, wp["g"], name="mm_gates")
    kr_pad = _mm(u_bf, wp["kr"], name="mm_kr")
    dt_raw = _mm(u_bf, wp["dt"], name="mm_dt")

    def qkv_norm(i, n, cq, ckv, gq, gkv):
        return cq * _rstd(cq) * gq, ckv * _rstd(ckv) * gkv

    cqn, ckvn = _rows(qkv_norm, S, 512, [_row(cqkv, Q_LORA, 0), _row(cqkv, KV_LORA, 1), _full(sp["q_norm"]), _full(sp["kv_norm"])],
                      [("row", Q_LORA, BF16), ("row", KV_LORA, BF16)], "qkv_norm")
    q_raw = _mm(cqn, w_uq_p, name="mm_uq")
    kv = _mm(ckvn, W["w_ukv"], out_dtype=BF16, name="mm_ukv")
    q_bf, kp_bf = _rope_fwd(q_raw, kr_pad, pos_col, invf)
    attn, lse = _attn_fwd(q_bf, kv, kp_bf, tile)

    xbc_c = _conv_fwd(xbc, sp["conv_w"], sp["conv_b"])
    dt, cum, cumt = _dt_prep(dt_raw, dt_bias_p, a_log_p)
    dt_g, cum_g = _group_cols(dt), _group_cols(cum)
    cumt_g = cumt[:N_HEADS_SSM].reshape(SSM_GROUPS, HEADS_PER_GROUP, S)
    y, states = _ssd_fwd(xbc_c, dt_g, cum_g, cumt_g)

    GN = D_INNER // SSM_GROUPS

    def gated(y, xs, z, dsk):
        yt = y + dsk * xs
        sz = _sigmoid(z)
        return yt, sz, yt * (z * sz)

    def gated_norm(i, n, y, xs, z, dsk, gn):
        _, _, yg = gated(y, xs, z, dsk)
        parts = []
        for g in range(SSM_GROUPS):
            blk = yg[:, g * GN:(g + 1) * GN]
            parts.append(blk * _rstd(blk) * gn[:, g * GN:(g + 1) * GN])
        return (jnp.concatenate(parts, axis=1),)

    (ssm,) = _rows(gated_norm, S, 128, [_row(y), _row(xbc_c, D_INNER, 0), _row(z), _full(dskip_ch), _full(sp["ssm_norm"])],
                   [("row", D_INNER, BF16)], "gated_norm")

    a_o = _mm(attn, W["w_attn_o"], name="mm_attn_o")
    b_o = _mm(ssm, W["w_ssm_o"], name="mm_ssm_o")

    def mix(i, n, ga, gs, a, b):
        return (_sigmoid(ga) * a + _sigmoid(gs) * b,)

    (mixed,) = _rows(mix, S, RW, [_row(gates, D_MODEL, 0), _row(gates, D_MODEL, 1), _row(a_o), _row(b_o)],
                     [("row", D_MODEL, BF16)], "mix")
    m2 = _mm(mixed, W["w_out"], name="mm_out")

    def post(i, n, h, m, gpost, gpre):
        hn = h + m * _rstd(m) * gpost
        return hn, hn * _rstd(hn) * gpre

    h1, f_bf = _rows(post, S, RW, [_row(x), _row(m2), _full(sp["mix_norm_post"]), _full(sp["ffn_norm_pre"])],
                     [("row", D_MODEL, F32), ("row", D_MODEL, BF16)], "post_mix")
    ga = _mm(f_bf, W["w_gate"], name="mm_gate")
    up = _mm(f_bf, W["w_up"], name="mm_up")
    (s_bf,) = _rows(lambda i, n, a, b: (a * _sigmoid(a) * b,), S, RW, [_row(ga), _row(up)], [("row", D_FF, BF16)], "swiglu")
    f2 = _mm(s_bf, W["w_down"], name="mm_down")
    h2, n3_bf = _rows(post, S, RW, [_row(h1), _row(f2), _full(sp["ffn_norm_post"]), _full(sp["ple_norm_pre"])],
                      [("row", D_MODEL, F32), ("row", D_MODEL, BF16)], "post_ffn")
    gpre = _mm(n3_bf, W["w_ple_gate"], name="mm_ple_gate")
    pe = _mm(p_bf, W["w_ple"], name="mm_ple")

    def ple_loss(i, n, h2, gpre, pe, tgt, gpost):
        gate = _sigmoid(gpre)
        e = pe * gate
        r = _rstd(e)
        diff = h2 + e * r * gpost - tgt
        loss = 0.5 * jnp.sum(jnp.mean(diff * diff, axis=1, keepdims=True))
        dh3 = diff * (1.0 / D_MODEL)
        de, dg_rows = _norm_bwd(e, r, gpost, dh3)
        return (jnp.full((1, LANE), loss, F32), dh3, de * gate, de * pe * gate * (1.0 - gate), _colsum(dg_rows))

    loss, dh3, dpe, dgpre, g_ple_post = _rows(
        ple_loss, S, 128, [_row(h2), _row(gpre), _row(pe), _row(target), _full(sp["ple_norm_post"])],
        [("acc", (1, LANE), F32), ("row", D_MODEL, F32), ("row", D_MODEL, BF16), ("row", D_MODEL, BF16),
         ("acc", (1, D_MODEL), F32)], "ple_loss")

    gw = {}
    gs = {"ple_norm_post": g_ple_post}
    gw["w_ple"] = _mm(p_bf, dpe, ta=True, name="mmg_ple")
    gw["w_ple_gate"] = _mm(n3_bf, dgpre, ta=True, name="mmg_ple_gate")
    dn3 = _mm(dgpre, W["w_ple_gate"], tb=True, name="mmb_ple_gate")

    def post_bwd(i, n, h, m, dhn, dn, gpost, gpre):
        rm = _rstd(m)
        hn = h + m * rm * gpost
        dx, dgpre_rows = _norm_bwd(hn, _rstd(hn), gpre, dn)
        dhn_t = dhn + dx
        dm, dgpost_rows = _norm_bwd(m, rm, gpost, dhn_t)
        return dhn_t, dm, _colsum(dgpre_rows), _colsum(dgpost_rows)

    def run_post_bwd(h, m, dhn, dn, gpost, gpre, name):
        return _rows(post_bwd, S, 128, [_row(h), _row(m), _row(dhn), _row(dn), _full(gpost), _full(gpre)],
                     [("row", D_MODEL, F32), ("row", D_MODEL, BF16), ("acc", (1, D_MODEL), F32), ("acc", (1, D_MODEL), F32)], name)

    dh2, df2, gs["ple_norm_pre"], gs["ffn_norm_post"] = run_post_bwd(
        h1, f2, dh3, dn3, sp["ffn_norm_post"], sp["ple_norm_pre"], "post_ffn_bwd")
    gw["w_down"] = _mm(s_bf, df2, ta=True, name="mmg_down")
    ds = _mm(df2, W["w_down"], tb=True, name="mmb_down")

    def swiglu_bwd(i, n, a, b, ds):
        sa = _sigmoid(a)
        return ds * b * (sa * (1.0 + a * (1.0 - sa))), ds * (a * sa)

    dga, dup = _rows(swiglu_bwd, S, RW, [_row(ga), _row(up), _row(ds)], [("row", D_FF, BF16), ("row", D_FF, BF16)], "swiglu_bwd")
    gw["w_gate"] = _mm(f_bf, dga, ta=True, name="mmg_gate")
    gw["w_up"] = _mm(f_bf, dup, ta=True, name="mmg_up")
    df = _mm(dga, W["w_gate"], tb=True, name="mmb_gate")
    df = _mm(dup, W["w_up"], tb=True, add=df, name="mmb_up")
    dh1, dm2, gs["ffn_norm_pre"], gs["mix_norm_post"] = run_post_bwd(
        x, m2, dh2, df, sp["mix_norm_post"], sp["ffn_norm_pre"], "post_mix_bwd")
    gw["w_out"] = _mm(mixed, dm2, ta=True, name="mmg_out")
    dmixed = _mm(dm2, W["w_out"], tb=True, name="mmb_out")

    def mix_bwd(i, n, ga, gs_, a, b, dm):
        sa, ss = _sigmoid(ga), _sigmoid(gs_)
        return dm * sa, dm * ss, jnp.concatenate([dm * a * sa * (1.0 - sa), dm * b * ss * (1.0 - ss)], axis=1)

    da_o, db_o, dgates = _rows(mix_bwd, S, RW, [_row(gates, D_MODEL, 0), _row(gates, D_MODEL, 1), _row(a_o), _row(b_o), _row(dmixed)],
                               [("row", D_MODEL, BF16), ("row", D_MODEL, BF16), ("row", 2 * D_MODEL, BF16)], "mix_bwd")
    gw["w_attn_o"] = _mm(attn, da_o, ta=True, name="mmg_attn_o")
    dattn = _mm(da_o, W["w_attn_o"], tb=True, name="mmb_attn_o")
    gw["w_ssm_o"] = _mm(ssm, db_o, ta=True, name="mmg_ssm_o")
    dssm = _mm(db_o, W["w_ssm_o"], tb=True, name="mmb_ssm_o")

    delta, dattn_bf = _attn_delta(attn, dattn, tile)
    dq, dkv, dkp = _attn_bwd(q_bf, kv, kp_bf, dattn_bf, lse, delta, tile)
    dq_raw, dkr = _rope_bwd(dq, dkp, pos_col, invf)
    g_uq_p = _mm(cqn, dq_raw, ta=True, name="mmg_uq")
    gw["w_uq"] = g_uq_p.reshape(Q_LORA, N_HEADS_MLA, HEAD_PAD)[:, :, :QK_DIM].reshape(Q_LORA, N_HEADS_MLA * QK_DIM)
    dcqn = _mm(dq_raw, w_uq_p, tb=True, name="mmb_uq")
    gw["w_ukv"] = _mm(ckvn, dkv, ta=True, name="mmg_ukv")
    dckvn = _mm(dkv, W["w_ukv"], tb=True, name="mmb_ukv")

    def qkv_norm_bwd(i, n, cq, ckv, dq_, dkv_, gq, gkv):
        dcq, gq_rows = _norm_bwd(cq, _rstd(cq), gq, dq_)
        dckv, gkv_rows = _norm_bwd(ckv, _rstd(ckv), gkv, dkv_)
        return jnp.concatenate([dcq, dckv], axis=1), _colsum(gq_rows), _colsum(gkv_rows)

    dcqkv, gs["q_norm"], gs["kv_norm"] = _rows(
        qkv_norm_bwd, S, 512, [_row(cqkv, Q_LORA, 0), _row(cqkv, KV_LORA, 1), _row(dcqn), _row(dckvn), _full(sp["q_norm"]), _full(sp["kv_norm"])],
        [("row", Q_LORA + KV_LORA, BF16), ("acc", (1, Q_LORA), F32), ("acc", (1, KV_LORA), F32)], "qkv_norm_bwd")

    def gated_norm_bwd(i, n, y, xs, z, dssm, dsk, gn):
        yt, sz, yg = gated(y, xs, z, dsk)
        dyg_parts, gn_parts = [], []
        for g in range(SSM_GROUPS):
            sl = slice(g * GN, (g + 1) * GN)
            blk = yg[:, sl]
            dblk, rows = _norm_bwd(blk, _rstd(blk), gn[:, sl], dssm[:, sl])
            dyg_parts.append(dblk)
            gn_parts.append(_colsum(rows))
        dyg = jnp.concatenate(dyg_parts, axis=1)
        dyt = dyg * (z * sz)
        dz = dyg * yt * (sz * (1.0 + z * (1.0 - sz)))
        return dyt, dz, dyt * dsk, jnp.concatenate(gn_parts, axis=1), _colsum(dyt * xs)

    dy, dz, dx_skip, gs["ssm_norm"], g_dskip_ch = _rows(
        gated_norm_bwd, S, 128, [_row(y), _row(xbc_c, D_INNER, 0), _row(z), _row(dssm), _full(dskip_ch), _full(sp["ssm_norm"])],
        [("row", D_INNER, F32), ("row", D_INNER, BF16), ("row", D_INNER, F32), ("acc", (1, D_INNER), F32), ("acc", (1, D_INNER), F32)],
        "gated_norm_bwd")
    gs["d_skip"] = jnp.sum(g_dskip_ch.reshape(N_HEADS_SSM, SSM_HEADDIM), axis=1).reshape(1, N_HEADS_SSM)
    dxs, dbm, dcm, ddt_x, dcum = _ssd_bwd(xbc_c, dt_g, cum_g, cumt_g, states, dy, dx_skip)
    ddt_raw, g_dtb, g_alog = _dt_bwd(dt_raw, dt_bias_p, a_log_p, _ungroup_cols(ddt_x), _ungroup_cols(dcum))
    gs["dt_bias"] = g_dtb[:, :N_HEADS_SSM]
    gs["a_log"] = g_alog[:, :N_HEADS_SSM]
    dxbc_c = jnp.concatenate([dxs, dbm, dcm], axis=1)
    dxbc, g_conv_w8, gs["conv_b"] = _conv_bwd(xbc, dxbc_c, sp["conv_w"], sp["conv_b"])
    gs["conv_w"] = g_conv_w8[:CONV_WIDTH]

    g_qkv = _mm(u_bf, dcqkv, ta=True, name="mmg_qkv")
    g_kr = _mm(u_bf, dkr, ta=True, name="mmg_kr")
    g_z = _mm(u_bf, dz, ta=True, name="mmg_z")
    g_xbc = _mm(u_bf, dxbc, ta=True, name="mmg_xbc")
    g_dt = _mm(u_bf, ddt_raw, ta=True, name="mmg_dt")
    g_g = _mm(u_bf, dgates, ta=True, name="mmg_gates")
    gw["w_in"] = jnp.concatenate([g_qkv, g_kr[:, :QK_ROPE], g_z, g_xbc, g_dt[:, :N_HEADS_SSM], g_g], axis=1)
    du = _mm(dcqkv, wp["qkv"], tb=True, name="mmb_qkv")
    du = _mm(dkr, wp["kr"], tb=True, add=du, name="mmb_kr")
    du = _mm(dz, wp["z"], tb=True, add=du, name="mmb_z")
    du = _mm(dxbc, wp["xbc"], tb=True, add=du, name="mmb_xbc")
    du = _mm(ddt_raw, wp["dt"], tb=True, add=du, name="mmb_dt")
    du = _mm(dgates, wp["g"], tb=True, add=du, name="mmb_gates")

    def pre_bwd(i, n, x, du, dh, g):
        dx, rows = _norm_bwd(x, _rstd(x), g, du)
        return dh + dx, _colsum(rows)

    grad_x, gs["mix_norm_pre"] = _rows(pre_bwd, S, RW, [_row(x), _row(du), _row(dh1), _full(sp["mix_norm_pre"])],
                                       [("row", D_MODEL, F32), ("acc", (1, D_MODEL), F32)], "norm_pre_bwd")
    return loss, grad_x, gw, gs


PACK_COLS = 1024
BIG = (
    ("w_in", (2048, 3872), 1), ("w_uq", (512, 768), 1), ("w_ukv", (512, 1024), 1), ("w_attn_o", (512, 2048), 0),
    ("w_ssm_o", (1024, 2048), 0), ("w_out", (512, 2048), 0), ("w_gate", (2048, 1408), 1), ("w_up", (2048, 1408), 1),
    ("w_down", (1408, 2048), 0), ("w_ple_gate", (512, 2048), 0), ("w_ple", (256, 512), 1),
)
PACK_ROWS = 22528
SMALL = (
    ("mix_norm_pre", 2048), ("mix_norm_post", 2048), ("q_norm", 512), ("kv_norm", 512), ("conv_b", 6144), ("dt_bias", 64),
    ("a_log", 64), ("d_skip", 64), ("ssm_norm", 4096), ("ffn_norm_pre", 2048), ("ffn_norm_post", 2048),
    ("ple_norm_pre", 2048), ("ple_norm_post", 2048),
)
CONV_W_LEN = CONV_WIDTH * CONV_DIM
SMALL_ROWS = 384


def _place():
    return lax.axis_index("x"), lax.axis_index("y"), lax.axis_index("c")


def _flip(v, bit):
    return 1 - v if bit else v


def _pack(parts, rows, dtype):
    flat = jnp.concatenate([t.reshape(-1).astype(dtype) for t in parts])
    flat = jnp.pad(flat, (0, rows * PACK_COLS - flat.shape[0]))
    return flat.reshape(rows, PACK_COLS)


def _gather_chips(shard):
    R, C = shard.shape

    def body(x_hbm, o_hbm, send_sems, recv_sems, local_sem):
        x, y, c = _place()
        mine = o_hbm.at[2 * x + y]
        local = pltpu.make_async_copy(x_hbm, mine, local_sem)
        local.start()
        copies = []
        for k in (1, 2, 3):
            peer = (_flip(x, k >> 1), _flip(y, k & 1), c)
            copies.append(pltpu.make_async_remote_copy(
                src_ref=x_hbm, dst_ref=mine, send_sem=send_sems.at[k - 1], recv_sem=recv_sems.at[k - 1],
                device_id=peer, device_id_type=MESH_ID))
        for cp in copies:
            cp.start()
        for cp in copies:
            cp.wait()
        local.wait()

    return pl.pallas_call(
        body, name="gather_chips",
        out_shape=jax.ShapeDtypeStruct((N_CHIPS, R, C), shard.dtype),
        in_specs=[pl.BlockSpec(memory_space=pl.ANY)],
        out_specs=pl.BlockSpec(memory_space=pl.ANY),
        scratch_shapes=[pltpu.SemaphoreType.DMA((3,)), pltpu.SemaphoreType.DMA((3,)), pltpu.SemaphoreType.DMA(())],
    )(shard)


def _swap_halves(g):
    n, R, C = g.shape
    half = R // 2

    def body(g_hbm, o_hbm, send_sem, recv_sem):
        x, y, c = _place()
        src = g_hbm.at[:, pl.ds(pl.multiple_of((1 - c) * half, SUBLANE), half), :]
        cp = pltpu.make_async_remote_copy(src_ref=src, dst_ref=o_hbm, send_sem=send_sem, recv_sem=recv_sem,
                                          device_id=(x, y, 1 - c), device_id_type=MESH_ID)
        cp.start()
        cp.wait()

    return pl.pallas_call(
        body, name="swap_halves",
        out_shape=jax.ShapeDtypeStruct((n, half, C), g.dtype),
        in_specs=[pl.BlockSpec(memory_space=pl.ANY)],
        out_specs=pl.BlockSpec(memory_space=pl.ANY),
        scratch_shapes=[pltpu.SemaphoreType.DMA(()), pltpu.SemaphoreType.DMA(())],
    )(g)


def _add_half(g, other, c):
    n, R, C = g.shape
    half = R // 2
    tr = _tile(half, 1024, SUBLANE)
    nb = half // tr

    def body(c_ref, g_ref, o_ref, out_ref):
        out_ref[...] = g_ref[...] + o_ref[...]

    return pl.pallas_call(
        body, name="add_half",
        out_shape=jax.ShapeDtypeStruct((n, half, C), F32),
        grid_spec=pltpu.PrefetchScalarGridSpec(
            num_scalar_prefetch=1, grid=(n, nb),
            in_specs=[pl.BlockSpec((1, tr, C), lambda j, i, c_ref: (j, c_ref[0] * nb + i, 0)),
                      pl.BlockSpec((1, tr, C), lambda j, i, c_ref: (j, i, 0))],
            out_specs=pl.BlockSpec((1, tr, C), lambda j, i, c_ref: (j, i, 0))),
        compiler_params=_params(("parallel", "parallel")),
    )(c, g, other)


def _scatter_chips(part):
    n, R, C = part.shape

    def body(p_hbm, o_hbm, send_sems, recv_sems):
        x, y, c = _place()
        copies = []
        for k in (1, 2, 3):
            px, py = _flip(x, k >> 1), _flip(y, k & 1)
            copies.append(pltpu.make_async_remote_copy(
                src_ref=p_hbm.at[2 * px + py], dst_ref=o_hbm.at[k - 1], send_sem=send_sems.at[k - 1],
                recv_sem=recv_sems.at[k - 1], device_id=(px, py, c), device_id_type=MESH_ID))
        for cp in copies:
            cp.start()
        for cp in copies:
            cp.wait()

    return pl.pallas_call(
        body, name="scatter_chips",
        out_shape=jax.ShapeDtypeStruct((3, R, C), part.dtype),
        in_specs=[pl.BlockSpec(memory_space=pl.ANY)],
        out_specs=pl.BlockSpec(memory_space=pl.ANY),
        scratch_shapes=[pltpu.SemaphoreType.DMA((3,)), pltpu.SemaphoreType.DMA((3,))],
    )(part)


def _add_chips(part, got, chip):
    n, R, C = part.shape
    tr = _tile(R, 1024, SUBLANE)

    def body(chip_ref, p_ref, g_ref, out_ref):
        out_ref[...] = ((p_ref[0] + g_ref[0]) + g_ref[1]) + g_ref[2]

    return pl.pallas_call(
        body, name="add_chips",
        out_shape=jax.ShapeDtypeStruct((R, C), F32),
        grid_spec=pltpu.PrefetchScalarGridSpec(
            num_scalar_prefetch=1, grid=(R // tr,),
            in_specs=[pl.BlockSpec((1, tr, C), lambda i, chip_ref: (chip_ref[0], i, 0)),
                      pl.BlockSpec((3, tr, C), lambda i, chip_ref: (0, i, 0))],
            out_specs=pl.BlockSpec((tr, C), lambda i, chip_ref: (i, 0))),
        compiler_params=_params(("parallel",)),
    )(chip, part, got)


def _join_halves(mine):
    half, C = mine.shape

    def body(m_hbm, o_hbm, send_sem, recv_sem, local_sem):
        x, y, c = _place()
        dst = o_hbm.at[pl.ds(pl.multiple_of(c * half, SUBLANE), half), :]
        local = pltpu.make_async_copy(m_hbm, dst, local_sem)
        local.start()
        cp = pltpu.make_async_remote_copy(src_ref=m_hbm, dst_ref=dst, send_sem=send_sem, recv_sem=recv_sem,
                                          device_id=(x, y, 1 - c), device_id_type=MESH_ID)
        cp.start()
        cp.wait()
        local.wait()

    return pl.pallas_call(
        body, name="join_halves",
        out_shape=jax.ShapeDtypeStruct((2 * half, C), mine.dtype),
        in_specs=[pl.BlockSpec(memory_space=pl.ANY)],
        out_specs=pl.BlockSpec(memory_space=pl.ANY),
        scratch_shapes=[pltpu.SemaphoreType.DMA(()), pltpu.SemaphoreType.DMA(()), pltpu.SemaphoreType.DMA(())],
    )(mine)


def _allreduce_small(vec, name):
    R, C = vec.shape

    def body(v_ref, o_ref, buf, send_sems, recv_sems):
        x, y, c = _place()
        me = 4 * x + 2 * y + c
        buf[me] = v_ref[...]
        copies = []
        for k in range(1, N_DEV):
            peer = (_flip(x, (k >> 2) & 1), _flip(y, (k >> 1) & 1), _flip(c, k & 1))
            copies.append(pltpu.make_async_remote_copy(
                src_ref=v_ref, dst_ref=buf.at[me], send_sem=send_sems.at[k - 1], recv_sem=recv_sems.at[k - 1],
                device_id=peer, device_id_type=MESH_ID))
        for cp in copies:
            cp.start()
        for cp in copies:
            cp.wait()
        tot = buf[0]
        for d in range(1, N_DEV):
            tot = tot + buf[d]
        o_ref[...] = tot

    return pl.pallas_call(
        body, name=name,
        out_shape=jax.ShapeDtypeStruct((R, C), F32),
        in_specs=[pl.BlockSpec(memory_space=pltpu.VMEM)],
        out_specs=pl.BlockSpec(memory_space=pltpu.VMEM),
        scratch_shapes=[pltpu.VMEM((N_DEV, R, C), F32), pltpu.SemaphoreType.DMA((N_DEV - 1,)), pltpu.SemaphoreType.DMA((N_DEV - 1,))],
    )(vec)


def _unstack(gathered, shape, axis):
    if axis == 0:
        return gathered.reshape(N_CHIPS * shape[0], shape[1])
    return jnp.transpose(gathered, (1, 0, 2)).reshape(shape[0], N_CHIPS * shape[1])


def _stack(whole, shape, axis):
    if axis == 0:
        return whole.reshape(N_CHIPS, shape[0], shape[1])
    return jnp.transpose(whole.reshape(shape[0], N_CHIPS, shape[1]), (1, 0, 2))


def kernel(x, p, positions, mix_norm_pre, mix_norm_post, w_in, q_norm, w_uq, kv_norm, w_ukv, conv_w, conv_b, dt_bias, a_log, d_skip, ssm_norm, w_attn_o, w_ssm_o, w_out, ffn_norm_pre, ffn_norm_post, w_gate, w_up, w_down, ple_norm_pre, ple_norm_post, w_ple_gate, w_ple, loss_target, m_mix_norm_pre, m_mix_norm_post, m_w_in, m_q_norm, m_w_uq, m_kv_norm, m_w_ukv, m_conv_w, m_conv_b, m_dt_bias, m_a_log, m_d_skip, m_ssm_norm, m_w_attn_o, m_w_ssm_o, m_w_out, m_ffn_norm_pre, m_ffn_norm_post, m_w_gate, m_w_up, m_w_down, m_ple_norm_pre, m_ple_norm_post, m_w_ple_gate, m_w_ple, v_mix_norm_pre, v_mix_norm_post, v_w_in, v_q_norm, v_w_uq, v_kv_norm, v_w_ukv, v_conv_w, v_conv_b, v_dt_bias, v_a_log, v_d_skip, v_ssm_norm, v_w_attn_o, v_w_ssm_o, v_w_out, v_ffn_norm_pre, v_ffn_norm_post, v_w_gate, v_w_up, v_w_down, v_ple_norm_pre, v_ple_norm_post, v_w_ple_gate, v_w_ple):
    given = dict(locals())
    names = [n for n, _, _ in BIG] + [n for n, _ in SMALL] + ["conv_w"]
    order = ["mix_norm_pre", "mix_norm_post", "w_in", "q_norm", "w_uq", "kv_norm", "w_ukv", "conv_w", "conv_b", "dt_bias", "a_log",
             "d_skip", "ssm_norm", "w_attn_o", "w_ssm_o", "w_out", "ffn_norm_pre", "ffn_norm_post", "w_gate", "w_up", "w_down",
             "ple_norm_pre", "ple_norm_post", "w_ple_gate", "w_ple"]
    assert sorted(names) == sorted(order)
    cx, cy, cc = _place()
    chip = 2 * cx + cy
    conv_cols = CONV_DIM // N_CHIPS

    shards = _pack([given[n][0] for n, _, _ in BIG], PACK_ROWS, BF16)
    gathered = _gather_chips(shards)
    W, off = {}, 0
    for n, shape, axis in BIG:
        rows = shape[0] * shape[1] // PACK_COLS
        W[n] = _unstack(gathered[:, off:off + rows, :].reshape(N_CHIPS, *shape), shape, axis)
        off += rows
    own = jnp.where(cc == 0, conv_w[0], 0.0)
    conv_vec = lax.dynamic_update_slice(jnp.zeros((CONV_WIDTH, CONV_DIM), F32), own, (0, chip * conv_cols))
    conv_full = _allreduce_small(conv_vec.reshape(CONV_W_LEN // LANE, LANE), "gather_conv_w").reshape(CONV_WIDTH, CONV_DIM)
    sp = {n: given[n] for n, _ in SMALL}
    sp["conv_w"] = conv_full

    loss_part, grad_x, gw, gs = _local_step(x[0], p[0, 0], positions[0], W, sp, loss_target[0])

    stacked = [_stack(gw[n], shape, axis).reshape(N_CHIPS, -1) for n, shape, axis in BIG]
    flat = jnp.concatenate(stacked, axis=1)
    flat = jnp.pad(flat, ((0, 0), (0, PACK_ROWS * PACK_COLS - flat.shape[1]))).reshape(N_CHIPS, PACK_ROWS, PACK_COLS)
    c_arr = cc.reshape(1).astype(jnp.int32)
    chip_arr = chip.reshape(1).astype(jnp.int32)
    part = _add_half(flat, _swap_halves(flat), c_arr)
    mine = _add_chips(part, _scatter_chips(part), chip_arr)
    reduced = _join_halves(mine)
    g_big, off = {}, 0
    for n, shape, axis in BIG:
        rows = shape[0] * shape[1] // PACK_COLS
        g_big[n] = reduced[off:off + rows].reshape(1, *shape)
        off += rows

    small_parts = [gs[n] for n, _ in SMALL] + [gs["conv_w"], loss_part[:, :1]]
    small_vec = jnp.concatenate([t.reshape(-1) for t in small_parts])
    small_vec = jnp.pad(small_vec, (0, SMALL_ROWS * LANE - small_vec.shape[0])).reshape(SMALL_ROWS, LANE)
    small_sum = _allreduce_small(small_vec, "allreduce_small").reshape(-1)
    g_small, off = {}, 0
    for n, length in SMALL:
        g_small[n] = small_sum[off:off + length].reshape(1, length)
        off += length
    g_conv = small_sum[off:off + CONV_W_LEN].reshape(CONV_WIDTH, CONV_DIM)
    g_small["conv_w"] = lax.dynamic_slice(g_conv, (0, chip * conv_cols), (CONV_WIDTH, conv_cols)).reshape(1, CONV_WIDTH, conv_cols)
    loss = small_sum[off + CONV_W_LEN]

    grads, deltas, new_m, new_v = [], [], [], []
    for n in order:
        g = g_big[n] if n in g_big else g_small[n]
        d, m_, v_ = _adamw(given[n], g, given["m_" + n], given["v_" + n], "adamw_" + n)
        grads.append(g)
        deltas.append(d)
        new_m.append(m_)
        new_v.append(v_)
    return (loss, grad_x.reshape(x.shape), *grads, *deltas, *new_m, *new_v)
```

```python
import functools
import math

import numpy as np
import jax
import jax.numpy as jnp
from jax import lax
from jax.experimental import pallas as pl
from jax.experimental.pallas import tpu as pltpu

F32 = jnp.float32
BF16 = jnp.bfloat16

D_MODEL = 2048
N_HEADS_MLA = 16
Q_LORA = 512
KV_LORA = 512
QK_NOPE = 128
QK_ROPE = 64
V_DIM = 128
QK_DIM = QK_NOPE + QK_ROPE
ROPE_THETA = 10000.0
D_INNER = 4096
SSM_HEADDIM = 64
N_HEADS_SSM = 64
SSM_GROUPS = 8
HEADS_PER_GROUP = 8
D_STATE = 128
CONV_WIDTH = 4
CHUNK = 256
CONV_DIM = D_INNER + 2 * SSM_GROUPS * D_STATE
D_FF = 5632
PLE_DIM = 256
EPS = 1e-6
IN_SPLITS = (Q_LORA, KV_LORA, QK_ROPE, D_INNER, CONV_DIM, N_HEADS_SSM, D_MODEL, D_MODEL)

ADAM_LR = 0.001
ADAM_B1 = 0.9
ADAM_B2 = 0.999
ADAM_EPS = 1e-08
ADAM_WD = 0.01
ADAM_STEP = 10

LANE = 128
SUBLANE = 8
HEAD_PAD = 256
VMEM_LIMIT = 56 * 1024 * 1024
ATTN_TILE = 512
NEG = -1e30

MESH_ID = pl.DeviceIdType.MESH
N_CHIPS = 4
N_DEV = 8


def _tile(n, pref, mult=LANE):
    if n <= pref:
        return n
    t = (pref // mult) * mult
    while t >= mult:
        if n % t == 0:
            return t
        t -= mult
    return n


def _params(sem, vmem=VMEM_LIMIT):
    return pltpu.CompilerParams(dimension_semantics=sem, vmem_limit_bytes=vmem)


def _mm(a, b, *, ta=False, tb=False, add=None, out_dtype=F32, name, tm=1024, tn=1024, tk=2048):
    if ta:
        K, M = a.shape
    else:
        M, K = a.shape
    N = b.shape[0] if tb else b.shape[1]
    assert (b.shape[1] if tb else b.shape[0]) == K, (a.shape, b.shape, ta, tb)
    tm, tn, tk = _tile(M, tm), _tile(N, tn), _tile(K, tk)
    nk = K // tk
    dn = (((0 if ta else 1,), (1 if tb else 0,)), ((), ()))
    has_add = add is not None

    def body(*refs):
        if has_add:
            a_ref, b_ref, c_ref, o_ref = refs[:4]
        else:
            a_ref, b_ref, o_ref = refs[:3]
        prod = lax.dot_general(a_ref[...].astype(BF16), b_ref[...].astype(BF16), dn, preferred_element_type=F32)
        if nk == 1:
            o_ref[...] = ((c_ref[...] + prod) if has_add else prod).astype(out_dtype)
            return
        acc = refs[-1]
        k = pl.program_id(2)

        @pl.when(k == 0)
        def _():
            acc[...] = (c_ref[...] + prod) if has_add else prod

        @pl.when(k > 0)
        def _():
            acc[...] += prod

        @pl.when(k == nk - 1)
        def _():
            o_ref[...] = acc[...].astype(out_dtype)

    a_spec = pl.BlockSpec((tk, tm), lambda i, j, k: (k, i)) if ta else pl.BlockSpec((tm, tk), lambda i, j, k: (i, k))
    b_spec = pl.BlockSpec((tn, tk), lambda i, j, k: (j, k)) if tb else pl.BlockSpec((tk, tn), lambda i, j, k: (k, j))
    in_specs = [a_spec, b_spec]
    args = [a, b]
    if has_add:
        in_specs.append(pl.BlockSpec((tm, tn), lambda i, j, k: (i, j)))
        args.append(add)
    return pl.pallas_call(
        body, name=name,
        out_shape=jax.ShapeDtypeStruct((M, N), out_dtype),
        grid=(M // tm, N // tn, nk),
        in_specs=in_specs,
        out_specs=pl.BlockSpec((tm, tn), lambda i, j, k: (i, j)),
        scratch_shapes=[pltpu.VMEM((tm, tn), F32)] if nk > 1 else [],
        compiler_params=_params(("parallel", "parallel", "arbitrary")),
    )(*args)


def _row(arr, width=None, cblk=0):
    return ("row", arr, arr.shape[1] if width is None else width, cblk)


def _full(arr):
    return ("full", arr)


def _prev8(arr):
    return ("prev8", arr)


def _next8(arr):
    return ("next8", arr)


def _rows(fn, n_rows, tm, ins, outs, name):
    tm = min(tm, n_rows)
    assert n_rows % tm == 0 and tm % SUBLANE == 0
    n = n_rows // tm
    in_specs, args = [], []
    for spec in ins:
        kind, arr = spec[0], spec[1]
        if kind == "row":
            _, _, w, cb = spec
            in_specs.append(pl.BlockSpec((tm, w), lambda i, cb=cb: (i, cb)))
        elif kind == "full":
            in_specs.append(pl.BlockSpec(arr.shape, lambda i, nd=arr.ndim: (0,) * nd))
        elif kind == "prev8":
            in_specs.append(pl.BlockSpec((SUBLANE, arr.shape[1]),
                                         lambda i: (jnp.maximum(i * (tm // SUBLANE) - 1, 0), 0)))
        elif kind == "next8":
            last = n_rows // SUBLANE - 1
            in_specs.append(pl.BlockSpec((SUBLANE, arr.shape[1]),
                                         lambda i: (jnp.minimum((i + 1) * (tm // SUBLANE), last), 0)))
        else:
            raise ValueError(kind)
        args.append(arr)
    out_shapes, out_specs = [], []
    any_acc = False
    for spec in outs:
        if spec[0] == "row":
            _, w, dt = spec
            out_shapes.append(jax.ShapeDtypeStruct((n_rows, w), dt))
            out_specs.append(pl.BlockSpec((tm, w), lambda i: (i, 0)))
        else:
            _, shp, dt = spec
            any_acc = True
            out_shapes.append(jax.ShapeDtypeStruct(shp, dt))
            out_specs.append(pl.BlockSpec(shp, lambda i, nd=len(shp): (0,) * nd))
    nin = len(ins)

    def body(*refs):
        i = pl.program_id(0)
        vals = fn(i, n, *[r[...] for r in refs[:nin]])
        for o_ref, spec, v in zip(refs[nin:], outs, vals):
            if spec[0] == "acc":
                @pl.when(i == 0)
                def _(o_ref=o_ref):
                    o_ref[...] = jnp.zeros_like(o_ref)

                o_ref[...] += v.astype(o_ref.dtype)
            else:
                o_ref[...] = v.astype(o_ref.dtype)

    res = pl.pallas_call(
        body, name=name,
        out_shape=tuple(out_shapes),
        grid=(n,),
        in_specs=in_specs,
        out_specs=tuple(out_specs),
        compiler_params=_params(("arbitrary",) if any_acc else ("parallel",)),
    )(*args)
    return res


def _rstd(x):
    return lax.rsqrt(jnp.mean(x * x, axis=-1, keepdims=True) + EPS)


def _norm_bwd(x, r, g, dy):
    xh = x * r
    dyg = dy * g
    dx = r * (dyg - xh * jnp.mean(dyg * xh, axis=-1, keepdims=True))
    return dx, dy * xh


def _sigmoid(x):
    return 1.0 / (1.0 + jnp.exp(-x))


def _colsum(v):
    return jnp.sum(v, axis=0, keepdims=True)


def _rope_tables(pos, invf):
    ang = pos.astype(F32) * invf
    lane = lax.broadcasted_iota(jnp.int32, ang.shape, 1)
    cos, sin = jnp.cos(ang), jnp.sin(ang)
    ct = jnp.where(lane < QK_ROPE, cos, 0.0)
    sa = jnp.where(lane < QK_ROPE // 2, -sin, 0.0)
    sb = jnp.where((lane >= QK_ROPE // 2) & (lane < QK_ROPE), sin, 0.0)
    return ct, sa, sb


def _rope(b, ct, sa, sb):
    return ct * b + sa * pltpu.roll(b, LANE - QK_ROPE // 2, 1) + sb * pltpu.roll(b, QK_ROPE // 2, 1)


def _rope_t(d, ct, sa, sb):
    return ct * d + pltpu.roll(sa * d, QK_ROPE // 2, 1) + pltpu.roll(sb * d, LANE - QK_ROPE // 2, 1)


def _rope_fwd(q_raw, kr_pad, pos_col, invf):
    S = q_raw.shape[0]

    def fn(i, n, q, kr, pos, invf):
        ct, sa, sb = _rope_tables(pos, invf)
        parts = []
        for h in range(N_HEADS_MLA):
            parts.append(q[:, h * HEAD_PAD:h * HEAD_PAD + LANE])
            parts.append(_rope(q[:, h * HEAD_PAD + LANE:(h + 1) * HEAD_PAD], ct, sa, sb))
        return jnp.concatenate(parts, axis=1), _rope(kr, ct, sa, sb)

    return _rows(fn, S, 256, [_row(q_raw), _row(kr_pad), _row(pos_col), _full(invf)],
                 [("row", N_HEADS_MLA * HEAD_PAD, BF16), ("row", LANE, BF16)], "rope_fwd")


def _rope_bwd(dq, dkp, pos_col, invf):
    S = dq.shape[0]
    tm = 256

    def body(dq_ref, dkp_ref, pos_ref, invf_ref, dqo_ref, dkr_ref):
        ct, sa, sb = _rope_tables(pos_ref[...], invf_ref[...])
        for h in range(N_HEADS_MLA):
            dqo_ref[:, h * HEAD_PAD:h * HEAD_PAD + LANE] = dq_ref[:, h * HEAD_PAD:h * HEAD_PAD + LANE].astype(BF16)
            dqo_ref[:, h * HEAD_PAD + LANE:(h + 1) * HEAD_PAD] = _rope_t(
                dq_ref[:, h * HEAD_PAD + LANE:(h + 1) * HEAD_PAD], ct, sa, sb).astype(BF16)
        tot = dkp_ref[0]
        for h in range(1, N_HEADS_MLA):
            tot = tot + dkp_ref[h]
        dkr_ref[...] = _rope_t(tot, ct, sa, sb).astype(BF16)

    return pl.pallas_call(
        body, name="rope_bwd",
        out_shape=(jax.ShapeDtypeStruct(dq.shape, BF16), jax.ShapeDtypeStruct((S, LANE), BF16)),
        grid=(S // tm,),
        in_specs=[pl.BlockSpec((tm, dq.shape[1]), lambda i: (i, 0)),
                  pl.BlockSpec((N_HEADS_MLA, tm, LANE), lambda i: (0, i, 0)),
                  pl.BlockSpec((tm, 1), lambda i: (i, 0)),
                  pl.BlockSpec((1, LANE), lambda i: (0, 0))],
        out_specs=(pl.BlockSpec((tm, dq.shape[1]), lambda i: (i, 0)), pl.BlockSpec((tm, LANE), lambda i: (i, 0))),
        compiler_params=_params(("parallel",)),
    )(dq, dkp, pos_col, invf)


def _row_of(col, n):
    eye = lax.broadcasted_iota(jnp.int32, (n, n), 0) == lax.broadcasted_iota(jnp.int32, (n, n), 1)
    return jnp.sum(jnp.where(eye, col, 0.0), axis=0, keepdims=True)


def _attn_fwd(q, kv, kp, tile):
    S = q.shape[0]
    nq = S // tile
    scale = QK_DIM ** -0.5
    nt = (((1,), (1,)), ((), ()))

    def body(q_ref, kv_ref, kp_ref, o_ref, lse_ref, m_s, l_s, acc_s, s_buf):
        qi = pl.program_id(1)
        qv = q_ref[...]
        m_s[...] = jnp.full_like(m_s, NEG)
        l_s[...] = jnp.zeros_like(l_s)
        acc_s[...] = jnp.zeros_like(acc_s)

        def scores(j):
            start = pl.multiple_of(j * tile, tile)
            k = jnp.concatenate([kv_ref[pl.ds(start, tile), 0:LANE], kp_ref[pl.ds(start, tile), :]], axis=1)
            return lax.dot_general(qv, k, nt, preferred_element_type=F32) * scale

        def update(s, j):
            v = kv_ref[pl.ds(pl.multiple_of(j * tile, tile), tile), LANE:2 * LANE]
            m_old = m_s[...]
            m_new = jnp.maximum(m_old, jnp.max(s, axis=1, keepdims=True))
            alpha = jnp.exp(m_old - m_new)
            p = jnp.exp(s - m_new)
            l_s[...] = alpha * l_s[...] + jnp.sum(p, axis=1, keepdims=True)
            acc_s[...] = alpha * acc_s[...] + jnp.dot(p.astype(BF16), v, preferred_element_type=F32)
            m_s[...] = m_new

        s_buf[0] = scores(0)

        def loop_body(j, carry):
            nxt = scores(j + 1)
            update(s_buf[lax.rem(j, 2)], j)
            s_buf[lax.rem(j + 1, 2)] = nxt
            return carry

        lax.fori_loop(0, qi, loop_body, 0)
        s = s_buf[lax.rem(qi, 2)]
        row = lax.broadcasted_iota(jnp.int32, s.shape, 0)
        col = lax.broadcasted_iota(jnp.int32, s.shape, 1)
        update(jnp.where(row >= col, s, NEG), qi)
        l = l_s[...]
        o_ref[...] = (acc_s[...] / l).astype(o_ref.dtype)
        lse_ref[0, 0] = _row_of(m_s[...] + jnp.log(l), tile)

    return pl.pallas_call(
        body, name="attn_fwd",
        out_shape=(jax.ShapeDtypeStruct((S, N_HEADS_MLA * V_DIM), BF16),
                   jax.ShapeDtypeStruct((N_HEADS_MLA, nq, 1, tile), F32)),
        grid=(N_HEADS_MLA, nq),
        in_specs=[pl.BlockSpec((tile, HEAD_PAD), lambda h, i: (i, h)),
                  pl.BlockSpec((S, HEAD_PAD), lambda h, i: (0, h)),
                  pl.BlockSpec((S, LANE), lambda h, i: (0, 0))],
        out_specs=(pl.BlockSpec((tile, V_DIM), lambda h, i: (i, h)),
                   pl.BlockSpec((1, 1, 1, tile), lambda h, i: (h, i, 0, 0))),
        scratch_shapes=[pltpu.VMEM((tile, 1), F32), pltpu.VMEM((tile, 1), F32), pltpu.VMEM((tile, V_DIM), F32),
                        pltpu.VMEM((2, tile, tile), F32)],
        compiler_params=_params(("parallel", "arbitrary")),
    )(q, kv, kp)


def _attn_delta(o, do, tile):
    S = o.shape[0]
    nq = S // tile

    def body(o_ref, do_ref, d_ref, dob_ref):
        dov = do_ref[...]
        prod = o_ref[...].astype(F32) * dov
        dob_ref[...] = dov.astype(BF16)
        for h in range(N_HEADS_MLA):
            col = jnp.sum(prod[:, h * V_DIM:(h + 1) * V_DIM], axis=1, keepdims=True)
            d_ref[h, 0] = _row_of(col, tile)

    return pl.pallas_call(
        body, name="attn_delta",
        out_shape=(jax.ShapeDtypeStruct((N_HEADS_MLA, nq, 1, tile), F32), jax.ShapeDtypeStruct(o.shape, BF16)),
        grid=(nq,),
        in_specs=[pl.BlockSpec((tile, o.shape[1]), lambda i: (i, 0)), pl.BlockSpec((tile, o.shape[1]), lambda i: (i, 0))],
        out_specs=(pl.BlockSpec((N_HEADS_MLA, 1, 1, tile), lambda i: (0, i, 0, 0)),
                   pl.BlockSpec((tile, o.shape[1]), lambda i: (i, 0))),
        compiler_params=_params(("parallel",)),
    )(o, do)


def _attn_bwd(q, kv, kp, do, lse, delta, tile):
    S = q.shape[0]
    nq = S // tile
    scale = QK_DIM ** -0.5
    nt = (((1,), (1,)), ((), ()))
    tn = (((0,), (0,)), ((), ()))

    def body(kv_ref, kp_ref, q_ref, do_ref, lse_ref, d_ref, dq_ref, dkv_ref, dkp_ref, dk_s, dv_s, st_buf, dpt_buf):
        ki = pl.program_id(1)
        k = jnp.concatenate([kv_ref[:, 0:LANE], kp_ref[...]], axis=1)
        v = kv_ref[:, LANE:2 * LANE]

        @pl.when(ki == 0)
        def _():
            dq_ref[...] = jnp.zeros_like(dq_ref)

        dk_s[...] = jnp.zeros_like(dk_s)
        dv_s[...] = jnp.zeros_like(dv_s)

        def products(qi):
            start = pl.multiple_of(qi * tile, tile)
            st = lax.dot_general(k, q_ref[pl.ds(start, tile), :], nt, preferred_element_type=F32) * scale
            return st, lax.dot_general(v, do_ref[pl.ds(start, tile), :], nt, preferred_element_type=F32)

        def update(qi, slot):
            start = pl.multiple_of(qi * tile, tile)
            qv = q_ref[pl.ds(start, tile), :]
            dov = do_ref[pl.ds(start, tile), :]
            pt = jnp.exp(st_buf[slot] - lse_ref[0, qi])
            krow = lax.broadcasted_iota(jnp.int32, pt.shape, 0)
            qcol = lax.broadcasted_iota(jnp.int32, pt.shape, 1)
            pt = jnp.where((krow <= qcol) | (qi > ki), pt, 0.0)
            dv_s[...] += jnp.dot(pt.astype(BF16), dov, preferred_element_type=F32)
            dst = (pt * (dpt_buf[slot] - d_ref[0, qi]) * scale).astype(BF16)
            dk_s[...] += jnp.dot(dst, qv, preferred_element_type=F32)
            dq_ref[pl.ds(start, tile), :] += lax.dot_general(dst, k, tn, preferred_element_type=F32)

        st_buf[0], dpt_buf[0] = products(ki)

        def loop_body(qi, carry):
            slot = lax.rem(qi - ki, 2)
            st_n, dpt_n = products(qi + 1)
            update(qi, slot)
            st_buf[1 - slot] = st_n
            dpt_buf[1 - slot] = dpt_n
            return carry

        lax.fori_loop(ki, nq - 1, loop_body, 0)
        update(nq - 1, lax.rem(nq - 1 - ki, 2))
        dkv_ref[...] = jnp.concatenate([dk_s[:, 0:LANE], dv_s[...]], axis=1).astype(dkv_ref.dtype)
        dkp_ref[0] = dk_s[:, LANE:2 * LANE]

    return pl.pallas_call(
        body, name="attn_bwd",
        out_shape=(jax.ShapeDtypeStruct((S, N_HEADS_MLA * HEAD_PAD), F32),
                   jax.ShapeDtypeStruct((S, N_HEADS_MLA * HEAD_PAD), BF16),
                   jax.ShapeDtypeStruct((N_HEADS_MLA, S, LANE), F32)),
        grid=(N_HEADS_MLA, nq),
        in_specs=[pl.BlockSpec((tile, HEAD_PAD), lambda h, i: (i, h)),
                  pl.BlockSpec((tile, LANE), lambda h, i: (i, 0)),
                  pl.BlockSpec((S, HEAD_PAD), lambda h, i: (0, h)),
                  pl.BlockSpec((S, V_DIM), lambda h, i: (0, h)),
                  pl.BlockSpec((1, nq, 1, tile), lambda h, i: (h, 0, 0, 0)),
                  pl.BlockSpec((1, nq, 1, tile), lambda h, i: (h, 0, 0, 0))],
        out_specs=(pl.BlockSpec((S, HEAD_PAD), lambda h, i: (0, h)),
                   pl.BlockSpec((tile, HEAD_PAD), lambda h, i: (i, h)),
                   pl.BlockSpec((1, tile, LANE), lambda h, i: (h, i, 0))),
        scratch_shapes=[pltpu.VMEM((tile, HEAD_PAD), F32), pltpu.VMEM((tile, V_DIM), F32),
                        pltpu.VMEM((2, tile, tile), F32), pltpu.VMEM((2, tile, tile), F32)],
        compiler_params=_params(("parallel", "arbitrary")),
    )(kv, kp, q, do, lse, delta)


def _shift_down(cur, halo, k):
    sh = pltpu.roll(cur, k, 0)
    hs = pltpu.roll(halo, k, 0)
    rows = lax.broadcasted_iota(jnp.int32, hs.shape, 0)
    first = jnp.where(rows < k, hs, sh[0:SUBLANE])
    if cur.shape[0] == SUBLANE:
        return first
    return jnp.concatenate([first, sh[SUBLANE:]], axis=0)


def _shift_up(cur, nxt, k):
    n = cur.shape[0]
    sh = pltpu.roll(cur, n - k, 0)
    ns = pltpu.roll(nxt, SUBLANE - k, 0)
    rows = lax.broadcasted_iota(jnp.int32, ns.shape, 0)
    last = jnp.where(rows >= SUBLANE - k, ns, sh[n - SUBLANE:])
    if n == SUBLANE:
        return last
    return jnp.concatenate([sh[:n - SUBLANE], last], axis=0)


def _conv_pre(cur, halo, w, b):
    out = b + w[3:4] * cur
    for k in range(1, CONV_WIDTH):
        out = out + w[3 - k:4 - k] * _shift_down(cur, halo, k)
    return out


def _conv_fwd(xbc, w, b):
    S = xbc.shape[0]

    def fn(i, n, cur, prev, w, b):
        halo = jnp.where(i > 0, prev, 0.0)
        pre = _conv_pre(cur, halo, w, b)
        return (pre * _sigmoid(pre),)

    return _rows(fn, S, 256, [_row(xbc), _prev8(xbc), _full(w), _full(b)], [("row", xbc.shape[1], F32)], "conv_fwd")[0]


def _conv_bwd(xbc, dact, w, b):
    S, C = xbc.shape

    def dsilu(pre):
        s = _sigmoid(pre)
        return s * (1.0 + pre * (1.0 - s))

    def fn(i, n, cur, prev, nxt, dcur, dnxt, w, b):
        halo = jnp.where(i > 0, prev, 0.0)
        pre = _conv_pre(cur, halo, w, b)
        dpre = dcur * dsilu(pre)
        pre_n = _conv_pre(nxt, cur[cur.shape[0] - SUBLANE:], w, b)
        dpre_n = jnp.where(i < n - 1, dnxt * dsilu(pre_n), 0.0)
        dx = w[3:4] * dpre
        rows = lax.broadcasted_iota(jnp.int32, (SUBLANE, C), 0)
        dw = jnp.where(rows == 3, _colsum(dpre * cur), 0.0)
        for k in range(1, CONV_WIDTH):
            dx = dx + w[3 - k:4 - k] * _shift_up(dpre, dpre_n, k)
            dw = dw + jnp.where(rows == 3 - k, _colsum(dpre * _shift_down(cur, halo, k)), 0.0)
        return dx, dw, _colsum(dpre)

    return _rows(fn, S, 256, [_row(xbc), _prev8(xbc), _next8(xbc), _row(dact), _next8(dact), _full(w), _full(b)],
                 [("row", C, BF16), ("acc", (SUBLANE, C), F32), ("acc", (1, C), F32)], "conv_bwd")


def _softplus(x):
    return jnp.maximum(x, 0.0) + jnp.log1p(jnp.exp(-jnp.abs(x)))


def _cumsum_rows(x):
    rows = lax.broadcasted_iota(jnp.int32, x.shape, 0)
    s = 1
    while s < x.shape[0]:
        x = x + jnp.where(rows >= s, pltpu.roll(x, s, 0), 0.0)
        s *= 2
    return x


def _revcumsum_rows(x):
    n = x.shape[0]
    rows = lax.broadcasted_iota(jnp.int32, x.shape, 0)
    s = 1
    while s < n:
        x = x + jnp.where(rows < n - s, pltpu.roll(x, n - s, 0), 0.0)
        s *= 2
    return x


def _dt_prep(dt_raw, dt_bias, a_log):
    S = dt_raw.shape[0]

    def body(raw_ref, bias_ref, alog_ref, dt_ref, cum_ref, cumt_ref):
        dt = _softplus(raw_ref[...] + bias_ref[...])
        cum = _cumsum_rows(dt * (-jnp.exp(alog_ref[...])))
        dt_ref[...] = dt
        cum_ref[...] = cum
        cumt_ref[...] = cum.T

    return pl.pallas_call(
        body, name="dt_prep",
        out_shape=(jax.ShapeDtypeStruct((S, LANE), F32), jax.ShapeDtypeStruct((S, LANE), F32),
                   jax.ShapeDtypeStruct((LANE, S), F32)),
        grid=(S // CHUNK,),
        in_specs=[pl.BlockSpec((CHUNK, LANE), lambda i: (i, 0)), pl.BlockSpec((1, LANE), lambda i: (0, 0)),
                  pl.BlockSpec((1, LANE), lambda i: (0, 0))],
        out_specs=(pl.BlockSpec((CHUNK, LANE), lambda i: (i, 0)), pl.BlockSpec((CHUNK, LANE), lambda i: (i, 0)),
                   pl.BlockSpec((LANE, CHUNK), lambda i: (0, i))),
        compiler_params=_params(("parallel",)),
    )(dt_raw, dt_bias, a_log)


def _group_cols(t):
    S = t.shape[0]
    return jnp.transpose(t[:, :N_HEADS_SSM].reshape(S, SSM_GROUPS, HEADS_PER_GROUP), (1, 0, 2))


def _ungroup_cols(t):
    S = t.shape[1]
    flat = jnp.transpose(t, (1, 0, 2)).reshape(S, N_HEADS_SSM)
    return jnp.pad(flat, ((0, 0), (0, LANE - N_HEADS_SSM)))


_NT = (((1,), (1,)), ((), ()))
_TN = (((0,), (0,)), ((), ()))
P = SSM_HEADDIM
GW = HEADS_PER_GROUP * SSM_HEADDIM


def _decay(cc, cr):
    L = cc.shape[0]
    i = lax.broadcasted_iota(jnp.int32, (L, L), 0)
    j = lax.broadcasted_iota(jnp.int32, (L, L), 1)
    return jnp.exp(jnp.where(i >= j, cc - cr, NEG))


def _ssd_fwd(xbc_c, dt_g, cum_g, cumt_g):
    S = xbc_c.shape[0]
    nc = S // CHUNK
    L = CHUNK
    boff = D_INNER // D_STATE

    def body(x_ref, b_ref, c_ref, dt_ref, cum_ref, cumt_ref, y_ref, st_ref, state):
        c = pl.program_id(1)

        @pl.when(c == 0)
        def _():
            state[...] = jnp.zeros_like(state)

        bm = b_ref[...].astype(BF16)
        cm = c_ref[...].astype(BF16)
        cb = lax.dot_general(cm, bm, _NT, preferred_element_type=F32)
        for r in range(HEADS_PER_GROUP):
            cc = cum_ref[0, :, r:r + 1]
            cr = cumt_ref[0, r:r + 1, :]
            m = (cb * _decay(cc, cr)).astype(BF16)
            xdt = x_ref[:, r * P:(r + 1) * P] * dt_ref[0, :, r:r + 1]
            st = state[r * P:(r + 1) * P, :]
            st_ref[0, 0, r * P:(r + 1) * P, :] = st
            y = jnp.dot(m, xdt.astype(BF16), preferred_element_type=F32)
            y = y + lax.dot_general(cm, st.astype(BF16), _NT, preferred_element_type=F32) * jnp.exp(cc)
            y_ref[:, r * P:(r + 1) * P] = y
            cl = cum_ref[0, L - 1:L, r:r + 1]
            wend = jnp.exp(cl - cc)
            state[r * P:(r + 1) * P, :] = st * jnp.exp(cl) + lax.dot_general(
                (xdt * wend).astype(BF16), bm, _TN, preferred_element_type=F32)

    return pl.pallas_call(
        body, name="ssd_fwd",
        out_shape=(jax.ShapeDtypeStruct((S, D_INNER), F32), jax.ShapeDtypeStruct((SSM_GROUPS, nc, GW, D_STATE), F32)),
        grid=(SSM_GROUPS, nc),
        in_specs=[pl.BlockSpec((L, GW), lambda g, c: (c, g)),
                  pl.BlockSpec((L, D_STATE), lambda g, c: (c, boff + g)),
                  pl.BlockSpec((L, D_STATE), lambda g, c: (c, boff + SSM_GROUPS + g)),
                  pl.BlockSpec((1, L, HEADS_PER_GROUP), lambda g, c: (g, c, 0)),
                  pl.BlockSpec((1, L, HEADS_PER_GROUP), lambda g, c: (g, c, 0)),
                  pl.BlockSpec((1, HEADS_PER_GROUP, L), lambda g, c: (g, 0, c))],
        out_specs=(pl.BlockSpec((L, GW), lambda g, c: (c, g)),
                   pl.BlockSpec((1, 1, GW, D_STATE), lambda g, c: (g, c, 0, 0))),
        scratch_shapes=[pltpu.VMEM((GW, D_STATE), F32)],
        compiler_params=_params(("parallel", "arbitrary")),
    )(xbc_c, xbc_c, xbc_c, dt_g, cum_g, cumt_g)


def _ssd_bwd(xbc_c, dt_g, cum_g, cumt_g, states, dy, dx_skip):
    S = xbc_c.shape[0]
    nc = S // CHUNK
    L = CHUNK
    boff = D_INNER // D_STATE
    rev = lambda c: nc - 1 - c

    def body(x_ref, b_ref, c_ref, dt_ref, cum_ref, cumt_ref, st_ref, dy_ref, skip_ref,
             dx_ref, db_ref, dc_ref, ddt_ref, dcum_ref, dstate):
        c = pl.program_id(1)

        @pl.when(c == 0)
        def _():
            dstate[...] = jnp.zeros_like(dstate)

        bf = b_ref[...]
        bm = bf.astype(BF16)
        cm = c_ref[...].astype(BF16)
        cb = lax.dot_general(cm, bm, _NT, preferred_element_type=F32)
        dcb = jnp.zeros((L, L), F32)
        dbs = jnp.zeros((L, D_STATE), F32)
        dcs = jnp.zeros((L, D_STATE), F32)
        rowid = lax.broadcasted_iota(jnp.int32, (L, 1), 0)
        for r in range(HEADS_PER_GROUP):
            sl = slice(r * P, (r + 1) * P)
            cc = cum_ref[0, :, r:r + 1]
            cr = cumt_ref[0, r:r + 1, :]
            dtc = dt_ref[0, :, r:r + 1]
            decay = _decay(cc, cr)
            m = cb * decay
            xr = x_ref[:, sl]
            xdt = xr * dtc
            xdb = xdt.astype(BF16)
            dyr = dy_ref[:, sl]
            dyb = dyr.astype(BF16)
            st = st_ref[0, 0, sl, :]
            stb = st.astype(BF16)
            ds = dstate[sl, :]
            dsb = ds.astype(BF16)
            ecc = jnp.exp(cc)
            cl = cum_ref[0, L - 1:L, r:r + 1]
            ecl = jnp.exp(cl)
            wend = jnp.exp(cl - cc)

            g = lax.dot_general(dyb, xdb, _NT, preferred_element_type=F32)
            q = g * m
            dcb = dcb + g * decay
            dcum = jnp.sum(q, axis=1, keepdims=True) - _row_of_t(jnp.sum(q, axis=0, keepdims=True), L)
            dxd = lax.dot_general(m.astype(BF16), dyb, _TN, preferred_element_type=F32)
            dxd = dxd + lax.dot_general(bm, dsb, _NT, preferred_element_type=F32) * wend
            yoff = lax.dot_general(cm, stb, _NT, preferred_element_type=F32) * ecc
            dcum = dcum + jnp.sum(dyr * yoff, axis=1, keepdims=True)
            dcs = dcs + jnp.dot(dyb, stb, preferred_element_type=F32) * ecc
            t = jnp.dot(xdb, dsb, preferred_element_type=F32)
            dbs = dbs + t * wend
            vj = jnp.sum(t * bf, axis=1, keepdims=True) * wend
            dcum = dcum - vj
            dlast = jnp.sum(vj) + ecl * jnp.sum(ds * st)
            dcum = dcum + jnp.where(rowid == L - 1, dlast, 0.0)
            dstate[sl, :] = ecl * ds + lax.dot_general((dyr * ecc).astype(BF16), cm, _TN, preferred_element_type=F32)

            dx_ref[:, sl] = dxd * dtc + skip_ref[:, sl]
            ddt_ref[0, :, r:r + 1] = jnp.sum(dxd * xr, axis=1, keepdims=True)
            dcum_ref[0, :, r:r + 1] = dcum
        dcbb = dcb.astype(BF16)
        dc_ref[...] = dcs + jnp.dot(dcbb, bm, preferred_element_type=F32)
        db_ref[...] = dbs + lax.dot_general(dcbb, cm, _TN, preferred_element_type=F32)

    return pl.pallas_call(
        body, name="ssd_bwd",
        out_shape=(jax.ShapeDtypeStruct((S, D_INNER), F32),
                   jax.ShapeDtypeStruct((S, SSM_GROUPS * D_STATE), F32),
                   jax.ShapeDtypeStruct((S, SSM_GROUPS * D_STATE), F32),
                   jax.ShapeDtypeStruct((SSM_GROUPS, S, HEADS_PER_GROUP), F32),
                   jax.ShapeDtypeStruct((SSM_GROUPS, S, HEADS_PER_GROUP), F32)),
        grid=(SSM_GROUPS, nc),
        in_specs=[pl.BlockSpec((L, GW), lambda g, c: (rev(c), g)),
                  pl.BlockSpec((L, D_STATE), lambda g, c: (rev(c), boff + g)),
                  pl.BlockSpec((L, D_STATE), lambda g, c: (rev(c), boff + SSM_GROUPS + g)),
                  pl.BlockSpec((1, L, HEADS_PER_GROUP), lambda g, c: (g, rev(c), 0)),
                  pl.BlockSpec((1, L, HEADS_PER_GROUP), lambda g, c: (g, rev(c), 0)),
                  pl.BlockSpec((1, HEADS_PER_GROUP, L), lambda g, c: (g, 0, rev(c))),
                  pl.BlockSpec((1, 1, GW, D_STATE), lambda g, c: (g, rev(c), 0, 0)),
                  pl.BlockSpec((L, GW), lambda g, c: (rev(c), g)),
                  pl.BlockSpec((L, GW), lambda g, c: (rev(c), g))],
        out_specs=(pl.BlockSpec((L, GW), lambda g, c: (rev(c), g)),
                   pl.BlockSpec((L, D_STATE), lambda g, c: (rev(c), g)),
                   pl.BlockSpec((L, D_STATE), lambda g, c: (rev(c), g)),
                   pl.BlockSpec((1, L, HEADS_PER_GROUP), lambda g, c: (g, rev(c), 0)),
                   pl.BlockSpec((1, L, HEADS_PER_GROUP), lambda g, c: (g, rev(c), 0))),
        scratch_shapes=[pltpu.VMEM((GW, D_STATE), F32)],
        compiler_params=_params(("parallel", "arbitrary")),
    )(xbc_c, xbc_c, xbc_c, dt_g, cum_g, cumt_g, states, dy, dx_skip)


def _row_of_t(row, n):
    eye = lax.broadcasted_iota(jnp.int32, (n, n), 0) == lax.broadcasted_iota(jnp.int32, (n, n), 1)
    return jnp.sum(jnp.where(eye, row, 0.0), axis=1, keepdims=True)


def _dt_bwd(dt_raw, dt_bias, a_log, ddt_x, dcum):
    S = dt_raw.shape[0]

    def fn(i, n, raw, ddx, dcu, bias, alog):
        xx = raw + bias
        dt = _softplus(xx)
        a = -jnp.exp(alog)
        dda = _revcumsum_rows(dcu)
        ddt = ddx + dda * a
        lane = lax.broadcasted_iota(jnp.int32, raw.shape, 1)
        draw = jnp.where(lane < N_HEADS_SSM, ddt * _sigmoid(xx), 0.0)
        return draw, _colsum(draw), _colsum(dda * dt) * a

    return _rows(fn, S, CHUNK, [_row(dt_raw), _row(ddt_x), _row(dcum), _full(dt_bias), _full(a_log)],
                 [("row", LANE, BF16), ("acc", (1, LANE), F32), ("acc", (1, LANE), F32)], "dt_bwd")


def _adamw(w, g, m, v, name):
    shape = w.shape
    cols = shape[-1]
    rows = int(np.prod(shape[:-1]))
    w2, g2, m2, v2 = (t.reshape(rows, cols) for t in (w, g, m, v))
    tr = rows if rows * cols <= 512 * 1024 else _tile(rows, max(SUBLANE, (512 * 1024 // cols) // SUBLANE * SUBLANE), SUBLANE)
    c1 = 1.0 - ADAM_B1 ** ADAM_STEP
    c2 = 1.0 - ADAM_B2 ** ADAM_STEP

    def body(w_ref, g_ref, m_ref, v_ref, d_ref, mo_ref, vo_ref):
        gv = g_ref[...]
        mn = ADAM_B1 * m_ref[...] + (1.0 - ADAM_B1) * gv
        vn = ADAM_B2 * v_ref[...] + (1.0 - ADAM_B2) * (gv * gv)
        d_ref[...] = -ADAM_LR * ((mn / c1) / (jnp.sqrt(vn / c2) + ADAM_EPS) + ADAM_WD * w_ref[...])
        mo_ref[...] = mn
        vo_ref[...] = vn

    spec = pl.BlockSpec((tr, cols), lambda i: (i, 0))
    outs = pl.pallas_call(
        body, name=name,
        out_shape=tuple(jax.ShapeDtypeStruct((rows, cols), F32) for _ in range(3)),
        grid=(rows // tr,),
        in_specs=[spec] * 4, out_specs=(spec,) * 3,
        compiler_params=_params(("parallel",)),
    )(w2, g2, m2, v2)
    return tuple(o.reshape(shape) for o in outs)


def _prep_weights(w_in, w_uq):
    offs = np.cumsum((0,) + IN_SPLITS)
    pad = lambda t: jnp.pad(t, ((0, 0), (0, LANE - t.shape[1])))
    pieces = dict(
        qkv=w_in[:, offs[0]:offs[2]],
        kr=pad(w_in[:, offs[2]:offs[3]]),
        z=w_in[:, offs[3]:offs[4]],
        xbc=w_in[:, offs[4]:offs[5]],
        dt=pad(w_in[:, offs[5]:offs[6]]),
        g=w_in[:, offs[6]:offs[8]],
    )
    uq = w_uq.reshape(Q_LORA, N_HEADS_MLA, QK_DIM)
    uq = jnp.pad(uq, ((0, 0), (0, 0), (0, HEAD_PAD - QK_DIM))).reshape(Q_LORA, N_HEADS_MLA * HEAD_PAD)
    return pieces, uq


def _local_step(x, p, positions, W, sp, target):
    S = x.shape[0]
    tile = min(ATTN_TILE, S)
    pos_col = positions.reshape(S, 1)
    invf = ROPE_THETA ** (-jnp.arange(0, QK_ROPE, 2, dtype=F32) / QK_ROPE)
    invf = jnp.pad(jnp.concatenate([invf, invf]), (0, LANE - QK_ROPE)).reshape(1, LANE)
    wp, w_uq_p = _prep_weights(W["w_in"], W["w_uq"])
    padl = lambda t: jnp.pad(t, ((0, 0), (0, LANE - t.shape[1])))
    dt_bias_p, a_log_p = padl(sp["dt_bias"]), padl(sp["a_log"])
    dskip_ch = jnp.repeat(sp["d_skip"], SSM_HEADDIM, axis=1)
    p_bf = p.astype(BF16)
    RW = 256

    (u_bf,) = _rows(lambda i, n, x, g: (x * _rstd(x) * g,), S, RW, [_row(x), _full(sp["mix_norm_pre"])],
                    [("row", D_MODEL, BF16)], "norm_pre")
    cqkv = _mm(u_bf, wp["qkv"], name="mm_qkv")
    z = _mm(u_bf, wp["z"], name="mm_z")
    xbc = _mm(u_bf, wp["xbc"], name="mm_xbc")
    gates = _mm(u_bf, wp["g"], name="mm_gates")
    kr_pad = _mm(u_bf, wp["kr"], name="mm_kr")
    dt_raw = _mm(u_bf, wp["dt"], name="mm_dt")

    def qkv_norm(i, n, cq, ckv, gq, gkv):
        return cq * _rstd(cq) * gq, ckv * _rstd(ckv) * gkv

    cqn, ckvn = _rows(qkv_norm, S, 512, [_row(cqkv, Q_LORA, 0), _row(cqkv, KV_LORA, 1), _full(sp["q_norm"]), _full(sp["kv_norm"])],
                      [("row", Q_LORA, BF16), ("row", KV_LORA, BF16)], "qkv_norm")
    q_raw = _mm(cqn, w_uq_p, name="mm_uq")
    kv = _mm(ckvn, W["w_ukv"], out_dtype=BF16, name="mm_ukv")
    q_bf, kp_bf = _rope_fwd(q_raw, kr_pad, pos_col, invf)
    attn, lse = _attn_fwd(q_bf, kv, kp_bf, tile)

    xbc_c = _conv_fwd(xbc, sp["conv_w"], sp["conv_b"])
    dt, cum, cumt = _dt_prep(dt_raw, dt_bias_p, a_log_p)
    dt_g, cum_g = _group_cols(dt), _group_cols(cum)
    cumt_g = cumt[:N_HEADS_SSM].reshape(SSM_GROUPS, HEADS_PER_GROUP, S)
    y, states = _ssd_fwd(xbc_c, dt_g, cum_g, cumt_g)

    GN = D_INNER // SSM_GROUPS

    def gated(y, xs, z, dsk):
        yt = y + dsk * xs
        sz = _sigmoid(z)
        return yt, sz, yt * (z * sz)

    def gated_norm(i, n, y, xs, z, dsk, gn):
        _, _, yg = gated(y, xs, z, dsk)
        parts = []
        for g in range(SSM_GROUPS):
            blk = yg[:, g * GN:(g + 1) * GN]
            parts.append(blk * _rstd(blk) * gn[:, g * GN:(g + 1) * GN])
        return (jnp.concatenate(parts, axis=1),)

    (ssm,) = _rows(gated_norm, S, 128, [_row(y), _row(xbc_c, D_INNER, 0), _row(z), _full(dskip_ch), _full(sp["ssm_norm"])],
                   [("row", D_INNER, BF16)], "gated_norm")

    a_o = _mm(attn, W["w_attn_o"], name="mm_attn_o")
    b_o = _mm(ssm, W["w_ssm_o"], name="mm_ssm_o")

    def mix(i, n, ga, gs, a, b):
        return (_sigmoid(ga) * a + _sigmoid(gs) * b,)

    (mixed,) = _rows(mix, S, RW, [_row(gates, D_MODEL, 0), _row(gates, D_MODEL, 1), _row(a_o), _row(b_o)],
                     [("row", D_MODEL, BF16)], "mix")
    m2 = _mm(mixed, W["w_out"], name="mm_out")

    def post(i, n, h, m, gpost, gpre):
        hn = h + m * _rstd(m) * gpost
        return hn, hn * _rstd(hn) * gpre

    h1, f_bf = _rows(post, S, RW, [_row(x), _row(m2), _full(sp["mix_norm_post"]), _full(sp["ffn_norm_pre"])],
                     [("row", D_MODEL, F32), ("row", D_MODEL, BF16)], "post_mix")
    ga = _mm(f_bf, W["w_gate"], name="mm_gate")
    up = _mm(f_bf, W["w_up"], name="mm_up")
    (s_bf,) = _rows(lambda i, n, a, b: (a * _sigmoid(a) * b,), S, RW, [_row(ga), _row(up)], [("row", D_FF, BF16)], "swiglu")
    f2 = _mm(s_bf, W["w_down"], name="mm_down")
    h2, n3_bf = _rows(post, S, RW, [_row(h1), _row(f2), _full(sp["ffn_norm_post"]), _full(sp["ple_norm_pre"])],
                      [("row", D_MODEL, F32), ("row", D_MODEL, BF16)], "post_ffn")
    gpre = _mm(n3_bf, W["w_ple_gate"], name="mm_ple_gate")
    pe = _mm(p_bf, W["w_ple"], name="mm_ple")

    def ple_loss(i, n, h2, gpre, pe, tgt, gpost):
        gate = _sigmoid(gpre)
        e = pe * gate
        r = _rstd(e)
        diff = h2 + e * r * gpost - tgt
        loss = 0.5 * jnp.sum(jnp.mean(diff * diff, axis=1, keepdims=True))
        dh3 = diff * (1.0 / D_MODEL)
        de, dg_rows = _norm_bwd(e, r, gpost, dh3)
        return (jnp.full((1, LANE), loss, F32), dh3, de * gate, de * pe * gate * (1.0 - gate), _colsum(dg_rows))

    loss, dh3, dpe, dgpre, g_ple_post = _rows(
        ple_loss, S, 128, [_row(h2), _row(gpre), _row(pe), _row(target), _full(sp["ple_norm_post"])],
        [("acc", (1, LANE), F32), ("row", D_MODEL, F32), ("row", D_MODEL, BF16), ("row", D_MODEL, BF16),
         ("acc", (1, D_MODEL), F32)], "ple_loss")

    gw = {}
    gs = {"ple_norm_post": g_ple_post}
    gw["w_ple"] = _mm(p_bf, dpe, ta=True, name="mmg_ple")
    gw["w_ple_gate"] = _mm(n3_bf, dgpre, ta=True, name="mmg_ple_gate")
    dn3 = _mm(dgpre, W["w_ple_gate"], tb=True, name="mmb_ple_gate")

    def post_bwd(i, n, h, m, dhn, dn, gpost, gpre):
        rm = _rstd(m)
        hn = h + m * rm * gpost
        dx, dgpre_rows = _norm_bwd(hn, _rstd(hn), gpre, dn)
        dhn_t = dhn + dx
        dm, dgpost_rows = _norm_bwd(m, rm, gpost, dhn_t)
        return dhn_t, dm, _colsum(dgpre_rows), _colsum(dgpost_rows)

    def run_post_bwd(h, m, dhn, dn, gpost, gpre, name):
        return _rows(post_bwd, S, 128, [_row(h), _row(m), _row(dhn), _row(dn), _full(gpost), _full(gpre)],
                     [("row", D_MODEL, F32), ("row", D_MODEL, BF16), ("acc", (1, D_MODEL), F32), ("acc", (1, D_MODEL), F32)], name)

    dh2, df2, gs["ple_norm_pre"], gs["ffn_norm_post"] = run_post_bwd(
        h1, f2, dh3, dn3, sp["ffn_norm_post"], sp["ple_norm_pre"], "post_ffn_bwd")
    gw["w_down"] = _mm(s_bf, df2, ta=True, name="mmg_down")
    ds = _mm(df2, W["w_down"], tb=True, name="mmb_down")

    def swiglu_bwd(i, n, a, b, ds):
        sa = _sigmoid(a)
        return ds * b * (sa * (1.0 + a * (1.0 - sa))), ds * (a * sa)

    dga, dup = _rows(swiglu_bwd, S, RW, [_row(ga), _row(up), _row(ds)], [("row", D_FF, BF16), ("row", D_FF, BF16)], "swiglu_bwd")
    gw["w_gate"] = _mm(f_bf, dga, ta=True, name="mmg_gate")
    gw["w_up"] = _mm(f_bf, dup, ta=True, name="mmg_up")
    df = _mm(dga, W["w_gate"], tb=True, name="mmb_gate")
    df = _mm(dup, W["w_up"], tb=True, add=df, name="mmb_up")
    dh1, dm2, gs["ffn_norm_pre"], gs["mix_norm_post"] = run_post_bwd(
        x, m2, dh2, df, sp["mix_norm_post"], sp["ffn_norm_pre"], "post_mix_bwd")
    gw["w_out"] = _mm(mixed, dm2, ta=True, name="mmg_out")
    dmixed = _mm(dm2, W["w_out"], tb=True, name="mmb_out")

    def mix_bwd(i, n, ga, gs_, a, b, dm):
        sa, ss = _sigmoid(ga), _sigmoid(gs_)
        return dm * sa, dm * ss, jnp.concatenate([dm * a * sa * (1.0 - sa), dm * b * ss * (1.0 - ss)], axis=1)

    da_o, db_o, dgates = _rows(mix_bwd, S, RW, [_row(gates, D_MODEL, 0), _row(gates, D_MODEL, 1), _row(a_o), _row(b_o), _row(dmixed)],
                               [("row", D_MODEL, BF16), ("row", D_MODEL, BF16), ("row", 2 * D_MODEL, BF16)], "mix_bwd")
    gw["w_attn_o"] = _mm(attn, da_o, ta=True, name="mmg_attn_o")
    dattn = _mm(da_o, W["w_attn_o"], tb=True, name="mmb_attn_o")
    gw["w_ssm_o"] = _mm(ssm, db_o, ta=True, name="mmg_ssm_o")
    dssm = _mm(db_o, W["w_ssm_o"], tb=True, name="mmb_ssm_o")

    delta, dattn_bf = _attn_delta(attn, dattn, tile)
    dq, dkv, dkp = _attn_bwd(q_bf, kv, kp_bf, dattn_bf, lse, delta, tile)
    dq_raw, dkr = _rope_bwd(dq, dkp, pos_col, invf)
    g_uq_p = _mm(cqn, dq_raw, ta=True, name="mmg_uq")
    gw["w_uq"] = g_uq_p.reshape(Q_LORA, N_HEADS_MLA, HEAD_PAD)[:, :, :QK_DIM].reshape(Q_LORA, N_HEADS_MLA * QK_DIM)
    dcqn = _mm(dq_raw, w_uq_p, tb=True, name="mmb_uq")
    gw["w_ukv"] = _mm(ckvn, dkv, ta=True, name="mmg_ukv")
    dckvn = _mm(dkv, W["w_ukv"], tb=True, name="mmb_ukv")

    def qkv_norm_bwd(i, n, cq, ckv, dq_, dkv_, gq, gkv):
        dcq, gq_rows = _norm_bwd(cq, _rstd(cq), gq, dq_)
        dckv, gkv_rows = _norm_bwd(ckv, _rstd(ckv), gkv, dkv_)
        return jnp.concatenate([dcq, dckv], axis=1), _colsum(gq_rows), _colsum(gkv_rows)

    dcqkv, gs["q_norm"], gs["kv_norm"] = _rows(
        qkv_norm_bwd, S, 512, [_row(cqkv, Q_LORA, 0), _row(cqkv, KV_LORA, 1), _row(dcqn), _row(dckvn), _full(sp["q_norm"]), _full(sp["kv_norm"])],
        [("row", Q_LORA + KV_LORA, BF16), ("acc", (1, Q_LORA), F32), ("acc", (1, KV_LORA), F32)], "qkv_norm_bwd")

    def gated_norm_bwd(i, n, y, xs, z, dssm, dsk, gn):
        yt, sz, yg = gated(y, xs, z, dsk)
        dyg_parts, gn_parts = [], []
        for g in range(SSM_GROUPS):
            sl = slice(g * GN, (g + 1) * GN)
            blk = yg[:, sl]
            dblk, rows = _norm_bwd(blk, _rstd(blk), gn[:, sl], dssm[:, sl])
            dyg_parts.append(dblk)
            gn_parts.append(_colsum(rows))
        dyg = jnp.concatenate(dyg_parts, axis=1)
        dyt = dyg * (z * sz)
        dz = dyg * yt * (sz * (1.0 + z * (1.0 - sz)))
        return dyt, dz, dyt * dsk, jnp.concatenate(gn_parts, axis=1), _colsum(dyt * xs)

    dy, dz, dx_skip, gs["ssm_norm"], g_dskip_ch = _rows(
        gated_norm_bwd, S, 128, [_row(y), _row(xbc_c, D_INNER, 0), _row(z), _row(dssm), _full(dskip_ch), _full(sp["ssm_norm"])],
        [("row", D_INNER, F32), ("row", D_INNER, BF16), ("row", D_INNER, F32), ("acc", (1, D_INNER), F32), ("acc", (1, D_INNER), F32)],
        "gated_norm_bwd")
    gs["d_skip"] = jnp.sum(g_dskip_ch.reshape(N_HEADS_SSM, SSM_HEADDIM), axis=1).reshape(1, N_HEADS_SSM)
    dxs, dbm, dcm, ddt_x, dcum = _ssd_bwd(xbc_c, dt_g, cum_g, cumt_g, states, dy, dx_skip)
    ddt_raw, g_dtb, g_alog = _dt_bwd(dt_raw, dt_bias_p, a_log_p, _ungroup_cols(ddt_x), _ungroup_cols(dcum))
    gs["dt_bias"] = g_dtb[:, :N_HEADS_SSM]
    gs["a_log"] = g_alog[:, :N_HEADS_SSM]
    dxbc_c = jnp.concatenate([dxs, dbm, dcm], axis=1)
    dxbc, g_conv_w8, gs["conv_b"] = _conv_bwd(xbc, dxbc_c, sp["conv_w"], sp["conv_b"])
    gs["conv_w"] = g_conv_w8[:CONV_WIDTH]

    g_qkv = _mm(u_bf, dcqkv, ta=True, name="mmg_qkv")
    g_kr = _mm(u_bf, dkr, ta=True, name="mmg_kr")
    g_z = _mm(u_bf, dz, ta=True, name="mmg_z")
    g_xbc = _mm(u_bf, dxbc, ta=True, name="mmg_xbc")
    g_dt = _mm(u_bf, ddt_raw, ta=True, name="mmg_dt")
    g_g = _mm(u_bf, dgates, ta=True, name="mmg_gates")
    gw["w_in"] = jnp.concatenate([g_qkv, g_kr[:, :QK_ROPE], g_z, g_xbc, g_dt[:, :N_HEADS_SSM], g_g], axis=1)
    du = _mm(dcqkv, wp["qkv"], tb=True, name="mmb_qkv")
    du = _mm(dkr, wp["kr"], tb=True, add=du, name="mmb_kr")
    du = _mm(dz, wp["z"], tb=True, add=du, name="mmb_z")
    du = _mm(dxbc, wp["xbc"], tb=True, add=du, name="mmb_xbc")
    du = _mm(ddt_raw, wp["dt"], tb=True, add=du, name="mmb_dt")
    du = _mm(dgates, wp["g"], tb=True, add=du, name="mmb_gates")

    def pre_bwd(i, n, x, du, dh, g):
        dx, rows = _norm_bwd(x, _rstd(x), g, du)
        return dh + dx, _colsum(rows)

    grad_x, gs["mix_norm_pre"] = _rows(pre_bwd, S, RW, [_row(x), _row(du), _row(dh1), _full(sp["mix_norm_pre"])],
                                       [("row", D_MODEL, F32), ("acc", (1, D_MODEL), F32)], "norm_pre_bwd")
    return loss, grad_x, gw, gs


BIG = (
    ("w_in", (2048, 3872), 1), ("w_uq", (512, 768), 1), ("w_ukv", (512, 1024), 1), ("w_attn_o", (512, 2048), 0),
    ("w_ssm_o", (1024, 2048), 0), ("w_out", (512, 2048), 0), ("w_gate", (2048, 1408), 1), ("w_up", (2048, 1408), 1),
    ("w_down", (1408, 2048), 0), ("w_ple_gate", (512, 2048), 0), ("w_ple", (256, 512), 1),
)
SMALL = (
    ("mix_norm_pre", 2048), ("mix_norm_post", 2048), ("q_norm", 512), ("kv_norm", 512), ("conv_b", 6144), ("dt_bias", 64),
    ("a_log", 64), ("d_skip", 64), ("ssm_norm", 4096), ("ffn_norm_pre", 2048), ("ffn_norm_post", 2048),
    ("ple_norm_pre", 2048), ("ple_norm_post", 2048),
)
CONV_W_LEN = CONV_WIDTH * CONV_DIM
SMALL_ROWS = 384


def _place():
    return lax.axis_index("x"), lax.axis_index("y"), lax.axis_index("c")


def _flip(v, bit):
    return 1 - v if bit else v


def _start_then_wait(copies):
    for cp in copies:
        cp.start()
    for cp in copies:
        cp.wait()


def _hbm_call(body, name, ins, out_shapes, sems):
    return pl.pallas_call(
        body, name=name, out_shape=tuple(out_shapes),
        in_specs=[pl.BlockSpec(memory_space=pl.ANY)] * len(ins),
        out_specs=tuple(pl.BlockSpec(memory_space=pl.ANY) for _ in out_shapes),
        scratch_shapes=[pltpu.SemaphoreType.DMA((s,)) for s in sems],
    )(*ins)


def _gather_chips(shards):
    n = len(shards)

    def body(*refs):
        ins, outs = refs[:n], refs[n:2 * n]
        send_sems, recv_sems, local_sems = refs[2 * n:]
        x, y, c = _place()
        copies = []
        for a in range(n):
            mine = outs[a].at[2 * x + y]
            copies.append(pltpu.make_async_copy(ins[a], mine, local_sems.at[a]))
            for k in (1, 2, 3):
                peer = (_flip(x, k >> 1), _flip(y, k & 1), c)
                copies.append(pltpu.make_async_remote_copy(
                    src_ref=ins[a], dst_ref=mine, send_sem=send_sems.at[3 * a + k - 1], recv_sem=recv_sems.at[3 * a + k - 1],
                    device_id=peer, device_id_type=MESH_ID))
        _start_then_wait(copies)

    return _hbm_call(body, "gather_chips", shards,
                     [jax.ShapeDtypeStruct((N_CHIPS, *s.shape), s.dtype) for s in shards], (3 * n, 3 * n, n))


def _swap_halves(gs):
    n = len(gs)

    def body(*refs):
        ins, outs = refs[:n], refs[n:2 * n]
        send_sems, recv_sems = refs[2 * n:]
        x, y, c = _place()
        copies = []
        for a in range(n):
            half = gs[a].shape[1] // 2
            src = ins[a].at[:, pl.ds(pl.multiple_of((1 - c) * half, SUBLANE), half), :]
            copies.append(pltpu.make_async_remote_copy(
                src_ref=src, dst_ref=outs[a], send_sem=send_sems.at[a], recv_sem=recv_sems.at[a],
                device_id=(x, y, 1 - c), device_id_type=MESH_ID))
        _start_then_wait(copies)

    return _hbm_call(body, "swap_halves", gs,
                     [jax.ShapeDtypeStruct((g.shape[0], g.shape[1] // 2, g.shape[2]), g.dtype) for g in gs], (n, n))


def _sum_rows_tile(rows, cols):
    return _tile(rows, max(2 * SUBLANE, (512 * 1024 // cols) // (2 * SUBLANE) * (2 * SUBLANE)), 2 * SUBLANE)


def _add_half(g, other, c, name):
    n, R, C = g.shape
    half = R // 2
    tr = _sum_rows_tile(half, C)
    nb = half // tr

    def body(c_ref, g_ref, o_ref, out_ref):
        out_ref[...] = (g_ref[...] + o_ref[...]).astype(out_ref.dtype)

    return pl.pallas_call(
        body, name=name,
        out_shape=jax.ShapeDtypeStruct((n, half, C), BF16),
        grid_spec=pltpu.PrefetchScalarGridSpec(
            num_scalar_prefetch=1, grid=(n, nb),
            in_specs=[pl.BlockSpec((1, tr, C), lambda j, i, c_ref: (j, c_ref[0] * nb + i, 0)),
                      pl.BlockSpec((1, tr, C), lambda j, i, c_ref: (j, i, 0))],
            out_specs=pl.BlockSpec((1, tr, C), lambda j, i, c_ref: (j, i, 0))),
        compiler_params=_params(("parallel", "parallel")),
    )(c, g, other)


def _scatter_chips(parts):
    n = len(parts)

    def body(*refs):
        ins, outs = refs[:n], refs[n:2 * n]
        send_sems, recv_sems = refs[2 * n:]
        x, y, c = _place()
        copies = []
        for a in range(n):
            for k in (1, 2, 3):
                px, py = _flip(x, k >> 1), _flip(y, k & 1)
                copies.append(pltpu.make_async_remote_copy(
                    src_ref=ins[a].at[2 * px + py], dst_ref=outs[a].at[k - 1], send_sem=send_sems.at[3 * a + k - 1],
                    recv_sem=recv_sems.at[3 * a + k - 1], device_id=(px, py, c), device_id_type=MESH_ID))
        _start_then_wait(copies)

    return _hbm_call(body, "scatter_chips", parts,
                     [jax.ShapeDtypeStruct((3, *p.shape[1:]), p.dtype) for p in parts], (3 * n, 3 * n))


def _add_chips(part, got, chip, name):
    n, R, C = part.shape
    tr = _sum_rows_tile(R, C)

    def body(chip_ref, p_ref, g_ref, out_ref):
        out_ref[...] = ((p_ref[0].astype(F32) + g_ref[0].astype(F32)) + g_ref[1].astype(F32)) + g_ref[2].astype(F32)

    return pl.pallas_call(
        body, name=name,
        out_shape=jax.ShapeDtypeStruct((R, C), F32),
        grid_spec=pltpu.PrefetchScalarGridSpec(
            num_scalar_prefetch=1, grid=(R // tr,),
            in_specs=[pl.BlockSpec((1, tr, C), lambda i, chip_ref: (chip_ref[0], i, 0)),
                      pl.BlockSpec((3, tr, C), lambda i, chip_ref: (0, i, 0))],
            out_specs=pl.BlockSpec((tr, C), lambda i, chip_ref: (i, 0))),
        compiler_params=_params(("parallel",)),
    )(chip, part, got)


def _join_halves(mines):
    n = len(mines)

    def body(*refs):
        ins, outs = refs[:n], refs[n:2 * n]
        send_sems, recv_sems, local_sems = refs[2 * n:]
        x, y, c = _place()
        copies = []
        for a in range(n):
            half = mines[a].shape[0]
            dst = outs[a].at[pl.ds(pl.multiple_of(c * half, SUBLANE), half), :]
            copies.append(pltpu.make_async_copy(ins[a], dst, local_sems.at[a]))
            copies.append(pltpu.make_async_remote_copy(
                src_ref=ins[a], dst_ref=dst, send_sem=send_sems.at[a], recv_sem=recv_sems.at[a],
                device_id=(x, y, 1 - c), device_id_type=MESH_ID))
        _start_then_wait(copies)

    return _hbm_call(body, "join_halves", mines,
                     [jax.ShapeDtypeStruct((2 * m.shape[0], m.shape[1]), m.dtype) for m in mines], (n, n, n))


def _allreduce_small(vec, name):
    R, C = vec.shape

    def body(v_ref, o_ref, buf, send_sems, recv_sems):
        x, y, c = _place()
        me = 4 * x + 2 * y + c
        buf[me] = v_ref[...]
        copies = []
        for k in range(1, N_DEV):
            peer = (_flip(x, (k >> 2) & 1), _flip(y, (k >> 1) & 1), _flip(c, k & 1))
            copies.append(pltpu.make_async_remote_copy(
                src_ref=v_ref, dst_ref=buf.at[me], send_sem=send_sems.at[k - 1], recv_sem=recv_sems.at[k - 1],
                device_id=peer, device_id_type=MESH_ID))
        for cp in copies:
            cp.start()
        for cp in copies:
            cp.wait()
        tot = buf[0]
        for d in range(1, N_DEV):
            tot = tot + buf[d]
        o_ref[...] = tot

    return pl.pallas_call(
        body, name=name,
        out_shape=jax.ShapeDtypeStruct((R, C), F32),
        in_specs=[pl.BlockSpec(memory_space=pltpu.VMEM)],
        out_specs=pl.BlockSpec(memory_space=pltpu.VMEM),
        scratch_shapes=[pltpu.VMEM((N_DEV, R, C), F32), pltpu.SemaphoreType.DMA((N_DEV - 1,)), pltpu.SemaphoreType.DMA((N_DEV - 1,))],
    )(vec)


def _unstack(gathered, shape, axis):
    if axis == 0:
        return gathered.reshape(N_CHIPS * shape[0], shape[1])
    return jnp.concatenate([gathered[j] for j in range(N_CHIPS)], axis=1)


def _stack(whole, shape, axis):
    if axis == 0:
        return whole.reshape(N_CHIPS, shape[0], shape[1])
    return jnp.stack([whole[:, j * shape[1]:(j + 1) * shape[1]] for j in range(N_CHIPS)])


def kernel(x, p, positions, mix_norm_pre, mix_norm_post, w_in, q_norm, w_uq, kv_norm, w_ukv, conv_w, conv_b, dt_bias, a_log, d_skip, ssm_norm, w_attn_o, w_ssm_o, w_out, ffn_norm_pre, ffn_norm_post, w_gate, w_up, w_down, ple_norm_pre, ple_norm_post, w_ple_gate, w_ple, loss_target, m_mix_norm_pre, m_mix_norm_post, m_w_in, m_q_norm, m_w_uq, m_kv_norm, m_w_ukv, m_conv_w, m_conv_b, m_dt_bias, m_a_log, m_d_skip, m_ssm_norm, m_w_attn_o, m_w_ssm_o, m_w_out, m_ffn_norm_pre, m_ffn_norm_post, m_w_gate, m_w_up, m_w_down, m_ple_norm_pre, m_ple_norm_post, m_w_ple_gate, m_w_ple, v_mix_norm_pre, v_mix_norm_post, v_w_in, v_q_norm, v_w_uq, v_kv_norm, v_w_ukv, v_conv_w, v_conv_b, v_dt_bias, v_a_log, v_d_skip, v_ssm_norm, v_w_attn_o, v_w_ssm_o, v_w_out, v_ffn_norm_pre, v_ffn_norm_post, v_w_gate, v_w_up, v_w_down, v_ple_norm_pre, v_ple_norm_post, v_w_ple_gate, v_w_ple):
    given = dict(locals())
    names = [n for n, _, _ in BIG] + [n for n, _ in SMALL] + ["conv_w"]
    order = ["mix_norm_pre", "mix_norm_post", "w_in", "q_norm", "w_uq", "kv_norm", "w_ukv", "conv_w", "conv_b", "dt_bias", "a_log",
             "d_skip", "ssm_norm", "w_attn_o", "w_ssm_o", "w_out", "ffn_norm_pre", "ffn_norm_post", "w_gate", "w_up", "w_down",
             "ple_norm_pre", "ple_norm_post", "w_ple_gate", "w_ple"]
    assert sorted(names) == sorted(order)
    cx, cy, cc = _place()
    chip = 2 * cx + cy
    conv_cols = CONV_DIM // N_CHIPS

    gathered = _gather_chips([given[n][0].astype(BF16) for n, _, _ in BIG])
    W = {n: _unstack(g, shape, axis) for (n, shape, axis), g in zip(BIG, gathered)}
    own = jnp.where(cc == 0, conv_w[0], 0.0)
    conv_vec = lax.dynamic_update_slice(jnp.zeros((CONV_WIDTH, CONV_DIM), F32), own, (0, chip * conv_cols))
    conv_full = _allreduce_small(conv_vec.reshape(CONV_W_LEN // LANE, LANE), "gather_conv_w").reshape(CONV_WIDTH, CONV_DIM)
    sp = {n: given[n] for n, _ in SMALL}
    sp["conv_w"] = conv_full

    loss_part, grad_x, gw, gs = _local_step(x[0], p[0, 0], positions[0], W, sp, loss_target[0])

    stacked = [_stack(gw[n], shape, axis) for n, shape, axis in BIG]
    c_arr = cc.reshape(1).astype(jnp.int32)
    chip_arr = chip.reshape(1).astype(jnp.int32)
    parts = [_add_half(g, o, c_arr, "add_half_" + n) for (n, _, _), g, o in zip(BIG, stacked, _swap_halves(stacked))]
    mines = [_add_chips(q, o, chip_arr, "add_chips_" + n) for (n, _, _), q, o in zip(BIG, parts, _scatter_chips(parts))]
    g_big = {n: r.reshape(1, *shape) for (n, shape, _), r in zip(BIG, _join_halves(mines))}

    small_parts = [gs[n] for n, _ in SMALL] + [gs["conv_w"], loss_part[:, :1]]
    small_vec = jnp.concatenate([t.reshape(-1) for t in small_parts])
    small_vec = jnp.pad(small_vec, (0, SMALL_ROWS * LANE - small_vec.shape[0])).reshape(SMALL_ROWS, LANE)
    small_sum = _allreduce_small(small_vec, "allreduce_small").reshape(-1)
    g_small, off = {}, 0
    for n, length in SMALL:
        g_small[n] = small_sum[off:off + length].reshape(1, length)
        off += length
    g_conv = small_sum[off:off + CONV_W_LEN].reshape(CONV_WIDTH, CONV_DIM)
    g_small["conv_w"] = lax.dynamic_slice(g_conv, (0, chip * conv_cols), (CONV_WIDTH, conv_cols)).reshape(1, CONV_WIDTH, conv_cols)
    loss = small_sum[off + CONV_W_LEN]

    grads, deltas, new_m, new_v = [], [], [], []
    for n in order:
        g = g_big[n] if n in g_big else g_small[n]
        d, m_, v_ = _adamw(given[n], g, given["m_" + n], given["v_" + n], "adamw_" + n)
        grads.append(g)
        deltas.append(d)
        new_m.append(m_)
        new_v.append(v_)
    return (loss, grad_x.reshape(x.shape), *grads, *deltas, *new_m, *new_v)
```

```python
import functools
import math

import numpy as np
import jax
import jax.numpy as jnp
from jax import lax
from jax.experimental import pallas as pl
from jax.experimental.pallas import tpu as pltpu

F32 = jnp.float32
BF16 = jnp.bfloat16

D_MODEL = 2048
N_HEADS_MLA = 16
Q_LORA = 512
KV_LORA = 512
QK_NOPE = 128
QK_ROPE = 64
V_DIM = 128
QK_DIM = QK_NOPE + QK_ROPE
ROPE_THETA = 10000.0
D_INNER = 4096
SSM_HEADDIM = 64
N_HEADS_SSM = 64
SSM_GROUPS = 8
HEADS_PER_GROUP = 8
D_STATE = 128
CONV_WIDTH = 4
CHUNK = 256
CONV_DIM = D_INNER + 2 * SSM_GROUPS * D_STATE
D_FF = 5632
PLE_DIM = 256
EPS = 1e-6
IN_SPLITS = (Q_LORA, KV_LORA, QK_ROPE, D_INNER, CONV_DIM, N_HEADS_SSM, D_MODEL, D_MODEL)

ADAM_LR = 0.001
ADAM_B1 = 0.9
ADAM_B2 = 0.999
ADAM_EPS = 1e-08
ADAM_WD = 0.01
ADAM_STEP = 10

LANE = 128
SUBLANE = 8
HEAD_PAD = 256
VMEM_LIMIT = 56 * 1024 * 1024
ATTN_TILE = 512
NEG = -1e30

MESH_ID = pl.DeviceIdType.MESH
N_CHIPS = 4
N_DEV = 8


def _tile(n, pref, mult=LANE):
    if n <= pref:
        return n
    t = (pref // mult) * mult
    while t >= mult:
        if n % t == 0:
            return t
        t -= mult
    return n


def _params(sem, vmem=VMEM_LIMIT):
    return pltpu.CompilerParams(dimension_semantics=sem, vmem_limit_bytes=vmem)


def _mm(a, b, *, ta=False, tb=False, add=None, out_dtype=F32, name, tm=1024, tn=1024, tk=2048):
    if ta:
        K, M = a.shape
    else:
        M, K = a.shape
    N = b.shape[0] if tb else b.shape[1]
    assert (b.shape[1] if tb else b.shape[0]) == K, (a.shape, b.shape, ta, tb)
    tm, tn, tk = _tile(M, tm), _tile(N, tn), _tile(K, tk)
    nk = K // tk
    dn = (((0 if ta else 1,), (1 if tb else 0,)), ((), ()))
    has_add = add is not None

    def body(*refs):
        if has_add:
            a_ref, b_ref, c_ref, o_ref = refs[:4]
        else:
            a_ref, b_ref, o_ref = refs[:3]
        prod = lax.dot_general(a_ref[...].astype(BF16), b_ref[...].astype(BF16), dn, preferred_element_type=F32)
        if nk == 1:
            o_ref[...] = ((c_ref[...] + prod) if has_add else prod).astype(out_dtype)
            return
        acc = refs[-1]
        k = pl.program_id(2)

        @pl.when(k == 0)
        def _():
            acc[...] = (c_ref[...] + prod) if has_add else prod

        @pl.when(k > 0)
        def _():
            acc[...] += prod

        @pl.when(k == nk - 1)
        def _():
            o_ref[...] = acc[...].astype(out_dtype)

    a_spec = pl.BlockSpec((tk, tm), lambda i, j, k: (k, i)) if ta else pl.BlockSpec((tm, tk), lambda i, j, k: (i, k))
    b_spec = pl.BlockSpec((tn, tk), lambda i, j, k: (j, k)) if tb else pl.BlockSpec((tk, tn), lambda i, j, k: (k, j))
    in_specs = [a_spec, b_spec]
    args = [a, b]
    if has_add:
        in_specs.append(pl.BlockSpec((tm, tn), lambda i, j, k: (i, j)))
        args.append(add)
    return pl.pallas_call(
        body, name=name,
        out_shape=jax.ShapeDtypeStruct((M, N), out_dtype),
        grid=(M // tm, N // tn, nk),
        in_specs=in_specs,
        out_specs=pl.BlockSpec((tm, tn), lambda i, j, k: (i, j)),
        scratch_shapes=[pltpu.VMEM((tm, tn), F32)] if nk > 1 else [],
        compiler_params=_params(("parallel", "parallel", "arbitrary")),
    )(*args)


def _row(arr, width=None, cblk=0):
    return ("row", arr, arr.shape[1] if width is None else width, cblk)


def _full(arr):
    return ("full", arr)


def _prev8(arr):
    return ("prev8", arr)


def _next8(arr):
    return ("next8", arr)


def _rows(fn, n_rows, tm, ins, outs, name):
    tm = min(tm, n_rows)
    assert n_rows % tm == 0 and tm % SUBLANE == 0
    n = n_rows // tm
    in_specs, args = [], []
    for spec in ins:
        kind, arr = spec[0], spec[1]
        if kind == "row":
            _, _, w, cb = spec
            in_specs.append(pl.BlockSpec((tm, w), lambda i, cb=cb: (i, cb)))
        elif kind == "full":
            in_specs.append(pl.BlockSpec(arr.shape, lambda i, nd=arr.ndim: (0,) * nd))
        elif kind == "prev8":
            in_specs.append(pl.BlockSpec((SUBLANE, arr.shape[1]),
                                         lambda i: (jnp.maximum(i * (tm // SUBLANE) - 1, 0), 0)))
        elif kind == "next8":
            last = n_rows // SUBLANE - 1
            in_specs.append(pl.BlockSpec((SUBLANE, arr.shape[1]),
                                         lambda i: (jnp.minimum((i + 1) * (tm // SUBLANE), last), 0)))
        else:
            raise ValueError(kind)
        args.append(arr)
    out_shapes, out_specs = [], []
    any_acc = False
    for spec in outs:
        if spec[0] == "row":
            _, w, dt = spec
            out_shapes.append(jax.ShapeDtypeStruct((n_rows, w), dt))
            out_specs.append(pl.BlockSpec((tm, w), lambda i: (i, 0)))
        else:
            _, shp, dt = spec
            any_acc = True
            out_shapes.append(jax.ShapeDtypeStruct(shp, dt))
            out_specs.append(pl.BlockSpec(shp, lambda i, nd=len(shp): (0,) * nd))
    nin = len(ins)

    def body(*refs):
        i = pl.program_id(0)
        vals = fn(i, n, *[r[...] for r in refs[:nin]])
        for o_ref, spec, v in zip(refs[nin:], outs, vals):
            if spec[0] == "acc":
                @pl.when(i == 0)
                def _(o_ref=o_ref):
                    o_ref[...] = jnp.zeros_like(o_ref)

                o_ref[...] += v.astype(o_ref.dtype)
            else:
                o_ref[...] = v.astype(o_ref.dtype)

    res = pl.pallas_call(
        body, name=name,
        out_shape=tuple(out_shapes),
        grid=(n,),
        in_specs=in_specs,
        out_specs=tuple(out_specs),
        compiler_params=_params(("arbitrary",) if any_acc else ("parallel",)),
    )(*args)
    return res


def _rstd(x):
    return lax.rsqrt(jnp.mean(x * x, axis=-1, keepdims=True) + EPS)


def _norm_bwd(x, r, g, dy):
    xh = x * r
    dyg = dy * g
    dx = r * (dyg - xh * jnp.mean(dyg * xh, axis=-1, keepdims=True))
    return dx, dy * xh


def _sigmoid(x):
    return 1.0 / (1.0 + jnp.exp(-x))


def _colsum(v):
    return jnp.sum(v, axis=0, keepdims=True)


def _rope_tables(pos, invf):
    ang = pos.astype(F32) * invf
    lane = lax.broadcasted_iota(jnp.int32, ang.shape, 1)
    cos, sin = jnp.cos(ang), jnp.sin(ang)
    ct = jnp.where(lane < QK_ROPE, cos, 0.0)
    sa = jnp.where(lane < QK_ROPE // 2, -sin, 0.0)
    sb = jnp.where((lane >= QK_ROPE // 2) & (lane < QK_ROPE), sin, 0.0)
    return ct, sa, sb


def _rope(b, ct, sa, sb):
    return ct * b + sa * pltpu.roll(b, LANE - QK_ROPE // 2, 1) + sb * pltpu.roll(b, QK_ROPE // 2, 1)


def _rope_t(d, ct, sa, sb):
    return ct * d + pltpu.roll(sa * d, QK_ROPE // 2, 1) + pltpu.roll(sb * d, LANE - QK_ROPE // 2, 1)


def _rope_fwd(q_raw, kr_pad, pos_col, invf):
    S = q_raw.shape[0]

    def fn(i, n, q, kr, pos, invf):
        ct, sa, sb = _rope_tables(pos, invf)
        parts = []
        for h in range(N_HEADS_MLA):
            parts.append(q[:, h * HEAD_PAD:h * HEAD_PAD + LANE])
            parts.append(_rope(q[:, h * HEAD_PAD + LANE:(h + 1) * HEAD_PAD], ct, sa, sb))
        return jnp.concatenate(parts, axis=1), _rope(kr, ct, sa, sb)

    return _rows(fn, S, 256, [_row(q_raw), _row(kr_pad), _row(pos_col), _full(invf)],
                 [("row", N_HEADS_MLA * HEAD_PAD, BF16), ("row", LANE, BF16)], "rope_fwd")


def _rope_bwd(dq, dkp, pos_col, invf):
    S = dq.shape[0]
    tm = 256

    def body(dq_ref, dkp_ref, pos_ref, invf_ref, dqo_ref, dkr_ref):
        ct, sa, sb = _rope_tables(pos_ref[...], invf_ref[...])
        for h in range(N_HEADS_MLA):
            dqo_ref[:, h * HEAD_PAD:h * HEAD_PAD + LANE] = dq_ref[:, h * HEAD_PAD:h * HEAD_PAD + LANE].astype(BF16)
            dqo_ref[:, h * HEAD_PAD + LANE:(h + 1) * HEAD_PAD] = _rope_t(
                dq_ref[:, h * HEAD_PAD + LANE:(h + 1) * HEAD_PAD], ct, sa, sb).astype(BF16)
        tot = dkp_ref[0]
        for h in range(1, N_HEADS_MLA):
            tot = tot + dkp_ref[h]
        dkr_ref[...] = _rope_t(tot, ct, sa, sb).astype(BF16)

    return pl.pallas_call(
        body, name="rope_bwd",
        out_shape=(jax.ShapeDtypeStruct(dq.shape, BF16), jax.ShapeDtypeStruct((S, LANE), BF16)),
        grid=(S // tm,),
        in_specs=[pl.BlockSpec((tm, dq.shape[1]), lambda i: (i, 0)),
                  pl.BlockSpec((N_HEADS_MLA, tm, LANE), lambda i: (0, i, 0)),
                  pl.BlockSpec((tm, 1), lambda i: (i, 0)),
                  pl.BlockSpec((1, LANE), lambda i: (0, 0))],
        out_specs=(pl.BlockSpec((tm, dq.shape[1]), lambda i: (i, 0)), pl.BlockSpec((tm, LANE), lambda i: (i, 0))),
        compiler_params=_params(("parallel",)),
    )(dq, dkp, pos_col, invf)


def _row_of(col, n):
    eye = lax.broadcasted_iota(jnp.int32, (n, n), 0) == lax.broadcasted_iota(jnp.int32, (n, n), 1)
    return jnp.sum(jnp.where(eye, col, 0.0), axis=0, keepdims=True)


def _attn_fwd(q, kv, kp, tile):
    S = q.shape[0]
    nq = S // tile
    scale = QK_DIM ** -0.5
    nt = (((1,), (1,)), ((), ()))

    def body(q_ref, kv_ref, kp_ref, o_ref, lse_ref, m_s, l_s, acc_s, s_buf):
        qi = pl.program_id(1)
        qv = q_ref[...]
        m_s[...] = jnp.full_like(m_s, NEG)
        l_s[...] = jnp.zeros_like(l_s)
        acc_s[...] = jnp.zeros_like(acc_s)

        def scores(j):
            start = pl.multiple_of(j * tile, tile)
            k = jnp.concatenate([kv_ref[pl.ds(start, tile), 0:LANE], kp_ref[pl.ds(start, tile), :]], axis=1)
            return lax.dot_general(qv, k, nt, preferred_element_type=F32) * scale

        def update(s, j):
            v = kv_ref[pl.ds(pl.multiple_of(j * tile, tile), tile), LANE:2 * LANE]
            m_old = m_s[...]
            m_new = jnp.maximum(m_old, jnp.max(s, axis=1, keepdims=True))
            alpha = jnp.exp(m_old - m_new)
            p = jnp.exp(s - m_new)
            l_s[...] = alpha * l_s[...] + jnp.sum(p, axis=1, keepdims=True)
            acc_s[...] = alpha * acc_s[...] + jnp.dot(p.astype(BF16), v, preferred_element_type=F32)
            m_s[...] = m_new

        s_buf[0] = scores(0)

        def loop_body(j, carry):
            nxt = scores(j + 1)
            update(s_buf[lax.rem(j, 2)], j)
            s_buf[lax.rem(j + 1, 2)] = nxt
            return carry

        lax.fori_loop(0, qi, loop_body, 0)
        s = s_buf[lax.rem(qi, 2)]
        row = lax.broadcasted_iota(jnp.int32, s.shape, 0)
        col = lax.broadcasted_iota(jnp.int32, s.shape, 1)
        update(jnp.where(row >= col, s, NEG), qi)
        l = l_s[...]
        o_ref[...] = (acc_s[...] / l).astype(o_ref.dtype)
        lse_ref[0, 0] = _row_of(m_s[...] + jnp.log(l), tile)

    return pl.pallas_call(
        body, name="attn_fwd",
        out_shape=(jax.ShapeDtypeStruct((S, N_HEADS_MLA * V_DIM), BF16),
                   jax.ShapeDtypeStruct((N_HEADS_MLA, nq, 1, tile), F32)),
        grid=(N_HEADS_MLA, nq),
        in_specs=[pl.BlockSpec((tile, HEAD_PAD), lambda h, i: (i, h)),
                  pl.BlockSpec((S, HEAD_PAD), lambda h, i: (0, h)),
                  pl.BlockSpec((S, LANE), lambda h, i: (0, 0))],
        out_specs=(pl.BlockSpec((tile, V_DIM), lambda h, i: (i, h)),
                   pl.BlockSpec((1, 1, 1, tile), lambda h, i: (h, i, 0, 0))),
        scratch_shapes=[pltpu.VMEM((tile, 1), F32), pltpu.VMEM((tile, 1), F32), pltpu.VMEM((tile, V_DIM), F32),
                        pltpu.VMEM((2, tile, tile), F32)],
        compiler_params=_params(("parallel", "arbitrary")),
    )(q, kv, kp)


def _attn_delta(o, do, tile):
    S = o.shape[0]
    nq = S // tile

    def body(o_ref, do_ref, d_ref, dob_ref):
        dov = do_ref[...]
        prod = o_ref[...].astype(F32) * dov
        dob_ref[...] = dov.astype(BF16)
        for h in range(N_HEADS_MLA):
            col = jnp.sum(prod[:, h * V_DIM:(h + 1) * V_DIM], axis=1, keepdims=True)
            d_ref[h, 0] = _row_of(col, tile)

    return pl.pallas_call(
        body, name="attn_delta",
        out_shape=(jax.ShapeDtypeStruct((N_HEADS_MLA, nq, 1, tile), F32), jax.ShapeDtypeStruct(o.shape, BF16)),
        grid=(nq,),
        in_specs=[pl.BlockSpec((tile, o.shape[1]), lambda i: (i, 0)), pl.BlockSpec((tile, o.shape[1]), lambda i: (i, 0))],
        out_specs=(pl.BlockSpec((N_HEADS_MLA, 1, 1, tile), lambda i: (0, i, 0, 0)),
                   pl.BlockSpec((tile, o.shape[1]), lambda i: (i, 0))),
        compiler_params=_params(("parallel",)),
    )(o, do)


def _attn_bwd(q, kv, kp, do, lse, delta, tile):
    S = q.shape[0]
    nq = S // tile
    scale = QK_DIM ** -0.5
    nt = (((1,), (1,)), ((), ()))
    tn = (((0,), (0,)), ((), ()))

    def body(kv_ref, kp_ref, q_ref, do_ref, lse_ref, d_ref, dq_ref, dkv_ref, dkp_ref, dk_s, dv_s):
        ki = pl.program_id(1)
        k = jnp.concatenate([kv_ref[:, 0:LANE], kp_ref[...]], axis=1)
        v = kv_ref[:, LANE:2 * LANE]

        @pl.when(ki == 0)
        def _():
            dq_ref[...] = jnp.zeros_like(dq_ref)

        dk_s[...] = jnp.zeros_like(dk_s)
        dv_s[...] = jnp.zeros_like(dv_s)

        def step(qi, masked):
            start = pl.multiple_of(qi * tile, tile)
            qv = q_ref[pl.ds(start, tile), :]
            dov = do_ref[pl.ds(start, tile), :]
            st = lax.dot_general(k, qv, nt, preferred_element_type=F32) * scale
            pt = jnp.exp(st - lse_ref[0, qi])
            if masked:
                krow = lax.broadcasted_iota(jnp.int32, pt.shape, 0)
                qcol = lax.broadcasted_iota(jnp.int32, pt.shape, 1)
                pt = jnp.where(krow <= qcol, pt, 0.0)
            dv_s[...] += jnp.dot(pt.astype(BF16), dov, preferred_element_type=F32)
            dpt = lax.dot_general(v, dov, nt, preferred_element_type=F32)
            dst = (pt * (dpt - d_ref[0, qi]) * scale).astype(BF16)
            dk_s[...] += jnp.dot(dst, qv, preferred_element_type=F32)
            dq_ref[pl.ds(start, tile), :] += lax.dot_general(dst, k, tn, preferred_element_type=F32)

        step(ki, True)

        def loop_body(qi, carry):
            step(qi, False)
            return carry

        lax.fori_loop(ki + 1, nq, loop_body, 0)
        dkv_ref[...] = jnp.concatenate([dk_s[:, 0:LANE], dv_s[...]], axis=1).astype(dkv_ref.dtype)
        dkp_ref[0] = dk_s[:, LANE:2 * LANE]

    return pl.pallas_call(
        body, name="attn_bwd",
        out_shape=(jax.ShapeDtypeStruct((S, N_HEADS_MLA * HEAD_PAD), F32),
                   jax.ShapeDtypeStruct((S, N_HEADS_MLA * HEAD_PAD), BF16),
                   jax.ShapeDtypeStruct((N_HEADS_MLA, S, LANE), F32)),
        grid=(N_HEADS_MLA, nq),
        in_specs=[pl.BlockSpec((tile, HEAD_PAD), lambda h, i: (i, h)),
                  pl.BlockSpec((tile, LANE), lambda h, i: (i, 0)),
                  pl.BlockSpec((S, HEAD_PAD), lambda h, i: (0, h)),
                  pl.BlockSpec((S, V_DIM), lambda h, i: (0, h)),
                  pl.BlockSpec((1, nq, 1, tile), lambda h, i: (h, 0, 0, 0)),
                  pl.BlockSpec((1, nq, 1, tile), lambda h, i: (h, 0, 0, 0))],
        out_specs=(pl.BlockSpec((S, HEAD_PAD), lambda h, i: (0, h)),
                   pl.BlockSpec((tile, HEAD_PAD), lambda h, i: (i, h)),
                   pl.BlockSpec((1, tile, LANE), lambda h, i: (h, i, 0))),
        scratch_shapes=[pltpu.VMEM((tile, HEAD_PAD), F32), pltpu.VMEM((tile, V_DIM), F32)],
        compiler_params=_params(("parallel", "arbitrary")),
    )(kv, kp, q, do, lse, delta)


def _shift_down(cur, halo, k):
    sh = pltpu.roll(cur, k, 0)
    hs = pltpu.roll(halo, k, 0)
    rows = lax.broadcasted_iota(jnp.int32, hs.shape, 0)
    first = jnp.where(rows < k, hs, sh[0:SUBLANE])
    if cur.shape[0] == SUBLANE:
        return first
    return jnp.concatenate([first, sh[SUBLANE:]], axis=0)


def _shift_up(cur, nxt, k):
    n = cur.shape[0]
    sh = pltpu.roll(cur, n - k, 0)
    ns = pltpu.roll(nxt, SUBLANE - k, 0)
    rows = lax.broadcasted_iota(jnp.int32, ns.shape, 0)
    last = jnp.where(rows >= SUBLANE - k, ns, sh[n - SUBLANE:])
    if n == SUBLANE:
        return last
    return jnp.concatenate([sh[:n - SUBLANE], last], axis=0)


def _conv_pre(cur, halo, w, b):
    out = b + w[3:4] * cur
    for k in range(1, CONV_WIDTH):
        out = out + w[3 - k:4 - k] * _shift_down(cur, halo, k)
    return out


def _conv_fwd(xbc, w, b):
    S = xbc.shape[0]

    def fn(i, n, cur, prev, w, b):
        halo = jnp.where(i > 0, prev, 0.0)
        pre = _conv_pre(cur, halo, w, b)
        return (pre * _sigmoid(pre),)

    return _rows(fn, S, 256, [_row(xbc), _prev8(xbc), _full(w), _full(b)], [("row", xbc.shape[1], F32)], "conv_fwd")[0]


def _conv_bwd(xbc, dact, w, b):
    S, C = xbc.shape

    def dsilu(pre):
        s = _sigmoid(pre)
        return s * (1.0 + pre * (1.0 - s))

    def fn(i, n, cur, prev, nxt, dcur, dnxt, w, b):
        halo = jnp.where(i > 0, prev, 0.0)
        pre = _conv_pre(cur, halo, w, b)
        dpre = dcur * dsilu(pre)
        pre_n = _conv_pre(nxt, cur[cur.shape[0] - SUBLANE:], w, b)
        dpre_n = jnp.where(i < n - 1, dnxt * dsilu(pre_n), 0.0)
        dx = w[3:4] * dpre
        rows = lax.broadcasted_iota(jnp.int32, (SUBLANE, C), 0)
        dw = jnp.where(rows == 3, _colsum(dpre * cur), 0.0)
        for k in range(1, CONV_WIDTH):
            dx = dx + w[3 - k:4 - k] * _shift_up(dpre, dpre_n, k)
            dw = dw + jnp.where(rows == 3 - k, _colsum(dpre * _shift_down(cur, halo, k)), 0.0)
        return dx, dw, _colsum(dpre)

    return _rows(fn, S, 256, [_row(xbc), _prev8(xbc), _next8(xbc), _row(dact), _next8(dact), _full(w), _full(b)],
                 [("row", C, BF16), ("acc", (SUBLANE, C), F32), ("acc", (1, C), F32)], "conv_bwd")


def _softplus(x):
    return jnp.maximum(x, 0.0) + jnp.log1p(jnp.exp(-jnp.abs(x)))


def _cumsum_rows(x):
    rows = lax.broadcasted_iota(jnp.int32, x.shape, 0)
    s = 1
    while s < x.shape[0]:
        x = x + jnp.where(rows >= s, pltpu.roll(x, s, 0), 0.0)
        s *= 2
    return x


def _revcumsum_rows(x):
    n = x.shape[0]
    rows = lax.broadcasted_iota(jnp.int32, x.shape, 0)
    s = 1
    while s < n:
        x = x + jnp.where(rows < n - s, pltpu.roll(x, n - s, 0), 0.0)
        s *= 2
    return x


def _dt_prep(dt_raw, dt_bias, a_log):
    S = dt_raw.shape[0]

    def body(raw_ref, bias_ref, alog_ref, dt_ref, cum_ref, cumt_ref):
        dt = _softplus(raw_ref[...] + bias_ref[...])
        cum = _cumsum_rows(dt * (-jnp.exp(alog_ref[...])))
        dt_ref[...] = dt
        cum_ref[...] = cum
        cumt_ref[...] = cum.T

    return pl.pallas_call(
        body, name="dt_prep",
        out_shape=(jax.ShapeDtypeStruct((S, LANE), F32), jax.ShapeDtypeStruct((S, LANE), F32),
                   jax.ShapeDtypeStruct((LANE, S), F32)),
        grid=(S // CHUNK,),
        in_specs=[pl.BlockSpec((CHUNK, LANE), lambda i: (i, 0)), pl.BlockSpec((1, LANE), lambda i: (0, 0)),
                  pl.BlockSpec((1, LANE), lambda i: (0, 0))],
        out_specs=(pl.BlockSpec((CHUNK, LANE), lambda i: (i, 0)), pl.BlockSpec((CHUNK, LANE), lambda i: (i, 0)),
                   pl.BlockSpec((LANE, CHUNK), lambda i: (0, i))),
        compiler_params=_params(("parallel",)),
    )(dt_raw, dt_bias, a_log)


def _group_cols(t):
    S = t.shape[0]
    return jnp.transpose(t[:, :N_HEADS_SSM].reshape(S, SSM_GROUPS, HEADS_PER_GROUP), (1, 0, 2))


def _ungroup_cols(t):
    S = t.shape[1]
    flat = jnp.transpose(t, (1, 0, 2)).reshape(S, N_HEADS_SSM)
    return jnp.pad(flat, ((0, 0), (0, LANE - N_HEADS_SSM)))


_NT = (((1,), (1,)), ((), ()))
_TN = (((0,), (0,)), ((), ()))
P = SSM_HEADDIM
GW = HEADS_PER_GROUP * SSM_HEADDIM


def _decay(cc, cr):
    L = cc.shape[0]
    i = lax.broadcasted_iota(jnp.int32, (L, L), 0)
    j = lax.broadcasted_iota(jnp.int32, (L, L), 1)
    return jnp.exp(jnp.where(i >= j, cc - cr, NEG))


def _ssd_fwd(xbc_c, dt_g, cum_g, cumt_g):
    S = xbc_c.shape[0]
    nc = S // CHUNK
    L = CHUNK
    boff = D_INNER // D_STATE

    def body(x_ref, b_ref, c_ref, dt_ref, cum_ref, cumt_ref, y_ref, st_ref, state):
        c = pl.program_id(1)

        @pl.when(c == 0)
        def _():
            state[...] = jnp.zeros_like(state)

        bm = b_ref[...].astype(BF16)
        cm = c_ref[...].astype(BF16)
        cb = lax.dot_general(cm, bm, _NT, preferred_element_type=F32)
        for r in range(HEADS_PER_GROUP):
            cc = cum_ref[0, :, r:r + 1]
            cr = cumt_ref[0, r:r + 1, :]
            m = (cb * _decay(cc, cr)).astype(BF16)
            xdt = x_ref[:, r * P:(r + 1) * P] * dt_ref[0, :, r:r + 1]
            st = state[r * P:(r + 1) * P, :]
            st_ref[0, 0, r * P:(r + 1) * P, :] = st
            y = jnp.dot(m, xdt.astype(BF16), preferred_element_type=F32)
            y = y + lax.dot_general(cm, st.astype(BF16), _NT, preferred_element_type=F32) * jnp.exp(cc)
            y_ref[:, r * P:(r + 1) * P] = y
            cl = cum_ref[0, L - 1:L, r:r + 1]
            wend = jnp.exp(cl - cc)
            state[r * P:(r + 1) * P, :] = st * jnp.exp(cl) + lax.dot_general(
                (xdt * wend).astype(BF16), bm, _TN, preferred_element_type=F32)

    return pl.pallas_call(
        body, name="ssd_fwd",
        out_shape=(jax.ShapeDtypeStruct((S, D_INNER), F32), jax.ShapeDtypeStruct((SSM_GROUPS, nc, GW, D_STATE), F32)),
        grid=(SSM_GROUPS, nc),
        in_specs=[pl.BlockSpec((L, GW), lambda g, c: (c, g)),
                  pl.BlockSpec((L, D_STATE), lambda g, c: (c, boff + g)),
                  pl.BlockSpec((L, D_STATE), lambda g, c: (c, boff + SSM_GROUPS + g)),
                  pl.BlockSpec((1, L, HEADS_PER_GROUP), lambda g, c: (g, c, 0)),
                  pl.BlockSpec((1, L, HEADS_PER_GROUP), lambda g, c: (g, c, 0)),
                  pl.BlockSpec((1, HEADS_PER_GROUP, L), lambda g, c: (g, 0, c))],
        out_specs=(pl.BlockSpec((L, GW), lambda g, c: (c, g)),
                   pl.BlockSpec((1, 1, GW, D_STATE), lambda g, c: (g, c, 0, 0))),
        scratch_shapes=[pltpu.VMEM((GW, D_STATE), F32)],
        compiler_params=_params(("parallel", "arbitrary")),
    )(xbc_c, xbc_c, xbc_c, dt_g, cum_g, cumt_g)


def _ssd_bwd(xbc_c, dt_g, cum_g, cumt_g, states, dy, dx_skip):
    S = xbc_c.shape[0]
    nc = S // CHUNK
    L = CHUNK
    boff = D_INNER // D_STATE
    rev = lambda c: nc - 1 - c

    def body(x_ref, b_ref, c_ref, dt_ref, cum_ref, cumt_ref, st_ref, dy_ref, skip_ref,
             dx_ref, db_ref, dc_ref, ddt_ref, dcum_ref, dstate):
        c = pl.program_id(1)

        @pl.when(c == 0)
        def _():
            dstate[...] = jnp.zeros_like(dstate)

        bf = b_ref[...]
        bm = bf.astype(BF16)
        cm = c_ref[...].astype(BF16)
        cb = lax.dot_general(cm, bm, _NT, preferred_element_type=F32)
        dcb = jnp.zeros((L, L), F32)
        dbs = jnp.zeros((L, D_STATE), F32)
        dcs = jnp.zeros((L, D_STATE), F32)
        rowid = lax.broadcasted_iota(jnp.int32, (L, 1), 0)
        for r in range(HEADS_PER_GROUP):
            sl = slice(r * P, (r + 1) * P)
            cc = cum_ref[0, :, r:r + 1]
            cr = cumt_ref[0, r:r + 1, :]
            dtc = dt_ref[0, :, r:r + 1]
            decay = _decay(cc, cr)
            m = cb * decay
            xr = x_ref[:, sl]
            xdt = xr * dtc
            xdb = xdt.astype(BF16)
            dyr = dy_ref[:, sl]
            dyb = dyr.astype(BF16)
            st = st_ref[0, 0, sl, :]
            stb = st.astype(BF16)
            ds = dstate[sl, :]
            dsb = ds.astype(BF16)
            ecc = jnp.exp(cc)
            cl = cum_ref[0, L - 1:L, r:r + 1]
            ecl = jnp.exp(cl)
            wend = jnp.exp(cl - cc)

            g = lax.dot_general(dyb, xdb, _NT, preferred_element_type=F32)
            q = g * m
            dcb = dcb + g * decay
            dcum = jnp.sum(q, axis=1, keepdims=True) - _row_of_t(jnp.sum(q, axis=0, keepdims=True), L)
            dxd = lax.dot_general(m.astype(BF16), dyb, _TN, preferred_element_type=F32)
            dxd = dxd + lax.dot_general(bm, dsb, _NT, preferred_element_type=F32) * wend
            yoff = lax.dot_general(cm, stb, _NT, preferred_element_type=F32) * ecc
            dcum = dcum + jnp.sum(dyr * yoff, axis=1, keepdims=True)
            dcs = dcs + jnp.dot(dyb, stb, preferred_element_type=F32) * ecc
            t = jnp.dot(xdb, dsb, preferred_element_type=F32)
            dbs = dbs + t * wend
            vj = jnp.sum(t * bf, axis=1, keepdims=True) * wend
            dcum = dcum - vj
            dlast = jnp.sum(vj) + ecl * jnp.sum(ds * st)
            dcum = dcum + jnp.where(rowid == L - 1, dlast, 0.0)
            dstate[sl, :] = ecl * ds + lax.dot_general((dyr * ecc).astype(BF16), cm, _TN, preferred_element_type=F32)

            dx_ref[:, sl] = dxd * dtc + skip_ref[:, sl]
            ddt_ref[0, :, r:r + 1] = jnp.sum(dxd * xr, axis=1, keepdims=True)
            dcum_ref[0, :, r:r + 1] = dcum
        dcbb = dcb.astype(BF16)
        dc_ref[...] = dcs + jnp.dot(dcbb, bm, preferred_element_type=F32)
        db_ref[...] = dbs + lax.dot_general(dcbb, cm, _TN, preferred_element_type=F32)

    return pl.pallas_call(
        body, name="ssd_bwd",
        out_shape=(jax.ShapeDtypeStruct((S, D_INNER), F32),
                   jax.ShapeDtypeStruct((S, SSM_GROUPS * D_STATE), F32),
                   jax.ShapeDtypeStruct((S, SSM_GROUPS * D_STATE), F32),
                   jax.ShapeDtypeStruct((SSM_GROUPS, S, HEADS_PER_GROUP), F32),
                   jax.ShapeDtypeStruct((SSM_GROUPS, S, HEADS_PER_GROUP), F32)),
        grid=(SSM_GROUPS, nc),
        in_specs=[pl.BlockSpec((L, GW), lambda g, c: (rev(c), g)),
                  pl.BlockSpec((L, D_STATE), lambda g, c: (rev(c), boff + g)),
                  pl.BlockSpec((L, D_STATE), lambda g, c: (rev(c), boff + SSM_GROUPS + g)),
                  pl.BlockSpec((1, L, HEADS_PER_GROUP), lambda g, c: (g, rev(c), 0)),
                  pl.BlockSpec((1, L, HEADS_PER_GROUP), lambda g, c: (g, rev(c), 0)),
                  pl.BlockSpec((1, HEADS_PER_GROUP, L), lambda g, c: (g, 0, rev(c))),
                  pl.BlockSpec((1, 1, GW, D_STATE), lambda g, c: (g, rev(c), 0, 0)),
                  pl.BlockSpec((L, GW), lambda g, c: (rev(c), g)),
                  pl.BlockSpec((L, GW), lambda g, c: (rev(c), g))],
        out_specs=(pl.BlockSpec((L, GW), lambda g, c: (rev(c), g)),
                   pl.BlockSpec((L, D_STATE), lambda g, c: (rev(c), g)),
                   pl.BlockSpec((L, D_STATE), lambda g, c: (rev(c), g)),
                   pl.BlockSpec((1, L, HEADS_PER_GROUP), lambda g, c: (g, rev(c), 0)),
                   pl.BlockSpec((1, L, HEADS_PER_GROUP), lambda g, c: (g, rev(c), 0))),
        scratch_shapes=[pltpu.VMEM((GW, D_STATE), F32)],
        compiler_params=_params(("parallel", "arbitrary")),
    )(xbc_c, xbc_c, xbc_c, dt_g, cum_g, cumt_g, states, dy, dx_skip)


def _row_of_t(row, n):
    eye = lax.broadcasted_iota(jnp.int32, (n, n), 0) == lax.broadcasted_iota(jnp.int32, (n, n), 1)
    return jnp.sum(jnp.where(eye, row, 0.0), axis=1, keepdims=True)


def _dt_bwd(dt_raw, dt_bias, a_log, ddt_x, dcum):
    S = dt_raw.shape[0]

    def fn(i, n, raw, ddx, dcu, bias, alog):
        xx = raw + bias
        dt = _softplus(xx)
        a = -jnp.exp(alog)
        dda = _revcumsum_rows(dcu)
        ddt = ddx + dda * a
        lane = lax.broadcasted_iota(jnp.int32, raw.shape, 1)
        draw = jnp.where(lane < N_HEADS_SSM, ddt * _sigmoid(xx), 0.0)
        return draw, _colsum(draw), _colsum(dda * dt) * a

    return _rows(fn, S, CHUNK, [_row(dt_raw), _row(ddt_x), _row(dcum), _full(dt_bias), _full(a_log)],
                 [("row", LANE, BF16), ("acc", (1, LANE), F32), ("acc", (1, LANE), F32)], "dt_bwd")


def _adamw(w, g, m, v, name):
    shape = w.shape
    cols = shape[-1]
    rows = int(np.prod(shape[:-1]))
    w2, g2, m2, v2 = (t.reshape(rows, cols) for t in (w, g, m, v))
    tr = rows if rows * cols <= 512 * 1024 else _tile(rows, max(SUBLANE, (512 * 1024 // cols) // SUBLANE * SUBLANE), SUBLANE)
    c1 = 1.0 - ADAM_B1 ** ADAM_STEP
    c2 = 1.0 - ADAM_B2 ** ADAM_STEP

    def body(w_ref, g_ref, m_ref, v_ref, d_ref, mo_ref, vo_ref):
        gv = g_ref[...]
        mn = ADAM_B1 * m_ref[...] + (1.0 - ADAM_B1) * gv
        vn = ADAM_B2 * v_ref[...] + (1.0 - ADAM_B2) * (gv * gv)
        d_ref[...] = -ADAM_LR * ((mn / c1) / (jnp.sqrt(vn / c2) + ADAM_EPS) + ADAM_WD * w_ref[...])
        mo_ref[...] = mn
        vo_ref[...] = vn

    spec = pl.BlockSpec((tr, cols), lambda i: (i, 0))
    outs = pl.pallas_call(
        body, name=name,
        out_shape=tuple(jax.ShapeDtypeStruct((rows, cols), F32) for _ in range(3)),
        grid=(rows // tr,),
        in_specs=[spec] * 4, out_specs=(spec,) * 3,
        compiler_params=_params(("parallel",)),
    )(w2, g2, m2, v2)
    return tuple(o.reshape(shape) for o in outs)


def _prep_weights(w_in, w_uq):
    offs = np.cumsum((0,) + IN_SPLITS)
    pad = lambda t: jnp.pad(t, ((0, 0), (0, LANE - t.shape[1])))
    pieces = dict(
        qkv=w_in[:, offs[0]:offs[2]],
        kr=pad(w_in[:, offs[2]:offs[3]]),
        z=w_in[:, offs[3]:offs[4]],
        xbc=w_in[:, offs[4]:offs[5]],
        dt=pad(w_in[:, offs[5]:offs[6]]),
        g=w_in[:, offs[6]:offs[8]],
    )
    uq = w_uq.reshape(Q_LORA, N_HEADS_MLA, QK_DIM)
    uq = jnp.pad(uq, ((0, 0), (0, 0), (0, HEAD_PAD - QK_DIM))).reshape(Q_LORA, N_HEADS_MLA * HEAD_PAD)
    return pieces, uq


def _local_step(x, p, positions, W, sp, target):
    S = x.shape[0]
    tile = min(ATTN_TILE, S)
    pos_col = positions.reshape(S, 1)
    invf = ROPE_THETA ** (-jnp.arange(0, QK_ROPE, 2, dtype=F32) / QK_ROPE)
    invf = jnp.pad(jnp.concatenate([invf, invf]), (0, LANE - QK_ROPE)).reshape(1, LANE)
    wp, w_uq_p = _prep_weights(W["w_in"], W["w_uq"])
    padl = lambda t: jnp.pad(t, ((0, 0), (0, LANE - t.shape[1])))
    dt_bias_p, a_log_p = padl(sp["dt_bias"]), padl(sp["a_log"])
    dskip_ch = jnp.repeat(sp["d_skip"], SSM_HEADDIM, axis=1)
    p_bf = p.astype(BF16)
    RW = 256

    (u_bf,) = _rows(lambda i, n, x, g: (x * _rstd(x) * g,), S, RW, [_row(x), _full(sp["mix_norm_pre"])],
                    [("row", D_MODEL, BF16)], "norm_pre")
    cqkv = _mm(u_bf, wp["qkv"], name="mm_qkv")
    z = _mm(u_bf, wp["z"], name="mm_z")
    xbc = _mm(u_bf, wp["xbc"], name="mm_xbc")
    gates = _mm(u_bf, wp["g"], name="mm_gates")
    kr_pad = _mm(u_bf, wp["kr"], name="mm_kr")
    dt_raw = _mm(u_bf, wp["dt"], name="mm_dt")

    def qkv_norm(i, n, cq, ckv, gq, gkv):
        return cq * _rstd(cq) * gq, ckv * _rstd(ckv) * gkv

    cqn, ckvn = _rows(qkv_norm, S, 512, [_row(cqkv, Q_LORA, 0), _row(cqkv, KV_LORA, 1), _full(sp["q_norm"]), _full(sp["kv_norm"])],
                      [("row", Q_LORA, BF16), ("row", KV_LORA, BF16)], "qkv_norm")
    q_raw = _mm(cqn, w_uq_p, name="mm_uq")
    kv = _mm(ckvn, W["w_ukv"], out_dtype=BF16, name="mm_ukv")
    q_bf, kp_bf = _rope_fwd(q_raw, kr_pad, pos_col, invf)
    attn, lse = _attn_fwd(q_bf, kv, kp_bf, tile)

    xbc_c = _conv_fwd(xbc, sp["conv_w"], sp["conv_b"])
    dt, cum, cumt = _dt_prep(dt_raw, dt_bias_p, a_log_p)
    dt_g, cum_g = _group_cols(dt), _group_cols(cum)
    cumt_g = cumt[:N_HEADS_SSM].reshape(SSM_GROUPS, HEADS_PER_GROUP, S)
    y, states = _ssd_fwd(xbc_c, dt_g, cum_g, cumt_g)

    GN = D_INNER // SSM_GROUPS

    def gated(y, xs, z, dsk):
        yt = y + dsk * xs
        sz = _sigmoid(z)
        return yt, sz, yt * (z * sz)

    def gated_norm(i, n, y, xs, z, dsk, gn):
        _, _, yg = gated(y, xs, z, dsk)
        parts = []
        for g in range(SSM_GROUPS):
            blk = yg[:, g * GN:(g + 1) * GN]
            parts.append(blk * _rstd(blk) * gn[:, g * GN:(g + 1) * GN])
        return (jnp.concatenate(parts, axis=1),)

    (ssm,) = _rows(gated_norm, S, 128, [_row(y), _row(xbc_c, D_INNER, 0), _row(z), _full(dskip_ch), _full(sp["ssm_norm"])],
                   [("row", D_INNER, BF16)], "gated_norm")

    a_o = _mm(attn, W["w_attn_o"], name="mm_attn_o")
    b_o = _mm(ssm, W["w_ssm_o"], name="mm_ssm_o")

    def mix(i, n, ga, gs, a, b):
        return (_sigmoid(ga) * a + _sigmoid(gs) * b,)

    (mixed,) = _rows(mix, S, RW, [_row(gates, D_MODEL, 0), _row(gates, D_MODEL, 1), _row(a_o), _row(b_o)],
                     [("row", D_MODEL, BF16)], "mix")
    m2 = _mm(mixed, W["w_out"], name="mm_out")

    def post(i, n, h, m, gpost, gpre):
        hn = h + m * _rstd(m) * gpost
        return hn, hn * _rstd(hn) * gpre

    h1, f_bf = _rows(post, S, RW, [_row(x), _row(m2), _full(sp["mix_norm_post"]), _full(sp["ffn_norm_pre"])],
                     [("row", D_MODEL, F32), ("row", D_MODEL, BF16)], "post_mix")
    ga = _mm(f_bf, W["w_gate"], name="mm_gate")
    up = _mm(f_bf, W["w_up"], name="mm_up")
    (s_bf,) = _rows(lambda i, n, a, b: (a * _sigmoid(a) * b,), S, RW, [_row(ga), _row(up)], [("row", D_FF, BF16)], "swiglu")
    f2 = _mm(s_bf, W["w_down"], name="mm_down")
    h2, n3_bf = _rows(post, S, RW, [_row(h1), _row(f2), _full(sp["ffn_norm_post"]), _full(sp["ple_norm_pre"])],
                      [("row", D_MODEL, F32), ("row", D_MODEL, BF16)], "post_ffn")
    gpre = _mm(n3_bf, W["w_ple_gate"], name="mm_ple_gate")
    pe = _mm(p_bf, W["w_ple"], name="mm_ple")

    def ple_loss(i, n, h2, gpre, pe, tgt, gpost):
        gate = _sigmoid(gpre)
        e = pe * gate
        r = _rstd(e)
        diff = h2 + e * r * gpost - tgt
        loss = 0.5 * jnp.sum(jnp.mean(diff * diff, axis=1, keepdims=True))
        dh3 = diff * (1.0 / D_MODEL)
        de, dg_rows = _norm_bwd(e, r, gpost, dh3)
        return (jnp.full((1, LANE), loss, F32), dh3, de * gate, de * pe * gate * (1.0 - gate), _colsum(dg_rows))

    loss, dh3, dpe, dgpre, g_ple_post = _rows(
        ple_loss, S, 128, [_row(h2), _row(gpre), _row(pe), _row(target), _full(sp["ple_norm_post"])],
        [("acc", (1, LANE), F32), ("row", D_MODEL, F32), ("row", D_MODEL, BF16), ("row", D_MODEL, BF16),
         ("acc", (1, D_MODEL), F32)], "ple_loss")

    gw = {}
    gs = {"ple_norm_post": g_ple_post}
    gw["w_ple"] = _mm(p_bf, dpe, ta=True, name="mmg_ple")
    gw["w_ple_gate"] = _mm(n3_bf, dgpre, ta=True, name="mmg_ple_gate")
    dn3 = _mm(dgpre, W["w_ple_gate"], tb=True, name="mmb_ple_gate")

    def post_bwd(i, n, h, m, dhn, dn, gpost, gpre):
        rm = _rstd(m)
        hn = h + m * rm * gpost
        dx, dgpre_rows = _norm_bwd(hn, _rstd(hn), gpre, dn)
        dhn_t = dhn + dx
        dm, dgpost_rows = _norm_bwd(m, rm, gpost, dhn_t)
        return dhn_t, dm, _colsum(dgpre_rows), _colsum(dgpost_rows)

    def run_post_bwd(h, m, dhn, dn, gpost, gpre, name):
        return _rows(post_bwd, S, 128, [_row(h), _row(m), _row(dhn), _row(dn), _full(gpost), _full(gpre)],
                     [("row", D_MODEL, F32), ("row", D_MODEL, BF16), ("acc", (1, D_MODEL), F32), ("acc", (1, D_MODEL), F32)], name)

    dh2, df2, gs["ple_norm_pre"], gs["ffn_norm_post"] = run_post_bwd(
        h1, f2, dh3, dn3, sp["ffn_norm_post"], sp["ple_norm_pre"], "post_ffn_bwd")
    gw["w_down"] = _mm(s_bf, df2, ta=True, name="mmg_down")
    ds = _mm(df2, W["w_down"], tb=True, name="mmb_down")

    def swiglu_bwd(i, n, a, b, ds):
        sa = _sigmoid(a)
        return ds * b * (sa * (1.0 + a * (1.0 - sa))), ds * (a * sa)

    dga, dup = _rows(swiglu_bwd, S, RW, [_row(ga), _row(up), _row(ds)], [("row", D_FF, BF16), ("row", D_FF, BF16)], "swiglu_bwd")
    gw["w_gate"] = _mm(f_bf, dga, ta=True, name="mmg_gate")
    gw["w_up"] = _mm(f_bf, dup, ta=True, name="mmg_up")
    df = _mm(dga, W["w_gate"], tb=True, name="mmb_gate")
    df = _mm(dup, W["w_up"], tb=True, add=df, name="mmb_up")
    dh1, dm2, gs["ffn_norm_pre"], gs["mix_norm_post"] = run_post_bwd(
        x, m2, dh2, df, sp["mix_norm_post"], sp["ffn_norm_pre"], "post_mix_bwd")
    gw["w_out"] = _mm(mixed, dm2, ta=True, name="mmg_out")
    dmixed = _mm(dm2, W["w_out"], tb=True, name="mmb_out")

    def mix_bwd(i, n, ga, gs_, a, b, dm):
        sa, ss = _sigmoid(ga), _sigmoid(gs_)
        return dm * sa, dm * ss, jnp.concatenate([dm * a * sa * (1.0 - sa), dm * b * ss * (1.0 - ss)], axis=1)

    da_o, db_o, dgates = _rows(mix_bwd, S, RW, [_row(gates, D_MODEL, 0), _row(gates, D_MODEL, 1), _row(a_o), _row(b_o), _row(dmixed)],
                               [("row", D_MODEL, BF16), ("row", D_MODEL, BF16), ("row", 2 * D_MODEL, BF16)], "mix_bwd")
    gw["w_attn_o"] = _mm(attn, da_o, ta=True, name="mmg_attn_o")
    dattn = _mm(da_o, W["w_attn_o"], tb=True, name="mmb_attn_o")
    gw["w_ssm_o"] = _mm(ssm, db_o, ta=True, name="mmg_ssm_o")
    dssm = _mm(db_o, W["w_ssm_o"], tb=True, name="mmb_ssm_o")

    delta, dattn_bf = _attn_delta(attn, dattn, tile)
    dq, dkv, dkp = _attn_bwd(q_bf, kv, kp_bf, dattn_bf, lse, delta, tile)
    dq_raw, dkr = _rope_bwd(dq, dkp, pos_col, invf)
    g_uq_p = _mm(cqn, dq_raw, ta=True, name="mmg_uq")
    gw["w_uq"] = g_uq_p.reshape(Q_LORA, N_HEADS_MLA, HEAD_PAD)[:, :, :QK_DIM].reshape(Q_LORA, N_HEADS_MLA * QK_DIM)
    dcqn = _mm(dq_raw, w_uq_p, tb=True, name="mmb_uq")
    gw["w_ukv"] = _mm(ckvn, dkv, ta=True, name="mmg_ukv")
    dckvn = _mm(dkv, W["w_ukv"], tb=True, name="mmb_ukv")

    def qkv_norm_bwd(i, n, cq, ckv, dq_, dkv_, gq, gkv):
        dcq, gq_rows = _norm_bwd(cq, _rstd(cq), gq, dq_)
        dckv, gkv_rows = _norm_bwd(ckv, _rstd(ckv), gkv, dkv_)
        return jnp.concatenate([dcq, dckv], axis=1), _colsum(gq_rows), _colsum(gkv_rows)

    dcqkv, gs["q_norm"], gs["kv_norm"] = _rows(
        qkv_norm_bwd, S, 512, [_row(cqkv, Q_LORA, 0), _row(cqkv, KV_LORA, 1), _row(dcqn), _row(dckvn), _full(sp["q_norm"]), _full(sp["kv_norm"])],
        [("row", Q_LORA + KV_LORA, BF16), ("acc", (1, Q_LORA), F32), ("acc", (1, KV_LORA), F32)], "qkv_norm_bwd")

    def gated_norm_bwd(i, n, y, xs, z, dssm, dsk, gn):
        yt, sz, yg = gated(y, xs, z, dsk)
        dyg_parts, gn_parts = [], []
        for g in range(SSM_GROUPS):
            sl = slice(g * GN, (g + 1) * GN)
            blk = yg[:, sl]
            dblk, rows = _norm_bwd(blk, _rstd(blk), gn[:, sl], dssm[:, sl])
            dyg_parts.append(dblk)
            gn_parts.append(_colsum(rows))
        dyg = jnp.concatenate(dyg_parts, axis=1)
        dyt = dyg * (z * sz)
        dz = dyg * yt * (sz * (1.0 + z * (1.0 - sz)))
        return dyt, dz, dyt * dsk, jnp.concatenate(gn_parts, axis=1), _colsum(dyt * xs)

    dy, dz, dx_skip, gs["ssm_norm"], g_dskip_ch = _rows(
        gated_norm_bwd, S, 128, [_row(y), _row(xbc_c, D_INNER, 0), _row(z), _row(dssm), _full(dskip_ch), _full(sp["ssm_norm"])],
        [("row", D_INNER, F32), ("row", D_INNER, BF16), ("row", D_INNER, F32), ("acc", (1, D_INNER), F32), ("acc", (1, D_INNER), F32)],
        "gated_norm_bwd")
    gs["d_skip"] = jnp.sum(g_dskip_ch.reshape(N_HEADS_SSM, SSM_HEADDIM), axis=1).reshape(1, N_HEADS_SSM)
    dxs, dbm, dcm, ddt_x, dcum = _ssd_bwd(xbc_c, dt_g, cum_g, cumt_g, states, dy, dx_skip)
    ddt_raw, g_dtb, g_alog = _dt_bwd(dt_raw, dt_bias_p, a_log_p, _ungroup_cols(ddt_x), _ungroup_cols(dcum))
    gs["dt_bias"] = g_dtb[:, :N_HEADS_SSM]
    gs["a_log"] = g_alog[:, :N_HEADS_SSM]
    dxbc_c = jnp.concatenate([dxs, dbm, dcm], axis=1)
    dxbc, g_conv_w8, gs["conv_b"] = _conv_bwd(xbc, dxbc_c, sp["conv_w"], sp["conv_b"])
    gs["conv_w"] = g_conv_w8[:CONV_WIDTH]

    g_qkv = _mm(u_bf, dcqkv, ta=True, name="mmg_qkv")
    g_kr = _mm(u_bf, dkr, ta=True, name="mmg_kr")
    g_z = _mm(u_bf, dz, ta=True, name="mmg_z")
    g_xbc = _mm(u_bf, dxbc, ta=True, name="mmg_xbc")
    g_dt = _mm(u_bf, ddt_raw, ta=True, name="mmg_dt")
    g_g = _mm(u_bf, dgates, ta=True, name="mmg_gates")
    gw["w_in"] = jnp.concatenate([g_qkv, g_kr[:, :QK_ROPE], g_z, g_xbc, g_dt[:, :N_HEADS_SSM], g_g], axis=1)
    du = _mm(dcqkv, wp["qkv"], tb=True, name="mmb_qkv")
    du = _mm(dkr, wp["kr"], tb=True, add=du, name="mmb_kr")
    du = _mm(dz, wp["z"], tb=True, add=du, name="mmb_z")
    du = _mm(dxbc, wp["xbc"], tb=True, add=du, name="mmb_xbc")
    du = _mm(ddt_raw, wp["dt"], tb=True, add=du, name="mmb_dt")
    du = _mm(dgates, wp["g"], tb=True, add=du, name="mmb_gates")

    def pre_bwd(i, n, x, du, dh, g):
        dx, rows = _norm_bwd(x, _rstd(x), g, du)
        return dh + dx, _colsum(rows)

    grad_x, gs["mix_norm_pre"] = _rows(pre_bwd, S, RW, [_row(x), _row(du), _row(dh1), _full(sp["mix_norm_pre"])],
                                       [("row", D_MODEL, F32), ("acc", (1, D_MODEL), F32)], "norm_pre_bwd")
    return loss, grad_x, gw, gs


BIG = (
    ("w_in", (2048, 3872), 1), ("w_uq", (512, 768), 1), ("w_ukv", (512, 1024), 1), ("w_attn_o", (512, 2048), 0),
    ("w_ssm_o", (1024, 2048), 0), ("w_out", (512, 2048), 0), ("w_gate", (2048, 1408), 1), ("w_up", (2048, 1408), 1),
    ("w_down", (1408, 2048), 0), ("w_ple_gate", (512, 2048), 0), ("w_ple", (256, 512), 1),
)
SMALL = (
    ("mix_norm_pre", 2048), ("mix_norm_post", 2048), ("q_norm", 512), ("kv_norm", 512), ("conv_b", 6144), ("dt_bias", 64),
    ("a_log", 64), ("d_skip", 64), ("ssm_norm", 4096), ("ffn_norm_pre", 2048), ("ffn_norm_post", 2048),
    ("ple_norm_pre", 2048), ("ple_norm_post", 2048),
)
CONV_W_LEN = CONV_WIDTH * CONV_DIM
SMALL_ROWS = 384


def _place():
    return lax.axis_index("x"), lax.axis_index("y"), lax.axis_index("c")


def _flip(v, bit):
    return 1 - v if bit else v


def _start_then_wait(copies):
    for cp in copies:
        cp.start()
    for cp in copies:
        cp.wait()


def _hbm_call(body, name, ins, out_shapes, sems, aliases=None):
    return pl.pallas_call(
        body, name=name, out_shape=tuple(out_shapes),
        in_specs=[pl.BlockSpec(memory_space=pl.ANY)] * len(ins),
        out_specs=tuple(pl.BlockSpec(memory_space=pl.ANY) for _ in out_shapes),
        scratch_shapes=[pltpu.SemaphoreType.DMA((s,)) for s in sems],
        input_output_aliases=aliases or {},
    )(*ins)


def _gather_chips(shards):
    n = len(shards)

    def body(*refs):
        ins, outs = refs[:n], refs[n:2 * n]
        send_sems, recv_sems, fwd_send_sems, fwd_recv_sems = refs[2 * n:]
        x, y, c = _place()
        me = 2 * x + y
        far, near = [], []
        for a in range(n):
            half = shards[a].shape[0] // 2
            lo = pl.multiple_of(c * half, SUBLANE)
            for k in (1, 2, 3):
                px, py = _flip(x, k >> 1), _flip(y, k & 1)
                far.append(pltpu.make_async_remote_copy(
                    src_ref=ins[a].at[pl.ds(lo, half), :], dst_ref=outs[a].at[me, pl.ds(lo, half), :],
                    send_sem=send_sems.at[3 * a + k - 1], recv_sem=recv_sems.at[3 * a + k - 1],
                    device_id=(px, py, c), device_id_type=MESH_ID))
                got = outs[a].at[2 * px + py, pl.ds(lo, half), :]
                near.append(pltpu.make_async_remote_copy(
                    src_ref=got, dst_ref=got, send_sem=fwd_send_sems.at[3 * a + k - 1], recv_sem=fwd_recv_sems.at[3 * a + k - 1],
                    device_id=(x, y, 1 - c), device_id_type=MESH_ID))
        for cp in far:
            cp.start()
        for cp, fwd in zip(far, near):
            cp.wait_recv()
            fwd.start()
        for cp, fwd in zip(far, near):
            cp.wait_send()
            fwd.wait()

    return _hbm_call(body, "gather_chips", shards,
                     [jax.ShapeDtypeStruct((N_CHIPS, *s.shape), s.dtype) for s in shards], (3 * n,) * 4)


def _swap_halves(gs):
    n = len(gs)

    def body(*refs):
        ins, outs = refs[:n], refs[n:2 * n]
        send_sems, recv_sems = refs[2 * n:]
        x, y, c = _place()
        copies = []
        for a in range(n):
            half = gs[a].shape[1] // 2
            src = ins[a].at[:, pl.ds(pl.multiple_of((1 - c) * half, SUBLANE), half), :]
            copies.append(pltpu.make_async_remote_copy(
                src_ref=src, dst_ref=outs[a], send_sem=send_sems.at[a], recv_sem=recv_sems.at[a],
                device_id=(x, y, 1 - c), device_id_type=MESH_ID))
        _start_then_wait(copies)

    return _hbm_call(body, "swap_halves", gs,
                     [jax.ShapeDtypeStruct((g.shape[0], g.shape[1] // 2, g.shape[2]), g.dtype) for g in gs], (n, n))


def _sum_rows_tile(rows, cols):
    return _tile(rows, max(2 * SUBLANE, (512 * 1024 // cols) // (2 * SUBLANE) * (2 * SUBLANE)), 2 * SUBLANE)


def _add_half(g, other, c, name):
    n, R, C = g.shape
    half = R // 2
    tr = _sum_rows_tile(half, C)
    nb = half // tr

    def body(c_ref, g_ref, o_ref, out_ref):
        out_ref[...] = (g_ref[...] + o_ref[...]).astype(out_ref.dtype)

    return pl.pallas_call(
        body, name=name,
        out_shape=jax.ShapeDtypeStruct((n, half, C), BF16),
        grid_spec=pltpu.PrefetchScalarGridSpec(
            num_scalar_prefetch=1, grid=(n, nb),
            in_specs=[pl.BlockSpec((1, tr, C), lambda j, i, c_ref: (j, c_ref[0] * nb + i, 0)),
                      pl.BlockSpec((1, tr, C), lambda j, i, c_ref: (j, i, 0))],
            out_specs=pl.BlockSpec((1, tr, C), lambda j, i, c_ref: (j, i, 0))),
        compiler_params=_params(("parallel", "parallel")),
    )(c, g, other)


def _scatter_chips(parts):
    n = len(parts)

    def body(*refs):
        ins, outs = refs[:n], refs[n:2 * n]
        send_sems, recv_sems = refs[2 * n:]
        x, y, c = _place()
        copies = []
        for a in range(n):
            for k in (1, 2, 3):
                px, py = _flip(x, k >> 1), _flip(y, k & 1)
                copies.append(pltpu.make_async_remote_copy(
                    src_ref=ins[a].at[2 * px + py], dst_ref=outs[a].at[k - 1], send_sem=send_sems.at[3 * a + k - 1],
                    recv_sem=recv_sems.at[3 * a + k - 1], device_id=(px, py, c), device_id_type=MESH_ID))
        _start_then_wait(copies)

    return _hbm_call(body, "scatter_chips", parts,
                     [jax.ShapeDtypeStruct((3, *p.shape[1:]), p.dtype) for p in parts], (3 * n, 3 * n))


def _add_chips(part, got, place, name):
    n, R, C = part.shape
    tr = _sum_rows_tile(R, C)
    nb = R // tr

    def body(place_ref, p_ref, g_ref, out_ref):
        out_ref[...] = ((p_ref[0].astype(F32) + g_ref[0].astype(F32)) + g_ref[1].astype(F32)) + g_ref[2].astype(F32)

    return pl.pallas_call(
        body, name=name,
        out_shape=jax.ShapeDtypeStruct((2 * R, C), F32),
        grid_spec=pltpu.PrefetchScalarGridSpec(
            num_scalar_prefetch=1, grid=(nb,),
            in_specs=[pl.BlockSpec((1, tr, C), lambda i, place_ref: (place_ref[0], i, 0)),
                      pl.BlockSpec((3, tr, C), lambda i, place_ref: (0, i, 0))],
            out_specs=pl.BlockSpec((tr, C), lambda i, place_ref: (place_ref[1] * nb + i, 0))),
        compiler_params=_params(("parallel",)),
    )(place, part, got)


def _join_halves(wholes):
    n = len(wholes)

    def body(*refs):
        outs = refs[n:2 * n]
        send_sems, recv_sems = refs[2 * n:]
        x, y, c = _place()
        copies = []
        for a in range(n):
            half = wholes[a].shape[0] // 2
            rows = outs[a].at[pl.ds(pl.multiple_of(c * half, SUBLANE), half), :]
            copies.append(pltpu.make_async_remote_copy(
                src_ref=rows, dst_ref=rows, send_sem=send_sems.at[a], recv_sem=recv_sems.at[a],
                device_id=(x, y, 1 - c), device_id_type=MESH_ID))
        _start_then_wait(copies)

    return _hbm_call(body, "join_halves", wholes, [jax.ShapeDtypeStruct(w.shape, w.dtype) for w in wholes], (n, n),
                     aliases={a: a for a in range(n)})


def _allreduce_small(vec, name):
    R, C = vec.shape

    def body(v_ref, o_ref, buf, send_sems, recv_sems):
        x, y, c = _place()
        me = 4 * x + 2 * y + c
        buf[me] = v_ref[...]
        copies = []
        for k in range(1, N_DEV):
            peer = (_flip(x, (k >> 2) & 1), _flip(y, (k >> 1) & 1), _flip(c, k & 1))
            copies.append(pltpu.make_async_remote_copy(
                src_ref=v_ref, dst_ref=buf.at[me], send_sem=send_sems.at[k - 1], recv_sem=recv_sems.at[k - 1],
                device_id=peer, device_id_type=MESH_ID))
        for cp in copies:
            cp.start()
        for cp in copies:
            cp.wait()
        tot = buf[0]
        for d in range(1, N_DEV):
            tot = tot + buf[d]
        o_ref[...] = tot

    return pl.pallas_call(
        body, name=name,
        out_shape=jax.ShapeDtypeStruct((R, C), F32),
        in_specs=[pl.BlockSpec(memory_space=pltpu.VMEM)],
        out_specs=pl.BlockSpec(memory_space=pltpu.VMEM),
        scratch_shapes=[pltpu.VMEM((N_DEV, R, C), F32), pltpu.SemaphoreType.DMA((N_DEV - 1,)), pltpu.SemaphoreType.DMA((N_DEV - 1,))],
    )(vec)


def _unstack(gathered, shape, axis):
    if axis == 0:
        return gathered.reshape(N_CHIPS * shape[0], shape[1])
    return jnp.concatenate([gathered[j] for j in range(N_CHIPS)], axis=1)


def _stack(whole, shape, axis):
    if axis == 0:
        return whole.reshape(N_CHIPS, shape[0], shape[1])
    return jnp.stack([whole[:, j * shape[1]:(j + 1) * shape[1]] for j in range(N_CHIPS)])


def kernel(x, p, positions, mix_norm_pre, mix_norm_post, w_in, q_norm, w_uq, kv_norm, w_ukv, conv_w, conv_b, dt_bias, a_log, d_skip, ssm_norm, w_attn_o, w_ssm_o, w_out, ffn_norm_pre, ffn_norm_post, w_gate, w_up, w_down, ple_norm_pre, ple_norm_post, w_ple_gate, w_ple, loss_target, m_mix_norm_pre, m_mix_norm_post, m_w_in, m_q_norm, m_w_uq, m_kv_norm, m_w_ukv, m_conv_w, m_conv_b, m_dt_bias, m_a_log, m_d_skip, m_ssm_norm, m_w_attn_o, m_w_ssm_o, m_w_out, m_ffn_norm_pre, m_ffn_norm_post, m_w_gate, m_w_up, m_w_down, m_ple_norm_pre, m_ple_norm_post, m_w_ple_gate, m_w_ple, v_mix_norm_pre, v_mix_norm_post, v_w_in, v_q_norm, v_w_uq, v_kv_norm, v_w_ukv, v_conv_w, v_conv_b, v_dt_bias, v_a_log, v_d_skip, v_ssm_norm, v_w_attn_o, v_w_ssm_o, v_w_out, v_ffn_norm_pre, v_ffn_norm_post, v_w_gate, v_w_up, v_w_down, v_ple_norm_pre, v_ple_norm_post, v_w_ple_gate, v_w_ple):
    given = dict(locals())
    names = [n for n, _, _ in BIG] + [n for n, _ in SMALL] + ["conv_w"]
    order = ["mix_norm_pre", "mix_norm_post", "w_in", "q_norm", "w_uq", "kv_norm", "w_ukv", "conv_w", "conv_b", "dt_bias", "a_log",
             "d_skip", "ssm_norm", "w_attn_o", "w_ssm_o", "w_out", "ffn_norm_pre", "ffn_norm_post", "w_gate", "w_up", "w_down",
             "ple_norm_pre", "ple_norm_post", "w_ple_gate", "w_ple"]
    assert sorted(names) == sorted(order)
    cx, cy, cc = _place()
    chip = 2 * cx + cy
    conv_cols = CONV_DIM // N_CHIPS

    shards = [given[n][0].astype(BF16) for n, _, _ in BIG]
    gathered = [lax.dynamic_update_slice(g, s[None], (chip, 0, 0)) for g, s in zip(_gather_chips(shards), shards)]
    W = {n: _unstack(g, shape, axis) for (n, shape, axis), g in zip(BIG, gathered)}
    own = jnp.where(cc == 0, conv_w[0], 0.0)
    conv_vec = lax.dynamic_update_slice(jnp.zeros((CONV_WIDTH, CONV_DIM), F32), own, (0, chip * conv_cols))
    conv_full = _allreduce_small(conv_vec.reshape(CONV_W_LEN // LANE, LANE), "gather_conv_w").reshape(CONV_WIDTH, CONV_DIM)
    sp = {n: given[n] for n, _ in SMALL}
    sp["conv_w"] = conv_full

    loss_part, grad_x, gw, gs = _local_step(x[0], p[0, 0], positions[0], W, sp, loss_target[0])

    stacked = [_stack(gw[n], shape, axis) for n, shape, axis in BIG]
    c_arr = cc.reshape(1).astype(jnp.int32)
    place_arr = jnp.stack([chip, cc]).astype(jnp.int32)
    parts = [_add_half(g, o, c_arr, "add_half_" + n) for (n, _, _), g, o in zip(BIG, stacked, _swap_halves(stacked))]
    wholes = [_add_chips(q, o, place_arr, "add_chips_" + n) for (n, _, _), q, o in zip(BIG, parts, _scatter_chips(parts))]
    g_big = {n: r.reshape(1, *shape) for (n, shape, _), r in zip(BIG, _join_halves(wholes))}

    small_parts = [gs[n] for n, _ in SMALL] + [gs["conv_w"], loss_part[:, :1]]
    small_vec = jnp.concatenate([t.reshape(-1) for t in small_parts])
    small_vec = jnp.pad(small_vec, (0, SMALL_ROWS * LANE - small_vec.shape[0])).reshape(SMALL_ROWS, LANE)
    small_sum = _allreduce_small(small_vec, "allreduce_small").reshape(-1)
    g_small, off = {}, 0
    for n, length in SMALL:
        g_small[n] = small_sum[off:off + length].reshape(1, length)
        off += length
    g_conv = small_sum[off:off + CONV_W_LEN].reshape(CONV_WIDTH, CONV_DIM)
    g_small["conv_w"] = lax.dynamic_slice(g_conv, (0, chip * conv_cols), (CONV_WIDTH, conv_cols)).reshape(1, CONV_WIDTH, conv_cols)
    loss = small_sum[off + CONV_W_LEN]

    grads, deltas, new_m, new_v = [], [], [], []
    for n in order:
        g = g_big[n] if n in g_big else g_small[n]
        d, m_, v_ = _adamw(given[n], g, given["m_" + n], given["v_" + n], "adamw_" + n)
        grads.append(g)
        deltas.append(d)
        new_m.append(m_)
        new_v.append(v_)
    return (loss, grad_x.reshape(x.shape), *grads, *deltas, *new_m, *new_v)
```

```python
import functools
import math

import numpy as np
import jax
import jax.numpy as jnp
from jax import lax
from jax.experimental import pallas as pl
from jax.experimental.pallas import tpu as pltpu

F32 = jnp.float32
BF16 = jnp.bfloat16

D_MODEL = 2048
N_HEADS_MLA = 16
Q_LORA = 512
KV_LORA = 512
QK_NOPE = 128
QK_ROPE = 64
V_DIM = 128
QK_DIM = QK_NOPE + QK_ROPE
ROPE_THETA = 10000.0
D_INNER = 4096
SSM_HEADDIM = 64
N_HEADS_SSM = 64
SSM_GROUPS = 8
HEADS_PER_GROUP = 8
D_STATE = 128
CONV_WIDTH = 4
CHUNK = 256
CONV_DIM = D_INNER + 2 * SSM_GROUPS * D_STATE
D_FF = 5632
PLE_DIM = 256
EPS = 1e-6
IN_SPLITS = (Q_LORA, KV_LORA, QK_ROPE, D_INNER, CONV_DIM, N_HEADS_SSM, D_MODEL, D_MODEL)

ADAM_LR = 0.001
ADAM_B1 = 0.9
ADAM_B2 = 0.999
ADAM_EPS = 1e-08
ADAM_WD = 0.01
ADAM_STEP = 10

LANE = 128
SUBLANE = 8
HEAD_PAD = 256
VMEM_LIMIT = 56 * 1024 * 1024
ATTN_TILE = 512
NEG = -1e30

MESH_ID = pl.DeviceIdType.MESH
N_CHIPS = 4
N_DEV = 8


def _tile(n, pref, mult=LANE):
    if n <= pref:
        return n
    t = (pref // mult) * mult
    while t >= mult:
        if n % t == 0:
            return t
        t -= mult
    return n


def _params(sem, vmem=VMEM_LIMIT):
    return pltpu.CompilerParams(dimension_semantics=sem, vmem_limit_bytes=vmem)


def _mm(a, b, *, ta=False, tb=False, add=None, out_dtype=F32, name, tm=1024, tn=1024, tk=2048):
    if ta:
        K, M = a.shape
    else:
        M, K = a.shape
    N = b.shape[0] if tb else b.shape[1]
    assert (b.shape[1] if tb else b.shape[0]) == K, (a.shape, b.shape, ta, tb)
    tm, tn, tk = _tile(M, tm), _tile(N, tn), _tile(K, tk)
    nk = K // tk
    dn = (((0 if ta else 1,), (1 if tb else 0,)), ((), ()))
    has_add = add is not None

    def body(*refs):
        if has_add:
            a_ref, b_ref, c_ref, o_ref = refs[:4]
        else:
            a_ref, b_ref, o_ref = refs[:3]
        prod = lax.dot_general(a_ref[...].astype(BF16), b_ref[...].astype(BF16), dn, preferred_element_type=F32)
        if nk == 1:
            o_ref[...] = ((c_ref[...] + prod) if has_add else prod).astype(out_dtype)
            return
        acc = refs[-1]
        k = pl.program_id(2)

        @pl.when(k == 0)
        def _():
            acc[...] = (c_ref[...] + prod) if has_add else prod

        @pl.when(k > 0)
        def _():
            acc[...] += prod

        @pl.when(k == nk - 1)
        def _():
            o_ref[...] = acc[...].astype(out_dtype)

    a_spec = pl.BlockSpec((tk, tm), lambda i, j, k: (k, i)) if ta else pl.BlockSpec((tm, tk), lambda i, j, k: (i, k))
    b_spec = pl.BlockSpec((tn, tk), lambda i, j, k: (j, k)) if tb else pl.BlockSpec((tk, tn), lambda i, j, k: (k, j))
    in_specs = [a_spec, b_spec]
    args = [a, b]
    if has_add:
        in_specs.append(pl.BlockSpec((tm, tn), lambda i, j, k: (i, j)))
        args.append(add)
    return pl.pallas_call(
        body, name=name,
        out_shape=jax.ShapeDtypeStruct((M, N), out_dtype),
        grid=(M // tm, N // tn, nk),
        in_specs=in_specs,
        out_specs=pl.BlockSpec((tm, tn), lambda i, j, k: (i, j)),
        scratch_shapes=[pltpu.VMEM((tm, tn), F32)] if nk > 1 else [],
        compiler_params=_params(("parallel", "parallel", "arbitrary")),
    )(*args)


def _row(arr, width=None, cblk=0):
    return ("row", arr, arr.shape[1] if width is None else width, cblk)


def _full(arr):
    return ("full", arr)


def _prev8(arr):
    return ("prev8", arr)


def _next8(arr):
    return ("next8", arr)


def _rows(fn, n_rows, tm, ins, outs, name):
    tm = min(tm, n_rows)
    assert n_rows % tm == 0 and tm % SUBLANE == 0
    n = n_rows // tm
    in_specs, args = [], []
    for spec in ins:
        kind, arr = spec[0], spec[1]
        if kind == "row":
            _, _, w, cb = spec
            in_specs.append(pl.BlockSpec((tm, w), lambda i, cb=cb: (i, cb)))
        elif kind == "full":
            in_specs.append(pl.BlockSpec(arr.shape, lambda i, nd=arr.ndim: (0,) * nd))
        elif kind == "prev8":
            in_specs.append(pl.BlockSpec((SUBLANE, arr.shape[1]),
                                         lambda i: (jnp.maximum(i * (tm // SUBLANE) - 1, 0), 0)))
        elif kind == "next8":
            last = n_rows // SUBLANE - 1
            in_specs.append(pl.BlockSpec((SUBLANE, arr.shape[1]),
                                         lambda i: (jnp.minimum((i + 1) * (tm // SUBLANE), last), 0)))
        else:
            raise ValueError(kind)
        args.append(arr)
    out_shapes, out_specs = [], []
    any_acc = False
    for spec in outs:
        if spec[0] == "row":
            _, w, dt = spec
            out_shapes.append(jax.ShapeDtypeStruct((n_rows, w), dt))
            out_specs.append(pl.BlockSpec((tm, w), lambda i: (i, 0)))
        else:
            _, shp, dt = spec
            any_acc = True
            out_shapes.append(jax.ShapeDtypeStruct(shp, dt))
            out_specs.append(pl.BlockSpec(shp, lambda i, nd=len(shp): (0,) * nd))
    nin = len(ins)

    def body(*refs):
        i = pl.program_id(0)
        vals = fn(i, n, *[r[...] for r in refs[:nin]])
        for o_ref, spec, v in zip(refs[nin:], outs, vals):
            if spec[0] == "acc":
                @pl.when(i == 0)
                def _(o_ref=o_ref):
                    o_ref[...] = jnp.zeros_like(o_ref)

                o_ref[...] += v.astype(o_ref.dtype)
            else:
                o_ref[...] = v.astype(o_ref.dtype)

    res = pl.pallas_call(
        body, name=name,
        out_shape=tuple(out_shapes),
        grid=(n,),
        in_specs=in_specs,
        out_specs=tuple(out_specs),
        compiler_params=_params(("arbitrary",) if any_acc else ("parallel",)),
    )(*args)
    return res


def _rstd(x):
    return lax.rsqrt(jnp.mean(x * x, axis=-1, keepdims=True) + EPS)


def _norm_bwd(x, r, g, dy):
    xh = x * r
    dyg = dy * g
    dx = r * (dyg - xh * jnp.mean(dyg * xh, axis=-1, keepdims=True))
    return dx, dy * xh


def _sigmoid(x):
    return 1.0 / (1.0 + jnp.exp(-x))


def _colsum(v):
    return jnp.sum(v, axis=0, keepdims=True)


def _rope_tables(pos, invf):
    ang = pos.astype(F32) * invf
    lane = lax.broadcasted_iota(jnp.int32, ang.shape, 1)
    cos, sin = jnp.cos(ang), jnp.sin(ang)
    ct = jnp.where(lane < QK_ROPE, cos, 0.0)
    sa = jnp.where(lane < QK_ROPE // 2, -sin, 0.0)
    sb = jnp.where((lane >= QK_ROPE // 2) & (lane < QK_ROPE), sin, 0.0)
    return ct, sa, sb


def _rope(b, ct, sa, sb):
    return ct * b + sa * pltpu.roll(b, LANE - QK_ROPE // 2, 1) + sb * pltpu.roll(b, QK_ROPE // 2, 1)


def _rope_t(d, ct, sa, sb):
    return ct * d + pltpu.roll(sa * d, QK_ROPE // 2, 1) + pltpu.roll(sb * d, LANE - QK_ROPE // 2, 1)


def _rope_fwd(q_raw, kr_pad, pos_col, invf):
    S = q_raw.shape[0]

    def fn(i, n, q, kr, pos, invf):
        ct, sa, sb = _rope_tables(pos, invf)
        parts = []
        for h in range(N_HEADS_MLA):
            parts.append(q[:, h * HEAD_PAD:h * HEAD_PAD + LANE])
            parts.append(_rope(q[:, h * HEAD_PAD + LANE:(h + 1) * HEAD_PAD], ct, sa, sb))
        return jnp.concatenate(parts, axis=1), _rope(kr, ct, sa, sb)

    return _rows(fn, S, 256, [_row(q_raw), _row(kr_pad), _row(pos_col), _full(invf)],
                 [("row", N_HEADS_MLA * HEAD_PAD, BF16), ("row", LANE, BF16)], "rope_fwd")


def _rope_bwd(dq, dkp, pos_col, invf):
    S = dq.shape[0]
    tm = 256

    def body(dq_ref, dkp_ref, pos_ref, invf_ref, dqo_ref, dkr_ref):
        ct, sa, sb = _rope_tables(pos_ref[...], invf_ref[...])
        for h in range(N_HEADS_MLA):
            dqo_ref[:, h * HEAD_PAD:h * HEAD_PAD + LANE] = dq_ref[:, h * HEAD_PAD:h * HEAD_PAD + LANE].astype(BF16)
            dqo_ref[:, h * HEAD_PAD + LANE:(h + 1) * HEAD_PAD] = _rope_t(
                dq_ref[:, h * HEAD_PAD + LANE:(h + 1) * HEAD_PAD], ct, sa, sb).astype(BF16)
        tot = dkp_ref[0]
        for h in range(1, N_HEADS_MLA):
            tot = tot + dkp_ref[h]
        dkr_ref[...] = _rope_t(tot, ct, sa, sb).astype(BF16)

    return pl.pallas_call(
        body, name="rope_bwd",
        out_shape=(jax.ShapeDtypeStruct(dq.shape, BF16), jax.ShapeDtypeStruct((S, LANE), BF16)),
        grid=(S // tm,),
        in_specs=[pl.BlockSpec((tm, dq.shape[1]), lambda i: (i, 0)),
                  pl.BlockSpec((N_HEADS_MLA, tm, LANE), lambda i: (0, i, 0)),
                  pl.BlockSpec((tm, 1), lambda i: (i, 0)),
                  pl.BlockSpec((1, LANE), lambda i: (0, 0))],
        out_specs=(pl.BlockSpec((tm, dq.shape[1]), lambda i: (i, 0)), pl.BlockSpec((tm, LANE), lambda i: (i, 0))),
        compiler_params=_params(("parallel",)),
    )(dq, dkp, pos_col, invf)


def _row_of(col, n):
    eye = lax.broadcasted_iota(jnp.int32, (n, n), 0) == lax.broadcasted_iota(jnp.int32, (n, n), 1)
    return jnp.sum(jnp.where(eye, col, 0.0), axis=0, keepdims=True)


def _attn_fwd(q, kv, kp, tile):
    S = q.shape[0]
    nq = S // tile
    scale = QK_DIM ** -0.5
    nt = (((1,), (1,)), ((), ()))

    def body(q_ref, kv_ref, kp_ref, o_ref, lse_ref, m_s, l_s, acc_s, s_buf):
        qi = pl.program_id(1)
        qv = q_ref[...]
        m_s[...] = jnp.full_like(m_s, NEG)
        l_s[...] = jnp.zeros_like(l_s)
        acc_s[...] = jnp.zeros_like(acc_s)

        def scores(j):
            start = pl.multiple_of(j * tile, tile)
            k = jnp.concatenate([kv_ref[pl.ds(start, tile), 0:LANE], kp_ref[pl.ds(start, tile), :]], axis=1)
            return lax.dot_general(qv, k, nt, preferred_element_type=F32) * scale

        def update(s, j):
            v = kv_ref[pl.ds(pl.multiple_of(j * tile, tile), tile), LANE:2 * LANE]
            m_old = m_s[...]
            m_new = jnp.maximum(m_old, jnp.max(s, axis=1, keepdims=True))
            alpha = jnp.exp(m_old - m_new)
            p = jnp.exp(s - m_new)
            l_s[...] = alpha * l_s[...] + jnp.sum(p, axis=1, keepdims=True)
            acc_s[...] = alpha * acc_s[...] + jnp.dot(p.astype(BF16), v, preferred_element_type=F32)
            m_s[...] = m_new

        s_buf[0] = scores(0)

        def loop_body(j, carry):
            nxt = scores(j + 1)
            update(s_buf[lax.rem(j, 2)], j)
            s_buf[lax.rem(j + 1, 2)] = nxt
            return carry

        lax.fori_loop(0, qi, loop_body, 0)
        s = s_buf[lax.rem(qi, 2)]
        row = lax.broadcasted_iota(jnp.int32, s.shape, 0)
        col = lax.broadcasted_iota(jnp.int32, s.shape, 1)
        update(jnp.where(row >= col, s, NEG), qi)
        l = l_s[...]
        o_ref[...] = (acc_s[...] / l).astype(o_ref.dtype)
        lse_ref[0, 0] = _row_of(m_s[...] + jnp.log(l), tile)

    return pl.pallas_call(
        body, name="attn_fwd",
        out_shape=(jax.ShapeDtypeStruct((S, N_HEADS_MLA * V_DIM), BF16),
                   jax.ShapeDtypeStruct((N_HEADS_MLA, nq, 1, tile), F32)),
        grid=(N_HEADS_MLA, nq),
        in_specs=[pl.BlockSpec((tile, HEAD_PAD), lambda h, i: (i, h)),
                  pl.BlockSpec((S, HEAD_PAD), lambda h, i: (0, h)),
                  pl.BlockSpec((S, LANE), lambda h, i: (0, 0))],
        out_specs=(pl.BlockSpec((tile, V_DIM), lambda h, i: (i, h)),
                   pl.BlockSpec((1, 1, 1, tile), lambda h, i: (h, i, 0, 0))),
        scratch_shapes=[pltpu.VMEM((tile, 1), F32), pltpu.VMEM((tile, 1), F32), pltpu.VMEM((tile, V_DIM), F32),
                        pltpu.VMEM((2, tile, tile), F32)],
        compiler_params=_params(("parallel", "arbitrary")),
    )(q, kv, kp)


def _attn_delta(o, do, tile):
    S = o.shape[0]
    nq = S // tile

    def body(o_ref, do_ref, d_ref, dob_ref):
        dov = do_ref[...]
        prod = o_ref[...].astype(F32) * dov
        dob_ref[...] = dov.astype(BF16)
        for h in range(N_HEADS_MLA):
            col = jnp.sum(prod[:, h * V_DIM:(h + 1) * V_DIM], axis=1, keepdims=True)
            d_ref[h, 0] = _row_of(col, tile)

    return pl.pallas_call(
        body, name="attn_delta",
        out_shape=(jax.ShapeDtypeStruct((N_HEADS_MLA, nq, 1, tile), F32), jax.ShapeDtypeStruct(o.shape, BF16)),
        grid=(nq,),
        in_specs=[pl.BlockSpec((tile, o.shape[1]), lambda i: (i, 0)), pl.BlockSpec((tile, o.shape[1]), lambda i: (i, 0))],
        out_specs=(pl.BlockSpec((N_HEADS_MLA, 1, 1, tile), lambda i: (0, i, 0, 0)),
                   pl.BlockSpec((tile, o.shape[1]), lambda i: (i, 0))),
        compiler_params=_params(("parallel",)),
    )(o, do)


def _attn_bwd(q, kv, kp, do, lse, delta, tile):
    S = q.shape[0]
    nq = S // tile
    scale = QK_DIM ** -0.5
    nt = (((1,), (1,)), ((), ()))
    tn = (((0,), (0,)), ((), ()))

    def body(kv_ref, kp_ref, q_ref, do_ref, lse_ref, d_ref, dq_ref, dkv_ref, dkp_ref, dk_s, dv_s):
        ki = pl.program_id(1)
        k = jnp.concatenate([kv_ref[:, 0:LANE], kp_ref[...]], axis=1)
        v = kv_ref[:, LANE:2 * LANE]

        @pl.when(ki == 0)
        def _():
            dq_ref[...] = jnp.zeros_like(dq_ref)

        dk_s[...] = jnp.zeros_like(dk_s)
        dv_s[...] = jnp.zeros_like(dv_s)

        def step(qi, masked):
            start = pl.multiple_of(qi * tile, tile)
            qv = q_ref[pl.ds(start, tile), :]
            dov = do_ref[pl.ds(start, tile), :]
            st = lax.dot_general(k, qv, nt, preferred_element_type=F32) * scale
            pt = jnp.exp(st - lse_ref[0, qi])
            if masked:
                krow = lax.broadcasted_iota(jnp.int32, pt.shape, 0)
                qcol = lax.broadcasted_iota(jnp.int32, pt.shape, 1)
                pt = jnp.where(krow <= qcol, pt, 0.0)
            dv_s[...] += jnp.dot(pt.astype(BF16), dov, preferred_element_type=F32)
            dpt = lax.dot_general(v, dov, nt, preferred_element_type=F32)
            dst = (pt * (dpt - d_ref[0, qi]) * scale).astype(BF16)
            dk_s[...] += jnp.dot(dst, qv, preferred_element_type=F32)
            dq_ref[pl.ds(start, tile), :] += lax.dot_general(dst, k, tn, preferred_element_type=F32)

        step(ki, True)

        def loop_body(qi, carry):
            step(qi, False)
            return carry

        lax.fori_loop(ki + 1, nq, loop_body, 0)
        dkv_ref[...] = jnp.concatenate([dk_s[:, 0:LANE], dv_s[...]], axis=1).astype(dkv_ref.dtype)
        dkp_ref[0] = dk_s[:, LANE:2 * LANE]

    return pl.pallas_call(
        body, name="attn_bwd",
        out_shape=(jax.ShapeDtypeStruct((S, N_HEADS_MLA * HEAD_PAD), F32),
                   jax.ShapeDtypeStruct((S, N_HEADS_MLA * HEAD_PAD), BF16),
                   jax.ShapeDtypeStruct((N_HEADS_MLA, S, LANE), F32)),
        grid=(N_HEADS_MLA, nq),
        in_specs=[pl.BlockSpec((tile, HEAD_PAD), lambda h, i: (i, h)),
                  pl.BlockSpec((tile, LANE), lambda h, i: (i, 0)),
                  pl.BlockSpec((S, HEAD_PAD), lambda h, i: (0, h)),
                  pl.BlockSpec((S, V_DIM), lambda h, i: (0, h)),
                  pl.BlockSpec((1, nq, 1, tile), lambda h, i: (h, 0, 0, 0)),
                  pl.BlockSpec((1, nq, 1, tile), lambda h, i: (h, 0, 0, 0))],
        out_specs=(pl.BlockSpec((S, HEAD_PAD), lambda h, i: (0, h)),
                   pl.BlockSpec((tile, HEAD_PAD), lambda h, i: (i, h)),
                   pl.BlockSpec((1, tile, LANE), lambda h, i: (h, i, 0))),
        scratch_shapes=[pltpu.VMEM((tile, HEAD_PAD), F32), pltpu.VMEM((tile, V_DIM), F32)],
        compiler_params=_params(("parallel", "arbitrary")),
    )(kv, kp, q, do, lse, delta)


def _shift_down(cur, halo, k):
    sh = pltpu.roll(cur, k, 0)
    hs = pltpu.roll(halo, k, 0)
    rows = lax.broadcasted_iota(jnp.int32, hs.shape, 0)
    first = jnp.where(rows < k, hs, sh[0:SUBLANE])
    if cur.shape[0] == SUBLANE:
        return first
    return jnp.concatenate([first, sh[SUBLANE:]], axis=0)


def _shift_up(cur, nxt, k):
    n = cur.shape[0]
    sh = pltpu.roll(cur, n - k, 0)
    ns = pltpu.roll(nxt, SUBLANE - k, 0)
    rows = lax.broadcasted_iota(jnp.int32, ns.shape, 0)
    last = jnp.where(rows >= SUBLANE - k, ns, sh[n - SUBLANE:])
    if n == SUBLANE:
        return last
    return jnp.concatenate([sh[:n - SUBLANE], last], axis=0)


def _conv_pre(cur, halo, w, b):
    out = b + w[3:4] * cur
    for k in range(1, CONV_WIDTH):
        out = out + w[3 - k:4 - k] * _shift_down(cur, halo, k)
    return out


def _conv_fwd(xbc, w, b):
    S = xbc.shape[0]

    def fn(i, n, cur, prev, w, b):
        halo = jnp.where(i > 0, prev, 0.0)
        pre = _conv_pre(cur, halo, w, b)
        return (pre * _sigmoid(pre),)

    return _rows(fn, S, 256, [_row(xbc), _prev8(xbc), _full(w), _full(b)], [("row", xbc.shape[1], F32)], "conv_fwd")[0]


def _conv_bwd(xbc, dact, w, b):
    S, C = xbc.shape

    def dsilu(pre):
        s = _sigmoid(pre)
        return s * (1.0 + pre * (1.0 - s))

    def fn(i, n, cur, prev, nxt, dcur, dnxt, w, b):
        halo = jnp.where(i > 0, prev, 0.0)
        pre = _conv_pre(cur, halo, w, b)
        dpre = dcur * dsilu(pre)
        pre_n = _conv_pre(nxt, cur[cur.shape[0] - SUBLANE:], w, b)
        dpre_n = jnp.where(i < n - 1, dnxt * dsilu(pre_n), 0.0)
        dx = w[3:4] * dpre
        rows = lax.broadcasted_iota(jnp.int32, (SUBLANE, C), 0)
        dw = jnp.where(rows == 3, _colsum(dpre * cur), 0.0)
        for k in range(1, CONV_WIDTH):
            dx = dx + w[3 - k:4 - k] * _shift_up(dpre, dpre_n, k)
            dw = dw + jnp.where(rows == 3 - k, _colsum(dpre * _shift_down(cur, halo, k)), 0.0)
        return dx, dw, _colsum(dpre)

    return _rows(fn, S, 256, [_row(xbc), _prev8(xbc), _next8(xbc), _row(dact), _next8(dact), _full(w), _full(b)],
                 [("row", C, BF16), ("acc", (SUBLANE, C), F32), ("acc", (1, C), F32)], "conv_bwd")


def _softplus(x):
    return jnp.maximum(x, 0.0) + jnp.log1p(jnp.exp(-jnp.abs(x)))


def _cumsum_rows(x):
    rows = lax.broadcasted_iota(jnp.int32, x.shape, 0)
    s = 1
    while s < x.shape[0]:
        x = x + jnp.where(rows >= s, pltpu.roll(x, s, 0), 0.0)
        s *= 2
    return x


def _revcumsum_rows(x):
    n = x.shape[0]
    rows = lax.broadcasted_iota(jnp.int32, x.shape, 0)
    s = 1
    while s < n:
        x = x + jnp.where(rows < n - s, pltpu.roll(x, n - s, 0), 0.0)
        s *= 2
    return x


def _dt_prep(dt_raw, dt_bias, a_log):
    S = dt_raw.shape[0]

    def body(raw_ref, bias_ref, alog_ref, dt_ref, cum_ref, cumt_ref):
        dt = _softplus(raw_ref[...] + bias_ref[...])
        cum = _cumsum_rows(dt * (-jnp.exp(alog_ref[...])))
        dt_ref[...] = dt
        cum_ref[...] = cum
        cumt_ref[...] = cum.T

    return pl.pallas_call(
        body, name="dt_prep",
        out_shape=(jax.ShapeDtypeStruct((S, LANE), F32), jax.ShapeDtypeStruct((S, LANE), F32),
                   jax.ShapeDtypeStruct((LANE, S), F32)),
        grid=(S // CHUNK,),
        in_specs=[pl.BlockSpec((CHUNK, LANE), lambda i: (i, 0)), pl.BlockSpec((1, LANE), lambda i: (0, 0)),
                  pl.BlockSpec((1, LANE), lambda i: (0, 0))],
        out_specs=(pl.BlockSpec((CHUNK, LANE), lambda i: (i, 0)), pl.BlockSpec((CHUNK, LANE), lambda i: (i, 0)),
                   pl.BlockSpec((LANE, CHUNK), lambda i: (0, i))),
        compiler_params=_params(("parallel",)),
    )(dt_raw, dt_bias, a_log)


def _group_cols(t):
    S = t.shape[0]
    return jnp.transpose(t[:, :N_HEADS_SSM].reshape(S, SSM_GROUPS, HEADS_PER_GROUP), (1, 0, 2))


def _ungroup_cols(t):
    S = t.shape[1]
    flat = jnp.transpose(t, (1, 0, 2)).reshape(S, N_HEADS_SSM)
    return jnp.pad(flat, ((0, 0), (0, LANE - N_HEADS_SSM)))


_NT = (((1,), (1,)), ((), ()))
_TN = (((0,), (0,)), ((), ()))
P = SSM_HEADDIM
GW = HEADS_PER_GROUP * SSM_HEADDIM


def _decay(cc, cr):
    L = cc.shape[0]
    i = lax.broadcasted_iota(jnp.int32, (L, L), 0)
    j = lax.broadcasted_iota(jnp.int32, (L, L), 1)
    return jnp.exp(jnp.where(i >= j, cc - cr, NEG))


def _ssd_fwd(xbc_c, dt_g, cum_g, cumt_g):
    S = xbc_c.shape[0]
    nc = S // CHUNK
    L = CHUNK
    boff = D_INNER // D_STATE

    def body(x_ref, b_ref, c_ref, dt_ref, cum_ref, cumt_ref, y_ref, st_ref, state):
        c = pl.program_id(1)

        @pl.when(c == 0)
        def _():
            state[...] = jnp.zeros_like(state)

        bm = b_ref[...].astype(BF16)
        cm = c_ref[...].astype(BF16)
        cb = lax.dot_general(cm, bm, _NT, preferred_element_type=F32)
        for r in range(HEADS_PER_GROUP):
            cc = cum_ref[0, :, r:r + 1]
            cr = cumt_ref[0, r:r + 1, :]
            m = (cb * _decay(cc, cr)).astype(BF16)
            xdt = x_ref[:, r * P:(r + 1) * P] * dt_ref[0, :, r:r + 1]
            st = state[r * P:(r + 1) * P, :]
            st_ref[0, 0, r * P:(r + 1) * P, :] = st
            y = jnp.dot(m, xdt.astype(BF16), preferred_element_type=F32)
            y = y + lax.dot_general(cm, st.astype(BF16), _NT, preferred_element_type=F32) * jnp.exp(cc)
            y_ref[:, r * P:(r + 1) * P] = y
            cl = cum_ref[0, L - 1:L, r:r + 1]
            wend = jnp.exp(cl - cc)
            state[r * P:(r + 1) * P, :] = st * jnp.exp(cl) + lax.dot_general(
                (xdt * wend).astype(BF16), bm, _TN, preferred_element_type=F32)

    return pl.pallas_call(
        body, name="ssd_fwd",
        out_shape=(jax.ShapeDtypeStruct((S, D_INNER), F32), jax.ShapeDtypeStruct((SSM_GROUPS, nc, GW, D_STATE), F32)),
        grid=(SSM_GROUPS, nc),
        in_specs=[pl.BlockSpec((L, GW), lambda g, c: (c, g)),
                  pl.BlockSpec((L, D_STATE), lambda g, c: (c, boff + g)),
                  pl.BlockSpec((L, D_STATE), lambda g, c: (c, boff + SSM_GROUPS + g)),
                  pl.BlockSpec((1, L, HEADS_PER_GROUP), lambda g, c: (g, c, 0)),
                  pl.BlockSpec((1, L, HEADS_PER_GROUP), lambda g, c: (g, c, 0)),
                  pl.BlockSpec((1, HEADS_PER_GROUP, L), lambda g, c: (g, 0, c))],
        out_specs=(pl.BlockSpec((L, GW), lambda g, c: (c, g)),
                   pl.BlockSpec((1, 1, GW, D_STATE), lambda g, c: (g, c, 0, 0))),
        scratch_shapes=[pltpu.VMEM((GW, D_STATE), F32)],
        compiler_params=_params(("parallel", "arbitrary")),
    )(xbc_c, xbc_c, xbc_c, dt_g, cum_g, cumt_g)


def _ssd_bwd(xbc_c, dt_g, cum_g, cumt_g, states, dy, dx_skip):
    S = xbc_c.shape[0]
    nc = S // CHUNK
    L = CHUNK
    boff = D_INNER // D_STATE
    rev = lambda c: nc - 1 - c

    def body(x_ref, b_ref, c_ref, dt_ref, cum_ref, cumt_ref, st_ref, dy_ref, skip_ref,
             dx_ref, db_ref, dc_ref, ddt_ref, dcum_ref, dstate):
        c = pl.program_id(1)

        @pl.when(c == 0)
        def _():
            dstate[...] = jnp.zeros_like(dstate)

        bf = b_ref[...]
        bm = bf.astype(BF16)
        cm = c_ref[...].astype(BF16)
        cb = lax.dot_general(cm, bm, _NT, preferred_element_type=F32)
        dcb = jnp.zeros((L, L), F32)
        dbs = jnp.zeros((L, D_STATE), F32)
        dcs = jnp.zeros((L, D_STATE), F32)
        rowid = lax.broadcasted_iota(jnp.int32, (L, 1), 0)
        for r in range(HEADS_PER_GROUP):
            sl = slice(r * P, (r + 1) * P)
            cc = cum_ref[0, :, r:r + 1]
            cr = cumt_ref[0, r:r + 1, :]
            dtc = dt_ref[0, :, r:r + 1]
            decay = _decay(cc, cr)
            m = cb * decay
            xr = x_ref[:, sl]
            xdt = xr * dtc
            xdb = xdt.astype(BF16)
            dyr = dy_ref[:, sl]
            dyb = dyr.astype(BF16)
            st = st_ref[0, 0, sl, :]
            stb = st.astype(BF16)
            ds = dstate[sl, :]
            dsb = ds.astype(BF16)
            ecc = jnp.exp(cc)
            cl = cum_ref[0, L - 1:L, r:r + 1]
            ecl = jnp.exp(cl)
            wend = jnp.exp(cl - cc)

            g = lax.dot_general(dyb, xdb, _NT, preferred_element_type=F32)
            q = g * m
            dcb = dcb + g * decay
            dcum = jnp.sum(q, axis=1, keepdims=True) - _row_of_t(jnp.sum(q, axis=0, keepdims=True), L)
            dxd = lax.dot_general(m.astype(BF16), dyb, _TN, preferred_element_type=F32)
            dxd = dxd + lax.dot_general(bm, dsb, _NT, preferred_element_type=F32) * wend
            yoff = lax.dot_general(cm, stb, _NT, preferred_element_type=F32) * ecc
            dcum = dcum + jnp.sum(dyr * yoff, axis=1, keepdims=True)
            dcs = dcs + jnp.dot(dyb, stb, preferred_element_type=F32) * ecc
            t = jnp.dot(xdb, dsb, preferred_element_type=F32)
            dbs = dbs + t * wend
            vj = jnp.sum(t * bf, axis=1, keepdims=True) * wend
            dcum = dcum - vj
            dlast = jnp.sum(vj) + ecl * jnp.sum(ds * st)
            dcum = dcum + jnp.where(rowid == L - 1, dlast, 0.0)
            dstate[sl, :] = ecl * ds + lax.dot_general((dyr * ecc).astype(BF16), cm, _TN, preferred_element_type=F32)

            dx_ref[:, sl] = dxd * dtc + skip_ref[:, sl]
            ddt_ref[0, :, r:r + 1] = jnp.sum(dxd * xr, axis=1, keepdims=True)
            dcum_ref[0, :, r:r + 1] = dcum
        dcbb = dcb.astype(BF16)
        dc_ref[...] = dcs + jnp.dot(dcbb, bm, preferred_element_type=F32)
        db_ref[...] = dbs + lax.dot_general(dcbb, cm, _TN, preferred_element_type=F32)

    return pl.pallas_call(
        body, name="ssd_bwd",
        out_shape=(jax.ShapeDtypeStruct((S, D_INNER), F32),
                   jax.ShapeDtypeStruct((S, SSM_GROUPS * D_STATE), F32),
                   jax.ShapeDtypeStruct((S, SSM_GROUPS * D_STATE), F32),
                   jax.ShapeDtypeStruct((SSM_GROUPS, S, HEADS_PER_GROUP), F32),
                   jax.ShapeDtypeStruct((SSM_GROUPS, S, HEADS_PER_GROUP), F32)),
        grid=(SSM_GROUPS, nc),
        in_specs=[pl.BlockSpec((L, GW), lambda g, c: (rev(c), g)),
                  pl.BlockSpec((L, D_STATE), lambda g, c: (rev(c), boff + g)),
                  pl.BlockSpec((L, D_STATE), lambda g, c: (rev(c), boff + SSM_GROUPS + g)),
                  pl.BlockSpec((1, L, HEADS_PER_GROUP), lambda g, c: (g, rev(c), 0)),
                  pl.BlockSpec((1, L, HEADS_PER_GROUP), lambda g, c: (g, rev(c), 0)),
                  pl.BlockSpec((1, HEADS_PER_GROUP, L), lambda g, c: (g, 0, rev(c))),
                  pl.BlockSpec((1, 1, GW, D_STATE), lambda g, c: (g, rev(c), 0, 0)),
                  pl.BlockSpec((L, GW), lambda g, c: (rev(c), g)),
                  pl.BlockSpec((L, GW), lambda g, c: (rev(c), g))],
        out_specs=(pl.BlockSpec((L, GW), lambda g, c: (rev(c), g)),
                   pl.BlockSpec((L, D_STATE), lambda g, c: (rev(c), g)),
                   pl.BlockSpec((L, D_STATE), lambda g, c: (rev(c), g)),
                   pl.BlockSpec((1, L, HEADS_PER_GROUP), lambda g, c: (g, rev(c), 0)),
                   pl.BlockSpec((1, L, HEADS_PER_GROUP), lambda g, c: (g, rev(c), 0))),
        scratch_shapes=[pltpu.VMEM((GW, D_STATE), F32)],
        compiler_params=_params(("parallel", "arbitrary")),
    )(xbc_c, xbc_c, xbc_c, dt_g, cum_g, cumt_g, states, dy, dx_skip)


def _row_of_t(row, n):
    eye = lax.broadcasted_iota(jnp.int32, (n, n), 0) == lax.broadcasted_iota(jnp.int32, (n, n), 1)
    return jnp.sum(jnp.where(eye, row, 0.0), axis=1, keepdims=True)


def _dt_bwd(dt_raw, dt_bias, a_log, ddt_x, dcum):
    S = dt_raw.shape[0]

    def fn(i, n, raw, ddx, dcu, bias, alog):
        xx = raw + bias
        dt = _softplus(xx)
        a = -jnp.exp(alog)
        dda = _revcumsum_rows(dcu)
        ddt = ddx + dda * a
        lane = lax.broadcasted_iota(jnp.int32, raw.shape, 1)
        draw = jnp.where(lane < N_HEADS_SSM, ddt * _sigmoid(xx), 0.0)
        return draw, _colsum(draw), _colsum(dda * dt) * a

    return _rows(fn, S, CHUNK, [_row(dt_raw), _row(ddt_x), _row(dcum), _full(dt_bias), _full(a_log)],
                 [("row", LANE, BF16), ("acc", (1, LANE), F32), ("acc", (1, LANE), F32)], "dt_bwd")


PAIRS = HEADS_PER_GROUP // 2
SPREAD_W = HEADS_PER_GROUP * LANE


def _spread_matrix():
    e = np.zeros((SSM_GROUPS, LANE, SPREAD_W), np.float32)
    for g in range(SSM_GROUPS):
        for r in range(HEADS_PER_GROUP):
            e[g, g * HEADS_PER_GROUP + r, r * LANE:(r + 1) * LANE] = 1.0
    return jnp.asarray(e, BF16)


def _pieces(v, n):
    out = []
    for _ in range(n):
        p = v.astype(BF16)
        out.append(p)
        v = v - p.astype(F32)
    return out


def _spread(v, e, n):
    tot = None
    for p in _pieces(v, n):
        t = jnp.dot(p, e, preferred_element_type=F32)
        tot = t if tot is None else tot + t
    return tot


def _gather_rows(z, e):
    hi, lo = _pieces(z, 2)
    return lax.dot_general(hi, e, _NT, preferred_element_type=F32) + lax.dot_general(lo, e, _NT, preferred_element_type=F32)


def _decay_pair(cc, cr, transposed):
    L = cc.shape[0]
    halves = []
    for h in range(L // LANE):
        i = lax.broadcasted_iota(jnp.int32, (L, LANE), 0)
        j = lax.broadcasted_iota(jnp.int32, (L, LANE), 1) + h * LANE
        crh = cr[:, h * LANE:(h + 1) * LANE]
        if transposed:
            halves.append(jnp.exp(jnp.where(j >= i, crh - cc, NEG)))
        else:
            halves.append(jnp.exp(jnp.where(i >= j, cc - crh, NEG)))
    return jnp.concatenate(halves, axis=1)


def _ssd_fwd2(xbc_c, dt, cum, cumt_g, spread):
    S = xbc_c.shape[0]
    nc = S // CHUNK
    L = CHUNK
    boff = D_INNER // D_STATE

    def body(x_ref, b_ref, c_ref, dt_ref, cum_ref, cumt_ref, e_ref, y_ref, st_ref, state):
        c = pl.program_id(1)

        @pl.when(c == 0)
        def _():
            state[...] = jnp.zeros_like(state)

        e = e_ref[0]
        bm = b_ref[...].astype(BF16)
        cm = c_ref[...].astype(BF16)
        cb = lax.dot_general(cm, bm, _NT, preferred_element_type=F32)
        rep_cum = _spread(cum_ref[...], e, 3)
        rep_dt = _spread(dt_ref[...], e, 2)
        lo = lax.broadcasted_iota(jnp.int32, (L, LANE), 1) < P
        lo1 = lax.broadcasted_iota(jnp.int32, (1, LANE), 1) < P
        top = lax.broadcasted_iota(jnp.int32, (LANE, LANE), 0) < P
        for p in range(PAIRS):
            t0, t1 = 2 * p * LANE, (2 * p + 1) * LANE
            cc0, cc1 = rep_cum[:, t0:t0 + LANE], rep_cum[:, t1:t1 + LANE]
            ccp = jnp.where(lo, cc0, cc1)
            cl0, cl1 = cc0[L - 1:L, :], cc1[L - 1:L, :]
            clp = jnp.where(lo1, cl0, cl1)
            xdt = x_ref[:, p * LANE:(p + 1) * LANE] * jnp.where(lo, rep_dt[:, t0:t0 + LANE], rep_dt[:, t1:t1 + LANE])
            xdb = xdt.astype(BF16)
            ys = []
            for r, cc in ((2 * p, cc0), (2 * p + 1, cc1)):
                m = (cb * _decay_pair(cc, cumt_ref[0, r:r + 1, :], False)).astype(BF16)
                ys.append(jnp.dot(m, xdb, preferred_element_type=F32))
            st = state[p * LANE:(p + 1) * LANE, :]
            st_ref[0, 0, p * LANE:(p + 1) * LANE, :] = st
            yoff = lax.dot_general(cm, st.astype(BF16), _NT, preferred_element_type=F32) * jnp.exp(ccp)
            y_ref[:, p * LANE:(p + 1) * LANE] = jnp.where(lo, ys[0], ys[1]) + yoff
            wend = jnp.exp(clp - ccp)
            ecl = jnp.where(top, jnp.exp(cl0), jnp.exp(cl1))
            state[p * LANE:(p + 1) * LANE, :] = st * ecl + lax.dot_general(
                (xdt * wend).astype(BF16), bm, _TN, preferred_element_type=F32)

    return pl.pallas_call(
        body, name="ssd_fwd",
        out_shape=(jax.ShapeDtypeStruct((S, D_INNER), F32), jax.ShapeDtypeStruct((SSM_GROUPS, nc, GW, D_STATE), F32)),
        grid=(SSM_GROUPS, nc),
        in_specs=[pl.BlockSpec((L, GW), lambda g, c: (c, g)),
                  pl.BlockSpec((L, D_STATE), lambda g, c: (c, boff + g)),
                  pl.BlockSpec((L, D_STATE), lambda g, c: (c, boff + SSM_GROUPS + g)),
                  pl.BlockSpec((L, LANE), lambda g, c: (c, 0)),
                  pl.BlockSpec((L, LANE), lambda g, c: (c, 0)),
                  pl.BlockSpec((1, HEADS_PER_GROUP, L), lambda g, c: (g, 0, c)),
                  pl.BlockSpec((1, LANE, SPREAD_W), lambda g, c: (g, 0, 0))],
        out_specs=(pl.BlockSpec((L, GW), lambda g, c: (c, g)),
                   pl.BlockSpec((1, 1, GW, D_STATE), lambda g, c: (g, c, 0, 0))),
        scratch_shapes=[pltpu.VMEM((GW, D_STATE), F32)],
        compiler_params=_params(("parallel", "arbitrary")),
    )(xbc_c, xbc_c, xbc_c, dt, cum, cumt_g, spread)


def _ssd_bwd2(xbc_c, dt, cum, cumt_g, spread, states, dy, dx_skip):
    S = xbc_c.shape[0]
    nc = S // CHUNK
    L = CHUNK
    boff = D_INNER // D_STATE
    rev = lambda c: nc - 1 - c

    def body(x_ref, b_ref, c_ref, dt_ref, cum_ref, cumt_ref, e_ref, st_ref, dy_ref, skip_ref,
             dx_ref, db_ref, dc_ref, ddt_ref, dcum_ref, dstate):
        c = pl.program_id(1)

        @pl.when(c == 0)
        def _():
            dstate[...] = jnp.zeros_like(dstate)

        e = e_ref[0]
        bf = b_ref[...]
        bm = bf.astype(BF16)
        cm = c_ref[...].astype(BF16)
        cb = lax.dot_general(cm, bm, _NT, preferred_element_type=F32)
        cbt = lax.dot_general(bm, cm, _NT, preferred_element_type=F32)
        rep_cum = _spread(cum_ref[...], e, 3)
        rep_dt = _spread(dt_ref[...], e, 2)
        lane = lax.broadcasted_iota(jnp.int32, (L, LANE), 1)
        lo = lane < P
        lo1 = lax.broadcasted_iota(jnp.int32, (1, LANE), 1) < P
        top = lax.broadcasted_iota(jnp.int32, (LANE, LANE), 0) < P
        last = lax.broadcasted_iota(jnp.int32, (L, LANE), 0) == L - 1
        dcb = jnp.zeros((L, L), F32)
        dcbt = jnp.zeros((L, L), F32)
        dbs = jnp.zeros((L, D_STATE), F32)
        dcs = jnp.zeros((L, D_STATE), F32)
        zs, zds = [], []
        for p in range(PAIRS):
            sl = slice(p * LANE, (p + 1) * LANE)
            t0, t1 = 2 * p * LANE, (2 * p + 1) * LANE
            cc0, cc1 = rep_cum[:, t0:t0 + LANE], rep_cum[:, t1:t1 + LANE]
            ccp = jnp.where(lo, cc0, cc1)
            cl0, cl1 = cc0[L - 1:L, :], cc1[L - 1:L, :]
            w0, w1 = jnp.exp(cl0 - cc0), jnp.exp(cl1 - cc1)
            wend = jnp.where(lo, w0, w1)
            ecc = jnp.exp(ccp)
            ecl0, ecl1 = jnp.exp(cl0), jnp.exp(cl1)
            dtp = jnp.where(lo, rep_dt[:, t0:t0 + LANE], rep_dt[:, t1:t1 + LANE])
            xp = x_ref[:, sl]
            xdt = xp * dtp
            xdb = xdt.astype(BF16)
            dyp = dy_ref[:, sl]
            st = st_ref[0, 0, sl, :]
            stb = st.astype(BF16)
            ds = dstate[sl, :]
            dsb = ds.astype(BF16)
            yoff = lax.dot_general(cm, stb, _NT, preferred_element_type=F32) * ecc
            dye = (dyp * ecc).astype(BF16)
            dcs = dcs + jnp.dot(dye, stb, preferred_element_type=F32)
            dstate[sl, :] = jnp.where(top, ecl0, ecl1) * ds + lax.dot_general(dye, cm, _TN, preferred_element_type=F32)
            dxd = lax.dot_general(bm, dsb, _NT, preferred_element_type=F32) * wend
            sst = ds * st
            dyo = dyp * yoff
            mts = []
            for r, cc, w, ecl, keep, keep_rows in ((2 * p, cc0, w0, ecl0, lo, top), (2 * p + 1, cc1, w1, ecl1, ~lo, ~top)):
                cr = cumt_ref[0, r:r + 1, :]
                decay = _decay_pair(cc, cr, False)
                decay_t = _decay_pair(cc, cr, True)
                m = cb * decay
                mt = cbt * decay_t
                dyr = jnp.where(keep, dyp, 0.0).astype(BF16)
                g = lax.dot_general(dyr, xdb, _NT, preferred_element_type=F32)
                gt = lax.dot_general(xdb, dyr, _NT, preferred_element_type=F32)
                q = g * m
                qt = gt * mt
                dcb = dcb + g * decay
                dcbt = dcbt + gt * decay_t
                mts.append(jnp.dot(mt.astype(BF16), dyr, preferred_element_type=F32))
                t = jnp.dot(jnp.where(keep, xdt, 0.0).astype(BF16), dsb, preferred_element_type=F32)
                dbs = dbs + t * w
                tb = t * bf * w
                end_row = _colsum(tb) + ecl * _colsum(jnp.where(keep_rows, sst, 0.0))
                z = (q[:, 0:LANE] + q[:, LANE:2 * LANE]) - (qt[:, 0:LANE] + qt[:, LANE:2 * LANE])
                z = z + jnp.where(keep, dyo, 0.0) - tb + jnp.where(last, end_row, 0.0)
                zs.append(z)
            dxd = dxd + mts[0] + mts[1]
            dx_ref[:, sl] = dxd * dtp + skip_ref[:, sl]
            zd = dxd * xp
            zds.append(jnp.where(lo, zd, 0.0))
            zds.append(jnp.where(lo, 0.0, zd))
        dc_ref[...] = dcs + jnp.dot(dcb.astype(BF16), bm, preferred_element_type=F32)
        db_ref[...] = dbs + jnp.dot(dcbt.astype(BF16), cm, preferred_element_type=F32)
        dcum_ref[0] = _gather_rows(jnp.concatenate(zs, axis=1), e)
        ddt_ref[0] = _gather_rows(jnp.concatenate(zds, axis=1), e)

    return pl.pallas_call(
        body, name="ssd_bwd",
        out_shape=(jax.ShapeDtypeStruct((S, D_INNER), F32),
                   jax.ShapeDtypeStruct((S, SSM_GROUPS * D_STATE), F32),
                   jax.ShapeDtypeStruct((S, SSM_GROUPS * D_STATE), F32),
                   jax.ShapeDtypeStruct((SSM_GROUPS, S, LANE), F32),
                   jax.ShapeDtypeStruct((SSM_GROUPS, S, LANE), F32)),
        grid=(SSM_GROUPS, nc),
        in_specs=[pl.BlockSpec((L, GW), lambda g, c: (rev(c), g)),
                  pl.BlockSpec((L, D_STATE), lambda g, c: (rev(c), boff + g)),
                  pl.BlockSpec((L, D_STATE), lambda g, c: (rev(c), boff + SSM_GROUPS + g)),
                  pl.BlockSpec((L, LANE), lambda g, c: (rev(c), 0)),
                  pl.BlockSpec((L, LANE), lambda g, c: (rev(c), 0)),
                  pl.BlockSpec((1, HEADS_PER_GROUP, L), lambda g, c: (g, 0, rev(c))),
                  pl.BlockSpec((1, LANE, SPREAD_W), lambda g, c: (g, 0, 0)),
                  pl.BlockSpec((1, 1, GW, D_STATE), lambda g, c: (g, rev(c), 0, 0)),
                  pl.BlockSpec((L, GW), lambda g, c: (rev(c), g)),
                  pl.BlockSpec((L, GW), lambda g, c: (rev(c), g))],
        out_specs=(pl.BlockSpec((L, GW), lambda g, c: (rev(c), g)),
                   pl.BlockSpec((L, D_STATE), lambda g, c: (rev(c), g)),
                   pl.BlockSpec((L, D_STATE), lambda g, c: (rev(c), g)),
                   pl.BlockSpec((1, L, LANE), lambda g, c: (g, rev(c), 0)),
                   pl.BlockSpec((1, L, LANE), lambda g, c: (g, rev(c), 0))),
        scratch_shapes=[pltpu.VMEM((GW, D_STATE), F32)],
        compiler_params=_params(("parallel", "arbitrary")),
    )(xbc_c, xbc_c, xbc_c, dt, cum, cumt_g, spread, states, dy, dx_skip)


def _dt_bwd2(dt_raw, dt_bias, a_log, ddt_x, dcum):
    S = dt_raw.shape[0]
    n = S // CHUNK

    def body(raw_ref, ddx_ref, dcu_ref, bias_ref, alog_ref, draw_ref, gb_ref, ga_ref):
        i = pl.program_id(0)

        @pl.when(i == 0)
        def _():
            gb_ref[...] = jnp.zeros_like(gb_ref)
            ga_ref[...] = jnp.zeros_like(ga_ref)

        ddx, dcu = ddx_ref[0], dcu_ref[0]
        for g in range(1, SSM_GROUPS):
            ddx = ddx + ddx_ref[g]
            dcu = dcu + dcu_ref[g]
        xx = raw_ref[...] + bias_ref[...]
        dt = _softplus(xx)
        a = -jnp.exp(alog_ref[...])
        dda = _revcumsum_rows(dcu)
        lane = lax.broadcasted_iota(jnp.int32, xx.shape, 1)
        draw = jnp.where(lane < N_HEADS_SSM, (ddx + dda * a) * _sigmoid(xx), 0.0)
        draw_ref[...] = draw.astype(draw_ref.dtype)
        gb_ref[...] += _colsum(draw)
        ga_ref[...] += _colsum(dda * dt) * a

    row = pl.BlockSpec((CHUNK, LANE), lambda i: (i, 0))
    grp = pl.BlockSpec((SSM_GROUPS, CHUNK, LANE), lambda i: (0, i, 0))
    one = pl.BlockSpec((1, LANE), lambda i: (0, 0))
    return pl.pallas_call(
        body, name="dt_bwd",
        out_shape=(jax.ShapeDtypeStruct((S, LANE), BF16), jax.ShapeDtypeStruct((1, LANE), F32), jax.ShapeDtypeStruct((1, LANE), F32)),
        grid=(n,),
        in_specs=[row, grp, grp, one, one],
        out_specs=(row, one, one),
        compiler_params=_params(("arbitrary",)),
    )(dt_raw, ddt_x, dcum, dt_bias, a_log)


def _adamw(w, g, m, v, name):
    shape = w.shape
    cols = shape[-1]
    rows = int(np.prod(shape[:-1]))
    w2, g2, m2, v2 = (t.reshape(rows, cols) for t in (w, g, m, v))
    tr = rows if rows * cols <= 512 * 1024 else _tile(rows, max(SUBLANE, (512 * 1024 // cols) // SUBLANE * SUBLANE), SUBLANE)
    c1 = 1.0 - ADAM_B1 ** ADAM_STEP
    c2 = 1.0 - ADAM_B2 ** ADAM_STEP

    def body(w_ref, g_ref, m_ref, v_ref, d_ref, mo_ref, vo_ref):
        gv = g_ref[...]
        mn = ADAM_B1 * m_ref[...] + (1.0 - ADAM_B1) * gv
        vn = ADAM_B2 * v_ref[...] + (1.0 - ADAM_B2) * (gv * gv)
        d_ref[...] = -ADAM_LR * ((mn / c1) / (jnp.sqrt(vn / c2) + ADAM_EPS) + ADAM_WD * w_ref[...])
        mo_ref[...] = mn
        vo_ref[...] = vn

    spec = pl.BlockSpec((tr, cols), lambda i: (i, 0))
    outs = pl.pallas_call(
        body, name=name,
        out_shape=tuple(jax.ShapeDtypeStruct((rows, cols), F32) for _ in range(3)),
        grid=(rows // tr,),
        in_specs=[spec] * 4, out_specs=(spec,) * 3,
        compiler_params=_params(("parallel",)),
    )(w2, g2, m2, v2)
    return tuple(o.reshape(shape) for o in outs)


def _prep_weights(w_in, w_uq):
    offs = np.cumsum((0,) + IN_SPLITS)
    pad = lambda t: jnp.pad(t, ((0, 0), (0, LANE - t.shape[1])))
    pieces = dict(
        qkv=w_in[:, offs[0]:offs[2]],
        kr=pad(w_in[:, offs[2]:offs[3]]),
        z=w_in[:, offs[3]:offs[4]],
        xbc=w_in[:, offs[4]:offs[5]],
        dt=pad(w_in[:, offs[5]:offs[6]]),
        g=w_in[:, offs[6]:offs[8]],
    )
    uq = w_uq.reshape(Q_LORA, N_HEADS_MLA, QK_DIM)
    uq = jnp.pad(uq, ((0, 0), (0, 0), (0, HEAD_PAD - QK_DIM))).reshape(Q_LORA, N_HEADS_MLA * HEAD_PAD)
    return pieces, uq


def _local_step(x, p, positions, W, sp, target):
    S = x.shape[0]
    tile = min(ATTN_TILE, S)
    pos_col = positions.reshape(S, 1)
    invf = ROPE_THETA ** (-jnp.arange(0, QK_ROPE, 2, dtype=F32) / QK_ROPE)
    invf = jnp.pad(jnp.concatenate([invf, invf]), (0, LANE - QK_ROPE)).reshape(1, LANE)
    wp, w_uq_p = _prep_weights(W["w_in"], W["w_uq"])
    padl = lambda t: jnp.pad(t, ((0, 0), (0, LANE - t.shape[1])))
    dt_bias_p, a_log_p = padl(sp["dt_bias"]), padl(sp["a_log"])
    dskip_ch = jnp.repeat(sp["d_skip"], SSM_HEADDIM, axis=1)
    p_bf = p.astype(BF16)
    RW = 256

    (u_bf,) = _rows(lambda i, n, x, g: (x * _rstd(x) * g,), S, RW, [_row(x), _full(sp["mix_norm_pre"])],
                    [("row", D_MODEL, BF16)], "norm_pre")
    cqkv = _mm(u_bf, wp["qkv"], name="mm_qkv")
    z = _mm(u_bf, wp["z"], name="mm_z")
    xbc = _mm(u_bf, wp["xbc"], name="mm_xbc")
    gates = _mm(u_bf, wp["g"], name="mm_gates")
    kr_pad = _mm(u_bf, wp["kr"], name="mm_kr")
    dt_raw = _mm(u_bf, wp["dt"], name="mm_dt")

    def qkv_norm(i, n, cq, ckv, gq, gkv):
        return cq * _rstd(cq) * gq, ckv * _rstd(ckv) * gkv

    cqn, ckvn = _rows(qkv_norm, S, 512, [_row(cqkv, Q_LORA, 0), _row(cqkv, KV_LORA, 1), _full(sp["q_norm"]), _full(sp["kv_norm"])],
                      [("row", Q_LORA, BF16), ("row", KV_LORA, BF16)], "qkv_norm")
    q_raw = _mm(cqn, w_uq_p, name="mm_uq")
    kv = _mm(ckvn, W["w_ukv"], out_dtype=BF16, name="mm_ukv")
    q_bf, kp_bf = _rope_fwd(q_raw, kr_pad, pos_col, invf)
    attn, lse = _attn_fwd(q_bf, kv, kp_bf, tile)

    xbc_c = _conv_fwd(xbc, sp["conv_w"], sp["conv_b"])
    dt, cum, cumt = _dt_prep(dt_raw, dt_bias_p, a_log_p)
    cumt_g = cumt[:N_HEADS_SSM].reshape(SSM_GROUPS, HEADS_PER_GROUP, S)
    spread = _spread_matrix()
    y, states = _ssd_fwd2(xbc_c, dt, cum, cumt_g, spread)

    GN = D_INNER // SSM_GROUPS

    def gated(y, xs, z, dsk):
        yt = y + dsk * xs
        sz = _sigmoid(z)
        return yt, sz, yt * (z * sz)

    def gated_norm(i, n, y, xs, z, dsk, gn):
        _, _, yg = gated(y, xs, z, dsk)
        parts = []
        for g in range(SSM_GROUPS):
            blk = yg[:, g * GN:(g + 1) * GN]
            parts.append(blk * _rstd(blk) * gn[:, g * GN:(g + 1) * GN])
        return (jnp.concatenate(parts, axis=1),)

    (ssm,) = _rows(gated_norm, S, 128, [_row(y), _row(xbc_c, D_INNER, 0), _row(z), _full(dskip_ch), _full(sp["ssm_norm"])],
                   [("row", D_INNER, BF16)], "gated_norm")

    a_o = _mm(attn, W["w_attn_o"], name="mm_attn_o")
    b_o = _mm(ssm, W["w_ssm_o"], name="mm_ssm_o")

    def mix(i, n, ga, gs, a, b):
        return (_sigmoid(ga) * a + _sigmoid(gs) * b,)

    (mixed,) = _rows(mix, S, RW, [_row(gates, D_MODEL, 0), _row(gates, D_MODEL, 1), _row(a_o), _row(b_o)],
                     [("row", D_MODEL, BF16)], "mix")
    m2 = _mm(mixed, W["w_out"], name="mm_out")

    def post(i, n, h, m, gpost, gpre):
        hn = h + m * _rstd(m) * gpost
        return hn, hn * _rstd(hn) * gpre

    h1, f_bf = _rows(post, S, RW, [_row(x), _row(m2), _full(sp["mix_norm_post"]), _full(sp["ffn_norm_pre"])],
                     [("row", D_MODEL, F32), ("row", D_MODEL, BF16)], "post_mix")
    ga = _mm(f_bf, W["w_gate"], name="mm_gate")
    up = _mm(f_bf, W["w_up"], name="mm_up")
    (s_bf,) = _rows(lambda i, n, a, b: (a * _sigmoid(a) * b,), S, RW, [_row(ga), _row(up)], [("row", D_FF, BF16)], "swiglu")
    f2 = _mm(s_bf, W["w_down"], name="mm_down")
    h2, n3_bf = _rows(post, S, RW, [_row(h1), _row(f2), _full(sp["ffn_norm_post"]), _full(sp["ple_norm_pre"])],
                      [("row", D_MODEL, F32), ("row", D_MODEL, BF16)], "post_ffn")
    gpre = _mm(n3_bf, W["w_ple_gate"], name="mm_ple_gate")
    pe = _mm(p_bf, W["w_ple"], name="mm_ple")

    def ple_loss(i, n, h2, gpre, pe, tgt, gpost):
        gate = _sigmoid(gpre)
        e = pe * gate
        r = _rstd(e)
        diff = h2 + e * r * gpost - tgt
        loss = 0.5 * jnp.sum(jnp.mean(diff * diff, axis=1, keepdims=True))
        dh3 = diff * (1.0 / D_MODEL)
        de, dg_rows = _norm_bwd(e, r, gpost, dh3)
        return (jnp.full((1, LANE), loss, F32), dh3, de * gate, de * pe * gate * (1.0 - gate), _colsum(dg_rows))

    loss, dh3, dpe, dgpre, g_ple_post = _rows(
        ple_loss, S, 128, [_row(h2), _row(gpre), _row(pe), _row(target), _full(sp["ple_norm_post"])],
        [("acc", (1, LANE), F32), ("row", D_MODEL, F32), ("row", D_MODEL, BF16), ("row", D_MODEL, BF16),
         ("acc", (1, D_MODEL), F32)], "ple_loss")

    gw = {}
    gs = {"ple_norm_post": g_ple_post}
    gw["w_ple"] = _mm(p_bf, dpe, ta=True, name="mmg_ple")
    gw["w_ple_gate"] = _mm(n3_bf, dgpre, ta=True, name="mmg_ple_gate")
    dn3 = _mm(dgpre, W["w_ple_gate"], tb=True, name="mmb_ple_gate")

    def post_bwd(i, n, h, m, dhn, dn, gpost, gpre):
        rm = _rstd(m)
        hn = h + m * rm * gpost
        dx, dgpre_rows = _norm_bwd(hn, _rstd(hn), gpre, dn)
        dhn_t = dhn + dx
        dm, dgpost_rows = _norm_bwd(m, rm, gpost, dhn_t)
        return dhn_t, dm, _colsum(dgpre_rows), _colsum(dgpost_rows)

    def run_post_bwd(h, m, dhn, dn, gpost, gpre, name):
        return _rows(post_bwd, S, 128, [_row(h), _row(m), _row(dhn), _row(dn), _full(gpost), _full(gpre)],
                     [("row", D_MODEL, F32), ("row", D_MODEL, BF16), ("acc", (1, D_MODEL), F32), ("acc", (1, D_MODEL), F32)], name)

    dh2, df2, gs["ple_norm_pre"], gs["ffn_norm_post"] = run_post_bwd(
        h1, f2, dh3, dn3, sp["ffn_norm_post"], sp["ple_norm_pre"], "post_ffn_bwd")
    gw["w_down"] = _mm(s_bf, df2, ta=True, name="mmg_down")
    ds = _mm(df2, W["w_down"], tb=True, name="mmb_down")

    def swiglu_bwd(i, n, a, b, ds):
        sa = _sigmoid(a)
        return ds * b * (sa * (1.0 + a * (1.0 - sa))), ds * (a * sa)

    dga, dup = _rows(swiglu_bwd, S, RW, [_row(ga), _row(up), _row(ds)], [("row", D_FF, BF16), ("row", D_FF, BF16)], "swiglu_bwd")
    gw["w_gate"] = _mm(f_bf, dga, ta=True, name="mmg_gate")
    gw["w_up"] = _mm(f_bf, dup, ta=True, name="mmg_up")
    df = _mm(dga, W["w_gate"], tb=True, name="mmb_gate")
    df = _mm(dup, W["w_up"], tb=True, add=df, name="mmb_up")
    dh1, dm2, gs["ffn_norm_pre"], gs["mix_norm_post"] = run_post_bwd(
        x, m2, dh2, df, sp["mix_norm_post"], sp["ffn_norm_pre"], "post_mix_bwd")
    gw["w_out"] = _mm(mixed, dm2, ta=True, name="mmg_out")
    dmixed = _mm(dm2, W["w_out"], tb=True, name="mmb_out")

    def mix_bwd(i, n, ga, gs_, a, b, dm):
        sa, ss = _sigmoid(ga), _sigmoid(gs_)
        return dm * sa, dm * ss, jnp.concatenate([dm * a * sa * (1.0 - sa), dm * b * ss * (1.0 - ss)], axis=1)

    da_o, db_o, dgates = _rows(mix_bwd, S, RW, [_row(gates, D_MODEL, 0), _row(gates, D_MODEL, 1), _row(a_o), _row(b_o), _row(dmixed)],
                               [("row", D_MODEL, BF16), ("row", D_MODEL, BF16), ("row", 2 * D_MODEL, BF16)], "mix_bwd")
    gw["w_attn_o"] = _mm(attn, da_o, ta=True, name="mmg_attn_o")
    dattn = _mm(da_o, W["w_attn_o"], tb=True, name="mmb_attn_o")
    gw["w_ssm_o"] = _mm(ssm, db_o, ta=True, name="mmg_ssm_o")
    dssm = _mm(db_o, W["w_ssm_o"], tb=True, name="mmb_ssm_o")

    delta, dattn_bf = _attn_delta(attn, dattn, tile)
    dq, dkv, dkp = _attn_bwd(q_bf, kv, kp_bf, dattn_bf, lse, delta, tile)
    dq_raw, dkr = _rope_bwd(dq, dkp, pos_col, invf)
    g_uq_p = _mm(cqn, dq_raw, ta=True, name="mmg_uq")
    gw["w_uq"] = g_uq_p.reshape(Q_LORA, N_HEADS_MLA, HEAD_PAD)[:, :, :QK_DIM].reshape(Q_LORA, N_HEADS_MLA * QK_DIM)
    dcqn = _mm(dq_raw, w_uq_p, tb=True, name="mmb_uq")
    gw["w_ukv"] = _mm(ckvn, dkv, ta=True, name="mmg_ukv")
    dckvn = _mm(dkv, W["w_ukv"], tb=True, name="mmb_ukv")

    def qkv_norm_bwd(i, n, cq, ckv, dq_, dkv_, gq, gkv):
        dcq, gq_rows = _norm_bwd(cq, _rstd(cq), gq, dq_)
        dckv, gkv_rows = _norm_bwd(ckv, _rstd(ckv), gkv, dkv_)
        return jnp.concatenate([dcq, dckv], axis=1), _colsum(gq_rows), _colsum(gkv_rows)

    dcqkv, gs["q_norm"], gs["kv_norm"] = _rows(
        qkv_norm_bwd, S, 512, [_row(cqkv, Q_LORA, 0), _row(cqkv, KV_LORA, 1), _row(dcqn), _row(dckvn), _full(sp["q_norm"]), _full(sp["kv_norm"])],
        [("row", Q_LORA + KV_LORA, BF16), ("acc", (1, Q_LORA), F32), ("acc", (1, KV_LORA), F32)], "qkv_norm_bwd")

    def gated_norm_bwd(i, n, y, xs, z, dssm, dsk, gn):
        yt, sz, yg = gated(y, xs, z, dsk)
        dyg_parts, gn_parts = [], []
        for g in range(SSM_GROUPS):
            sl = slice(g * GN, (g + 1) * GN)
            blk = yg[:, sl]
            dblk, rows = _norm_bwd(blk, _rstd(blk), gn[:, sl], dssm[:, sl])
            dyg_parts.append(dblk)
            gn_parts.append(_colsum(rows))
        dyg = jnp.concatenate(dyg_parts, axis=1)
        dyt = dyg * (z * sz)
        dz = dyg * yt * (sz * (1.0 + z * (1.0 - sz)))
        return dyt, dz, dyt * dsk, jnp.concatenate(gn_parts, axis=1), _colsum(dyt * xs)

    dy, dz, dx_skip, gs["ssm_norm"], g_dskip_ch = _rows(
        gated_norm_bwd, S, 128, [_row(y), _row(xbc_c, D_INNER, 0), _row(z), _row(dssm), _full(dskip_ch), _full(sp["ssm_norm"])],
        [("row", D_INNER, F32), ("row", D_INNER, BF16), ("row", D_INNER, F32), ("acc", (1, D_INNER), F32), ("acc", (1, D_INNER), F32)],
        "gated_norm_bwd")
    gs["d_skip"] = jnp.sum(g_dskip_ch.reshape(N_HEADS_SSM, SSM_HEADDIM), axis=1).reshape(1, N_HEADS_SSM)
    dxs, dbm, dcm, ddt_x, dcum = _ssd_bwd2(xbc_c, dt, cum, cumt_g, spread, states, dy, dx_skip)
    ddt_raw, g_dtb, g_alog = _dt_bwd2(dt_raw, dt_bias_p, a_log_p, ddt_x, dcum)
    gs["dt_bias"] = g_dtb[:, :N_HEADS_SSM]
    gs["a_log"] = g_alog[:, :N_HEADS_SSM]
    dxbc_c = jnp.concatenate([dxs, dbm, dcm], axis=1)
    dxbc, g_conv_w8, gs["conv_b"] = _conv_bwd(xbc, dxbc_c, sp["conv_w"], sp["conv_b"])
    gs["conv_w"] = g_conv_w8[:CONV_WIDTH]

    g_qkv = _mm(u_bf, dcqkv, ta=True, name="mmg_qkv")
    g_kr = _mm(u_bf, dkr, ta=True, name="mmg_kr")
    g_z = _mm(u_bf, dz, ta=True, name="mmg_z")
    g_xbc = _mm(u_bf, dxbc, ta=True, name="mmg_xbc")
    g_dt = _mm(u_bf, ddt_raw, ta=True, name="mmg_dt")
    g_g = _mm(u_bf, dgates, ta=True, name="mmg_gates")
    gw["w_in"] = jnp.concatenate([g_qkv, g_kr[:, :QK_ROPE], g_z, g_xbc, g_dt[:, :N_HEADS_SSM], g_g], axis=1)
    du = _mm(dcqkv, wp["qkv"], tb=True, name="mmb_qkv")
    du = _mm(dkr, wp["kr"], tb=True, add=du, name="mmb_kr")
    du = _mm(dz, wp["z"], tb=True, add=du, name="mmb_z")
    du = _mm(dxbc, wp["xbc"], tb=True, add=du, name="mmb_xbc")
    du = _mm(ddt_raw, wp["dt"], tb=True, add=du, name="mmb_dt")
    du = _mm(dgates, wp["g"], tb=True, add=du, name="mmb_gates")

    def pre_bwd(i, n, x, du, dh, g):
        dx, rows = _norm_bwd(x, _rstd(x), g, du)
        return dh + dx, _colsum(rows)

    grad_x, gs["mix_norm_pre"] = _rows(pre_bwd, S, RW, [_row(x), _row(du), _row(dh1), _full(sp["mix_norm_pre"])],
                                       [("row", D_MODEL, F32), ("acc", (1, D_MODEL), F32)], "norm_pre_bwd")
    return loss, grad_x, gw, gs


BIG = (
    ("w_in", (2048, 3872), 1), ("w_uq", (512, 768), 1), ("w_ukv", (512, 1024), 1), ("w_attn_o", (512, 2048), 0),
    ("w_ssm_o", (1024, 2048), 0), ("w_out", (512, 2048), 0), ("w_gate", (2048, 1408), 1), ("w_up", (2048, 1408), 1),
    ("w_down", (1408, 2048), 0), ("w_ple_gate", (512, 2048), 0), ("w_ple", (256, 512), 1),
)
SMALL = (
    ("mix_norm_pre", 2048), ("mix_norm_post", 2048), ("q_norm", 512), ("kv_norm", 512), ("conv_b", 6144), ("dt_bias", 64),
    ("a_log", 64), ("d_skip", 64), ("ssm_norm", 4096), ("ffn_norm_pre", 2048), ("ffn_norm_post", 2048),
    ("ple_norm_pre", 2048), ("ple_norm_post", 2048),
)
CONV_W_LEN = CONV_WIDTH * CONV_DIM
SMALL_ROWS = 384


def _place():
    return lax.axis_index("x"), lax.axis_index("y"), lax.axis_index("c")


def _flip(v, bit):
    return 1 - v if bit else v


def _start_then_wait(copies):
    for cp in copies:
        cp.start()
    for cp in copies:
        cp.wait()


def _hbm_call(body, name, ins, out_shapes, sems, aliases=None):
    return pl.pallas_call(
        body, name=name, out_shape=tuple(out_shapes),
        in_specs=[pl.BlockSpec(memory_space=pl.ANY)] * len(ins),
        out_specs=tuple(pl.BlockSpec(memory_space=pl.ANY) for _ in out_shapes),
        scratch_shapes=[pltpu.SemaphoreType.DMA((s,)) for s in sems],
        input_output_aliases=aliases or {},
    )(*ins)


def _gather_chips(shards):
    n = len(shards)

    def body(*refs):
        ins, outs = refs[:n], refs[n:2 * n]
        send_sems, recv_sems, fwd_send_sems, fwd_recv_sems = refs[2 * n:]
        x, y, c = _place()
        me = 2 * x + y
        far, near = [], []
        for a in range(n):
            half = shards[a].shape[0] // 2
            lo = pl.multiple_of(c * half, SUBLANE)
            for k in (1, 2, 3):
                px, py = _flip(x, k >> 1), _flip(y, k & 1)
                far.append(pltpu.make_async_remote_copy(
                    src_ref=ins[a].at[pl.ds(lo, half), :], dst_ref=outs[a].at[me, pl.ds(lo, half), :],
                    send_sem=send_sems.at[3 * a + k - 1], recv_sem=recv_sems.at[3 * a + k - 1],
                    device_id=(px, py, c), device_id_type=MESH_ID))
                got = outs[a].at[2 * px + py, pl.ds(lo, half), :]
                near.append(pltpu.make_async_remote_copy(
                    src_ref=got, dst_ref=got, send_sem=fwd_send_sems.at[3 * a + k - 1], recv_sem=fwd_recv_sems.at[3 * a + k - 1],
                    device_id=(x, y, 1 - c), device_id_type=MESH_ID))
        for cp in far:
            cp.start()
        for cp, fwd in zip(far, near):
            cp.wait_recv()
            fwd.start()
        for cp, fwd in zip(far, near):
            cp.wait_send()
            fwd.wait()

    return _hbm_call(body, "gather_chips", shards,
                     [jax.ShapeDtypeStruct((N_CHIPS, *s.shape), s.dtype) for s in shards], (3 * n,) * 4)


def _swap_halves(gs):
    n = len(gs)

    def body(*refs):
        ins, outs = refs[:n], refs[n:2 * n]
        send_sems, recv_sems = refs[2 * n:]
        x, y, c = _place()
        copies = []
        for a in range(n):
            half = gs[a].shape[1] // 2
            src = ins[a].at[:, pl.ds(pl.multiple_of((1 - c) * half, SUBLANE), half), :]
            copies.append(pltpu.make_async_remote_copy(
                src_ref=src, dst_ref=outs[a], send_sem=send_sems.at[a], recv_sem=recv_sems.at[a],
                device_id=(x, y, 1 - c), device_id_type=MESH_ID))
        _start_then_wait(copies)

    return _hbm_call(body, "swap_halves", gs,
                     [jax.ShapeDtypeStruct((g.shape[0], g.shape[1] // 2, g.shape[2]), g.dtype) for g in gs], (n, n))


def _sum_rows_tile(rows, cols):
    return _tile(rows, max(2 * SUBLANE, (512 * 1024 // cols) // (2 * SUBLANE) * (2 * SUBLANE)), 2 * SUBLANE)


def _add_half(g, other, c, name):
    n, R, C = g.shape
    half = R // 2
    tr = _sum_rows_tile(half, C)
    nb = half // tr

    def body(c_ref, g_ref, o_ref, out_ref):
        out_ref[...] = (g_ref[...] + o_ref[...]).astype(out_ref.dtype)

    return pl.pallas_call(
        body, name=name,
        out_shape=jax.ShapeDtypeStruct((n, half, C), BF16),
        grid_spec=pltpu.PrefetchScalarGridSpec(
            num_scalar_prefetch=1, grid=(n, nb),
            in_specs=[pl.BlockSpec((1, tr, C), lambda j, i, c_ref: (j, c_ref[0] * nb + i, 0)),
                      pl.BlockSpec((1, tr, C), lambda j, i, c_ref: (j, i, 0))],
            out_specs=pl.BlockSpec((1, tr, C), lambda j, i, c_ref: (j, i, 0))),
        compiler_params=_params(("parallel", "parallel")),
    )(c, g, other)


def _scatter_chips(parts):
    n = len(parts)

    def body(*refs):
        ins, outs = refs[:n], refs[n:2 * n]
        send_sems, recv_sems = refs[2 * n:]
        x, y, c = _place()
        copies = []
        for a in range(n):
            for k in (1, 2, 3):
                px, py = _flip(x, k >> 1), _flip(y, k & 1)
                copies.append(pltpu.make_async_remote_copy(
                    src_ref=ins[a].at[2 * px + py], dst_ref=outs[a].at[k - 1], send_sem=send_sems.at[3 * a + k - 1],
                    recv_sem=recv_sems.at[3 * a + k - 1], device_id=(px, py, c), device_id_type=MESH_ID))
        _start_then_wait(copies)

    return _hbm_call(body, "scatter_chips", parts,
                     [jax.ShapeDtypeStruct((3, *p.shape[1:]), p.dtype) for p in parts], (3 * n, 3 * n))


def _add_chips(part, got, place, name):
    n, R, C = part.shape
    tr = _sum_rows_tile(R, C)
    nb = R // tr

    def body(place_ref, p_ref, g_ref, out_ref):
        out_ref[...] = ((p_ref[0].astype(F32) + g_ref[0].astype(F32)) + g_ref[1].astype(F32)) + g_ref[2].astype(F32)

    return pl.pallas_call(
        body, name=name,
        out_shape=jax.ShapeDtypeStruct((2 * R, C), F32),
        grid_spec=pltpu.PrefetchScalarGridSpec(
            num_scalar_prefetch=1, grid=(nb,),
            in_specs=[pl.BlockSpec((1, tr, C), lambda i, place_ref: (place_ref[0], i, 0)),
                      pl.BlockSpec((3, tr, C), lambda i, place_ref: (0, i, 0))],
            out_specs=pl.BlockSpec((tr, C), lambda i, place_ref: (place_ref[1] * nb + i, 0))),
        compiler_params=_params(("parallel",)),
    )(place, part, got)


def _join_halves(wholes):
    n = len(wholes)

    def body(*refs):
        outs = refs[n:2 * n]
        send_sems, recv_sems = refs[2 * n:]
        x, y, c = _place()
        copies = []
        for a in range(n):
            half = wholes[a].shape[0] // 2
            rows = outs[a].at[pl.ds(pl.multiple_of(c * half, SUBLANE), half), :]
            copies.append(pltpu.make_async_remote_copy(
                src_ref=rows, dst_ref=rows, send_sem=send_sems.at[a], recv_sem=recv_sems.at[a],
                device_id=(x, y, 1 - c), device_id_type=MESH_ID))
        _start_then_wait(copies)

    return _hbm_call(body, "join_halves", wholes, [jax.ShapeDtypeStruct(w.shape, w.dtype) for w in wholes], (n, n),
                     aliases={a: a for a in range(n)})


def _allreduce_small(vec, name):
    R, C = vec.shape

    def body(v_ref, o_ref, buf, send_sems, recv_sems):
        x, y, c = _place()
        me = 4 * x + 2 * y + c
        buf[me] = v_ref[...]
        copies = []
        for k in range(1, N_DEV):
            peer = (_flip(x, (k >> 2) & 1), _flip(y, (k >> 1) & 1), _flip(c, k & 1))
            copies.append(pltpu.make_async_remote_copy(
                src_ref=v_ref, dst_ref=buf.at[me], send_sem=send_sems.at[k - 1], recv_sem=recv_sems.at[k - 1],
                device_id=peer, device_id_type=MESH_ID))
        for cp in copies:
            cp.start()
        for cp in copies:
            cp.wait()
        tot = buf[0]
        for d in range(1, N_DEV):
            tot = tot + buf[d]
        o_ref[...] = tot

    return pl.pallas_call(
        body, name=name,
        out_shape=jax.ShapeDtypeStruct((R, C), F32),
        in_specs=[pl.BlockSpec(memory_space=pltpu.VMEM)],
        out_specs=pl.BlockSpec(memory_space=pltpu.VMEM),
        scratch_shapes=[pltpu.VMEM((N_DEV, R, C), F32), pltpu.SemaphoreType.DMA((N_DEV - 1,)), pltpu.SemaphoreType.DMA((N_DEV - 1,))],
    )(vec)


def _unstack(gathered, shape, axis):
    if axis == 0:
        return gathered.reshape(N_CHIPS * shape[0], shape[1])
    return jnp.concatenate([gathered[j] for j in range(N_CHIPS)], axis=1)


def _stack(whole, shape, axis):
    if axis == 0:
        return whole.reshape(N_CHIPS, shape[0], shape[1])
    return jnp.stack([whole[:, j * shape[1]:(j + 1) * shape[1]] for j in range(N_CHIPS)])


def kernel(x, p, positions, mix_norm_pre, mix_norm_post, w_in, q_norm, w_uq, kv_norm, w_ukv, conv_w, conv_b, dt_bias, a_log, d_skip, ssm_norm, w_attn_o, w_ssm_o, w_out, ffn_norm_pre, ffn_norm_post, w_gate, w_up, w_down, ple_norm_pre, ple_norm_post, w_ple_gate, w_ple, loss_target, m_mix_norm_pre, m_mix_norm_post, m_w_in, m_q_norm, m_w_uq, m_kv_norm, m_w_ukv, m_conv_w, m_conv_b, m_dt_bias, m_a_log, m_d_skip, m_ssm_norm, m_w_attn_o, m_w_ssm_o, m_w_out, m_ffn_norm_pre, m_ffn_norm_post, m_w_gate, m_w_up, m_w_down, m_ple_norm_pre, m_ple_norm_post, m_w_ple_gate, m_w_ple, v_mix_norm_pre, v_mix_norm_post, v_w_in, v_q_norm, v_w_uq, v_kv_norm, v_w_ukv, v_conv_w, v_conv_b, v_dt_bias, v_a_log, v_d_skip, v_ssm_norm, v_w_attn_o, v_w_ssm_o, v_w_out, v_ffn_norm_pre, v_ffn_norm_post, v_w_gate, v_w_up, v_w_down, v_ple_norm_pre, v_ple_norm_post, v_w_ple_gate, v_w_ple):
    given = dict(locals())
    names = [n for n, _, _ in BIG] + [n for n, _ in SMALL] + ["conv_w"]
    order = ["mix_norm_pre", "mix_norm_post", "w_in", "q_norm", "w_uq", "kv_norm", "w_ukv", "conv_w", "conv_b", "dt_bias", "a_log",
             "d_skip", "ssm_norm", "w_attn_o", "w_ssm_o", "w_out", "ffn_norm_pre", "ffn_norm_post", "w_gate", "w_up", "w_down",
             "ple_norm_pre", "ple_norm_post", "w_ple_gate", "w_ple"]
    assert sorted(names) == sorted(order)
    cx, cy, cc = _place()
    chip = 2 * cx + cy
    conv_cols = CONV_DIM // N_CHIPS

    shards = [given[n][0].astype(BF16) for n, _, _ in BIG]
    gathered = [lax.dynamic_update_slice(g, s[None], (chip, 0, 0)) for g, s in zip(_gather_chips(shards), shards)]
    W = {n: _unstack(g, shape, axis) for (n, shape, axis), g in zip(BIG, gathered)}
    own = jnp.where(cc == 0, conv_w[0], 0.0)
    conv_vec = lax.dynamic_update_slice(jnp.zeros((CONV_WIDTH, CONV_DIM), F32), own, (0, chip * conv_cols))
    conv_full = _allreduce_small(conv_vec.reshape(CONV_W_LEN // LANE, LANE), "gather_conv_w").reshape(CONV_WIDTH, CONV_DIM)
    sp = {n: given[n] for n, _ in SMALL}
    sp["conv_w"] = conv_full

    loss_part, grad_x, gw, gs = _local_step(x[0], p[0, 0], positions[0], W, sp, loss_target[0])

    stacked = [_stack(gw[n], shape, axis) for n, shape, axis in BIG]
    c_arr = cc.reshape(1).astype(jnp.int32)
    place_arr = jnp.stack([chip, cc]).astype(jnp.int32)
    parts = [_add_half(g, o, c_arr, "add_half_" + n) for (n, _, _), g, o in zip(BIG, stacked, _swap_halves(stacked))]
    wholes = [_add_chips(q, o, place_arr, "add_chips_" + n) for (n, _, _), q, o in zip(BIG, parts, _scatter_chips(parts))]
    g_big = {n: r.reshape(1, *shape) for (n, shape, _), r in zip(BIG, _join_halves(wholes))}

    small_parts = [gs[n] for n, _ in SMALL] + [gs["conv_w"], loss_part[:, :1]]
    small_vec = jnp.concatenate([t.reshape(-1) for t in small_parts])
    small_vec = jnp.pad(small_vec, (0, SMALL_ROWS * LANE - small_vec.shape[0])).reshape(SMALL_ROWS, LANE)
    small_sum = _allreduce_small(small_vec, "allreduce_small").reshape(-1)
    g_small, off = {}, 0
    for n, length in SMALL:
        g_small[n] = small_sum[off:off + length].reshape(1, length)
        off += length
    g_conv = small_sum[off:off + CONV_W_LEN].reshape(CONV_WIDTH, CONV_DIM)
    g_small["conv_w"] = lax.dynamic_slice(g_conv, (0, chip * conv_cols), (CONV_WIDTH, conv_cols)).reshape(1, CONV_WIDTH, conv_cols)
    loss = small_sum[off + CONV_W_LEN]

    grads, deltas, new_m, new_v = [], [], [], []
    for n in order:
        g = g_big[n] if n in g_big else g_small[n]
        d, m_, v_ = _adamw(given[n], g, given["m_" + n], given["v_" + n], "adamw_" + n)
        grads.append(g)
        deltas.append(d)
        new_m.append(m_)
        new_v.append(v_)
    return (loss, grad_x.reshape(x.shape), *grads, *deltas, *new_m, *new_v)
```

```python
import functools
import math

import numpy as np
import jax
import jax.numpy as jnp
from jax import lax
from jax.experimental import pallas as pl
from jax.experimental.pallas import tpu as pltpu

F32 = jnp.float32
BF16 = jnp.bfloat16

D_MODEL = 2048
N_HEADS_MLA = 16
Q_LORA = 512
KV_LORA = 512
QK_NOPE = 128
QK_ROPE = 64
V_DIM = 128
QK_DIM = QK_NOPE + QK_ROPE
ROPE_THETA = 10000.0
D_INNER = 4096
SSM_HEADDIM = 64
N_HEADS_SSM = 64
SSM_GROUPS = 8
HEADS_PER_GROUP = 8
D_STATE = 128
CONV_WIDTH = 4
CHUNK = 256
CONV_DIM = D_INNER + 2 * SSM_GROUPS * D_STATE
D_FF = 5632
PLE_DIM = 256
EPS = 1e-6
IN_SPLITS = (Q_LORA, KV_LORA, QK_ROPE, D_INNER, CONV_DIM, N_HEADS_SSM, D_MODEL, D_MODEL)

ADAM_LR = 0.001
ADAM_B1 = 0.9
ADAM_B2 = 0.999
ADAM_EPS = 1e-08
ADAM_WD = 0.01
ADAM_STEP = 10

LANE = 128
SUBLANE = 8
HEAD_PAD = 256
VMEM_LIMIT = 56 * 1024 * 1024
ATTN_TILE = 512
NEG = -1e30

MESH_ID = pl.DeviceIdType.MESH
N_CHIPS = 4
N_DEV = 8


def _tile(n, pref, mult=LANE):
    if n <= pref:
        return n
    t = (pref // mult) * mult
    while t >= mult:
        if n % t == 0:
            return t
        t -= mult
    return n


def _params(sem, vmem=VMEM_LIMIT):
    return pltpu.CompilerParams(dimension_semantics=sem, vmem_limit_bytes=vmem)


class _Hook:
    def __init__(self, ins, out_shapes, sems, make, aliases=None):
        self.ins, self.out_shapes, self.sems, self.make, self.aliases = list(ins), list(out_shapes), tuple(sems), make, dict(aliases or {})


def _merge_hooks(hooks):
    hooks = [h for h in hooks if h is not None]
    if not hooks:
        return None
    ins, outs, sems, aliases, cuts = [], [], [], {}, []
    for h in hooks:
        cuts.append((len(ins), len(outs), len(sems)))
        aliases.update({len(ins) + i: len(outs) + o for i, o in h.aliases.items()})
        ins += h.ins
        outs += h.out_shapes
        sems += h.sems

    def make(in_refs, out_refs, sem_refs):
        pairs = []
        for h, (i0, o0, s0) in zip(hooks, cuts):
            pairs.append(h.make(in_refs[i0:i0 + len(h.ins)], out_refs[o0:o0 + len(h.out_shapes)], sem_refs[s0:s0 + len(h.sems)]))

        def start():
            for st, _ in pairs:
                st()

        def finish():
            for _, fin in pairs:
                fin()

        return start, finish

    return _Hook(ins, outs, sems, make, aliases)


def _mm(a, b, *, ta=False, tb=False, add=None, out_dtype=F32, name, tm=1024, tn=1024, tk=2048, hook=None):
    if ta:
        K, M = a.shape
    else:
        M, K = a.shape
    N = b.shape[0] if tb else b.shape[1]
    assert (b.shape[1] if tb else b.shape[0]) == K, (a.shape, b.shape, ta, tb)
    tm, tn, tk = _tile(M, tm), _tile(N, tn), _tile(K, tk)
    nk = K // tk
    dn = (((0 if ta else 1,), (1 if tb else 0,)), ((), ()))
    has_add = add is not None
    n_own = 3 if has_add else 2
    n_hin = len(hook.ins) if hook else 0
    n_hout = len(hook.out_shapes) if hook else 0
    grid = (M // tm, N // tn, nk)

    def body(*refs):
        a_ref, b_ref = refs[:2]
        c_ref = refs[2] if has_add else None
        o_ref = refs[n_own + n_hin]
        scratch = refs[n_own + n_hin + 1 + n_hout:]
        if hook:
            start, finish = hook.make(refs[n_own:n_own + n_hin], refs[n_own + n_hin + 1:n_own + n_hin + 1 + n_hout],
                                      scratch[len(scratch) - len(hook.sems):])
            ids = [pl.program_id(d) for d in range(3)]
            pl.when((ids[0] == 0) & (ids[1] == 0) & (ids[2] == 0))(start)
        prod = lax.dot_general(a_ref[...].astype(BF16), b_ref[...].astype(BF16), dn, preferred_element_type=F32)
        if nk == 1:
            o_ref[...] = ((c_ref[...] + prod) if has_add else prod).astype(out_dtype)
        else:
            acc = scratch[0]
            k = pl.program_id(2)

            @pl.when(k == 0)
            def _():
                acc[...] = (c_ref[...] + prod) if has_add else prod

            @pl.when(k > 0)
            def _():
                acc[...] += prod

            @pl.when(k == nk - 1)
            def _():
                o_ref[...] = acc[...].astype(out_dtype)
        if hook:
            pl.when((ids[0] == grid[0] - 1) & (ids[1] == grid[1] - 1) & (ids[2] == grid[2] - 1))(finish)

    a_spec = pl.BlockSpec((tk, tm), lambda i, j, k: (k, i)) if ta else pl.BlockSpec((tm, tk), lambda i, j, k: (i, k))
    b_spec = pl.BlockSpec((tn, tk), lambda i, j, k: (j, k)) if tb else pl.BlockSpec((tk, tn), lambda i, j, k: (k, j))
    in_specs = [a_spec, b_spec]
    args = [a, b]
    if has_add:
        in_specs.append(pl.BlockSpec((tm, tn), lambda i, j, k: (i, j)))
        args.append(add)
    hbm = pl.BlockSpec(memory_space=pl.ANY)
    scratch_shapes = [pltpu.VMEM((tm, tn), F32)] if nk > 1 else []
    out_shape = jax.ShapeDtypeStruct((M, N), out_dtype)
    out_spec = pl.BlockSpec((tm, tn), lambda i, j, k: (i, j))
    if not hook:
        return pl.pallas_call(
            body, name=name, out_shape=out_shape, grid=grid, in_specs=in_specs, out_specs=out_spec,
            scratch_shapes=scratch_shapes, compiler_params=_params(("parallel", "parallel", "arbitrary")),
        )(*args)
    outs = pl.pallas_call(
        body, name=name, out_shape=(out_shape, *hook.out_shapes), grid=grid,
        in_specs=in_specs + [hbm] * n_hin, out_specs=(out_spec, *[hbm] * n_hout),
        scratch_shapes=scratch_shapes + [pltpu.SemaphoreType.DMA((s,)) for s in hook.sems],
        input_output_aliases={n_own + i: 1 + o for i, o in hook.aliases.items()},
        compiler_params=_params(("arbitrary", "arbitrary", "arbitrary")),
    )(*args, *hook.ins)
    return outs[0], list(outs[1:])


def _row(arr, width=None, cblk=0):
    return ("row", arr, arr.shape[1] if width is None else width, cblk)


def _full(arr):
    return ("full", arr)


def _prev8(arr):
    return ("prev8", arr)


def _next8(arr):
    return ("next8", arr)


def _rows(fn, n_rows, tm, ins, outs, name):
    tm = min(tm, n_rows)
    assert n_rows % tm == 0 and tm % SUBLANE == 0
    n = n_rows // tm
    in_specs, args = [], []
    for spec in ins:
        kind, arr = spec[0], spec[1]
        if kind == "row":
            _, _, w, cb = spec
            in_specs.append(pl.BlockSpec((tm, w), lambda i, cb=cb: (i, cb)))
        elif kind == "full":
            in_specs.append(pl.BlockSpec(arr.shape, lambda i, nd=arr.ndim: (0,) * nd))
        elif kind == "prev8":
            in_specs.append(pl.BlockSpec((SUBLANE, arr.shape[1]),
                                         lambda i: (jnp.maximum(i * (tm // SUBLANE) - 1, 0), 0)))
        elif kind == "next8":
            last = n_rows // SUBLANE - 1
            in_specs.append(pl.BlockSpec((SUBLANE, arr.shape[1]),
                                         lambda i: (jnp.minimum((i + 1) * (tm // SUBLANE), last), 0)))
        else:
            raise ValueError(kind)
        args.append(arr)
    out_shapes, out_specs = [], []
    any_acc = False
    for spec in outs:
        if spec[0] == "row":
            _, w, dt = spec
            out_shapes.append(jax.ShapeDtypeStruct((n_rows, w), dt))
            out_specs.append(pl.BlockSpec((tm, w), lambda i: (i, 0)))
        else:
            _, shp, dt = spec
            any_acc = True
            out_shapes.append(jax.ShapeDtypeStruct(shp, dt))
            out_specs.append(pl.BlockSpec(shp, lambda i, nd=len(shp): (0,) * nd))
    nin = len(ins)

    def body(*refs):
        i = pl.program_id(0)
        vals = fn(i, n, *[r[...] for r in refs[:nin]])
        for o_ref, spec, v in zip(refs[nin:], outs, vals):
            if spec[0] == "acc":
                @pl.when(i == 0)
                def _(o_ref=o_ref):
                    o_ref[...] = jnp.zeros_like(o_ref)

                o_ref[...] += v.astype(o_ref.dtype)
            else:
                o_ref[...] = v.astype(o_ref.dtype)

    res = pl.pallas_call(
        body, name=name,
        out_shape=tuple(out_shapes),
        grid=(n,),
        in_specs=in_specs,
        out_specs=tuple(out_specs),
        compiler_params=_params(("arbitrary",) if any_acc else ("parallel",)),
    )(*args)
    return res


def _rstd(x):
    return lax.rsqrt(jnp.mean(x * x, axis=-1, keepdims=True) + EPS)


def _norm_bwd(x, r, g, dy):
    xh = x * r
    dyg = dy * g
    dx = r * (dyg - xh * jnp.mean(dyg * xh, axis=-1, keepdims=True))
    return dx, dy * xh


def _sigmoid(x):
    return 1.0 / (1.0 + jnp.exp(-x))


def _colsum(v):
    return jnp.sum(v, axis=0, keepdims=True)


def _rope_tables(pos, invf):
    ang = pos.astype(F32) * invf
    lane = lax.broadcasted_iota(jnp.int32, ang.shape, 1)
    cos, sin = jnp.cos(ang), jnp.sin(ang)
    ct = jnp.where(lane < QK_ROPE, cos, 0.0)
    sa = jnp.where(lane < QK_ROPE // 2, -sin, 0.0)
    sb = jnp.where((lane >= QK_ROPE // 2) & (lane < QK_ROPE), sin, 0.0)
    return ct, sa, sb


def _rope(b, ct, sa, sb):
    return ct * b + sa * pltpu.roll(b, LANE - QK_ROPE // 2, 1) + sb * pltpu.roll(b, QK_ROPE // 2, 1)


def _rope_t(d, ct, sa, sb):
    return ct * d + pltpu.roll(sa * d, QK_ROPE // 2, 1) + pltpu.roll(sb * d, LANE - QK_ROPE // 2, 1)


def _rope_fwd(q_raw, kr_pad, pos_col, invf):
    S = q_raw.shape[0]

    def fn(i, n, q, kr, pos, invf):
        ct, sa, sb = _rope_tables(pos, invf)
        parts = []
        for h in range(N_HEADS_MLA):
            parts.append(q[:, h * HEAD_PAD:h * HEAD_PAD + LANE])
            parts.append(_rope(q[:, h * HEAD_PAD + LANE:(h + 1) * HEAD_PAD], ct, sa, sb))
        return jnp.concatenate(parts, axis=1), _rope(kr, ct, sa, sb)

    return _rows(fn, S, 256, [_row(q_raw), _row(kr_pad), _row(pos_col), _full(invf)],
                 [("row", N_HEADS_MLA * HEAD_PAD, BF16), ("row", LANE, BF16)], "rope_fwd")


def _rope_bwd(dq, dkp, pos_col, invf):
    S = dq.shape[0]
    tm = 256

    def body(dq_ref, dkp_ref, pos_ref, invf_ref, dqo_ref, dkr_ref):
        ct, sa, sb = _rope_tables(pos_ref[...], invf_ref[...])
        for h in range(N_HEADS_MLA):
            dqo_ref[:, h * HEAD_PAD:h * HEAD_PAD + LANE] = dq_ref[:, h * HEAD_PAD:h * HEAD_PAD + LANE].astype(BF16)
            dqo_ref[:, h * HEAD_PAD + LANE:(h + 1) * HEAD_PAD] = _rope_t(
                dq_ref[:, h * HEAD_PAD + LANE:(h + 1) * HEAD_PAD], ct, sa, sb).astype(BF16)
        tot = dkp_ref[0]
        for h in range(1, N_HEADS_MLA):
            tot = tot + dkp_ref[h]
        dkr_ref[...] = _rope_t(tot, ct, sa, sb).astype(BF16)

    return pl.pallas_call(
        body, name="rope_bwd",
        out_shape=(jax.ShapeDtypeStruct(dq.shape, BF16), jax.ShapeDtypeStruct((S, LANE), BF16)),
        grid=(S // tm,),
        in_specs=[pl.BlockSpec((tm, dq.shape[1]), lambda i: (i, 0)),
                  pl.BlockSpec((N_HEADS_MLA, tm, LANE), lambda i: (0, i, 0)),
                  pl.BlockSpec((tm, 1), lambda i: (i, 0)),
                  pl.BlockSpec((1, LANE), lambda i: (0, 0))],
        out_specs=(pl.BlockSpec((tm, dq.shape[1]), lambda i: (i, 0)), pl.BlockSpec((tm, LANE), lambda i: (i, 0))),
        compiler_params=_params(("parallel",)),
    )(dq, dkp, pos_col, invf)


def _row_of(col, n):
    eye = lax.broadcasted_iota(jnp.int32, (n, n), 0) == lax.broadcasted_iota(jnp.int32, (n, n), 1)
    return jnp.sum(jnp.where(eye, col, 0.0), axis=0, keepdims=True)


def _attn_fwd(q, kv, kp, tile):
    S = q.shape[0]
    nq = S // tile
    scale = QK_DIM ** -0.5
    nt = (((1,), (1,)), ((), ()))

    def body(q_ref, kv_ref, kp_ref, o_ref, lse_ref, m_s, l_s, acc_s, s_buf):
        qi = pl.program_id(1)
        qv = q_ref[...]
        m_s[...] = jnp.full_like(m_s, NEG)
        l_s[...] = jnp.zeros_like(l_s)
        acc_s[...] = jnp.zeros_like(acc_s)

        def scores(j):
            start = pl.multiple_of(j * tile, tile)
            k = jnp.concatenate([kv_ref[pl.ds(start, tile), 0:LANE], kp_ref[pl.ds(start, tile), :]], axis=1)
            return lax.dot_general(qv, k, nt, preferred_element_type=F32) * scale

        def update(s, j):
            v = kv_ref[pl.ds(pl.multiple_of(j * tile, tile), tile), LANE:2 * LANE]
            m_old = m_s[...]
            m_new = jnp.maximum(m_old, jnp.max(s, axis=1, keepdims=True))
            alpha = jnp.exp(m_old - m_new)
            p = jnp.exp(s - m_new)
            l_s[...] = alpha * l_s[...] + jnp.sum(p, axis=1, keepdims=True)
            acc_s[...] = alpha * acc_s[...] + jnp.dot(p.astype(BF16), v, preferred_element_type=F32)
            m_s[...] = m_new

        s_buf[0] = scores(0)

        def loop_body(j, carry):
            nxt = scores(j + 1)
            update(s_buf[lax.rem(j, 2)], j)
            s_buf[lax.rem(j + 1, 2)] = nxt
            return carry

        lax.fori_loop(0, qi, loop_body, 0)
        s = s_buf[lax.rem(qi, 2)]
        row = lax.broadcasted_iota(jnp.int32, s.shape, 0)
        col = lax.broadcasted_iota(jnp.int32, s.shape, 1)
        update(jnp.where(row >= col, s, NEG), qi)
        l = l_s[...]
        o_ref[...] = (acc_s[...] / l).astype(o_ref.dtype)
        lse_ref[0, 0] = _row_of(m_s[...] + jnp.log(l), tile)

    return pl.pallas_call(
        body, name="attn_fwd",
        out_shape=(jax.ShapeDtypeStruct((S, N_HEADS_MLA * V_DIM), BF16),
                   jax.ShapeDtypeStruct((N_HEADS_MLA, nq, 1, tile), F32)),
        grid=(N_HEADS_MLA, nq),
        in_specs=[pl.BlockSpec((tile, HEAD_PAD), lambda h, i: (i, h)),
                  pl.BlockSpec((S, HEAD_PAD), lambda h, i: (0, h)),
                  pl.BlockSpec((S, LANE), lambda h, i: (0, 0))],
        out_specs=(pl.BlockSpec((tile, V_DIM), lambda h, i: (i, h)),
                   pl.BlockSpec((1, 1, 1, tile), lambda h, i: (h, i, 0, 0))),
        scratch_shapes=[pltpu.VMEM((tile, 1), F32), pltpu.VMEM((tile, 1), F32), pltpu.VMEM((tile, V_DIM), F32),
                        pltpu.VMEM((2, tile, tile), F32)],
        compiler_params=_params(("parallel", "arbitrary")),
    )(q, kv, kp)


def _attn_delta(o, do, tile):
    S = o.shape[0]
    nq = S // tile

    def body(o_ref, do_ref, d_ref, dob_ref):
        dov = do_ref[...]
        prod = o_ref[...].astype(F32) * dov
        dob_ref[...] = dov.astype(BF16)
        for h in range(N_HEADS_MLA):
            col = jnp.sum(prod[:, h * V_DIM:(h + 1) * V_DIM], axis=1, keepdims=True)
            d_ref[h, 0] = _row_of(col, tile)

    return pl.pallas_call(
        body, name="attn_delta",
        out_shape=(jax.ShapeDtypeStruct((N_HEADS_MLA, nq, 1, tile), F32), jax.ShapeDtypeStruct(o.shape, BF16)),
        grid=(nq,),
        in_specs=[pl.BlockSpec((tile, o.shape[1]), lambda i: (i, 0)), pl.BlockSpec((tile, o.shape[1]), lambda i: (i, 0))],
        out_specs=(pl.BlockSpec((N_HEADS_MLA, 1, 1, tile), lambda i: (0, i, 0, 0)),
                   pl.BlockSpec((tile, o.shape[1]), lambda i: (i, 0))),
        compiler_params=_params(("parallel",)),
    )(o, do)


def _attn_bwd(q, kv, kp, do, lse, delta, tile):
    S = q.shape[0]
    nq = S // tile
    scale = QK_DIM ** -0.5
    nt = (((1,), (1,)), ((), ()))
    tn = (((0,), (0,)), ((), ()))

    def body(kv_ref, kp_ref, q_ref, do_ref, lse_ref, d_ref, dq_ref, dkv_ref, dkp_ref, dk_s, dv_s):
        ki = pl.program_id(1)
        k = jnp.concatenate([kv_ref[:, 0:LANE], kp_ref[...]], axis=1)
        v = kv_ref[:, LANE:2 * LANE]

        @pl.when(ki == 0)
        def _():
            dq_ref[...] = jnp.zeros_like(dq_ref)

        dk_s[...] = jnp.zeros_like(dk_s)
        dv_s[...] = jnp.zeros_like(dv_s)

        def step(qi, masked):
            start = pl.multiple_of(qi * tile, tile)
            qv = q_ref[pl.ds(start, tile), :]
            dov = do_ref[pl.ds(start, tile), :]
            st = lax.dot_general(k, qv, nt, preferred_element_type=F32) * scale
            pt = jnp.exp(st - lse_ref[0, qi])
            if masked:
                krow = lax.broadcasted_iota(jnp.int32, pt.shape, 0)
                qcol = lax.broadcasted_iota(jnp.int32, pt.shape, 1)
                pt = jnp.where(krow <= qcol, pt, 0.0)
            dv_s[...] += jnp.dot(pt.astype(BF16), dov, preferred_element_type=F32)
            dpt = lax.dot_general(v, dov, nt, preferred_element_type=F32)
            dst = (pt * (dpt - d_ref[0, qi]) * scale).astype(BF16)
            dk_s[...] += jnp.dot(dst, qv, preferred_element_type=F32)
            dq_ref[pl.ds(start, tile), :] += lax.dot_general(dst, k, tn, preferred_element_type=F32)

        step(ki, True)

        def loop_body(qi, carry):
            step(qi, False)
            return carry

        lax.fori_loop(ki + 1, nq, loop_body, 0)
        dkv_ref[...] = jnp.concatenate([dk_s[:, 0:LANE], dv_s[...]], axis=1).astype(dkv_ref.dtype)
        dkp_ref[0] = dk_s[:, LANE:2 * LANE]

    return pl.pallas_call(
        body, name="attn_bwd",
        out_shape=(jax.ShapeDtypeStruct((S, N_HEADS_MLA * HEAD_PAD), F32),
                   jax.ShapeDtypeStruct((S, N_HEADS_MLA * HEAD_PAD), BF16),
                   jax.ShapeDtypeStruct((N_HEADS_MLA, S, LANE), F32)),
        grid=(N_HEADS_MLA, nq),
        in_specs=[pl.BlockSpec((tile, HEAD_PAD), lambda h, i: (i, h)),
                  pl.BlockSpec((tile, LANE), lambda h, i: (i, 0)),
                  pl.BlockSpec((S, HEAD_PAD), lambda h, i: (0, h)),
                  pl.BlockSpec((S, V_DIM), lambda h, i: (0, h)),
                  pl.BlockSpec((1, nq, 1, tile), lambda h, i: (h, 0, 0, 0)),
                  pl.BlockSpec((1, nq, 1, tile), lambda h, i: (h, 0, 0, 0))],
        out_specs=(pl.BlockSpec((S, HEAD_PAD), lambda h, i: (0, h)),
                   pl.BlockSpec((tile, HEAD_PAD), lambda h, i: (i, h)),
                   pl.BlockSpec((1, tile, LANE), lambda h, i: (h, i, 0))),
        scratch_shapes=[pltpu.VMEM((tile, HEAD_PAD), F32), pltpu.VMEM((tile, V_DIM), F32)],
        compiler_params=_params(("parallel", "arbitrary")),
    )(kv, kp, q, do, lse, delta)


def _shift_down(cur, halo, k):
    sh = pltpu.roll(cur, k, 0)
    hs = pltpu.roll(halo, k, 0)
    rows = lax.broadcasted_iota(jnp.int32, hs.shape, 0)
    first = jnp.where(rows < k, hs, sh[0:SUBLANE])
    if cur.shape[0] == SUBLANE:
        return first
    return jnp.concatenate([first, sh[SUBLANE:]], axis=0)


def _shift_up(cur, nxt, k):
    n = cur.shape[0]
    sh = pltpu.roll(cur, n - k, 0)
    ns = pltpu.roll(nxt, SUBLANE - k, 0)
    rows = lax.broadcasted_iota(jnp.int32, ns.shape, 0)
    last = jnp.where(rows >= SUBLANE - k, ns, sh[n - SUBLANE:])
    if n == SUBLANE:
        return last
    return jnp.concatenate([sh[:n - SUBLANE], last], axis=0)


def _conv_pre(cur, halo, w, b):
    out = b + w[3:4] * cur
    for k in range(1, CONV_WIDTH):
        out = out + w[3 - k:4 - k] * _shift_down(cur, halo, k)
    return out


def _conv_fwd(xbc, w, b):
    S = xbc.shape[0]

    def fn(i, n, cur, prev, w, b):
        halo = jnp.where(i > 0, prev, 0.0)
        pre = _conv_pre(cur, halo, w, b)
        return (pre * _sigmoid(pre),)

    return _rows(fn, S, 256, [_row(xbc), _prev8(xbc), _full(w), _full(b)], [("row", xbc.shape[1], F32)], "conv_fwd")[0]


def _conv_bwd(xbc, dact, w, b):
    S, C = xbc.shape

    def dsilu(pre):
        s = _sigmoid(pre)
        return s * (1.0 + pre * (1.0 - s))

    def fn(i, n, cur, prev, nxt, dcur, dnxt, w, b):
        halo = jnp.where(i > 0, prev, 0.0)
        pre = _conv_pre(cur, halo, w, b)
        dpre = dcur * dsilu(pre)
        pre_n = _conv_pre(nxt, cur[cur.shape[0] - SUBLANE:], w, b)
        dpre_n = jnp.where(i < n - 1, dnxt * dsilu(pre_n), 0.0)
        dx = w[3:4] * dpre
        rows = lax.broadcasted_iota(jnp.int32, (SUBLANE, C), 0)
        dw = jnp.where(rows == 3, _colsum(dpre * cur), 0.0)
        for k in range(1, CONV_WIDTH):
            dx = dx + w[3 - k:4 - k] * _shift_up(dpre, dpre_n, k)
            dw = dw + jnp.where(rows == 3 - k, _colsum(dpre * _shift_down(cur, halo, k)), 0.0)
        return dx, dw, _colsum(dpre)

    return _rows(fn, S, 256, [_row(xbc), _prev8(xbc), _next8(xbc), _row(dact), _next8(dact), _full(w), _full(b)],
                 [("row", C, BF16), ("acc", (SUBLANE, C), F32), ("acc", (1, C), F32)], "conv_bwd")


def _softplus(x):
    return jnp.maximum(x, 0.0) + jnp.log1p(jnp.exp(-jnp.abs(x)))


def _cumsum_rows(x):
    rows = lax.broadcasted_iota(jnp.int32, x.shape, 0)
    s = 1
    while s < x.shape[0]:
        x = x + jnp.where(rows >= s, pltpu.roll(x, s, 0), 0.0)
        s *= 2
    return x


def _revcumsum_rows(x):
    n = x.shape[0]
    rows = lax.broadcasted_iota(jnp.int32, x.shape, 0)
    s = 1
    while s < n:
        x = x + jnp.where(rows < n - s, pltpu.roll(x, n - s, 0), 0.0)
        s *= 2
    return x


def _dt_prep(dt_raw, dt_bias, a_log):
    S = dt_raw.shape[0]

    def body(raw_ref, bias_ref, alog_ref, dt_ref, cum_ref, cumt_ref):
        dt = _softplus(raw_ref[...] + bias_ref[...])
        cum = _cumsum_rows(dt * (-jnp.exp(alog_ref[...])))
        dt_ref[...] = dt
        cum_ref[...] = cum
        cumt_ref[...] = cum.T

    return pl.pallas_call(
        body, name="dt_prep",
        out_shape=(jax.ShapeDtypeStruct((S, LANE), F32), jax.ShapeDtypeStruct((S, LANE), F32),
                   jax.ShapeDtypeStruct((LANE, S), F32)),
        grid=(S // CHUNK,),
        in_specs=[pl.BlockSpec((CHUNK, LANE), lambda i: (i, 0)), pl.BlockSpec((1, LANE), lambda i: (0, 0)),
                  pl.BlockSpec((1, LANE), lambda i: (0, 0))],
        out_specs=(pl.BlockSpec((CHUNK, LANE), lambda i: (i, 0)), pl.BlockSpec((CHUNK, LANE), lambda i: (i, 0)),
                   pl.BlockSpec((LANE, CHUNK), lambda i: (0, i))),
        compiler_params=_params(("parallel",)),
    )(dt_raw, dt_bias, a_log)


def _group_cols(t):
    S = t.shape[0]
    return jnp.transpose(t[:, :N_HEADS_SSM].reshape(S, SSM_GROUPS, HEADS_PER_GROUP), (1, 0, 2))


def _ungroup_cols(t):
    S = t.shape[1]
    flat = jnp.transpose(t, (1, 0, 2)).reshape(S, N_HEADS_SSM)
    return jnp.pad(flat, ((0, 0), (0, LANE - N_HEADS_SSM)))


_NT = (((1,), (1,)), ((), ()))
_TN = (((0,), (0,)), ((), ()))
P = SSM_HEADDIM
GW = HEADS_PER_GROUP * SSM_HEADDIM


def _decay(cc, cr):
    L = cc.shape[0]
    i = lax.broadcasted_iota(jnp.int32, (L, L), 0)
    j = lax.broadcasted_iota(jnp.int32, (L, L), 1)
    return jnp.exp(jnp.where(i >= j, cc - cr, NEG))


def _ssd_fwd(xbc_c, dt_g, cum_g, cumt_g):
    S = xbc_c.shape[0]
    nc = S // CHUNK
    L = CHUNK
    boff = D_INNER // D_STATE

    def body(x_ref, b_ref, c_ref, dt_ref, cum_ref, cumt_ref, y_ref, st_ref, state):
        c = pl.program_id(1)

        @pl.when(c == 0)
        def _():
            state[...] = jnp.zeros_like(state)

        bm = b_ref[...].astype(BF16)
        cm = c_ref[...].astype(BF16)
        cb = lax.dot_general(cm, bm, _NT, preferred_element_type=F32)
        for r in range(HEADS_PER_GROUP):
            cc = cum_ref[0, :, r:r + 1]
            cr = cumt_ref[0, r:r + 1, :]
            m = (cb * _decay(cc, cr)).astype(BF16)
            xdt = x_ref[:, r * P:(r + 1) * P] * dt_ref[0, :, r:r + 1]
            st = state[r * P:(r + 1) * P, :]
            st_ref[0, 0, r * P:(r + 1) * P, :] = st
            y = jnp.dot(m, xdt.astype(BF16), preferred_element_type=F32)
            y = y + lax.dot_general(cm, st.astype(BF16), _NT, preferred_element_type=F32) * jnp.exp(cc)
            y_ref[:, r * P:(r + 1) * P] = y
            cl = cum_ref[0, L - 1:L, r:r + 1]
            wend = jnp.exp(cl - cc)
            state[r * P:(r + 1) * P, :] = st * jnp.exp(cl) + lax.dot_general(
                (xdt * wend).astype(BF16), bm, _TN, preferred_element_type=F32)

    return pl.pallas_call(
        body, name="ssd_fwd",
        out_shape=(jax.ShapeDtypeStruct((S, D_INNER), F32), jax.ShapeDtypeStruct((SSM_GROUPS, nc, GW, D_STATE), F32)),
        grid=(SSM_GROUPS, nc),
        in_specs=[pl.BlockSpec((L, GW), lambda g, c: (c, g)),
                  pl.BlockSpec((L, D_STATE), lambda g, c: (c, boff + g)),
                  pl.BlockSpec((L, D_STATE), lambda g, c: (c, boff + SSM_GROUPS + g)),
                  pl.BlockSpec((1, L, HEADS_PER_GROUP), lambda g, c: (g, c, 0)),
                  pl.BlockSpec((1, L, HEADS_PER_GROUP), lambda g, c: (g, c, 0)),
                  pl.BlockSpec((1, HEADS_PER_GROUP, L), lambda g, c: (g, 0, c))],
        out_specs=(pl.BlockSpec((L, GW), lambda g, c: (c, g)),
                   pl.BlockSpec((1, 1, GW, D_STATE), lambda g, c: (g, c, 0, 0))),
        scratch_shapes=[pltpu.VMEM((GW, D_STATE), F32)],
        compiler_params=_params(("parallel", "arbitrary")),
    )(xbc_c, xbc_c, xbc_c, dt_g, cum_g, cumt_g)


def _ssd_bwd(xbc_c, dt_g, cum_g, cumt_g, states, dy, dx_skip):
    S = xbc_c.shape[0]
    nc = S // CHUNK
    L = CHUNK
    boff = D_INNER // D_STATE
    rev = lambda c: nc - 1 - c

    def body(x_ref, b_ref, c_ref, dt_ref, cum_ref, cumt_ref, st_ref, dy_ref, skip_ref,
             dx_ref, db_ref, dc_ref, ddt_ref, dcum_ref, dstate):
        c = pl.program_id(1)

        @pl.when(c == 0)
        def _():
            dstate[...] = jnp.zeros_like(dstate)

        bf = b_ref[...]
        bm = bf.astype(BF16)
        cm = c_ref[...].astype(BF16)
        cb = lax.dot_general(cm, bm, _NT, preferred_element_type=F32)
        dcb = jnp.zeros((L, L), F32)
        dbs = jnp.zeros((L, D_STATE), F32)
        dcs = jnp.zeros((L, D_STATE), F32)
        rowid = lax.broadcasted_iota(jnp.int32, (L, 1), 0)
        for r in range(HEADS_PER_GROUP):
            sl = slice(r * P, (r + 1) * P)
            cc = cum_ref[0, :, r:r + 1]
            cr = cumt_ref[0, r:r + 1, :]
            dtc = dt_ref[0, :, r:r + 1]
            decay = _decay(cc, cr)
            m = cb * decay
            xr = x_ref[:, sl]
            xdt = xr * dtc
            xdb = xdt.astype(BF16)
            dyr = dy_ref[:, sl]
            dyb = dyr.astype(BF16)
            st = st_ref[0, 0, sl, :]
            stb = st.astype(BF16)
            ds = dstate[sl, :]
            dsb = ds.astype(BF16)
            ecc = jnp.exp(cc)
            cl = cum_ref[0, L - 1:L, r:r + 1]
            ecl = jnp.exp(cl)
            wend = jnp.exp(cl - cc)

            g = lax.dot_general(dyb, xdb, _NT, preferred_element_type=F32)
            q = g * m
            dcb = dcb + g * decay
            dcum = jnp.sum(q, axis=1, keepdims=True) - _row_of_t(jnp.sum(q, axis=0, keepdims=True), L)
            dxd = lax.dot_general(m.astype(BF16), dyb, _TN, preferred_element_type=F32)
            dxd = dxd + lax.dot_general(bm, dsb, _NT, preferred_element_type=F32) * wend
            yoff = lax.dot_general(cm, stb, _NT, preferred_element_type=F32) * ecc
            dcum = dcum + jnp.sum(dyr * yoff, axis=1, keepdims=True)
            dcs = dcs + jnp.dot(dyb, stb, preferred_element_type=F32) * ecc
            t = jnp.dot(xdb, dsb, preferred_element_type=F32)
            dbs = dbs + t * wend
            vj = jnp.sum(t * bf, axis=1, keepdims=True) * wend
            dcum = dcum - vj
            dlast = jnp.sum(vj) + ecl * jnp.sum(ds * st)
            dcum = dcum + jnp.where(rowid == L - 1, dlast, 0.0)
            dstate[sl, :] = ecl * ds + lax.dot_general((dyr * ecc).astype(BF16), cm, _TN, preferred_element_type=F32)

            dx_ref[:, sl] = dxd * dtc + skip_ref[:, sl]
            ddt_ref[0, :, r:r + 1] = jnp.sum(dxd * xr, axis=1, keepdims=True)
            dcum_ref[0, :, r:r + 1] = dcum
        dcbb = dcb.astype(BF16)
        dc_ref[...] = dcs + jnp.dot(dcbb, bm, preferred_element_type=F32)
        db_ref[...] = dbs + lax.dot_general(dcbb, cm, _TN, preferred_element_type=F32)

    return pl.pallas_call(
        body, name="ssd_bwd",
        out_shape=(jax.ShapeDtypeStruct((S, D_INNER), F32),
                   jax.ShapeDtypeStruct((S, SSM_GROUPS * D_STATE), F32),
                   jax.ShapeDtypeStruct((S, SSM_GROUPS * D_STATE), F32),
                   jax.ShapeDtypeStruct((SSM_GROUPS, S, HEADS_PER_GROUP), F32),
                   jax.ShapeDtypeStruct((SSM_GROUPS, S, HEADS_PER_GROUP), F32)),
        grid=(SSM_GROUPS, nc),
        in_specs=[pl.BlockSpec((L, GW), lambda g, c: (rev(c), g)),
                  pl.BlockSpec((L, D_STATE), lambda g, c: (rev(c), boff + g)),
                  pl.BlockSpec((L, D_STATE), lambda g, c: (rev(c), boff + SSM_GROUPS + g)),
                  pl.BlockSpec((1, L, HEADS_PER_GROUP), lambda g, c: (g, rev(c), 0)),
                  pl.BlockSpec((1, L, HEADS_PER_GROUP), lambda g, c: (g, rev(c), 0)),
                  pl.BlockSpec((1, HEADS_PER_GROUP, L), lambda g, c: (g, 0, rev(c))),
                  pl.BlockSpec((1, 1, GW, D_STATE), lambda g, c: (g, rev(c), 0, 0)),
                  pl.BlockSpec((L, GW), lambda g, c: (rev(c), g)),
                  pl.BlockSpec((L, GW), lambda g, c: (rev(c), g))],
        out_specs=(pl.BlockSpec((L, GW), lambda g, c: (rev(c), g)),
                   pl.BlockSpec((L, D_STATE), lambda g, c: (rev(c), g)),
                   pl.BlockSpec((L, D_STATE), lambda g, c: (rev(c), g)),
                   pl.BlockSpec((1, L, HEADS_PER_GROUP), lambda g, c: (g, rev(c), 0)),
                   pl.BlockSpec((1, L, HEADS_PER_GROUP), lambda g, c: (g, rev(c), 0))),
        scratch_shapes=[pltpu.VMEM((GW, D_STATE), F32)],
        compiler_params=_params(("parallel", "arbitrary")),
    )(xbc_c, xbc_c, xbc_c, dt_g, cum_g, cumt_g, states, dy, dx_skip)


def _row_of_t(row, n):
    eye = lax.broadcasted_iota(jnp.int32, (n, n), 0) == lax.broadcasted_iota(jnp.int32, (n, n), 1)
    return jnp.sum(jnp.where(eye, row, 0.0), axis=1, keepdims=True)


def _dt_bwd(dt_raw, dt_bias, a_log, ddt_x, dcum):
    S = dt_raw.shape[0]

    def fn(i, n, raw, ddx, dcu, bias, alog):
        xx = raw + bias
        dt = _softplus(xx)
        a = -jnp.exp(alog)
        dda = _revcumsum_rows(dcu)
        ddt = ddx + dda * a
        lane = lax.broadcasted_iota(jnp.int32, raw.shape, 1)
        draw = jnp.where(lane < N_HEADS_SSM, ddt * _sigmoid(xx), 0.0)
        return draw, _colsum(draw), _colsum(dda * dt) * a

    return _rows(fn, S, CHUNK, [_row(dt_raw), _row(ddt_x), _row(dcum), _full(dt_bias), _full(a_log)],
                 [("row", LANE, BF16), ("acc", (1, LANE), F32), ("acc", (1, LANE), F32)], "dt_bwd")


PAIRS = HEADS_PER_GROUP // 2
SPREAD_W = HEADS_PER_GROUP * LANE


def _spread_matrix():
    e = np.zeros((SSM_GROUPS, LANE, SPREAD_W), np.float32)
    for g in range(SSM_GROUPS):
        for r in range(HEADS_PER_GROUP):
            e[g, g * HEADS_PER_GROUP + r, r * LANE:(r + 1) * LANE] = 1.0
    return jnp.asarray(e, BF16)


def _pieces(v, n):
    out = []
    for _ in range(n):
        p = v.astype(BF16)
        out.append(p)
        v = v - p.astype(F32)
    return out


def _spread(v, e, n):
    tot = None
    for p in _pieces(v, n):
        t = jnp.dot(p, e, preferred_element_type=F32)
        tot = t if tot is None else tot + t
    return tot


def _gather_rows(z, e):
    hi, lo = _pieces(z, 2)
    return lax.dot_general(hi, e, _NT, preferred_element_type=F32) + lax.dot_general(lo, e, _NT, preferred_element_type=F32)


def _decay_pair(cc, cr, transposed):
    L = cc.shape[0]
    halves = []
    for h in range(L // LANE):
        i = lax.broadcasted_iota(jnp.int32, (L, LANE), 0)
        j = lax.broadcasted_iota(jnp.int32, (L, LANE), 1) + h * LANE
        crh = cr[:, h * LANE:(h + 1) * LANE]
        if transposed:
            halves.append(jnp.exp(jnp.where(j >= i, crh - cc, NEG)))
        else:
            halves.append(jnp.exp(jnp.where(i >= j, cc - crh, NEG)))
    return jnp.concatenate(halves, axis=1)


def _ssd_fwd2(xbc_c, dt, cum, cumt_g, spread):
    S = xbc_c.shape[0]
    nc = S // CHUNK
    L = CHUNK
    boff = D_INNER // D_STATE

    def body(x_ref, b_ref, c_ref, dt_ref, cum_ref, cumt_ref, e_ref, y_ref, st_ref, state):
        c = pl.program_id(1)

        @pl.when(c == 0)
        def _():
            state[...] = jnp.zeros_like(state)

        e = e_ref[0]
        bm = b_ref[...].astype(BF16)
        cm = c_ref[...].astype(BF16)
        cb = lax.dot_general(cm, bm, _NT, preferred_element_type=F32)
        rep_cum = _spread(cum_ref[...], e, 3)
        rep_dt = _spread(dt_ref[...], e, 2)
        lo = lax.broadcasted_iota(jnp.int32, (L, LANE), 1) < P
        lo1 = lax.broadcasted_iota(jnp.int32, (1, LANE), 1) < P
        top = lax.broadcasted_iota(jnp.int32, (LANE, LANE), 0) < P
        for p in range(PAIRS):
            t0, t1 = 2 * p * LANE, (2 * p + 1) * LANE
            cc0, cc1 = rep_cum[:, t0:t0 + LANE], rep_cum[:, t1:t1 + LANE]
            ccp = jnp.where(lo, cc0, cc1)
            cl0, cl1 = cc0[L - 1:L, :], cc1[L - 1:L, :]
            clp = jnp.where(lo1, cl0, cl1)
            xdt = x_ref[:, p * LANE:(p + 1) * LANE] * jnp.where(lo, rep_dt[:, t0:t0 + LANE], rep_dt[:, t1:t1 + LANE])
            xdb = xdt.astype(BF16)
            ys = []
            for r, cc in ((2 * p, cc0), (2 * p + 1, cc1)):
                m = (cb * _decay_pair(cc, cumt_ref[0, r:r + 1, :], False)).astype(BF16)
                ys.append(jnp.dot(m, xdb, preferred_element_type=F32))
            st = state[p * LANE:(p + 1) * LANE, :]
            st_ref[0, 0, p * LANE:(p + 1) * LANE, :] = st
            yoff = lax.dot_general(cm, st.astype(BF16), _NT, preferred_element_type=F32) * jnp.exp(ccp)
            y_ref[:, p * LANE:(p + 1) * LANE] = jnp.where(lo, ys[0], ys[1]) + yoff
            wend = jnp.exp(clp - ccp)
            ecl = jnp.where(top, jnp.exp(cl0), jnp.exp(cl1))
            state[p * LANE:(p + 1) * LANE, :] = st * ecl + lax.dot_general(
                (xdt * wend).astype(BF16), bm, _TN, preferred_element_type=F32)

    return pl.pallas_call(
        body, name="ssd_fwd",
        out_shape=(jax.ShapeDtypeStruct((S, D_INNER), F32), jax.ShapeDtypeStruct((SSM_GROUPS, nc, GW, D_STATE), F32)),
        grid=(SSM_GROUPS, nc),
        in_specs=[pl.BlockSpec((L, GW), lambda g, c: (c, g)),
                  pl.BlockSpec((L, D_STATE), lambda g, c: (c, boff + g)),
                  pl.BlockSpec((L, D_STATE), lambda g, c: (c, boff + SSM_GROUPS + g)),
                  pl.BlockSpec((L, LANE), lambda g, c: (c, 0)),
                  pl.BlockSpec((L, LANE), lambda g, c: (c, 0)),
                  pl.BlockSpec((1, HEADS_PER_GROUP, L), lambda g, c: (g, 0, c)),
                  pl.BlockSpec((1, LANE, SPREAD_W), lambda g, c: (g, 0, 0))],
        out_specs=(pl.BlockSpec((L, GW), lambda g, c: (c, g)),
                   pl.BlockSpec((1, 1, GW, D_STATE), lambda g, c: (g, c, 0, 0))),
        scratch_shapes=[pltpu.VMEM((GW, D_STATE), F32)],
        compiler_params=_params(("parallel", "arbitrary")),
    )(xbc_c, xbc_c, xbc_c, dt, cum, cumt_g, spread)


def _ssd_bwd2(xbc_c, dt, cum, cumt_g, spread, states, dy, dx_skip):
    S = xbc_c.shape[0]
    nc = S // CHUNK
    L = CHUNK
    boff = D_INNER // D_STATE
    rev = lambda c: nc - 1 - c

    def body(x_ref, b_ref, c_ref, dt_ref, cum_ref, cumt_ref, e_ref, st_ref, dy_ref, skip_ref,
             dx_ref, db_ref, dc_ref, ddt_ref, dcum_ref, dstate):
        c = pl.program_id(1)

        @pl.when(c == 0)
        def _():
            dstate[...] = jnp.zeros_like(dstate)

        e = e_ref[0]
        bf = b_ref[...]
        bm = bf.astype(BF16)
        cm = c_ref[...].astype(BF16)
        cb = lax.dot_general(cm, bm, _NT, preferred_element_type=F32)
        cbt = lax.dot_general(bm, cm, _NT, preferred_element_type=F32)
        rep_cum = _spread(cum_ref[...], e, 3)
        rep_dt = _spread(dt_ref[...], e, 2)
        lane = lax.broadcasted_iota(jnp.int32, (L, LANE), 1)
        lo = lane < P
        lo1 = lax.broadcasted_iota(jnp.int32, (1, LANE), 1) < P
        top = lax.broadcasted_iota(jnp.int32, (LANE, LANE), 0) < P
        last = lax.broadcasted_iota(jnp.int32, (L, LANE), 0) == L - 1
        dcb = jnp.zeros((L, L), F32)
        dcbt = jnp.zeros((L, L), F32)
        dbs = jnp.zeros((L, D_STATE), F32)
        dcs = jnp.zeros((L, D_STATE), F32)
        zs, zds = [], []
        for p in range(PAIRS):
            sl = slice(p * LANE, (p + 1) * LANE)
            t0, t1 = 2 * p * LANE, (2 * p + 1) * LANE
            cc0, cc1 = rep_cum[:, t0:t0 + LANE], rep_cum[:, t1:t1 + LANE]
            ccp = jnp.where(lo, cc0, cc1)
            cl0, cl1 = cc0[L - 1:L, :], cc1[L - 1:L, :]
            w0, w1 = jnp.exp(cl0 - cc0), jnp.exp(cl1 - cc1)
            wend = jnp.where(lo, w0, w1)
            ecc = jnp.exp(ccp)
            ecl0, ecl1 = jnp.exp(cl0), jnp.exp(cl1)
            dtp = jnp.where(lo, rep_dt[:, t0:t0 + LANE], rep_dt[:, t1:t1 + LANE])
            xp = x_ref[:, sl]
            xdt = xp * dtp
            xdb = xdt.astype(BF16)
            dyp = dy_ref[:, sl]
            st = st_ref[0, 0, sl, :]
            stb = st.astype(BF16)
            ds = dstate[sl, :]
            dsb = ds.astype(BF16)
            yoff = lax.dot_general(cm, stb, _NT, preferred_element_type=F32) * ecc
            dye = (dyp * ecc).astype(BF16)
            dcs = dcs + jnp.dot(dye, stb, preferred_element_type=F32)
            dstate[sl, :] = jnp.where(top, ecl0, ecl1) * ds + lax.dot_general(dye, cm, _TN, preferred_element_type=F32)
            dxd = lax.dot_general(bm, dsb, _NT, preferred_element_type=F32) * wend
            sst = ds * st
            dyo = dyp * yoff
            mts = []
            for r, cc, w, ecl, keep, keep_rows in ((2 * p, cc0, w0, ecl0, lo, top), (2 * p + 1, cc1, w1, ecl1, ~lo, ~top)):
                cr = cumt_ref[0, r:r + 1, :]
                decay = _decay_pair(cc, cr, False)
                decay_t = _decay_pair(cc, cr, True)
                m = cb * decay
                mt = cbt * decay_t
                dyr = jnp.where(keep, dyp, 0.0).astype(BF16)
                g = lax.dot_general(dyr, xdb, _NT, preferred_element_type=F32)
                gt = lax.dot_general(xdb, dyr, _NT, preferred_element_type=F32)
                q = g * m
                qt = gt * mt
                dcb = dcb + g * decay
                dcbt = dcbt + gt * decay_t
                mts.append(jnp.dot(mt.astype(BF16), dyr, preferred_element_type=F32))
                t = jnp.dot(jnp.where(keep, xdt, 0.0).astype(BF16), dsb, preferred_element_type=F32)
                dbs = dbs + t * w
                tb = t * bf * w
                end_row = _colsum(tb) + ecl * _colsum(jnp.where(keep_rows, sst, 0.0))
                z = (q[:, 0:LANE] + q[:, LANE:2 * LANE]) - (qt[:, 0:LANE] + qt[:, LANE:2 * LANE])
                z = z + jnp.where(keep, dyo, 0.0) - tb + jnp.where(last, end_row, 0.0)
                zs.append(z)
            dxd = dxd + mts[0] + mts[1]
            dx_ref[:, sl] = dxd * dtp + skip_ref[:, sl]
            zd = dxd * xp
            zds.append(jnp.where(lo, zd, 0.0))
            zds.append(jnp.where(lo, 0.0, zd))
        dc_ref[...] = dcs + jnp.dot(dcb.astype(BF16), bm, preferred_element_type=F32)
        db_ref[...] = dbs + jnp.dot(dcbt.astype(BF16), cm, preferred_element_type=F32)
        dcum_ref[0] = _gather_rows(jnp.concatenate(zs, axis=1), e)
        ddt_ref[0] = _gather_rows(jnp.concatenate(zds, axis=1), e)

    return pl.pallas_call(
        body, name="ssd_bwd",
        out_shape=(jax.ShapeDtypeStruct((S, D_INNER), F32),
                   jax.ShapeDtypeStruct((S, SSM_GROUPS * D_STATE), F32),
                   jax.ShapeDtypeStruct((S, SSM_GROUPS * D_STATE), F32),
                   jax.ShapeDtypeStruct((SSM_GROUPS, S, LANE), F32),
                   jax.ShapeDtypeStruct((SSM_GROUPS, S, LANE), F32)),
        grid=(SSM_GROUPS, nc),
        in_specs=[pl.BlockSpec((L, GW), lambda g, c: (rev(c), g)),
                  pl.BlockSpec((L, D_STATE), lambda g, c: (rev(c), boff + g)),
                  pl.BlockSpec((L, D_STATE), lambda g, c: (rev(c), boff + SSM_GROUPS + g)),
                  pl.BlockSpec((L, LANE), lambda g, c: (rev(c), 0)),
                  pl.BlockSpec((L, LANE), lambda g, c: (rev(c), 0)),
                  pl.BlockSpec((1, HEADS_PER_GROUP, L), lambda g, c: (g, 0, rev(c))),
                  pl.BlockSpec((1, LANE, SPREAD_W), lambda g, c: (g, 0, 0)),
                  pl.BlockSpec((1, 1, GW, D_STATE), lambda g, c: (g, rev(c), 0, 0)),
                  pl.BlockSpec((L, GW), lambda g, c: (rev(c), g)),
                  pl.BlockSpec((L, GW), lambda g, c: (rev(c), g))],
        out_specs=(pl.BlockSpec((L, GW), lambda g, c: (rev(c), g)),
                   pl.BlockSpec((L, D_STATE), lambda g, c: (rev(c), g)),
                   pl.BlockSpec((L, D_STATE), lambda g, c: (rev(c), g)),
                   pl.BlockSpec((1, L, LANE), lambda g, c: (g, rev(c), 0)),
                   pl.BlockSpec((1, L, LANE), lambda g, c: (g, rev(c), 0))),
        scratch_shapes=[pltpu.VMEM((GW, D_STATE), F32)],
        compiler_params=_params(("parallel", "arbitrary")),
    )(xbc_c, xbc_c, xbc_c, dt, cum, cumt_g, spread, states, dy, dx_skip)


def _dt_bwd2(dt_raw, dt_bias, a_log, ddt_x, dcum):
    S = dt_raw.shape[0]
    n = S // CHUNK

    def body(raw_ref, ddx_ref, dcu_ref, bias_ref, alog_ref, draw_ref, gb_ref, ga_ref):
        i = pl.program_id(0)

        @pl.when(i == 0)
        def _():
            gb_ref[...] = jnp.zeros_like(gb_ref)
            ga_ref[...] = jnp.zeros_like(ga_ref)

        ddx, dcu = ddx_ref[0], dcu_ref[0]
        for g in range(1, SSM_GROUPS):
            ddx = ddx + ddx_ref[g]
            dcu = dcu + dcu_ref[g]
        xx = raw_ref[...] + bias_ref[...]
        dt = _softplus(xx)
        a = -jnp.exp(alog_ref[...])
        dda = _revcumsum_rows(dcu)
        lane = lax.broadcasted_iota(jnp.int32, xx.shape, 1)
        draw = jnp.where(lane < N_HEADS_SSM, (ddx + dda * a) * _sigmoid(xx), 0.0)
        draw_ref[...] = draw.astype(draw_ref.dtype)
        gb_ref[...] += _colsum(draw)
        ga_ref[...] += _colsum(dda * dt) * a

    row = pl.BlockSpec((CHUNK, LANE), lambda i: (i, 0))
    grp = pl.BlockSpec((SSM_GROUPS, CHUNK, LANE), lambda i: (0, i, 0))
    one = pl.BlockSpec((1, LANE), lambda i: (0, 0))
    return pl.pallas_call(
        body, name="dt_bwd",
        out_shape=(jax.ShapeDtypeStruct((S, LANE), BF16), jax.ShapeDtypeStruct((1, LANE), F32), jax.ShapeDtypeStruct((1, LANE), F32)),
        grid=(n,),
        in_specs=[row, grp, grp, one, one],
        out_specs=(row, one, one),
        compiler_params=_params(("arbitrary",)),
    )(dt_raw, ddt_x, dcum, dt_bias, a_log)


def _adamw(w, g, m, v, name):
    shape = w.shape
    cols = shape[-1]
    rows = int(np.prod(shape[:-1]))
    w2, g2, m2, v2 = (t.reshape(rows, cols) for t in (w, g, m, v))
    tr = rows if rows * cols <= 512 * 1024 else _tile(rows, max(SUBLANE, (512 * 1024 // cols) // SUBLANE * SUBLANE), SUBLANE)
    c1 = 1.0 - ADAM_B1 ** ADAM_STEP
    c2 = 1.0 - ADAM_B2 ** ADAM_STEP

    def body(w_ref, g_ref, m_ref, v_ref, d_ref, mo_ref, vo_ref):
        gv = g_ref[...]
        mn = ADAM_B1 * m_ref[...] + (1.0 - ADAM_B1) * gv
        vn = ADAM_B2 * v_ref[...] + (1.0 - ADAM_B2) * (gv * gv)
        d_ref[...] = -ADAM_LR * ((mn / c1) / (jnp.sqrt(vn / c2) + ADAM_EPS) + ADAM_WD * w_ref[...])
        mo_ref[...] = mn
        vo_ref[...] = vn

    spec = pl.BlockSpec((tr, cols), lambda i: (i, 0))
    outs = pl.pallas_call(
        body, name=name,
        out_shape=tuple(jax.ShapeDtypeStruct((rows, cols), F32) for _ in range(3)),
        grid=(rows // tr,),
        in_specs=[spec] * 4, out_specs=(spec,) * 3,
        compiler_params=_params(("parallel",)),
    )(w2, g2, m2, v2)
    return tuple(o.reshape(shape) for o in outs)


def _prep_weights(w_in, w_uq):
    offs = np.cumsum((0,) + IN_SPLITS)
    pad = lambda t: jnp.pad(t, ((0, 0), (0, LANE - t.shape[1])))
    pieces = dict(
        qkv=w_in[:, offs[0]:offs[2]],
        kr=pad(w_in[:, offs[2]:offs[3]]),
        z=w_in[:, offs[3]:offs[4]],
        xbc=w_in[:, offs[4]:offs[5]],
        dt=pad(w_in[:, offs[5]:offs[6]]),
        g=w_in[:, offs[6]:offs[8]],
    )
    uq = w_uq.reshape(Q_LORA, N_HEADS_MLA, QK_DIM)
    uq = jnp.pad(uq, ((0, 0), (0, 0), (0, HEAD_PAD - QK_DIM))).reshape(Q_LORA, N_HEADS_MLA * HEAD_PAD)
    return pieces, uq


def _local_step(x, p, positions, ex, sp, target):
    W = gw = ex
    S = x.shape[0]
    tile = min(ATTN_TILE, S)
    pos_col = positions.reshape(S, 1)
    invf = ROPE_THETA ** (-jnp.arange(0, QK_ROPE, 2, dtype=F32) / QK_ROPE)
    invf = jnp.pad(jnp.concatenate([invf, invf]), (0, LANE - QK_ROPE)).reshape(1, LANE)
    wp, w_uq_p = _prep_weights(W["w_in"], W["w_uq"])
    padl = lambda t: jnp.pad(t, ((0, 0), (0, LANE - t.shape[1])))
    dt_bias_p, a_log_p = padl(sp["dt_bias"]), padl(sp["a_log"])
    dskip_ch = jnp.repeat(sp["d_skip"], SSM_HEADDIM, axis=1)
    p_bf = p.astype(BF16)
    RW = 256

    (u_bf,) = _rows(lambda i, n, x, g: (x * _rstd(x) * g,), S, RW, [_row(x), _full(sp["mix_norm_pre"])],
                    [("row", D_MODEL, BF16)], "norm_pre")
    cqkv = ex.mm(u_bf, wp["qkv"], name="mm_qkv")
    z = ex.mm(u_bf, wp["z"], name="mm_z")
    xbc = ex.mm(u_bf, wp["xbc"], name="mm_xbc")
    gates = ex.mm(u_bf, wp["g"], name="mm_gates")
    kr_pad = ex.mm(u_bf, wp["kr"], name="mm_kr")
    dt_raw = ex.mm(u_bf, wp["dt"], name="mm_dt")

    def qkv_norm(i, n, cq, ckv, gq, gkv):
        return cq * _rstd(cq) * gq, ckv * _rstd(ckv) * gkv

    cqn, ckvn = _rows(qkv_norm, S, 512, [_row(cqkv, Q_LORA, 0), _row(cqkv, KV_LORA, 1), _full(sp["q_norm"]), _full(sp["kv_norm"])],
                      [("row", Q_LORA, BF16), ("row", KV_LORA, BF16)], "qkv_norm")
    q_raw = ex.mm(cqn, w_uq_p, name="mm_uq")
    kv = ex.mm(ckvn, W["w_ukv"], out_dtype=BF16, name="mm_ukv")
    q_bf, kp_bf = _rope_fwd(q_raw, kr_pad, pos_col, invf)
    attn, lse = _attn_fwd(q_bf, kv, kp_bf, tile)

    xbc_c = _conv_fwd(xbc, sp["conv_w"], sp["conv_b"])
    dt, cum, cumt = _dt_prep(dt_raw, dt_bias_p, a_log_p)
    cumt_g = cumt[:N_HEADS_SSM].reshape(SSM_GROUPS, HEADS_PER_GROUP, S)
    spread = _spread_matrix()
    y, states = _ssd_fwd2(xbc_c, dt, cum, cumt_g, spread)

    GN = D_INNER // SSM_GROUPS

    def gated(y, xs, z, dsk):
        yt = y + dsk * xs
        sz = _sigmoid(z)
        return yt, sz, yt * (z * sz)

    def gated_norm(i, n, y, xs, z, dsk, gn):
        _, _, yg = gated(y, xs, z, dsk)
        parts = []
        for g in range(SSM_GROUPS):
            blk = yg[:, g * GN:(g + 1) * GN]
            parts.append(blk * _rstd(blk) * gn[:, g * GN:(g + 1) * GN])
        return (jnp.concatenate(parts, axis=1),)

    (ssm,) = _rows(gated_norm, S, 128, [_row(y), _row(xbc_c, D_INNER, 0), _row(z), _full(dskip_ch), _full(sp["ssm_norm"])],
                   [("row", D_INNER, BF16)], "gated_norm")

    a_o = ex.mm(attn, W["w_attn_o"], name="mm_attn_o")
    b_o = ex.mm(ssm, W["w_ssm_o"], name="mm_ssm_o")

    def mix(i, n, ga, gs, a, b):
        return (_sigmoid(ga) * a + _sigmoid(gs) * b,)

    (mixed,) = _rows(mix, S, RW, [_row(gates, D_MODEL, 0), _row(gates, D_MODEL, 1), _row(a_o), _row(b_o)],
                     [("row", D_MODEL, BF16)], "mix")
    m2 = ex.mm(mixed, W["w_out"], name="mm_out")

    def post(i, n, h, m, gpost, gpre):
        hn = h + m * _rstd(m) * gpost
        return hn, hn * _rstd(hn) * gpre

    h1, f_bf = _rows(post, S, RW, [_row(x), _row(m2), _full(sp["mix_norm_post"]), _full(sp["ffn_norm_pre"])],
                     [("row", D_MODEL, F32), ("row", D_MODEL, BF16)], "post_mix")
    ga = ex.mm(f_bf, W["w_gate"], name="mm_gate")
    up = ex.mm(f_bf, W["w_up"], name="mm_up")
    (s_bf,) = _rows(lambda i, n, a, b: (a * _sigmoid(a) * b,), S, RW, [_row(ga), _row(up)], [("row", D_FF, BF16)], "swiglu")
    f2 = ex.mm(s_bf, W["w_down"], name="mm_down")
    h2, n3_bf = _rows(post, S, RW, [_row(h1), _row(f2), _full(sp["ffn_norm_post"]), _full(sp["ple_norm_pre"])],
                      [("row", D_MODEL, F32), ("row", D_MODEL, BF16)], "post_ffn")
    gpre = ex.mm(n3_bf, W["w_ple_gate"], name="mm_ple_gate")
    pe = ex.mm(p_bf, W["w_ple"], name="mm_ple")

    def ple_loss(i, n, h2, gpre, pe, tgt, gpost):
        gate = _sigmoid(gpre)
        e = pe * gate
        r = _rstd(e)
        diff = h2 + e * r * gpost - tgt
        loss = 0.5 * jnp.sum(jnp.mean(diff * diff, axis=1, keepdims=True))
        dh3 = diff * (1.0 / D_MODEL)
        de, dg_rows = _norm_bwd(e, r, gpost, dh3)
        return (jnp.full((1, LANE), loss, F32), dh3, de * gate, de * pe * gate * (1.0 - gate), _colsum(dg_rows))

    loss, dh3, dpe, dgpre, g_ple_post = _rows(
        ple_loss, S, 128, [_row(h2), _row(gpre), _row(pe), _row(target), _full(sp["ple_norm_post"])],
        [("acc", (1, LANE), F32), ("row", D_MODEL, F32), ("row", D_MODEL, BF16), ("row", D_MODEL, BF16),
         ("acc", (1, D_MODEL), F32)], "ple_loss")

    gs = {"ple_norm_post": g_ple_post}
    gw["w_ple"] = ex.mm(p_bf, dpe, ta=True, name="mmg_ple")
    gw["w_ple_gate"] = ex.mm(n3_bf, dgpre, ta=True, name="mmg_ple_gate")
    dn3 = ex.mm(dgpre, W["w_ple_gate"], tb=True, name="mmb_ple_gate")

    def post_bwd(i, n, h, m, dhn, dn, gpost, gpre):
        rm = _rstd(m)
        hn = h + m * rm * gpost
        dx, dgpre_rows = _norm_bwd(hn, _rstd(hn), gpre, dn)
        dhn_t = dhn + dx
        dm, dgpost_rows = _norm_bwd(m, rm, gpost, dhn_t)
        return dhn_t, dm, _colsum(dgpre_rows), _colsum(dgpost_rows)

    def run_post_bwd(h, m, dhn, dn, gpost, gpre, name):
        return _rows(post_bwd, S, 128, [_row(h), _row(m), _row(dhn), _row(dn), _full(gpost), _full(gpre)],
                     [("row", D_MODEL, F32), ("row", D_MODEL, BF16), ("acc", (1, D_MODEL), F32), ("acc", (1, D_MODEL), F32)], name)

    dh2, df2, gs["ple_norm_pre"], gs["ffn_norm_post"] = run_post_bwd(
        h1, f2, dh3, dn3, sp["ffn_norm_post"], sp["ple_norm_pre"], "post_ffn_bwd")
    gw["w_down"] = ex.mm(s_bf, df2, ta=True, name="mmg_down")
    ds = ex.mm(df2, W["w_down"], tb=True, name="mmb_down")

    def swiglu_bwd(i, n, a, b, ds):
        sa = _sigmoid(a)
        return ds * b * (sa * (1.0 + a * (1.0 - sa))), ds * (a * sa)

    dga, dup = _rows(swiglu_bwd, S, RW, [_row(ga), _row(up), _row(ds)], [("row", D_FF, BF16), ("row", D_FF, BF16)], "swiglu_bwd")
    gw["w_gate"] = ex.mm(f_bf, dga, ta=True, name="mmg_gate")
    gw["w_up"] = ex.mm(f_bf, dup, ta=True, name="mmg_up")
    df = ex.mm(dga, W["w_gate"], tb=True, name="mmb_gate")
    df = ex.mm(dup, W["w_up"], tb=True, add=df, name="mmb_up")
    dh1, dm2, gs["ffn_norm_pre"], gs["mix_norm_post"] = run_post_bwd(
        x, m2, dh2, df, sp["mix_norm_post"], sp["ffn_norm_pre"], "post_mix_bwd")
    gw["w_out"] = ex.mm(mixed, dm2, ta=True, name="mmg_out")
    dmixed = ex.mm(dm2, W["w_out"], tb=True, name="mmb_out")

    def mix_bwd(i, n, ga, gs_, a, b, dm):
        sa, ss = _sigmoid(ga), _sigmoid(gs_)
        return dm * sa, dm * ss, jnp.concatenate([dm * a * sa * (1.0 - sa), dm * b * ss * (1.0 - ss)], axis=1)

    da_o, db_o, dgates = _rows(mix_bwd, S, RW, [_row(gates, D_MODEL, 0), _row(gates, D_MODEL, 1), _row(a_o), _row(b_o), _row(dmixed)],
                               [("row", D_MODEL, BF16), ("row", D_MODEL, BF16), ("row", 2 * D_MODEL, BF16)], "mix_bwd")
    gw["w_attn_o"] = ex.mm(attn, da_o, ta=True, name="mmg_attn_o")
    dattn = ex.mm(da_o, W["w_attn_o"], tb=True, name="mmb_attn_o")
    gw["w_ssm_o"] = ex.mm(ssm, db_o, ta=True, name="mmg_ssm_o")
    dssm = ex.mm(db_o, W["w_ssm_o"], tb=True, name="mmb_ssm_o")

    delta, dattn_bf = _attn_delta(attn, dattn, tile)
    dq, dkv, dkp = _attn_bwd(q_bf, kv, kp_bf, dattn_bf, lse, delta, tile)
    dq_raw, dkr = _rope_bwd(dq, dkp, pos_col, invf)
    g_uq_p = ex.mm(cqn, dq_raw, ta=True, name="mmg_uq")
    gw["w_uq"] = g_uq_p.reshape(Q_LORA, N_HEADS_MLA, HEAD_PAD)[:, :, :QK_DIM].reshape(Q_LORA, N_HEADS_MLA * QK_DIM)
    dcqn = ex.mm(dq_raw, w_uq_p, tb=True, name="mmb_uq")
    gw["w_ukv"] = ex.mm(ckvn, dkv, ta=True, name="mmg_ukv")
    dckvn = ex.mm(dkv, W["w_ukv"], tb=True, name="mmb_ukv")

    def qkv_norm_bwd(i, n, cq, ckv, dq_, dkv_, gq, gkv):
        dcq, gq_rows = _norm_bwd(cq, _rstd(cq), gq, dq_)
        dckv, gkv_rows = _norm_bwd(ckv, _rstd(ckv), gkv, dkv_)
        return jnp.concatenate([dcq, dckv], axis=1), _colsum(gq_rows), _colsum(gkv_rows)

    dcqkv, gs["q_norm"], gs["kv_norm"] = _rows(
        qkv_norm_bwd, S, 512, [_row(cqkv, Q_LORA, 0), _row(cqkv, KV_LORA, 1), _row(dcqn), _row(dckvn), _full(sp["q_norm"]), _full(sp["kv_norm"])],
        [("row", Q_LORA + KV_LORA, BF16), ("acc", (1, Q_LORA), F32), ("acc", (1, KV_LORA), F32)], "qkv_norm_bwd")

    def gated_norm_bwd(i, n, y, xs, z, dssm, dsk, gn):
        yt, sz, yg = gated(y, xs, z, dsk)
        dyg_parts, gn_parts = [], []
        for g in range(SSM_GROUPS):
            sl = slice(g * GN, (g + 1) * GN)
            blk = yg[:, sl]
            dblk, rows = _norm_bwd(blk, _rstd(blk), gn[:, sl], dssm[:, sl])
            dyg_parts.append(dblk)
            gn_parts.append(_colsum(rows))
        dyg = jnp.concatenate(dyg_parts, axis=1)
        dyt = dyg * (z * sz)
        dz = dyg * yt * (sz * (1.0 + z * (1.0 - sz)))
        return dyt, dz, dyt * dsk, jnp.concatenate(gn_parts, axis=1), _colsum(dyt * xs)

    dy, dz, dx_skip, gs["ssm_norm"], g_dskip_ch = _rows(
        gated_norm_bwd, S, 128, [_row(y), _row(xbc_c, D_INNER, 0), _row(z), _row(dssm), _full(dskip_ch), _full(sp["ssm_norm"])],
        [("row", D_INNER, F32), ("row", D_INNER, BF16), ("row", D_INNER, F32), ("acc", (1, D_INNER), F32), ("acc", (1, D_INNER), F32)],
        "gated_norm_bwd")
    gs["d_skip"] = jnp.sum(g_dskip_ch.reshape(N_HEADS_SSM, SSM_HEADDIM), axis=1).reshape(1, N_HEADS_SSM)
    dxs, dbm, dcm, ddt_x, dcum = _ssd_bwd2(xbc_c, dt, cum, cumt_g, spread, states, dy, dx_skip)
    ddt_raw, g_dtb, g_alog = _dt_bwd2(dt_raw, dt_bias_p, a_log_p, ddt_x, dcum)
    gs["dt_bias"] = g_dtb[:, :N_HEADS_SSM]
    gs["a_log"] = g_alog[:, :N_HEADS_SSM]
    dxbc_c = jnp.concatenate([dxs, dbm, dcm], axis=1)
    dxbc, g_conv_w8, gs["conv_b"] = _conv_bwd(xbc, dxbc_c, sp["conv_w"], sp["conv_b"])
    gs["conv_w"] = g_conv_w8[:CONV_WIDTH]

    g_qkv = ex.mm(u_bf, dcqkv, ta=True, name="mmg_qkv")
    g_kr = ex.mm(u_bf, dkr, ta=True, name="mmg_kr")
    g_z = ex.mm(u_bf, dz, ta=True, name="mmg_z")
    g_xbc = ex.mm(u_bf, dxbc, ta=True, name="mmg_xbc")
    g_dt = ex.mm(u_bf, ddt_raw, ta=True, name="mmg_dt")
    g_g = ex.mm(u_bf, dgates, ta=True, name="mmg_gates")
    gw["w_in"] = jnp.concatenate([g_qkv, g_kr[:, :QK_ROPE], g_z, g_xbc, g_dt[:, :N_HEADS_SSM], g_g], axis=1)
    du = ex.mm(dcqkv, wp["qkv"], tb=True, name="mmb_qkv")
    du = ex.mm(dkr, wp["kr"], tb=True, add=du, name="mmb_kr")
    du = ex.mm(ddt_raw, wp["dt"], tb=True, add=du, name="mmb_dt")
    du = ex.mm(dz, wp["z"], tb=True, add=du, name="mmb_z")
    du = ex.mm(dxbc, wp["xbc"], tb=True, add=du, name="mmb_xbc")
    du = ex.mm(dgates, wp["g"], tb=True, add=du, name="mmb_gates")

    def pre_bwd(i, n, x, du, dh, g):
        dx, rows = _norm_bwd(x, _rstd(x), g, du)
        return dh + dx, _colsum(rows)

    grad_x, gs["mix_norm_pre"] = _rows(pre_bwd, S, RW, [_row(x), _row(du), _row(dh1), _full(sp["mix_norm_pre"])],
                                       [("row", D_MODEL, F32), ("acc", (1, D_MODEL), F32)], "norm_pre_bwd")
    return loss, grad_x, gs


BIG = (
    ("w_in", (2048, 3872), 1), ("w_uq", (512, 768), 1), ("w_ukv", (512, 1024), 1), ("w_attn_o", (512, 2048), 0),
    ("w_ssm_o", (1024, 2048), 0), ("w_out", (512, 2048), 0), ("w_gate", (2048, 1408), 1), ("w_up", (2048, 1408), 1),
    ("w_down", (1408, 2048), 0), ("w_ple_gate", (512, 2048), 0), ("w_ple", (256, 512), 1),
)
SMALL = (
    ("mix_norm_pre", 2048), ("mix_norm_post", 2048), ("q_norm", 512), ("kv_norm", 512), ("conv_b", 6144), ("dt_bias", 64),
    ("a_log", 64), ("d_skip", 64), ("ssm_norm", 4096), ("ffn_norm_pre", 2048), ("ffn_norm_post", 2048),
    ("ple_norm_pre", 2048), ("ple_norm_post", 2048),
)
CONV_W_LEN = CONV_WIDTH * CONV_DIM
SMALL_ROWS = 384


def _place():
    return lax.axis_index("x"), lax.axis_index("y"), lax.axis_index("c")


def _flip(v, bit):
    return 1 - v if bit else v


def _alone(hook, name):
    n_in, n_out = len(hook.ins), len(hook.out_shapes)

    def body(*refs):
        start, finish = hook.make(refs[:n_in], refs[n_in:n_in + n_out], refs[n_in + n_out:])
        start()
        finish()

    return list(pl.pallas_call(
        body, name=name, out_shape=tuple(hook.out_shapes),
        in_specs=[pl.BlockSpec(memory_space=pl.ANY)] * n_in,
        out_specs=tuple(pl.BlockSpec(memory_space=pl.ANY) for _ in range(n_out)),
        scratch_shapes=[pltpu.SemaphoreType.DMA((s,)) for s in hook.sems],
        input_output_aliases=hook.aliases,
    )(*hook.ins))


def _simple(copies):
    def start():
        for cp in copies:
            cp.start()

    def finish():
        for cp in copies:
            cp.wait()

    return start, finish


def _gather_hook(shards):
    n = len(shards)

    def make(ins, outs, sems):
        send_sems, recv_sems, fwd_send_sems, fwd_recv_sems = sems
        x, y, c = _place()
        me = 2 * x + y
        far, near = [], []
        for a in range(n):
            half = shards[a].shape[0] // 2
            lo = pl.multiple_of(c * half, SUBLANE)
            for k in (1, 2, 3):
                px, py = _flip(x, k >> 1), _flip(y, k & 1)
                far.append(pltpu.make_async_remote_copy(
                    src_ref=ins[a].at[pl.ds(lo, half), :], dst_ref=outs[a].at[me, pl.ds(lo, half), :],
                    send_sem=send_sems.at[3 * a + k - 1], recv_sem=recv_sems.at[3 * a + k - 1],
                    device_id=(px, py, c), device_id_type=MESH_ID))
                got = outs[a].at[2 * px + py, pl.ds(lo, half), :]
                near.append(pltpu.make_async_remote_copy(
                    src_ref=got, dst_ref=got, send_sem=fwd_send_sems.at[3 * a + k - 1], recv_sem=fwd_recv_sems.at[3 * a + k - 1],
                    device_id=(x, y, 1 - c), device_id_type=MESH_ID))

        def start():
            for cp in far:
                cp.start()

        def finish():
            for cp, fwd in zip(far, near):
                cp.wait_recv()
                fwd.start()
            for cp, fwd in zip(far, near):
                cp.wait_send()
                fwd.wait()

        return start, finish

    return _Hook(shards, [jax.ShapeDtypeStruct((N_CHIPS, *s.shape), s.dtype) for s in shards], (3 * n,) * 4, make)


def _swap_hook(gs):
    n = len(gs)

    def make(ins, outs, sems):
        send_sems, recv_sems = sems
        x, y, c = _place()
        copies = []
        for a in range(n):
            half = gs[a].shape[1] // 2
            src = ins[a].at[:, pl.ds(pl.multiple_of((1 - c) * half, SUBLANE), half), :]
            copies.append(pltpu.make_async_remote_copy(
                src_ref=src, dst_ref=outs[a], send_sem=send_sems.at[a], recv_sem=recv_sems.at[a],
                device_id=(x, y, 1 - c), device_id_type=MESH_ID))
        return _simple(copies)

    return _Hook(gs, [jax.ShapeDtypeStruct((g.shape[0], g.shape[1] // 2, g.shape[2]), g.dtype) for g in gs], (n, n), make)


def _sum_rows_tile(rows, cols):
    return _tile(rows, max(2 * SUBLANE, (512 * 1024 // cols) // (2 * SUBLANE) * (2 * SUBLANE)), 2 * SUBLANE)


def _add_half(g, other, c, name):
    n, R, C = g.shape
    half = R // 2
    tr = _sum_rows_tile(half, C)
    nb = half // tr

    def body(c_ref, g_ref, o_ref, out_ref):
        out_ref[...] = (g_ref[...] + o_ref[...]).astype(out_ref.dtype)

    return pl.pallas_call(
        body, name=name,
        out_shape=jax.ShapeDtypeStruct((n, half, C), BF16),
        grid_spec=pltpu.PrefetchScalarGridSpec(
            num_scalar_prefetch=1, grid=(n, nb),
            in_specs=[pl.BlockSpec((1, tr, C), lambda j, i, c_ref: (j, c_ref[0] * nb + i, 0)),
                      pl.BlockSpec((1, tr, C), lambda j, i, c_ref: (j, i, 0))],
            out_specs=pl.BlockSpec((1, tr, C), lambda j, i, c_ref: (j, i, 0))),
        compiler_params=_params(("parallel", "parallel")),
    )(c, g, other)


def _scatter_hook(parts):
    n = len(parts)

    def make(ins, outs, sems):
        send_sems, recv_sems = sems
        x, y, c = _place()
        copies = []
        for a in range(n):
            for k in (1, 2, 3):
                px, py = _flip(x, k >> 1), _flip(y, k & 1)
                copies.append(pltpu.make_async_remote_copy(
                    src_ref=ins[a].at[2 * px + py], dst_ref=outs[a].at[k - 1], send_sem=send_sems.at[3 * a + k - 1],
                    recv_sem=recv_sems.at[3 * a + k - 1], device_id=(px, py, c), device_id_type=MESH_ID))
        return _simple(copies)

    return _Hook(parts, [jax.ShapeDtypeStruct((3, *p.shape[1:]), p.dtype) for p in parts], (3 * n, 3 * n), make)


def _add_chips(part, got, place, name):
    n, R, C = part.shape
    tr = _sum_rows_tile(R, C)
    nb = R // tr

    def body(place_ref, p_ref, g_ref, out_ref):
        out_ref[...] = ((p_ref[0].astype(F32) + g_ref[0].astype(F32)) + g_ref[1].astype(F32)) + g_ref[2].astype(F32)

    return pl.pallas_call(
        body, name=name,
        out_shape=jax.ShapeDtypeStruct((2 * R, C), F32),
        grid_spec=pltpu.PrefetchScalarGridSpec(
            num_scalar_prefetch=1, grid=(nb,),
            in_specs=[pl.BlockSpec((1, tr, C), lambda i, place_ref: (place_ref[0], i, 0)),
                      pl.BlockSpec((3, tr, C), lambda i, place_ref: (0, i, 0))],
            out_specs=pl.BlockSpec((tr, C), lambda i, place_ref: (place_ref[1] * nb + i, 0))),
        compiler_params=_params(("parallel",)),
    )(place, part, got)


def _join_hook(wholes):
    n = len(wholes)

    def make(ins, outs, sems):
        send_sems, recv_sems = sems
        x, y, c = _place()
        copies = []
        for a in range(n):
            half = wholes[a].shape[0] // 2
            rows = outs[a].at[pl.ds(pl.multiple_of(c * half, SUBLANE), half), :]
            copies.append(pltpu.make_async_remote_copy(
                src_ref=rows, dst_ref=rows, send_sem=send_sems.at[a], recv_sem=recv_sems.at[a],
                device_id=(x, y, 1 - c), device_id_type=MESH_ID))
        return _simple(copies)

    return _Hook(wholes, [jax.ShapeDtypeStruct(w.shape, w.dtype) for w in wholes], (n, n), make, aliases={a: a for a in range(n)})


def _allreduce_small(vec, name):
    R, C = vec.shape

    def body(v_ref, o_ref, buf, send_sems, recv_sems):
        x, y, c = _place()
        me = 4 * x + 2 * y + c
        buf[me] = v_ref[...]
        copies = []
        for k in range(1, N_DEV):
            peer = (_flip(x, (k >> 2) & 1), _flip(y, (k >> 1) & 1), _flip(c, k & 1))
            copies.append(pltpu.make_async_remote_copy(
                src_ref=v_ref, dst_ref=buf.at[me], send_sem=send_sems.at[k - 1], recv_sem=recv_sems.at[k - 1],
                device_id=peer, device_id_type=MESH_ID))
        for cp in copies:
            cp.start()
        for cp in copies:
            cp.wait()
        tot = buf[0]
        for d in range(1, N_DEV):
            tot = tot + buf[d]
        o_ref[...] = tot

    return pl.pallas_call(
        body, name=name,
        out_shape=jax.ShapeDtypeStruct((R, C), F32),
        in_specs=[pl.BlockSpec(memory_space=pltpu.VMEM)],
        out_specs=pl.BlockSpec(memory_space=pltpu.VMEM),
        scratch_shapes=[pltpu.VMEM((N_DEV, R, C), F32), pltpu.SemaphoreType.DMA((N_DEV - 1,)), pltpu.SemaphoreType.DMA((N_DEV - 1,))],
    )(vec)


def _unstack(gathered, shape, axis):
    if axis == 0:
        return gathered.reshape(N_CHIPS * shape[0], shape[1])
    return jnp.concatenate([gathered[j] for j in range(N_CHIPS)], axis=1)


def _stack(whole, shape, axis):
    if axis == 0:
        return whole.reshape(N_CHIPS, shape[0], shape[1])
    return jnp.stack([whole[:, j * shape[1]:(j + 1) * shape[1]] for j in range(N_CHIPS)])


GATHER_FIRST = ("w_in", "w_uq", "w_ukv")
GATHER_IN = {"mm_z": ("w_attn_o", "w_ssm_o", "w_out"), "mm_xbc": ("w_gate", "w_up"), "mm_gates": ("w_down", "w_ple_gate", "w_ple")}
REDUCE = (
    (("w_ple", "w_ple_gate", "w_down"), "mmb_down", "mmg_gate", "mmg_up"),
    (("w_gate", "w_up"), "mmb_gate", "mmb_up", "mmg_out"),
    (("w_out", "w_attn_o", "w_ssm_o"), "mmb_ssm_o", "mmg_z", "mmg_xbc"),
    (("w_uq", "w_ukv"), "mmb_ukv", "mmg_gates", "mmb_z"),
    (("w_in",), "mmb_z", "mmb_xbc", "mmb_gates"),
)


class _Exchange:
    def __init__(self, shards, chip, core):
        self.shards, self.chip = shards, chip
        self.core_arr = core.reshape(1).astype(jnp.int32)
        self.place_arr = jnp.stack([chip, core]).astype(jnp.int32)
        self.shape = {n: (shape, axis) for n, shape, axis in BIG}
        self.whole, self.grads, self.reduced, self.pending, self.tails = {}, {}, {}, {}, 0
        hook, done = self._gather(GATHER_FIRST)
        done(_alone(hook, "gather_first"))
        for host, names in GATHER_IN.items():
            self._arm(host, *self._gather(names))

    def _arm(self, host, hook, done):
        self.pending.setdefault(host, []).append((hook, done))

    def _gather(self, names):
        shards = [self.shards[n].astype(BF16) for n in names]

        def done(outs):
            for n, s, g in zip(names, shards, outs):
                self.whole[n] = _unstack(lax.dynamic_update_slice(g, s[None], (self.chip, 0, 0)), *self.shape[n])

        return _gather_hook(shards), done

    def __getitem__(self, name):
        return self.whole[name]

    def __setitem__(self, name, grad):
        self.grads[name] = grad
        for names, swap_host, scatter_host, join_host in REDUCE:
            if name in names and all(n in self.grads for n in names):
                self._reduce(names, swap_host, scatter_host, join_host)

    def _reduce(self, names, swap_host, scatter_host, join_host):
        stacked = [_stack(self.grads[n], *self.shape[n]) for n in names]

        def joined(outs):
            for n, r in zip(names, outs):
                self.reduced[n] = r.reshape(1, *self.shape[n][0])

        def swapped(outs):
            parts = [_add_half(g, o, self.core_arr, "add_half_" + n) for n, g, o in zip(names, stacked, outs)]

            def scattered(gots):
                wholes = [_add_chips(q, o, self.place_arr, "add_chips_" + n) for n, q, o in zip(names, parts, gots)]
                self._arm(join_host, _join_hook(wholes), joined)

            self._arm(scatter_host, _scatter_hook(parts), scattered)

        self._arm(swap_host, _swap_hook(stacked), swapped)

    def _run(self, todo, call):
        hook = _merge_hooks([h for h, _ in todo])
        result, outs = call(hook)
        off = 0
        for h, done in todo:
            done(outs[off:off + len(h.out_shapes)])
            off += len(h.out_shapes)
        return result

    def mm(self, a, b, *, name, **kw):
        todo = self.pending.pop(name, None)
        if not todo:
            return _mm(a, b, name=name, **kw)
        return self._run(todo, lambda hook: _mm(a, b, name=name, hook=hook, **kw))

    def finish(self):
        while self.pending:
            todo = self.pending.pop(next(iter(self.pending)))
            self.tails += 1
            self._run(todo, lambda hook: (None, _alone(hook, "exchange_tail_%d" % self.tails)))
        return self.reduced


def kernel(x, p, positions, mix_norm_pre, mix_norm_post, w_in, q_norm, w_uq, kv_norm, w_ukv, conv_w, conv_b, dt_bias, a_log, d_skip, ssm_norm, w_attn_o, w_ssm_o, w_out, ffn_norm_pre, ffn_norm_post, w_gate, w_up, w_down, ple_norm_pre, ple_norm_post, w_ple_gate, w_ple, loss_target, m_mix_norm_pre, m_mix_norm_post, m_w_in, m_q_norm, m_w_uq, m_kv_norm, m_w_ukv, m_conv_w, m_conv_b, m_dt_bias, m_a_log, m_d_skip, m_ssm_norm, m_w_attn_o, m_w_ssm_o, m_w_out, m_ffn_norm_pre, m_ffn_norm_post, m_w_gate, m_w_up, m_w_down, m_ple_norm_pre, m_ple_norm_post, m_w_ple_gate, m_w_ple, v_mix_norm_pre, v_mix_norm_post, v_w_in, v_q_norm, v_w_uq, v_kv_norm, v_w_ukv, v_conv_w, v_conv_b, v_dt_bias, v_a_log, v_d_skip, v_ssm_norm, v_w_attn_o, v_w_ssm_o, v_w_out, v_ffn_norm_pre, v_ffn_norm_post, v_w_gate, v_w_up, v_w_down, v_ple_norm_pre, v_ple_norm_post, v_w_ple_gate, v_w_ple):
    given = dict(locals())
    names = [n for n, _, _ in BIG] + [n for n, _ in SMALL] + ["conv_w"]
    order = ["mix_norm_pre", "mix_norm_post", "w_in", "q_norm", "w_uq", "kv_norm", "w_ukv", "conv_w", "conv_b", "dt_bias", "a_log",
             "d_skip", "ssm_norm", "w_attn_o", "w_ssm_o", "w_out", "ffn_norm_pre", "ffn_norm_post", "w_gate", "w_up", "w_down",
             "ple_norm_pre", "ple_norm_post", "w_ple_gate", "w_ple"]
    assert sorted(names) == sorted(order)
    cx, cy, cc = _place()
    chip = 2 * cx + cy
    conv_cols = CONV_DIM // N_CHIPS

    ex = _Exchange({n: given[n][0] for n, _, _ in BIG}, chip, cc)
    own = jnp.where(cc == 0, conv_w[0], 0.0)
    conv_vec = lax.dynamic_update_slice(jnp.zeros((CONV_WIDTH, CONV_DIM), F32), own, (0, chip * conv_cols))
    conv_full = _allreduce_small(conv_vec.reshape(CONV_W_LEN // LANE, LANE), "gather_conv_w").reshape(CONV_WIDTH, CONV_DIM)
    sp = {n: given[n] for n, _ in SMALL}
    sp["conv_w"] = conv_full

    loss_part, grad_x, gs = _local_step(x[0], p[0, 0], positions[0], ex, sp, loss_target[0])

    g_big = ex.finish()

    small_parts = [gs[n] for n, _ in SMALL] + [gs["conv_w"], loss_part[:, :1]]
    small_vec = jnp.concatenate([t.reshape(-1) for t in small_parts])
    small_vec = jnp.pad(small_vec, (0, SMALL_ROWS * LANE - small_vec.shape[0])).reshape(SMALL_ROWS, LANE)
    small_sum = _allreduce_small(small_vec, "allreduce_small").reshape(-1)
    g_small, off = {}, 0
    for n, length in SMALL:
        g_small[n] = small_sum[off:off + length].reshape(1, length)
        off += length
    g_conv = small_sum[off:off + CONV_W_LEN].reshape(CONV_WIDTH, CONV_DIM)
    g_small["conv_w"] = lax.dynamic_slice(g_conv, (0, chip * conv_cols), (CONV_WIDTH, conv_cols)).reshape(1, CONV_WIDTH, conv_cols)
    loss = small_sum[off + CONV_W_LEN]

    grads, deltas, new_m, new_v = [], [], [], []
    for n in order:
        g = g_big[n] if n in g_big else g_small[n]
        d, m_, v_ = _adamw(given[n], g, given["m_" + n], given["v_" + n], "adamw_" + n)
        grads.append(g)
        deltas.append(d)
        new_m.append(m_)
        new_v.append(v_)
    return (loss, grad_x.reshape(x.shape), *grads, *deltas, *new_m, *new_v)
```

```python
import functools
import math

import numpy as np
import jax
import jax.numpy as jnp
from jax import lax
from jax.experimental import pallas as pl
from jax.experimental.pallas import tpu as pltpu

F32 = jnp.float32
BF16 = jnp.bfloat16

D_MODEL = 2048
N_HEADS_MLA = 16
Q_LORA = 512
KV_LORA = 512
QK_NOPE = 128
QK_ROPE = 64
V_DIM = 128
QK_DIM = QK_NOPE + QK_ROPE
ROPE_THETA = 10000.0
D_INNER = 4096
SSM_HEADDIM = 64
N_HEADS_SSM = 64
SSM_GROUPS = 8
HEADS_PER_GROUP = 8
D_STATE = 128
CONV_WIDTH = 4
CHUNK = 256
CONV_DIM = D_INNER + 2 * SSM_GROUPS * D_STATE
D_FF = 5632
PLE_DIM = 256
EPS = 1e-6
IN_SPLITS = (Q_LORA, KV_LORA, QK_ROPE, D_INNER, CONV_DIM, N_HEADS_SSM, D_MODEL, D_MODEL)

ADAM_LR = 0.001
ADAM_B1 = 0.9
ADAM_B2 = 0.999
ADAM_EPS = 1e-08
ADAM_WD = 0.01
ADAM_STEP = 10

LANE = 128
SUBLANE = 8
HEAD_PAD = 256
VMEM_LIMIT = 56 * 1024 * 1024
ATTN_TILE = 1024
NEG = -1e30

MESH_ID = pl.DeviceIdType.MESH
N_CHIPS = 4
N_DEV = 8


def _tile(n, pref, mult=LANE):
    if n <= pref:
        return n
    t = (pref // mult) * mult
    while t >= mult:
        if n % t == 0:
            return t
        t -= mult
    return n


def _params(sem, vmem=VMEM_LIMIT):
    return pltpu.CompilerParams(dimension_semantics=sem, vmem_limit_bytes=vmem)


class _Hook:
    def __init__(self, ins, out_shapes, sems, make, aliases=None):
        self.ins, self.out_shapes, self.sems, self.make, self.aliases = list(ins), list(out_shapes), tuple(sems), make, dict(aliases or {})


def _merge_hooks(hooks):
    hooks = [h for h in hooks if h is not None]
    if not hooks:
        return None
    ins, outs, sems, aliases, cuts = [], [], [], {}, []
    for h in hooks:
        cuts.append((len(ins), len(outs), len(sems)))
        aliases.update({len(ins) + i: len(outs) + o for i, o in h.aliases.items()})
        ins += h.ins
        outs += h.out_shapes
        sems += h.sems

    def make(in_refs, out_refs, sem_refs):
        pairs = []
        for h, (i0, o0, s0) in zip(hooks, cuts):
            pairs.append(h.make(in_refs[i0:i0 + len(h.ins)], out_refs[o0:o0 + len(h.out_shapes)], sem_refs[s0:s0 + len(h.sems)]))

        def start():
            for st, _ in pairs:
                st()

        def finish():
            for _, fin in pairs:
                fin()

        return start, finish

    return _Hook(ins, outs, sems, make, aliases)


def _mm(a, b, *, ta=False, tb=False, add=None, out_dtype=F32, name, tm=1024, tn=1024, tk=2048, hook=None):
    if ta:
        K, M = a.shape
    else:
        M, K = a.shape
    N = b.shape[0] if tb else b.shape[1]
    assert (b.shape[1] if tb else b.shape[0]) == K, (a.shape, b.shape, ta, tb)
    tm, tn, tk = _tile(M, tm), _tile(N, tn), _tile(K, tk)
    nk = K // tk
    dn = (((0 if ta else 1,), (1 if tb else 0,)), ((), ()))
    has_add = add is not None
    n_own = 3 if has_add else 2
    n_hin = len(hook.ins) if hook else 0
    n_hout = len(hook.out_shapes) if hook else 0
    grid = (M // tm, N // tn, nk)

    def body(*refs):
        a_ref, b_ref = refs[:2]
        c_ref = refs[2] if has_add else None
        o_ref = refs[n_own + n_hin]
        scratch = refs[n_own + n_hin + 1 + n_hout:]
        if hook:
            start, finish = hook.make(refs[n_own:n_own + n_hin], refs[n_own + n_hin + 1:n_own + n_hin + 1 + n_hout],
                                      scratch[len(scratch) - len(hook.sems):])
            ids = [pl.program_id(d) for d in range(3)]
            pl.when((ids[0] == 0) & (ids[1] == 0) & (ids[2] == 0))(start)
        prod = lax.dot_general(a_ref[...].astype(BF16), b_ref[...].astype(BF16), dn, preferred_element_type=F32)
        if nk == 1:
            o_ref[...] = ((c_ref[...] + prod) if has_add else prod).astype(out_dtype)
        else:
            acc = scratch[0]
            k = pl.program_id(2)

            @pl.when(k == 0)
            def _():
                acc[...] = (c_ref[...] + prod) if has_add else prod

            @pl.when(k > 0)
            def _():
                acc[...] += prod

            @pl.when(k == nk - 1)
            def _():
                o_ref[...] = acc[...].astype(out_dtype)
        if hook:
            pl.when((ids[0] == grid[0] - 1) & (ids[1] == grid[1] - 1) & (ids[2] == grid[2] - 1))(finish)

    a_spec = pl.BlockSpec((tk, tm), lambda i, j, k: (k, i)) if ta else pl.BlockSpec((tm, tk), lambda i, j, k: (i, k))
    b_spec = pl.BlockSpec((tn, tk), lambda i, j, k: (j, k)) if tb else pl.BlockSpec((tk, tn), lambda i, j, k: (k, j))
    in_specs = [a_spec, b_spec]
    args = [a, b]
    if has_add:
        in_specs.append(pl.BlockSpec((tm, tn), lambda i, j, k: (i, j)))
        args.append(add)
    hbm = pl.BlockSpec(memory_space=pl.ANY)
    scratch_shapes = [pltpu.VMEM((tm, tn), F32)] if nk > 1 else []
    out_shape = jax.ShapeDtypeStruct((M, N), out_dtype)
    out_spec = pl.BlockSpec((tm, tn), lambda i, j, k: (i, j))
    if not hook:
        return pl.pallas_call(
            body, name=name, out_shape=out_shape, grid=grid, in_specs=in_specs, out_specs=out_spec,
            scratch_shapes=scratch_shapes, compiler_params=_params(("parallel", "parallel", "arbitrary")),
        )(*args)
    outs = pl.pallas_call(
        body, name=name, out_shape=(out_shape, *hook.out_shapes), grid=grid,
        in_specs=in_specs + [hbm] * n_hin, out_specs=(out_spec, *[hbm] * n_hout),
        scratch_shapes=scratch_shapes + [pltpu.SemaphoreType.DMA((s,)) for s in hook.sems],
        input_output_aliases={n_own + i: 1 + o for i, o in hook.aliases.items()},
        compiler_params=_params(("arbitrary", "arbitrary", "arbitrary")),
    )(*args, *hook.ins)
    return outs[0], list(outs[1:])


def _row(arr, width=None, cblk=0):
    return ("row", arr, arr.shape[1] if width is None else width, cblk)


def _full(arr):
    return ("full", arr)


def _prev8(arr):
    return ("prev8", arr)


def _next8(arr):
    return ("next8", arr)


def _rows(fn, n_rows, tm, ins, outs, name):
    tm = min(tm, n_rows)
    assert n_rows % tm == 0 and tm % SUBLANE == 0
    n = n_rows // tm
    in_specs, args = [], []
    for spec in ins:
        kind, arr = spec[0], spec[1]
        if kind == "row":
            _, _, w, cb = spec
            in_specs.append(pl.BlockSpec((tm, w), lambda i, cb=cb: (i, cb)))
        elif kind == "full":
            in_specs.append(pl.BlockSpec(arr.shape, lambda i, nd=arr.ndim: (0,) * nd))
        elif kind == "prev8":
            in_specs.append(pl.BlockSpec((SUBLANE, arr.shape[1]),
                                         lambda i: (jnp.maximum(i * (tm // SUBLANE) - 1, 0), 0)))
        elif kind == "next8":
            last = n_rows // SUBLANE - 1
            in_specs.append(pl.BlockSpec((SUBLANE, arr.shape[1]),
                                         lambda i: (jnp.minimum((i + 1) * (tm // SUBLANE), last), 0)))
        else:
            raise ValueError(kind)
        args.append(arr)
    out_shapes, out_specs = [], []
    any_acc = False
    for spec in outs:
        if spec[0] == "row":
            _, w, dt = spec
            out_shapes.append(jax.ShapeDtypeStruct((n_rows, w), dt))
            out_specs.append(pl.BlockSpec((tm, w), lambda i: (i, 0)))
        else:
            _, shp, dt = spec
            any_acc = True
            out_shapes.append(jax.ShapeDtypeStruct(shp, dt))
            out_specs.append(pl.BlockSpec(shp, lambda i, nd=len(shp): (0,) * nd))
    nin = len(ins)

    def body(*refs):
        i = pl.program_id(0)
        vals = fn(i, n, *[r[...] for r in refs[:nin]])
        for o_ref, spec, v in zip(refs[nin:], outs, vals):
            if spec[0] == "acc":
                @pl.when(i == 0)
                def _(o_ref=o_ref):
                    o_ref[...] = jnp.zeros_like(o_ref)

                o_ref[...] += v.astype(o_ref.dtype)
            else:
                o_ref[...] = v.astype(o_ref.dtype)

    res = pl.pallas_call(
        body, name=name,
        out_shape=tuple(out_shapes),
        grid=(n,),
        in_specs=in_specs,
        out_specs=tuple(out_specs),
        compiler_params=_params(("arbitrary",) if any_acc else ("parallel",)),
    )(*args)
    return res


def _rstd(x):
    return lax.rsqrt(jnp.mean(x * x, axis=-1, keepdims=True) + EPS)


def _norm_bwd(x, r, g, dy):
    xh = x * r
    dyg = dy * g
    dx = r * (dyg - xh * jnp.mean(dyg * xh, axis=-1, keepdims=True))
    return dx, dy * xh


def _sigmoid(x):
    return 1.0 / (1.0 + jnp.exp(-x))


def _colsum(v):
    return jnp.sum(v, axis=0, keepdims=True)


def _rope_tables(pos, invf):
    ang = pos.astype(F32) * invf
    lane = lax.broadcasted_iota(jnp.int32, ang.shape, 1)
    cos, sin = jnp.cos(ang), jnp.sin(ang)
    ct = jnp.where(lane < QK_ROPE, cos, 0.0)
    sa = jnp.where(lane < QK_ROPE // 2, -sin, 0.0)
    sb = jnp.where((lane >= QK_ROPE // 2) & (lane < QK_ROPE), sin, 0.0)
    return ct, sa, sb


def _rope(b, ct, sa, sb):
    return ct * b + sa * pltpu.roll(b, LANE - QK_ROPE // 2, 1) + sb * pltpu.roll(b, QK_ROPE // 2, 1)


def _rope_t(d, ct, sa, sb):
    return ct * d + pltpu.roll(sa * d, QK_ROPE // 2, 1) + pltpu.roll(sb * d, LANE - QK_ROPE // 2, 1)


def _rope_fwd(q_raw, kr_pad, pos_col, invf):
    S = q_raw.shape[0]

    def fn(i, n, q, kr, pos, invf):
        ct, sa, sb = _rope_tables(pos, invf)
        parts = []
        for h in range(N_HEADS_MLA):
            parts.append(q[:, h * HEAD_PAD:h * HEAD_PAD + LANE])
            parts.append(_rope(q[:, h * HEAD_PAD + LANE:(h + 1) * HEAD_PAD], ct, sa, sb))
        return jnp.concatenate(parts, axis=1), _rope(kr, ct, sa, sb)

    return _rows(fn, S, 256, [_row(q_raw), _row(kr_pad), _row(pos_col), _full(invf)],
                 [("row", N_HEADS_MLA * HEAD_PAD, BF16), ("row", LANE, BF16)], "rope_fwd")


def _rope_bwd(dq, dkp, pos_col, invf):
    S = dq.shape[0]
    tm = 256

    def body(dq_ref, dkp_ref, pos_ref, invf_ref, dqo_ref, dkr_ref):
        ct, sa, sb = _rope_tables(pos_ref[...], invf_ref[...])
        for h in range(N_HEADS_MLA):
            dqo_ref[:, h * HEAD_PAD:h * HEAD_PAD + LANE] = dq_ref[:, h * HEAD_PAD:h * HEAD_PAD + LANE].astype(BF16)
            dqo_ref[:, h * HEAD_PAD + LANE:(h + 1) * HEAD_PAD] = _rope_t(
                dq_ref[:, h * HEAD_PAD + LANE:(h + 1) * HEAD_PAD], ct, sa, sb).astype(BF16)
        tot = dkp_ref[0]
        for h in range(1, N_HEADS_MLA):
            tot = tot + dkp_ref[h]
        dkr_ref[...] = _rope_t(tot, ct, sa, sb).astype(BF16)

    return pl.pallas_call(
        body, name="rope_bwd",
        out_shape=(jax.ShapeDtypeStruct(dq.shape, BF16), jax.ShapeDtypeStruct((S, LANE), BF16)),
        grid=(S // tm,),
        in_specs=[pl.BlockSpec((tm, dq.shape[1]), lambda i: (i, 0)),
                  pl.BlockSpec((N_HEADS_MLA, tm, LANE), lambda i: (0, i, 0)),
                  pl.BlockSpec((tm, 1), lambda i: (i, 0)),
                  pl.BlockSpec((1, LANE), lambda i: (0, 0))],
        out_specs=(pl.BlockSpec((tm, dq.shape[1]), lambda i: (i, 0)), pl.BlockSpec((tm, LANE), lambda i: (i, 0))),
        compiler_params=_params(("parallel",)),
    )(dq, dkp, pos_col, invf)


def _row_of(col, n):
    eye = lax.broadcasted_iota(jnp.int32, (LANE, LANE), 0) == lax.broadcasted_iota(jnp.int32, (LANE, LANE), 1)
    parts = [jnp.sum(jnp.where(eye, col[i:i + LANE], 0.0), axis=0, keepdims=True) for i in range(0, n, LANE)]
    return parts[0] if len(parts) == 1 else jnp.concatenate(parts, axis=1)


def _attn_fwd(q, kv, kp, tile):
    S = q.shape[0]
    nq = S // tile
    scale = QK_DIM ** -0.5
    nt = (((1,), (1,)), ((), ()))

    def body(q_ref, kv_ref, kp_ref, o_ref, lse_ref, m_s, l_s, acc_s, s_buf):
        qi = pl.program_id(1)
        qv = q_ref[...]
        m_s[...] = jnp.full_like(m_s, NEG)
        l_s[...] = jnp.zeros_like(l_s)
        acc_s[...] = jnp.zeros_like(acc_s)

        def scores(j):
            start = pl.multiple_of(j * tile, tile)
            k = jnp.concatenate([kv_ref[pl.ds(start, tile), 0:LANE], kp_ref[pl.ds(start, tile), :]], axis=1)
            return lax.dot_general(qv, k, nt, preferred_element_type=F32) * scale

        def update(s, j):
            v = kv_ref[pl.ds(pl.multiple_of(j * tile, tile), tile), LANE:2 * LANE]
            m_old = m_s[...]
            m_new = jnp.maximum(m_old, jnp.max(s, axis=1, keepdims=True))
            alpha = jnp.exp(m_old - m_new)
            p = jnp.exp(s - m_new)
            l_s[...] = alpha * l_s[...] + jnp.sum(p, axis=1, keepdims=True)
            acc_s[...] = alpha * acc_s[...] + jnp.dot(p.astype(BF16), v, preferred_element_type=F32)
            m_s[...] = m_new

        s_buf[0] = scores(0)

        def loop_body(j, carry):
            nxt = scores(j + 1)
            update(s_buf[lax.rem(j, 2)], j)
            s_buf[lax.rem(j + 1, 2)] = nxt
            return carry

        lax.fori_loop(0, qi, loop_body, 0)
        s = s_buf[lax.rem(qi, 2)]
        row = lax.broadcasted_iota(jnp.int32, s.shape, 0)
        col = lax.broadcasted_iota(jnp.int32, s.shape, 1)
        update(jnp.where(row >= col, s, NEG), qi)
        l = l_s[...]
        o_ref[...] = (acc_s[...] / l).astype(o_ref.dtype)
        lse_ref[0, 0] = _row_of(m_s[...] + jnp.log(l), tile)

    return pl.pallas_call(
        body, name="attn_fwd",
        out_shape=(jax.ShapeDtypeStruct((S, N_HEADS_MLA * V_DIM), BF16),
                   jax.ShapeDtypeStruct((N_HEADS_MLA, nq, 1, tile), F32)),
        grid=(N_HEADS_MLA, nq),
        in_specs=[pl.BlockSpec((tile, HEAD_PAD), lambda h, i: (i, h)),
                  pl.BlockSpec((S, HEAD_PAD), lambda h, i: (0, h)),
                  pl.BlockSpec((S, LANE), lambda h, i: (0, 0))],
        out_specs=(pl.BlockSpec((tile, V_DIM), lambda h, i: (i, h)),
                   pl.BlockSpec((1, 1, 1, tile), lambda h, i: (h, i, 0, 0))),
        scratch_shapes=[pltpu.VMEM((tile, 1), F32), pltpu.VMEM((tile, 1), F32), pltpu.VMEM((tile, V_DIM), F32),
                        pltpu.VMEM((2, tile, tile), F32)],
        compiler_params=_params(("parallel", "arbitrary")),
    )(q, kv, kp)


def _attn_delta(o, do, tile):
    S = o.shape[0]
    nq = S // tile

    def body(o_ref, do_ref, d_ref, dob_ref):
        dov = do_ref[...]
        prod = o_ref[...].astype(F32) * dov
        dob_ref[...] = dov.astype(BF16)
        for h in range(N_HEADS_MLA):
            col = jnp.sum(prod[:, h * V_DIM:(h + 1) * V_DIM], axis=1, keepdims=True)
            d_ref[h, 0] = _row_of(col, tile)

    return pl.pallas_call(
        body, name="attn_delta",
        out_shape=(jax.ShapeDtypeStruct((N_HEADS_MLA, nq, 1, tile), F32), jax.ShapeDtypeStruct(o.shape, BF16)),
        grid=(nq,),
        in_specs=[pl.BlockSpec((tile, o.shape[1]), lambda i: (i, 0)), pl.BlockSpec((tile, o.shape[1]), lambda i: (i, 0))],
        out_specs=(pl.BlockSpec((N_HEADS_MLA, 1, 1, tile), lambda i: (0, i, 0, 0)),
                   pl.BlockSpec((tile, o.shape[1]), lambda i: (i, 0))),
        compiler_params=_params(("parallel",)),
    )(o, do)


def _attn_bwd(q, kv, kp, do, lse, delta, tile):
    S = q.shape[0]
    nq = S // tile
    scale = QK_DIM ** -0.5
    nt = (((1,), (1,)), ((), ()))
    tn = (((0,), (0,)), ((), ()))

    def body(kv_ref, kp_ref, q_ref, do_ref, lse_ref, d_ref, dq_ref, dkv_ref, dkp_ref, dk_s, dv_s):
        ki = pl.program_id(1)
        k = jnp.concatenate([kv_ref[:, 0:LANE], kp_ref[...]], axis=1)
        v = kv_ref[:, LANE:2 * LANE]

        @pl.when(ki == 0)
        def _():
            dq_ref[...] = jnp.zeros_like(dq_ref)

        dk_s[...] = jnp.zeros_like(dk_s)
        dv_s[...] = jnp.zeros_like(dv_s)

        def step(qi, masked):
            start = pl.multiple_of(qi * tile, tile)
            qv = q_ref[pl.ds(start, tile), :]
            dov = do_ref[pl.ds(start, tile), :]
            st = lax.dot_general(k, qv, nt, preferred_element_type=F32) * scale
            pt = jnp.exp(st - lse_ref[0, qi])
            if masked:
                krow = lax.broadcasted_iota(jnp.int32, pt.shape, 0)
                qcol = lax.broadcasted_iota(jnp.int32, pt.shape, 1)
                pt = jnp.where(krow <= qcol, pt, 0.0)
            dv_s[...] += jnp.dot(pt.astype(BF16), dov, preferred_element_type=F32)
            dpt = lax.dot_general(v, dov, nt, preferred_element_type=F32)
            dst = (pt * (dpt - d_ref[0, qi]) * scale).astype(BF16)
            dk_s[...] += jnp.dot(dst, qv, preferred_element_type=F32)
            dq_ref[pl.ds(start, tile), :] += lax.dot_general(dst, k, tn, preferred_element_type=F32)

        step(ki, True)

        def loop_body(qi, carry):
            step(qi, False)
            return carry

        lax.fori_loop(ki + 1, nq, loop_body, 0)
        dkv_ref[...] = jnp.concatenate([dk_s[:, 0:LANE], dv_s[...]], axis=1).astype(dkv_ref.dtype)
        dkp_ref[0] = dk_s[:, LANE:2 * LANE]

    return pl.pallas_call(
        body, name="attn_bwd",
        out_shape=(jax.ShapeDtypeStruct((S, N_HEADS_MLA * HEAD_PAD), F32),
                   jax.ShapeDtypeStruct((S, N_HEADS_MLA * HEAD_PAD), BF16),
                   jax.ShapeDtypeStruct((N_HEADS_MLA, S, LANE), F32)),
        grid=(N_HEADS_MLA, nq),
        in_specs=[pl.BlockSpec((tile, HEAD_PAD), lambda h, i: (i, h)),
                  pl.BlockSpec((tile, LANE), lambda h, i: (i, 0)),
                  pl.BlockSpec((S, HEAD_PAD), lambda h, i: (0, h)),
                  pl.BlockSpec((S, V_DIM), lambda h, i: (0, h)),
                  pl.BlockSpec((1, nq, 1, tile), lambda h, i: (h, 0, 0, 0)),
                  pl.BlockSpec((1, nq, 1, tile), lambda h, i: (h, 0, 0, 0))],
        out_specs=(pl.BlockSpec((S, HEAD_PAD), lambda h, i: (0, h)),
                   pl.BlockSpec((tile, HEAD_PAD), lambda h, i: (i, h)),
                   pl.BlockSpec((1, tile, LANE), lambda h, i: (h, i, 0))),
        scratch_shapes=[pltpu.VMEM((tile, HEAD_PAD), F32), pltpu.VMEM((tile, V_DIM), F32)],
        compiler_params=_params(("parallel", "arbitrary")),
    )(kv, kp, q, do, lse, delta)


def _shift_down(cur, halo, k):
    sh = pltpu.roll(cur, k, 0)
    hs = pltpu.roll(halo, k, 0)
    rows = lax.broadcasted_iota(jnp.int32, hs.shape, 0)
    first = jnp.where(rows < k, hs, sh[0:SUBLANE])
    if cur.shape[0] == SUBLANE:
        return first
    return jnp.concatenate([first, sh[SUBLANE:]], axis=0)


def _shift_up(cur, nxt, k):
    n = cur.shape[0]
    sh = pltpu.roll(cur, n - k, 0)
    ns = pltpu.roll(nxt, SUBLANE - k, 0)
    rows = lax.broadcasted_iota(jnp.int32, ns.shape, 0)
    last = jnp.where(rows >= SUBLANE - k, ns, sh[n - SUBLANE:])
    if n == SUBLANE:
        return last
    return jnp.concatenate([sh[:n - SUBLANE], last], axis=0)


def _conv_pre(cur, halo, w, b):
    out = b + w[3:4] * cur
    for k in range(1, CONV_WIDTH):
        out = out + w[3 - k:4 - k] * _shift_down(cur, halo, k)
    return out


def _conv_fwd(xbc, w, b):
    S = xbc.shape[0]

    def fn(i, n, cur, prev, w, b):
        halo = jnp.where(i > 0, prev, 0.0)
        pre = _conv_pre(cur, halo, w, b)
        return (pre * _sigmoid(pre),)

    return _rows(fn, S, 256, [_row(xbc), _prev8(xbc), _full(w), _full(b)], [("row", xbc.shape[1], F32)], "conv_fwd")[0]


def _conv_bwd(xbc, dacts, w, b):
    S, C = xbc.shape

    def dsilu(pre):
        s = _sigmoid(pre)
        return s * (1.0 + pre * (1.0 - s))

    def fn(i, n, cur, prev, nxt, *rest):
        k3 = len(dacts)
        dcur = jnp.concatenate(rest[:k3], axis=1)
        dnxt = jnp.concatenate(rest[k3:2 * k3], axis=1)
        w, b = rest[2 * k3:]
        halo = jnp.where(i > 0, prev, 0.0)
        pre = _conv_pre(cur, halo, w, b)
        dpre = dcur * dsilu(pre)
        pre_n = _conv_pre(nxt, cur[cur.shape[0] - SUBLANE:], w, b)
        dpre_n = jnp.where(i < n - 1, dnxt * dsilu(pre_n), 0.0)
        dx = w[3:4] * dpre
        rows = lax.broadcasted_iota(jnp.int32, (SUBLANE, C), 0)
        dw = jnp.where(rows == 3, _colsum(dpre * cur), 0.0)
        for k in range(1, CONV_WIDTH):
            dx = dx + w[3 - k:4 - k] * _shift_up(dpre, dpre_n, k)
            dw = dw + jnp.where(rows == 3 - k, _colsum(dpre * _shift_down(cur, halo, k)), 0.0)
        return dx, dw, _colsum(dpre)

    return _rows(fn, S, 256, [_row(xbc), _prev8(xbc), _next8(xbc), *[_row(d) for d in dacts], *[_next8(d) for d in dacts],
                              _full(w), _full(b)],
                 [("row", C, BF16), ("acc", (SUBLANE, C), F32), ("acc", (1, C), F32)], "conv_bwd")


def _softplus(x):
    return jnp.maximum(x, 0.0) + jnp.log1p(jnp.exp(-jnp.abs(x)))


def _cumsum_rows(x):
    rows = lax.broadcasted_iota(jnp.int32, x.shape, 0)
    s = 1
    while s < x.shape[0]:
        x = x + jnp.where(rows >= s, pltpu.roll(x, s, 0), 0.0)
        s *= 2
    return x


def _revcumsum_rows(x):
    n = x.shape[0]
    rows = lax.broadcasted_iota(jnp.int32, x.shape, 0)
    s = 1
    while s < n:
        x = x + jnp.where(rows < n - s, pltpu.roll(x, n - s, 0), 0.0)
        s *= 2
    return x


def _dt_prep(dt_raw, dt_bias, a_log):
    S = dt_raw.shape[0]

    def body(raw_ref, bias_ref, alog_ref, dt_ref, cum_ref, cumt_ref):
        dt = _softplus(raw_ref[...] + bias_ref[...])
        cum = _cumsum_rows(dt * (-jnp.exp(alog_ref[...])))
        dt_ref[...] = dt
        cum_ref[...] = cum
        cumt_ref[...] = cum.T

    return pl.pallas_call(
        body, name="dt_prep",
        out_shape=(jax.ShapeDtypeStruct((S, LANE), F32), jax.ShapeDtypeStruct((S, LANE), F32),
                   jax.ShapeDtypeStruct((LANE, S), F32)),
        grid=(S // CHUNK,),
        in_specs=[pl.BlockSpec((CHUNK, LANE), lambda i: (i, 0)), pl.BlockSpec((1, LANE), lambda i: (0, 0)),
                  pl.BlockSpec((1, LANE), lambda i: (0, 0))],
        out_specs=(pl.BlockSpec((CHUNK, LANE), lambda i: (i, 0)), pl.BlockSpec((CHUNK, LANE), lambda i: (i, 0)),
                   pl.BlockSpec((LANE, CHUNK), lambda i: (0, i))),
        compiler_params=_params(("parallel",)),
    )(dt_raw, dt_bias, a_log)


def _group_cols(t):
    S = t.shape[0]
    return jnp.transpose(t[:, :N_HEADS_SSM].reshape(S, SSM_GROUPS, HEADS_PER_GROUP), (1, 0, 2))


def _ungroup_cols(t):
    S = t.shape[1]
    flat = jnp.transpose(t, (1, 0, 2)).reshape(S, N_HEADS_SSM)
    return jnp.pad(flat, ((0, 0), (0, LANE - N_HEADS_SSM)))


_NT = (((1,), (1,)), ((), ()))
_TN = (((0,), (0,)), ((), ()))
P = SSM_HEADDIM
GW = HEADS_PER_GROUP * SSM_HEADDIM


def _decay(cc, cr):
    L = cc.shape[0]
    i = lax.broadcasted_iota(jnp.int32, (L, L), 0)
    j = lax.broadcasted_iota(jnp.int32, (L, L), 1)
    return jnp.exp(jnp.where(i >= j, cc - cr, NEG))


def _ssd_fwd(xbc_c, dt_g, cum_g, cumt_g):
    S = xbc_c.shape[0]
    nc = S // CHUNK
    L = CHUNK
    boff = D_INNER // D_STATE

    def body(x_ref, b_ref, c_ref, dt_ref, cum_ref, cumt_ref, y_ref, st_ref, state):
        c = pl.program_id(1)

        @pl.when(c == 0)
        def _():
            state[...] = jnp.zeros_like(state)

        bm = b_ref[...].astype(BF16)
        cm = c_ref[...].astype(BF16)
        cb = lax.dot_general(cm, bm, _NT, preferred_element_type=F32)
        for r in range(HEADS_PER_GROUP):
            cc = cum_ref[0, :, r:r + 1]
            cr = cumt_ref[0, r:r + 1, :]
            m = (cb * _decay(cc, cr)).astype(BF16)
            xdt = x_ref[:, r * P:(r + 1) * P] * dt_ref[0, :, r:r + 1]
            st = state[r * P:(r + 1) * P, :]
            st_ref[0, 0, r * P:(r + 1) * P, :] = st
            y = jnp.dot(m, xdt.astype(BF16), preferred_element_type=F32)
            y = y + lax.dot_general(cm, st.astype(BF16), _NT, preferred_element_type=F32) * jnp.exp(cc)
            y_ref[:, r * P:(r + 1) * P] = y
            cl = cum_ref[0, L - 1:L, r:r + 1]
            wend = jnp.exp(cl - cc)
            state[r * P:(r + 1) * P, :] = st * jnp.exp(cl) + lax.dot_general(
                (xdt * wend).astype(BF16), bm, _TN, preferred_element_type=F32)

    return pl.pallas_call(
        body, name="ssd_fwd",
        out_shape=(jax.ShapeDtypeStruct((S, D_INNER), F32), jax.ShapeDtypeStruct((SSM_GROUPS, nc, GW, D_STATE), F32)),
        grid=(SSM_GROUPS, nc),
        in_specs=[pl.BlockSpec((L, GW), lambda g, c: (c, g)),
                  pl.BlockSpec((L, D_STATE), lambda g, c: (c, boff + g)),
                  pl.BlockSpec((L, D_STATE), lambda g, c: (c, boff + SSM_GROUPS + g)),
                  pl.BlockSpec((1, L, HEADS_PER_GROUP), lambda g, c: (g, c, 0)),
                  pl.BlockSpec((1, L, HEADS_PER_GROUP), lambda g, c: (g, c, 0)),
                  pl.BlockSpec((1, HEADS_PER_GROUP, L), lambda g, c: (g, 0, c))],
        out_specs=(pl.BlockSpec((L, GW), lambda g, c: (c, g)),
                   pl.BlockSpec((1, 1, GW, D_STATE), lambda g, c: (g, c, 0, 0))),
        scratch_shapes=[pltpu.VMEM((GW, D_STATE), F32)],
        compiler_params=_params(("parallel", "arbitrary")),
    )(xbc_c, xbc_c, xbc_c, dt_g, cum_g, cumt_g)


def _ssd_bwd(xbc_c, dt_g, cum_g, cumt_g, states, dy, dx_skip):
    S = xbc_c.shape[0]
    nc = S // CHUNK
    L = CHUNK
    boff = D_INNER // D_STATE
    rev = lambda c: nc - 1 - c

    def body(x_ref, b_ref, c_ref, dt_ref, cum_ref, cumt_ref, st_ref, dy_ref, skip_ref,
             dx_ref, db_ref, dc_ref, ddt_ref, dcum_ref, dstate):
        c = pl.program_id(1)

        @pl.when(c == 0)
        def _():
            dstate[...] = jnp.zeros_like(dstate)

        bf = b_ref[...]
        bm = bf.astype(BF16)
        cm = c_ref[...].astype(BF16)
        cb = lax.dot_general(cm, bm, _NT, preferred_element_type=F32)
        dcb = jnp.zeros((L, L), F32)
        dbs = jnp.zeros((L, D_STATE), F32)
        dcs = jnp.zeros((L, D_STATE), F32)
        rowid = lax.broadcasted_iota(jnp.int32, (L, 1), 0)
        for r in range(HEADS_PER_GROUP):
            sl = slice(r * P, (r + 1) * P)
            cc = cum_ref[0, :, r:r + 1]
            cr = cumt_ref[0, r:r + 1, :]
            dtc = dt_ref[0, :, r:r + 1]
            decay = _decay(cc, cr)
            m = cb * decay
            xr = x_ref[:, sl]
            xdt = xr * dtc
            xdb = xdt.astype(BF16)
            dyr = dy_ref[:, sl]
            dyb = dyr.astype(BF16)
            st = st_ref[0, 0, sl, :]
            stb = st.astype(BF16)
            ds = dstate[sl, :]
            dsb = ds.astype(BF16)
            ecc = jnp.exp(cc)
            cl = cum_ref[0, L - 1:L, r:r + 1]
            ecl = jnp.exp(cl)
            wend = jnp.exp(cl - cc)

            g = lax.dot_general(dyb, xdb, _NT, preferred_element_type=F32)
            q = g * m
            dcb = dcb + g * decay
            dcum = jnp.sum(q, axis=1, keepdims=True) - _row_of_t(jnp.sum(q, axis=0, keepdims=True), L)
            dxd = lax.dot_general(m.astype(BF16), dyb, _TN, preferred_element_type=F32)
            dxd = dxd + lax.dot_general(bm, dsb, _NT, preferred_element_type=F32) * wend
            yoff = lax.dot_general(cm, stb, _NT, preferred_element_type=F32) * ecc
            dcum = dcum + jnp.sum(dyr * yoff, axis=1, keepdims=True)
            dcs = dcs + jnp.dot(dyb, stb, preferred_element_type=F32) * ecc
            t = jnp.dot(xdb, dsb, preferred_element_type=F32)
            dbs = dbs + t * wend
            vj = jnp.sum(t * bf, axis=1, keepdims=True) * wend
            dcum = dcum - vj
            dlast = jnp.sum(vj) + ecl * jnp.sum(ds * st)
            dcum = dcum + jnp.where(rowid == L - 1, dlast, 0.0)
            dstate[sl, :] = ecl * ds + lax.dot_general((dyr * ecc).astype(BF16), cm, _TN, preferred_element_type=F32)

            dx_ref[:, sl] = dxd * dtc + skip_ref[:, sl]
            ddt_ref[0, :, r:r + 1] = jnp.sum(dxd * xr, axis=1, keepdims=True)
            dcum_ref[0, :, r:r + 1] = dcum
        dcbb = dcb.astype(BF16)
        dc_ref[...] = dcs + jnp.dot(dcbb, bm, preferred_element_type=F32)
        db_ref[...] = dbs + lax.dot_general(dcbb, cm, _TN, preferred_element_type=F32)

    return pl.pallas_call(
        body, name="ssd_bwd",
        out_shape=(jax.ShapeDtypeStruct((S, D_INNER), F32),
                   jax.ShapeDtypeStruct((S, SSM_GROUPS * D_STATE), F32),
                   jax.ShapeDtypeStruct((S, SSM_GROUPS * D_STATE), F32),
                   jax.ShapeDtypeStruct((SSM_GROUPS, S, HEADS_PER_GROUP), F32),
                   jax.ShapeDtypeStruct((SSM_GROUPS, S, HEADS_PER_GROUP), F32)),
        grid=(SSM_GROUPS, nc),
        in_specs=[pl.BlockSpec((L, GW), lambda g, c: (rev(c), g)),
                  pl.BlockSpec((L, D_STATE), lambda g, c: (rev(c), boff + g)),
                  pl.BlockSpec((L, D_STATE), lambda g, c: (rev(c), boff + SSM_GROUPS + g)),
                  pl.BlockSpec((1, L, HEADS_PER_GROUP), lambda g, c: (g, rev(c), 0)),
                  pl.BlockSpec((1, L, HEADS_PER_GROUP), lambda g, c: (g, rev(c), 0)),
                  pl.BlockSpec((1, HEADS_PER_GROUP, L), lambda g, c: (g, 0, rev(c))),
                  pl.BlockSpec((1, 1, GW, D_STATE), lambda g, c: (g, rev(c), 0, 0)),
                  pl.BlockSpec((L, GW), lambda g, c: (rev(c), g)),
                  pl.BlockSpec((L, GW), lambda g, c: (rev(c), g))],
        out_specs=(pl.BlockSpec((L, GW), lambda g, c: (rev(c), g)),
                   pl.BlockSpec((L, D_STATE), lambda g, c: (rev(c), g)),
                   pl.BlockSpec((L, D_STATE), lambda g, c: (rev(c), g)),
                   pl.BlockSpec((1, L, HEADS_PER_GROUP), lambda g, c: (g, rev(c), 0)),
                   pl.BlockSpec((1, L, HEADS_PER_GROUP), lambda g, c: (g, rev(c), 0))),
        scratch_shapes=[pltpu.VMEM((GW, D_STATE), F32)],
        compiler_params=_params(("parallel", "arbitrary")),
    )(xbc_c, xbc_c, xbc_c, dt_g, cum_g, cumt_g, states, dy, dx_skip)


def _row_of_t(row, n):
    eye = lax.broadcasted_iota(jnp.int32, (n, n), 0) == lax.broadcasted_iota(jnp.int32, (n, n), 1)
    return jnp.sum(jnp.where(eye, row, 0.0), axis=1, keepdims=True)


def _dt_bwd(dt_raw, dt_bias, a_log, ddt_x, dcum):
    S = dt_raw.shape[0]

    def fn(i, n, raw, ddx, dcu, bias, alog):
        xx = raw + bias
        dt = _softplus(xx)
        a = -jnp.exp(alog)
        dda = _revcumsum_rows(dcu)
        ddt = ddx + dda * a
        lane = lax.broadcasted_iota(jnp.int32, raw.shape, 1)
        draw = jnp.where(lane < N_HEADS_SSM, ddt * _sigmoid(xx), 0.0)
        return draw, _colsum(draw), _colsum(dda * dt) * a

    return _rows(fn, S, CHUNK, [_row(dt_raw), _row(ddt_x), _row(dcum), _full(dt_bias), _full(a_log)],
                 [("row", LANE, BF16), ("acc", (1, LANE), F32), ("acc", (1, LANE), F32)], "dt_bwd")


PAIRS = HEADS_PER_GROUP // 2
SPREAD_W = HEADS_PER_GROUP * LANE


def _spread_matrix():
    e = np.zeros((SSM_GROUPS, LANE, SPREAD_W), np.float32)
    for g in range(SSM_GROUPS):
        for r in range(HEADS_PER_GROUP):
            e[g, g * HEADS_PER_GROUP + r, r * LANE:(r + 1) * LANE] = 1.0
    return jnp.asarray(e, BF16)


def _pieces(v, n):
    out = []
    for _ in range(n):
        p = v.astype(BF16)
        out.append(p)
        v = v - p.astype(F32)
    return out


def _spread(v, e, n):
    tot = None
    for p in _pieces(v, n):
        t = jnp.dot(p, e, preferred_element_type=F32)
        tot = t if tot is None else tot + t
    return tot


def _gather_rows(z, e):
    hi, lo = _pieces(z, 2)
    return lax.dot_general(hi, e, _NT, preferred_element_type=F32) + lax.dot_general(lo, e, _NT, preferred_element_type=F32)


def _decay_pair(cc, cr, transposed):
    L = cc.shape[0]
    halves = []
    for h in range(L // LANE):
        i = lax.broadcasted_iota(jnp.int32, (L, LANE), 0)
        j = lax.broadcasted_iota(jnp.int32, (L, LANE), 1) + h * LANE
        crh = cr[:, h * LANE:(h + 1) * LANE]
        if transposed:
            halves.append(jnp.exp(jnp.where(j >= i, crh - cc, NEG)))
        else:
            halves.append(jnp.exp(jnp.where(i >= j, cc - crh, NEG)))
    return jnp.concatenate(halves, axis=1)


def _ssd_fwd2(xbc_c, dt, cum, cumt_g, spread):
    S = xbc_c.shape[0]
    nc = S // CHUNK
    L = CHUNK
    boff = D_INNER // D_STATE

    def body(x_ref, b_ref, c_ref, dt_ref, cum_ref, cumt_ref, e_ref, y_ref, st_ref, state):
        c = pl.program_id(1)

        @pl.when(c == 0)
        def _():
            state[...] = jnp.zeros_like(state)

        e = e_ref[0]
        bm = b_ref[...].astype(BF16)
        cm = c_ref[...].astype(BF16)
        cb = lax.dot_general(cm, bm, _NT, preferred_element_type=F32)
        rep_cum = _spread(cum_ref[...], e, 3)
        rep_dt = _spread(dt_ref[...], e, 2)
        lo = lax.broadcasted_iota(jnp.int32, (L, LANE), 1) < P
        lo1 = lax.broadcasted_iota(jnp.int32, (1, LANE), 1) < P
        top = lax.broadcasted_iota(jnp.int32, (LANE, LANE), 0) < P
        for p in range(PAIRS):
            t0, t1 = 2 * p * LANE, (2 * p + 1) * LANE
            cc0, cc1 = rep_cum[:, t0:t0 + LANE], rep_cum[:, t1:t1 + LANE]
            ccp = jnp.where(lo, cc0, cc1)
            cl0, cl1 = cc0[L - 1:L, :], cc1[L - 1:L, :]
            clp = jnp.where(lo1, cl0, cl1)
            xdt = x_ref[:, p * LANE:(p + 1) * LANE] * jnp.where(lo, rep_dt[:, t0:t0 + LANE], rep_dt[:, t1:t1 + LANE])
            xdb = xdt.astype(BF16)
            ys = []
            for r, cc in ((2 * p, cc0), (2 * p + 1, cc1)):
                m = (cb * _decay_pair(cc, cumt_ref[0, r:r + 1, :], False)).astype(BF16)
                ys.append(jnp.dot(m, xdb, preferred_element_type=F32))
            st = state[p * LANE:(p + 1) * LANE, :]
            st_ref[0, 0, p * LANE:(p + 1) * LANE, :] = st
            yoff = lax.dot_general(cm, st.astype(BF16), _NT, preferred_element_type=F32) * jnp.exp(ccp)
            y_ref[:, p * LANE:(p + 1) * LANE] = jnp.where(lo, ys[0], ys[1]) + yoff
            wend = jnp.exp(clp - ccp)
            ecl = jnp.where(top, jnp.exp(cl0), jnp.exp(cl1))
            state[p * LANE:(p + 1) * LANE, :] = st * ecl + lax.dot_general(
                (xdt * wend).astype(BF16), bm, _TN, preferred_element_type=F32)

    return pl.pallas_call(
        body, name="ssd_fwd",
        out_shape=(jax.ShapeDtypeStruct((S, D_INNER), F32), jax.ShapeDtypeStruct((SSM_GROUPS, nc, GW, D_STATE), F32)),
        grid=(SSM_GROUPS, nc),
        in_specs=[pl.BlockSpec((L, GW), lambda g, c: (c, g)),
                  pl.BlockSpec((L, D_STATE), lambda g, c: (c, boff + g)),
                  pl.BlockSpec((L, D_STATE), lambda g, c: (c, boff + SSM_GROUPS + g)),
                  pl.BlockSpec((L, LANE), lambda g, c: (c, 0)),
                  pl.BlockSpec((L, LANE), lambda g, c: (c, 0)),
                  pl.BlockSpec((1, HEADS_PER_GROUP, L), lambda g, c: (g, 0, c)),
                  pl.BlockSpec((1, LANE, SPREAD_W), lambda g, c: (g, 0, 0))],
        out_specs=(pl.BlockSpec((L, GW), lambda g, c: (c, g)),
                   pl.BlockSpec((1, 1, GW, D_STATE), lambda g, c: (g, c, 0, 0))),
        scratch_shapes=[pltpu.VMEM((GW, D_STATE), F32)],
        compiler_params=_params(("parallel", "arbitrary")),
    )(xbc_c, xbc_c, xbc_c, dt, cum, cumt_g, spread)


def _ssd_bwd2(xbc_c, dt, cum, cumt_g, spread, states, dy, dx_skip):
    S = xbc_c.shape[0]
    nc = S // CHUNK
    L = CHUNK
    boff = D_INNER // D_STATE
    rev = lambda c: nc - 1 - c

    def body(x_ref, b_ref, c_ref, dt_ref, cum_ref, cumt_ref, e_ref, st_ref, dy_ref, skip_ref,
             dx_ref, db_ref, dc_ref, ddt_ref, dcum_ref, dstate):
        c = pl.program_id(1)

        @pl.when(c == 0)
        def _():
            dstate[...] = jnp.zeros_like(dstate)

        e = e_ref[0]
        bf = b_ref[...]
        bm = bf.astype(BF16)
        cm = c_ref[...].astype(BF16)
        cb = lax.dot_general(cm, bm, _NT, preferred_element_type=F32)
        cbt = lax.dot_general(bm, cm, _NT, preferred_element_type=F32)
        rep_cum = _spread(cum_ref[...], e, 3)
        rep_dt = _spread(dt_ref[...], e, 2)
        lane = lax.broadcasted_iota(jnp.int32, (L, LANE), 1)
        lo = lane < P
        lo1 = lax.broadcasted_iota(jnp.int32, (1, LANE), 1) < P
        top = lax.broadcasted_iota(jnp.int32, (LANE, LANE), 0) < P
        last = lax.broadcasted_iota(jnp.int32, (L, LANE), 0) == L - 1
        dcb = jnp.zeros((L, L), F32)
        dcbt = jnp.zeros((L, L), F32)
        dbs = jnp.zeros((L, D_STATE), F32)
        dcs = jnp.zeros((L, D_STATE), F32)
        zs, zds = [], []
        for p in range(PAIRS):
            sl = slice(p * LANE, (p + 1) * LANE)
            t0, t1 = 2 * p * LANE, (2 * p + 1) * LANE
            cc0, cc1 = rep_cum[:, t0:t0 + LANE], rep_cum[:, t1:t1 + LANE]
            ccp = jnp.where(lo, cc0, cc1)
            cl0, cl1 = cc0[L - 1:L, :], cc1[L - 1:L, :]
            w0, w1 = jnp.exp(cl0 - cc0), jnp.exp(cl1 - cc1)
            wend = jnp.where(lo, w0, w1)
            ecc = jnp.exp(ccp)
            ecl0, ecl1 = jnp.exp(cl0), jnp.exp(cl1)
            dtp = jnp.where(lo, rep_dt[:, t0:t0 + LANE], rep_dt[:, t1:t1 + LANE])
            xp = x_ref[:, sl]
            xdt = xp * dtp
            xdb = xdt.astype(BF16)
            dyp = dy_ref[:, sl]
            st = st_ref[0, 0, sl, :]
            stb = st.astype(BF16)
            ds = dstate[sl, :]
            dsb = ds.astype(BF16)
            yoff = lax.dot_general(cm, stb, _NT, preferred_element_type=F32) * ecc
            dye = (dyp * ecc).astype(BF16)
            dcs = dcs + jnp.dot(dye, stb, preferred_element_type=F32)
            dstate[sl, :] = jnp.where(top, ecl0, ecl1) * ds + lax.dot_general(dye, cm, _TN, preferred_element_type=F32)
            dxd = lax.dot_general(bm, dsb, _NT, preferred_element_type=F32) * wend
            sst = ds * st
            dyo = dyp * yoff
            mts = []
            for r, cc, w, ecl, keep, keep_rows in ((2 * p, cc0, w0, ecl0, lo, top), (2 * p + 1, cc1, w1, ecl1, ~lo, ~top)):
                cr = cumt_ref[0, r:r + 1, :]
                decay = _decay_pair(cc, cr, False)
                decay_t = _decay_pair(cc, cr, True)
                m = cb * decay
                mt = cbt * decay_t
                dyr = jnp.where(keep, dyp, 0.0).astype(BF16)
                g = lax.dot_general(dyr, xdb, _NT, preferred_element_type=F32)
                gt = lax.dot_general(xdb, dyr, _NT, preferred_element_type=F32)
                q = g * m
                qt = gt * mt
                dcb = dcb + g * decay
                dcbt = dcbt + gt * decay_t
                mts.append(jnp.dot(mt.astype(BF16), dyr, preferred_element_type=F32))
                t = jnp.dot(jnp.where(keep, xdt, 0.0).astype(BF16), dsb, preferred_element_type=F32)
                dbs = dbs + t * w
                tb = t * bf * w
                end_row = _colsum(tb) + ecl * _colsum(jnp.where(keep_rows, sst, 0.0))
                z = (q[:, 0:LANE] + q[:, LANE:2 * LANE]) - (qt[:, 0:LANE] + qt[:, LANE:2 * LANE])
                z = z + jnp.where(keep, dyo, 0.0) - tb + jnp.where(last, end_row, 0.0)
                zs.append(z)
            dxd = dxd + mts[0] + mts[1]
            dx_ref[:, sl] = dxd * dtp + skip_ref[:, sl]
            zd = dxd * xp
            zds.append(jnp.where(lo, zd, 0.0))
            zds.append(jnp.where(lo, 0.0, zd))
        dc_ref[...] = dcs + jnp.dot(dcb.astype(BF16), bm, preferred_element_type=F32)
        db_ref[...] = dbs + jnp.dot(dcbt.astype(BF16), cm, preferred_element_type=F32)
        dcum_ref[0] = _gather_rows(jnp.concatenate(zs, axis=1), e)
        ddt_ref[0] = _gather_rows(jnp.concatenate(zds, axis=1), e)

    return pl.pallas_call(
        body, name="ssd_bwd",
        out_shape=(jax.ShapeDtypeStruct((S, D_INNER), F32),
                   jax.ShapeDtypeStruct((S, SSM_GROUPS * D_STATE), F32),
                   jax.ShapeDtypeStruct((S, SSM_GROUPS * D_STATE), F32),
                   jax.ShapeDtypeStruct((SSM_GROUPS, S, LANE), F32),
                   jax.ShapeDtypeStruct((SSM_GROUPS, S, LANE), F32)),
        grid=(SSM_GROUPS, nc),
        in_specs=[pl.BlockSpec((L, GW), lambda g, c: (rev(c), g)),
                  pl.BlockSpec((L, D_STATE), lambda g, c: (rev(c), boff + g)),
                  pl.BlockSpec((L, D_STATE), lambda g, c: (rev(c), boff + SSM_GROUPS + g)),
                  pl.BlockSpec((L, LANE), lambda g, c: (rev(c), 0)),
                  pl.BlockSpec((L, LANE), lambda g, c: (rev(c), 0)),
                  pl.BlockSpec((1, HEADS_PER_GROUP, L), lambda g, c: (g, 0, rev(c))),
                  pl.BlockSpec((1, LANE, SPREAD_W), lambda g, c: (g, 0, 0)),
                  pl.BlockSpec((1, 1, GW, D_STATE), lambda g, c: (g, rev(c), 0, 0)),
                  pl.BlockSpec((L, GW), lambda g, c: (rev(c), g)),
                  pl.BlockSpec((L, GW), lambda g, c: (rev(c), g))],
        out_specs=(pl.BlockSpec((L, GW), lambda g, c: (rev(c), g)),
                   pl.BlockSpec((L, D_STATE), lambda g, c: (rev(c), g)),
                   pl.BlockSpec((L, D_STATE), lambda g, c: (rev(c), g)),
                   pl.BlockSpec((1, L, LANE), lambda g, c: (g, rev(c), 0)),
                   pl.BlockSpec((1, L, LANE), lambda g, c: (g, rev(c), 0))),
        scratch_shapes=[pltpu.VMEM((GW, D_STATE), F32)],
        compiler_params=_params(("parallel", "arbitrary")),
    )(xbc_c, xbc_c, xbc_c, dt, cum, cumt_g, spread, states, dy, dx_skip)


def _dt_bwd2(dt_raw, dt_bias, a_log, ddt_x, dcum):
    S = dt_raw.shape[0]
    n = S // CHUNK

    def body(raw_ref, ddx_ref, dcu_ref, bias_ref, alog_ref, draw_ref, gb_ref, ga_ref):
        i = pl.program_id(0)

        @pl.when(i == 0)
        def _():
            gb_ref[...] = jnp.zeros_like(gb_ref)
            ga_ref[...] = jnp.zeros_like(ga_ref)

        ddx, dcu = ddx_ref[0], dcu_ref[0]
        for g in range(1, SSM_GROUPS):
            ddx = ddx + ddx_ref[g]
            dcu = dcu + dcu_ref[g]
        xx = raw_ref[...] + bias_ref[...]
        dt = _softplus(xx)
        a = -jnp.exp(alog_ref[...])
        dda = _revcumsum_rows(dcu)
        lane = lax.broadcasted_iota(jnp.int32, xx.shape, 1)
        draw = jnp.where(lane < N_HEADS_SSM, (ddx + dda * a) * _sigmoid(xx), 0.0)
        draw_ref[...] = draw.astype(draw_ref.dtype)
        gb_ref[...] += _colsum(draw)
        ga_ref[...] += _colsum(dda * dt) * a

    row = pl.BlockSpec((CHUNK, LANE), lambda i: (i, 0))
    grp = pl.BlockSpec((SSM_GROUPS, CHUNK, LANE), lambda i: (0, i, 0))
    one = pl.BlockSpec((1, LANE), lambda i: (0, 0))
    return pl.pallas_call(
        body, name="dt_bwd",
        out_shape=(jax.ShapeDtypeStruct((S, LANE), BF16), jax.ShapeDtypeStruct((1, LANE), F32), jax.ShapeDtypeStruct((1, LANE), F32)),
        grid=(n,),
        in_specs=[row, grp, grp, one, one],
        out_specs=(row, one, one),
        compiler_params=_params(("arbitrary",)),
    )(dt_raw, ddt_x, dcum, dt_bias, a_log)


def _adamw(w, g, m, v, name):
    shape = w.shape
    cols = shape[-1]
    rows = int(np.prod(shape[:-1]))
    w2, g2, m2, v2 = (t.reshape(rows, cols) for t in (w, g, m, v))
    tr = rows if rows * cols <= 512 * 1024 else _tile(rows, max(SUBLANE, (512 * 1024 // cols) // SUBLANE * SUBLANE), SUBLANE)
    c1 = 1.0 - ADAM_B1 ** ADAM_STEP
    c2 = 1.0 - ADAM_B2 ** ADAM_STEP

    def body(w_ref, g_ref, m_ref, v_ref, d_ref, mo_ref, vo_ref):
        gv = g_ref[...]
        mn = ADAM_B1 * m_ref[...] + (1.0 - ADAM_B1) * gv
        vn = ADAM_B2 * v_ref[...] + (1.0 - ADAM_B2) * (gv * gv)
        d_ref[...] = -ADAM_LR * ((mn / c1) / (jnp.sqrt(vn / c2) + ADAM_EPS) + ADAM_WD * w_ref[...])
        mo_ref[...] = mn
        vo_ref[...] = vn

    spec = pl.BlockSpec((tr, cols), lambda i: (i, 0))
    outs = pl.pallas_call(
        body, name=name,
        out_shape=tuple(jax.ShapeDtypeStruct((rows, cols), F32) for _ in range(3)),
        grid=(rows // tr,),
        in_specs=[spec] * 4, out_specs=(spec,) * 3,
        compiler_params=_params(("parallel",)),
    )(w2, g2, m2, v2)
    return tuple(o.reshape(shape) for o in outs)


def _prep_weights(w_in, w_uq):
    offs = np.cumsum((0,) + IN_SPLITS)
    pad = lambda t: jnp.pad(t, ((0, 0), (0, LANE - t.shape[1])))
    pieces = dict(
        qkv=w_in[:, offs[0]:offs[2]],
        kr=pad(w_in[:, offs[2]:offs[3]]),
        z=w_in[:, offs[3]:offs[4]],
        xbc=w_in[:, offs[4]:offs[5]],
        dt=pad(w_in[:, offs[5]:offs[6]]),
        g=w_in[:, offs[6]:offs[8]],
    )
    uq = w_uq.reshape(Q_LORA, N_HEADS_MLA, QK_DIM)
    uq = jnp.pad(uq, ((0, 0), (0, 0), (0, HEAD_PAD - QK_DIM))).reshape(Q_LORA, N_HEADS_MLA * HEAD_PAD)
    return pieces, uq


def _local_step(x, p, positions, ex, sp, target):
    W = gw = ex
    S = x.shape[0]
    tile = min(ATTN_TILE, S)
    pos_col = positions.reshape(S, 1)
    invf = ROPE_THETA ** (-jnp.arange(0, QK_ROPE, 2, dtype=F32) / QK_ROPE)
    invf = jnp.pad(jnp.concatenate([invf, invf]), (0, LANE - QK_ROPE)).reshape(1, LANE)
    wp, w_uq_p = _prep_weights(W["w_in"], W["w_uq"])
    padl = lambda t: jnp.pad(t, ((0, 0), (0, LANE - t.shape[1])))
    dt_bias_p, a_log_p = padl(sp["dt_bias"]), padl(sp["a_log"])
    dskip_ch = jnp.repeat(sp["d_skip"], SSM_HEADDIM, axis=1)
    p_bf = p.astype(BF16)
    RW = 256

    (u_bf,) = _rows(lambda i, n, x, g: (x * _rstd(x) * g,), S, RW, [_row(x), _full(sp["mix_norm_pre"])],
                    [("row", D_MODEL, BF16)], "norm_pre")
    cqkv = ex.mm(u_bf, wp["qkv"], name="mm_qkv")
    z = ex.mm(u_bf, wp["z"], name="mm_z")
    xbc = ex.mm(u_bf, wp["xbc"], name="mm_xbc")
    gates = ex.mm(u_bf, wp["g"], name="mm_gates")
    kr_pad = ex.mm(u_bf, wp["kr"], name="mm_kr")
    dt_raw = ex.mm(u_bf, wp["dt"], name="mm_dt")

    def qkv_norm(i, n, cq, ckv, gq, gkv):
        return cq * _rstd(cq) * gq, ckv * _rstd(ckv) * gkv

    cqn, ckvn = _rows(qkv_norm, S, 512, [_row(cqkv, Q_LORA, 0), _row(cqkv, KV_LORA, 1), _full(sp["q_norm"]), _full(sp["kv_norm"])],
                      [("row", Q_LORA, BF16), ("row", KV_LORA, BF16)], "qkv_norm")
    q_raw = ex.mm(cqn, w_uq_p, name="mm_uq")
    kv = ex.mm(ckvn, W["w_ukv"], out_dtype=BF16, name="mm_ukv")
    q_bf, kp_bf = _rope_fwd(q_raw, kr_pad, pos_col, invf)
    attn, lse = _attn_fwd(q_bf, kv, kp_bf, tile)

    xbc_c = _conv_fwd(xbc, sp["conv_w"], sp["conv_b"])
    dt, cum, cumt = _dt_prep(dt_raw, dt_bias_p, a_log_p)
    cumt_g = cumt[:N_HEADS_SSM].reshape(SSM_GROUPS, HEADS_PER_GROUP, S)
    spread = _spread_matrix()
    y, states = _ssd_fwd2(xbc_c, dt, cum, cumt_g, spread)

    GN = D_INNER // SSM_GROUPS

    def gated(y, xs, z, dsk):
        yt = y + dsk * xs
        sz = _sigmoid(z)
        return yt, sz, yt * (z * sz)

    def gated_norm(i, n, y, xs, z, dsk, gn):
        _, _, yg = gated(y, xs, z, dsk)
        parts = []
        for g in range(SSM_GROUPS):
            blk = yg[:, g * GN:(g + 1) * GN]
            parts.append(blk * _rstd(blk) * gn[:, g * GN:(g + 1) * GN])
        return (jnp.concatenate(parts, axis=1),)

    (ssm,) = _rows(gated_norm, S, 128, [_row(y), _row(xbc_c, D_INNER, 0), _row(z), _full(dskip_ch), _full(sp["ssm_norm"])],
                   [("row", D_INNER, BF16)], "gated_norm")

    a_o = ex.mm(attn, W["w_attn_o"], name="mm_attn_o")
    b_o = ex.mm(ssm, W["w_ssm_o"], name="mm_ssm_o")

    def mix(i, n, ga, gs, a, b):
        return (_sigmoid(ga) * a + _sigmoid(gs) * b,)

    (mixed,) = _rows(mix, S, RW, [_row(gates, D_MODEL, 0), _row(gates, D_MODEL, 1), _row(a_o), _row(b_o)],
                     [("row", D_MODEL, BF16)], "mix")
    m2 = ex.mm(mixed, W["w_out"], name="mm_out")

    def post(i, n, h, m, gpost, gpre):
        hn = h + m * _rstd(m) * gpost
        return hn, hn * _rstd(hn) * gpre

    h1, f_bf = _rows(post, S, RW, [_row(x), _row(m2), _full(sp["mix_norm_post"]), _full(sp["ffn_norm_pre"])],
                     [("row", D_MODEL, F32), ("row", D_MODEL, BF16)], "post_mix")
    ga = ex.mm(f_bf, W["w_gate"], name="mm_gate")
    up = ex.mm(f_bf, W["w_up"], name="mm_up")
    (s_bf,) = _rows(lambda i, n, a, b: (a * _sigmoid(a) * b,), S, RW, [_row(ga), _row(up)], [("row", D_FF, BF16)], "swiglu")
    f2 = ex.mm(s_bf, W["w_down"], name="mm_down")
    h2, n3_bf = _rows(post, S, RW, [_row(h1), _row(f2), _full(sp["ffn_norm_post"]), _full(sp["ple_norm_pre"])],
                      [("row", D_MODEL, F32), ("row", D_MODEL, BF16)], "post_ffn")
    gpre = ex.mm(n3_bf, W["w_ple_gate"], name="mm_ple_gate")
    pe = ex.mm(p_bf, W["w_ple"], name="mm_ple")

    def ple_loss(i, n, h2, gpre, pe, tgt, gpost):
        gate = _sigmoid(gpre)
        e = pe * gate
        r = _rstd(e)
        diff = h2 + e * r * gpost - tgt
        loss = 0.5 * jnp.sum(jnp.mean(diff * diff, axis=1, keepdims=True))
        dh3 = diff * (1.0 / D_MODEL)
        de, dg_rows = _norm_bwd(e, r, gpost, dh3)
        return (jnp.full((1, LANE), loss, F32), dh3, de * gate, de * pe * gate * (1.0 - gate), _colsum(dg_rows))

    loss, dh3, dpe, dgpre, g_ple_post = _rows(
        ple_loss, S, 128, [_row(h2), _row(gpre), _row(pe), _row(target), _full(sp["ple_norm_post"])],
        [("acc", (1, LANE), F32), ("row", D_MODEL, F32), ("row", D_MODEL, BF16), ("row", D_MODEL, BF16),
         ("acc", (1, D_MODEL), F32)], "ple_loss")

    gs = {"ple_norm_post": g_ple_post}
    gw["w_ple"] = ex.mm(p_bf, dpe, ta=True, name="mmg_ple")
    gw["w_ple_gate"] = ex.mm(n3_bf, dgpre, ta=True, name="mmg_ple_gate")
    dn3 = ex.mm(dgpre, W["w_ple_gate"], tb=True, name="mmb_ple_gate")

    def post_bwd(i, n, h, m, dhn, dn, gpost, gpre):
        rm = _rstd(m)
        hn = h + m * rm * gpost
        dx, dgpre_rows = _norm_bwd(hn, _rstd(hn), gpre, dn)
        dhn_t = dhn + dx
        dm, dgpost_rows = _norm_bwd(m, rm, gpost, dhn_t)
        return dhn_t, dm, _colsum(dgpre_rows), _colsum(dgpost_rows)

    def run_post_bwd(h, m, dhn, dn, gpost, gpre, name):
        return _rows(post_bwd, S, 128, [_row(h), _row(m), _row(dhn), _row(dn), _full(gpost), _full(gpre)],
                     [("row", D_MODEL, F32), ("row", D_MODEL, BF16), ("acc", (1, D_MODEL), F32), ("acc", (1, D_MODEL), F32)], name)

    dh2, df2, gs["ple_norm_pre"], gs["ffn_norm_post"] = run_post_bwd(
        h1, f2, dh3, dn3, sp["ffn_norm_post"], sp["ple_norm_pre"], "post_ffn_bwd")
    gw["w_down"] = ex.mm(s_bf, df2, ta=True, name="mmg_down")
    ds = ex.mm(df2, W["w_down"], tb=True, name="mmb_down")

    def swiglu_bwd(i, n, a, b, ds):
        sa = _sigmoid(a)
        return ds * b * (sa * (1.0 + a * (1.0 - sa))), ds * (a * sa)

    dga, dup = _rows(swiglu_bwd, S, RW, [_row(ga), _row(up), _row(ds)], [("row", D_FF, BF16), ("row", D_FF, BF16)], "swiglu_bwd")
    gw["w_gate"] = ex.mm(f_bf, dga, ta=True, name="mmg_gate")
    gw["w_up"] = ex.mm(f_bf, dup, ta=True, name="mmg_up")
    df = ex.mm(dga, W["w_gate"], tb=True, name="mmb_gate")
    df = ex.mm(dup, W["w_up"], tb=True, add=df, name="mmb_up")
    dh1, dm2, gs["ffn_norm_pre"], gs["mix_norm_post"] = run_post_bwd(
        x, m2, dh2, df, sp["mix_norm_post"], sp["ffn_norm_pre"], "post_mix_bwd")
    gw["w_out"] = ex.mm(mixed, dm2, ta=True, name="mmg_out")
    dmixed = ex.mm(dm2, W["w_out"], tb=True, name="mmb_out")

    def mix_bwd(i, n, ga, gs_, a, b, dm):
        sa, ss = _sigmoid(ga), _sigmoid(gs_)
        return dm * sa, dm * ss, jnp.concatenate([dm * a * sa * (1.0 - sa), dm * b * ss * (1.0 - ss)], axis=1)

    da_o, db_o, dgates = _rows(mix_bwd, S, RW, [_row(gates, D_MODEL, 0), _row(gates, D_MODEL, 1), _row(a_o), _row(b_o), _row(dmixed)],
                               [("row", D_MODEL, BF16), ("row", D_MODEL, BF16), ("row", 2 * D_MODEL, BF16)], "mix_bwd")
    gw["w_attn_o"] = ex.mm(attn, da_o, ta=True, name="mmg_attn_o")
    dattn = ex.mm(da_o, W["w_attn_o"], tb=True, name="mmb_attn_o")
    gw["w_ssm_o"] = ex.mm(ssm, db_o, ta=True, name="mmg_ssm_o")
    dssm = ex.mm(db_o, W["w_ssm_o"], tb=True, name="mmb_ssm_o")

    delta, dattn_bf = _attn_delta(attn, dattn, tile)
    dq, dkv, dkp = _attn_bwd(q_bf, kv, kp_bf, dattn_bf, lse, delta, tile)
    dq_raw, dkr = _rope_bwd(dq, dkp, pos_col, invf)
    g_uq_p = ex.mm(cqn, dq_raw, ta=True, name="mmg_uq")
    gw["w_uq"] = g_uq_p.reshape(Q_LORA, N_HEADS_MLA, HEAD_PAD)[:, :, :QK_DIM].reshape(Q_LORA, N_HEADS_MLA * QK_DIM)
    dcqn = ex.mm(dq_raw, w_uq_p, tb=True, name="mmb_uq")
    gw["w_ukv"] = ex.mm(ckvn, dkv, ta=True, name="mmg_ukv")
    dckvn = ex.mm(dkv, W["w_ukv"], tb=True, name="mmb_ukv")

    def qkv_norm_bwd(i, n, cq, ckv, dq_, dkv_, gq, gkv):
        dcq, gq_rows = _norm_bwd(cq, _rstd(cq), gq, dq_)
        dckv, gkv_rows = _norm_bwd(ckv, _rstd(ckv), gkv, dkv_)
        return jnp.concatenate([dcq, dckv], axis=1), _colsum(gq_rows), _colsum(gkv_rows)

    dcqkv, gs["q_norm"], gs["kv_norm"] = _rows(
        qkv_norm_bwd, S, 512, [_row(cqkv, Q_LORA, 0), _row(cqkv, KV_LORA, 1), _row(dcqn), _row(dckvn), _full(sp["q_norm"]), _full(sp["kv_norm"])],
        [("row", Q_LORA + KV_LORA, BF16), ("acc", (1, Q_LORA), F32), ("acc", (1, KV_LORA), F32)], "qkv_norm_bwd")

    def gated_norm_bwd(i, n, y, xs, z, dssm, dsk, gn):
        yt, sz, yg = gated(y, xs, z, dsk)
        dyg_parts, gn_parts = [], []
        for g in range(SSM_GROUPS):
            sl = slice(g * GN, (g + 1) * GN)
            blk = yg[:, sl]
            dblk, rows = _norm_bwd(blk, _rstd(blk), gn[:, sl], dssm[:, sl])
            dyg_parts.append(dblk)
            gn_parts.append(_colsum(rows))
        dyg = jnp.concatenate(dyg_parts, axis=1)
        dyt = dyg * (z * sz)
        dz = dyg * yt * (sz * (1.0 + z * (1.0 - sz)))
        return dyt, dz, dyt * dsk, jnp.concatenate(gn_parts, axis=1), _colsum(dyt * xs)

    dy, dz, dx_skip, gs["ssm_norm"], g_dskip_ch = _rows(
        gated_norm_bwd, S, 128, [_row(y), _row(xbc_c, D_INNER, 0), _row(z), _row(dssm), _full(dskip_ch), _full(sp["ssm_norm"])],
        [("row", D_INNER, F32), ("row", D_INNER, BF16), ("row", D_INNER, F32), ("acc", (1, D_INNER), F32), ("acc", (1, D_INNER), F32)],
        "gated_norm_bwd")
    gs["d_skip"] = jnp.sum(g_dskip_ch.reshape(N_HEADS_SSM, SSM_HEADDIM), axis=1).reshape(1, N_HEADS_SSM)
    dxs, dbm, dcm, ddt_x, dcum = _ssd_bwd2(xbc_c, dt, cum, cumt_g, spread, states, dy, dx_skip)
    ddt_raw, g_dtb, g_alog = _dt_bwd2(dt_raw, dt_bias_p, a_log_p, ddt_x, dcum)
    gs["dt_bias"] = g_dtb[:, :N_HEADS_SSM]
    gs["a_log"] = g_alog[:, :N_HEADS_SSM]
    dxbc, g_conv_w8, gs["conv_b"] = _conv_bwd(xbc, [dxs, dbm, dcm], sp["conv_w"], sp["conv_b"])
    gs["conv_w"] = g_conv_w8[:CONV_WIDTH]

    g_qkv = ex.mm(u_bf, dcqkv, ta=True, name="mmg_qkv")
    g_kr = ex.mm(u_bf, dkr, ta=True, name="mmg_kr")
    g_z = ex.mm(u_bf, dz, ta=True, name="mmg_z")
    g_xbc = ex.mm(u_bf, dxbc, ta=True, name="mmg_xbc")
    g_dt = ex.mm(u_bf, ddt_raw, ta=True, name="mmg_dt")
    g_g = ex.mm(u_bf, dgates, ta=True, name="mmg_gates")
    gw["w_in"] = jnp.concatenate([g_qkv, g_kr[:, :QK_ROPE], g_z, g_xbc, g_dt[:, :N_HEADS_SSM], g_g], axis=1)
    du = ex.mm(dcqkv, wp["qkv"], tb=True, name="mmb_qkv")
    du = ex.mm(dkr, wp["kr"], tb=True, add=du, name="mmb_kr")
    du = ex.mm(ddt_raw, wp["dt"], tb=True, add=du, name="mmb_dt")
    du = ex.mm(dz, wp["z"], tb=True, add=du, name="mmb_z")
    du = ex.mm(dxbc, wp["xbc"], tb=True, add=du, name="mmb_xbc")
    du = ex.mm(dgates, wp["g"], tb=True, add=du, name="mmb_gates")

    def pre_bwd(i, n, x, du, dh, g):
        dx, rows = _norm_bwd(x, _rstd(x), g, du)
        return dh + dx, _colsum(rows)

    grad_x, gs["mix_norm_pre"] = _rows(pre_bwd, S, RW, [_row(x), _row(du), _row(dh1), _full(sp["mix_norm_pre"])],
                                       [("row", D_MODEL, F32), ("acc", (1, D_MODEL), F32)], "norm_pre_bwd")
    return loss, grad_x, gs


BIG = (
    ("w_in", (2048, 3872), 1), ("w_uq", (512, 768), 1), ("w_ukv", (512, 1024), 1), ("w_attn_o", (512, 2048), 0),
    ("w_ssm_o", (1024, 2048), 0), ("w_out", (512, 2048), 0), ("w_gate", (2048, 1408), 1), ("w_up", (2048, 1408), 1),
    ("w_down", (1408, 2048), 0), ("w_ple_gate", (512, 2048), 0), ("w_ple", (256, 512), 1),
)
SMALL = (
    ("mix_norm_pre", 2048), ("mix_norm_post", 2048), ("q_norm", 512), ("kv_norm", 512), ("conv_b", 6144), ("dt_bias", 64),
    ("a_log", 64), ("d_skip", 64), ("ssm_norm", 4096), ("ffn_norm_pre", 2048), ("ffn_norm_post", 2048),
    ("ple_norm_pre", 2048), ("ple_norm_post", 2048),
)
CONV_W_LEN = CONV_WIDTH * CONV_DIM
SMALL_ROWS = 384


def _place():
    return lax.axis_index("x"), lax.axis_index("y"), lax.axis_index("c")


def _flip(v, bit):
    return 1 - v if bit else v


def _alone(hook, name):
    n_in, n_out = len(hook.ins), len(hook.out_shapes)

    def body(*refs):
        start, finish = hook.make(refs[:n_in], refs[n_in:n_in + n_out], refs[n_in + n_out:])
        start()
        finish()

    return list(pl.pallas_call(
        body, name=name, out_shape=tuple(hook.out_shapes),
        in_specs=[pl.BlockSpec(memory_space=pl.ANY)] * n_in,
        out_specs=tuple(pl.BlockSpec(memory_space=pl.ANY) for _ in range(n_out)),
        scratch_shapes=[pltpu.SemaphoreType.DMA((s,)) for s in hook.sems],
        input_output_aliases=hook.aliases,
    )(*hook.ins))


def _simple(copies):
    def start():
        for cp in copies:
            cp.start()

    def finish():
        for cp in copies:
            cp.wait()

    return start, finish


def _gather_hook(shards):
    n = len(shards)

    def make(ins, outs, sems):
        send_sems, recv_sems, fwd_send_sems, fwd_recv_sems = sems
        x, y, c = _place()
        me = 2 * x + y
        far, near = [], []
        for a in range(n):
            half = shards[a].shape[0] // 2
            lo = pl.multiple_of(c * half, SUBLANE)
            for k in (1, 2, 3):
                px, py = _flip(x, k >> 1), _flip(y, k & 1)
                far.append(pltpu.make_async_remote_copy(
                    src_ref=ins[a].at[pl.ds(lo, half), :], dst_ref=outs[a].at[me, pl.ds(lo, half), :],
                    send_sem=send_sems.at[3 * a + k - 1], recv_sem=recv_sems.at[3 * a + k - 1],
                    device_id=(px, py, c), device_id_type=MESH_ID))
                got = outs[a].at[2 * px + py, pl.ds(lo, half), :]
                near.append(pltpu.make_async_remote_copy(
                    src_ref=got, dst_ref=got, send_sem=fwd_send_sems.at[3 * a + k - 1], recv_sem=fwd_recv_sems.at[3 * a + k - 1],
                    device_id=(x, y, 1 - c), device_id_type=MESH_ID))

        def start():
            for cp in far:
                cp.start()

        def finish():
            for cp, fwd in zip(far, near):
                cp.wait_recv()
                fwd.start()
            for cp, fwd in zip(far, near):
                cp.wait_send()
                fwd.wait()

        return start, finish

    return _Hook(shards, [jax.ShapeDtypeStruct((N_CHIPS, *s.shape), s.dtype) for s in shards], (3 * n,) * 4, make)


def _swap_hook(gs):
    n = len(gs)

    def make(ins, outs, sems):
        send_sems, recv_sems = sems
        x, y, c = _place()
        copies = []
        for a in range(n):
            half = gs[a].shape[1] // 2
            src = ins[a].at[:, pl.ds(pl.multiple_of((1 - c) * half, SUBLANE), half), :]
            copies.append(pltpu.make_async_remote_copy(
                src_ref=src, dst_ref=outs[a], send_sem=send_sems.at[a], recv_sem=recv_sems.at[a],
                device_id=(x, y, 1 - c), device_id_type=MESH_ID))
        return _simple(copies)

    return _Hook(gs, [jax.ShapeDtypeStruct((g.shape[0], g.shape[1] // 2, g.shape[2]), g.dtype) for g in gs], (n, n), make)


def _sum_rows_tile(rows, cols):
    return _tile(rows, max(2 * SUBLANE, (512 * 1024 // cols) // (2 * SUBLANE) * (2 * SUBLANE)), 2 * SUBLANE)


def _add_half(g, other, c, name):
    n, R, C = g.shape
    half = R // 2
    tr = _sum_rows_tile(half, C)
    nb = half // tr

    def body(c_ref, g_ref, o_ref, out_ref):
        out_ref[...] = (g_ref[...] + o_ref[...]).astype(out_ref.dtype)

    return pl.pallas_call(
        body, name=name,
        out_shape=jax.ShapeDtypeStruct((n, half, C), BF16),
        grid_spec=pltpu.PrefetchScalarGridSpec(
            num_scalar_prefetch=1, grid=(n, nb),
            in_specs=[pl.BlockSpec((1, tr, C), lambda j, i, c_ref: (j, c_ref[0] * nb + i, 0)),
                      pl.BlockSpec((1, tr, C), lambda j, i, c_ref: (j, i, 0))],
            out_specs=pl.BlockSpec((1, tr, C), lambda j, i, c_ref: (j, i, 0))),
        compiler_params=_params(("parallel", "parallel")),
    )(c, g, other)


def _scatter_hook(parts):
    n = len(parts)

    def make(ins, outs, sems):
        send_sems, recv_sems = sems
        x, y, c = _place()
        copies = []
        for a in range(n):
            for k in (1, 2, 3):
                px, py = _flip(x, k >> 1), _flip(y, k & 1)
                copies.append(pltpu.make_async_remote_copy(
                    src_ref=ins[a].at[2 * px + py], dst_ref=outs[a].at[k - 1], send_sem=send_sems.at[3 * a + k - 1],
                    recv_sem=recv_sems.at[3 * a + k - 1], device_id=(px, py, c), device_id_type=MESH_ID))
        return _simple(copies)

    return _Hook(parts, [jax.ShapeDtypeStruct((3, *p.shape[1:]), p.dtype) for p in parts], (3 * n, 3 * n), make)


def _add_chips(part, got, place, name):
    n, R, C = part.shape
    tr = _sum_rows_tile(R, C)
    nb = R // tr

    def body(place_ref, p_ref, g_ref, out_ref):
        out_ref[...] = ((p_ref[0].astype(F32) + g_ref[0].astype(F32)) + g_ref[1].astype(F32)) + g_ref[2].astype(F32)

    return pl.pallas_call(
        body, name=name,
        out_shape=jax.ShapeDtypeStruct((2 * R, C), F32),
        grid_spec=pltpu.PrefetchScalarGridSpec(
            num_scalar_prefetch=1, grid=(nb,),
            in_specs=[pl.BlockSpec((1, tr, C), lambda i, place_ref: (place_ref[0], i, 0)),
                      pl.BlockSpec((3, tr, C), lambda i, place_ref: (0, i, 0))],
            out_specs=pl.BlockSpec((tr, C), lambda i, place_ref: (place_ref[1] * nb + i, 0))),
        compiler_params=_params(("parallel",)),
    )(place, part, got)


def _join_hook(wholes):
    n = len(wholes)

    def make(ins, outs, sems):
        send_sems, recv_sems = sems
        x, y, c = _place()
        copies = []
        for a in range(n):
            half = wholes[a].shape[0] // 2
            rows = outs[a].at[pl.ds(pl.multiple_of(c * half, SUBLANE), half), :]
            copies.append(pltpu.make_async_remote_copy(
                src_ref=rows, dst_ref=rows, send_sem=send_sems.at[a], recv_sem=recv_sems.at[a],
                device_id=(x, y, 1 - c), device_id_type=MESH_ID))
        return _simple(copies)

    return _Hook(wholes, [jax.ShapeDtypeStruct(w.shape, w.dtype) for w in wholes], (n, n), make, aliases={a: a for a in range(n)})


def _allreduce_small(vec, name):
    R, C = vec.shape

    def body(v_ref, o_ref, buf, send_sems, recv_sems):
        x, y, c = _place()
        me = 4 * x + 2 * y + c
        buf[me] = v_ref[...]
        copies = []
        for k in range(1, N_DEV):
            peer = (_flip(x, (k >> 2) & 1), _flip(y, (k >> 1) & 1), _flip(c, k & 1))
            copies.append(pltpu.make_async_remote_copy(
                src_ref=v_ref, dst_ref=buf.at[me], send_sem=send_sems.at[k - 1], recv_sem=recv_sems.at[k - 1],
                device_id=peer, device_id_type=MESH_ID))
        for cp in copies:
            cp.start()
        for cp in copies:
            cp.wait()
        tot = buf[0]
        for d in range(1, N_DEV):
            tot = tot + buf[d]
        o_ref[...] = tot

    return pl.pallas_call(
        body, name=name,
        out_shape=jax.ShapeDtypeStruct((R, C), F32),
        in_specs=[pl.BlockSpec(memory_space=pltpu.VMEM)],
        out_specs=pl.BlockSpec(memory_space=pltpu.VMEM),
        scratch_shapes=[pltpu.VMEM((N_DEV, R, C), F32), pltpu.SemaphoreType.DMA((N_DEV - 1,)), pltpu.SemaphoreType.DMA((N_DEV - 1,))],
    )(vec)


def _unstack(gathered, shape, axis):
    if axis == 0:
        return gathered.reshape(N_CHIPS * shape[0], shape[1])
    return jnp.concatenate([gathered[j] for j in range(N_CHIPS)], axis=1)


def _stack(whole, shape, axis):
    if axis == 0:
        return whole.reshape(N_CHIPS, shape[0], shape[1])
    return jnp.stack([whole[:, j * shape[1]:(j + 1) * shape[1]] for j in range(N_CHIPS)])


GATHER_FIRST = ("w_in", "w_uq", "w_ukv")
GATHER_IN = {"mm_z": ("w_attn_o", "w_ssm_o", "w_out"), "mm_xbc": ("w_gate", "w_up"), "mm_gates": ("w_down", "w_ple_gate", "w_ple")}
REDUCE = (
    (("w_ple", "w_ple_gate", "w_down"), "mmb_down", "mmg_gate", "mmg_up"),
    (("w_gate", "w_up"), "mmb_gate", "mmb_up", "mmg_out"),
    (("w_out", "w_attn_o", "w_ssm_o"), "mmb_ssm_o", "mmg_z", "mmg_xbc"),
    (("w_uq", "w_ukv"), "mmb_ukv", "mmg_gates", "mmb_z"),
    (("w_in",), "mmb_z", "mmb_xbc", "mmb_gates"),
)


class _Exchange:
    def __init__(self, shards, chip, core):
        self.shards, self.chip = shards, chip
        self.core_arr = core.reshape(1).astype(jnp.int32)
        self.place_arr = jnp.stack([chip, core]).astype(jnp.int32)
        self.shape = {n: (shape, axis) for n, shape, axis in BIG}
        self.whole, self.grads, self.reduced, self.pending, self.tails = {}, {}, {}, {}, 0
        hook, done = self._gather(GATHER_FIRST)
        done(_alone(hook, "gather_first"))
        for host, names in GATHER_IN.items():
            self._arm(host, *self._gather(names))

    def _arm(self, host, hook, done):
        self.pending.setdefault(host, []).append((hook, done))

    def _gather(self, names):
        shards = [self.shards[n].astype(BF16) for n in names]

        def done(outs):
            for n, s, g in zip(names, shards, outs):
                self.whole[n] = _unstack(lax.dynamic_update_slice(g, s[None], (self.chip, 0, 0)), *self.shape[n])

        return _gather_hook(shards), done

    def __getitem__(self, name):
        return self.whole[name]

    def __setitem__(self, name, grad):
        self.grads[name] = grad
        for names, swap_host, scatter_host, join_host in REDUCE:
            if name in names and all(n in self.grads for n in names):
                self._reduce(names, swap_host, scatter_host, join_host)

    def _reduce(self, names, swap_host, scatter_host, join_host):
        stacked = [_stack(self.grads[n], *self.shape[n]) for n in names]

        def joined(outs):
            for n, r in zip(names, outs):
                self.reduced[n] = r.reshape(1, *self.shape[n][0])

        def swapped(outs):
            parts = [_add_half(g, o, self.core_arr, "add_half_" + n) for n, g, o in zip(names, stacked, outs)]

            def scattered(gots):
                wholes = [_add_chips(q, o, self.place_arr, "add_chips_" + n) for n, q, o in zip(names, parts, gots)]
                self._arm(join_host, _join_hook(wholes), joined)

            self._arm(scatter_host, _scatter_hook(parts), scattered)

        self._arm(swap_host, _swap_hook(stacked), swapped)

    def _run(self, todo, call):
        hook = _merge_hooks([h for h, _ in todo])
        result, outs = call(hook)
        off = 0
        for h, done in todo:
            done(outs[off:off + len(h.out_shapes)])
            off += len(h.out_shapes)
        return result

    def mm(self, a, b, *, name, **kw):
        todo = self.pending.pop(name, None)
        if not todo:
            return _mm(a, b, name=name, **kw)
        return self._run(todo, lambda hook: _mm(a, b, name=name, hook=hook, **kw))

    def finish(self):
        while self.pending:
            todo = self.pending.pop(next(iter(self.pending)))
            self.tails += 1
            self._run(todo, lambda hook: (None, _alone(hook, "exchange_tail_%d" % self.tails)))
        return self.reduced


def kernel(x, p, positions, mix_norm_pre, mix_norm_post, w_in, q_norm, w_uq, kv_norm, w_ukv, conv_w, conv_b, dt_bias, a_log, d_skip, ssm_norm, w_attn_o, w_ssm_o, w_out, ffn_norm_pre, ffn_norm_post, w_gate, w_up, w_down, ple_norm_pre, ple_norm_post, w_ple_gate, w_ple, loss_target, m_mix_norm_pre, m_mix_norm_post, m_w_in, m_q_norm, m_w_uq, m_kv_norm, m_w_ukv, m_conv_w, m_conv_b, m_dt_bias, m_a_log, m_d_skip, m_ssm_norm, m_w_attn_o, m_w_ssm_o, m_w_out, m_ffn_norm_pre, m_ffn_norm_post, m_w_gate, m_w_up, m_w_down, m_ple_norm_pre, m_ple_norm_post, m_w_ple_gate, m_w_ple, v_mix_norm_pre, v_mix_norm_post, v_w_in, v_q_norm, v_w_uq, v_kv_norm, v_w_ukv, v_conv_w, v_conv_b, v_dt_bias, v_a_log, v_d_skip, v_ssm_norm, v_w_attn_o, v_w_ssm_o, v_w_out, v_ffn_norm_pre, v_ffn_norm_post, v_w_gate, v_w_up, v_w_down, v_ple_norm_pre, v_ple_norm_post, v_w_ple_gate, v_w_ple):
    given = dict(locals())
    names = [n for n, _, _ in BIG] + [n for n, _ in SMALL] + ["conv_w"]
    order = ["mix_norm_pre", "mix_norm_post", "w_in", "q_norm", "w_uq", "kv_norm", "w_ukv", "conv_w", "conv_b", "dt_bias", "a_log",
             "d_skip", "ssm_norm", "w_attn_o", "w_ssm_o", "w_out", "ffn_norm_pre", "ffn_norm_post", "w_gate", "w_up", "w_down",
             "ple_norm_pre", "ple_norm_post", "w_ple_gate", "w_ple"]
    assert sorted(names) == sorted(order)
    cx, cy, cc = _place()
    chip = 2 * cx + cy
    conv_cols = CONV_DIM // N_CHIPS

    ex = _Exchange({n: given[n][0] for n, _, _ in BIG}, chip, cc)
    own = jnp.where(cc == 0, conv_w[0], 0.0)
    conv_vec = lax.dynamic_update_slice(jnp.zeros((CONV_WIDTH, CONV_DIM), F32), own, (0, chip * conv_cols))
    conv_full = _allreduce_small(conv_vec.reshape(CONV_W_LEN // LANE, LANE), "gather_conv_w").reshape(CONV_WIDTH, CONV_DIM)
    sp = {n: given[n] for n, _ in SMALL}
    sp["conv_w"] = conv_full

    loss_part, grad_x, gs = _local_step(x[0], p[0, 0], positions[0], ex, sp, loss_target[0])

    g_big = ex.finish()

    small_parts = [gs[n] for n, _ in SMALL] + [gs["conv_w"], loss_part[:, :1]]
    small_vec = jnp.concatenate([t.reshape(-1) for t in small_parts])
    small_vec = jnp.pad(small_vec, (0, SMALL_ROWS * LANE - small_vec.shape[0])).reshape(SMALL_ROWS, LANE)
    small_sum = _allreduce_small(small_vec, "allreduce_small").reshape(-1)
    g_small, off = {}, 0
    for n, length in SMALL:
        g_small[n] = small_sum[off:off + length].reshape(1, length)
        off += length
    g_conv = small_sum[off:off + CONV_W_LEN].reshape(CONV_WIDTH, CONV_DIM)
    g_small["conv_w"] = lax.dynamic_slice(g_conv, (0, chip * conv_cols), (CONV_WIDTH, conv_cols)).reshape(1, CONV_WIDTH, conv_cols)
    loss = small_sum[off + CONV_W_LEN]

    grads, deltas, new_m, new_v = [], [], [], []
    for n in order:
        g = g_big[n] if n in g_big else g_small[n]
        d, m_, v_ = _adamw(given[n], g, given["m_" + n], given["v_" + n], "adamw_" + n)
        grads.append(g)
        deltas.append(d)
        new_m.append(m_)
        new_v.append(v_)
    return (loss, grad_x.reshape(x.shape), *grads, *deltas, *new_m, *new_v)
```

```python
import numpy as np
import jax
import jax.numpy as jnp
from jax import lax
from jax.experimental import pallas as pl
from jax.experimental.pallas import tpu as pltpu

F32 = jnp.float32
BF16 = jnp.bfloat16

D_MODEL = 2048
N_HEADS_MLA = 16
Q_LORA = 512
KV_LORA = 512
QK_NOPE = 128
QK_ROPE = 64
V_DIM = 128
QK_DIM = QK_NOPE + QK_ROPE
ROPE_THETA = 10000.0
D_INNER = 4096
SSM_HEADDIM = 64
N_HEADS_SSM = 64
SSM_GROUPS = 8
HEADS_PER_GROUP = 8
D_STATE = 128
CONV_WIDTH = 4
CHUNK = 256
CONV_DIM = D_INNER + 2 * SSM_GROUPS * D_STATE
D_FF = 5632
PLE_DIM = 256
EPS = 1e-6
IN_SPLITS = (Q_LORA, KV_LORA, QK_ROPE, D_INNER, CONV_DIM, N_HEADS_SSM, D_MODEL, D_MODEL)

ADAM_LR = 0.001
ADAM_B1 = 0.9
ADAM_B2 = 0.999
ADAM_EPS = 1e-08
ADAM_WD = 0.01
ADAM_STEP = 10

LANE = 128
SUBLANE = 8
HEAD_PAD = 256
VMEM_LIMIT = 56 * 1024 * 1024
ATTN_TILE = 1024
NEG = -1e30

MESH_ID = pl.DeviceIdType.MESH
N_CHIPS = 4
N_DEV = 8


def _tile(n, pref, mult=LANE):
    if n <= pref:
        return n
    t = (pref // mult) * mult
    while t >= mult:
        if n % t == 0:
            return t
        t -= mult
    return n


def _params(sem, vmem=VMEM_LIMIT, **kw):
    return pltpu.CompilerParams(dimension_semantics=sem, vmem_limit_bytes=vmem, **kw)


class _Hook:
    def __init__(self, ins, out_shapes, sems, make, aliases=None):
        self.ins, self.out_shapes, self.sems, self.make, self.aliases = list(ins), list(out_shapes), tuple(sems), make, dict(aliases or {})


def _merge_hooks(hooks):
    hooks = [h for h in hooks if h is not None]
    if not hooks:
        return None
    ins, outs, sems, aliases, cuts = [], [], [], {}, []
    for h in hooks:
        cuts.append((len(ins), len(outs), len(sems)))
        aliases.update({len(ins) + i: len(outs) + o for i, o in h.aliases.items()})
        ins += h.ins
        outs += h.out_shapes
        sems += h.sems

    def make(in_refs, out_refs, sem_refs):
        pairs = []
        for h, (i0, o0, s0) in zip(hooks, cuts):
            pairs.append(h.make(in_refs[i0:i0 + len(h.ins)], out_refs[o0:o0 + len(h.out_shapes)], sem_refs[s0:s0 + len(h.sems)]))

        def start():
            for st, _ in pairs:
                st()

        def finish():
            for _, fin in pairs:
                fin()

        return start, finish

    return _Hook(ins, outs, sems, make, aliases)


def _mm(a, b, *, ta=False, tb=False, add=None, out_dtype=F32, name, tm=1024, tn=1536, tk=2048, hook=None):
    if ta:
        K, M = a.shape
    else:
        M, K = a.shape
    N = b.shape[0] if tb else b.shape[1]
    assert (b.shape[1] if tb else b.shape[0]) == K, (a.shape, b.shape, ta, tb)
    tm, tn, tk = _tile(M, tm), _tile(N, tn), _tile(K, tk)
    nk = K // tk
    dn = (((0 if ta else 1,), (1 if tb else 0,)), ((), ()))
    has_add = add is not None
    n_own = 3 if has_add else 2
    n_hin = len(hook.ins) if hook else 0
    n_hout = len(hook.out_shapes) if hook else 0
    grid = (M // tm, N // tn, nk)

    def body(*refs):
        a_ref, b_ref = refs[:2]
        c_ref = refs[2] if has_add else None
        o_ref = refs[n_own + n_hin]
        scratch = refs[n_own + n_hin + 1 + n_hout:]
        if hook:
            start, finish = hook.make(refs[n_own:n_own + n_hin], refs[n_own + n_hin + 1:n_own + n_hin + 1 + n_hout],
                                      scratch[len(scratch) - len(hook.sems):])
            ids = [pl.program_id(d) for d in range(3)]
            pl.when((ids[0] == 0) & (ids[1] == 0) & (ids[2] == 0))(start)
        prod = lax.dot_general(a_ref[...].astype(BF16), b_ref[...].astype(BF16), dn, preferred_element_type=F32)
        if nk == 1:
            o_ref[...] = ((c_ref[...] + prod) if has_add else prod).astype(out_dtype)
        else:
            acc = scratch[0]
            k = pl.program_id(2)

            @pl.when(k == 0)
            def _():
                acc[...] = (c_ref[...] + prod) if has_add else prod

            @pl.when(k > 0)
            def _():
                acc[...] += prod

            @pl.when(k == nk - 1)
            def _():
                o_ref[...] = acc[...].astype(out_dtype)
        if hook:
            pl.when((ids[0] == grid[0] - 1) & (ids[1] == grid[1] - 1) & (ids[2] == grid[2] - 1))(finish)

    a_spec = pl.BlockSpec((tk, tm), lambda i, j, k: (k, i)) if ta else pl.BlockSpec((tm, tk), lambda i, j, k: (i, k))
    b_spec = pl.BlockSpec((tn, tk), lambda i, j, k: (j, k)) if tb else pl.BlockSpec((tk, tn), lambda i, j, k: (k, j))
    in_specs = [a_spec, b_spec]
    args = [a, b]
    if has_add:
        in_specs.append(pl.BlockSpec((tm, tn), lambda i, j, k: (i, j)))
        args.append(add)
    hbm = pl.BlockSpec(memory_space=pl.ANY)
    scratch_shapes = [pltpu.VMEM((tm, tn), F32)] if nk > 1 else []
    out_shape = jax.ShapeDtypeStruct((M, N), out_dtype)
    out_spec = pl.BlockSpec((tm, tn), lambda i, j, k: (i, j))
    if not hook:
        return pl.pallas_call(
            body, name=name, out_shape=out_shape, grid=grid, in_specs=in_specs, out_specs=out_spec,
            scratch_shapes=scratch_shapes,
            compiler_params=_params(("parallel", "parallel", "arbitrary")),
        )(*args)
    outs = pl.pallas_call(
        body, name=name, out_shape=(out_shape, *hook.out_shapes), grid=grid,
        in_specs=in_specs + [hbm] * n_hin, out_specs=(out_spec, *[hbm] * n_hout),
        scratch_shapes=scratch_shapes + [pltpu.SemaphoreType.DMA((s,)) for s in hook.sems],
        input_output_aliases={n_own + i: 1 + o for i, o in hook.aliases.items()},
        compiler_params=_params(("arbitrary", "arbitrary", "arbitrary")),
    )(*args, *hook.ins)
    return outs[0], list(outs[1:])


def _row(arr, width=None, cblk=0):
    return ("row", arr, arr.shape[1] if width is None else width, cblk)


def _full(arr):
    return ("full", arr)


def _prev8(arr):
    return ("prev8", arr)


def _next8(arr):
    return ("next8", arr)


def _rows(fn, n_rows, tm, ins, outs, name):
    tm = min(tm, n_rows)
    assert n_rows % tm == 0 and tm % SUBLANE == 0
    n = n_rows // tm
    in_specs, args = [], []
    for spec in ins:
        kind, arr = spec[0], spec[1]
        if kind == "row":
            _, _, w, cb = spec
            in_specs.append(pl.BlockSpec((tm, w), lambda i, cb=cb: (i, cb)))
        elif kind == "full":
            in_specs.append(pl.BlockSpec(arr.shape, lambda i, nd=arr.ndim: (0,) * nd))
        elif kind == "prev8":
            in_specs.append(pl.BlockSpec((SUBLANE, arr.shape[1]),
                                         lambda i: (jnp.maximum(i * (tm // SUBLANE) - 1, 0), 0)))
        elif kind == "next8":
            last = n_rows // SUBLANE - 1
            in_specs.append(pl.BlockSpec((SUBLANE, arr.shape[1]),
                                         lambda i: (jnp.minimum((i + 1) * (tm // SUBLANE), last), 0)))
        else:
            raise ValueError(kind)
        args.append(arr)
    out_shapes, out_specs = [], []
    any_acc = False
    for spec in outs:
        if spec[0] == "row":
            _, w, dt = spec
            out_shapes.append(jax.ShapeDtypeStruct((n_rows, w), dt))
            out_specs.append(pl.BlockSpec((tm, w), lambda i: (i, 0)))
        else:
            _, shp, dt = spec
            any_acc = True
            out_shapes.append(jax.ShapeDtypeStruct(shp, dt))
            out_specs.append(pl.BlockSpec(shp, lambda i, nd=len(shp): (0,) * nd))
    nin = len(ins)

    def body(*refs):
        i = pl.program_id(0)
        vals = fn(i, n, *[r[...] for r in refs[:nin]])
        for o_ref, spec, v in zip(refs[nin:], outs, vals):
            if spec[0] == "acc":
                @pl.when(i == 0)
                def _(o_ref=o_ref):
                    o_ref[...] = jnp.zeros_like(o_ref)

                o_ref[...] += v.astype(o_ref.dtype)
            else:
                o_ref[...] = v.astype(o_ref.dtype)

    res = pl.pallas_call(
        body, name=name,
        out_shape=tuple(out_shapes),
        grid=(n,),
        in_specs=in_specs,
        out_specs=tuple(out_specs),
        compiler_params=_params(("arbitrary",) if any_acc else ("parallel",)),
    )(*args)
    return res


def _rstd(x):
    return lax.rsqrt(jnp.mean(x * x, axis=-1, keepdims=True) + EPS)


def _norm_bwd(x, r, g, dy):
    xh = x * r
    dyg = dy * g
    dx = r * (dyg - xh * jnp.mean(dyg * xh, axis=-1, keepdims=True))
    return dx, dy * xh


def _sigmoid(x):
    return 0.5 * jnp.tanh(0.5 * x) + 0.5


def _colsum(v):
    return jnp.sum(v, axis=0, keepdims=True)


def _rope_tables(pos, invf):
    ang = pos.astype(F32) * invf
    lane = lax.broadcasted_iota(jnp.int32, ang.shape, 1)
    cos, sin = jnp.cos(ang), jnp.sin(ang)
    ct = jnp.where(lane < QK_ROPE, cos, 0.0)
    sa = jnp.where(lane < QK_ROPE // 2, -sin, 0.0)
    sb = jnp.where((lane >= QK_ROPE // 2) & (lane < QK_ROPE), sin, 0.0)
    return ct, sa, sb


def _rope(b, ct, sa, sb):
    return ct * b + sa * pltpu.roll(b, LANE - QK_ROPE // 2, 1) + sb * pltpu.roll(b, QK_ROPE // 2, 1)


def _rope_t(d, ct, sa, sb):
    return ct * d + pltpu.roll(sa * d, QK_ROPE // 2, 1) + pltpu.roll(sb * d, LANE - QK_ROPE // 2, 1)


def _rope_fwd(q_raw, kr_pad, pos_col, invf):
    S = q_raw.shape[0]

    def fn(i, n, q, kr, pos, invf):
        ct, sa, sb = _rope_tables(pos, invf)
        parts = []
        for h in range(N_HEADS_MLA):
            parts.append(q[:, h * HEAD_PAD:h * HEAD_PAD + LANE])
            parts.append(_rope(q[:, h * HEAD_PAD + LANE:(h + 1) * HEAD_PAD], ct, sa, sb))
        return jnp.concatenate(parts, axis=1), _rope(kr, ct, sa, sb)

    return _rows(fn, S, 256, [_row(q_raw), _row(kr_pad), _row(pos_col), _full(invf)],
                 [("row", N_HEADS_MLA * HEAD_PAD, BF16), ("row", LANE, BF16)], "rope_fwd")


def _rope_bwd(dq, dkp, pos_col, invf):
    S = dq.shape[0]
    tm = 256

    def body(dq_ref, dkp_ref, pos_ref, invf_ref, dqo_ref, dkr_ref):
        ct, sa, sb = _rope_tables(pos_ref[...], invf_ref[...])
        for h in range(N_HEADS_MLA):
            dqo_ref[:, h * HEAD_PAD:h * HEAD_PAD + LANE] = dq_ref[:, h * HEAD_PAD:h * HEAD_PAD + LANE].astype(BF16)
            dqo_ref[:, h * HEAD_PAD + LANE:(h + 1) * HEAD_PAD] = _rope_t(
                dq_ref[:, h * HEAD_PAD + LANE:(h + 1) * HEAD_PAD], ct, sa, sb).astype(BF16)
        tot = dkp_ref[0]
        for h in range(1, N_HEADS_MLA):
            tot = tot + dkp_ref[h]
        dkr_ref[...] = _rope_t(tot, ct, sa, sb).astype(BF16)

    return pl.pallas_call(
        body, name="rope_bwd",
        out_shape=(jax.ShapeDtypeStruct(dq.shape, BF16), jax.ShapeDtypeStruct((S, LANE), BF16)),
        grid=(S // tm,),
        in_specs=[pl.BlockSpec((tm, dq.shape[1]), lambda i: (i, 0)),
                  pl.BlockSpec((N_HEADS_MLA, tm, LANE), lambda i: (0, i, 0)),
                  pl.BlockSpec((tm, 1), lambda i: (i, 0)),
                  pl.BlockSpec((1, LANE), lambda i: (0, 0))],
        out_specs=(pl.BlockSpec((tm, dq.shape[1]), lambda i: (i, 0)), pl.BlockSpec((tm, LANE), lambda i: (i, 0))),
        compiler_params=_params(("parallel",)),
    )(dq, dkp, pos_col, invf)


def _row_of(col, n):
    eye = lax.broadcasted_iota(jnp.int32, (LANE, LANE), 0) == lax.broadcasted_iota(jnp.int32, (LANE, LANE), 1)
    parts = [jnp.sum(jnp.where(eye, col[i:i + LANE], 0.0), axis=0, keepdims=True) for i in range(0, n, LANE)]
    return parts[0] if len(parts) == 1 else jnp.concatenate(parts, axis=1)


def _attn_fwd(q, kv, kp, tile):
    S = q.shape[0]
    nq = S // tile
    scale = QK_DIM ** -0.5
    nt = (((1,), (1,)), ((), ()))

    def body(q_ref, kv_ref, kp_ref, o_ref, lse_ref, m_s, l_s, acc_s, s_buf):
        qi = pl.program_id(1)
        qv = q_ref[...]
        m_s[...] = jnp.full_like(m_s, NEG)
        l_s[...] = jnp.zeros_like(l_s)
        acc_s[...] = jnp.zeros_like(acc_s)

        def scores(j):
            start = pl.multiple_of(j * tile, tile)
            k = jnp.concatenate([kv_ref[pl.ds(start, tile), 0:LANE], kp_ref[pl.ds(start, tile), :]], axis=1)
            return lax.dot_general(qv, k, nt, preferred_element_type=F32) * scale

        def update(s, j):
            v = kv_ref[pl.ds(pl.multiple_of(j * tile, tile), tile), LANE:2 * LANE]
            m_old = m_s[...]
            m_new = jnp.maximum(m_old, jnp.max(s, axis=1, keepdims=True))
            alpha = jnp.exp(m_old - m_new)
            p = jnp.exp(s - m_new)
            l_s[...] = alpha * l_s[...] + jnp.sum(p, axis=1, keepdims=True)
            acc_s[...] = alpha * acc_s[...] + jnp.dot(p.astype(BF16), v, preferred_element_type=F32)
            m_s[...] = m_new

        s_buf[0] = scores(0)

        def loop_body(j, carry):
            nxt = scores(j + 1)
            update(s_buf[lax.rem(j, 2)], j)
            s_buf[lax.rem(j + 1, 2)] = nxt
            return carry

        lax.fori_loop(0, qi, loop_body, 0)
        s = s_buf[lax.rem(qi, 2)]
        row = lax.broadcasted_iota(jnp.int32, s.shape, 0)
        col = lax.broadcasted_iota(jnp.int32, s.shape, 1)
        update(jnp.where(row >= col, s, NEG), qi)
        l = l_s[...]
        o_ref[...] = (acc_s[...] / l).astype(o_ref.dtype)
        lse_ref[0, 0] = _row_of(m_s[...] + jnp.log(l), tile)

    return pl.pallas_call(
        body, name="attn_fwd",
        out_shape=(jax.ShapeDtypeStruct((S, N_HEADS_MLA * V_DIM), BF16),
                   jax.ShapeDtypeStruct((N_HEADS_MLA, nq, 1, tile), F32)),
        grid=(N_HEADS_MLA, nq),
        in_specs=[pl.BlockSpec((tile, HEAD_PAD), lambda h, i: (i, h)),
                  pl.BlockSpec((S, HEAD_PAD), lambda h, i: (0, h)),
                  pl.BlockSpec((S, LANE), lambda h, i: (0, 0))],
        out_specs=(pl.BlockSpec((tile, V_DIM), lambda h, i: (i, h)),
                   pl.BlockSpec((1, 1, 1, tile), lambda h, i: (h, i, 0, 0))),
        scratch_shapes=[pltpu.VMEM((tile, 1), F32), pltpu.VMEM((tile, 1), F32), pltpu.VMEM((tile, V_DIM), F32),
                        pltpu.VMEM((2, tile, tile), F32)],
        compiler_params=_params(("parallel", "arbitrary")),
    )(q, kv, kp)


def _attn_delta(o, do, tile):
    S = o.shape[0]
    nq = S // tile

    def body(o_ref, do_ref, d_ref, dob_ref):
        dov = do_ref[...]
        prod = o_ref[...].astype(F32) * dov
        dob_ref[...] = dov.astype(BF16)
        for h in range(N_HEADS_MLA):
            col = jnp.sum(prod[:, h * V_DIM:(h + 1) * V_DIM], axis=1, keepdims=True)
            d_ref[h, 0] = _row_of(col, tile)

    return pl.pallas_call(
        body, name="attn_delta",
        out_shape=(jax.ShapeDtypeStruct((N_HEADS_MLA, nq, 1, tile), F32), jax.ShapeDtypeStruct(o.shape, BF16)),
        grid=(nq,),
        in_specs=[pl.BlockSpec((tile, o.shape[1]), lambda i: (i, 0)), pl.BlockSpec((tile, o.shape[1]), lambda i: (i, 0))],
        out_specs=(pl.BlockSpec((N_HEADS_MLA, 1, 1, tile), lambda i: (0, i, 0, 0)),
                   pl.BlockSpec((tile, o.shape[1]), lambda i: (i, 0))),
        compiler_params=_params(("parallel",)),
    )(o, do)


def _attn_bwd(q, kv, kp, do, lse, delta, tile):
    S = q.shape[0]
    nq = S // tile
    scale = QK_DIM ** -0.5
    nt = (((1,), (1,)), ((), ()))
    tn = (((0,), (0,)), ((), ()))

    def body(kv_ref, kp_ref, q_ref, do_ref, lse_ref, d_ref, dq_ref, dkv_ref, dkp_ref, dk_s, dv_s):
        ki = pl.program_id(1)
        k = jnp.concatenate([kv_ref[:, 0:LANE], kp_ref[...]], axis=1)
        v = kv_ref[:, LANE:2 * LANE]

        @pl.when(ki == 0)
        def _():
            dq_ref[...] = jnp.zeros_like(dq_ref)

        dk_s[...] = jnp.zeros_like(dk_s)
        dv_s[...] = jnp.zeros_like(dv_s)

        def step(qi, masked):
            start = pl.multiple_of(qi * tile, tile)
            qv = q_ref[pl.ds(start, tile), :]
            dov = do_ref[pl.ds(start, tile), :]
            st = lax.dot_general(k, qv, nt, preferred_element_type=F32) * scale
            pt = jnp.exp(st - lse_ref[0, qi])
            if masked:
                krow = lax.broadcasted_iota(jnp.int32, pt.shape, 0)
                qcol = lax.broadcasted_iota(jnp.int32, pt.shape, 1)
                pt = jnp.where(krow <= qcol, pt, 0.0)
            dv_s[...] += jnp.dot(pt.astype(BF16), dov, preferred_element_type=F32)
            dpt = lax.dot_general(v, dov, nt, preferred_element_type=F32)
            dst = (pt * (dpt - d_ref[0, qi]) * scale).astype(BF16)
            dk_s[...] += jnp.dot(dst, qv, preferred_element_type=F32)
            dq_ref[pl.ds(start, tile), :] += lax.dot_general(dst, k, tn, preferred_element_type=F32)

        step(ki, True)

        def loop_body(qi, carry):
            step(qi, False)
            return carry

        lax.fori_loop(ki + 1, nq, loop_body, 0)
        dkv_ref[...] = jnp.concatenate([dk_s[:, 0:LANE], dv_s[...]], axis=1).astype(dkv_ref.dtype)
        dkp_ref[0] = dk_s[:, LANE:2 * LANE]

    return pl.pallas_call(
        body, name="attn_bwd",
        out_shape=(jax.ShapeDtypeStruct((S, N_HEADS_MLA * HEAD_PAD), F32),
                   jax.ShapeDtypeStruct((S, N_HEADS_MLA * HEAD_PAD), BF16),
                   jax.ShapeDtypeStruct((N_HEADS_MLA, S, LANE), F32)),
        grid=(N_HEADS_MLA, nq),
        in_specs=[pl.BlockSpec((tile, HEAD_PAD), lambda h, i: (i, h)),
                  pl.BlockSpec((tile, LANE), lambda h, i: (i, 0)),
                  pl.BlockSpec((S, HEAD_PAD), lambda h, i: (0, h)),
                  pl.BlockSpec((S, V_DIM), lambda h, i: (0, h)),
                  pl.BlockSpec((1, nq, 1, tile), lambda h, i: (h, 0, 0, 0)),
                  pl.BlockSpec((1, nq, 1, tile), lambda h, i: (h, 0, 0, 0))],
        out_specs=(pl.BlockSpec((S, HEAD_PAD), lambda h, i: (0, h)),
                   pl.BlockSpec((tile, HEAD_PAD), lambda h, i: (i, h)),
                   pl.BlockSpec((1, tile, LANE), lambda h, i: (h, i, 0))),
        scratch_shapes=[pltpu.VMEM((tile, HEAD_PAD), F32), pltpu.VMEM((tile, V_DIM), F32)],
        compiler_params=_params(("parallel", "arbitrary")),
    )(kv, kp, q, do, lse, delta)


def _shift_down(cur, halo, k):
    sh = pltpu.roll(cur, k, 0)
    hs = pltpu.roll(halo, k, 0)
    rows = lax.broadcasted_iota(jnp.int32, hs.shape, 0)
    first = jnp.where(rows < k, hs, sh[0:SUBLANE])
    if cur.shape[0] == SUBLANE:
        return first
    return jnp.concatenate([first, sh[SUBLANE:]], axis=0)


def _shift_up(cur, nxt, k):
    n = cur.shape[0]
    sh = pltpu.roll(cur, n - k, 0)
    ns = pltpu.roll(nxt, SUBLANE - k, 0)
    rows = lax.broadcasted_iota(jnp.int32, ns.shape, 0)
    last = jnp.where(rows >= SUBLANE - k, ns, sh[n - SUBLANE:])
    if n == SUBLANE:
        return last
    return jnp.concatenate([sh[:n - SUBLANE], last], axis=0)


def _conv_pre(cur, halo, w, b):
    shifted = [_shift_down(cur, halo, k) for k in range(1, CONV_WIDTH)]
    out = b + w[3:4] * cur
    for k in range(1, CONV_WIDTH):
        out = out + w[3 - k:4 - k] * shifted[k - 1]
    return out, shifted


def _conv_fwd(xbc, w, b):
    S = xbc.shape[0]

    def fn(i, n, cur, prev, w, b):
        halo = jnp.where(i > 0, prev, 0.0)
        pre, _ = _conv_pre(cur, halo, w, b)
        return (pre * _sigmoid(pre),)

    return _rows(fn, S, 256, [_row(xbc), _prev8(xbc), _full(w), _full(b)], [("row", xbc.shape[1], F32)], "conv_fwd")[0]


def _conv_bwd(xbc, dacts, w, b):
    S, C = xbc.shape

    def dsilu(pre):
        s = _sigmoid(pre)
        return s * (1.0 + pre * (1.0 - s))

    def fn(i, n, cur, prev, nxt, *rest):
        k3 = len(dacts)
        dcur = jnp.concatenate(rest[:k3], axis=1)
        dnxt = jnp.concatenate(rest[k3:2 * k3], axis=1)
        w, b = rest[2 * k3:]
        halo = jnp.where(i > 0, prev, 0.0)
        pre, shifted = _conv_pre(cur, halo, w, b)
        dpre = dcur * dsilu(pre)
        pre_n, _ = _conv_pre(nxt, cur[cur.shape[0] - SUBLANE:], w, b)
        dpre_n = jnp.where(i < n - 1, dnxt * dsilu(pre_n), 0.0)
        dx = w[3:4] * dpre
        rows = lax.broadcasted_iota(jnp.int32, (SUBLANE, C), 0)
        dw = jnp.where(rows == 3, _colsum(dpre * cur), 0.0)
        for k in range(1, CONV_WIDTH):
            dx = dx + w[3 - k:4 - k] * _shift_up(dpre, dpre_n, k)
            dw = dw + jnp.where(rows == 3 - k, _colsum(dpre * shifted[k - 1]), 0.0)
        return dx, dw, _colsum(dpre)

    return _rows(fn, S, 256, [_row(xbc), _prev8(xbc), _next8(xbc), *[_row(d) for d in dacts], *[_next8(d) for d in dacts],
                              _full(w), _full(b)],
                 [("row", C, BF16), ("acc", (SUBLANE, C), F32), ("acc", (1, C), F32)], "conv_bwd")


def _softplus(x):
    return jnp.maximum(x, 0.0) + jnp.log1p(jnp.exp(-jnp.abs(x)))


def _cumsum_rows(x):
    rows = lax.broadcasted_iota(jnp.int32, x.shape, 0)
    s = 1
    while s < x.shape[0]:
        x = x + jnp.where(rows >= s, pltpu.roll(x, s, 0), 0.0)
        s *= 2
    return x


def _revcumsum_rows(x):
    n = x.shape[0]
    rows = lax.broadcasted_iota(jnp.int32, x.shape, 0)
    s = 1
    while s < n:
        x = x + jnp.where(rows < n - s, pltpu.roll(x, n - s, 0), 0.0)
        s *= 2
    return x


def _dt_prep(dt_raw, dt_bias, a_log):
    S = dt_raw.shape[0]

    def body(raw_ref, bias_ref, alog_ref, dt_ref, cum_ref, cumt_ref):
        dt = _softplus(raw_ref[...] + bias_ref[...])
        cum = _cumsum_rows(dt * (-jnp.exp(alog_ref[...])))
        dt_ref[...] = dt
        cum_ref[...] = cum
        cumt_ref[...] = cum.T

    return pl.pallas_call(
        body, name="dt_prep",
        out_shape=(jax.ShapeDtypeStruct((S, LANE), F32), jax.ShapeDtypeStruct((S, LANE), F32),
                   jax.ShapeDtypeStruct((LANE, S), F32)),
        grid=(S // CHUNK,),
        in_specs=[pl.BlockSpec((CHUNK, LANE), lambda i: (i, 0)), pl.BlockSpec((1, LANE), lambda i: (0, 0)),
                  pl.BlockSpec((1, LANE), lambda i: (0, 0))],
        out_specs=(pl.BlockSpec((CHUNK, LANE), lambda i: (i, 0)), pl.BlockSpec((CHUNK, LANE), lambda i: (i, 0)),
                   pl.BlockSpec((LANE, CHUNK), lambda i: (0, i))),
        compiler_params=_params(("parallel",)),
    )(dt_raw, dt_bias, a_log)


_NT = (((1,), (1,)), ((), ()))
_TN = (((0,), (0,)), ((), ()))
P = SSM_HEADDIM
GW = HEADS_PER_GROUP * SSM_HEADDIM


PAIRS = HEADS_PER_GROUP // 2
SPREAD_W = HEADS_PER_GROUP * LANE


def _spread_matrix():
    e = np.zeros((SSM_GROUPS, LANE, SPREAD_W), np.float32)
    for g in range(SSM_GROUPS):
        for r in range(HEADS_PER_GROUP):
            e[g, g * HEADS_PER_GROUP + r, r * LANE:(r + 1) * LANE] = 1.0
    return jnp.asarray(e, BF16)


def _pieces(v, n):
    out = []
    for _ in range(n):
        p = v.astype(BF16)
        out.append(p)
        v = v - p.astype(F32)
    return out


def _spread(v, e, n):
    tot = None
    for p in _pieces(v, n):
        t = jnp.dot(p, e, preferred_element_type=F32)
        tot = t if tot is None else tot + t
    return tot


def _gather_rows(z, e):
    hi, lo = _pieces(z, 2)
    return lax.dot_general(hi, e, _NT, preferred_element_type=F32) + lax.dot_general(lo, e, _NT, preferred_element_type=F32)


def _decay_pair(cc, cr, transposed):
    L = cc.shape[0]
    halves = []
    for h in range(L // LANE):
        i = lax.broadcasted_iota(jnp.int32, (L, LANE), 0)
        j = lax.broadcasted_iota(jnp.int32, (L, LANE), 1) + h * LANE
        crh = cr[:, h * LANE:(h + 1) * LANE]
        if transposed:
            halves.append(jnp.exp(jnp.where(j >= i, crh - cc, NEG)))
        else:
            halves.append(jnp.exp(jnp.where(i >= j, cc - crh, NEG)))
    return jnp.concatenate(halves, axis=1)


def _ssd_fwd(xbc_c, dt, cum, cumt_g, spread):
    S = xbc_c.shape[0]
    nc = S // CHUNK
    L = CHUNK
    boff = D_INNER // D_STATE

    def body(x_ref, b_ref, c_ref, dt_ref, cum_ref, cumt_ref, e_ref, y_ref, st_ref, state):
        c = pl.program_id(1)

        @pl.when(c == 0)
        def _():
            state[...] = jnp.zeros_like(state)

        e = e_ref[0]
        bm = b_ref[...].astype(BF16)
        cm = c_ref[...].astype(BF16)
        cb = lax.dot_general(cm, bm, _NT, preferred_element_type=F32)
        rep_cum = _spread(cum_ref[...], e, 3)
        rep_dt = _spread(dt_ref[...], e, 2)
        lo = lax.broadcasted_iota(jnp.int32, (L, LANE), 1) < P
        lo1 = lax.broadcasted_iota(jnp.int32, (1, LANE), 1) < P
        top = lax.broadcasted_iota(jnp.int32, (LANE, LANE), 0) < P
        for p in range(PAIRS):
            t0, t1 = 2 * p * LANE, (2 * p + 1) * LANE
            cc0, cc1 = rep_cum[:, t0:t0 + LANE], rep_cum[:, t1:t1 + LANE]
            ccp = jnp.where(lo, cc0, cc1)
            cl0, cl1 = cc0[L - 1:L, :], cc1[L - 1:L, :]
            clp = jnp.where(lo1, cl0, cl1)
            xdt = x_ref[:, p * LANE:(p + 1) * LANE] * jnp.where(lo, rep_dt[:, t0:t0 + LANE], rep_dt[:, t1:t1 + LANE])
            xdb = xdt.astype(BF16)
            ys = []
            for r, cc in ((2 * p, cc0), (2 * p + 1, cc1)):
                m = (cb * _decay_pair(cc, cumt_ref[0, r:r + 1, :], False)).astype(BF16)
                ys.append(jnp.dot(m, xdb, preferred_element_type=F32))
            st = state[p * LANE:(p + 1) * LANE, :]
            st_ref[0, 0, p * LANE:(p + 1) * LANE, :] = st
            yoff = lax.dot_general(cm, st.astype(BF16), _NT, preferred_element_type=F32) * jnp.exp(ccp)
            y_ref[:, p * LANE:(p + 1) * LANE] = jnp.where(lo, ys[0], ys[1]) + yoff
            wend = jnp.exp(clp - ccp)
            ecl = jnp.where(top, jnp.exp(cl0), jnp.exp(cl1))
            state[p * LANE:(p + 1) * LANE, :] = st * ecl + lax.dot_general(
                (xdt * wend).astype(BF16), bm, _TN, preferred_element_type=F32)

    return pl.pallas_call(
        body, name="ssd_fwd",
        out_shape=(jax.ShapeDtypeStruct((S, D_INNER), F32), jax.ShapeDtypeStruct((SSM_GROUPS, nc, GW, D_STATE), F32)),
        grid=(SSM_GROUPS, nc),
        in_specs=[pl.BlockSpec((L, GW), lambda g, c: (c, g)),
                  pl.BlockSpec((L, D_STATE), lambda g, c: (c, boff + g)),
                  pl.BlockSpec((L, D_STATE), lambda g, c: (c, boff + SSM_GROUPS + g)),
                  pl.BlockSpec((L, LANE), lambda g, c: (c, 0)),
                  pl.BlockSpec((L, LANE), lambda g, c: (c, 0)),
                  pl.BlockSpec((1, HEADS_PER_GROUP, L), lambda g, c: (g, 0, c)),
                  pl.BlockSpec((1, LANE, SPREAD_W), lambda g, c: (g, 0, 0))],
        out_specs=(pl.BlockSpec((L, GW), lambda g, c: (c, g)),
                   pl.BlockSpec((1, 1, GW, D_STATE), lambda g, c: (g, c, 0, 0))),
        scratch_shapes=[pltpu.VMEM((GW, D_STATE), F32)],
        compiler_params=_params(("parallel", "arbitrary")),
    )(xbc_c, xbc_c, xbc_c, dt, cum, cumt_g, spread)


def _ssd_bwd(xbc_c, dt, cum, cumt_g, spread, states, dy, dx_skip):
    S = xbc_c.shape[0]
    nc = S // CHUNK
    L = CHUNK
    boff = D_INNER // D_STATE
    rev = lambda c: nc - 1 - c

    def body(x_ref, b_ref, c_ref, dt_ref, cum_ref, cumt_ref, e_ref, st_ref, dy_ref, skip_ref,
             dx_ref, db_ref, dc_ref, ddt_ref, dcum_ref, dstate):
        c = pl.program_id(1)

        @pl.when(c == 0)
        def _():
            dstate[...] = jnp.zeros_like(dstate)

        e = e_ref[0]
        bf = b_ref[...]
        bm = bf.astype(BF16)
        cm = c_ref[...].astype(BF16)
        cb = lax.dot_general(cm, bm, _NT, preferred_element_type=F32)
        cbt = lax.dot_general(bm, cm, _NT, preferred_element_type=F32)
        rep_cum = _spread(cum_ref[...], e, 3)
        rep_dt = _spread(dt_ref[...], e, 2)
        lane = lax.broadcasted_iota(jnp.int32, (L, LANE), 1)
        lo = lane < P
        lo1 = lax.broadcasted_iota(jnp.int32, (1, LANE), 1) < P
        top = lax.broadcasted_iota(jnp.int32, (LANE, LANE), 0) < P
        last = lax.broadcasted_iota(jnp.int32, (L, LANE), 0) == L - 1
        dcb = jnp.zeros((L, L), F32)
        dcbt = jnp.zeros((L, L), F32)
        dbs = jnp.zeros((L, D_STATE), F32)
        dcs = jnp.zeros((L, D_STATE), F32)
        zs, zds = [], []
        for p in range(PAIRS):
            sl = slice(p * LANE, (p + 1) * LANE)
            t0, t1 = 2 * p * LANE, (2 * p + 1) * LANE
            cc0, cc1 = rep_cum[:, t0:t0 + LANE], rep_cum[:, t1:t1 + LANE]
            ccp = jnp.where(lo, cc0, cc1)
            cl0, cl1 = cc0[L - 1:L, :], cc1[L - 1:L, :]
            w0, w1 = jnp.exp(cl0 - cc0), jnp.exp(cl1 - cc1)
            wend = jnp.where(lo, w0, w1)
            ecc = jnp.exp(ccp)
            ecl0, ecl1 = jnp.exp(cl0), jnp.exp(cl1)
            dtp = jnp.where(lo, rep_dt[:, t0:t0 + LANE], rep_dt[:, t1:t1 + LANE])
            xp = x_ref[:, sl]
            xdt = xp * dtp
            xdb = xdt.astype(BF16)
            dyp = dy_ref[:, sl]
            st = st_ref[0, 0, sl, :]
            stb = st.astype(BF16)
            ds = dstate[sl, :]
            dsb = ds.astype(BF16)
            yoff = lax.dot_general(cm, stb, _NT, preferred_element_type=F32) * ecc
            dye = (dyp * ecc).astype(BF16)
            dcs = dcs + jnp.dot(dye, stb, preferred_element_type=F32)
            dstate[sl, :] = jnp.where(top, ecl0, ecl1) * ds + lax.dot_general(dye, cm, _TN, preferred_element_type=F32)
            dxd = lax.dot_general(bm, dsb, _NT, preferred_element_type=F32) * wend
            sst = ds * st
            dyo = dyp * yoff
            mts = []
            for r, cc, w, ecl, keep, keep_rows in ((2 * p, cc0, w0, ecl0, lo, top), (2 * p + 1, cc1, w1, ecl1, ~lo, ~top)):
                cr = cumt_ref[0, r:r + 1, :]
                decay = _decay_pair(cc, cr, False)
                decay_t = _decay_pair(cc, cr, True)
                m = cb * decay
                mt = cbt * decay_t
                dyr = jnp.where(keep, dyp, 0.0).astype(BF16)
                g = lax.dot_general(dyr, xdb, _NT, preferred_element_type=F32)
                gt = lax.dot_general(xdb, dyr, _NT, preferred_element_type=F32)
                q = g * m
                qt = gt * mt
                dcb = dcb + g * decay
                dcbt = dcbt + gt * decay_t
                mts.append(jnp.dot(mt.astype(BF16), dyr, preferred_element_type=F32))
                t = jnp.dot(jnp.where(keep, xdt, 0.0).astype(BF16), dsb, preferred_element_type=F32)
                dbs = dbs + t * w
                tb = t * bf * w
                end_row = _colsum(tb) + ecl * _colsum(jnp.where(keep_rows, sst, 0.0))
                z = (q[:, 0:LANE] + q[:, LANE:2 * LANE]) - (qt[:, 0:LANE] + qt[:, LANE:2 * LANE])
                z = z + jnp.where(keep, dyo, 0.0) - tb + jnp.where(last, end_row, 0.0)
                zs.append(z)
            dxd = dxd + mts[0] + mts[1]
            dx_ref[:, sl] = dxd * dtp + skip_ref[:, sl]
            zd = dxd * xp
            zds.append(jnp.where(lo, zd, 0.0))
            zds.append(jnp.where(lo, 0.0, zd))
        dc_ref[...] = dcs + jnp.dot(dcb.astype(BF16), bm, preferred_element_type=F32)
        db_ref[...] = dbs + jnp.dot(dcbt.astype(BF16), cm, preferred_element_type=F32)
        dcum_ref[0] = _gather_rows(jnp.concatenate(zs, axis=1), e)
        ddt_ref[0] = _gather_rows(jnp.concatenate(zds, axis=1), e)

    return pl.pallas_call(
        body, name="ssd_bwd",
        out_shape=(jax.ShapeDtypeStruct((S, D_INNER), F32),
                   jax.ShapeDtypeStruct((S, SSM_GROUPS * D_STATE), F32),
                   jax.ShapeDtypeStruct((S, SSM_GROUPS * D_STATE), F32),
                   jax.ShapeDtypeStruct((SSM_GROUPS, S, LANE), F32),
                   jax.ShapeDtypeStruct((SSM_GROUPS, S, LANE), F32)),
        grid=(SSM_GROUPS, nc),
        in_specs=[pl.BlockSpec((L, GW), lambda g, c: (rev(c), g)),
                  pl.BlockSpec((L, D_STATE), lambda g, c: (rev(c), boff + g)),
                  pl.BlockSpec((L, D_STATE), lambda g, c: (rev(c), boff + SSM_GROUPS + g)),
                  pl.BlockSpec((L, LANE), lambda g, c: (rev(c), 0)),
                  pl.BlockSpec((L, LANE), lambda g, c: (rev(c), 0)),
                  pl.BlockSpec((1, HEADS_PER_GROUP, L), lambda g, c: (g, 0, rev(c))),
                  pl.BlockSpec((1, LANE, SPREAD_W), lambda g, c: (g, 0, 0)),
                  pl.BlockSpec((1, 1, GW, D_STATE), lambda g, c: (g, rev(c), 0, 0)),
                  pl.BlockSpec((L, GW), lambda g, c: (rev(c), g)),
                  pl.BlockSpec((L, GW), lambda g, c: (rev(c), g))],
        out_specs=(pl.BlockSpec((L, GW), lambda g, c: (rev(c), g)),
                   pl.BlockSpec((L, D_STATE), lambda g, c: (rev(c), g)),
                   pl.BlockSpec((L, D_STATE), lambda g, c: (rev(c), g)),
                   pl.BlockSpec((1, L, LANE), lambda g, c: (g, rev(c), 0)),
                   pl.BlockSpec((1, L, LANE), lambda g, c: (g, rev(c), 0))),
        scratch_shapes=[pltpu.VMEM((GW, D_STATE), F32)],
        compiler_params=_params(("parallel", "arbitrary")),
    )(xbc_c, xbc_c, xbc_c, dt, cum, cumt_g, spread, states, dy, dx_skip)


def _dt_bwd(dt_raw, dt_bias, a_log, ddt_x, dcum):
    S = dt_raw.shape[0]
    n = S // CHUNK

    def body(raw_ref, ddx_ref, dcu_ref, bias_ref, alog_ref, draw_ref, gb_ref, ga_ref):
        i = pl.program_id(0)

        @pl.when(i == 0)
        def _():
            gb_ref[...] = jnp.zeros_like(gb_ref)
            ga_ref[...] = jnp.zeros_like(ga_ref)

        ddx, dcu = ddx_ref[0], dcu_ref[0]
        for g in range(1, SSM_GROUPS):
            ddx = ddx + ddx_ref[g]
            dcu = dcu + dcu_ref[g]
        xx = raw_ref[...] + bias_ref[...]
        dt = _softplus(xx)
        a = -jnp.exp(alog_ref[...])
        dda = _revcumsum_rows(dcu)
        lane = lax.broadcasted_iota(jnp.int32, xx.shape, 1)
        draw = jnp.where(lane < N_HEADS_SSM, (ddx + dda * a) * _sigmoid(xx), 0.0)
        draw_ref[...] = draw.astype(draw_ref.dtype)
        gb_ref[...] += _colsum(draw)
        ga_ref[...] += _colsum(dda * dt) * a

    row = pl.BlockSpec((CHUNK, LANE), lambda i: (i, 0))
    grp = pl.BlockSpec((SSM_GROUPS, CHUNK, LANE), lambda i: (0, i, 0))
    one = pl.BlockSpec((1, LANE), lambda i: (0, 0))
    return pl.pallas_call(
        body, name="dt_bwd",
        out_shape=(jax.ShapeDtypeStruct((S, LANE), BF16), jax.ShapeDtypeStruct((1, LANE), F32), jax.ShapeDtypeStruct((1, LANE), F32)),
        grid=(n,),
        in_specs=[row, grp, grp, one, one],
        out_specs=(row, one, one),
        compiler_params=_params(("arbitrary",)),
    )(dt_raw, ddt_x, dcum, dt_bias, a_log)


def _adamw(w, g, m, v, name):
    shape = w.shape
    cols = shape[-1]
    rows = int(np.prod(shape[:-1]))
    w2, g2, m2, v2 = (t.reshape(rows, cols) for t in (w, g, m, v))
    tr = rows if rows * cols <= 512 * 1024 else _tile(rows, max(SUBLANE, (512 * 1024 // cols) // SUBLANE * SUBLANE), SUBLANE)
    c1 = 1.0 - ADAM_B1 ** ADAM_STEP
    c2 = 1.0 - ADAM_B2 ** ADAM_STEP

    def body(w_ref, g_ref, m_ref, v_ref, d_ref, mo_ref, vo_ref):
        gv = g_ref[...]
        mn = ADAM_B1 * m_ref[...] + (1.0 - ADAM_B1) * gv
        vn = ADAM_B2 * v_ref[...] + (1.0 - ADAM_B2) * (gv * gv)
        d_ref[...] = -ADAM_LR * ((mn / c1) / (jnp.sqrt(vn / c2) + ADAM_EPS) + ADAM_WD * w_ref[...])
        mo_ref[...] = mn
        vo_ref[...] = vn

    spec = pl.BlockSpec((tr, cols), lambda i: (i, 0))
    outs = pl.pallas_call(
        body, name=name,
        out_shape=tuple(jax.ShapeDtypeStruct((rows, cols), F32) for _ in range(3)),
        grid=(rows // tr,),
        in_specs=[spec] * 4, out_specs=(spec,) * 3,
        compiler_params=_params(("parallel",)),
    )(w2, g2, m2, v2)
    return tuple(o.reshape(shape) for o in outs)


def _prep_weights(w_in, w_uq):
    offs = np.cumsum((0,) + IN_SPLITS)
    pad = lambda t: jnp.pad(t, ((0, 0), (0, LANE - t.shape[1])))
    pieces = dict(
        qkv=w_in[:, offs[0]:offs[2]],
        kr=pad(w_in[:, offs[2]:offs[3]]),
        z=w_in[:, offs[3]:offs[4]],
        xbc=w_in[:, offs[4]:offs[5]],
        dt=pad(w_in[:, offs[5]:offs[6]]),
        g=w_in[:, offs[6]:offs[8]],
    )
    uq = w_uq.reshape(Q_LORA, N_HEADS_MLA, QK_DIM)
    uq = jnp.pad(uq, ((0, 0), (0, 0), (0, HEAD_PAD - QK_DIM))).reshape(Q_LORA, N_HEADS_MLA * HEAD_PAD)
    return pieces, uq


def _local_step(x, p, positions, ex, sp, target):
    W = gw = ex
    S = x.shape[0]
    tile = min(ATTN_TILE, S)
    pos_col = positions.reshape(S, 1)
    invf = ROPE_THETA ** (-jnp.arange(0, QK_ROPE, 2, dtype=F32) / QK_ROPE)
    invf = jnp.pad(jnp.concatenate([invf, invf]), (0, LANE - QK_ROPE)).reshape(1, LANE)
    wp, w_uq_p = _prep_weights(W["w_in"], W["w_uq"])
    padl = lambda t: jnp.pad(t, ((0, 0), (0, LANE - t.shape[1])))
    dt_bias_p, a_log_p = padl(sp["dt_bias"]), padl(sp["a_log"])
    dskip_ch = jnp.repeat(sp["d_skip"], SSM_HEADDIM, axis=1)
    p_bf = p.astype(BF16)
    RW = 256

    (u_bf,) = _rows(lambda i, n, x, g: (x * _rstd(x) * g,), S, RW, [_row(x), _full(sp["mix_norm_pre"])],
                    [("row", D_MODEL, BF16)], "norm_pre")
    cqkv = ex.mm(u_bf, wp["qkv"], name="mm_qkv")
    z = ex.mm(u_bf, wp["z"], name="mm_z")
    xbc = ex.mm(u_bf, wp["xbc"], name="mm_xbc")
    gates = ex.mm(u_bf, wp["g"], name="mm_gates")
    kr_pad = ex.mm(u_bf, wp["kr"], name="mm_kr")
    dt_raw = ex.mm(u_bf, wp["dt"], name="mm_dt")

    def qkv_norm(i, n, cq, ckv, gq, gkv):
        return cq * _rstd(cq) * gq, ckv * _rstd(ckv) * gkv

    cqn, ckvn = _rows(qkv_norm, S, 512, [_row(cqkv, Q_LORA, 0), _row(cqkv, KV_LORA, 1), _full(sp["q_norm"]), _full(sp["kv_norm"])],
                      [("row", Q_LORA, BF16), ("row", KV_LORA, BF16)], "qkv_norm")
    q_raw = ex.mm(cqn, w_uq_p, name="mm_uq")
    kv = ex.mm(ckvn, W["w_ukv"], out_dtype=BF16, name="mm_ukv")
    q_bf, kp_bf = _rope_fwd(q_raw, kr_pad, pos_col, invf)
    attn, lse = _attn_fwd(q_bf, kv, kp_bf, tile)

    xbc_c = _conv_fwd(xbc, sp["conv_w"], sp["conv_b"])
    dt, cum, cumt = _dt_prep(dt_raw, dt_bias_p, a_log_p)
    cumt_g = cumt[:N_HEADS_SSM].reshape(SSM_GROUPS, HEADS_PER_GROUP, S)
    spread = _spread_matrix()
    y, states = _ssd_fwd(xbc_c, dt, cum, cumt_g, spread)

    GN = D_INNER // SSM_GROUPS

    def gated(y, xs, z, dsk):
        yt = y + dsk * xs
        sz = _sigmoid(z)
        return yt, sz, yt * (z * sz)

    def gated_norm(i, n, y, xs, z, dsk, gn):
        _, _, yg = gated(y, xs, z, dsk)
        parts = []
        for g in range(SSM_GROUPS):
            blk = yg[:, g * GN:(g + 1) * GN]
            parts.append(blk * _rstd(blk) * gn[:, g * GN:(g + 1) * GN])
        return (jnp.concatenate(parts, axis=1),)

    (ssm,) = _rows(gated_norm, S, 128, [_row(y), _row(xbc_c, D_INNER, 0), _row(z), _full(dskip_ch), _full(sp["ssm_norm"])],
                   [("row", D_INNER, BF16)], "gated_norm")

    a_o = ex.mm(attn, W["w_attn_o"], name="mm_attn_o")
    b_o = ex.mm(ssm, W["w_ssm_o"], name="mm_ssm_o")

    def mix(i, n, ga, gs, a, b):
        return (_sigmoid(ga) * a + _sigmoid(gs) * b,)

    (mixed,) = _rows(mix, S, RW, [_row(gates, D_MODEL, 0), _row(gates, D_MODEL, 1), _row(a_o), _row(b_o)],
                     [("row", D_MODEL, BF16)], "mix")
    m2 = ex.mm(mixed, W["w_out"], name="mm_out")

    def post(i, n, h, m, gpost, gpre):
        hn = h + m * _rstd(m) * gpost
        return hn, hn * _rstd(hn) * gpre

    h1, f_bf = _rows(post, S, RW, [_row(x), _row(m2), _full(sp["mix_norm_post"]), _full(sp["ffn_norm_pre"])],
                     [("row", D_MODEL, F32), ("row", D_MODEL, BF16)], "post_mix")
    ga = ex.mm(f_bf, W["w_gate"], name="mm_gate")
    up = ex.mm(f_bf, W["w_up"], name="mm_up")
    (s_bf,) = _rows(lambda i, n, a, b: (a * _sigmoid(a) * b,), S, RW, [_row(ga), _row(up)], [("row", D_FF, BF16)], "swiglu")
    f2 = ex.mm(s_bf, W["w_down"], name="mm_down")
    h2, n3_bf = _rows(post, S, RW, [_row(h1), _row(f2), _full(sp["ffn_norm_post"]), _full(sp["ple_norm_pre"])],
                      [("row", D_MODEL, F32), ("row", D_MODEL, BF16)], "post_ffn")
    gpre = ex.mm(n3_bf, W["w_ple_gate"], name="mm_ple_gate")
    pe = ex.mm(p_bf, W["w_ple"], name="mm_ple")

    def ple_loss(i, n, h2, gpre, pe, tgt, gpost):
        gate = _sigmoid(gpre)
        e = pe * gate
        r = _rstd(e)
        diff = h2 + e * r * gpost - tgt
        loss = 0.5 * jnp.sum(jnp.mean(diff * diff, axis=1, keepdims=True))
        dh3 = diff * (1.0 / D_MODEL)
        de, dg_rows = _norm_bwd(e, r, gpost, dh3)
        return (jnp.full((1, LANE), loss, F32), dh3, de * gate, de * pe * gate * (1.0 - gate), _colsum(dg_rows))

    loss, dh3, dpe, dgpre, g_ple_post = _rows(
        ple_loss, S, 128, [_row(h2), _row(gpre), _row(pe), _row(target), _full(sp["ple_norm_post"])],
        [("acc", (1, LANE), F32), ("row", D_MODEL, F32), ("row", D_MODEL, BF16), ("row", D_MODEL, BF16),
         ("acc", (1, D_MODEL), F32)], "ple_loss")

    gs = {"ple_norm_post": g_ple_post}
    gw["w_ple"] = ex.mm(p_bf, dpe, ta=True, name="mmg_ple")
    gw["w_ple_gate"] = ex.mm(n3_bf, dgpre, ta=True, name="mmg_ple_gate")
    dn3 = ex.mm(dgpre, W["w_ple_gate"], tb=True, name="mmb_ple_gate")

    def post_bwd(i, n, h, m, dhn, dn, gpost, gpre):
        rm = _rstd(m)
        hn = h + m * rm * gpost
        dx, dgpre_rows = _norm_bwd(hn, _rstd(hn), gpre, dn)
        dhn_t = dhn + dx
        dm, dgpost_rows = _norm_bwd(m, rm, gpost, dhn_t)
        return dhn_t, dm, _colsum(dgpre_rows), _colsum(dgpost_rows)

    def run_post_bwd(h, m, dhn, dn, gpost, gpre, name):
        return _rows(post_bwd, S, 128, [_row(h), _row(m), _row(dhn), _row(dn), _full(gpost), _full(gpre)],
                     [("row", D_MODEL, F32), ("row", D_MODEL, BF16), ("acc", (1, D_MODEL), F32), ("acc", (1, D_MODEL), F32)], name)

    dh2, df2, gs["ple_norm_pre"], gs["ffn_norm_post"] = run_post_bwd(
        h1, f2, dh3, dn3, sp["ffn_norm_post"], sp["ple_norm_pre"], "post_ffn_bwd")
    gw["w_down"] = ex.mm(s_bf, df2, ta=True, name="mmg_down")
    ds = ex.mm(df2, W["w_down"], tb=True, name="mmb_down")

    def swiglu_bwd(i, n, a, b, ds):
        sa = _sigmoid(a)
        return ds * b * (sa * (1.0 + a * (1.0 - sa))), ds * (a * sa)

    dga, dup = _rows(swiglu_bwd, S, RW, [_row(ga), _row(up), _row(ds)], [("row", D_FF, BF16), ("row", D_FF, BF16)], "swiglu_bwd")
    gw["w_gate"] = ex.mm(f_bf, dga, ta=True, name="mmg_gate")
    gw["w_up"] = ex.mm(f_bf, dup, ta=True, name="mmg_up")
    df = ex.mm(dga, W["w_gate"], tb=True, name="mmb_gate")
    df = ex.mm(dup, W["w_up"], tb=True, add=df, name="mmb_up")
    dh1, dm2, gs["ffn_norm_pre"], gs["mix_norm_post"] = run_post_bwd(
        x, m2, dh2, df, sp["mix_norm_post"], sp["ffn_norm_pre"], "post_mix_bwd")
    gw["w_out"] = ex.mm(mixed, dm2, ta=True, name="mmg_out")
    dmixed = ex.mm(dm2, W["w_out"], tb=True, name="mmb_out")

    def mix_bwd(i, n, ga, gs_, a, b, dm):
        sa, ss = _sigmoid(ga), _sigmoid(gs_)
        return dm * sa, dm * ss, jnp.concatenate([dm * a * sa * (1.0 - sa), dm * b * ss * (1.0 - ss)], axis=1)

    da_o, db_o, dgates = _rows(mix_bwd, S, RW, [_row(gates, D_MODEL, 0), _row(gates, D_MODEL, 1), _row(a_o), _row(b_o), _row(dmixed)],
                               [("row", D_MODEL, BF16), ("row", D_MODEL, BF16), ("row", 2 * D_MODEL, BF16)], "mix_bwd")
    gw["w_attn_o"] = ex.mm(attn, da_o, ta=True, name="mmg_attn_o")
    dattn = ex.mm(da_o, W["w_attn_o"], tb=True, name="mmb_attn_o")
    gw["w_ssm_o"] = ex.mm(ssm, db_o, ta=True, name="mmg_ssm_o")
    dssm = ex.mm(db_o, W["w_ssm_o"], tb=True, name="mmb_ssm_o")

    delta, dattn_bf = _attn_delta(attn, dattn, tile)
    dq, dkv, dkp = _attn_bwd(q_bf, kv, kp_bf, dattn_bf, lse, delta, tile)
    dq_raw, dkr = _rope_bwd(dq, dkp, pos_col, invf)
    g_uq_p = ex.mm(cqn, dq_raw, ta=True, name="mmg_uq")
    gw["w_uq"] = g_uq_p.reshape(Q_LORA, N_HEADS_MLA, HEAD_PAD)[:, :, :QK_DIM].reshape(Q_LORA, N_HEADS_MLA * QK_DIM)
    dcqn = ex.mm(dq_raw, w_uq_p, tb=True, name="mmb_uq")
    gw["w_ukv"] = ex.mm(ckvn, dkv, ta=True, name="mmg_ukv")
    dckvn = ex.mm(dkv, W["w_ukv"], tb=True, name="mmb_ukv")

    def qkv_norm_bwd(i, n, cq, ckv, dq_, dkv_, gq, gkv):
        dcq, gq_rows = _norm_bwd(cq, _rstd(cq), gq, dq_)
        dckv, gkv_rows = _norm_bwd(ckv, _rstd(ckv), gkv, dkv_)
        return jnp.concatenate([dcq, dckv], axis=1), _colsum(gq_rows), _colsum(gkv_rows)

    dcqkv, gs["q_norm"], gs["kv_norm"] = _rows(
        qkv_norm_bwd, S, 512, [_row(cqkv, Q_LORA, 0), _row(cqkv, KV_LORA, 1), _row(dcqn), _row(dckvn), _full(sp["q_norm"]), _full(sp["kv_norm"])],
        [("row", Q_LORA + KV_LORA, BF16), ("acc", (1, Q_LORA), F32), ("acc", (1, KV_LORA), F32)], "qkv_norm_bwd")

    def gated_norm_bwd(i, n, y, xs, z, dssm, dsk, gn):
        yt, sz, yg = gated(y, xs, z, dsk)
        dyg_parts, gn_parts = [], []
        for g in range(SSM_GROUPS):
            sl = slice(g * GN, (g + 1) * GN)
            blk = yg[:, sl]
            dblk, rows = _norm_bwd(blk, _rstd(blk), gn[:, sl], dssm[:, sl])
            dyg_parts.append(dblk)
            gn_parts.append(_colsum(rows))
        dyg = jnp.concatenate(dyg_parts, axis=1)
        dyt = dyg * (z * sz)
        dz = dyg * yt * (sz * (1.0 + z * (1.0 - sz)))
        return dyt, dz, dyt * dsk, jnp.concatenate(gn_parts, axis=1), _colsum(dyt * xs)

    dy, dz, dx_skip, gs["ssm_norm"], g_dskip_ch = _rows(
        gated_norm_bwd, S, 128, [_row(y), _row(xbc_c, D_INNER, 0), _row(z), _row(dssm), _full(dskip_ch), _full(sp["ssm_norm"])],
        [("row", D_INNER, F32), ("row", D_INNER, BF16), ("row", D_INNER, F32), ("acc", (1, D_INNER), F32), ("acc", (1, D_INNER), F32)],
        "gated_norm_bwd")
    gs["d_skip"] = jnp.sum(g_dskip_ch.reshape(N_HEADS_SSM, SSM_HEADDIM), axis=1).reshape(1, N_HEADS_SSM)
    dxs, dbm, dcm, ddt_x, dcum = _ssd_bwd(xbc_c, dt, cum, cumt_g, spread, states, dy, dx_skip)
    ddt_raw, g_dtb, g_alog = _dt_bwd(dt_raw, dt_bias_p, a_log_p, ddt_x, dcum)
    gs["dt_bias"] = g_dtb[:, :N_HEADS_SSM]
    gs["a_log"] = g_alog[:, :N_HEADS_SSM]
    dxbc, g_conv_w8, gs["conv_b"] = _conv_bwd(xbc, [dxs, dbm, dcm], sp["conv_w"], sp["conv_b"])
    gs["conv_w"] = g_conv_w8[:CONV_WIDTH]

    g_qkv = ex.mm(u_bf, dcqkv, ta=True, name="mmg_qkv")
    g_kr = ex.mm(u_bf, dkr, ta=True, name="mmg_kr")
    g_z = ex.mm(u_bf, dz, ta=True, name="mmg_z")
    g_xbc = ex.mm(u_bf, dxbc, ta=True, name="mmg_xbc")
    g_dt = ex.mm(u_bf, ddt_raw, ta=True, name="mmg_dt")
    g_g = ex.mm(u_bf, dgates, ta=True, name="mmg_gates")
    gw["w_in"] = jnp.concatenate([g_qkv, g_kr[:, :QK_ROPE], g_z, g_xbc, g_dt[:, :N_HEADS_SSM], g_g], axis=1)
    du = ex.mm(dcqkv, wp["qkv"], tb=True, name="mmb_qkv")
    du = ex.mm(dkr, wp["kr"], tb=True, add=du, name="mmb_kr")
    du = ex.mm(ddt_raw, wp["dt"], tb=True, add=du, name="mmb_dt")
    du = ex.mm(dz, wp["z"], tb=True, add=du, name="mmb_z")
    du = ex.mm(dxbc, wp["xbc"], tb=True, add=du, name="mmb_xbc")
    du = ex.mm(dgates, wp["g"], tb=True, add=du, name="mmb_gates")

    def pre_bwd(i, n, x, du, dh, g):
        dx, rows = _norm_bwd(x, _rstd(x), g, du)
        return dh + dx, _colsum(rows)

    grad_x, gs["mix_norm_pre"] = _rows(pre_bwd, S, RW, [_row(x), _row(du), _row(dh1), _full(sp["mix_norm_pre"])],
                                       [("row", D_MODEL, F32), ("acc", (1, D_MODEL), F32)], "norm_pre_bwd")
    return loss, grad_x, gs


BIG = (
    ("w_in", (2048, 3872), 1), ("w_uq", (512, 768), 1), ("w_ukv", (512, 1024), 1), ("w_attn_o", (512, 2048), 0),
    ("w_ssm_o", (1024, 2048), 0), ("w_out", (512, 2048), 0), ("w_gate", (2048, 1408), 1), ("w_up", (2048, 1408), 1),
    ("w_down", (1408, 2048), 0), ("w_ple_gate", (512, 2048), 0), ("w_ple", (256, 512), 1),
)
SMALL = (
    ("mix_norm_pre", 2048), ("mix_norm_post", 2048), ("q_norm", 512), ("kv_norm", 512), ("conv_b", 6144), ("dt_bias", 64),
    ("a_log", 64), ("d_skip", 64), ("ssm_norm", 4096), ("ffn_norm_pre", 2048), ("ffn_norm_post", 2048),
    ("ple_norm_pre", 2048), ("ple_norm_post", 2048),
)
CONV_W_LEN = CONV_WIDTH * CONV_DIM
SMALL_ROWS = 384


def _place():
    return lax.axis_index("x"), lax.axis_index("y"), lax.axis_index("c")


def _flip(v, bit):
    return 1 - v if bit else v


def _alone(hook, name):
    n_in, n_out = len(hook.ins), len(hook.out_shapes)

    def body(*refs):
        start, finish = hook.make(refs[:n_in], refs[n_in:n_in + n_out], refs[n_in + n_out:])
        start()
        finish()

    return list(pl.pallas_call(
        body, name=name, out_shape=tuple(hook.out_shapes),
        in_specs=[pl.BlockSpec(memory_space=pl.ANY)] * n_in,
        out_specs=tuple(pl.BlockSpec(memory_space=pl.ANY) for _ in range(n_out)),
        scratch_shapes=[pltpu.SemaphoreType.DMA((s,)) for s in hook.sems],
        input_output_aliases=hook.aliases,
    )(*hook.ins))


def _simple(copies):
    def start():
        for cp in copies:
            cp.start()

    def finish():
        for cp in copies:
            cp.wait()

    return start, finish


def _gather_hook(shards):
    n = len(shards)

    def make(ins, outs, sems):
        send_sems, recv_sems, fwd_send_sems, fwd_recv_sems = sems
        x, y, c = _place()
        me = 2 * x + y
        far, near = [], []
        for a in range(n):
            half = shards[a].shape[0] // 2
            lo = pl.multiple_of(c * half, SUBLANE)
            for k in (1, 2, 3):
                px, py = _flip(x, k >> 1), _flip(y, k & 1)
                far.append(pltpu.make_async_remote_copy(
                    src_ref=ins[a].at[pl.ds(lo, half), :], dst_ref=outs[a].at[me, pl.ds(lo, half), :],
                    send_sem=send_sems.at[3 * a + k - 1], recv_sem=recv_sems.at[3 * a + k - 1],
                    device_id=(px, py, c), device_id_type=MESH_ID))
                got = outs[a].at[2 * px + py, pl.ds(lo, half), :]
                near.append(pltpu.make_async_remote_copy(
                    src_ref=got, dst_ref=got, send_sem=fwd_send_sems.at[3 * a + k - 1], recv_sem=fwd_recv_sems.at[3 * a + k - 1],
                    device_id=(x, y, 1 - c), device_id_type=MESH_ID))

        def start():
            for cp in far:
                cp.start()

        def finish():
            for cp, fwd in zip(far, near):
                cp.wait_recv()
                fwd.start()
            for cp, fwd in zip(far, near):
                cp.wait_send()
                fwd.wait()

        return start, finish

    return _Hook(shards, [jax.ShapeDtypeStruct((N_CHIPS, *s.shape), s.dtype) for s in shards], (3 * n,) * 4, make)


def _swap_hook(gs):
    n = len(gs)

    def make(ins, outs, sems):
        send_sems, recv_sems = sems
        x, y, c = _place()
        copies = []
        for a in range(n):
            half = gs[a].shape[1] // 2
            src = ins[a].at[:, pl.ds(pl.multiple_of((1 - c) * half, SUBLANE), half), :]
            copies.append(pltpu.make_async_remote_copy(
                src_ref=src, dst_ref=outs[a], send_sem=send_sems.at[a], recv_sem=recv_sems.at[a],
                device_id=(x, y, 1 - c), device_id_type=MESH_ID))
        return _simple(copies)

    return _Hook(gs, [jax.ShapeDtypeStruct((g.shape[0], g.shape[1] // 2, g.shape[2]), g.dtype) for g in gs], (n, n), make)


def _sum_rows_tile(rows, cols):
    return _tile(rows, max(2 * SUBLANE, (512 * 1024 // cols) // (2 * SUBLANE) * (2 * SUBLANE)), 2 * SUBLANE)


def _add_half(g, other, c, name):
    n, R, C = g.shape
    half = R // 2
    tr = _sum_rows_tile(half, C)
    nb = half // tr

    def body(c_ref, g_ref, o_ref, out_ref):
        out_ref[...] = (g_ref[...] + o_ref[...]).astype(out_ref.dtype)

    return pl.pallas_call(
        body, name=name,
        out_shape=jax.ShapeDtypeStruct((n, half, C), BF16),
        grid_spec=pltpu.PrefetchScalarGridSpec(
            num_scalar_prefetch=1, grid=(n, nb),
            in_specs=[pl.BlockSpec((1, tr, C), lambda j, i, c_ref: (j, c_ref[0] * nb + i, 0)),
                      pl.BlockSpec((1, tr, C), lambda j, i, c_ref: (j, i, 0))],
            out_specs=pl.BlockSpec((1, tr, C), lambda j, i, c_ref: (j, i, 0))),
        compiler_params=_params(("parallel", "parallel")),
    )(c, g, other)


def _scatter_hook(parts):
    n = len(parts)

    def make(ins, outs, sems):
        send_sems, recv_sems = sems
        x, y, c = _place()
        copies = []
        for a in range(n):
            for k in (1, 2, 3):
                px, py = _flip(x, k >> 1), _flip(y, k & 1)
                copies.append(pltpu.make_async_remote_copy(
                    src_ref=ins[a].at[2 * px + py], dst_ref=outs[a].at[k - 1], send_sem=send_sems.at[3 * a + k - 1],
                    recv_sem=recv_sems.at[3 * a + k - 1], device_id=(px, py, c), device_id_type=MESH_ID))
        return _simple(copies)

    return _Hook(parts, [jax.ShapeDtypeStruct((3, *p.shape[1:]), p.dtype) for p in parts], (3 * n, 3 * n), make)


def _add_chips(part, got, place, name):
    n, R, C = part.shape
    tr = _sum_rows_tile(R, C)
    nb = R // tr

    def body(place_ref, p_ref, g_ref, out_ref):
        out_ref[...] = ((p_ref[0].astype(F32) + g_ref[0].astype(F32)) + g_ref[1].astype(F32)) + g_ref[2].astype(F32)

    return pl.pallas_call(
        body, name=name,
        out_shape=jax.ShapeDtypeStruct((2 * R, C), F32),
        grid_spec=pltpu.PrefetchScalarGridSpec(
            num_scalar_prefetch=1, grid=(nb,),
            in_specs=[pl.BlockSpec((1, tr, C), lambda i, place_ref: (place_ref[0], i, 0)),
                      pl.BlockSpec((3, tr, C), lambda i, place_ref: (0, i, 0))],
            out_specs=pl.BlockSpec((tr, C), lambda i, place_ref: (place_ref[1] * nb + i, 0))),
        compiler_params=_params(("parallel",)),
    )(place, part, got)


def _join_hook(wholes):
    n = len(wholes)

    def make(ins, outs, sems):
        send_sems, recv_sems = sems
        x, y, c = _place()
        copies = []
        for a in range(n):
            half = wholes[a].shape[0] // 2
            rows = outs[a].at[pl.ds(pl.multiple_of(c * half, SUBLANE), half), :]
            copies.append(pltpu.make_async_remote_copy(
                src_ref=rows, dst_ref=rows, send_sem=send_sems.at[a], recv_sem=recv_sems.at[a],
                device_id=(x, y, 1 - c), device_id_type=MESH_ID))
        return _simple(copies)

    return _Hook(wholes, [jax.ShapeDtypeStruct(w.shape, w.dtype) for w in wholes], (n, n), make, aliases={a: a for a in range(n)})


def _allreduce_small(vec, name):
    R, C = vec.shape

    def body(v_ref, o_ref, buf, send_sems, recv_sems):
        x, y, c = _place()
        me = 4 * x + 2 * y + c
        buf[me] = v_ref[...]
        copies = []
        for k in range(1, N_DEV):
            peer = (_flip(x, (k >> 2) & 1), _flip(y, (k >> 1) & 1), _flip(c, k & 1))
            copies.append(pltpu.make_async_remote_copy(
                src_ref=v_ref, dst_ref=buf.at[me], send_sem=send_sems.at[k - 1], recv_sem=recv_sems.at[k - 1],
                device_id=peer, device_id_type=MESH_ID))
        for cp in copies:
            cp.start()
        for cp in copies:
            cp.wait()
        tot = buf[0]
        for d in range(1, N_DEV):
            tot = tot + buf[d]
        o_ref[...] = tot

    return pl.pallas_call(
        body, name=name,
        out_shape=jax.ShapeDtypeStruct((R, C), F32),
        in_specs=[pl.BlockSpec(memory_space=pltpu.VMEM)],
        out_specs=pl.BlockSpec(memory_space=pltpu.VMEM),
        scratch_shapes=[pltpu.VMEM((N_DEV, R, C), F32), pltpu.SemaphoreType.DMA((N_DEV - 1,)), pltpu.SemaphoreType.DMA((N_DEV - 1,))],
    )(vec)


def _unstack(gathered, shape, axis):
    if axis == 0:
        return gathered.reshape(N_CHIPS * shape[0], shape[1])
    return jnp.concatenate([gathered[j] for j in range(N_CHIPS)], axis=1)


def _stack(whole, shape, axis):
    if axis == 0:
        return whole.reshape(N_CHIPS, shape[0], shape[1])
    return jnp.stack([whole[:, j * shape[1]:(j + 1) * shape[1]] for j in range(N_CHIPS)])


GATHER_FIRST = ("w_in", "w_uq", "w_ukv")
GATHER_IN = {"mm_z": ("w_attn_o", "w_ssm_o", "w_out"), "mm_xbc": ("w_gate", "w_up"), "mm_gates": ("w_down", "w_ple_gate", "w_ple")}
REDUCE = (
    (("w_ple", "w_ple_gate", "w_down"), "mmb_down", "mmg_gate", "mmg_up"),
    (("w_gate", "w_up"), "mmb_gate", "mmb_up", "mmg_out"),
    (("w_out", "w_attn_o", "w_ssm_o"), "mmb_ssm_o", "mmg_z", "mmg_xbc"),
    (("w_uq", "w_ukv"), "mmb_ukv", "mmg_gates", "mmb_z"),
    (("w_in",), "mmb_z", "mmb_xbc", "mmb_gates"),
)


class _Exchange:
    def __init__(self, shards, chip, core):
        self.shards, self.chip = shards, chip
        self.core_arr = core.reshape(1).astype(jnp.int32)
        self.place_arr = jnp.stack([chip, core]).astype(jnp.int32)
        self.shape = {n: (shape, axis) for n, shape, axis in BIG}
        self.whole, self.grads, self.reduced, self.pending, self.tails = {}, {}, {}, {}, 0
        hook, done = self._gather(GATHER_FIRST)
        done(_alone(hook, "gather_first"))
        for host, names in GATHER_IN.items():
            self._arm(host, *self._gather(names))

    def _arm(self, host, hook, done):
        self.pending.setdefault(host, []).append((hook, done))

    def _gather(self, names):
        shards = [self.shards[n].astype(BF16) for n in names]

        def done(outs):
            for n, s, g in zip(names, shards, outs):
                self.whole[n] = _unstack(lax.dynamic_update_slice(g, s[None], (self.chip, 0, 0)), *self.shape[n])

        return _gather_hook(shards), done

    def __getitem__(self, name):
        return self.whole[name]

    def __setitem__(self, name, grad):
        self.grads[name] = grad
        for names, swap_host, scatter_host, join_host in REDUCE:
            if name in names and all(n in self.grads for n in names):
                self._reduce(names, swap_host, scatter_host, join_host)

    def _reduce(self, names, swap_host, scatter_host, join_host):
        stacked = [_stack(self.grads[n], *self.shape[n]) for n in names]

        def joined(outs):
            for n, r in zip(names, outs):
                self.reduced[n] = r.reshape(1, *self.shape[n][0])

        def swapped(outs):
            parts = [_add_half(g, o, self.core_arr, "add_half_" + n) for n, g, o in zip(names, stacked, outs)]

            def scattered(gots):
                wholes = [_add_chips(q, o, self.place_arr, "add_chips_" + n) for n, q, o in zip(names, parts, gots)]
                self._arm(join_host, _join_hook(wholes), joined)

            self._arm(scatter_host, _scatter_hook(parts), scattered)

        self._arm(swap_host, _swap_hook(stacked), swapped)

    def _run(self, todo, call):
        hook = _merge_hooks([h for h, _ in todo])
        result, outs = call(hook)
        off = 0
        for h, done in todo:
            done(outs[off:off + len(h.out_shapes)])
            off += len(h.out_shapes)
        return result

    def mm(self, a, b, *, name, **kw):
        todo = self.pending.pop(name, None)
        if not todo:
            return _mm(a, b, name=name, **kw)
        return self._run(todo, lambda hook: _mm(a, b, name=name, hook=hook, **kw))

    def finish(self):
        while self.pending:
            todo = self.pending.pop(next(iter(self.pending)))
            self.tails += 1
            self._run(todo, lambda hook: (None, _alone(hook, "exchange_tail_%d" % self.tails)))
        return self.reduced


def kernel(x, p, positions, mix_norm_pre, mix_norm_post, w_in, q_norm, w_uq, kv_norm, w_ukv, conv_w, conv_b, dt_bias, a_log, d_skip, ssm_norm, w_attn_o, w_ssm_o, w_out, ffn_norm_pre, ffn_norm_post, w_gate, w_up, w_down, ple_norm_pre, ple_norm_post, w_ple_gate, w_ple, loss_target, m_mix_norm_pre, m_mix_norm_post, m_w_in, m_q_norm, m_w_uq, m_kv_norm, m_w_ukv, m_conv_w, m_conv_b, m_dt_bias, m_a_log, m_d_skip, m_ssm_norm, m_w_attn_o, m_w_ssm_o, m_w_out, m_ffn_norm_pre, m_ffn_norm_post, m_w_gate, m_w_up, m_w_down, m_ple_norm_pre, m_ple_norm_post, m_w_ple_gate, m_w_ple, v_mix_norm_pre, v_mix_norm_post, v_w_in, v_q_norm, v_w_uq, v_kv_norm, v_w_ukv, v_conv_w, v_conv_b, v_dt_bias, v_a_log, v_d_skip, v_ssm_norm, v_w_attn_o, v_w_ssm_o, v_w_out, v_ffn_norm_pre, v_ffn_norm_post, v_w_gate, v_w_up, v_w_down, v_ple_norm_pre, v_ple_norm_post, v_w_ple_gate, v_w_ple):
    given = dict(locals())
    names = [n for n, _, _ in BIG] + [n for n, _ in SMALL] + ["conv_w"]
    order = ["mix_norm_pre", "mix_norm_post", "w_in", "q_norm", "w_uq", "kv_norm", "w_ukv", "conv_w", "conv_b", "dt_bias", "a_log",
             "d_skip", "ssm_norm", "w_attn_o", "w_ssm_o", "w_out", "ffn_norm_pre", "ffn_norm_post", "w_gate", "w_up", "w_down",
             "ple_norm_pre", "ple_norm_post", "w_ple_gate", "w_ple"]
    assert sorted(names) == sorted(order)
    cx, cy, cc = _place()
    chip = 2 * cx + cy
    conv_cols = CONV_DIM // N_CHIPS

    ex = _Exchange({n: given[n][0] for n, _, _ in BIG}, chip, cc)
    own = jnp.where(cc == 0, conv_w[0], 0.0)
    conv_vec = lax.dynamic_update_slice(jnp.zeros((CONV_WIDTH, CONV_DIM), F32), own, (0, chip * conv_cols))
    conv_full = _allreduce_small(conv_vec.reshape(CONV_W_LEN // LANE, LANE), "gather_conv_w").reshape(CONV_WIDTH, CONV_DIM)
    sp = {n: given[n] for n, _ in SMALL}
    sp["conv_w"] = conv_full

    loss_part, grad_x, gs = _local_step(x[0], p[0, 0], positions[0], ex, sp, loss_target[0])

    g_big = ex.finish()

    small_parts = [gs[n] for n, _ in SMALL] + [gs["conv_w"], loss_part[:, :1]]
    small_vec = jnp.concatenate([t.reshape(-1) for t in small_parts])
    small_vec = jnp.pad(small_vec, (0, SMALL_ROWS * LANE - small_vec.shape[0])).reshape(SMALL_ROWS, LANE)
    small_sum = _allreduce_small(small_vec, "allreduce_small").reshape(-1)
    g_small, off = {}, 0
    for n, length in SMALL:
        g_small[n] = small_sum[off:off + length].reshape(1, length)
        off += length
    g_conv = small_sum[off:off + CONV_W_LEN].reshape(CONV_WIDTH, CONV_DIM)
    g_small["conv_w"] = lax.dynamic_slice(g_conv, (0, chip * conv_cols), (CONV_WIDTH, conv_cols)).reshape(1, CONV_WIDTH, conv_cols)
    loss = small_sum[off + CONV_W_LEN]

    grads, deltas, new_m, new_v = [], [], [], []
    for n in order:
        g = g_big[n] if n in g_big else g_small[n]
        d, m_, v_ = _adamw(given[n], g, given["m_" + n], given["v_" + n], "adamw_" + n)
        grads.append(g)
        deltas.append(d)
        new_m.append(m_)
        new_v.append(v_)
    return (loss, grad_x.reshape(x.shape), *grads, *deltas, *new_m, *new_v)
```

```python
import numpy as np
import jax
import jax.numpy as jnp
from jax import lax
from jax.experimental import pallas as pl
from jax.experimental.pallas import tpu as pltpu

F32 = jnp.float32
BF16 = jnp.bfloat16

D_MODEL = 2048
N_HEADS_MLA = 16
Q_LORA = 512
KV_LORA = 512
QK_NOPE = 128
QK_ROPE = 64
V_DIM = 128
QK_DIM = QK_NOPE + QK_ROPE
ROPE_THETA = 10000.0
D_INNER = 4096
SSM_HEADDIM = 64
N_HEADS_SSM = 64
SSM_GROUPS = 8
HEADS_PER_GROUP = 8
D_STATE = 128
CONV_WIDTH = 4
CHUNK = 256
CONV_DIM = D_INNER + 2 * SSM_GROUPS * D_STATE
D_FF = 5632
PLE_DIM = 256
EPS = 1e-6
IN_SPLITS = (Q_LORA, KV_LORA, QK_ROPE, D_INNER, CONV_DIM, N_HEADS_SSM, D_MODEL, D_MODEL)

ADAM_LR = 0.001
ADAM_B1 = 0.9
ADAM_B2 = 0.999
ADAM_EPS = 1e-08
ADAM_WD = 0.01
ADAM_STEP = 10

LANE = 128
SUBLANE = 8
HEAD_PAD = 256
VMEM_LIMIT = 56 * 1024 * 1024
ATTN_TILE = 1024
NEG = -1e30

MESH_ID = pl.DeviceIdType.MESH
N_CHIPS = 4
N_DEV = 8


def _tile(n, pref, mult=LANE):
    if n <= pref:
        return n
    t = (pref // mult) * mult
    while t >= mult:
        if n % t == 0:
            return t
        t -= mult
    return n


def _params(sem, vmem=VMEM_LIMIT, **kw):
    return pltpu.CompilerParams(dimension_semantics=sem, vmem_limit_bytes=vmem, **kw)


class _Hook:
    def __init__(self, ins, out_shapes, sems, make, aliases=None):
        self.ins, self.out_shapes, self.sems, self.make, self.aliases = list(ins), list(out_shapes), tuple(sems), make, dict(aliases or {})


def _merge_hooks(hooks):
    hooks = [h for h in hooks if h is not None]
    if not hooks:
        return None
    ins, outs, sems, aliases, cuts = [], [], [], {}, []
    for h in hooks:
        cuts.append((len(ins), len(outs), len(sems)))
        aliases.update({len(ins) + i: len(outs) + o for i, o in h.aliases.items()})
        ins += h.ins
        outs += h.out_shapes
        sems += h.sems

    def make(in_refs, out_refs, sem_refs):
        pairs = []
        for h, (i0, o0, s0) in zip(hooks, cuts):
            pairs.append(h.make(in_refs[i0:i0 + len(h.ins)], out_refs[o0:o0 + len(h.out_shapes)], sem_refs[s0:s0 + len(h.sems)]))

        def start():
            for st, _ in pairs:
                st()

        def finish():
            for _, fin in pairs:
                fin()

        return start, finish

    return _Hook(ins, outs, sems, make, aliases)


def _mm(a, b, *, ta=False, tb=False, add=None, out_dtype=F32, name, tm=1024, tn=1536, tk=2048, hook=None):
    if ta:
        K, M = a.shape
    else:
        M, K = a.shape
    N = b.shape[0] if tb else b.shape[1]
    assert (b.shape[1] if tb else b.shape[0]) == K, (a.shape, b.shape, ta, tb)
    tm, tn, tk = _tile(M, tm), _tile(N, tn), _tile(K, tk)
    nk = K // tk
    dn = (((0 if ta else 1,), (1 if tb else 0,)), ((), ()))
    has_add = add is not None
    n_own = 3 if has_add else 2
    n_hin = len(hook.ins) if hook else 0
    n_hout = len(hook.out_shapes) if hook else 0
    grid = (M // tm, N // tn, nk)

    def body(*refs):
        a_ref, b_ref = refs[:2]
        c_ref = refs[2] if has_add else None
        o_ref = refs[n_own + n_hin]
        scratch = refs[n_own + n_hin + 1 + n_hout:]
        if hook:
            start, finish = hook.make(refs[n_own:n_own + n_hin], refs[n_own + n_hin + 1:n_own + n_hin + 1 + n_hout],
                                      scratch[len(scratch) - len(hook.sems):])
            ids = [pl.program_id(d) for d in range(3)]
            pl.when((ids[0] == 0) & (ids[1] == 0) & (ids[2] == 0))(start)
        prod = lax.dot_general(a_ref[...].astype(BF16), b_ref[...].astype(BF16), dn, preferred_element_type=F32)
        if nk == 1:
            o_ref[...] = ((c_ref[...] + prod) if has_add else prod).astype(out_dtype)
        else:
            acc = scratch[0]
            k = pl.program_id(2)

            @pl.when(k == 0)
            def _():
                acc[...] = (c_ref[...] + prod) if has_add else prod

            @pl.when(k > 0)
            def _():
                acc[...] += prod

            @pl.when(k == nk - 1)
            def _():
                o_ref[...] = acc[...].astype(out_dtype)
        if hook:
            pl.when((ids[0] == grid[0] - 1) & (ids[1] == grid[1] - 1) & (ids[2] == grid[2] - 1))(finish)

    a_spec = pl.BlockSpec((tk, tm), lambda i, j, k: (k, i)) if ta else pl.BlockSpec((tm, tk), lambda i, j, k: (i, k))
    b_spec = pl.BlockSpec((tn, tk), lambda i, j, k: (j, k)) if tb else pl.BlockSpec((tk, tn), lambda i, j, k: (k, j))
    in_specs = [a_spec, b_spec]
    args = [a, b]
    if has_add:
        in_specs.append(pl.BlockSpec((tm, tn), lambda i, j, k: (i, j)))
        args.append(add)
    hbm = pl.BlockSpec(memory_space=pl.ANY)
    scratch_shapes = [pltpu.VMEM((tm, tn), F32)] if nk > 1 else []
    out_shape = jax.ShapeDtypeStruct((M, N), out_dtype)
    out_spec = pl.BlockSpec((tm, tn), lambda i, j, k: (i, j))
    if not hook:
        return pl.pallas_call(
            body, name=name, out_shape=out_shape, grid=grid, in_specs=in_specs, out_specs=out_spec,
            scratch_shapes=scratch_shapes,
            compiler_params=_params(("parallel", "parallel", "arbitrary")),
        )(*args)
    outs = pl.pallas_call(
        body, name=name, out_shape=(out_shape, *hook.out_shapes), grid=grid,
        in_specs=in_specs + [hbm] * n_hin, out_specs=(out_spec, *[hbm] * n_hout),
        scratch_shapes=scratch_shapes + [pltpu.SemaphoreType.DMA((s,)) for s in hook.sems],
        input_output_aliases={n_own + i: 1 + o for i, o in hook.aliases.items()},
        compiler_params=_params(("arbitrary", "arbitrary", "arbitrary")),
    )(*args, *hook.ins)
    return outs[0], list(outs[1:])


def _row(arr, width=None, cblk=0):
    return ("row", arr, arr.shape[1] if width is None else width, cblk)


def _full(arr):
    return ("full", arr)


def _prev8(arr):
    return ("prev8", arr)


def _next8(arr):
    return ("next8", arr)


def _rows(fn, n_rows, tm, ins, outs, name):
    tm = min(tm, n_rows)
    assert n_rows % tm == 0 and tm % SUBLANE == 0
    n = n_rows // tm
    in_specs, args = [], []
    for spec in ins:
        kind, arr = spec[0], spec[1]
        if kind == "row":
            _, _, w, cb = spec
            in_specs.append(pl.BlockSpec((tm, w), lambda i, cb=cb: (i, cb)))
        elif kind == "full":
            in_specs.append(pl.BlockSpec(arr.shape, lambda i, nd=arr.ndim: (0,) * nd))
        elif kind == "prev8":
            in_specs.append(pl.BlockSpec((SUBLANE, arr.shape[1]),
                                         lambda i: (jnp.maximum(i * (tm // SUBLANE) - 1, 0), 0)))
        elif kind == "next8":
            last = n_rows // SUBLANE - 1
            in_specs.append(pl.BlockSpec((SUBLANE, arr.shape[1]),
                                         lambda i: (jnp.minimum((i + 1) * (tm // SUBLANE), last), 0)))
        else:
            raise ValueError(kind)
        args.append(arr)
    out_shapes, out_specs = [], []
    any_acc = False
    for spec in outs:
        if spec[0] == "row":
            _, w, dt = spec
            out_shapes.append(jax.ShapeDtypeStruct((n_rows, w), dt))
            out_specs.append(pl.BlockSpec((tm, w), lambda i: (i, 0)))
        else:
            _, shp, dt = spec
            any_acc = True
            out_shapes.append(jax.ShapeDtypeStruct(shp, dt))
            out_specs.append(pl.BlockSpec(shp, lambda i, nd=len(shp): (0,) * nd))
    nin = len(ins)

    def body(*refs):
        i = pl.program_id(0)
        vals = fn(i, n, *[r[...] for r in refs[:nin]])
        for o_ref, spec, v in zip(refs[nin:], outs, vals):
            if spec[0] == "acc":
                @pl.when(i == 0)
                def _(o_ref=o_ref):
                    o_ref[...] = jnp.zeros_like(o_ref)

                o_ref[...] += v.astype(o_ref.dtype)
            else:
                o_ref[...] = v.astype(o_ref.dtype)

    res = pl.pallas_call(
        body, name=name,
        out_shape=tuple(out_shapes),
        grid=(n,),
        in_specs=in_specs,
        out_specs=tuple(out_specs),
        compiler_params=_params(("arbitrary",) if any_acc else ("parallel",)),
    )(*args)
    return res


def _rstd(x):
    return lax.rsqrt(jnp.mean(x * x, axis=-1, keepdims=True) + EPS)


def _norm_bwd(x, r, g, dy):
    xh = x * r
    dyg = dy * g
    dx = r * (dyg - xh * jnp.mean(dyg * xh, axis=-1, keepdims=True))
    return dx, dy * xh


def _sigmoid(x):
    return 0.5 * jnp.tanh(0.5 * x) + 0.5


def _colsum(v):
    return jnp.sum(v, axis=0, keepdims=True)


def _rope_tables(pos, invf):
    ang = pos.astype(F32) * invf
    lane = lax.broadcasted_iota(jnp.int32, ang.shape, 1)
    cos, sin = jnp.cos(ang), jnp.sin(ang)
    ct = jnp.where(lane < QK_ROPE, cos, 0.0)
    sa = jnp.where(lane < QK_ROPE // 2, -sin, 0.0)
    sb = jnp.where((lane >= QK_ROPE // 2) & (lane < QK_ROPE), sin, 0.0)
    return ct, sa, sb


def _rope(b, ct, sa, sb):
    return ct * b + sa * pltpu.roll(b, LANE - QK_ROPE // 2, 1) + sb * pltpu.roll(b, QK_ROPE // 2, 1)


def _rope_t(d, ct, sa, sb):
    return ct * d + pltpu.roll(sa * d, QK_ROPE // 2, 1) + pltpu.roll(sb * d, LANE - QK_ROPE // 2, 1)


def _rope_fwd(q_raw, kr_pad, pos_col, invf):
    S = q_raw.shape[0]

    def fn(i, n, q, kr, pos, invf):
        ct, sa, sb = _rope_tables(pos, invf)
        parts = []
        for h in range(N_HEADS_MLA):
            parts.append(q[:, h * HEAD_PAD:h * HEAD_PAD + LANE])
            parts.append(_rope(q[:, h * HEAD_PAD + LANE:(h + 1) * HEAD_PAD], ct, sa, sb))
        return jnp.concatenate(parts, axis=1), _rope(kr, ct, sa, sb)

    return _rows(fn, S, 256, [_row(q_raw), _row(kr_pad), _row(pos_col), _full(invf)],
                 [("row", N_HEADS_MLA * HEAD_PAD, BF16), ("row", LANE, BF16)], "rope_fwd")


def _rope_bwd(dq, dkp, pos_col, invf):
    S = dq.shape[0]
    tm = 256

    def body(dq_ref, dkp_ref, pos_ref, invf_ref, dqo_ref, dkr_ref):
        ct, sa, sb = _rope_tables(pos_ref[...], invf_ref[...])
        for h in range(N_HEADS_MLA):
            dqo_ref[:, h * HEAD_PAD:h * HEAD_PAD + LANE] = dq_ref[:, h * HEAD_PAD:h * HEAD_PAD + LANE].astype(BF16)
            dqo_ref[:, h * HEAD_PAD + LANE:(h + 1) * HEAD_PAD] = _rope_t(
                dq_ref[:, h * HEAD_PAD + LANE:(h + 1) * HEAD_PAD], ct, sa, sb).astype(BF16)
        tot = dkp_ref[0]
        for h in range(1, N_HEADS_MLA):
            tot = tot + dkp_ref[h]
        dkr_ref[...] = _rope_t(tot, ct, sa, sb).astype(BF16)

    return pl.pallas_call(
        body, name="rope_bwd",
        out_shape=(jax.ShapeDtypeStruct(dq.shape, BF16), jax.ShapeDtypeStruct((S, LANE), BF16)),
        grid=(S // tm,),
        in_specs=[pl.BlockSpec((tm, dq.shape[1]), lambda i: (i, 0)),
                  pl.BlockSpec((N_HEADS_MLA, tm, LANE), lambda i: (0, i, 0)),
                  pl.BlockSpec((tm, 1), lambda i: (i, 0)),
                  pl.BlockSpec((1, LANE), lambda i: (0, 0))],
        out_specs=(pl.BlockSpec((tm, dq.shape[1]), lambda i: (i, 0)), pl.BlockSpec((tm, LANE), lambda i: (i, 0))),
        compiler_params=_params(("parallel",)),
    )(dq, dkp, pos_col, invf)


def _row_of(col, n):
    eye = lax.broadcasted_iota(jnp.int32, (LANE, LANE), 0) == lax.broadcasted_iota(jnp.int32, (LANE, LANE), 1)
    parts = [jnp.sum(jnp.where(eye, col[i:i + LANE], 0.0), axis=0, keepdims=True) for i in range(0, n, LANE)]
    return parts[0] if len(parts) == 1 else jnp.concatenate(parts, axis=1)


def _attn_fwd(q, kv, kp, tile):
    S = q.shape[0]
    nq = S // tile
    scale = QK_DIM ** -0.5
    nt = (((1,), (1,)), ((), ()))

    def body(q_ref, kv_ref, kp_ref, o_ref, lse_ref, m_s, l_s, acc_s, s_buf):
        qi = pl.program_id(1)
        qv = q_ref[...]
        m_s[...] = jnp.full_like(m_s, NEG)
        l_s[...] = jnp.zeros_like(l_s)
        acc_s[...] = jnp.zeros_like(acc_s)

        def scores(j):
            start = pl.multiple_of(j * tile, tile)
            k = jnp.concatenate([kv_ref[pl.ds(start, tile), 0:LANE], kp_ref[pl.ds(start, tile), :]], axis=1)
            return lax.dot_general(qv, k, nt, preferred_element_type=F32) * scale

        def update(s, j):
            v = kv_ref[pl.ds(pl.multiple_of(j * tile, tile), tile), LANE:2 * LANE]
            m_old = m_s[...]
            m_new = jnp.maximum(m_old, jnp.max(s, axis=1, keepdims=True))
            alpha = jnp.exp(m_old - m_new)
            p = jnp.exp(s - m_new)
            l_s[...] = alpha * l_s[...] + jnp.sum(p, axis=1, keepdims=True)
            acc_s[...] = alpha * acc_s[...] + jnp.dot(p.astype(BF16), v, preferred_element_type=F32)
            m_s[...] = m_new

        s_buf[0] = scores(0)

        def loop_body(j, carry):
            nxt = scores(j + 1)
            update(s_buf[lax.rem(j, 2)], j)
            s_buf[lax.rem(j + 1, 2)] = nxt
            return carry

        lax.fori_loop(0, qi, loop_body, 0)
        s = s_buf[lax.rem(qi, 2)]
        row = lax.broadcasted_iota(jnp.int32, s.shape, 0)
        col = lax.broadcasted_iota(jnp.int32, s.shape, 1)
        update(jnp.where(row >= col, s, NEG), qi)
        l = l_s[...]
        o_ref[...] = (acc_s[...] / l).astype(o_ref.dtype)
        lse_ref[0, 0] = _row_of(m_s[...] + jnp.log(l), tile)

    return pl.pallas_call(
        body, name="attn_fwd",
        out_shape=(jax.ShapeDtypeStruct((S, N_HEADS_MLA * V_DIM), BF16),
                   jax.ShapeDtypeStruct((N_HEADS_MLA, nq, 1, tile), F32)),
        grid=(N_HEADS_MLA, nq),
        in_specs=[pl.BlockSpec((tile, HEAD_PAD), lambda h, i: (i, h)),
                  pl.BlockSpec((S, HEAD_PAD), lambda h, i: (0, h)),
                  pl.BlockSpec((S, LANE), lambda h, i: (0, 0))],
        out_specs=(pl.BlockSpec((tile, V_DIM), lambda h, i: (i, h)),
                   pl.BlockSpec((1, 1, 1, tile), lambda h, i: (h, i, 0, 0))),
        scratch_shapes=[pltpu.VMEM((tile, 1), F32), pltpu.VMEM((tile, 1), F32), pltpu.VMEM((tile, V_DIM), F32),
                        pltpu.VMEM((2, tile, tile), F32)],
        compiler_params=_params(("parallel", "arbitrary")),
    )(q, kv, kp)


def _attn_delta(o, do, tile):
    S = o.shape[0]
    nq = S // tile

    def body(o_ref, do_ref, d_ref):
        prod = o_ref[...].astype(F32) * do_ref[...].astype(F32)
        for h in range(N_HEADS_MLA):
            col = jnp.sum(prod[:, h * V_DIM:(h + 1) * V_DIM], axis=1, keepdims=True)
            d_ref[h, 0] = _row_of(col, tile)

    return pl.pallas_call(
        body, name="attn_delta",
        out_shape=jax.ShapeDtypeStruct((N_HEADS_MLA, nq, 1, tile), F32),
        grid=(nq,),
        in_specs=[pl.BlockSpec((tile, o.shape[1]), lambda i: (i, 0)), pl.BlockSpec((tile, o.shape[1]), lambda i: (i, 0))],
        out_specs=pl.BlockSpec((N_HEADS_MLA, 1, 1, tile), lambda i: (0, i, 0, 0)),
        compiler_params=_params(("parallel",)),
    )(o, do)


def _attn_bwd(q, kv, kp, do, lse, delta, tile):
    S = q.shape[0]
    nq = S // tile
    scale = QK_DIM ** -0.5
    nt = (((1,), (1,)), ((), ()))
    tn = (((0,), (0,)), ((), ()))

    def body(kv_ref, kp_ref, q_ref, do_ref, lse_ref, d_ref, dq_ref, dkv_ref, dkp_ref, dk_s, dv_s):
        ki = pl.program_id(1)
        k = jnp.concatenate([kv_ref[:, 0:LANE], kp_ref[...]], axis=1)
        v = kv_ref[:, LANE:2 * LANE]

        @pl.when(ki == 0)
        def _():
            dq_ref[...] = jnp.zeros_like(dq_ref)

        dk_s[...] = jnp.zeros_like(dk_s)
        dv_s[...] = jnp.zeros_like(dv_s)

        def step(qi, masked):
            start = pl.multiple_of(qi * tile, tile)
            qv = q_ref[pl.ds(start, tile), :]
            dov = do_ref[pl.ds(start, tile), :]
            st = lax.dot_general(k, qv, nt, preferred_element_type=F32) * scale
            pt = jnp.exp(st - lse_ref[0, qi])
            if masked:
                krow = lax.broadcasted_iota(jnp.int32, pt.shape, 0)
                qcol = lax.broadcasted_iota(jnp.int32, pt.shape, 1)
                pt = jnp.where(krow <= qcol, pt, 0.0)
            dv_s[...] += jnp.dot(pt.astype(BF16), dov, preferred_element_type=F32)
            dpt = lax.dot_general(v, dov, nt, preferred_element_type=F32)
            dst = (pt * (dpt - d_ref[0, qi]) * scale).astype(BF16)
            dk_s[...] += jnp.dot(dst, qv, preferred_element_type=F32)
            dq_ref[pl.ds(start, tile), :] += lax.dot_general(dst, k, tn, preferred_element_type=F32)

        step(ki, True)

        def loop_body(qi, carry):
            step(qi, False)
            return carry

        lax.fori_loop(ki + 1, nq, loop_body, 0)
        dkv_ref[...] = jnp.concatenate([dk_s[:, 0:LANE], dv_s[...]], axis=1).astype(dkv_ref.dtype)
        dkp_ref[0] = dk_s[:, LANE:2 * LANE]

    return pl.pallas_call(
        body, name="attn_bwd",
        out_shape=(jax.ShapeDtypeStruct((S, N_HEADS_MLA * HEAD_PAD), F32),
                   jax.ShapeDtypeStruct((S, N_HEADS_MLA * HEAD_PAD), BF16),
                   jax.ShapeDtypeStruct((N_HEADS_MLA, S, LANE), F32)),
        grid=(N_HEADS_MLA, nq),
        in_specs=[pl.BlockSpec((tile, HEAD_PAD), lambda h, i: (i, h)),
                  pl.BlockSpec((tile, LANE), lambda h, i: (i, 0)),
                  pl.BlockSpec((S, HEAD_PAD), lambda h, i: (0, h)),
                  pl.BlockSpec((S, V_DIM), lambda h, i: (0, h)),
                  pl.BlockSpec((1, nq, 1, tile), lambda h, i: (h, 0, 0, 0)),
                  pl.BlockSpec((1, nq, 1, tile), lambda h, i: (h, 0, 0, 0))],
        out_specs=(pl.BlockSpec((S, HEAD_PAD), lambda h, i: (0, h)),
                   pl.BlockSpec((tile, HEAD_PAD), lambda h, i: (i, h)),
                   pl.BlockSpec((1, tile, LANE), lambda h, i: (h, i, 0))),
        scratch_shapes=[pltpu.VMEM((tile, HEAD_PAD), F32), pltpu.VMEM((tile, V_DIM), F32)],
        compiler_params=_params(("parallel", "arbitrary")),
    )(kv, kp, q, do, lse, delta)


def _shift_down(cur, halo, k):
    sh = pltpu.roll(cur, k, 0)
    hs = pltpu.roll(halo, k, 0)
    rows = lax.broadcasted_iota(jnp.int32, hs.shape, 0)
    first = jnp.where(rows < k, hs, sh[0:SUBLANE])
    if cur.shape[0] == SUBLANE:
        return first
    return jnp.concatenate([first, sh[SUBLANE:]], axis=0)


def _shift_up(cur, nxt, k):
    n = cur.shape[0]
    sh = pltpu.roll(cur, n - k, 0)
    ns = pltpu.roll(nxt, SUBLANE - k, 0)
    rows = lax.broadcasted_iota(jnp.int32, ns.shape, 0)
    last = jnp.where(rows >= SUBLANE - k, ns, sh[n - SUBLANE:])
    if n == SUBLANE:
        return last
    return jnp.concatenate([sh[:n - SUBLANE], last], axis=0)


def _conv_pre(cur, halo, w, b):
    shifted = [_shift_down(cur, halo, k) for k in range(1, CONV_WIDTH)]
    out = b + w[3:4] * cur
    for k in range(1, CONV_WIDTH):
        out = out + w[3 - k:4 - k] * shifted[k - 1]
    return out, shifted


def _conv_fwd(xbc, w, b):
    S = xbc.shape[0]

    def fn(i, n, cur, prev, w, b):
        halo = jnp.where(i > 0, prev, 0.0)
        pre, _ = _conv_pre(cur, halo, w, b)
        return (pre * _sigmoid(pre),)

    return _rows(fn, S, 256, [_row(xbc), _prev8(xbc), _full(w), _full(b)], [("row", xbc.shape[1], F32)], "conv_fwd")[0]


def _conv_bwd(xbc, dacts, w, b):
    S, C = xbc.shape

    def dsilu(pre):
        s = _sigmoid(pre)
        return s * (1.0 + pre * (1.0 - s))

    def fn(i, n, cur, prev, nxt, *rest):
        k3 = len(dacts)
        dcur = jnp.concatenate(rest[:k3], axis=1)
        dnxt = jnp.concatenate(rest[k3:2 * k3], axis=1)
        w, b = rest[2 * k3:]
        halo = jnp.where(i > 0, prev, 0.0)
        pre, shifted = _conv_pre(cur, halo, w, b)
        dpre = dcur * dsilu(pre)
        pre_n, _ = _conv_pre(nxt, cur[cur.shape[0] - SUBLANE:], w, b)
        dpre_n = jnp.where(i < n - 1, dnxt * dsilu(pre_n), 0.0)
        dx = w[3:4] * dpre
        rows = lax.broadcasted_iota(jnp.int32, (SUBLANE, C), 0)
        dw = jnp.where(rows == 3, _colsum(dpre * cur), 0.0)
        for k in range(1, CONV_WIDTH):
            dx = dx + w[3 - k:4 - k] * _shift_up(dpre, dpre_n, k)
            dw = dw + jnp.where(rows == 3 - k, _colsum(dpre * shifted[k - 1]), 0.0)
        return dx, dw, _colsum(dpre)

    return _rows(fn, S, 256, [_row(xbc), _prev8(xbc), _next8(xbc), *[_row(d) for d in dacts], *[_next8(d) for d in dacts],
                              _full(w), _full(b)],
                 [("row", C, BF16), ("acc", (SUBLANE, C), F32), ("acc", (1, C), F32)], "conv_bwd")


def _softplus(x):
    return jnp.maximum(x, 0.0) + jnp.log1p(jnp.exp(-jnp.abs(x)))


def _cumsum_rows(x):
    rows = lax.broadcasted_iota(jnp.int32, x.shape, 0)
    s = 1
    while s < x.shape[0]:
        x = x + jnp.where(rows >= s, pltpu.roll(x, s, 0), 0.0)
        s *= 2
    return x


def _revcumsum_rows(x):
    n = x.shape[0]
    rows = lax.broadcasted_iota(jnp.int32, x.shape, 0)
    s = 1
    while s < n:
        x = x + jnp.where(rows < n - s, pltpu.roll(x, n - s, 0), 0.0)
        s *= 2
    return x


def _dt_prep(dt_raw, dt_bias, a_log):
    S = dt_raw.shape[0]

    def body(raw_ref, bias_ref, alog_ref, dt_ref, cum_ref, cumt_ref):
        dt = _softplus(raw_ref[...] + bias_ref[...])
        cum = _cumsum_rows(dt * (-jnp.exp(alog_ref[...])))
        dt_ref[...] = dt
        cum_ref[...] = cum
        cumt_ref[...] = cum.T

    return pl.pallas_call(
        body, name="dt_prep",
        out_shape=(jax.ShapeDtypeStruct((S, LANE), F32), jax.ShapeDtypeStruct((S, LANE), F32),
                   jax.ShapeDtypeStruct((LANE, S), F32)),
        grid=(S // CHUNK,),
        in_specs=[pl.BlockSpec((CHUNK, LANE), lambda i: (i, 0)), pl.BlockSpec((1, LANE), lambda i: (0, 0)),
                  pl.BlockSpec((1, LANE), lambda i: (0, 0))],
        out_specs=(pl.BlockSpec((CHUNK, LANE), lambda i: (i, 0)), pl.BlockSpec((CHUNK, LANE), lambda i: (i, 0)),
                   pl.BlockSpec((LANE, CHUNK), lambda i: (0, i))),
        compiler_params=_params(("parallel",)),
    )(dt_raw, dt_bias, a_log)


_NT = (((1,), (1,)), ((), ()))
_TN = (((0,), (0,)), ((), ()))
P = SSM_HEADDIM
GW = HEADS_PER_GROUP * SSM_HEADDIM


PAIRS = HEADS_PER_GROUP // 2
SPREAD_W = HEADS_PER_GROUP * LANE


def _spread_matrix():
    e = np.zeros((SSM_GROUPS, LANE, SPREAD_W), np.float32)
    for g in range(SSM_GROUPS):
        for r in range(HEADS_PER_GROUP):
            e[g, g * HEADS_PER_GROUP + r, r * LANE:(r + 1) * LANE] = 1.0
    return jnp.asarray(e, BF16)


def _pieces(v, n):
    out = []
    for _ in range(n):
        p = v.astype(BF16)
        out.append(p)
        v = v - p.astype(F32)
    return out


def _spread(v, e, n):
    tot = None
    for p in _pieces(v, n):
        t = jnp.dot(p, e, preferred_element_type=F32)
        tot = t if tot is None else tot + t
    return tot


def _gather_rows(z, e):
    hi, lo = _pieces(z, 2)
    return lax.dot_general(hi, e, _NT, preferred_element_type=F32) + lax.dot_general(lo, e, _NT, preferred_element_type=F32)


def _decay_pair(cc, cr, transposed):
    L = cc.shape[0]
    halves = []
    for h in range(L // LANE):
        i = lax.broadcasted_iota(jnp.int32, (L, LANE), 0)
        j = lax.broadcasted_iota(jnp.int32, (L, LANE), 1) + h * LANE
        crh = cr[:, h * LANE:(h + 1) * LANE]
        if transposed:
            halves.append(jnp.exp(jnp.where(j >= i, crh - cc, NEG)))
        else:
            halves.append(jnp.exp(jnp.where(i >= j, cc - crh, NEG)))
    return jnp.concatenate(halves, axis=1)


def _ssd_fwd(xbc_c, dt, cum, cumt_g, spread):
    S = xbc_c.shape[0]
    nc = S // CHUNK
    L = CHUNK
    boff = D_INNER // D_STATE

    def body(x_ref, b_ref, c_ref, dt_ref, cum_ref, cumt_ref, e_ref, y_ref, st_ref, state):
        c = pl.program_id(1)

        @pl.when(c == 0)
        def _():
            state[...] = jnp.zeros_like(state)

        e = e_ref[0]
        bm = b_ref[...].astype(BF16)
        cm = c_ref[...].astype(BF16)
        cb = lax.dot_general(cm, bm, _NT, preferred_element_type=F32)
        rep_cum = _spread(cum_ref[...], e, 3)
        rep_dt = _spread(dt_ref[...], e, 2)
        lo = lax.broadcasted_iota(jnp.int32, (L, LANE), 1) < P
        lo1 = lax.broadcasted_iota(jnp.int32, (1, LANE), 1) < P
        top = lax.broadcasted_iota(jnp.int32, (LANE, LANE), 0) < P
        for p in range(PAIRS):
            t0, t1 = 2 * p * LANE, (2 * p + 1) * LANE
            cc0, cc1 = rep_cum[:, t0:t0 + LANE], rep_cum[:, t1:t1 + LANE]
            ccp = jnp.where(lo, cc0, cc1)
            cl0, cl1 = cc0[L - 1:L, :], cc1[L - 1:L, :]
            clp = jnp.where(lo1, cl0, cl1)
            xdt = x_ref[:, p * LANE:(p + 1) * LANE] * jnp.where(lo, rep_dt[:, t0:t0 + LANE], rep_dt[:, t1:t1 + LANE])
            xdb = xdt.astype(BF16)
            ys = []
            for r, cc in ((2 * p, cc0), (2 * p + 1, cc1)):
                m = (cb * _decay_pair(cc, cumt_ref[0, r:r + 1, :], False)).astype(BF16)
                ys.append(jnp.dot(m, xdb, preferred_element_type=F32))
            st = state[p * LANE:(p + 1) * LANE, :]
            st_ref[0, 0, p * LANE:(p + 1) * LANE, :] = st
            yoff = lax.dot_general(cm, st.astype(BF16), _NT, preferred_element_type=F32) * jnp.exp(ccp)
            y_ref[:, p * LANE:(p + 1) * LANE] = jnp.where(lo, ys[0], ys[1]) + yoff
            wend = jnp.exp(clp - ccp)
            ecl = jnp.where(top, jnp.exp(cl0), jnp.exp(cl1))
            state[p * LANE:(p + 1) * LANE, :] = st * ecl + lax.dot_general(
                (xdt * wend).astype(BF16), bm, _TN, preferred_element_type=F32)

    return pl.pallas_call(
        body, name="ssd_fwd",
        out_shape=(jax.ShapeDtypeStruct((S, D_INNER), F32), jax.ShapeDtypeStruct((SSM_GROUPS, nc, GW, D_STATE), F32)),
        grid=(SSM_GROUPS, nc),
        in_specs=[pl.BlockSpec((L, GW), lambda g, c: (c, g)),
                  pl.BlockSpec((L, D_STATE), lambda g, c: (c, boff + g)),
                  pl.BlockSpec((L, D_STATE), lambda g, c: (c, boff + SSM_GROUPS + g)),
                  pl.BlockSpec((L, LANE), lambda g, c: (c, 0)),
                  pl.BlockSpec((L, LANE), lambda g, c: (c, 0)),
                  pl.BlockSpec((1, HEADS_PER_GROUP, L), lambda g, c: (g, 0, c)),
                  pl.BlockSpec((1, LANE, SPREAD_W), lambda g, c: (g, 0, 0))],
        out_specs=(pl.BlockSpec((L, GW), lambda g, c: (c, g)),
                   pl.BlockSpec((1, 1, GW, D_STATE), lambda g, c: (g, c, 0, 0))),
        scratch_shapes=[pltpu.VMEM((GW, D_STATE), F32)],
        compiler_params=_params(("parallel", "arbitrary")),
    )(xbc_c, xbc_c, xbc_c, dt, cum, cumt_g, spread)


def _ssd_bwd(xbc_c, dt, cum, cumt_g, spread, states, dy, d_skip):
    S = xbc_c.shape[0]
    nc = S // CHUNK
    L = CHUNK
    boff = D_INNER // D_STATE
    rev = lambda c: nc - 1 - c

    def body(x_ref, b_ref, c_ref, dt_ref, cum_ref, cumt_ref, e_ref, st_ref, dy_ref, skip_ref,
             dx_ref, db_ref, dc_ref, ddt_ref, dcum_ref, dstate):
        c = pl.program_id(1)

        @pl.when(c == 0)
        def _():
            dstate[...] = jnp.zeros_like(dstate)

        e = e_ref[0]
        bf = b_ref[...]
        bm = bf.astype(BF16)
        cm = c_ref[...].astype(BF16)
        cb = lax.dot_general(cm, bm, _NT, preferred_element_type=F32)
        cbt = lax.dot_general(bm, cm, _NT, preferred_element_type=F32)
        rep_cum = _spread(cum_ref[...], e, 3)
        rep_dt = _spread(dt_ref[...], e, 2)
        lane = lax.broadcasted_iota(jnp.int32, (L, LANE), 1)
        lo = lane < P
        lo1 = lax.broadcasted_iota(jnp.int32, (1, LANE), 1) < P
        top = lax.broadcasted_iota(jnp.int32, (LANE, LANE), 0) < P
        last = lax.broadcasted_iota(jnp.int32, (L, LANE), 0) == L - 1
        dcb = jnp.zeros((L, L), F32)
        dcbt = jnp.zeros((L, L), F32)
        dbs = jnp.zeros((L, D_STATE), F32)
        dcs = jnp.zeros((L, D_STATE), F32)
        zs, zds = [], []
        for p in range(PAIRS):
            sl = slice(p * LANE, (p + 1) * LANE)
            t0, t1 = 2 * p * LANE, (2 * p + 1) * LANE
            cc0, cc1 = rep_cum[:, t0:t0 + LANE], rep_cum[:, t1:t1 + LANE]
            ccp = jnp.where(lo, cc0, cc1)
            cl0, cl1 = cc0[L - 1:L, :], cc1[L - 1:L, :]
            w0, w1 = jnp.exp(cl0 - cc0), jnp.exp(cl1 - cc1)
            wend = jnp.where(lo, w0, w1)
            ecc = jnp.exp(ccp)
            ecl0, ecl1 = jnp.exp(cl0), jnp.exp(cl1)
            dtp = jnp.where(lo, rep_dt[:, t0:t0 + LANE], rep_dt[:, t1:t1 + LANE])
            xp = x_ref[:, sl]
            xdt = xp * dtp
            xdb = xdt.astype(BF16)
            dyp = dy_ref[:, sl]
            st = st_ref[0, 0, sl, :]
            stb = st.astype(BF16)
            ds = dstate[sl, :]
            dsb = ds.astype(BF16)
            yoff = lax.dot_general(cm, stb, _NT, preferred_element_type=F32) * ecc
            dye = (dyp * ecc).astype(BF16)
            dcs = dcs + jnp.dot(dye, stb, preferred_element_type=F32)
            dstate[sl, :] = jnp.where(top, ecl0, ecl1) * ds + lax.dot_general(dye, cm, _TN, preferred_element_type=F32)
            dxd = lax.dot_general(bm, dsb, _NT, preferred_element_type=F32) * wend
            sst = ds * st
            dyo = dyp * yoff
            mts = []
            for r, cc, w, ecl, keep, keep_rows in ((2 * p, cc0, w0, ecl0, lo, top), (2 * p + 1, cc1, w1, ecl1, ~lo, ~top)):
                cr = cumt_ref[0, r:r + 1, :]
                decay = _decay_pair(cc, cr, False)
                decay_t = _decay_pair(cc, cr, True)
                m = cb * decay
                mt = cbt * decay_t
                dyr = jnp.where(keep, dyp, 0.0).astype(BF16)
                g = lax.dot_general(dyr, xdb, _NT, preferred_element_type=F32)
                gt = lax.dot_general(xdb, dyr, _NT, preferred_element_type=F32)
                q = g * m
                qt = gt * mt
                dcb = dcb + g * decay
                dcbt = dcbt + gt * decay_t
                mts.append(jnp.dot(mt.astype(BF16), dyr, preferred_element_type=F32))
                t = jnp.dot(jnp.where(keep, xdt, 0.0).astype(BF16), dsb, preferred_element_type=F32)
                dbs = dbs + t * w
                tb = t * bf * w
                end_row = _colsum(tb) + ecl * _colsum(jnp.where(keep_rows, sst, 0.0))
                z = (q[:, 0:LANE] + q[:, LANE:2 * LANE]) - (qt[:, 0:LANE] + qt[:, LANE:2 * LANE])
                z = z + jnp.where(keep, dyo, 0.0) - tb + jnp.where(last, end_row, 0.0)
                zs.append(z)
            dxd = dxd + mts[0] + mts[1]
            dx_ref[:, sl] = dxd * dtp + dyp * skip_ref[:, sl]
            zd = dxd * xp
            zds.append(jnp.where(lo, zd, 0.0))
            zds.append(jnp.where(lo, 0.0, zd))
        dc_ref[...] = dcs + jnp.dot(dcb.astype(BF16), bm, preferred_element_type=F32)
        db_ref[...] = dbs + jnp.dot(dcbt.astype(BF16), cm, preferred_element_type=F32)
        dcum_ref[0] = _gather_rows(jnp.concatenate(zs, axis=1), e)
        ddt_ref[0] = _gather_rows(jnp.concatenate(zds, axis=1), e)

    return pl.pallas_call(
        body, name="ssd_bwd",
        out_shape=(jax.ShapeDtypeStruct((S, D_INNER), F32),
                   jax.ShapeDtypeStruct((S, SSM_GROUPS * D_STATE), F32),
                   jax.ShapeDtypeStruct((S, SSM_GROUPS * D_STATE), F32),
                   jax.ShapeDtypeStruct((SSM_GROUPS, S, LANE), F32),
                   jax.ShapeDtypeStruct((SSM_GROUPS, S, LANE), F32)),
        grid=(SSM_GROUPS, nc),
        in_specs=[pl.BlockSpec((L, GW), lambda g, c: (rev(c), g)),
                  pl.BlockSpec((L, D_STATE), lambda g, c: (rev(c), boff + g)),
                  pl.BlockSpec((L, D_STATE), lambda g, c: (rev(c), boff + SSM_GROUPS + g)),
                  pl.BlockSpec((L, LANE), lambda g, c: (rev(c), 0)),
                  pl.BlockSpec((L, LANE), lambda g, c: (rev(c), 0)),
                  pl.BlockSpec((1, HEADS_PER_GROUP, L), lambda g, c: (g, 0, rev(c))),
                  pl.BlockSpec((1, LANE, SPREAD_W), lambda g, c: (g, 0, 0)),
                  pl.BlockSpec((1, 1, GW, D_STATE), lambda g, c: (g, rev(c), 0, 0)),
                  pl.BlockSpec((L, GW), lambda g, c: (rev(c), g)),
                  pl.BlockSpec((1, GW), lambda g, c: (0, g))],
        out_specs=(pl.BlockSpec((L, GW), lambda g, c: (rev(c), g)),
                   pl.BlockSpec((L, D_STATE), lambda g, c: (rev(c), g)),
                   pl.BlockSpec((L, D_STATE), lambda g, c: (rev(c), g)),
                   pl.BlockSpec((1, L, LANE), lambda g, c: (g, rev(c), 0)),
                   pl.BlockSpec((1, L, LANE), lambda g, c: (g, rev(c), 0))),
        scratch_shapes=[pltpu.VMEM((GW, D_STATE), F32)],
        compiler_params=_params(("parallel", "arbitrary")),
    )(xbc_c, xbc_c, xbc_c, dt, cum, cumt_g, spread, states, dy, d_skip)


def _dt_bwd(dt_raw, dt_bias, a_log, ddt_x, dcum):
    S = dt_raw.shape[0]
    n = S // CHUNK

    def body(raw_ref, ddx_ref, dcu_ref, bias_ref, alog_ref, draw_ref, gb_ref, ga_ref):
        i = pl.program_id(0)

        @pl.when(i == 0)
        def _():
            gb_ref[...] = jnp.zeros_like(gb_ref)
            ga_ref[...] = jnp.zeros_like(ga_ref)

        ddx, dcu = ddx_ref[0], dcu_ref[0]
        for g in range(1, SSM_GROUPS):
            ddx = ddx + ddx_ref[g]
            dcu = dcu + dcu_ref[g]
        xx = raw_ref[...] + bias_ref[...]
        dt = _softplus(xx)
        a = -jnp.exp(alog_ref[...])
        dda = _revcumsum_rows(dcu)
        lane = lax.broadcasted_iota(jnp.int32, xx.shape, 1)
        draw = jnp.where(lane < N_HEADS_SSM, (ddx + dda * a) * _sigmoid(xx), 0.0)
        draw_ref[...] = draw.astype(draw_ref.dtype)
        gb_ref[...] += _colsum(draw)
        ga_ref[...] += _colsum(dda * dt) * a

    row = pl.BlockSpec((CHUNK, LANE), lambda i: (i, 0))
    grp = pl.BlockSpec((SSM_GROUPS, CHUNK, LANE), lambda i: (0, i, 0))
    one = pl.BlockSpec((1, LANE), lambda i: (0, 0))
    return pl.pallas_call(
        body, name="dt_bwd",
        out_shape=(jax.ShapeDtypeStruct((S, LANE), BF16), jax.ShapeDtypeStruct((1, LANE), F32), jax.ShapeDtypeStruct((1, LANE), F32)),
        grid=(n,),
        in_specs=[row, grp, grp, one, one],
        out_specs=(row, one, one),
        compiler_params=_params(("arbitrary",)),
    )(dt_raw, ddt_x, dcum, dt_bias, a_log)


def _adamw(w, g, m, v, name):
    shape = w.shape
    cols = shape[-1]
    rows = int(np.prod(shape[:-1]))
    w2, g2, m2, v2 = (t.reshape(rows, cols) for t in (w, g, m, v))
    tr = rows if rows * cols <= 512 * 1024 else _tile(rows, max(SUBLANE, (512 * 1024 // cols) // SUBLANE * SUBLANE), SUBLANE)
    c1 = 1.0 - ADAM_B1 ** ADAM_STEP
    c2 = 1.0 - ADAM_B2 ** ADAM_STEP

    def body(w_ref, g_ref, m_ref, v_ref, d_ref, mo_ref, vo_ref):
        gv = g_ref[...]
        mn = ADAM_B1 * m_ref[...] + (1.0 - ADAM_B1) * gv
        vn = ADAM_B2 * v_ref[...] + (1.0 - ADAM_B2) * (gv * gv)
        d_ref[...] = -ADAM_LR * ((mn / c1) / (jnp.sqrt(vn / c2) + ADAM_EPS) + ADAM_WD * w_ref[...])
        mo_ref[...] = mn
        vo_ref[...] = vn

    spec = pl.BlockSpec((tr, cols), lambda i: (i, 0))
    outs = pl.pallas_call(
        body, name=name,
        out_shape=tuple(jax.ShapeDtypeStruct((rows, cols), F32) for _ in range(3)),
        grid=(rows // tr,),
        in_specs=[spec] * 4, out_specs=(spec,) * 3,
        compiler_params=_params(("parallel",)),
    )(w2, g2, m2, v2)
    return tuple(o.reshape(shape) for o in outs)


def _prep_weights(w_in, w_uq):
    offs = np.cumsum((0,) + IN_SPLITS)
    pad = lambda t: jnp.pad(t, ((0, 0), (0, LANE - t.shape[1])))
    pieces = dict(
        qkv=w_in[:, offs[0]:offs[2]],
        kr=pad(w_in[:, offs[2]:offs[3]]),
        z=w_in[:, offs[3]:offs[4]],
        xbc=w_in[:, offs[4]:offs[5]],
        dt=pad(w_in[:, offs[5]:offs[6]]),
        g=w_in[:, offs[6]:offs[8]],
    )
    uq = w_uq.reshape(Q_LORA, N_HEADS_MLA, QK_DIM)
    uq = jnp.pad(uq, ((0, 0), (0, 0), (0, HEAD_PAD - QK_DIM))).reshape(Q_LORA, N_HEADS_MLA * HEAD_PAD)
    return pieces, uq


def _local_step(x, p, positions, ex, sp, target):
    W = gw = ex
    S = x.shape[0]
    tile = min(ATTN_TILE, S)
    pos_col = positions.reshape(S, 1)
    invf = ROPE_THETA ** (-jnp.arange(0, QK_ROPE, 2, dtype=F32) / QK_ROPE)
    invf = jnp.pad(jnp.concatenate([invf, invf]), (0, LANE - QK_ROPE)).reshape(1, LANE)
    wp, w_uq_p = _prep_weights(W["w_in"], W["w_uq"])
    padl = lambda t: jnp.pad(t, ((0, 0), (0, LANE - t.shape[1])))
    dt_bias_p, a_log_p = padl(sp["dt_bias"]), padl(sp["a_log"])
    dskip_ch = jnp.repeat(sp["d_skip"], SSM_HEADDIM, axis=1)
    p_bf = p.astype(BF16)
    RW = 256

    (u_bf,) = _rows(lambda i, n, x, g: (x * _rstd(x) * g,), S, RW, [_row(x), _full(sp["mix_norm_pre"])],
                    [("row", D_MODEL, BF16)], "norm_pre")
    cqkv = ex.mm(u_bf, wp["qkv"], name="mm_qkv")
    z = ex.mm(u_bf, wp["z"], name="mm_z")
    xbc = ex.mm(u_bf, wp["xbc"], name="mm_xbc")
    gates = ex.mm(u_bf, wp["g"], name="mm_gates")
    kr_pad = ex.mm(u_bf, wp["kr"], name="mm_kr")
    dt_raw = ex.mm(u_bf, wp["dt"], name="mm_dt")

    def qkv_norm(i, n, cq, ckv, gq, gkv):
        return cq * _rstd(cq) * gq, ckv * _rstd(ckv) * gkv

    cqn, ckvn = _rows(qkv_norm, S, 512, [_row(cqkv, Q_LORA, 0), _row(cqkv, KV_LORA, 1), _full(sp["q_norm"]), _full(sp["kv_norm"])],
                      [("row", Q_LORA, BF16), ("row", KV_LORA, BF16)], "qkv_norm")
    q_raw = ex.mm(cqn, w_uq_p, name="mm_uq")
    kv = ex.mm(ckvn, W["w_ukv"], out_dtype=BF16, name="mm_ukv")
    q_bf, kp_bf = _rope_fwd(q_raw, kr_pad, pos_col, invf)
    attn, lse = _attn_fwd(q_bf, kv, kp_bf, tile)

    xbc_c = _conv_fwd(xbc, sp["conv_w"], sp["conv_b"])
    dt, cum, cumt = _dt_prep(dt_raw, dt_bias_p, a_log_p)
    cumt_g = cumt[:N_HEADS_SSM].reshape(SSM_GROUPS, HEADS_PER_GROUP, S)
    spread = _spread_matrix()
    y, states = _ssd_fwd(xbc_c, dt, cum, cumt_g, spread)

    GN = D_INNER // SSM_GROUPS

    def gated(y, xs, z, dsk):
        yt = y + dsk * xs
        sz = _sigmoid(z)
        return yt, sz, yt * (z * sz)

    def gated_norm(i, n, y, xs, z, dsk, gn):
        _, _, yg = gated(y, xs, z, dsk)
        parts = []
        for g in range(SSM_GROUPS):
            blk = yg[:, g * GN:(g + 1) * GN]
            parts.append(blk * _rstd(blk) * gn[:, g * GN:(g + 1) * GN])
        return (jnp.concatenate(parts, axis=1),)

    (ssm,) = _rows(gated_norm, S, 128, [_row(y), _row(xbc_c, D_INNER, 0), _row(z), _full(dskip_ch), _full(sp["ssm_norm"])],
                   [("row", D_INNER, BF16)], "gated_norm")

    a_o = ex.mm(attn, W["w_attn_o"], name="mm_attn_o")
    b_o = ex.mm(ssm, W["w_ssm_o"], name="mm_ssm_o")

    def mix(i, n, ga, gs, a, b):
        return (_sigmoid(ga) * a + _sigmoid(gs) * b,)

    (mixed,) = _rows(mix, S, RW, [_row(gates, D_MODEL, 0), _row(gates, D_MODEL, 1), _row(a_o), _row(b_o)],
                     [("row", D_MODEL, BF16)], "mix")
    m2 = ex.mm(mixed, W["w_out"], name="mm_out")

    def post(i, n, h, m, gpost, gpre):
        hn = h + m * _rstd(m) * gpost
        return hn, hn * _rstd(hn) * gpre

    h1, f_bf = _rows(post, S, RW, [_row(x), _row(m2), _full(sp["mix_norm_post"]), _full(sp["ffn_norm_pre"])],
                     [("row", D_MODEL, F32), ("row", D_MODEL, BF16)], "post_mix")
    ga = ex.mm(f_bf, W["w_gate"], name="mm_gate")
    up = ex.mm(f_bf, W["w_up"], name="mm_up")
    (s_bf,) = _rows(lambda i, n, a, b: (a * _sigmoid(a) * b,), S, RW, [_row(ga), _row(up)], [("row", D_FF, BF16)], "swiglu")
    f2 = ex.mm(s_bf, W["w_down"], name="mm_down")
    h2, n3_bf = _rows(post, S, RW, [_row(h1), _row(f2), _full(sp["ffn_norm_post"]), _full(sp["ple_norm_pre"])],
                      [("row", D_MODEL, F32), ("row", D_MODEL, BF16)], "post_ffn")
    gpre = ex.mm(n3_bf, W["w_ple_gate"], name="mm_ple_gate")
    pe = ex.mm(p_bf, W["w_ple"], name="mm_ple")

    def ple_loss(i, n, h2, gpre, pe, tgt, gpost):
        gate = _sigmoid(gpre)
        e = pe * gate
        r = _rstd(e)
        diff = h2 + e * r * gpost - tgt
        loss = 0.5 * jnp.sum(jnp.mean(diff * diff, axis=1, keepdims=True))
        dh3 = diff * (1.0 / D_MODEL)
        de, dg_rows = _norm_bwd(e, r, gpost, dh3)
        return (jnp.full((1, LANE), loss, F32), dh3, de * gate, de * pe * gate * (1.0 - gate), _colsum(dg_rows))

    loss, dh3, dpe, dgpre, g_ple_post = _rows(
        ple_loss, S, 128, [_row(h2), _row(gpre), _row(pe), _row(target), _full(sp["ple_norm_post"])],
        [("acc", (1, LANE), F32), ("row", D_MODEL, F32), ("row", D_MODEL, BF16), ("row", D_MODEL, BF16),
         ("acc", (1, D_MODEL), F32)], "ple_loss")

    gs = {"ple_norm_post": g_ple_post}
    gw["w_ple"] = ex.mm(p_bf, dpe, ta=True, name="mmg_ple")
    gw["w_ple_gate"] = ex.mm(n3_bf, dgpre, ta=True, name="mmg_ple_gate")
    dn3 = ex.mm(dgpre, W["w_ple_gate"], tb=True, name="mmb_ple_gate")

    def post_bwd(i, n, h, m, dhn, dn, gpost, gpre):
        rm = _rstd(m)
        hn = h + m * rm * gpost
        dx, dgpre_rows = _norm_bwd(hn, _rstd(hn), gpre, dn)
        dhn_t = dhn + dx
        dm, dgpost_rows = _norm_bwd(m, rm, gpost, dhn_t)
        return dhn_t, dm, _colsum(dgpre_rows), _colsum(dgpost_rows)

    def run_post_bwd(h, m, dhn, dn, gpost, gpre, name):
        return _rows(post_bwd, S, 128, [_row(h), _row(m), _row(dhn), _row(dn), _full(gpost), _full(gpre)],
                     [("row", D_MODEL, F32), ("row", D_MODEL, BF16), ("acc", (1, D_MODEL), F32), ("acc", (1, D_MODEL), F32)], name)

    dh2, df2, gs["ple_norm_pre"], gs["ffn_norm_post"] = run_post_bwd(
        h1, f2, dh3, dn3, sp["ffn_norm_post"], sp["ple_norm_pre"], "post_ffn_bwd")
    gw["w_down"] = ex.mm(s_bf, df2, ta=True, name="mmg_down")
    ds = ex.mm(df2, W["w_down"], tb=True, out_dtype=BF16, name="mmb_down")

    def swiglu_bwd(i, n, a, b, ds):
        sa = _sigmoid(a)
        return ds * b * (sa * (1.0 + a * (1.0 - sa))), ds * (a * sa)

    dga, dup = _rows(swiglu_bwd, S, RW, [_row(ga), _row(up), _row(ds)], [("row", D_FF, BF16), ("row", D_FF, BF16)], "swiglu_bwd")
    gw["w_gate"] = ex.mm(f_bf, dga, ta=True, name="mmg_gate")
    gw["w_up"] = ex.mm(f_bf, dup, ta=True, name="mmg_up")
    df = ex.mm(dga, W["w_gate"], tb=True, name="mmb_gate")
    df = ex.mm(dup, W["w_up"], tb=True, add=df, name="mmb_up")
    dh1, dm2, gs["ffn_norm_pre"], gs["mix_norm_post"] = run_post_bwd(
        x, m2, dh2, df, sp["mix_norm_post"], sp["ffn_norm_pre"], "post_mix_bwd")
    gw["w_out"] = ex.mm(mixed, dm2, ta=True, name="mmg_out")
    dmixed = ex.mm(dm2, W["w_out"], tb=True, out_dtype=BF16, name="mmb_out")

    def mix_bwd(i, n, ga, gs_, a, b, dm):
        sa, ss = _sigmoid(ga), _sigmoid(gs_)
        return dm * sa, dm * ss, jnp.concatenate([dm * a * sa * (1.0 - sa), dm * b * ss * (1.0 - ss)], axis=1)

    da_o, db_o, dgates = _rows(mix_bwd, S, RW, [_row(gates, D_MODEL, 0), _row(gates, D_MODEL, 1), _row(a_o), _row(b_o), _row(dmixed)],
                               [("row", D_MODEL, BF16), ("row", D_MODEL, BF16), ("row", 2 * D_MODEL, BF16)], "mix_bwd")
    gw["w_attn_o"] = ex.mm(attn, da_o, ta=True, name="mmg_attn_o")
    dattn = ex.mm(da_o, W["w_attn_o"], tb=True, out_dtype=BF16, name="mmb_attn_o")
    gw["w_ssm_o"] = ex.mm(ssm, db_o, ta=True, name="mmg_ssm_o")
    dssm = ex.mm(db_o, W["w_ssm_o"], tb=True, out_dtype=BF16, name="mmb_ssm_o")

    delta = _attn_delta(attn, dattn, tile)
    dq, dkv, dkp = _attn_bwd(q_bf, kv, kp_bf, dattn, lse, delta, tile)
    dq_raw, dkr = _rope_bwd(dq, dkp, pos_col, invf)
    g_uq_p = ex.mm(cqn, dq_raw, ta=True, name="mmg_uq")
    gw["w_uq"] = g_uq_p.reshape(Q_LORA, N_HEADS_MLA, HEAD_PAD)[:, :, :QK_DIM].reshape(Q_LORA, N_HEADS_MLA * QK_DIM)
    dcqn = ex.mm(dq_raw, w_uq_p, tb=True, name="mmb_uq")
    gw["w_ukv"] = ex.mm(ckvn, dkv, ta=True, name="mmg_ukv")
    dckvn = ex.mm(dkv, W["w_ukv"], tb=True, name="mmb_ukv")

    def qkv_norm_bwd(i, n, cq, ckv, dq_, dkv_, gq, gkv):
        dcq, gq_rows = _norm_bwd(cq, _rstd(cq), gq, dq_)
        dckv, gkv_rows = _norm_bwd(ckv, _rstd(ckv), gkv, dkv_)
        return jnp.concatenate([dcq, dckv], axis=1), _colsum(gq_rows), _colsum(gkv_rows)

    dcqkv, gs["q_norm"], gs["kv_norm"] = _rows(
        qkv_norm_bwd, S, 512, [_row(cqkv, Q_LORA, 0), _row(cqkv, KV_LORA, 1), _row(dcqn), _row(dckvn), _full(sp["q_norm"]), _full(sp["kv_norm"])],
        [("row", Q_LORA + KV_LORA, BF16), ("acc", (1, Q_LORA), F32), ("acc", (1, KV_LORA), F32)], "qkv_norm_bwd")

    def gated_norm_bwd(i, n, y, xs, z, dssm, dsk, gn):
        yt, sz, yg = gated(y, xs, z, dsk)
        dyg_parts, gn_parts = [], []
        for g in range(SSM_GROUPS):
            sl = slice(g * GN, (g + 1) * GN)
            blk = yg[:, sl]
            dblk, rows = _norm_bwd(blk, _rstd(blk), gn[:, sl], dssm[:, sl])
            dyg_parts.append(dblk)
            gn_parts.append(_colsum(rows))
        dyg = jnp.concatenate(dyg_parts, axis=1)
        dyt = dyg * (z * sz)
        dz = dyg * yt * (sz * (1.0 + z * (1.0 - sz)))
        return dyt, dz, jnp.concatenate(gn_parts, axis=1), _colsum(dyt * xs)

    dy, dz, gs["ssm_norm"], g_dskip_ch = _rows(
        gated_norm_bwd, S, 128, [_row(y), _row(xbc_c, D_INNER, 0), _row(z), _row(dssm), _full(dskip_ch), _full(sp["ssm_norm"])],
        [("row", D_INNER, F32), ("row", D_INNER, BF16), ("acc", (1, D_INNER), F32), ("acc", (1, D_INNER), F32)],
        "gated_norm_bwd")
    gs["d_skip"] = jnp.sum(g_dskip_ch.reshape(N_HEADS_SSM, SSM_HEADDIM), axis=1).reshape(1, N_HEADS_SSM)
    dxs, dbm, dcm, ddt_x, dcum = _ssd_bwd(xbc_c, dt, cum, cumt_g, spread, states, dy, dskip_ch)
    ddt_raw, g_dtb, g_alog = _dt_bwd(dt_raw, dt_bias_p, a_log_p, ddt_x, dcum)
    gs["dt_bias"] = g_dtb[:, :N_HEADS_SSM]
    gs["a_log"] = g_alog[:, :N_HEADS_SSM]
    dxbc, g_conv_w8, gs["conv_b"] = _conv_bwd(xbc, [dxs, dbm, dcm], sp["conv_w"], sp["conv_b"])
    gs["conv_w"] = g_conv_w8[:CONV_WIDTH]

    g_qkv = ex.mm(u_bf, dcqkv, ta=True, name="mmg_qkv")
    g_kr = ex.mm(u_bf, dkr, ta=True, name="mmg_kr")
    g_z = ex.mm(u_bf, dz, ta=True, name="mmg_z")
    g_xbc = ex.mm(u_bf, dxbc, ta=True, name="mmg_xbc")
    g_dt = ex.mm(u_bf, ddt_raw, ta=True, name="mmg_dt")
    g_g = ex.mm(u_bf, dgates, ta=True, name="mmg_gates")
    gw["w_in"] = jnp.concatenate([g_qkv, g_kr[:, :QK_ROPE], g_z, g_xbc, g_dt[:, :N_HEADS_SSM], g_g], axis=1)
    du = ex.mm(dcqkv, wp["qkv"], tb=True, name="mmb_qkv")
    du = ex.mm(dkr, wp["kr"], tb=True, add=du, name="mmb_kr")
    du = ex.mm(ddt_raw, wp["dt"], tb=True, add=du, name="mmb_dt")
    du = ex.mm(dz, wp["z"], tb=True, add=du, name="mmb_z")
    du = ex.mm(dxbc, wp["xbc"], tb=True, add=du, name="mmb_xbc")
    du = ex.mm(dgates, wp["g"], tb=True, add=du, name="mmb_gates")

    def pre_bwd(i, n, x, du, dh, g):
        dx, rows = _norm_bwd(x, _rstd(x), g, du)
        return dh + dx, _colsum(rows)

    grad_x, gs["mix_norm_pre"] = _rows(pre_bwd, S, RW, [_row(x), _row(du), _row(dh1), _full(sp["mix_norm_pre"])],
                                       [("row", D_MODEL, F32), ("acc", (1, D_MODEL), F32)], "norm_pre_bwd")
    return loss, grad_x, gs


BIG = (
    ("w_in", (2048, 3872), 1), ("w_uq", (512, 768), 1), ("w_ukv", (512, 1024), 1), ("w_attn_o", (512, 2048), 0),
    ("w_ssm_o", (1024, 2048), 0), ("w_out", (512, 2048), 0), ("w_gate", (2048, 1408), 1), ("w_up", (2048, 1408), 1),
    ("w_down", (1408, 2048), 0), ("w_ple_gate", (512, 2048), 0), ("w_ple", (256, 512), 1),
)
SMALL = (
    ("mix_norm_pre", 2048), ("mix_norm_post", 2048), ("q_norm", 512), ("kv_norm", 512), ("conv_b", 6144), ("dt_bias", 64),
    ("a_log", 64), ("d_skip", 64), ("ssm_norm", 4096), ("ffn_norm_pre", 2048), ("ffn_norm_post", 2048),
    ("ple_norm_pre", 2048), ("ple_norm_post", 2048),
)
CONV_W_LEN = CONV_WIDTH * CONV_DIM
SMALL_ROWS = 384


def _place():
    return lax.axis_index("x"), lax.axis_index("y"), lax.axis_index("c")


def _flip(v, bit):
    return 1 - v if bit else v


def _alone(hook, name):
    n_in, n_out = len(hook.ins), len(hook.out_shapes)

    def body(*refs):
        start, finish = hook.make(refs[:n_in], refs[n_in:n_in + n_out], refs[n_in + n_out:])
        start()
        finish()

    return list(pl.pallas_call(
        body, name=name, out_shape=tuple(hook.out_shapes),
        in_specs=[pl.BlockSpec(memory_space=pl.ANY)] * n_in,
        out_specs=tuple(pl.BlockSpec(memory_space=pl.ANY) for _ in range(n_out)),
        scratch_shapes=[pltpu.SemaphoreType.DMA((s,)) for s in hook.sems],
        input_output_aliases=hook.aliases,
    )(*hook.ins))


def _simple(copies):
    def start():
        for cp in copies:
            cp.start()

    def finish():
        for cp in copies:
            cp.wait()

    return start, finish


def _gather_hook(shards):
    n = len(shards)

    def make(ins, outs, sems):
        send_sems, recv_sems, fwd_send_sems, fwd_recv_sems = sems
        x, y, c = _place()
        me = 2 * x + y
        far, near = [], []
        for a in range(n):
            half = shards[a].shape[0] // 2
            lo = pl.multiple_of(c * half, SUBLANE)
            for k in (1, 2, 3):
                px, py = _flip(x, k >> 1), _flip(y, k & 1)
                far.append(pltpu.make_async_remote_copy(
                    src_ref=ins[a].at[pl.ds(lo, half), :], dst_ref=outs[a].at[me, pl.ds(lo, half), :],
                    send_sem=send_sems.at[3 * a + k - 1], recv_sem=recv_sems.at[3 * a + k - 1],
                    device_id=(px, py, c), device_id_type=MESH_ID))
                got = outs[a].at[2 * px + py, pl.ds(lo, half), :]
                near.append(pltpu.make_async_remote_copy(
                    src_ref=got, dst_ref=got, send_sem=fwd_send_sems.at[3 * a + k - 1], recv_sem=fwd_recv_sems.at[3 * a + k - 1],
                    device_id=(x, y, 1 - c), device_id_type=MESH_ID))

        def start():
            for cp in far:
                cp.start()

        def finish():
            for cp, fwd in zip(far, near):
                cp.wait_recv()
                fwd.start()
            for cp, fwd in zip(far, near):
                cp.wait_send()
                fwd.wait()

        return start, finish

    return _Hook(shards, [jax.ShapeDtypeStruct((N_CHIPS, *s.shape), s.dtype) for s in shards], (3 * n,) * 4, make)


def _swap_hook(gs):
    n = len(gs)

    def make(ins, outs, sems):
        send_sems, recv_sems = sems
        x, y, c = _place()
        copies = []
        for a in range(n):
            half = gs[a].shape[1] // 2
            src = ins[a].at[:, pl.ds(pl.multiple_of((1 - c) * half, SUBLANE), half), :]
            copies.append(pltpu.make_async_remote_copy(
                src_ref=src, dst_ref=outs[a], send_sem=send_sems.at[a], recv_sem=recv_sems.at[a],
                device_id=(x, y, 1 - c), device_id_type=MESH_ID))
        return _simple(copies)

    return _Hook(gs, [jax.ShapeDtypeStruct((g.shape[0], g.shape[1] // 2, g.shape[2]), g.dtype) for g in gs], (n, n), make)


def _sum_rows_tile(rows, cols):
    return _tile(rows, max(2 * SUBLANE, (512 * 1024 // cols) // (2 * SUBLANE) * (2 * SUBLANE)), 2 * SUBLANE)


def _add_half(g, other, c, name):
    n, R, C = g.shape
    half = R // 2
    tr = _sum_rows_tile(half, C)
    nb = half // tr

    def body(c_ref, g_ref, o_ref, out_ref):
        out_ref[...] = (g_ref[...] + o_ref[...]).astype(out_ref.dtype)

    return pl.pallas_call(
        body, name=name,
        out_shape=jax.ShapeDtypeStruct((n, half, C), BF16),
        grid_spec=pltpu.PrefetchScalarGridSpec(
            num_scalar_prefetch=1, grid=(n, nb),
            in_specs=[pl.BlockSpec((1, tr, C), lambda j, i, c_ref: (j, c_ref[0] * nb + i, 0)),
                      pl.BlockSpec((1, tr, C), lambda j, i, c_ref: (j, i, 0))],
            out_specs=pl.BlockSpec((1, tr, C), lambda j, i, c_ref: (j, i, 0))),
        compiler_params=_params(("parallel", "parallel")),
    )(c, g, other)


def _scatter_hook(parts):
    n = len(parts)

    def make(ins, outs, sems):
        send_sems, recv_sems = sems
        x, y, c = _place()
        copies = []
        for a in range(n):
            for k in (1, 2, 3):
                px, py = _flip(x, k >> 1), _flip(y, k & 1)
                copies.append(pltpu.make_async_remote_copy(
                    src_ref=ins[a].at[2 * px + py], dst_ref=outs[a].at[k - 1], send_sem=send_sems.at[3 * a + k - 1],
                    recv_sem=recv_sems.at[3 * a + k - 1], device_id=(px, py, c), device_id_type=MESH_ID))
        return _simple(copies)

    return _Hook(parts, [jax.ShapeDtypeStruct((3, *p.shape[1:]), p.dtype) for p in parts], (3 * n, 3 * n), make)


def _add_chips(part, got, place, name):
    n, R, C = part.shape
    tr = _sum_rows_tile(R, C)
    nb = R // tr

    def body(place_ref, p_ref, g_ref, out_ref):
        out_ref[...] = ((p_ref[0].astype(F32) + g_ref[0].astype(F32)) + g_ref[1].astype(F32)) + g_ref[2].astype(F32)

    return pl.pallas_call(
        body, name=name,
        out_shape=jax.ShapeDtypeStruct((2 * R, C), F32),
        grid_spec=pltpu.PrefetchScalarGridSpec(
            num_scalar_prefetch=1, grid=(nb,),
            in_specs=[pl.BlockSpec((1, tr, C), lambda i, place_ref: (place_ref[0], i, 0)),
                      pl.BlockSpec((3, tr, C), lambda i, place_ref: (0, i, 0))],
            out_specs=pl.BlockSpec((tr, C), lambda i, place_ref: (place_ref[1] * nb + i, 0))),
        compiler_params=_params(("parallel",)),
    )(place, part, got)


def _join_hook(wholes):
    n = len(wholes)

    def make(ins, outs, sems):
        send_sems, recv_sems = sems
        x, y, c = _place()
        copies = []
        for a in range(n):
            half = wholes[a].shape[0] // 2
            rows = outs[a].at[pl.ds(pl.multiple_of(c * half, SUBLANE), half), :]
            copies.append(pltpu.make_async_remote_copy(
                src_ref=rows, dst_ref=rows, send_sem=send_sems.at[a], recv_sem=recv_sems.at[a],
                device_id=(x, y, 1 - c), device_id_type=MESH_ID))
        return _simple(copies)

    return _Hook(wholes, [jax.ShapeDtypeStruct(w.shape, w.dtype) for w in wholes], (n, n), make, aliases={a: a for a in range(n)})


def _allreduce_small(vec, name):
    R, C = vec.shape

    def body(v_ref, o_ref, buf, send_sems, recv_sems):
        x, y, c = _place()
        me = 4 * x + 2 * y + c
        buf[me] = v_ref[...]
        copies = []
        for k in range(1, N_DEV):
            peer = (_flip(x, (k >> 2) & 1), _flip(y, (k >> 1) & 1), _flip(c, k & 1))
            copies.append(pltpu.make_async_remote_copy(
                src_ref=v_ref, dst_ref=buf.at[me], send_sem=send_sems.at[k - 1], recv_sem=recv_sems.at[k - 1],
                device_id=peer, device_id_type=MESH_ID))
        for cp in copies:
            cp.start()
        for cp in copies:
            cp.wait()
        tot = buf[0]
        for d in range(1, N_DEV):
            tot = tot + buf[d]
        o_ref[...] = tot

    return pl.pallas_call(
        body, name=name,
        out_shape=jax.ShapeDtypeStruct((R, C), F32),
        in_specs=[pl.BlockSpec(memory_space=pltpu.VMEM)],
        out_specs=pl.BlockSpec(memory_space=pltpu.VMEM),
        scratch_shapes=[pltpu.VMEM((N_DEV, R, C), F32), pltpu.SemaphoreType.DMA((N_DEV - 1,)), pltpu.SemaphoreType.DMA((N_DEV - 1,))],
    )(vec)


def _unstack(gathered, shape, axis):
    if axis == 0:
        return gathered.reshape(N_CHIPS * shape[0], shape[1])
    return jnp.concatenate([gathered[j] for j in range(N_CHIPS)], axis=1)


def _stack(whole, shape, axis):
    if axis == 0:
        return whole.reshape(N_CHIPS, shape[0], shape[1])
    return jnp.stack([whole[:, j * shape[1]:(j + 1) * shape[1]] for j in range(N_CHIPS)])


GATHER_FIRST = ("w_in", "w_uq", "w_ukv")
GATHER_IN = {"mm_z": ("w_attn_o", "w_ssm_o", "w_out"), "mm_xbc": ("w_gate", "w_up"), "mm_gates": ("w_down", "w_ple_gate", "w_ple")}
REDUCE = (
    (("w_ple", "w_ple_gate", "w_down"), "mmb_down", "mmg_gate", "mmg_up"),
    (("w_gate", "w_up"), "mmb_gate", "mmb_up", "mmg_out"),
    (("w_out", "w_attn_o", "w_ssm_o"), "mmb_ssm_o", "mmg_z", "mmg_xbc"),
    (("w_uq", "w_ukv"), "mmb_ukv", "mmg_gates", "mmb_z"),
    (("w_in",), "mmb_z", "mmb_xbc", "mmb_gates"),
)


class _Exchange:
    def __init__(self, shards, chip, core):
        self.shards, self.chip = shards, chip
        self.core_arr = core.reshape(1).astype(jnp.int32)
        self.place_arr = jnp.stack([chip, core]).astype(jnp.int32)
        self.shape = {n: (shape, axis) for n, shape, axis in BIG}
        self.whole, self.grads, self.reduced, self.pending, self.tails = {}, {}, {}, {}, 0
        hook, done = self._gather(GATHER_FIRST)
        done(_alone(hook, "gather_first"))
        for host, names in GATHER_IN.items():
            self._arm(host, *self._gather(names))

    def _arm(self, host, hook, done):
        self.pending.setdefault(host, []).append((hook, done))

    def _gather(self, names):
        shards = [self.shards[n].astype(BF16) for n in names]

        def done(outs):
            for n, s, g in zip(names, shards, outs):
                self.whole[n] = _unstack(lax.dynamic_update_slice(g, s[None], (self.chip, 0, 0)), *self.shape[n])

        return _gather_hook(shards), done

    def __getitem__(self, name):
        return self.whole[name]

    def __setitem__(self, name, grad):
        self.grads[name] = grad
        for names, swap_host, scatter_host, join_host in REDUCE:
            if name in names and all(n in self.grads for n in names):
                self._reduce(names, swap_host, scatter_host, join_host)

    def _reduce(self, names, swap_host, scatter_host, join_host):
        stacked = [_stack(self.grads[n], *self.shape[n]) for n in names]

        def joined(outs):
            for n, r in zip(names, outs):
                self.reduced[n] = r.reshape(1, *self.shape[n][0])

        def swapped(outs):
            parts = [_add_half(g, o, self.core_arr, "add_half_" + n) for n, g, o in zip(names, stacked, outs)]

            def scattered(gots):
                wholes = [_add_chips(q, o, self.place_arr, "add_chips_" + n) for n, q, o in zip(names, parts, gots)]
                self._arm(join_host, _join_hook(wholes), joined)

            self._arm(scatter_host, _scatter_hook(parts), scattered)

        self._arm(swap_host, _swap_hook(stacked), swapped)

    def _run(self, todo, call):
        hook = _merge_hooks([h for h, _ in todo])
        result, outs = call(hook)
        off = 0
        for h, done in todo:
            done(outs[off:off + len(h.out_shapes)])
            off += len(h.out_shapes)
        return result

    def mm(self, a, b, *, name, **kw):
        todo = self.pending.pop(name, None)
        if not todo:
            return _mm(a, b, name=name, **kw)
        return self._run(todo, lambda hook: _mm(a, b, name=name, hook=hook, **kw))

    def finish(self):
        while self.pending:
            todo = self.pending.pop(next(iter(self.pending)))
            self.tails += 1
            self._run(todo, lambda hook: (None, _alone(hook, "exchange_tail_%d" % self.tails)))
        return self.reduced


def kernel(x, p, positions, mix_norm_pre, mix_norm_post, w_in, q_norm, w_uq, kv_norm, w_ukv, conv_w, conv_b, dt_bias, a_log, d_skip, ssm_norm, w_attn_o, w_ssm_o, w_out, ffn_norm_pre, ffn_norm_post, w_gate, w_up, w_down, ple_norm_pre, ple_norm_post, w_ple_gate, w_ple, loss_target, m_mix_norm_pre, m_mix_norm_post, m_w_in, m_q_norm, m_w_uq, m_kv_norm, m_w_ukv, m_conv_w, m_conv_b, m_dt_bias, m_a_log, m_d_skip, m_ssm_norm, m_w_attn_o, m_w_ssm_o, m_w_out, m_ffn_norm_pre, m_ffn_norm_post, m_w_gate, m_w_up, m_w_down, m_ple_norm_pre, m_ple_norm_post, m_w_ple_gate, m_w_ple, v_mix_norm_pre, v_mix_norm_post, v_w_in, v_q_norm, v_w_uq, v_kv_norm, v_w_ukv, v_conv_w, v_conv_b, v_dt_bias, v_a_log, v_d_skip, v_ssm_norm, v_w_attn_o, v_w_ssm_o, v_w_out, v_ffn_norm_pre, v_ffn_norm_post, v_w_gate, v_w_up, v_w_down, v_ple_norm_pre, v_ple_norm_post, v_w_ple_gate, v_w_ple):
    given = dict(locals())
    names = [n for n, _, _ in BIG] + [n for n, _ in SMALL] + ["conv_w"]
    order = ["mix_norm_pre", "mix_norm_post", "w_in", "q_norm", "w_uq", "kv_norm", "w_ukv", "conv_w", "conv_b", "dt_bias", "a_log",
             "d_skip", "ssm_norm", "w_attn_o", "w_ssm_o", "w_out", "ffn_norm_pre", "ffn_norm_post", "w_gate", "w_up", "w_down",
             "ple_norm_pre", "ple_norm_post", "w_ple_gate", "w_ple"]
    assert sorted(names) == sorted(order)
    cx, cy, cc = _place()
    chip = 2 * cx + cy
    conv_cols = CONV_DIM // N_CHIPS

    ex = _Exchange({n: given[n][0] for n, _, _ in BIG}, chip, cc)
    own = jnp.where(cc == 0, conv_w[0], 0.0)
    conv_vec = lax.dynamic_update_slice(jnp.zeros((CONV_WIDTH, CONV_DIM), F32), own, (0, chip * conv_cols))
    conv_full = _allreduce_small(conv_vec.reshape(CONV_W_LEN // LANE, LANE), "gather_conv_w").reshape(CONV_WIDTH, CONV_DIM)
    sp = {n: given[n] for n, _ in SMALL}
    sp["conv_w"] = conv_full

    loss_part, grad_x, gs = _local_step(x[0], p[0, 0], positions[0], ex, sp, loss_target[0])

    g_big = ex.finish()

    small_parts = [gs[n] for n, _ in SMALL] + [gs["conv_w"], loss_part[:, :1]]
    small_vec = jnp.concatenate([t.reshape(-1) for t in small_parts])
    small_vec = jnp.pad(small_vec, (0, SMALL_ROWS * LANE - small_vec.shape[0])).reshape(SMALL_ROWS, LANE)
    small_sum = _allreduce_small(small_vec, "allreduce_small").reshape(-1)
    g_small, off = {}, 0
    for n, length in SMALL:
        g_small[n] = small_sum[off:off + length].reshape(1, length)
        off += length
    g_conv = small_sum[off:off + CONV_W_LEN].reshape(CONV_WIDTH, CONV_DIM)
    g_small["conv_w"] = lax.dynamic_slice(g_conv, (0, chip * conv_cols), (CONV_WIDTH, conv_cols)).reshape(1, CONV_WIDTH, conv_cols)
    loss = small_sum[off + CONV_W_LEN]

    grads, deltas, new_m, new_v = [], [], [], []
    for n in order:
        g = g_big[n] if n in g_big else g_small[n]
        d, m_, v_ = _adamw(given[n], g, given["m_" + n], given["v_" + n], "adamw_" + n)
        grads.append(g)
        deltas.append(d)
        new_m.append(m_)
        new_v.append(v_)
    return (loss, grad_x.reshape(x.shape), *grads, *deltas, *new_m, *new_v)
```

```python
import numpy as np
import jax
import jax.numpy as jnp
from jax import lax
from jax.experimental import pallas as pl
from jax.experimental.pallas import tpu as pltpu

F32 = jnp.float32
BF16 = jnp.bfloat16

D_MODEL = 2048
N_HEADS_MLA = 16
Q_LORA = 512
KV_LORA = 512
QK_NOPE = 128
QK_ROPE = 64
V_DIM = 128
QK_DIM = QK_NOPE + QK_ROPE
ROPE_THETA = 10000.0
D_INNER = 4096
SSM_HEADDIM = 64
N_HEADS_SSM = 64
SSM_GROUPS = 8
HEADS_PER_GROUP = 8
D_STATE = 128
CONV_WIDTH = 4
CHUNK = 256
CONV_DIM = D_INNER + 2 * SSM_GROUPS * D_STATE
D_FF = 5632
PLE_DIM = 256
EPS = 1e-6
IN_SPLITS = (Q_LORA, KV_LORA, QK_ROPE, D_INNER, CONV_DIM, N_HEADS_SSM, D_MODEL, D_MODEL)

ADAM_LR = 0.001
ADAM_B1 = 0.9
ADAM_B2 = 0.999
ADAM_EPS = 1e-08
ADAM_WD = 0.01
ADAM_STEP = 10

LANE = 128
SUBLANE = 8
HEAD_PAD = 256
VMEM_LIMIT = 56 * 1024 * 1024
ATTN_TILE = 1024
NEG = -1e30

MESH_ID = pl.DeviceIdType.MESH
N_CHIPS = 4
N_DEV = 8


def _tile(n, pref, mult=LANE):
    if n <= pref:
        return n
    t = (pref // mult) * mult
    while t >= mult:
        if n % t == 0:
            return t
        t -= mult
    return n


def _params(sem, vmem=VMEM_LIMIT, **kw):
    return pltpu.CompilerParams(dimension_semantics=sem, vmem_limit_bytes=vmem, **kw)


class _Hook:
    def __init__(self, ins, out_shapes, sems, make, aliases=None):
        self.ins, self.out_shapes, self.sems, self.make, self.aliases = list(ins), list(out_shapes), tuple(sems), make, dict(aliases or {})


def _merge_hooks(hooks):
    hooks = [h for h in hooks if h is not None]
    if not hooks:
        return None
    ins, outs, sems, aliases, cuts = [], [], [], {}, []
    for h in hooks:
        cuts.append((len(ins), len(outs), len(sems)))
        aliases.update({len(ins) + i: len(outs) + o for i, o in h.aliases.items()})
        ins += h.ins
        outs += h.out_shapes
        sems += h.sems

    def make(in_refs, out_refs, sem_refs):
        pairs = []
        for h, (i0, o0, s0) in zip(hooks, cuts):
            pairs.append(h.make(in_refs[i0:i0 + len(h.ins)], out_refs[o0:o0 + len(h.out_shapes)], sem_refs[s0:s0 + len(h.sems)]))

        def start():
            for st, _ in pairs:
                st()

        def finish():
            for _, fin in pairs:
                fin()

        return start, finish

    return _Hook(ins, outs, sems, make, aliases)


def _mm(a, b, *, ta=False, tb=False, add=None, out_dtype=F32, name, tm=1024, tn=1536, tk=2048, hook=None):
    if ta:
        K, M = a.shape
    else:
        M, K = a.shape
    N = b.shape[0] if tb else b.shape[1]
    assert (b.shape[1] if tb else b.shape[0]) == K, (a.shape, b.shape, ta, tb)
    tm, tn, tk = _tile(M, tm), _tile(N, tn), _tile(K, tk)
    nk = K // tk
    dn = (((0 if ta else 1,), (1 if tb else 0,)), ((), ()))
    has_add = add is not None
    n_own = 3 if has_add else 2
    n_hin = len(hook.ins) if hook else 0
    n_hout = len(hook.out_shapes) if hook else 0
    grid = (M // tm, N // tn, nk)

    def body(*refs):
        a_ref, b_ref = refs[:2]
        c_ref = refs[2] if has_add else None
        o_ref = refs[n_own + n_hin]
        scratch = refs[n_own + n_hin + 1 + n_hout:]
        if hook:
            start, finish = hook.make(refs[n_own:n_own + n_hin], refs[n_own + n_hin + 1:n_own + n_hin + 1 + n_hout],
                                      scratch[len(scratch) - len(hook.sems):])
            ids = [pl.program_id(d) for d in range(3)]
            pl.when((ids[0] == 0) & (ids[1] == 0) & (ids[2] == 0))(start)
        prod = lax.dot_general(a_ref[...].astype(BF16), b_ref[...].astype(BF16), dn, preferred_element_type=F32)
        if nk == 1:
            o_ref[...] = ((c_ref[...] + prod) if has_add else prod).astype(out_dtype)
        else:
            acc = scratch[0]
            k = pl.program_id(2)

            @pl.when(k == 0)
            def _():
                acc[...] = (c_ref[...] + prod) if has_add else prod

            @pl.when(k > 0)
            def _():
                acc[...] += prod

            @pl.when(k == nk - 1)
            def _():
                o_ref[...] = acc[...].astype(out_dtype)
        if hook:
            pl.when((ids[0] == grid[0] - 1) & (ids[1] == grid[1] - 1) & (ids[2] == grid[2] - 1))(finish)

    a_spec = pl.BlockSpec((tk, tm), lambda i, j, k: (k, i)) if ta else pl.BlockSpec((tm, tk), lambda i, j, k: (i, k))
    b_spec = pl.BlockSpec((tn, tk), lambda i, j, k: (j, k)) if tb else pl.BlockSpec((tk, tn), lambda i, j, k: (k, j))
    in_specs = [a_spec, b_spec]
    args = [a, b]
    if has_add:
        in_specs.append(pl.BlockSpec((tm, tn), lambda i, j, k: (i, j)))
        args.append(add)
    hbm = pl.BlockSpec(memory_space=pl.ANY)
    scratch_shapes = [pltpu.VMEM((tm, tn), F32)] if nk > 1 else []
    out_shape = jax.ShapeDtypeStruct((M, N), out_dtype)
    out_spec = pl.BlockSpec((tm, tn), lambda i, j, k: (i, j))
    if not hook:
        return pl.pallas_call(
            body, name=name, out_shape=out_shape, grid=grid, in_specs=in_specs, out_specs=out_spec,
            scratch_shapes=scratch_shapes,
            compiler_params=_params(("parallel", "parallel", "arbitrary")),
        )(*args)
    outs = pl.pallas_call(
        body, name=name, out_shape=(out_shape, *hook.out_shapes), grid=grid,
        in_specs=in_specs + [hbm] * n_hin, out_specs=(out_spec, *[hbm] * n_hout),
        scratch_shapes=scratch_shapes + [pltpu.SemaphoreType.DMA((s,)) for s in hook.sems],
        input_output_aliases={n_own + i: 1 + o for i, o in hook.aliases.items()},
        compiler_params=_params(("arbitrary", "arbitrary", "arbitrary")),
    )(*args, *hook.ins)
    return outs[0], list(outs[1:])


def _row(arr, width=None, cblk=0):
    return ("row", arr, arr.shape[1] if width is None else width, cblk)


def _full(arr):
    return ("full", arr)


def _prev8(arr):
    return ("prev8", arr)


def _next8(arr):
    return ("next8", arr)


def _rows(fn, n_rows, tm, ins, outs, name):
    tm = min(tm, n_rows)
    assert n_rows % tm == 0 and tm % SUBLANE == 0
    n = n_rows // tm
    in_specs, args = [], []
    for spec in ins:
        kind, arr = spec[0], spec[1]
        if kind == "row":
            _, _, w, cb = spec
            in_specs.append(pl.BlockSpec((tm, w), lambda i, cb=cb: (i, cb)))
        elif kind == "full":
            in_specs.append(pl.BlockSpec(arr.shape, lambda i, nd=arr.ndim: (0,) * nd))
        elif kind == "prev8":
            in_specs.append(pl.BlockSpec((SUBLANE, arr.shape[1]),
                                         lambda i: (jnp.maximum(i * (tm // SUBLANE) - 1, 0), 0)))
        elif kind == "next8":
            last = n_rows // SUBLANE - 1
            in_specs.append(pl.BlockSpec((SUBLANE, arr.shape[1]),
                                         lambda i: (jnp.minimum((i + 1) * (tm // SUBLANE), last), 0)))
        else:
            raise ValueError(kind)
        args.append(arr)
    out_shapes, out_specs = [], []
    any_acc = False
    for spec in outs:
        if spec[0] == "row":
            _, w, dt = spec
            out_shapes.append(jax.ShapeDtypeStruct((n_rows, w), dt))
            out_specs.append(pl.BlockSpec((tm, w), lambda i: (i, 0)))
        else:
            _, shp, dt = spec
            any_acc = True
            out_shapes.append(jax.ShapeDtypeStruct(shp, dt))
            out_specs.append(pl.BlockSpec(shp, lambda i, nd=len(shp): (0,) * nd))
    nin = len(ins)

    def body(*refs):
        i = pl.program_id(0)
        vals = fn(i, n, *[r[...] for r in refs[:nin]])
        for o_ref, spec, v in zip(refs[nin:], outs, vals):
            if spec[0] == "acc":
                @pl.when(i == 0)
                def _(o_ref=o_ref):
                    o_ref[...] = jnp.zeros_like(o_ref)

                o_ref[...] += v.astype(o_ref.dtype)
            else:
                o_ref[...] = v.astype(o_ref.dtype)

    res = pl.pallas_call(
        body, name=name,
        out_shape=tuple(out_shapes),
        grid=(n,),
        in_specs=in_specs,
        out_specs=tuple(out_specs),
        compiler_params=_params(("arbitrary",) if any_acc else ("parallel",)),
    )(*args)
    return res


def _rstd(x):
    return lax.rsqrt(jnp.mean(x * x, axis=-1, keepdims=True) + EPS)


def _norm_bwd(x, r, g, dy):
    xh = x * r
    dyg = dy * g
    dx = r * (dyg - xh * jnp.mean(dyg * xh, axis=-1, keepdims=True))
    return dx, dy * xh


def _sigmoid(x):
    return 0.5 * jnp.tanh(0.5 * x) + 0.5


def _colsum(v):
    return jnp.sum(v, axis=0, keepdims=True)


def _rope_tables(pos, invf):
    ang = pos.astype(F32) * invf
    lane = lax.broadcasted_iota(jnp.int32, ang.shape, 1)
    cos, sin = jnp.cos(ang), jnp.sin(ang)
    ct = jnp.where(lane < QK_ROPE, cos, 0.0)
    sa = jnp.where(lane < QK_ROPE // 2, -sin, 0.0)
    sb = jnp.where((lane >= QK_ROPE // 2) & (lane < QK_ROPE), sin, 0.0)
    return ct, sa, sb


def _rope(b, ct, sa, sb):
    return ct * b + sa * pltpu.roll(b, LANE - QK_ROPE // 2, 1) + sb * pltpu.roll(b, QK_ROPE // 2, 1)


def _rope_t(d, ct, sa, sb):
    return ct * d + pltpu.roll(sa * d, QK_ROPE // 2, 1) + pltpu.roll(sb * d, LANE - QK_ROPE // 2, 1)


def _rope_fwd(q_raw, kr_pad, pos_col, invf):
    S = q_raw.shape[0]

    def fn(i, n, q, kr, pos, invf):
        ct, sa, sb = _rope_tables(pos, invf)
        parts = []
        for h in range(N_HEADS_MLA):
            parts.append(q[:, h * HEAD_PAD:h * HEAD_PAD + LANE])
            parts.append(_rope(q[:, h * HEAD_PAD + LANE:(h + 1) * HEAD_PAD], ct, sa, sb))
        return jnp.concatenate(parts, axis=1), _rope(kr, ct, sa, sb)

    return _rows(fn, S, 256, [_row(q_raw), _row(kr_pad), _row(pos_col), _full(invf)],
                 [("row", N_HEADS_MLA * HEAD_PAD, BF16), ("row", LANE, BF16)], "rope_fwd")


def _rope_bwd(dq, dkp, pos_col, invf):
    S = dq.shape[0]
    tm = 256

    def body(dq_ref, dkp_ref, pos_ref, invf_ref, dqo_ref, dkr_ref):
        ct, sa, sb = _rope_tables(pos_ref[...], invf_ref[...])
        for h in range(N_HEADS_MLA):
            dqo_ref[:, h * HEAD_PAD:h * HEAD_PAD + LANE] = dq_ref[:, h * HEAD_PAD:h * HEAD_PAD + LANE].astype(BF16)
            dqo_ref[:, h * HEAD_PAD + LANE:(h + 1) * HEAD_PAD] = _rope_t(
                dq_ref[:, h * HEAD_PAD + LANE:(h + 1) * HEAD_PAD], ct, sa, sb).astype(BF16)
        tot = dkp_ref[0]
        for h in range(1, N_HEADS_MLA):
            tot = tot + dkp_ref[h]
        dkr_ref[...] = _rope_t(tot, ct, sa, sb).astype(BF16)

    return pl.pallas_call(
        body, name="rope_bwd",
        out_shape=(jax.ShapeDtypeStruct(dq.shape, BF16), jax.ShapeDtypeStruct((S, LANE), BF16)),
        grid=(S // tm,),
        in_specs=[pl.BlockSpec((tm, dq.shape[1]), lambda i: (i, 0)),
                  pl.BlockSpec((N_HEADS_MLA, tm, LANE), lambda i: (0, i, 0)),
                  pl.BlockSpec((tm, 1), lambda i: (i, 0)),
                  pl.BlockSpec((1, LANE), lambda i: (0, 0))],
        out_specs=(pl.BlockSpec((tm, dq.shape[1]), lambda i: (i, 0)), pl.BlockSpec((tm, LANE), lambda i: (i, 0))),
        compiler_params=_params(("parallel",)),
    )(dq, dkp, pos_col, invf)


def _row_of(col, n):
    eye = lax.broadcasted_iota(jnp.int32, (LANE, LANE), 0) == lax.broadcasted_iota(jnp.int32, (LANE, LANE), 1)
    parts = [jnp.sum(jnp.where(eye, col[i:i + LANE], 0.0), axis=0, keepdims=True) for i in range(0, n, LANE)]
    return parts[0] if len(parts) == 1 else jnp.concatenate(parts, axis=1)


def _attn_fwd(q, kv, kp, tile):
    S = q.shape[0]
    nq = S // tile
    scale = QK_DIM ** -0.5
    nt = (((1,), (1,)), ((), ()))

    def body(q_ref, kv_ref, kp_ref, o_ref, lse_ref, m_s, l_s, acc_s, s_buf):
        qi = pl.program_id(1)
        qv = q_ref[...]
        m_s[...] = jnp.full_like(m_s, NEG)
        l_s[...] = jnp.zeros_like(l_s)
        acc_s[...] = jnp.zeros_like(acc_s)

        def scores(j):
            start = pl.multiple_of(j * tile, tile)
            k = jnp.concatenate([kv_ref[pl.ds(start, tile), 0:LANE], kp_ref[pl.ds(start, tile), :]], axis=1)
            return lax.dot_general(qv, k, nt, preferred_element_type=F32) * scale

        def update(s, j):
            v = kv_ref[pl.ds(pl.multiple_of(j * tile, tile), tile), LANE:2 * LANE]
            m_old = m_s[...]
            m_new = jnp.maximum(m_old, jnp.max(s, axis=1, keepdims=True))
            alpha = jnp.exp(m_old - m_new)
            p = jnp.exp(s - m_new)
            l_s[...] = alpha * l_s[...] + jnp.sum(p, axis=1, keepdims=True)
            acc_s[...] = alpha * acc_s[...] + jnp.dot(p.astype(BF16), v, preferred_element_type=F32)
            m_s[...] = m_new

        s_buf[0] = scores(0)

        def loop_body(j, carry):
            nxt = scores(j + 1)
            update(s_buf[lax.rem(j, 2)], j)
            s_buf[lax.rem(j + 1, 2)] = nxt
            return carry

        lax.fori_loop(0, qi, loop_body, 0)
        s = s_buf[lax.rem(qi, 2)]
        row = lax.broadcasted_iota(jnp.int32, s.shape, 0)
        col = lax.broadcasted_iota(jnp.int32, s.shape, 1)
        update(jnp.where(row >= col, s, NEG), qi)
        l = l_s[...]
        o_ref[...] = (acc_s[...] / l).astype(o_ref.dtype)
        lse_ref[0, 0] = _row_of(m_s[...] + jnp.log(l), tile)

    return pl.pallas_call(
        body, name="attn_fwd",
        out_shape=(jax.ShapeDtypeStruct((S, N_HEADS_MLA * V_DIM), BF16),
                   jax.ShapeDtypeStruct((N_HEADS_MLA, nq, 1, tile), F32)),
        grid=(N_HEADS_MLA, nq),
        in_specs=[pl.BlockSpec((tile, HEAD_PAD), lambda h, i: (i, h)),
                  pl.BlockSpec((S, HEAD_PAD), lambda h, i: (0, h)),
                  pl.BlockSpec((S, LANE), lambda h, i: (0, 0))],
        out_specs=(pl.BlockSpec((tile, V_DIM), lambda h, i: (i, h)),
                   pl.BlockSpec((1, 1, 1, tile), lambda h, i: (h, i, 0, 0))),
        scratch_shapes=[pltpu.VMEM((tile, 1), F32), pltpu.VMEM((tile, 1), F32), pltpu.VMEM((tile, V_DIM), F32),
                        pltpu.VMEM((2, tile, tile), F32)],
        compiler_params=_params(("parallel", "arbitrary")),
    )(q, kv, kp)


def _attn_delta(o, do, tile):
    S = o.shape[0]
    nq = S // tile

    def body(o_ref, do_ref, d_ref):
        prod = o_ref[...].astype(F32) * do_ref[...].astype(F32)
        for h in range(N_HEADS_MLA):
            col = jnp.sum(prod[:, h * V_DIM:(h + 1) * V_DIM], axis=1, keepdims=True)
            d_ref[h, 0] = _row_of(col, tile)

    return pl.pallas_call(
        body, name="attn_delta",
        out_shape=jax.ShapeDtypeStruct((N_HEADS_MLA, nq, 1, tile), F32),
        grid=(nq,),
        in_specs=[pl.BlockSpec((tile, o.shape[1]), lambda i: (i, 0)), pl.BlockSpec((tile, o.shape[1]), lambda i: (i, 0))],
        out_specs=pl.BlockSpec((N_HEADS_MLA, 1, 1, tile), lambda i: (0, i, 0, 0)),
        compiler_params=_params(("parallel",)),
    )(o, do)


def _attn_bwd(q, kv, kp, do, lse, delta, tile):
    S = q.shape[0]
    nq = S // tile
    scale = QK_DIM ** -0.5
    nt = (((1,), (1,)), ((), ()))
    tn = (((0,), (0,)), ((), ()))

    def body(kv_ref, kp_ref, q_ref, do_ref, lse_ref, d_ref, dq_ref, dkv_ref, dkp_ref, dk_s, dv_s):
        ki = pl.program_id(1)
        k = jnp.concatenate([kv_ref[:, 0:LANE], kp_ref[...]], axis=1)
        v = kv_ref[:, LANE:2 * LANE]

        @pl.when(ki == 0)
        def _():
            dq_ref[...] = jnp.zeros_like(dq_ref)

        dk_s[...] = jnp.zeros_like(dk_s)
        dv_s[...] = jnp.zeros_like(dv_s)

        def step(qi, masked):
            start = pl.multiple_of(qi * tile, tile)
            qv = q_ref[pl.ds(start, tile), :]
            dov = do_ref[pl.ds(start, tile), :]
            st = lax.dot_general(k, qv, nt, preferred_element_type=F32) * scale
            pt = jnp.exp(st - lse_ref[0, qi])
            if masked:
                krow = lax.broadcasted_iota(jnp.int32, pt.shape, 0)
                qcol = lax.broadcasted_iota(jnp.int32, pt.shape, 1)
                pt = jnp.where(krow <= qcol, pt, 0.0)
            dv_s[...] += jnp.dot(pt.astype(BF16), dov, preferred_element_type=F32)
            dpt = lax.dot_general(v, dov, nt, preferred_element_type=F32)
            dst = (pt * (dpt - d_ref[0, qi]) * scale).astype(BF16)
            dk_s[...] += jnp.dot(dst, qv, preferred_element_type=F32)
            dq_ref[pl.ds(start, tile), :] += lax.dot_general(dst, k, tn, preferred_element_type=F32)

        step(ki, True)

        def loop_body(qi, carry):
            step(qi, False)
            return carry

        lax.fori_loop(ki + 1, nq, loop_body, 0)
        dkv_ref[...] = jnp.concatenate([dk_s[:, 0:LANE], dv_s[...]], axis=1).astype(dkv_ref.dtype)
        dkp_ref[0] = dk_s[:, LANE:2 * LANE]

    return pl.pallas_call(
        body, name="attn_bwd",
        out_shape=(jax.ShapeDtypeStruct((S, N_HEADS_MLA * HEAD_PAD), F32),
                   jax.ShapeDtypeStruct((S, N_HEADS_MLA * HEAD_PAD), BF16),
                   jax.ShapeDtypeStruct((N_HEADS_MLA, S, LANE), F32)),
        grid=(N_HEADS_MLA, nq),
        in_specs=[pl.BlockSpec((tile, HEAD_PAD), lambda h, i: (i, h)),
                  pl.BlockSpec((tile, LANE), lambda h, i: (i, 0)),
                  pl.BlockSpec((S, HEAD_PAD), lambda h, i: (0, h)),
                  pl.BlockSpec((S, V_DIM), lambda h, i: (0, h)),
                  pl.BlockSpec((1, nq, 1, tile), lambda h, i: (h, 0, 0, 0)),
                  pl.BlockSpec((1, nq, 1, tile), lambda h, i: (h, 0, 0, 0))],
        out_specs=(pl.BlockSpec((S, HEAD_PAD), lambda h, i: (0, h)),
                   pl.BlockSpec((tile, HEAD_PAD), lambda h, i: (i, h)),
                   pl.BlockSpec((1, tile, LANE), lambda h, i: (h, i, 0))),
        scratch_shapes=[pltpu.VMEM((tile, HEAD_PAD), F32), pltpu.VMEM((tile, V_DIM), F32)],
        compiler_params=_params(("parallel", "arbitrary")),
    )(kv, kp, q, do, lse, delta)


def _shift_down(cur, halo, k):
    sh = pltpu.roll(cur, k, 0)
    hs = pltpu.roll(halo, k, 0)
    rows = lax.broadcasted_iota(jnp.int32, hs.shape, 0)
    first = jnp.where(rows < k, hs, sh[0:SUBLANE])
    if cur.shape[0] == SUBLANE:
        return first
    return jnp.concatenate([first, sh[SUBLANE:]], axis=0)


def _shift_up(cur, nxt, k):
    n = cur.shape[0]
    sh = pltpu.roll(cur, n - k, 0)
    ns = pltpu.roll(nxt, SUBLANE - k, 0)
    rows = lax.broadcasted_iota(jnp.int32, ns.shape, 0)
    last = jnp.where(rows >= SUBLANE - k, ns, sh[n - SUBLANE:])
    if n == SUBLANE:
        return last
    return jnp.concatenate([sh[:n - SUBLANE], last], axis=0)


def _conv_pre(cur, halo, w, b):
    shifted = [_shift_down(cur, halo, k) for k in range(1, CONV_WIDTH)]
    out = b + w[3:4] * cur
    for k in range(1, CONV_WIDTH):
        out = out + w[3 - k:4 - k] * shifted[k - 1]
    return out, shifted


def _conv_fwd(xbc, w, b):
    S = xbc.shape[0]

    def fn(i, n, cur, prev, w, b):
        halo = jnp.where(i > 0, prev, 0.0)
        pre, _ = _conv_pre(cur, halo, w, b)
        return (pre * _sigmoid(pre),)

    return _rows(fn, S, 256, [_row(xbc), _prev8(xbc), _full(w), _full(b)], [("row", xbc.shape[1], F32)], "conv_fwd")[0]


def _conv_bwd(xbc, dacts, w, b):
    S, C = xbc.shape

    def dsilu(pre):
        s = _sigmoid(pre)
        return s * (1.0 + pre * (1.0 - s))

    def fn(i, n, cur, prev, nxt, *rest):
        k3 = len(dacts)
        dcur = jnp.concatenate(rest[:k3], axis=1)
        dnxt = jnp.concatenate(rest[k3:2 * k3], axis=1)
        w, b = rest[2 * k3:]
        halo = jnp.where(i > 0, prev, 0.0)
        pre, shifted = _conv_pre(cur, halo, w, b)
        dpre = dcur * dsilu(pre)
        pre_n, _ = _conv_pre(nxt, cur[cur.shape[0] - SUBLANE:], w, b)
        dpre_n = jnp.where(i < n - 1, dnxt * dsilu(pre_n), 0.0)
        dx = w[3:4] * dpre
        rows = lax.broadcasted_iota(jnp.int32, (SUBLANE, C), 0)
        dw = jnp.where(rows == 3, _colsum(dpre * cur), 0.0)
        for k in range(1, CONV_WIDTH):
            dx = dx + w[3 - k:4 - k] * _shift_up(dpre, dpre_n, k)
            dw = dw + jnp.where(rows == 3 - k, _colsum(dpre * shifted[k - 1]), 0.0)
        return dx, dw, _colsum(dpre)

    return _rows(fn, S, 256, [_row(xbc), _prev8(xbc), _next8(xbc), *[_row(d) for d in dacts], *[_next8(d) for d in dacts],
                              _full(w), _full(b)],
                 [("row", C, BF16), ("acc", (SUBLANE, C), F32), ("acc", (1, C), F32)], "conv_bwd")


def _softplus(x):
    return jnp.maximum(x, 0.0) + jnp.log1p(jnp.exp(-jnp.abs(x)))


def _cumsum_rows(x):
    rows = lax.broadcasted_iota(jnp.int32, x.shape, 0)
    s = 1
    while s < x.shape[0]:
        x = x + jnp.where(rows >= s, pltpu.roll(x, s, 0), 0.0)
        s *= 2
    return x


def _revcumsum_rows(x):
    n = x.shape[0]
    rows = lax.broadcasted_iota(jnp.int32, x.shape, 0)
    s = 1
    while s < n:
        x = x + jnp.where(rows < n - s, pltpu.roll(x, n - s, 0), 0.0)
        s *= 2
    return x


def _dt_prep(dt_raw, dt_bias, a_log):
    S = dt_raw.shape[0]

    def body(raw_ref, bias_ref, alog_ref, dt_ref, cum_ref, cumt_ref):
        dt = _softplus(raw_ref[...] + bias_ref[...])
        cum = _cumsum_rows(dt * (-jnp.exp(alog_ref[...])))
        dt_ref[...] = dt
        cum_ref[...] = cum
        cumt_ref[...] = cum.T

    return pl.pallas_call(
        body, name="dt_prep",
        out_shape=(jax.ShapeDtypeStruct((S, LANE), F32), jax.ShapeDtypeStruct((S, LANE), F32),
                   jax.ShapeDtypeStruct((LANE, S), F32)),
        grid=(S // CHUNK,),
        in_specs=[pl.BlockSpec((CHUNK, LANE), lambda i: (i, 0)), pl.BlockSpec((1, LANE), lambda i: (0, 0)),
                  pl.BlockSpec((1, LANE), lambda i: (0, 0))],
        out_specs=(pl.BlockSpec((CHUNK, LANE), lambda i: (i, 0)), pl.BlockSpec((CHUNK, LANE), lambda i: (i, 0)),
                   pl.BlockSpec((LANE, CHUNK), lambda i: (0, i))),
        compiler_params=_params(("parallel",)),
    )(dt_raw, dt_bias, a_log)


_NT = (((1,), (1,)), ((), ()))
_TN = (((0,), (0,)), ((), ()))
P = SSM_HEADDIM
GW = HEADS_PER_GROUP * SSM_HEADDIM


PAIRS = HEADS_PER_GROUP // 2
SPREAD_W = HEADS_PER_GROUP * LANE


def _spread_matrix():
    e = np.zeros((SSM_GROUPS, LANE, SPREAD_W), np.float32)
    for g in range(SSM_GROUPS):
        for r in range(HEADS_PER_GROUP):
            e[g, g * HEADS_PER_GROUP + r, r * LANE:(r + 1) * LANE] = 1.0
    return jnp.asarray(e, BF16)


def _pieces(v, n):
    out = []
    for _ in range(n):
        p = v.astype(BF16)
        out.append(p)
        v = v - p.astype(F32)
    return out


def _spread(v, e, n):
    tot = None
    for p in _pieces(v, n):
        t = jnp.dot(p, e, preferred_element_type=F32)
        tot = t if tot is None else tot + t
    return tot


def _gather_rows(z, e):
    hi, lo = _pieces(z, 2)
    return lax.dot_general(hi, e, _NT, preferred_element_type=F32) + lax.dot_general(lo, e, _NT, preferred_element_type=F32)


def _decay_pair(cc, cr, transposed):
    L = cc.shape[0]
    halves = []
    for h in range(L // LANE):
        i = lax.broadcasted_iota(jnp.int32, (L, LANE), 0)
        j = lax.broadcasted_iota(jnp.int32, (L, LANE), 1) + h * LANE
        crh = cr[:, h * LANE:(h + 1) * LANE]
        if transposed:
            halves.append(jnp.exp(jnp.where(j >= i, crh - cc, NEG)))
        else:
            halves.append(jnp.exp(jnp.where(i >= j, cc - crh, NEG)))
    return jnp.concatenate(halves, axis=1)


def _ssd_fwd(xbc_c, dt, cum, cumt_g, spread):
    S = xbc_c.shape[0]
    nc = S // CHUNK
    L = CHUNK
    boff = D_INNER // D_STATE

    def body(x_ref, b_ref, c_ref, dt_ref, cum_ref, cumt_ref, e_ref, y_ref, st_ref, state):
        c = pl.program_id(1)

        @pl.when(c == 0)
        def _():
            state[...] = jnp.zeros_like(state)

        e = e_ref[0]
        bm = b_ref[...].astype(BF16)
        cm = c_ref[...].astype(BF16)
        cb = lax.dot_general(cm, bm, _NT, preferred_element_type=F32)
        rep_cum = _spread(cum_ref[...], e, 3)
        rep_dt = _spread(dt_ref[...], e, 2)
        lo = lax.broadcasted_iota(jnp.int32, (L, LANE), 1) < P
        lo1 = lax.broadcasted_iota(jnp.int32, (1, LANE), 1) < P
        top = lax.broadcasted_iota(jnp.int32, (LANE, LANE), 0) < P
        for p in range(PAIRS):
            t0, t1 = 2 * p * LANE, (2 * p + 1) * LANE
            cc0, cc1 = rep_cum[:, t0:t0 + LANE], rep_cum[:, t1:t1 + LANE]
            ccp = jnp.where(lo, cc0, cc1)
            cl0, cl1 = cc0[L - 1:L, :], cc1[L - 1:L, :]
            clp = jnp.where(lo1, cl0, cl1)
            xdt = x_ref[:, p * LANE:(p + 1) * LANE] * jnp.where(lo, rep_dt[:, t0:t0 + LANE], rep_dt[:, t1:t1 + LANE])
            xdb = xdt.astype(BF16)
            ys = []
            for r, cc in ((2 * p, cc0), (2 * p + 1, cc1)):
                m = (cb * _decay_pair(cc, cumt_ref[0, r:r + 1, :], False)).astype(BF16)
                ys.append(jnp.dot(m, xdb, preferred_element_type=F32))
            st = state[p * LANE:(p + 1) * LANE, :]
            st_ref[0, 0, p * LANE:(p + 1) * LANE, :] = st
            yoff = lax.dot_general(cm, st.astype(BF16), _NT, preferred_element_type=F32) * jnp.exp(ccp)
            y_ref[:, p * LANE:(p + 1) * LANE] = jnp.where(lo, ys[0], ys[1]) + yoff
            wend = jnp.exp(clp - ccp)
            ecl = jnp.where(top, jnp.exp(cl0), jnp.exp(cl1))
            state[p * LANE:(p + 1) * LANE, :] = st * ecl + lax.dot_general(
                (xdt * wend).astype(BF16), bm, _TN, preferred_element_type=F32)

    return pl.pallas_call(
        body, name="ssd_fwd",
        out_shape=(jax.ShapeDtypeStruct((S, D_INNER), F32), jax.ShapeDtypeStruct((SSM_GROUPS, nc, GW, D_STATE), F32)),
        grid=(SSM_GROUPS, nc),
        in_specs=[pl.BlockSpec((L, GW), lambda g, c: (c, g)),
                  pl.BlockSpec((L, D_STATE), lambda g, c: (c, boff + g)),
                  pl.BlockSpec((L, D_STATE), lambda g, c: (c, boff + SSM_GROUPS + g)),
                  pl.BlockSpec((L, LANE), lambda g, c: (c, 0)),
                  pl.BlockSpec((L, LANE), lambda g, c: (c, 0)),
                  pl.BlockSpec((1, HEADS_PER_GROUP, L), lambda g, c: (g, 0, c)),
                  pl.BlockSpec((1, LANE, SPREAD_W), lambda g, c: (g, 0, 0))],
        out_specs=(pl.BlockSpec((L, GW), lambda g, c: (c, g)),
                   pl.BlockSpec((1, 1, GW, D_STATE), lambda g, c: (g, c, 0, 0))),
        scratch_shapes=[pltpu.VMEM((GW, D_STATE), F32)],
        compiler_params=_params(("parallel", "arbitrary")),
    )(xbc_c, xbc_c, xbc_c, dt, cum, cumt_g, spread)


def _ssd_bwd(xbc_c, dt, cum, cumt_g, spread, states, dy, d_skip):
    S = xbc_c.shape[0]
    nc = S // CHUNK
    L = CHUNK
    boff = D_INNER // D_STATE
    rev = lambda c: nc - 1 - c

    def body(x_ref, b_ref, c_ref, dt_ref, cum_ref, cumt_ref, e_ref, st_ref, dy_ref, skip_ref,
             dx_ref, db_ref, dc_ref, ddt_ref, dcum_ref, dstate):
        c = pl.program_id(1)

        @pl.when(c == 0)
        def _():
            dstate[...] = jnp.zeros_like(dstate)

        e = e_ref[0]
        bf = b_ref[...]
        bm = bf.astype(BF16)
        cm = c_ref[...].astype(BF16)
        cb = lax.dot_general(cm, bm, _NT, preferred_element_type=F32)
        cbt = lax.dot_general(bm, cm, _NT, preferred_element_type=F32)
        rep_cum = _spread(cum_ref[...], e, 3)
        rep_dt = _spread(dt_ref[...], e, 2)
        lane = lax.broadcasted_iota(jnp.int32, (L, LANE), 1)
        lo = lane < P
        lo1 = lax.broadcasted_iota(jnp.int32, (1, LANE), 1) < P
        top = lax.broadcasted_iota(jnp.int32, (LANE, LANE), 0) < P
        last = lax.broadcasted_iota(jnp.int32, (L, LANE), 0) == L - 1
        dcb = jnp.zeros((L, L), F32)
        dcbt = jnp.zeros((L, L), F32)
        dbs = jnp.zeros((L, D_STATE), F32)
        dcs = jnp.zeros((L, D_STATE), F32)
        zs, zds = [], []
        for p in range(PAIRS):
            sl = slice(p * LANE, (p + 1) * LANE)
            t0, t1 = 2 * p * LANE, (2 * p + 1) * LANE
            cc0, cc1 = rep_cum[:, t0:t0 + LANE], rep_cum[:, t1:t1 + LANE]
            ccp = jnp.where(lo, cc0, cc1)
            cl0, cl1 = cc0[L - 1:L, :], cc1[L - 1:L, :]
            w0, w1 = jnp.exp(cl0 - cc0), jnp.exp(cl1 - cc1)
            wend = jnp.where(lo, w0, w1)
            ecc = jnp.exp(ccp)
            ecl0, ecl1 = jnp.exp(cl0), jnp.exp(cl1)
            dtp = jnp.where(lo, rep_dt[:, t0:t0 + LANE], rep_dt[:, t1:t1 + LANE])
            xp = x_ref[:, sl]
            xdt = xp * dtp
            xdb = xdt.astype(BF16)
            dyp = dy_ref[:, sl]
            st = st_ref[0, 0, sl, :]
            stb = st.astype(BF16)
            ds = dstate[sl, :]
            dsb = ds.astype(BF16)
            yoff = lax.dot_general(cm, stb, _NT, preferred_element_type=F32) * ecc
            dye = (dyp * ecc).astype(BF16)
            dcs = dcs + jnp.dot(dye, stb, preferred_element_type=F32)
            dstate[sl, :] = jnp.where(top, ecl0, ecl1) * ds + lax.dot_general(dye, cm, _TN, preferred_element_type=F32)
            dxd = lax.dot_general(bm, dsb, _NT, preferred_element_type=F32) * wend
            sst = ds * st
            dyo = dyp * yoff
            mts = []
            for r, cc, w, ecl, keep, keep_rows in ((2 * p, cc0, w0, ecl0, lo, top), (2 * p + 1, cc1, w1, ecl1, ~lo, ~top)):
                cr = cumt_ref[0, r:r + 1, :]
                decay = _decay_pair(cc, cr, False)
                decay_t = _decay_pair(cc, cr, True)
                m = cb * decay
                mt = cbt * decay_t
                dyr = jnp.where(keep, dyp, 0.0).astype(BF16)
                g = lax.dot_general(dyr, xdb, _NT, preferred_element_type=F32)
                gt = lax.dot_general(xdb, dyr, _NT, preferred_element_type=F32)
                q = g * m
                qt = gt * mt
                dcb = dcb + g * decay
                dcbt = dcbt + gt * decay_t
                mts.append(jnp.dot(mt.astype(BF16), dyr, preferred_element_type=F32))
                t = jnp.dot(jnp.where(keep, xdt, 0.0).astype(BF16), dsb, preferred_element_type=F32)
                dbs = dbs + t * w
                tb = t * bf * w
                end_row = _colsum(tb) + ecl * _colsum(jnp.where(keep_rows, sst, 0.0))
                z = (q[:, 0:LANE] + q[:, LANE:2 * LANE]) - (qt[:, 0:LANE] + qt[:, LANE:2 * LANE])
                z = z + jnp.where(keep, dyo, 0.0) - tb + jnp.where(last, end_row, 0.0)
                zs.append(z)
            dxd = dxd + mts[0] + mts[1]
            dx_ref[:, sl] = dxd * dtp + dyp * skip_ref[:, sl]
            zd = dxd * xp
            zds.append(jnp.where(lo, zd, 0.0))
            zds.append(jnp.where(lo, 0.0, zd))
        dc_ref[...] = dcs + jnp.dot(dcb.astype(BF16), bm, preferred_element_type=F32)
        db_ref[...] = dbs + jnp.dot(dcbt.astype(BF16), cm, preferred_element_type=F32)
        dcum_ref[0] = _gather_rows(jnp.concatenate(zs, axis=1), e)
        ddt_ref[0] = _gather_rows(jnp.concatenate(zds, axis=1), e)

    return pl.pallas_call(
        body, name="ssd_bwd",
        out_shape=(jax.ShapeDtypeStruct((S, D_INNER), F32),
                   jax.ShapeDtypeStruct((S, SSM_GROUPS * D_STATE), F32),
                   jax.ShapeDtypeStruct((S, SSM_GROUPS * D_STATE), F32),
                   jax.ShapeDtypeStruct((SSM_GROUPS, S, LANE), F32),
                   jax.ShapeDtypeStruct((SSM_GROUPS, S, LANE), F32)),
        grid=(SSM_GROUPS, nc),
        in_specs=[pl.BlockSpec((L, GW), lambda g, c: (rev(c), g)),
                  pl.BlockSpec((L, D_STATE), lambda g, c: (rev(c), boff + g)),
                  pl.BlockSpec((L, D_STATE), lambda g, c: (rev(c), boff + SSM_GROUPS + g)),
                  pl.BlockSpec((L, LANE), lambda g, c: (rev(c), 0)),
                  pl.BlockSpec((L, LANE), lambda g, c: (rev(c), 0)),
                  pl.BlockSpec((1, HEADS_PER_GROUP, L), lambda g, c: (g, 0, rev(c))),
                  pl.BlockSpec((1, LANE, SPREAD_W), lambda g, c: (g, 0, 0)),
                  pl.BlockSpec((1, 1, GW, D_STATE), lambda g, c: (g, rev(c), 0, 0)),
                  pl.BlockSpec((L, GW), lambda g, c: (rev(c), g)),
                  pl.BlockSpec((1, GW), lambda g, c: (0, g))],
        out_specs=(pl.BlockSpec((L, GW), lambda g, c: (rev(c), g)),
                   pl.BlockSpec((L, D_STATE), lambda g, c: (rev(c), g)),
                   pl.BlockSpec((L, D_STATE), lambda g, c: (rev(c), g)),
                   pl.BlockSpec((1, L, LANE), lambda g, c: (g, rev(c), 0)),
                   pl.BlockSpec((1, L, LANE), lambda g, c: (g, rev(c), 0))),
        scratch_shapes=[pltpu.VMEM((GW, D_STATE), F32)],
        compiler_params=_params(("parallel", "arbitrary")),
    )(xbc_c, xbc_c, xbc_c, dt, cum, cumt_g, spread, states, dy, d_skip)


def _dt_bwd(dt_raw, dt_bias, a_log, ddt_x, dcum):
    S = dt_raw.shape[0]
    n = S // CHUNK

    def body(raw_ref, ddx_ref, dcu_ref, bias_ref, alog_ref, draw_ref, gb_ref, ga_ref):
        i = pl.program_id(0)

        @pl.when(i == 0)
        def _():
            gb_ref[...] = jnp.zeros_like(gb_ref)
            ga_ref[...] = jnp.zeros_like(ga_ref)

        ddx, dcu = ddx_ref[0], dcu_ref[0]
        for g in range(1, SSM_GROUPS):
            ddx = ddx + ddx_ref[g]
            dcu = dcu + dcu_ref[g]
        xx = raw_ref[...] + bias_ref[...]
        dt = _softplus(xx)
        a = -jnp.exp(alog_ref[...])
        dda = _revcumsum_rows(dcu)
        lane = lax.broadcasted_iota(jnp.int32, xx.shape, 1)
        draw = jnp.where(lane < N_HEADS_SSM, (ddx + dda * a) * _sigmoid(xx), 0.0)
        draw_ref[...] = draw.astype(draw_ref.dtype)
        gb_ref[...] += _colsum(draw)
        ga_ref[...] += _colsum(dda * dt) * a

    row = pl.BlockSpec((CHUNK, LANE), lambda i: (i, 0))
    grp = pl.BlockSpec((SSM_GROUPS, CHUNK, LANE), lambda i: (0, i, 0))
    one = pl.BlockSpec((1, LANE), lambda i: (0, 0))
    return pl.pallas_call(
        body, name="dt_bwd",
        out_shape=(jax.ShapeDtypeStruct((S, LANE), BF16), jax.ShapeDtypeStruct((1, LANE), F32), jax.ShapeDtypeStruct((1, LANE), F32)),
        grid=(n,),
        in_specs=[row, grp, grp, one, one],
        out_specs=(row, one, one),
        compiler_params=_params(("arbitrary",)),
    )(dt_raw, ddt_x, dcum, dt_bias, a_log)


def _adamw(w, g, m, v, name):
    shape = w.shape
    cols = shape[-1]
    rows = int(np.prod(shape[:-1]))
    w2, g2, m2, v2 = (t.reshape(1, rows, cols) for t in (w, g, m, v))
    tr = rows if rows * cols <= 512 * 1024 else _tile(rows, max(SUBLANE, (512 * 1024 // cols) // SUBLANE * SUBLANE), SUBLANE)
    c1 = 1.0 - ADAM_B1 ** ADAM_STEP
    c2 = 1.0 - ADAM_B2 ** ADAM_STEP

    def body(w_ref, g_ref, m_ref, v_ref, d_ref, mo_ref, vo_ref):
        gv = g_ref[...]
        mn = ADAM_B1 * m_ref[...] + (1.0 - ADAM_B1) * gv
        vn = ADAM_B2 * v_ref[...] + (1.0 - ADAM_B2) * (gv * gv)
        d_ref[...] = -ADAM_LR * ((mn / c1) / (jnp.sqrt(vn / c2) + ADAM_EPS) + ADAM_WD * w_ref[...])
        mo_ref[...] = mn
        vo_ref[...] = vn

    spec = pl.BlockSpec((1, tr, cols), lambda i: (0, i, 0))
    outs = pl.pallas_call(
        body, name=name,
        out_shape=tuple(jax.ShapeDtypeStruct((1, rows, cols), F32) for _ in range(3)),
        grid=(rows // tr,),
        in_specs=[spec] * 4, out_specs=(spec,) * 3,
        compiler_params=_params(("parallel",)),
    )(w2, g2, m2, v2)
    return tuple(o.reshape(shape) for o in outs)


def _prep_weights(w_in, w_uq):
    offs = np.cumsum((0,) + IN_SPLITS)
    pad = lambda t: jnp.pad(t, ((0, 0), (0, LANE - t.shape[1])))
    pieces = dict(
        qkv=w_in[:, offs[0]:offs[2]],
        kr=pad(w_in[:, offs[2]:offs[3]]),
        z=w_in[:, offs[3]:offs[4]],
        xbc=w_in[:, offs[4]:offs[5]],
        dt=pad(w_in[:, offs[5]:offs[6]]),
        g=w_in[:, offs[6]:offs[8]],
    )
    uq = w_uq.reshape(Q_LORA, N_HEADS_MLA, QK_DIM)
    uq = jnp.pad(uq, ((0, 0), (0, 0), (0, HEAD_PAD - QK_DIM))).reshape(Q_LORA, N_HEADS_MLA * HEAD_PAD)
    return pieces, uq


def _local_step(x, p, positions, ex, sp, target):
    W = gw = ex
    S = x.shape[0]
    tile = min(ATTN_TILE, S)
    pos_col = positions.reshape(S, 1)
    invf = ROPE_THETA ** (-jnp.arange(0, QK_ROPE, 2, dtype=F32) / QK_ROPE)
    invf = jnp.pad(jnp.concatenate([invf, invf]), (0, LANE - QK_ROPE)).reshape(1, LANE)
    wp, w_uq_p = _prep_weights(W["w_in"], W["w_uq"])
    padl = lambda t: jnp.pad(t, ((0, 0), (0, LANE - t.shape[1])))
    dt_bias_p, a_log_p = padl(sp["dt_bias"]), padl(sp["a_log"])
    dskip_ch = jnp.repeat(sp["d_skip"], SSM_HEADDIM, axis=1)
    p_bf = p.astype(BF16)
    RW = 256

    (u_bf,) = _rows(lambda i, n, x, g: (x * _rstd(x) * g,), S, RW, [_row(x), _full(sp["mix_norm_pre"])],
                    [("row", D_MODEL, BF16)], "norm_pre")
    cqkv = ex.mm(u_bf, wp["qkv"], name="mm_qkv")
    z = ex.mm(u_bf, wp["z"], name="mm_z")
    xbc = ex.mm(u_bf, wp["xbc"], name="mm_xbc")
    gates = ex.mm(u_bf, wp["g"], name="mm_gates")
    kr_pad = ex.mm(u_bf, wp["kr"], name="mm_kr")
    dt_raw = ex.mm(u_bf, wp["dt"], name="mm_dt")

    def qkv_norm(i, n, cq, ckv, gq, gkv):
        return cq * _rstd(cq) * gq, ckv * _rstd(ckv) * gkv

    cqn, ckvn = _rows(qkv_norm, S, 512, [_row(cqkv, Q_LORA, 0), _row(cqkv, KV_LORA, 1), _full(sp["q_norm"]), _full(sp["kv_norm"])],
                      [("row", Q_LORA, BF16), ("row", KV_LORA, BF16)], "qkv_norm")
    q_raw = ex.mm(cqn, w_uq_p, name="mm_uq")
    kv = ex.mm(ckvn, W["w_ukv"], out_dtype=BF16, name="mm_ukv")
    q_bf, kp_bf = _rope_fwd(q_raw, kr_pad, pos_col, invf)
    attn, lse = _attn_fwd(q_bf, kv, kp_bf, tile)

    xbc_c = _conv_fwd(xbc, sp["conv_w"], sp["conv_b"])
    dt, cum, cumt = _dt_prep(dt_raw, dt_bias_p, a_log_p)
    cumt_g = cumt[:N_HEADS_SSM].reshape(SSM_GROUPS, HEADS_PER_GROUP, S)
    spread = _spread_matrix()
    y, states = _ssd_fwd(xbc_c, dt, cum, cumt_g, spread)

    GN = D_INNER // SSM_GROUPS

    def gated(y, xs, z, dsk):
        yt = y + dsk * xs
        sz = _sigmoid(z)
        return yt, sz, yt * (z * sz)

    def gated_norm(i, n, y, xs, z, dsk, gn):
        _, _, yg = gated(y, xs, z, dsk)
        parts = []
        for g in range(SSM_GROUPS):
            blk = yg[:, g * GN:(g + 1) * GN]
            parts.append(blk * _rstd(blk) * gn[:, g * GN:(g + 1) * GN])
        return (jnp.concatenate(parts, axis=1),)

    (ssm,) = _rows(gated_norm, S, 128, [_row(y), _row(xbc_c, D_INNER, 0), _row(z), _full(dskip_ch), _full(sp["ssm_norm"])],
                   [("row", D_INNER, BF16)], "gated_norm")

    a_o = ex.mm(attn, W["w_attn_o"], name="mm_attn_o")
    b_o = ex.mm(ssm, W["w_ssm_o"], name="mm_ssm_o")

    def mix(i, n, ga, gs, a, b):
        return (_sigmoid(ga) * a + _sigmoid(gs) * b,)

    (mixed,) = _rows(mix, S, RW, [_row(gates, D_MODEL, 0), _row(gates, D_MODEL, 1), _row(a_o), _row(b_o)],
                     [("row", D_MODEL, BF16)], "mix")
    m2 = ex.mm(mixed, W["w_out"], name="mm_out")

    def post(i, n, h, m, gpost, gpre):
        hn = h + m * _rstd(m) * gpost
        return hn, hn * _rstd(hn) * gpre

    h1, f_bf = _rows(post, S, RW, [_row(x), _row(m2), _full(sp["mix_norm_post"]), _full(sp["ffn_norm_pre"])],
                     [("row", D_MODEL, F32), ("row", D_MODEL, BF16)], "post_mix")
    ga = ex.mm(f_bf, W["w_gate"], name="mm_gate")
    up = ex.mm(f_bf, W["w_up"], name="mm_up")
    (s_bf,) = _rows(lambda i, n, a, b: (a * _sigmoid(a) * b,), S, RW, [_row(ga), _row(up)], [("row", D_FF, BF16)], "swiglu")
    f2 = ex.mm(s_bf, W["w_down"], name="mm_down")
    h2, n3_bf = _rows(post, S, RW, [_row(h1), _row(f2), _full(sp["ffn_norm_post"]), _full(sp["ple_norm_pre"])],
                      [("row", D_MODEL, F32), ("row", D_MODEL, BF16)], "post_ffn")
    gpre = ex.mm(n3_bf, W["w_ple_gate"], name="mm_ple_gate")
    pe = ex.mm(p_bf, W["w_ple"], name="mm_ple")

    def ple_loss(i, n, h2, gpre, pe, tgt, gpost):
        gate = _sigmoid(gpre)
        e = pe * gate
        r = _rstd(e)
        diff = h2 + e * r * gpost - tgt
        loss = 0.5 * jnp.sum(jnp.mean(diff * diff, axis=1, keepdims=True))
        dh3 = diff * (1.0 / D_MODEL)
        de, dg_rows = _norm_bwd(e, r, gpost, dh3)
        return (jnp.full((1, LANE), loss, F32), dh3, de * gate, de * pe * gate * (1.0 - gate), _colsum(dg_rows))

    loss, dh3, dpe, dgpre, g_ple_post = _rows(
        ple_loss, S, 128, [_row(h2), _row(gpre), _row(pe), _row(target), _full(sp["ple_norm_post"])],
        [("acc", (1, LANE), F32), ("row", D_MODEL, F32), ("row", D_MODEL, BF16), ("row", D_MODEL, BF16),
         ("acc", (1, D_MODEL), F32)], "ple_loss")

    gs = {"ple_norm_post": g_ple_post}
    gw["w_ple"] = ex.mm(p_bf, dpe, ta=True, name="mmg_ple")
    gw["w_ple_gate"] = ex.mm(n3_bf, dgpre, ta=True, name="mmg_ple_gate")
    dn3 = ex.mm(dgpre, W["w_ple_gate"], tb=True, name="mmb_ple_gate")

    def post_bwd(i, n, h, m, dhn, dn, gpost, gpre):
        rm = _rstd(m)
        hn = h + m * rm * gpost
        dx, dgpre_rows = _norm_bwd(hn, _rstd(hn), gpre, dn)
        dhn_t = dhn + dx
        dm, dgpost_rows = _norm_bwd(m, rm, gpost, dhn_t)
        return dhn_t, dm, _colsum(dgpre_rows), _colsum(dgpost_rows)

    def run_post_bwd(h, m, dhn, dn, gpost, gpre, name):
        return _rows(post_bwd, S, 128, [_row(h), _row(m), _row(dhn), _row(dn), _full(gpost), _full(gpre)],
                     [("row", D_MODEL, F32), ("row", D_MODEL, BF16), ("acc", (1, D_MODEL), F32), ("acc", (1, D_MODEL), F32)], name)

    dh2, df2, gs["ple_norm_pre"], gs["ffn_norm_post"] = run_post_bwd(
        h1, f2, dh3, dn3, sp["ffn_norm_post"], sp["ple_norm_pre"], "post_ffn_bwd")
    gw["w_down"] = ex.mm(s_bf, df2, ta=True, name="mmg_down")
    ds = ex.mm(df2, W["w_down"], tb=True, out_dtype=BF16, name="mmb_down")

    def swiglu_bwd(i, n, a, b, ds):
        sa = _sigmoid(a)
        return ds * b * (sa * (1.0 + a * (1.0 - sa))), ds * (a * sa)

    dga, dup = _rows(swiglu_bwd, S, RW, [_row(ga), _row(up), _row(ds)], [("row", D_FF, BF16), ("row", D_FF, BF16)], "swiglu_bwd")
    gw["w_gate"] = ex.mm(f_bf, dga, ta=True, name="mmg_gate")
    gw["w_up"] = ex.mm(f_bf, dup, ta=True, name="mmg_up")
    df = ex.mm(dga, W["w_gate"], tb=True, name="mmb_gate")
    df = ex.mm(dup, W["w_up"], tb=True, add=df, name="mmb_up")
    dh1, dm2, gs["ffn_norm_pre"], gs["mix_norm_post"] = run_post_bwd(
        x, m2, dh2, df, sp["mix_norm_post"], sp["ffn_norm_pre"], "post_mix_bwd")
    gw["w_out"] = ex.mm(mixed, dm2, ta=True, name="mmg_out")
    dmixed = ex.mm(dm2, W["w_out"], tb=True, out_dtype=BF16, name="mmb_out")

    def mix_bwd(i, n, ga, gs_, a, b, dm):
        sa, ss = _sigmoid(ga), _sigmoid(gs_)
        return dm * sa, dm * ss, jnp.concatenate([dm * a * sa * (1.0 - sa), dm * b * ss * (1.0 - ss)], axis=1)

    da_o, db_o, dgates = _rows(mix_bwd, S, RW, [_row(gates, D_MODEL, 0), _row(gates, D_MODEL, 1), _row(a_o), _row(b_o), _row(dmixed)],
                               [("row", D_MODEL, BF16), ("row", D_MODEL, BF16), ("row", 2 * D_MODEL, BF16)], "mix_bwd")
    gw["w_attn_o"] = ex.mm(attn, da_o, ta=True, name="mmg_attn_o")
    dattn = ex.mm(da_o, W["w_attn_o"], tb=True, out_dtype=BF16, name="mmb_attn_o")
    gw["w_ssm_o"] = ex.mm(ssm, db_o, ta=True, name="mmg_ssm_o")
    dssm = ex.mm(db_o, W["w_ssm_o"], tb=True, out_dtype=BF16, name="mmb_ssm_o")

    delta = _attn_delta(attn, dattn, tile)
    dq, dkv, dkp = _attn_bwd(q_bf, kv, kp_bf, dattn, lse, delta, tile)
    dq_raw, dkr = _rope_bwd(dq, dkp, pos_col, invf)
    g_uq_p = ex.mm(cqn, dq_raw, ta=True, name="mmg_uq")
    gw["w_uq"] = g_uq_p.reshape(Q_LORA, N_HEADS_MLA, HEAD_PAD)[:, :, :QK_DIM].reshape(Q_LORA, N_HEADS_MLA * QK_DIM)
    dcqn = ex.mm(dq_raw, w_uq_p, tb=True, name="mmb_uq")
    gw["w_ukv"] = ex.mm(ckvn, dkv, ta=True, name="mmg_ukv")
    dckvn = ex.mm(dkv, W["w_ukv"], tb=True, name="mmb_ukv")

    def qkv_norm_bwd(i, n, cq, ckv, dq_, dkv_, gq, gkv):
        dcq, gq_rows = _norm_bwd(cq, _rstd(cq), gq, dq_)
        dckv, gkv_rows = _norm_bwd(ckv, _rstd(ckv), gkv, dkv_)
        return jnp.concatenate([dcq, dckv], axis=1), _colsum(gq_rows), _colsum(gkv_rows)

    dcqkv, gs["q_norm"], gs["kv_norm"] = _rows(
        qkv_norm_bwd, S, 512, [_row(cqkv, Q_LORA, 0), _row(cqkv, KV_LORA, 1), _row(dcqn), _row(dckvn), _full(sp["q_norm"]), _full(sp["kv_norm"])],
        [("row", Q_LORA + KV_LORA, BF16), ("acc", (1, Q_LORA), F32), ("acc", (1, KV_LORA), F32)], "qkv_norm_bwd")

    def gated_norm_bwd(i, n, y, xs, z, dssm, dsk, gn):
        yt, sz, yg = gated(y, xs, z, dsk)
        dyg_parts, gn_parts = [], []
        for g in range(SSM_GROUPS):
            sl = slice(g * GN, (g + 1) * GN)
            blk = yg[:, sl]
            dblk, rows = _norm_bwd(blk, _rstd(blk), gn[:, sl], dssm[:, sl])
            dyg_parts.append(dblk)
            gn_parts.append(_colsum(rows))
        dyg = jnp.concatenate(dyg_parts, axis=1)
        dyt = dyg * (z * sz)
        dz = dyg * yt * (sz * (1.0 + z * (1.0 - sz)))
        return dyt, dz, jnp.concatenate(gn_parts, axis=1), _colsum(dyt * xs)

    dy, dz, gs["ssm_norm"], g_dskip_ch = _rows(
        gated_norm_bwd, S, 128, [_row(y), _row(xbc_c, D_INNER, 0), _row(z), _row(dssm), _full(dskip_ch), _full(sp["ssm_norm"])],
        [("row", D_INNER, F32), ("row", D_INNER, BF16), ("acc", (1, D_INNER), F32), ("acc", (1, D_INNER), F32)],
        "gated_norm_bwd")
    gs["d_skip"] = jnp.sum(g_dskip_ch.reshape(N_HEADS_SSM, SSM_HEADDIM), axis=1).reshape(1, N_HEADS_SSM)
    dxs, dbm, dcm, ddt_x, dcum = _ssd_bwd(xbc_c, dt, cum, cumt_g, spread, states, dy, dskip_ch)
    ddt_raw, g_dtb, g_alog = _dt_bwd(dt_raw, dt_bias_p, a_log_p, ddt_x, dcum)
    gs["dt_bias"] = g_dtb[:, :N_HEADS_SSM]
    gs["a_log"] = g_alog[:, :N_HEADS_SSM]
    dxbc, g_conv_w8, gs["conv_b"] = _conv_bwd(xbc, [dxs, dbm, dcm], sp["conv_w"], sp["conv_b"])
    gs["conv_w"] = g_conv_w8[:CONV_WIDTH]

    g_qkv = ex.mm(u_bf, dcqkv, ta=True, name="mmg_qkv")
    g_kr = ex.mm(u_bf, dkr, ta=True, name="mmg_kr")
    g_z = ex.mm(u_bf, dz, ta=True, name="mmg_z")
    g_xbc = ex.mm(u_bf, dxbc, ta=True, name="mmg_xbc")
    g_dt = ex.mm(u_bf, ddt_raw, ta=True, name="mmg_dt")
    g_g = ex.mm(u_bf, dgates, ta=True, name="mmg_gates")
    gw["w_in"] = [g_qkv, g_kr[:, :QK_ROPE], g_z, g_xbc, g_dt[:, :N_HEADS_SSM], g_g]
    du = ex.mm(dcqkv, wp["qkv"], tb=True, name="mmb_qkv")
    du = ex.mm(dkr, wp["kr"], tb=True, add=du, name="mmb_kr")
    du = ex.mm(ddt_raw, wp["dt"], tb=True, add=du, name="mmb_dt")
    du = ex.mm(dz, wp["z"], tb=True, add=du, name="mmb_z")
    du = ex.mm(dxbc, wp["xbc"], tb=True, add=du, name="mmb_xbc")
    du = ex.mm(dgates, wp["g"], tb=True, add=du, name="mmb_gates")

    def pre_bwd(i, n, x, du, dh, g):
        dx, rows = _norm_bwd(x, _rstd(x), g, du)
        return dh + dx, _colsum(rows)

    grad_x, gs["mix_norm_pre"] = _rows(pre_bwd, S, RW, [_row(x), _row(du), _row(dh1), _full(sp["mix_norm_pre"])],
                                       [("row", D_MODEL, F32), ("acc", (1, D_MODEL), F32)], "norm_pre_bwd")
    return loss, grad_x, gs


BIG = (
    ("w_in", (2048, 3872), 1), ("w_uq", (512, 768), 1), ("w_ukv", (512, 1024), 1), ("w_attn_o", (512, 2048), 0),
    ("w_ssm_o", (1024, 2048), 0), ("w_out", (512, 2048), 0), ("w_gate", (2048, 1408), 1), ("w_up", (2048, 1408), 1),
    ("w_down", (1408, 2048), 0), ("w_ple_gate", (512, 2048), 0), ("w_ple", (256, 512), 1),
)
SMALL = (
    ("mix_norm_pre", 2048), ("mix_norm_post", 2048), ("q_norm", 512), ("kv_norm", 512), ("conv_b", 6144), ("dt_bias", 64),
    ("a_log", 64), ("d_skip", 64), ("ssm_norm", 4096), ("ffn_norm_pre", 2048), ("ffn_norm_post", 2048),
    ("ple_norm_pre", 2048), ("ple_norm_post", 2048),
)
CONV_W_LEN = CONV_WIDTH * CONV_DIM
SMALL_ROWS = 384


def _place():
    return lax.axis_index("x"), lax.axis_index("y"), lax.axis_index("c")


def _flip(v, bit):
    return 1 - v if bit else v


def _alone(hook, name):
    n_in, n_out = len(hook.ins), len(hook.out_shapes)

    def body(*refs):
        start, finish = hook.make(refs[:n_in], refs[n_in:n_in + n_out], refs[n_in + n_out:])
        start()
        finish()

    return list(pl.pallas_call(
        body, name=name, out_shape=tuple(hook.out_shapes),
        in_specs=[pl.BlockSpec(memory_space=pl.ANY)] * n_in,
        out_specs=tuple(pl.BlockSpec(memory_space=pl.ANY) for _ in range(n_out)),
        scratch_shapes=[pltpu.SemaphoreType.DMA((s,)) for s in hook.sems],
        input_output_aliases=hook.aliases,
    )(*hook.ins))


def _simple(copies):
    def start():
        for cp in copies:
            cp.start()

    def finish():
        for cp in copies:
            cp.wait()

    return start, finish


def _gather_hook(shards):
    n = len(shards)

    def make(ins, outs, sems):
        send_sems, recv_sems, fwd_send_sems, fwd_recv_sems = sems
        x, y, c = _place()
        me = 2 * x + y
        far, near = [], []
        for a in range(n):
            half = shards[a].shape[0] // 2
            lo = pl.multiple_of(c * half, SUBLANE)
            for k in (1, 2, 3):
                px, py = _flip(x, k >> 1), _flip(y, k & 1)
                far.append(pltpu.make_async_remote_copy(
                    src_ref=ins[a].at[pl.ds(lo, half), :], dst_ref=outs[a].at[me, pl.ds(lo, half), :],
                    send_sem=send_sems.at[3 * a + k - 1], recv_sem=recv_sems.at[3 * a + k - 1],
                    device_id=(px, py, c), device_id_type=MESH_ID))
                got = outs[a].at[2 * px + py, pl.ds(lo, half), :]
                near.append(pltpu.make_async_remote_copy(
                    src_ref=got, dst_ref=got, send_sem=fwd_send_sems.at[3 * a + k - 1], recv_sem=fwd_recv_sems.at[3 * a + k - 1],
                    device_id=(x, y, 1 - c), device_id_type=MESH_ID))

        def start():
            for cp in far:
                cp.start()

        def finish():
            for cp, fwd in zip(far, near):
                cp.wait_recv()
                fwd.start()
            for cp, fwd in zip(far, near):
                cp.wait_send()
                fwd.wait()

        return start, finish

    return _Hook(shards, [jax.ShapeDtypeStruct((N_CHIPS, *s.shape), s.dtype) for s in shards], (3 * n,) * 4, make)


def _swap_hook(gs):
    n = len(gs)

    def make(ins, outs, sems):
        send_sems, recv_sems = sems
        x, y, c = _place()
        copies = []
        for a in range(n):
            half = gs[a].shape[1] // 2
            src = ins[a].at[:, pl.ds(pl.multiple_of((1 - c) * half, SUBLANE), half), :]
            copies.append(pltpu.make_async_remote_copy(
                src_ref=src, dst_ref=outs[a], send_sem=send_sems.at[a], recv_sem=recv_sems.at[a],
                device_id=(x, y, 1 - c), device_id_type=MESH_ID))
        return _simple(copies)

    return _Hook(gs, [jax.ShapeDtypeStruct((g.shape[0], g.shape[1] // 2, g.shape[2]), g.dtype) for g in gs], (n, n), make)


def _sum_rows_tile(rows, cols):
    return _tile(rows, max(2 * SUBLANE, (512 * 1024 // cols) // (2 * SUBLANE) * (2 * SUBLANE)), 2 * SUBLANE)


def _add_half(g, other, c, name):
    n, R, C = g.shape
    half = R // 2
    tr = _sum_rows_tile(half, C)
    nb = half // tr

    def body(c_ref, g_ref, o_ref, out_ref):
        out_ref[...] = (g_ref[...] + o_ref[...]).astype(out_ref.dtype)

    return pl.pallas_call(
        body, name=name,
        out_shape=jax.ShapeDtypeStruct((n, half, C), BF16),
        grid_spec=pltpu.PrefetchScalarGridSpec(
            num_scalar_prefetch=1, grid=(n, nb),
            in_specs=[pl.BlockSpec((1, tr, C), lambda j, i, c_ref: (j, c_ref[0] * nb + i, 0)),
                      pl.BlockSpec((1, tr, C), lambda j, i, c_ref: (j, i, 0))],
            out_specs=pl.BlockSpec((1, tr, C), lambda j, i, c_ref: (j, i, 0))),
        compiler_params=_params(("parallel", "parallel")),
    )(c, g, other)


def _scatter_hook(parts):
    n = len(parts)

    def make(ins, outs, sems):
        send_sems, recv_sems = sems
        x, y, c = _place()
        copies = []
        for a in range(n):
            for k in (1, 2, 3):
                px, py = _flip(x, k >> 1), _flip(y, k & 1)
                copies.append(pltpu.make_async_remote_copy(
                    src_ref=ins[a].at[2 * px + py], dst_ref=outs[a].at[k - 1], send_sem=send_sems.at[3 * a + k - 1],
                    recv_sem=recv_sems.at[3 * a + k - 1], device_id=(px, py, c), device_id_type=MESH_ID))
        return _simple(copies)

    return _Hook(parts, [jax.ShapeDtypeStruct((3, *p.shape[1:]), p.dtype) for p in parts], (3 * n, 3 * n), make)


def _add_chips(part, got, place, name):
    n, R, C = part.shape
    tr = _sum_rows_tile(R, C)
    nb = R // tr

    def body(place_ref, p_ref, g_ref, out_ref):
        out_ref[...] = ((p_ref[0].astype(F32) + g_ref[0].astype(F32)) + g_ref[1].astype(F32)) + g_ref[2].astype(F32)

    return pl.pallas_call(
        body, name=name,
        out_shape=jax.ShapeDtypeStruct((2 * R, C), F32),
        grid_spec=pltpu.PrefetchScalarGridSpec(
            num_scalar_prefetch=1, grid=(nb,),
            in_specs=[pl.BlockSpec((1, tr, C), lambda i, place_ref: (place_ref[0], i, 0)),
                      pl.BlockSpec((3, tr, C), lambda i, place_ref: (0, i, 0))],
            out_specs=pl.BlockSpec((tr, C), lambda i, place_ref: (place_ref[1] * nb + i, 0))),
        compiler_params=_params(("parallel",)),
    )(place, part, got)


def _join_hook(wholes):
    n = len(wholes)

    def make(ins, outs, sems):
        send_sems, recv_sems = sems
        x, y, c = _place()
        copies = []
        for a in range(n):
            half = wholes[a].shape[0] // 2
            rows = outs[a].at[pl.ds(pl.multiple_of(c * half, SUBLANE), half), :]
            copies.append(pltpu.make_async_remote_copy(
                src_ref=rows, dst_ref=rows, send_sem=send_sems.at[a], recv_sem=recv_sems.at[a],
                device_id=(x, y, 1 - c), device_id_type=MESH_ID))
        return _simple(copies)

    return _Hook(wholes, [jax.ShapeDtypeStruct(w.shape, w.dtype) for w in wholes], (n, n), make, aliases={a: a for a in range(n)})


def _allreduce_small(vec, name):
    R, C = vec.shape

    def body(v_ref, o_ref, buf, send_sems, recv_sems):
        x, y, c = _place()
        me = 4 * x + 2 * y + c
        buf[me] = v_ref[...]
        copies = []
        for k in range(1, N_DEV):
            peer = (_flip(x, (k >> 2) & 1), _flip(y, (k >> 1) & 1), _flip(c, k & 1))
            copies.append(pltpu.make_async_remote_copy(
                src_ref=v_ref, dst_ref=buf.at[me], send_sem=send_sems.at[k - 1], recv_sem=recv_sems.at[k - 1],
                device_id=peer, device_id_type=MESH_ID))
        for cp in copies:
            cp.start()
        for cp in copies:
            cp.wait()
        tot = buf[0]
        for d in range(1, N_DEV):
            tot = tot + buf[d]
        o_ref[...] = tot

    return pl.pallas_call(
        body, name=name,
        out_shape=jax.ShapeDtypeStruct((R, C), F32),
        in_specs=[pl.BlockSpec(memory_space=pltpu.VMEM)],
        out_specs=pl.BlockSpec(memory_space=pltpu.VMEM),
        scratch_shapes=[pltpu.VMEM((N_DEV, R, C), F32), pltpu.SemaphoreType.DMA((N_DEV - 1,)), pltpu.SemaphoreType.DMA((N_DEV - 1,))],
    )(vec)


def _unstack(gathered, shape, axis):
    if axis == 0:
        return gathered.reshape(N_CHIPS * shape[0], shape[1])
    return jnp.concatenate([gathered[j] for j in range(N_CHIPS)], axis=1)


def _stack(whole, shape, axis):
    if axis == 0:
        return whole.reshape(N_CHIPS, shape[0], shape[1])
    pieces = whole if isinstance(whole, (list, tuple)) else [whole]
    shards = []
    for j in range(N_CHIPS):
        lo, hi, off, cols = j * shape[1], (j + 1) * shape[1], 0, []
        for p in pieces:
            a, b = max(lo, off), min(hi, off + p.shape[1])
            if a < b:
                cols.append(p[:, a - off:b - off])
            off += p.shape[1]
        shards.append(cols[0] if len(cols) == 1 else jnp.concatenate(cols, axis=1))
    return jnp.stack(shards)


GATHER_FIRST = ("w_in", "w_uq", "w_ukv")
GATHER_IN = {"mm_z": ("w_attn_o", "w_ssm_o", "w_out"), "mm_xbc": ("w_gate", "w_up"), "mm_gates": ("w_down", "w_ple_gate", "w_ple")}
REDUCE = (
    (("w_ple", "w_ple_gate", "w_down"), "mmb_down", "mmg_gate", "mmg_up"),
    (("w_gate", "w_up"), "mmb_gate", "mmb_up", "mmg_out"),
    (("w_out", "w_attn_o", "w_ssm_o"), "mmb_ssm_o", "mmg_z", "mmg_xbc"),
    (("w_uq", "w_ukv"), "mmb_ukv", "mmg_gates", "mmb_z"),
    (("w_in",), "mmb_z", "mmb_xbc", "mmb_gates"),
)


class _Exchange:
    def __init__(self, shards, chip, core):
        self.shards, self.chip = shards, chip
        self.core_arr = core.reshape(1).astype(jnp.int32)
        self.place_arr = jnp.stack([chip, core]).astype(jnp.int32)
        self.shape = {n: (shape, axis) for n, shape, axis in BIG}
        self.whole, self.grads, self.reduced, self.pending, self.tails = {}, {}, {}, {}, 0
        hook, done = self._gather(GATHER_FIRST)
        done(_alone(hook, "gather_first"))
        for host, names in GATHER_IN.items():
            self._arm(host, *self._gather(names))

    def _arm(self, host, hook, done):
        self.pending.setdefault(host, []).append((hook, done))

    def _gather(self, names):
        shards = [self.shards[n].astype(BF16) for n in names]

        def done(outs):
            for n, s, g in zip(names, shards, outs):
                self.whole[n] = _unstack(lax.dynamic_update_slice(g, s[None], (self.chip, 0, 0)), *self.shape[n])

        return _gather_hook(shards), done

    def __getitem__(self, name):
        return self.whole[name]

    def __setitem__(self, name, grad):
        self.grads[name] = grad
        for names, swap_host, scatter_host, join_host in REDUCE:
            if name in names and all(n in self.grads for n in names):
                self._reduce(names, swap_host, scatter_host, join_host)

    def _reduce(self, names, swap_host, scatter_host, join_host):
        stacked = [_stack(self.grads[n], *self.shape[n]) for n in names]

        def joined(outs):
            for n, r in zip(names, outs):
                self.reduced[n] = r.reshape(1, *self.shape[n][0])

        def swapped(outs):
            parts = [_add_half(g, o, self.core_arr, "add_half_" + n) for n, g, o in zip(names, stacked, outs)]

            def scattered(gots):
                wholes = [_add_chips(q, o, self.place_arr, "add_chips_" + n) for n, q, o in zip(names, parts, gots)]
                self._arm(join_host, _join_hook(wholes), joined)

            self._arm(scatter_host, _scatter_hook(parts), scattered)

        self._arm(swap_host, _swap_hook(stacked), swapped)

    def _run(self, todo, call):
        hook = _merge_hooks([h for h, _ in todo])
        result, outs = call(hook)
        off = 0
        for h, done in todo:
            done(outs[off:off + len(h.out_shapes)])
            off += len(h.out_shapes)
        return result

    def mm(self, a, b, *, name, **kw):
        todo = self.pending.pop(name, None)
        if not todo:
            return _mm(a, b, name=name, **kw)
        return self._run(todo, lambda hook: _mm(a, b, name=name, hook=hook, **kw))

    def finish(self):
        while self.pending:
            todo = self.pending.pop(next(iter(self.pending)))
            self.tails += 1
            self._run(todo, lambda hook: (None, _alone(hook, "exchange_tail_%d" % self.tails)))
        return self.reduced


def kernel(x, p, positions, mix_norm_pre, mix_norm_post, w_in, q_norm, w_uq, kv_norm, w_ukv, conv_w, conv_b, dt_bias, a_log, d_skip, ssm_norm, w_attn_o, w_ssm_o, w_out, ffn_norm_pre, ffn_norm_post, w_gate, w_up, w_down, ple_norm_pre, ple_norm_post, w_ple_gate, w_ple, loss_target, m_mix_norm_pre, m_mix_norm_post, m_w_in, m_q_norm, m_w_uq, m_kv_norm, m_w_ukv, m_conv_w, m_conv_b, m_dt_bias, m_a_log, m_d_skip, m_ssm_norm, m_w_attn_o, m_w_ssm_o, m_w_out, m_ffn_norm_pre, m_ffn_norm_post, m_w_gate, m_w_up, m_w_down, m_ple_norm_pre, m_ple_norm_post, m_w_ple_gate, m_w_ple, v_mix_norm_pre, v_mix_norm_post, v_w_in, v_q_norm, v_w_uq, v_kv_norm, v_w_ukv, v_conv_w, v_conv_b, v_dt_bias, v_a_log, v_d_skip, v_ssm_norm, v_w_attn_o, v_w_ssm_o, v_w_out, v_ffn_norm_pre, v_ffn_norm_post, v_w_gate, v_w_up, v_w_down, v_ple_norm_pre, v_ple_norm_post, v_w_ple_gate, v_w_ple):
    given = dict(locals())
    names = [n for n, _, _ in BIG] + [n for n, _ in SMALL] + ["conv_w"]
    order = ["mix_norm_pre", "mix_norm_post", "w_in", "q_norm", "w_uq", "kv_norm", "w_ukv", "conv_w", "conv_b", "dt_bias", "a_log",
             "d_skip", "ssm_norm", "w_attn_o", "w_ssm_o", "w_out", "ffn_norm_pre", "ffn_norm_post", "w_gate", "w_up", "w_down",
             "ple_norm_pre", "ple_norm_post", "w_ple_gate", "w_ple"]
    assert sorted(names) == sorted(order)
    cx, cy, cc = _place()
    chip = 2 * cx + cy
    conv_cols = CONV_DIM // N_CHIPS

    ex = _Exchange({n: given[n][0] for n, _, _ in BIG}, chip, cc)
    own = jnp.where(cc == 0, conv_w[0], 0.0)
    conv_vec = lax.dynamic_update_slice(jnp.zeros((CONV_WIDTH, CONV_DIM), F32), own, (0, chip * conv_cols))
    conv_full = _allreduce_small(conv_vec.reshape(CONV_W_LEN // LANE, LANE), "gather_conv_w").reshape(CONV_WIDTH, CONV_DIM)
    sp = {n: given[n] for n, _ in SMALL}
    sp["conv_w"] = conv_full

    loss_part, grad_x, gs = _local_step(x[0], p[0, 0], positions[0], ex, sp, loss_target[0])

    g_big = ex.finish()

    small_parts = [gs[n] for n, _ in SMALL] + [gs["conv_w"], loss_part[:, :1]]
    small_vec = jnp.concatenate([t.reshape(-1) for t in small_parts])
    small_vec = jnp.pad(small_vec, (0, SMALL_ROWS * LANE - small_vec.shape[0])).reshape(SMALL_ROWS, LANE)
    small_sum = _allreduce_small(small_vec, "allreduce_small").reshape(-1)
    g_small, off = {}, 0
    for n, length in SMALL:
        g_small[n] = small_sum[off:off + length].reshape(1, length)
        off += length
    g_conv = small_sum[off:off + CONV_W_LEN].reshape(CONV_WIDTH, CONV_DIM)
    g_small["conv_w"] = lax.dynamic_slice(g_conv, (0, chip * conv_cols), (CONV_WIDTH, conv_cols)).reshape(1, CONV_WIDTH, conv_cols)
    loss = small_sum[off + CONV_W_LEN]

    grads, deltas, new_m, new_v = [], [], [], []
    for n in order:
        g = g_big[n] if n in g_big else g_small[n]
        d, m_, v_ = _adamw(given[n], g, given["m_" + n], given["v_" + n], "adamw_" + n)
        grads.append(g)
        deltas.append(d)
        new_m.append(m_)
        new_v.append(v_)
    return (loss, grad_x.reshape(x.shape), *grads, *deltas, *new_m, *new_v)
```

```python
import numpy as np
import jax
import jax.numpy as jnp
from jax import lax
from jax.experimental import pallas as pl
from jax.experimental.pallas import tpu as pltpu

F32 = jnp.float32
BF16 = jnp.bfloat16

D_MODEL = 2048
N_HEADS_MLA = 16
Q_LORA = 512
KV_LORA = 512
QK_NOPE = 128
QK_ROPE = 64
V_DIM = 128
QK_DIM = QK_NOPE + QK_ROPE
ROPE_THETA = 10000.0
D_INNER = 4096
SSM_HEADDIM = 64
N_HEADS_SSM = 64
SSM_GROUPS = 8
HEADS_PER_GROUP = 8
D_STATE = 128
CONV_WIDTH = 4
CHUNK = 256
CONV_DIM = D_INNER + 2 * SSM_GROUPS * D_STATE
D_FF = 5632
PLE_DIM = 256
EPS = 1e-6
IN_SPLITS = (Q_LORA, KV_LORA, QK_ROPE, D_INNER, CONV_DIM, N_HEADS_SSM, D_MODEL, D_MODEL)

ADAM_LR = 0.001
ADAM_B1 = 0.9
ADAM_B2 = 0.999
ADAM_EPS = 1e-08
ADAM_WD = 0.01
ADAM_STEP = 10

LANE = 128
SUBLANE = 8
HEAD_PAD = 256
VMEM_LIMIT = 56 * 1024 * 1024
ATTN_TILE = 1024
NEG = -1e30

MESH_ID = pl.DeviceIdType.MESH
N_CHIPS = 4
N_DEV = 8


def _tile(n, pref, mult=LANE):
    if n <= pref:
        return n
    t = (pref // mult) * mult
    while t >= mult:
        if n % t == 0:
            return t
        t -= mult
    return n


def _params(sem, vmem=VMEM_LIMIT, **kw):
    return pltpu.CompilerParams(dimension_semantics=sem, vmem_limit_bytes=vmem, **kw)


class _Hook:
    def __init__(self, ins, out_shapes, sems, make, aliases=None):
        self.ins, self.out_shapes, self.sems, self.make, self.aliases = list(ins), list(out_shapes), tuple(sems), make, dict(aliases or {})


def _merge_hooks(hooks):
    hooks = [h for h in hooks if h is not None]
    if not hooks:
        return None
    ins, outs, sems, aliases, cuts = [], [], [], {}, []
    for h in hooks:
        cuts.append((len(ins), len(outs), len(sems)))
        aliases.update({len(ins) + i: len(outs) + o for i, o in h.aliases.items()})
        ins += h.ins
        outs += h.out_shapes
        sems += h.sems

    def make(in_refs, out_refs, sem_refs):
        pairs = []
        for h, (i0, o0, s0) in zip(hooks, cuts):
            pairs.append(h.make(in_refs[i0:i0 + len(h.ins)], out_refs[o0:o0 + len(h.out_shapes)], sem_refs[s0:s0 + len(h.sems)]))

        def start():
            for st, _ in pairs:
                st()

        def finish():
            for _, fin in pairs:
                fin()

        return start, finish

    return _Hook(ins, outs, sems, make, aliases)


def _mm(a, b, *, ta=False, tb=False, add=None, out_dtype=F32, name, tm=1024, tn=1536, tk=2048, hook=None):
    if ta:
        K, M = a.shape
    else:
        M, K = a.shape
    N = b.shape[0] if tb else b.shape[1]
    assert (b.shape[1] if tb else b.shape[0]) == K, (a.shape, b.shape, ta, tb)
    tm, tn, tk = _tile(M, tm), _tile(N, tn), _tile(K, tk)
    nk = K // tk
    dn = (((0 if ta else 1,), (1 if tb else 0,)), ((), ()))
    has_add = add is not None
    n_own = 3 if has_add else 2
    n_hin = len(hook.ins) if hook else 0
    n_hout = len(hook.out_shapes) if hook else 0
    grid = (M // tm, N // tn, nk)

    def body(*refs):
        a_ref, b_ref = refs[:2]
        c_ref = refs[2] if has_add else None
        o_ref = refs[n_own + n_hin]
        scratch = refs[n_own + n_hin + 1 + n_hout:]
        if hook:
            start, finish = hook.make(refs[n_own:n_own + n_hin], refs[n_own + n_hin + 1:n_own + n_hin + 1 + n_hout],
                                      scratch[len(scratch) - len(hook.sems):])
            ids = [pl.program_id(d) for d in range(3)]
            pl.when((ids[0] == 0) & (ids[1] == 0) & (ids[2] == 0))(start)
        prod = lax.dot_general(a_ref[...].astype(BF16), b_ref[...].astype(BF16), dn, preferred_element_type=F32)
        if nk == 1:
            o_ref[...] = ((c_ref[...] + prod) if has_add else prod).astype(out_dtype)
        else:
            acc = scratch[0]
            k = pl.program_id(2)

            @pl.when(k == 0)
            def _():
                acc[...] = (c_ref[...] + prod) if has_add else prod

            @pl.when(k > 0)
            def _():
                acc[...] += prod

            @pl.when(k == nk - 1)
            def _():
                o_ref[...] = acc[...].astype(out_dtype)
        if hook:
            pl.when((ids[0] == grid[0] - 1) & (ids[1] == grid[1] - 1) & (ids[2] == grid[2] - 1))(finish)

    a_spec = pl.BlockSpec((tk, tm), lambda i, j, k: (k, i)) if ta else pl.BlockSpec((tm, tk), lambda i, j, k: (i, k))
    b_spec = pl.BlockSpec((tn, tk), lambda i, j, k: (j, k)) if tb else pl.BlockSpec((tk, tn), lambda i, j, k: (k, j))
    in_specs = [a_spec, b_spec]
    args = [a, b]
    if has_add:
        in_specs.append(pl.BlockSpec((tm, tn), lambda i, j, k: (i, j)))
        args.append(add)
    hbm = pl.BlockSpec(memory_space=pl.ANY)
    scratch_shapes = [pltpu.VMEM((tm, tn), F32)] if nk > 1 else []
    out_shape = jax.ShapeDtypeStruct((M, N), out_dtype)
    out_spec = pl.BlockSpec((tm, tn), lambda i, j, k: (i, j))
    if not hook:
        return pl.pallas_call(
            body, name=name, out_shape=out_shape, grid=grid, in_specs=in_specs, out_specs=out_spec,
            scratch_shapes=scratch_shapes,
            compiler_params=_params(("parallel", "parallel", "arbitrary")),
        )(*args)
    outs = pl.pallas_call(
        body, name=name, out_shape=(out_shape, *hook.out_shapes), grid=grid,
        in_specs=in_specs + [hbm] * n_hin, out_specs=(out_spec, *[hbm] * n_hout),
        scratch_shapes=scratch_shapes + [pltpu.SemaphoreType.DMA((s,)) for s in hook.sems],
        input_output_aliases={n_own + i: 1 + o for i, o in hook.aliases.items()},
        compiler_params=_params(("arbitrary", "arbitrary", "arbitrary")),
    )(*args, *hook.ins)
    return outs[0], list(outs[1:])


def _mm_swiglu(a, w_gate, w_up, tm=1024, tn=512):
    M, K = a.shape
    N = w_gate.shape[1]
    tm, tn = _tile(M, tm), _tile(N, tn)

    def body(a_ref, g_ref, u_ref, go_ref, uo_ref, s_ref):
        av = a_ref[...].astype(BF16)
        gate = jnp.dot(av, g_ref[...].astype(BF16), preferred_element_type=F32)
        up = jnp.dot(av, u_ref[...].astype(BF16), preferred_element_type=F32)
        go_ref[...] = gate
        uo_ref[...] = up
        s_ref[...] = (gate * _sigmoid(gate) * up).astype(s_ref.dtype)

    w_spec = pl.BlockSpec((K, tn), lambda i, j: (0, j))
    o_spec = pl.BlockSpec((tm, tn), lambda i, j: (i, j))
    return pl.pallas_call(
        body, name="mm_swiglu",
        out_shape=(jax.ShapeDtypeStruct((M, N), F32), jax.ShapeDtypeStruct((M, N), F32), jax.ShapeDtypeStruct((M, N), BF16)),
        grid=(M // tm, N // tn),
        in_specs=[pl.BlockSpec((tm, K), lambda i, j: (i, 0)), w_spec, w_spec],
        out_specs=(o_spec, o_spec, o_spec),
        compiler_params=_params(("parallel", "parallel")),
    )(a, w_gate, w_up)


def _row(arr, width=None, cblk=0):
    return ("row", arr, arr.shape[1] if width is None else width, cblk)


def _full(arr):
    return ("full", arr)


def _prev8(arr):
    return ("prev8", arr)


def _next8(arr):
    return ("next8", arr)


def _rows(fn, n_rows, tm, ins, outs, name):
    tm = min(tm, n_rows)
    assert n_rows % tm == 0 and tm % SUBLANE == 0
    n = n_rows // tm
    in_specs, args = [], []
    for spec in ins:
        kind, arr = spec[0], spec[1]
        if kind == "row":
            _, _, w, cb = spec
            in_specs.append(pl.BlockSpec((tm, w), lambda i, cb=cb: (i, cb)))
        elif kind == "full":
            in_specs.append(pl.BlockSpec(arr.shape, lambda i, nd=arr.ndim: (0,) * nd))
        elif kind == "prev8":
            in_specs.append(pl.BlockSpec((SUBLANE, arr.shape[1]),
                                         lambda i: (jnp.maximum(i * (tm // SUBLANE) - 1, 0), 0)))
        elif kind == "next8":
            last = n_rows // SUBLANE - 1
            in_specs.append(pl.BlockSpec((SUBLANE, arr.shape[1]),
                                         lambda i: (jnp.minimum((i + 1) * (tm // SUBLANE), last), 0)))
        else:
            raise ValueError(kind)
        args.append(arr)
    out_shapes, out_specs = [], []
    any_acc = False
    for spec in outs:
        if spec[0] == "row":
            _, w, dt = spec
            out_shapes.append(jax.ShapeDtypeStruct((n_rows, w), dt))
            out_specs.append(pl.BlockSpec((tm, w), lambda i: (i, 0)))
        else:
            _, shp, dt = spec
            any_acc = True
            out_shapes.append(jax.ShapeDtypeStruct(shp, dt))
            out_specs.append(pl.BlockSpec(shp, lambda i, nd=len(shp): (0,) * nd))
    nin = len(ins)

    def body(*refs):
        i = pl.program_id(0)
        vals = fn(i, n, *[r[...] for r in refs[:nin]])
        for o_ref, spec, v in zip(refs[nin:], outs, vals):
            if spec[0] == "acc":
                @pl.when(i == 0)
                def _(o_ref=o_ref):
                    o_ref[...] = jnp.zeros_like(o_ref)

                o_ref[...] += v.astype(o_ref.dtype)
            else:
                o_ref[...] = v.astype(o_ref.dtype)

    res = pl.pallas_call(
        body, name=name,
        out_shape=tuple(out_shapes),
        grid=(n,),
        in_specs=in_specs,
        out_specs=tuple(out_specs),
        compiler_params=_params(("arbitrary",) if any_acc else ("parallel",)),
    )(*args)
    return res


def _rstd(x):
    return lax.rsqrt(jnp.mean(x * x, axis=-1, keepdims=True) + EPS)


def _norm_bwd(x, r, g, dy):
    xh = x * r
    dyg = dy * g
    dx = r * (dyg - xh * jnp.mean(dyg * xh, axis=-1, keepdims=True))
    return dx, dy * xh


def _sigmoid(x):
    return 0.5 * jnp.tanh(0.5 * x) + 0.5


def _colsum(v):
    return jnp.sum(v, axis=0, keepdims=True)


def _rope_tables(pos, invf):
    ang = pos.astype(F32) * invf
    lane = lax.broadcasted_iota(jnp.int32, ang.shape, 1)
    cos, sin = jnp.cos(ang), jnp.sin(ang)
    ct = jnp.where(lane < QK_ROPE, cos, 0.0)
    sa = jnp.where(lane < QK_ROPE // 2, -sin, 0.0)
    sb = jnp.where((lane >= QK_ROPE // 2) & (lane < QK_ROPE), sin, 0.0)
    return ct, sa, sb


def _rope(b, ct, sa, sb):
    return ct * b + sa * pltpu.roll(b, LANE - QK_ROPE // 2, 1) + sb * pltpu.roll(b, QK_ROPE // 2, 1)


def _rope_t(d, ct, sa, sb):
    return ct * d + pltpu.roll(sa * d, QK_ROPE // 2, 1) + pltpu.roll(sb * d, LANE - QK_ROPE // 2, 1)


def _rope_fwd(q_raw, kr_pad, pos_col, invf):
    S = q_raw.shape[0]

    def fn(i, n, q, kr, pos, invf):
        ct, sa, sb = _rope_tables(pos, invf)
        parts = []
        for h in range(N_HEADS_MLA):
            parts.append(q[:, h * HEAD_PAD:h * HEAD_PAD + LANE])
            parts.append(_rope(q[:, h * HEAD_PAD + LANE:(h + 1) * HEAD_PAD], ct, sa, sb))
        return jnp.concatenate(parts, axis=1), _rope(kr, ct, sa, sb)

    return _rows(fn, S, 256, [_row(q_raw), _row(kr_pad), _row(pos_col), _full(invf)],
                 [("row", N_HEADS_MLA * HEAD_PAD, BF16), ("row", LANE, BF16)], "rope_fwd")


def _rope_bwd(dq, dkp, pos_col, invf):
    S = dq.shape[0]
    tm = 256

    def body(dq_ref, dkp_ref, pos_ref, invf_ref, dqo_ref, dkr_ref):
        ct, sa, sb = _rope_tables(pos_ref[...], invf_ref[...])
        for h in range(N_HEADS_MLA):
            dqo_ref[:, h * HEAD_PAD:h * HEAD_PAD + LANE] = dq_ref[:, h * HEAD_PAD:h * HEAD_PAD + LANE].astype(BF16)
            dqo_ref[:, h * HEAD_PAD + LANE:(h + 1) * HEAD_PAD] = _rope_t(
                dq_ref[:, h * HEAD_PAD + LANE:(h + 1) * HEAD_PAD], ct, sa, sb).astype(BF16)
        tot = dkp_ref[0]
        for h in range(1, N_HEADS_MLA):
            tot = tot + dkp_ref[h]
        dkr_ref[...] = _rope_t(tot, ct, sa, sb).astype(BF16)

    return pl.pallas_call(
        body, name="rope_bwd",
        out_shape=(jax.ShapeDtypeStruct(dq.shape, BF16), jax.ShapeDtypeStruct((S, LANE), BF16)),
        grid=(S // tm,),
        in_specs=[pl.BlockSpec((tm, dq.shape[1]), lambda i: (i, 0)),
                  pl.BlockSpec((N_HEADS_MLA, tm, LANE), lambda i: (0, i, 0)),
                  pl.BlockSpec((tm, 1), lambda i: (i, 0)),
                  pl.BlockSpec((1, LANE), lambda i: (0, 0))],
        out_specs=(pl.BlockSpec((tm, dq.shape[1]), lambda i: (i, 0)), pl.BlockSpec((tm, LANE), lambda i: (i, 0))),
        compiler_params=_params(("parallel",)),
    )(dq, dkp, pos_col, invf)


def _row_of(col, n):
    eye = lax.broadcasted_iota(jnp.int32, (LANE, LANE), 0) == lax.broadcasted_iota(jnp.int32, (LANE, LANE), 1)
    parts = [jnp.sum(jnp.where(eye, col[i:i + LANE], 0.0), axis=0, keepdims=True) for i in range(0, n, LANE)]
    return parts[0] if len(parts) == 1 else jnp.concatenate(parts, axis=1)


def _attn_fwd(q, kv, kp, tile):
    S = q.shape[0]
    nq = S // tile
    scale = QK_DIM ** -0.5
    nt = (((1,), (1,)), ((), ()))

    def body(q_ref, kv_ref, kp_ref, o_ref, lse_ref, m_s, l_s, acc_s, s_buf):
        qi = pl.program_id(1)
        qv = q_ref[...]
        m_s[...] = jnp.full_like(m_s, NEG)
        l_s[...] = jnp.zeros_like(l_s)
        acc_s[...] = jnp.zeros_like(acc_s)

        def scores(j):
            start = pl.multiple_of(j * tile, tile)
            k = jnp.concatenate([kv_ref[pl.ds(start, tile), 0:LANE], kp_ref[pl.ds(start, tile), :]], axis=1)
            return lax.dot_general(qv, k, nt, preferred_element_type=F32) * scale

        def update(s, j):
            v = kv_ref[pl.ds(pl.multiple_of(j * tile, tile), tile), LANE:2 * LANE]
            m_old = m_s[...]
            m_new = jnp.maximum(m_old, jnp.max(s, axis=1, keepdims=True))
            alpha = jnp.exp(m_old - m_new)
            p = jnp.exp(s - m_new)
            l_s[...] = alpha * l_s[...] + jnp.sum(p, axis=1, keepdims=True)
            acc_s[...] = alpha * acc_s[...] + jnp.dot(p.astype(BF16), v, preferred_element_type=F32)
            m_s[...] = m_new

        s_buf[0] = scores(0)

        def loop_body(j, carry):
            nxt = scores(j + 1)
            update(s_buf[lax.rem(j, 2)], j)
            s_buf[lax.rem(j + 1, 2)] = nxt
            return carry

        lax.fori_loop(0, qi, loop_body, 0)
        s = s_buf[lax.rem(qi, 2)]
        row = lax.broadcasted_iota(jnp.int32, s.shape, 0)
        col = lax.broadcasted_iota(jnp.int32, s.shape, 1)
        update(jnp.where(row >= col, s, NEG), qi)
        l = l_s[...]
        o_ref[...] = (acc_s[...] / l).astype(o_ref.dtype)
        lse_ref[0, 0] = _row_of(m_s[...] + jnp.log(l), tile)

    return pl.pallas_call(
        body, name="attn_fwd",
        out_shape=(jax.ShapeDtypeStruct((S, N_HEADS_MLA * V_DIM), BF16),
                   jax.ShapeDtypeStruct((N_HEADS_MLA, nq, 1, tile), F32)),
        grid=(N_HEADS_MLA, nq),
        in_specs=[pl.BlockSpec((tile, HEAD_PAD), lambda h, i: (i, h)),
                  pl.BlockSpec((S, HEAD_PAD), lambda h, i: (0, h)),
                  pl.BlockSpec((S, LANE), lambda h, i: (0, 0))],
        out_specs=(pl.BlockSpec((tile, V_DIM), lambda h, i: (i, h)),
                   pl.BlockSpec((1, 1, 1, tile), lambda h, i: (h, i, 0, 0))),
        scratch_shapes=[pltpu.VMEM((tile, 1), F32), pltpu.VMEM((tile, 1), F32), pltpu.VMEM((tile, V_DIM), F32),
                        pltpu.VMEM((2, tile, tile), F32)],
        compiler_params=_params(("parallel", "arbitrary")),
    )(q, kv, kp)


def _attn_delta(o, do, tile):
    S = o.shape[0]
    nq = S // tile

    def body(o_ref, do_ref, d_ref):
        prod = o_ref[...].astype(F32) * do_ref[...].astype(F32)
        for h in range(N_HEADS_MLA):
            col = jnp.sum(prod[:, h * V_DIM:(h + 1) * V_DIM], axis=1, keepdims=True)
            d_ref[h, 0] = _row_of(col, tile)

    return pl.pallas_call(
        body, name="attn_delta",
        out_shape=jax.ShapeDtypeStruct((N_HEADS_MLA, nq, 1, tile), F32),
        grid=(nq,),
        in_specs=[pl.BlockSpec((tile, o.shape[1]), lambda i: (i, 0)), pl.BlockSpec((tile, o.shape[1]), lambda i: (i, 0))],
        out_specs=pl.BlockSpec((N_HEADS_MLA, 1, 1, tile), lambda i: (0, i, 0, 0)),
        compiler_params=_params(("parallel",)),
    )(o, do)


def _attn_bwd(q, kv, kp, do, lse, delta, tile):
    S = q.shape[0]
    nq = S // tile
    scale = QK_DIM ** -0.5
    nt = (((1,), (1,)), ((), ()))
    tn = (((0,), (0,)), ((), ()))

    def body(kv_ref, kp_ref, q_ref, do_ref, lse_ref, d_ref, dq_ref, dkv_ref, dkp_ref, dk_s, dv_s):
        ki = pl.program_id(1)
        k = jnp.concatenate([kv_ref[:, 0:LANE], kp_ref[...]], axis=1)
        v = kv_ref[:, LANE:2 * LANE]

        @pl.when(ki == 0)
        def _():
            dq_ref[...] = jnp.zeros_like(dq_ref)

        dk_s[...] = jnp.zeros_like(dk_s)
        dv_s[...] = jnp.zeros_like(dv_s)

        def step(qi, masked):
            start = pl.multiple_of(qi * tile, tile)
            qv = q_ref[pl.ds(start, tile), :]
            dov = do_ref[pl.ds(start, tile), :]
            st = lax.dot_general(k, qv, nt, preferred_element_type=F32) * scale
            pt = jnp.exp(st - lse_ref[0, qi])
            if masked:
                krow = lax.broadcasted_iota(jnp.int32, pt.shape, 0)
                qcol = lax.broadcasted_iota(jnp.int32, pt.shape, 1)
                pt = jnp.where(krow <= qcol, pt, 0.0)
            dv_s[...] += jnp.dot(pt.astype(BF16), dov, preferred_element_type=F32)
            dpt = lax.dot_general(v, dov, nt, preferred_element_type=F32)
            dst = (pt * (dpt - d_ref[0, qi]) * scale).astype(BF16)
            dk_s[...] += jnp.dot(dst, qv, preferred_element_type=F32)
            dq_ref[pl.ds(start, tile), :] += lax.dot_general(dst, k, tn, preferred_element_type=F32)

        step(ki, True)

        def loop_body(qi, carry):
            step(qi, False)
            return carry

        lax.fori_loop(ki + 1, nq, loop_body, 0)
        dkv_ref[...] = jnp.concatenate([dk_s[:, 0:LANE], dv_s[...]], axis=1).astype(dkv_ref.dtype)
        dkp_ref[0] = dk_s[:, LANE:2 * LANE]

    return pl.pallas_call(
        body, name="attn_bwd",
        out_shape=(jax.ShapeDtypeStruct((S, N_HEADS_MLA * HEAD_PAD), F32),
                   jax.ShapeDtypeStruct((S, N_HEADS_MLA * HEAD_PAD), BF16),
                   jax.ShapeDtypeStruct((N_HEADS_MLA, S, LANE), F32)),
        grid=(N_HEADS_MLA, nq),
        in_specs=[pl.BlockSpec((tile, HEAD_PAD), lambda h, i: (i, h)),
                  pl.BlockSpec((tile, LANE), lambda h, i: (i, 0)),
                  pl.BlockSpec((S, HEAD_PAD), lambda h, i: (0, h)),
                  pl.BlockSpec((S, V_DIM), lambda h, i: (0, h)),
                  pl.BlockSpec((1, nq, 1, tile), lambda h, i: (h, 0, 0, 0)),
                  pl.BlockSpec((1, nq, 1, tile), lambda h, i: (h, 0, 0, 0))],
        out_specs=(pl.BlockSpec((S, HEAD_PAD), lambda h, i: (0, h)),
                   pl.BlockSpec((tile, HEAD_PAD), lambda h, i: (i, h)),
                   pl.BlockSpec((1, tile, LANE), lambda h, i: (h, i, 0))),
        scratch_shapes=[pltpu.VMEM((tile, HEAD_PAD), F32), pltpu.VMEM((tile, V_DIM), F32)],
        compiler_params=_params(("parallel", "arbitrary")),
    )(kv, kp, q, do, lse, delta)


def _shift_down(cur, halo, k):
    sh = pltpu.roll(cur, k, 0)
    hs = pltpu.roll(halo, k, 0)
    rows = lax.broadcasted_iota(jnp.int32, hs.shape, 0)
    first = jnp.where(rows < k, hs, sh[0:SUBLANE])
    if cur.shape[0] == SUBLANE:
        return first
    return jnp.concatenate([first, sh[SUBLANE:]], axis=0)


def _shift_up(cur, nxt, k):
    n = cur.shape[0]
    sh = pltpu.roll(cur, n - k, 0)
    ns = pltpu.roll(nxt, SUBLANE - k, 0)
    rows = lax.broadcasted_iota(jnp.int32, ns.shape, 0)
    last = jnp.where(rows >= SUBLANE - k, ns, sh[n - SUBLANE:])
    if n == SUBLANE:
        return last
    return jnp.concatenate([sh[:n - SUBLANE], last], axis=0)


def _conv_pre(cur, halo, w, b):
    shifted = [_shift_down(cur, halo, k) for k in range(1, CONV_WIDTH)]
    out = b + w[3:4] * cur
    for k in range(1, CONV_WIDTH):
        out = out + w[3 - k:4 - k] * shifted[k - 1]
    return out, shifted


def _conv_fwd(xbc, w, b):
    S = xbc.shape[0]

    def fn(i, n, cur, prev, w, b):
        halo = jnp.where(i > 0, prev, 0.0)
        pre, _ = _conv_pre(cur, halo, w, b)
        return (pre * _sigmoid(pre),)

    return _rows(fn, S, 256, [_row(xbc), _prev8(xbc), _full(w), _full(b)], [("row", xbc.shape[1], F32)], "conv_fwd")[0]


def _conv_bwd(xbc, dacts, w, b):
    S, C = xbc.shape

    def dsilu(pre):
        s = _sigmoid(pre)
        return s * (1.0 + pre * (1.0 - s))

    def fn(i, n, cur, prev, nxt, *rest):
        k3 = len(dacts)
        dcur = jnp.concatenate(rest[:k3], axis=1)
        dnxt = jnp.concatenate(rest[k3:2 * k3], axis=1)
        w, b = rest[2 * k3:]
        halo = jnp.where(i > 0, prev, 0.0)
        pre, shifted = _conv_pre(cur, halo, w, b)
        dpre = dcur * dsilu(pre)
        pre_n, _ = _conv_pre(nxt, cur[cur.shape[0] - SUBLANE:], w, b)
        dpre_n = jnp.where(i < n - 1, dnxt * dsilu(pre_n), 0.0)
        dx = w[3:4] * dpre
        rows = lax.broadcasted_iota(jnp.int32, (SUBLANE, C), 0)
        dw = jnp.where(rows == 3, _colsum(dpre * cur), 0.0)
        for k in range(1, CONV_WIDTH):
            dx = dx + w[3 - k:4 - k] * _shift_up(dpre, dpre_n, k)
            dw = dw + jnp.where(rows == 3 - k, _colsum(dpre * shifted[k - 1]), 0.0)
        return dx, dw, _colsum(dpre)

    return _rows(fn, S, 256, [_row(xbc), _prev8(xbc), _next8(xbc), *[_row(d) for d in dacts], *[_next8(d) for d in dacts],
                              _full(w), _full(b)],
                 [("row", C, BF16), ("acc", (SUBLANE, C), F32), ("acc", (1, C), F32)], "conv_bwd")


def _softplus(x):
    return jnp.maximum(x, 0.0) + jnp.log1p(jnp.exp(-jnp.abs(x)))


def _cumsum_rows(x):
    rows = lax.broadcasted_iota(jnp.int32, x.shape, 0)
    s = 1
    while s < x.shape[0]:
        x = x + jnp.where(rows >= s, pltpu.roll(x, s, 0), 0.0)
        s *= 2
    return x


def _revcumsum_rows(x):
    n = x.shape[0]
    rows = lax.broadcasted_iota(jnp.int32, x.shape, 0)
    s = 1
    while s < n:
        x = x + jnp.where(rows < n - s, pltpu.roll(x, n - s, 0), 0.0)
        s *= 2
    return x


def _dt_prep(dt_raw, dt_bias, a_log):
    S = dt_raw.shape[0]

    def body(raw_ref, bias_ref, alog_ref, dt_ref, cum_ref, cumt_ref):
        dt = _softplus(raw_ref[...] + bias_ref[...])
        cum = _cumsum_rows(dt * (-jnp.exp(alog_ref[...])))
        dt_ref[...] = dt
        cum_ref[...] = cum
        cumt_ref[...] = cum.T

    return pl.pallas_call(
        body, name="dt_prep",
        out_shape=(jax.ShapeDtypeStruct((S, LANE), F32), jax.ShapeDtypeStruct((S, LANE), F32),
                   jax.ShapeDtypeStruct((LANE, S), F32)),
        grid=(S // CHUNK,),
        in_specs=[pl.BlockSpec((CHUNK, LANE), lambda i: (i, 0)), pl.BlockSpec((1, LANE), lambda i: (0, 0)),
                  pl.BlockSpec((1, LANE), lambda i: (0, 0))],
        out_specs=(pl.BlockSpec((CHUNK, LANE), lambda i: (i, 0)), pl.BlockSpec((CHUNK, LANE), lambda i: (i, 0)),
                   pl.BlockSpec((LANE, CHUNK), lambda i: (0, i))),
        compiler_params=_params(("parallel",)),
    )(dt_raw, dt_bias, a_log)


_NT = (((1,), (1,)), ((), ()))
_TN = (((0,), (0,)), ((), ()))
P = SSM_HEADDIM
GW = HEADS_PER_GROUP * SSM_HEADDIM


PAIRS = HEADS_PER_GROUP // 2
SPREAD_W = HEADS_PER_GROUP * LANE


def _spread_matrix():
    e = np.zeros((SSM_GROUPS, LANE, SPREAD_W), np.float32)
    for g in range(SSM_GROUPS):
        for r in range(HEADS_PER_GROUP):
            e[g, g * HEADS_PER_GROUP + r, r * LANE:(r + 1) * LANE] = 1.0
    return jnp.asarray(e, BF16)


def _pieces(v, n):
    out = []
    for _ in range(n):
        p = v.astype(BF16)
        out.append(p)
        v = v - p.astype(F32)
    return out


def _spread(v, e, n):
    tot = None
    for p in _pieces(v, n):
        t = jnp.dot(p, e, preferred_element_type=F32)
        tot = t if tot is None else tot + t
    return tot


def _gather_rows(z, e):
    hi, lo = _pieces(z, 2)
    return lax.dot_general(hi, e, _NT, preferred_element_type=F32) + lax.dot_general(lo, e, _NT, preferred_element_type=F32)


def _decay_pair(cc, cr, transposed):
    L = cc.shape[0]
    halves = []
    for h in range(L // LANE):
        i = lax.broadcasted_iota(jnp.int32, (L, LANE), 0)
        j = lax.broadcasted_iota(jnp.int32, (L, LANE), 1) + h * LANE
        crh = cr[:, h * LANE:(h + 1) * LANE]
        if transposed:
            halves.append(jnp.exp(jnp.where(j >= i, crh - cc, NEG)))
        else:
            halves.append(jnp.exp(jnp.where(i >= j, cc - crh, NEG)))
    return jnp.concatenate(halves, axis=1)


def _ssd_fwd(xbc_c, dt, cum, cumt_g, spread):
    S = xbc_c.shape[0]
    nc = S // CHUNK
    L = CHUNK
    boff = D_INNER // D_STATE

    def body(x_ref, b_ref, c_ref, dt_ref, cum_ref, cumt_ref, e_ref, y_ref, st_ref, state):
        c = pl.program_id(1)

        @pl.when(c == 0)
        def _():
            state[...] = jnp.zeros_like(state)

        e = e_ref[0]
        bm = b_ref[...].astype(BF16)
        cm = c_ref[...].astype(BF16)
        cb = lax.dot_general(cm, bm, _NT, preferred_element_type=F32)
        rep_cum = _spread(cum_ref[...], e, 3)
        rep_dt = _spread(dt_ref[...], e, 2)
        lo = lax.broadcasted_iota(jnp.int32, (L, LANE), 1) < P
        lo1 = lax.broadcasted_iota(jnp.int32, (1, LANE), 1) < P
        top = lax.broadcasted_iota(jnp.int32, (LANE, LANE), 0) < P
        for p in range(PAIRS):
            t0, t1 = 2 * p * LANE, (2 * p + 1) * LANE
            cc0, cc1 = rep_cum[:, t0:t0 + LANE], rep_cum[:, t1:t1 + LANE]
            ccp = jnp.where(lo, cc0, cc1)
            cl0, cl1 = cc0[L - 1:L, :], cc1[L - 1:L, :]
            clp = jnp.where(lo1, cl0, cl1)
            xdt = x_ref[:, p * LANE:(p + 1) * LANE] * jnp.where(lo, rep_dt[:, t0:t0 + LANE], rep_dt[:, t1:t1 + LANE])
            xdb = xdt.astype(BF16)
            ys = []
            for r, cc in ((2 * p, cc0), (2 * p + 1, cc1)):
                m = (cb * _decay_pair(cc, cumt_ref[0, r:r + 1, :], False)).astype(BF16)
                ys.append(jnp.dot(m, xdb, preferred_element_type=F32))
            st = state[p * LANE:(p + 1) * LANE, :]
            st_ref[0, 0, p * LANE:(p + 1) * LANE, :] = st
            yoff = lax.dot_general(cm, st.astype(BF16), _NT, preferred_element_type=F32) * jnp.exp(ccp)
            y_ref[:, p * LANE:(p + 1) * LANE] = jnp.where(lo, ys[0], ys[1]) + yoff
            wend = jnp.exp(clp - ccp)
            ecl = jnp.where(top, jnp.exp(cl0), jnp.exp(cl1))
            state[p * LANE:(p + 1) * LANE, :] = st * ecl + lax.dot_general(
                (xdt * wend).astype(BF16), bm, _TN, preferred_element_type=F32)

    return pl.pallas_call(
        body, name="ssd_fwd",
        out_shape=(jax.ShapeDtypeStruct((S, D_INNER), F32), jax.ShapeDtypeStruct((SSM_GROUPS, nc, GW, D_STATE), F32)),
        grid=(SSM_GROUPS, nc),
        in_specs=[pl.BlockSpec((L, GW), lambda g, c: (c, g)),
                  pl.BlockSpec((L, D_STATE), lambda g, c: (c, boff + g)),
                  pl.BlockSpec((L, D_STATE), lambda g, c: (c, boff + SSM_GROUPS + g)),
                  pl.BlockSpec((L, LANE), lambda g, c: (c, 0)),
                  pl.BlockSpec((L, LANE), lambda g, c: (c, 0)),
                  pl.BlockSpec((1, HEADS_PER_GROUP, L), lambda g, c: (g, 0, c)),
                  pl.BlockSpec((1, LANE, SPREAD_W), lambda g, c: (g, 0, 0))],
        out_specs=(pl.BlockSpec((L, GW), lambda g, c: (c, g)),
                   pl.BlockSpec((1, 1, GW, D_STATE), lambda g, c: (g, c, 0, 0))),
        scratch_shapes=[pltpu.VMEM((GW, D_STATE), F32)],
        compiler_params=_params(("parallel", "arbitrary")),
    )(xbc_c, xbc_c, xbc_c, dt, cum, cumt_g, spread)


def _ssd_bwd(xbc_c, dt, cum, cumt_g, spread, states, dy, d_skip):
    S = xbc_c.shape[0]
    nc = S // CHUNK
    L = CHUNK
    boff = D_INNER // D_STATE
    rev = lambda c: nc - 1 - c

    def body(x_ref, b_ref, c_ref, dt_ref, cum_ref, cumt_ref, e_ref, st_ref, dy_ref, skip_ref,
             dx_ref, db_ref, dc_ref, ddt_ref, dcum_ref, dstate):
        c = pl.program_id(1)

        @pl.when(c == 0)
        def _():
            dstate[...] = jnp.zeros_like(dstate)

        e = e_ref[0]
        bf = b_ref[...]
        bm = bf.astype(BF16)
        cm = c_ref[...].astype(BF16)
        cb = lax.dot_general(cm, bm, _NT, preferred_element_type=F32)
        cbt = lax.dot_general(bm, cm, _NT, preferred_element_type=F32)
        rep_cum = _spread(cum_ref[...], e, 3)
        rep_dt = _spread(dt_ref[...], e, 2)
        lane = lax.broadcasted_iota(jnp.int32, (L, LANE), 1)
        lo = lane < P
        lo1 = lax.broadcasted_iota(jnp.int32, (1, LANE), 1) < P
        top = lax.broadcasted_iota(jnp.int32, (LANE, LANE), 0) < P
        last = lax.broadcasted_iota(jnp.int32, (L, LANE), 0) == L - 1
        dcb = jnp.zeros((L, L), F32)
        dcbt = jnp.zeros((L, L), F32)
        dbs = jnp.zeros((L, D_STATE), F32)
        dcs = jnp.zeros((L, D_STATE), F32)
        zs, zds = [], []
        for p in range(PAIRS):
            sl = slice(p * LANE, (p + 1) * LANE)
            t0, t1 = 2 * p * LANE, (2 * p + 1) * LANE
            cc0, cc1 = rep_cum[:, t0:t0 + LANE], rep_cum[:, t1:t1 + LANE]
            ccp = jnp.where(lo, cc0, cc1)
            cl0, cl1 = cc0[L - 1:L, :], cc1[L - 1:L, :]
            w0, w1 = jnp.exp(cl0 - cc0), jnp.exp(cl1 - cc1)
            wend = jnp.where(lo, w0, w1)
            ecc = jnp.exp(ccp)
            ecl0, ecl1 = jnp.exp(cl0), jnp.exp(cl1)
            dtp = jnp.where(lo, rep_dt[:, t0:t0 + LANE], rep_dt[:, t1:t1 + LANE])
            xp = x_ref[:, sl]
            xdt = xp * dtp
            xdb = xdt.astype(BF16)
            dyp = dy_ref[:, sl]
            st = st_ref[0, 0, sl, :]
            stb = st.astype(BF16)
            ds = dstate[sl, :]
            dsb = ds.astype(BF16)
            yoff = lax.dot_general(cm, stb, _NT, preferred_element_type=F32) * ecc
            dye = (dyp * ecc).astype(BF16)
            dcs = dcs + jnp.dot(dye, stb, preferred_element_type=F32)
            dstate[sl, :] = jnp.where(top, ecl0, ecl1) * ds + lax.dot_general(dye, cm, _TN, preferred_element_type=F32)
            dxd = lax.dot_general(bm, dsb, _NT, preferred_element_type=F32) * wend
            sst = ds * st
            dyo = dyp * yoff
            mts = []
            for r, cc, w, ecl, keep, keep_rows in ((2 * p, cc0, w0, ecl0, lo, top), (2 * p + 1, cc1, w1, ecl1, ~lo, ~top)):
                cr = cumt_ref[0, r:r + 1, :]
                decay = _decay_pair(cc, cr, False)
                decay_t = _decay_pair(cc, cr, True)
                m = cb * decay
                mt = cbt * decay_t
                dyr = jnp.where(keep, dyp, 0.0).astype(BF16)
                g = lax.dot_general(dyr, xdb, _NT, preferred_element_type=F32)
                gt = lax.dot_general(xdb, dyr, _NT, preferred_element_type=F32)
                q = g * m
                qt = gt * mt
                dcb = dcb + g * decay
                dcbt = dcbt + gt * decay_t
                mts.append(jnp.dot(mt.astype(BF16), dyr, preferred_element_type=F32))
                t = jnp.dot(jnp.where(keep, xdt, 0.0).astype(BF16), dsb, preferred_element_type=F32)
                dbs = dbs + t * w
                tb = t * bf * w
                end_row = _colsum(tb) + ecl * _colsum(jnp.where(keep_rows, sst, 0.0))
                z = (q[:, 0:LANE] + q[:, LANE:2 * LANE]) - (qt[:, 0:LANE] + qt[:, LANE:2 * LANE])
                z = z + jnp.where(keep, dyo, 0.0) - tb + jnp.where(last, end_row, 0.0)
                zs.append(z)
            dxd = dxd + mts[0] + mts[1]
            dx_ref[:, sl] = dxd * dtp + dyp * skip_ref[:, sl]
            zd = dxd * xp
            zds.append(jnp.where(lo, zd, 0.0))
            zds.append(jnp.where(lo, 0.0, zd))
        dc_ref[...] = dcs + jnp.dot(dcb.astype(BF16), bm, preferred_element_type=F32)
        db_ref[...] = dbs + jnp.dot(dcbt.astype(BF16), cm, preferred_element_type=F32)
        dcum_ref[0] = _gather_rows(jnp.concatenate(zs, axis=1), e)
        ddt_ref[0] = _gather_rows(jnp.concatenate(zds, axis=1), e)

    return pl.pallas_call(
        body, name="ssd_bwd",
        out_shape=(jax.ShapeDtypeStruct((S, D_INNER), F32),
                   jax.ShapeDtypeStruct((S, SSM_GROUPS * D_STATE), F32),
                   jax.ShapeDtypeStruct((S, SSM_GROUPS * D_STATE), F32),
                   jax.ShapeDtypeStruct((SSM_GROUPS, S, LANE), F32),
                   jax.ShapeDtypeStruct((SSM_GROUPS, S, LANE), F32)),
        grid=(SSM_GROUPS, nc),
        in_specs=[pl.BlockSpec((L, GW), lambda g, c: (rev(c), g)),
                  pl.BlockSpec((L, D_STATE), lambda g, c: (rev(c), boff + g)),
                  pl.BlockSpec((L, D_STATE), lambda g, c: (rev(c), boff + SSM_GROUPS + g)),
                  pl.BlockSpec((L, LANE), lambda g, c: (rev(c), 0)),
                  pl.BlockSpec((L, LANE), lambda g, c: (rev(c), 0)),
                  pl.BlockSpec((1, HEADS_PER_GROUP, L), lambda g, c: (g, 0, rev(c))),
                  pl.BlockSpec((1, LANE, SPREAD_W), lambda g, c: (g, 0, 0)),
                  pl.BlockSpec((1, 1, GW, D_STATE), lambda g, c: (g, rev(c), 0, 0)),
                  pl.BlockSpec((L, GW), lambda g, c: (rev(c), g)),
                  pl.BlockSpec((1, GW), lambda g, c: (0, g))],
        out_specs=(pl.BlockSpec((L, GW), lambda g, c: (rev(c), g)),
                   pl.BlockSpec((L, D_STATE), lambda g, c: (rev(c), g)),
                   pl.BlockSpec((L, D_STATE), lambda g, c: (rev(c), g)),
                   pl.BlockSpec((1, L, LANE), lambda g, c: (g, rev(c), 0)),
                   pl.BlockSpec((1, L, LANE), lambda g, c: (g, rev(c), 0))),
        scratch_shapes=[pltpu.VMEM((GW, D_STATE), F32)],
        compiler_params=_params(("parallel", "arbitrary")),
    )(xbc_c, xbc_c, xbc_c, dt, cum, cumt_g, spread, states, dy, d_skip)


def _dt_bwd(dt_raw, dt_bias, a_log, ddt_x, dcum):
    S = dt_raw.shape[0]
    n = S // CHUNK

    def body(raw_ref, ddx_ref, dcu_ref, bias_ref, alog_ref, draw_ref, gb_ref, ga_ref):
        i = pl.program_id(0)

        @pl.when(i == 0)
        def _():
            gb_ref[...] = jnp.zeros_like(gb_ref)
            ga_ref[...] = jnp.zeros_like(ga_ref)

        ddx, dcu = ddx_ref[0], dcu_ref[0]
        for g in range(1, SSM_GROUPS):
            ddx = ddx + ddx_ref[g]
            dcu = dcu + dcu_ref[g]
        xx = raw_ref[...] + bias_ref[...]
        dt = _softplus(xx)
        a = -jnp.exp(alog_ref[...])
        dda = _revcumsum_rows(dcu)
        lane = lax.broadcasted_iota(jnp.int32, xx.shape, 1)
        draw = jnp.where(lane < N_HEADS_SSM, (ddx + dda * a) * _sigmoid(xx), 0.0)
        draw_ref[...] = draw.astype(draw_ref.dtype)
        gb_ref[...] += _colsum(draw)
        ga_ref[...] += _colsum(dda * dt) * a

    row = pl.BlockSpec((CHUNK, LANE), lambda i: (i, 0))
    grp = pl.BlockSpec((SSM_GROUPS, CHUNK, LANE), lambda i: (0, i, 0))
    one = pl.BlockSpec((1, LANE), lambda i: (0, 0))
    return pl.pallas_call(
        body, name="dt_bwd",
        out_shape=(jax.ShapeDtypeStruct((S, LANE), BF16), jax.ShapeDtypeStruct((1, LANE), F32), jax.ShapeDtypeStruct((1, LANE), F32)),
        grid=(n,),
        in_specs=[row, grp, grp, one, one],
        out_specs=(row, one, one),
        compiler_params=_params(("arbitrary",)),
    )(dt_raw, ddt_x, dcum, dt_bias, a_log)


def _adamw(w, g, m, v, name):
    shape = w.shape
    cols = shape[-1]
    rows = int(np.prod(shape[:-1]))
    w2, g2, m2, v2 = (t.reshape(1, rows, cols) for t in (w, g, m, v))
    tr = rows if rows * cols <= 512 * 1024 else _tile(rows, max(SUBLANE, (512 * 1024 // cols) // SUBLANE * SUBLANE), SUBLANE)
    c1 = 1.0 - ADAM_B1 ** ADAM_STEP
    c2 = 1.0 - ADAM_B2 ** ADAM_STEP

    def body(w_ref, g_ref, m_ref, v_ref, d_ref, mo_ref, vo_ref):
        gv = g_ref[...]
        mn = ADAM_B1 * m_ref[...] + (1.0 - ADAM_B1) * gv
        vn = ADAM_B2 * v_ref[...] + (1.0 - ADAM_B2) * (gv * gv)
        d_ref[...] = -ADAM_LR * ((mn / c1) / (jnp.sqrt(vn / c2) + ADAM_EPS) + ADAM_WD * w_ref[...])
        mo_ref[...] = mn
        vo_ref[...] = vn

    spec = pl.BlockSpec((1, tr, cols), lambda i: (0, i, 0))
    outs = pl.pallas_call(
        body, name=name,
        out_shape=tuple(jax.ShapeDtypeStruct((1, rows, cols), F32) for _ in range(3)),
        grid=(rows // tr,),
        in_specs=[spec] * 4, out_specs=(spec,) * 3,
        compiler_params=_params(("parallel",)),
    )(w2, g2, m2, v2)
    return tuple(o.reshape(shape) for o in outs)


def _prep_weights(w_in, w_uq):
    offs = np.cumsum((0,) + IN_SPLITS)
    pad = lambda t: jnp.pad(t, ((0, 0), (0, LANE - t.shape[1])))
    pieces = dict(
        qkv=w_in[:, offs[0]:offs[2]],
        kr=pad(w_in[:, offs[2]:offs[3]]),
        z=w_in[:, offs[3]:offs[4]],
        xbc=w_in[:, offs[4]:offs[5]],
        dt=pad(w_in[:, offs[5]:offs[6]]),
        g=w_in[:, offs[6]:offs[8]],
    )
    uq = w_uq.reshape(Q_LORA, N_HEADS_MLA, QK_DIM)
    uq = jnp.pad(uq, ((0, 0), (0, 0), (0, HEAD_PAD - QK_DIM))).reshape(Q_LORA, N_HEADS_MLA * HEAD_PAD)
    return pieces, uq


def _local_step(x, p, positions, ex, sp, target):
    W = gw = ex
    S = x.shape[0]
    tile = min(ATTN_TILE, S)
    pos_col = positions.reshape(S, 1)
    invf = ROPE_THETA ** (-jnp.arange(0, QK_ROPE, 2, dtype=F32) / QK_ROPE)
    invf = jnp.pad(jnp.concatenate([invf, invf]), (0, LANE - QK_ROPE)).reshape(1, LANE)
    wp, w_uq_p = _prep_weights(W["w_in"], W["w_uq"])
    padl = lambda t: jnp.pad(t, ((0, 0), (0, LANE - t.shape[1])))
    dt_bias_p, a_log_p = padl(sp["dt_bias"]), padl(sp["a_log"])
    dskip_ch = jnp.repeat(sp["d_skip"], SSM_HEADDIM, axis=1)
    p_bf = p.astype(BF16)
    RW = 256

    (u_bf,) = _rows(lambda i, n, x, g: (x * _rstd(x) * g,), S, RW, [_row(x), _full(sp["mix_norm_pre"])],
                    [("row", D_MODEL, BF16)], "norm_pre")
    cqkv = ex.mm(u_bf, wp["qkv"], name="mm_qkv")
    z = ex.mm(u_bf, wp["z"], name="mm_z")
    xbc = ex.mm(u_bf, wp["xbc"], name="mm_xbc")
    gates = ex.mm(u_bf, wp["g"], name="mm_gates")
    kr_pad = ex.mm(u_bf, wp["kr"], name="mm_kr")
    dt_raw = ex.mm(u_bf, wp["dt"], name="mm_dt")

    def qkv_norm(i, n, cq, ckv, gq, gkv):
        return cq * _rstd(cq) * gq, ckv * _rstd(ckv) * gkv

    cqn, ckvn = _rows(qkv_norm, S, 512, [_row(cqkv, Q_LORA, 0), _row(cqkv, KV_LORA, 1), _full(sp["q_norm"]), _full(sp["kv_norm"])],
                      [("row", Q_LORA, BF16), ("row", KV_LORA, BF16)], "qkv_norm")
    q_raw = ex.mm(cqn, w_uq_p, name="mm_uq")
    kv = ex.mm(ckvn, W["w_ukv"], out_dtype=BF16, name="mm_ukv")
    q_bf, kp_bf = _rope_fwd(q_raw, kr_pad, pos_col, invf)
    attn, lse = _attn_fwd(q_bf, kv, kp_bf, tile)

    xbc_c = _conv_fwd(xbc, sp["conv_w"], sp["conv_b"])
    dt, cum, cumt = _dt_prep(dt_raw, dt_bias_p, a_log_p)
    cumt_g = cumt[:N_HEADS_SSM].reshape(SSM_GROUPS, HEADS_PER_GROUP, S)
    spread = _spread_matrix()
    y, states = _ssd_fwd(xbc_c, dt, cum, cumt_g, spread)

    GN = D_INNER // SSM_GROUPS

    def gated(y, xs, z, dsk):
        yt = y + dsk * xs
        sz = _sigmoid(z)
        return yt, sz, yt * (z * sz)

    def gated_norm(i, n, y, xs, z, dsk, gn):
        _, _, yg = gated(y, xs, z, dsk)
        parts = []
        for g in range(SSM_GROUPS):
            blk = yg[:, g * GN:(g + 1) * GN]
            parts.append(blk * _rstd(blk) * gn[:, g * GN:(g + 1) * GN])
        return (jnp.concatenate(parts, axis=1),)

    (ssm,) = _rows(gated_norm, S, 128, [_row(y), _row(xbc_c, D_INNER, 0), _row(z), _full(dskip_ch), _full(sp["ssm_norm"])],
                   [("row", D_INNER, BF16)], "gated_norm")

    a_o = ex.mm(attn, W["w_attn_o"], name="mm_attn_o")
    b_o = ex.mm(ssm, W["w_ssm_o"], name="mm_ssm_o")

    def mix(i, n, ga, gs, a, b):
        return (_sigmoid(ga) * a + _sigmoid(gs) * b,)

    (mixed,) = _rows(mix, S, RW, [_row(gates, D_MODEL, 0), _row(gates, D_MODEL, 1), _row(a_o), _row(b_o)],
                     [("row", D_MODEL, BF16)], "mix")
    m2 = ex.mm(mixed, W["w_out"], name="mm_out")

    def post(i, n, h, m, gpost, gpre):
        hn = h + m * _rstd(m) * gpost
        return hn, hn * _rstd(hn) * gpre

    h1, f_bf = _rows(post, S, RW, [_row(x), _row(m2), _full(sp["mix_norm_post"]), _full(sp["ffn_norm_pre"])],
                     [("row", D_MODEL, F32), ("row", D_MODEL, BF16)], "post_mix")
    ga, up, s_bf = _mm_swiglu(f_bf, W["w_gate"], W["w_up"])
    f2 = ex.mm(s_bf, W["w_down"], name="mm_down")
    h2, n3_bf = _rows(post, S, RW, [_row(h1), _row(f2), _full(sp["ffn_norm_post"]), _full(sp["ple_norm_pre"])],
                      [("row", D_MODEL, F32), ("row", D_MODEL, BF16)], "post_ffn")
    gpre = ex.mm(n3_bf, W["w_ple_gate"], name="mm_ple_gate")
    pe = ex.mm(p_bf, W["w_ple"], name="mm_ple")

    def ple_loss(i, n, h2, gpre, pe, tgt, gpost):
        gate = _sigmoid(gpre)
        e = pe * gate
        r = _rstd(e)
        diff = h2 + e * r * gpost - tgt
        loss = 0.5 * jnp.sum(jnp.mean(diff * diff, axis=1, keepdims=True))
        dh3 = diff * (1.0 / D_MODEL)
        de, dg_rows = _norm_bwd(e, r, gpost, dh3)
        return (jnp.full((1, LANE), loss, F32), dh3, de * gate, de * pe * gate * (1.0 - gate), _colsum(dg_rows))

    loss, dh3, dpe, dgpre, g_ple_post = _rows(
        ple_loss, S, 128, [_row(h2), _row(gpre), _row(pe), _row(target), _full(sp["ple_norm_post"])],
        [("acc", (1, LANE), F32), ("row", D_MODEL, F32), ("row", D_MODEL, BF16), ("row", D_MODEL, BF16),
         ("acc", (1, D_MODEL), F32)], "ple_loss")

    gs = {"ple_norm_post": g_ple_post}
    gw["w_ple"] = ex.mm(p_bf, dpe, ta=True, name="mmg_ple")
    gw["w_ple_gate"] = ex.mm(n3_bf, dgpre, ta=True, name="mmg_ple_gate")
    dn3 = ex.mm(dgpre, W["w_ple_gate"], tb=True, name="mmb_ple_gate")

    def post_bwd(i, n, h, m, dhn, dn, gpost, gpre):
        rm = _rstd(m)
        hn = h + m * rm * gpost
        dx, dgpre_rows = _norm_bwd(hn, _rstd(hn), gpre, dn)
        dhn_t = dhn + dx
        dm, dgpost_rows = _norm_bwd(m, rm, gpost, dhn_t)
        return dhn_t, dm, _colsum(dgpre_rows), _colsum(dgpost_rows)

    def run_post_bwd(h, m, dhn, dn, gpost, gpre, name):
        return _rows(post_bwd, S, 128, [_row(h), _row(m), _row(dhn), _row(dn), _full(gpost), _full(gpre)],
                     [("row", D_MODEL, F32), ("row", D_MODEL, BF16), ("acc", (1, D_MODEL), F32), ("acc", (1, D_MODEL), F32)], name)

    dh2, df2, gs["ple_norm_pre"], gs["ffn_norm_post"] = run_post_bwd(
        h1, f2, dh3, dn3, sp["ffn_norm_post"], sp["ple_norm_pre"], "post_ffn_bwd")
    gw["w_down"] = ex.mm(s_bf, df2, ta=True, name="mmg_down")
    ds = ex.mm(df2, W["w_down"], tb=True, out_dtype=BF16, name="mmb_down")

    def swiglu_bwd(i, n, a, b, ds):
        sa = _sigmoid(a)
        return ds * b * (sa * (1.0 + a * (1.0 - sa))), ds * (a * sa)

    dga, dup = _rows(swiglu_bwd, S, RW, [_row(ga), _row(up), _row(ds)], [("row", D_FF, BF16), ("row", D_FF, BF16)], "swiglu_bwd")
    gw["w_gate"] = ex.mm(f_bf, dga, ta=True, name="mmg_gate")
    gw["w_up"] = ex.mm(f_bf, dup, ta=True, name="mmg_up")
    df = ex.mm(dga, W["w_gate"], tb=True, name="mmb_gate")
    df = ex.mm(dup, W["w_up"], tb=True, add=df, name="mmb_up")
    dh1, dm2, gs["ffn_norm_pre"], gs["mix_norm_post"] = run_post_bwd(
        x, m2, dh2, df, sp["mix_norm_post"], sp["ffn_norm_pre"], "post_mix_bwd")
    gw["w_out"] = ex.mm(mixed, dm2, ta=True, name="mmg_out")
    dmixed = ex.mm(dm2, W["w_out"], tb=True, out_dtype=BF16, name="mmb_out")

    def mix_bwd(i, n, ga, gs_, a, b, dm):
        sa, ss = _sigmoid(ga), _sigmoid(gs_)
        return dm * sa, dm * ss, jnp.concatenate([dm * a * sa * (1.0 - sa), dm * b * ss * (1.0 - ss)], axis=1)

    da_o, db_o, dgates = _rows(mix_bwd, S, RW, [_row(gates, D_MODEL, 0), _row(gates, D_MODEL, 1), _row(a_o), _row(b_o), _row(dmixed)],
                               [("row", D_MODEL, BF16), ("row", D_MODEL, BF16), ("row", 2 * D_MODEL, BF16)], "mix_bwd")
    gw["w_attn_o"] = ex.mm(attn, da_o, ta=True, name="mmg_attn_o")
    dattn = ex.mm(da_o, W["w_attn_o"], tb=True, out_dtype=BF16, name="mmb_attn_o")
    gw["w_ssm_o"] = ex.mm(ssm, db_o, ta=True, name="mmg_ssm_o")
    dssm = ex.mm(db_o, W["w_ssm_o"], tb=True, out_dtype=BF16, name="mmb_ssm_o")

    delta = _attn_delta(attn, dattn, tile)
    dq, dkv, dkp = _attn_bwd(q_bf, kv, kp_bf, dattn, lse, delta, tile)
    dq_raw, dkr = _rope_bwd(dq, dkp, pos_col, invf)
    g_uq_p = ex.mm(cqn, dq_raw, ta=True, name="mmg_uq")
    gw["w_uq"] = g_uq_p.reshape(Q_LORA, N_HEADS_MLA, HEAD_PAD)[:, :, :QK_DIM].reshape(Q_LORA, N_HEADS_MLA * QK_DIM)
    dcqn = ex.mm(dq_raw, w_uq_p, tb=True, name="mmb_uq")
    gw["w_ukv"] = ex.mm(ckvn, dkv, ta=True, name="mmg_ukv")
    dckvn = ex.mm(dkv, W["w_ukv"], tb=True, name="mmb_ukv")

    def qkv_norm_bwd(i, n, cq, ckv, dq_, dkv_, gq, gkv):
        dcq, gq_rows = _norm_bwd(cq, _rstd(cq), gq, dq_)
        dckv, gkv_rows = _norm_bwd(ckv, _rstd(ckv), gkv, dkv_)
        return jnp.concatenate([dcq, dckv], axis=1), _colsum(gq_rows), _colsum(gkv_rows)

    dcqkv, gs["q_norm"], gs["kv_norm"] = _rows(
        qkv_norm_bwd, S, 512, [_row(cqkv, Q_LORA, 0), _row(cqkv, KV_LORA, 1), _row(dcqn), _row(dckvn), _full(sp["q_norm"]), _full(sp["kv_norm"])],
        [("row", Q_LORA + KV_LORA, BF16), ("acc", (1, Q_LORA), F32), ("acc", (1, KV_LORA), F32)], "qkv_norm_bwd")

    def gated_norm_bwd(i, n, y, xs, z, dssm, dsk, gn):
        yt, sz, yg = gated(y, xs, z, dsk)
        dyg_parts, gn_parts = [], []
        for g in range(SSM_GROUPS):
            sl = slice(g * GN, (g + 1) * GN)
            blk = yg[:, sl]
            dblk, rows = _norm_bwd(blk, _rstd(blk), gn[:, sl], dssm[:, sl])
            dyg_parts.append(dblk)
            gn_parts.append(_colsum(rows))
        dyg = jnp.concatenate(dyg_parts, axis=1)
        dyt = dyg * (z * sz)
        dz = dyg * yt * (sz * (1.0 + z * (1.0 - sz)))
        return dyt, dz, jnp.concatenate(gn_parts, axis=1), _colsum(dyt * xs)

    dy, dz, gs["ssm_norm"], g_dskip_ch = _rows(
        gated_norm_bwd, S, 128, [_row(y), _row(xbc_c, D_INNER, 0), _row(z), _row(dssm), _full(dskip_ch), _full(sp["ssm_norm"])],
        [("row", D_INNER, F32), ("row", D_INNER, BF16), ("acc", (1, D_INNER), F32), ("acc", (1, D_INNER), F32)],
        "gated_norm_bwd")
    gs["d_skip"] = jnp.sum(g_dskip_ch.reshape(N_HEADS_SSM, SSM_HEADDIM), axis=1).reshape(1, N_HEADS_SSM)
    dxs, dbm, dcm, ddt_x, dcum = _ssd_bwd(xbc_c, dt, cum, cumt_g, spread, states, dy, dskip_ch)
    ddt_raw, g_dtb, g_alog = _dt_bwd(dt_raw, dt_bias_p, a_log_p, ddt_x, dcum)
    gs["dt_bias"] = g_dtb[:, :N_HEADS_SSM]
    gs["a_log"] = g_alog[:, :N_HEADS_SSM]
    dxbc, g_conv_w8, gs["conv_b"] = _conv_bwd(xbc, [dxs, dbm, dcm], sp["conv_w"], sp["conv_b"])
    gs["conv_w"] = g_conv_w8[:CONV_WIDTH]

    g_qkv = ex.mm(u_bf, dcqkv, ta=True, name="mmg_qkv")
    g_kr = ex.mm(u_bf, dkr, ta=True, name="mmg_kr")
    g_z = ex.mm(u_bf, dz, ta=True, name="mmg_z")
    g_xbc = ex.mm(u_bf, dxbc, ta=True, name="mmg_xbc")
    g_dt = ex.mm(u_bf, ddt_raw, ta=True, name="mmg_dt")
    g_g = ex.mm(u_bf, dgates, ta=True, name="mmg_gates")
    gw["w_in"] = [g_qkv, g_kr[:, :QK_ROPE], g_z, g_xbc, g_dt[:, :N_HEADS_SSM], g_g]
    du = ex.mm(dcqkv, wp["qkv"], tb=True, name="mmb_qkv")
    du = ex.mm(dkr, wp["kr"], tb=True, add=du, name="mmb_kr")
    du = ex.mm(ddt_raw, wp["dt"], tb=True, add=du, name="mmb_dt")
    du = ex.mm(dz, wp["z"], tb=True, add=du, name="mmb_z")
    du = ex.mm(dxbc, wp["xbc"], tb=True, add=du, name="mmb_xbc")
    du = ex.mm(dgates, wp["g"], tb=True, add=du, name="mmb_gates")

    def pre_bwd(i, n, x, du, dh, g):
        dx, rows = _norm_bwd(x, _rstd(x), g, du)
        return dh + dx, _colsum(rows)

    grad_x, gs["mix_norm_pre"] = _rows(pre_bwd, S, RW, [_row(x), _row(du), _row(dh1), _full(sp["mix_norm_pre"])],
                                       [("row", D_MODEL, F32), ("acc", (1, D_MODEL), F32)], "norm_pre_bwd")
    return loss, grad_x, gs


BIG = (
    ("w_in", (2048, 3872), 1), ("w_uq", (512, 768), 1), ("w_ukv", (512, 1024), 1), ("w_attn_o", (512, 2048), 0),
    ("w_ssm_o", (1024, 2048), 0), ("w_out", (512, 2048), 0), ("w_gate", (2048, 1408), 1), ("w_up", (2048, 1408), 1),
    ("w_down", (1408, 2048), 0), ("w_ple_gate", (512, 2048), 0), ("w_ple", (256, 512), 1),
)
SMALL = (
    ("mix_norm_pre", 2048), ("mix_norm_post", 2048), ("q_norm", 512), ("kv_norm", 512), ("conv_b", 6144), ("dt_bias", 64),
    ("a_log", 64), ("d_skip", 64), ("ssm_norm", 4096), ("ffn_norm_pre", 2048), ("ffn_norm_post", 2048),
    ("ple_norm_pre", 2048), ("ple_norm_post", 2048),
)
CONV_W_LEN = CONV_WIDTH * CONV_DIM
SMALL_ROWS = 384


def _place():
    return lax.axis_index("x"), lax.axis_index("y"), lax.axis_index("c")


def _flip(v, bit):
    return 1 - v if bit else v


def _alone(hook, name):
    n_in, n_out = len(hook.ins), len(hook.out_shapes)

    def body(*refs):
        start, finish = hook.make(refs[:n_in], refs[n_in:n_in + n_out], refs[n_in + n_out:])
        start()
        finish()

    return list(pl.pallas_call(
        body, name=name, out_shape=tuple(hook.out_shapes),
        in_specs=[pl.BlockSpec(memory_space=pl.ANY)] * n_in,
        out_specs=tuple(pl.BlockSpec(memory_space=pl.ANY) for _ in range(n_out)),
        scratch_shapes=[pltpu.SemaphoreType.DMA((s,)) for s in hook.sems],
        input_output_aliases=hook.aliases,
    )(*hook.ins))


def _simple(copies):
    def start():
        for cp in copies:
            cp.start()

    def finish():
        for cp in copies:
            cp.wait()

    return start, finish


def _gather_hook(shards):
    n = len(shards)

    def make(ins, outs, sems):
        send_sems, recv_sems, fwd_send_sems, fwd_recv_sems = sems
        x, y, c = _place()
        me = 2 * x + y
        far, near = [], []
        for a in range(n):
            half = shards[a].shape[0] // 2
            lo = pl.multiple_of(c * half, SUBLANE)
            for k in (1, 2, 3):
                px, py = _flip(x, k >> 1), _flip(y, k & 1)
                far.append(pltpu.make_async_remote_copy(
                    src_ref=ins[a].at[pl.ds(lo, half), :], dst_ref=outs[a].at[me, pl.ds(lo, half), :],
                    send_sem=send_sems.at[3 * a + k - 1], recv_sem=recv_sems.at[3 * a + k - 1],
                    device_id=(px, py, c), device_id_type=MESH_ID))
                got = outs[a].at[2 * px + py, pl.ds(lo, half), :]
                near.append(pltpu.make_async_remote_copy(
                    src_ref=got, dst_ref=got, send_sem=fwd_send_sems.at[3 * a + k - 1], recv_sem=fwd_recv_sems.at[3 * a + k - 1],
                    device_id=(x, y, 1 - c), device_id_type=MESH_ID))

        def start():
            for cp in far:
                cp.start()

        def finish():
            for cp, fwd in zip(far, near):
                cp.wait_recv()
                fwd.start()
            for cp, fwd in zip(far, near):
                cp.wait_send()
                fwd.wait()

        return start, finish

    return _Hook(shards, [jax.ShapeDtypeStruct((N_CHIPS, *s.shape), s.dtype) for s in shards], (3 * n,) * 4, make)


def _swap_hook(gs):
    n = len(gs)

    def make(ins, outs, sems):
        send_sems, recv_sems = sems
        x, y, c = _place()
        copies = []
        for a in range(n):
            half = gs[a].shape[1] // 2
            src = ins[a].at[:, pl.ds(pl.multiple_of((1 - c) * half, SUBLANE), half), :]
            copies.append(pltpu.make_async_remote_copy(
                src_ref=src, dst_ref=outs[a], send_sem=send_sems.at[a], recv_sem=recv_sems.at[a],
                device_id=(x, y, 1 - c), device_id_type=MESH_ID))
        return _simple(copies)

    return _Hook(gs, [jax.ShapeDtypeStruct((g.shape[0], g.shape[1] // 2, g.shape[2]), g.dtype) for g in gs], (n, n), make)


def _sum_rows_tile(rows, cols):
    return _tile(rows, max(2 * SUBLANE, (512 * 1024 // cols) // (2 * SUBLANE) * (2 * SUBLANE)), 2 * SUBLANE)


def _add_half(g, other, c, name):
    n, R, C = g.shape
    half = R // 2
    tr = _sum_rows_tile(half, C)
    nb = half // tr

    def body(c_ref, g_ref, o_ref, out_ref):
        out_ref[...] = (g_ref[...] + o_ref[...]).astype(out_ref.dtype)

    return pl.pallas_call(
        body, name=name,
        out_shape=jax.ShapeDtypeStruct((n, half, C), BF16),
        grid_spec=pltpu.PrefetchScalarGridSpec(
            num_scalar_prefetch=1, grid=(n, nb),
            in_specs=[pl.BlockSpec((1, tr, C), lambda j, i, c_ref: (j, c_ref[0] * nb + i, 0)),
                      pl.BlockSpec((1, tr, C), lambda j, i, c_ref: (j, i, 0))],
            out_specs=pl.BlockSpec((1, tr, C), lambda j, i, c_ref: (j, i, 0))),
        compiler_params=_params(("parallel", "parallel")),
    )(c, g, other)


def _scatter_hook(parts):
    n = len(parts)

    def make(ins, outs, sems):
        send_sems, recv_sems = sems
        x, y, c = _place()
        copies = []
        for a in range(n):
            for k in (1, 2, 3):
                px, py = _flip(x, k >> 1), _flip(y, k & 1)
                copies.append(pltpu.make_async_remote_copy(
                    src_ref=ins[a].at[2 * px + py], dst_ref=outs[a].at[k - 1], send_sem=send_sems.at[3 * a + k - 1],
                    recv_sem=recv_sems.at[3 * a + k - 1], device_id=(px, py, c), device_id_type=MESH_ID))
        return _simple(copies)

    return _Hook(parts, [jax.ShapeDtypeStruct((3, *p.shape[1:]), p.dtype) for p in parts], (3 * n, 3 * n), make)


def _add_chips(part, got, place, name):
    n, R, C = part.shape
    tr = _sum_rows_tile(R, C)
    nb = R // tr

    def body(place_ref, p_ref, g_ref, out_ref):
        out_ref[...] = ((p_ref[0].astype(F32) + g_ref[0].astype(F32)) + g_ref[1].astype(F32)) + g_ref[2].astype(F32)

    return pl.pallas_call(
        body, name=name,
        out_shape=jax.ShapeDtypeStruct((2 * R, C), F32),
        grid_spec=pltpu.PrefetchScalarGridSpec(
            num_scalar_prefetch=1, grid=(nb,),
            in_specs=[pl.BlockSpec((1, tr, C), lambda i, place_ref: (place_ref[0], i, 0)),
                      pl.BlockSpec((3, tr, C), lambda i, place_ref: (0, i, 0))],
            out_specs=pl.BlockSpec((tr, C), lambda i, place_ref: (place_ref[1] * nb + i, 0))),
        compiler_params=_params(("parallel",)),
    )(place, part, got)


def _join_hook(wholes):
    n = len(wholes)

    def make(ins, outs, sems):
        send_sems, recv_sems = sems
        x, y, c = _place()
        copies = []
        for a in range(n):
            half = wholes[a].shape[0] // 2
            rows = outs[a].at[pl.ds(pl.multiple_of(c * half, SUBLANE), half), :]
            copies.append(pltpu.make_async_remote_copy(
                src_ref=rows, dst_ref=rows, send_sem=send_sems.at[a], recv_sem=recv_sems.at[a],
                device_id=(x, y, 1 - c), device_id_type=MESH_ID))
        return _simple(copies)

    return _Hook(wholes, [jax.ShapeDtypeStruct(w.shape, w.dtype) for w in wholes], (n, n), make, aliases={a: a for a in range(n)})


def _allreduce_small(vec, name):
    R, C = vec.shape

    def body(v_ref, o_ref, buf, send_sems, recv_sems):
        x, y, c = _place()
        me = 4 * x + 2 * y + c
        buf[me] = v_ref[...]
        copies = []
        for k in range(1, N_DEV):
            peer = (_flip(x, (k >> 2) & 1), _flip(y, (k >> 1) & 1), _flip(c, k & 1))
            copies.append(pltpu.make_async_remote_copy(
                src_ref=v_ref, dst_ref=buf.at[me], send_sem=send_sems.at[k - 1], recv_sem=recv_sems.at[k - 1],
                device_id=peer, device_id_type=MESH_ID))
        for cp in copies:
            cp.start()
        for cp in copies:
            cp.wait()
        tot = buf[0]
        for d in range(1, N_DEV):
            tot = tot + buf[d]
        o_ref[...] = tot

    return pl.pallas_call(
        body, name=name,
        out_shape=jax.ShapeDtypeStruct((R, C), F32),
        in_specs=[pl.BlockSpec(memory_space=pltpu.VMEM)],
        out_specs=pl.BlockSpec(memory_space=pltpu.VMEM),
        scratch_shapes=[pltpu.VMEM((N_DEV, R, C), F32), pltpu.SemaphoreType.DMA((N_DEV - 1,)), pltpu.SemaphoreType.DMA((N_DEV - 1,))],
    )(vec)


def _unstack(gathered, shape, axis):
    if axis == 0:
        return gathered.reshape(N_CHIPS * shape[0], shape[1])
    return jnp.concatenate([gathered[j] for j in range(N_CHIPS)], axis=1)


def _stack(whole, shape, axis):
    if axis == 0:
        return whole.reshape(N_CHIPS, shape[0], shape[1])
    pieces = whole if isinstance(whole, (list, tuple)) else [whole]
    shards = []
    for j in range(N_CHIPS):
        lo, hi, off, cols = j * shape[1], (j + 1) * shape[1], 0, []
        for p in pieces:
            a, b = max(lo, off), min(hi, off + p.shape[1])
            if a < b:
                cols.append(p[:, a - off:b - off])
            off += p.shape[1]
        shards.append(cols[0] if len(cols) == 1 else jnp.concatenate(cols, axis=1))
    return jnp.stack(shards)


GATHER_FIRST = ("w_in", "w_uq", "w_ukv")
GATHER_IN = {"mm_z": ("w_attn_o", "w_ssm_o", "w_out"), "mm_xbc": ("w_gate", "w_up"), "mm_gates": ("w_down", "w_ple_gate", "w_ple")}
REDUCE = (
    (("w_ple", "w_ple_gate", "w_down"), "mmb_down", "mmg_gate", "mmg_up"),
    (("w_gate", "w_up"), "mmb_gate", "mmb_up", "mmg_out"),
    (("w_out", "w_attn_o", "w_ssm_o"), "mmb_ssm_o", "mmg_z", "mmg_xbc"),
    (("w_uq", "w_ukv"), "mmb_ukv", "mmg_gates", "mmb_z"),
    (("w_in",), "mmb_z", "mmb_xbc", "mmb_gates"),
)


class _Exchange:
    def __init__(self, shards, chip, core):
        self.shards, self.chip = shards, chip
        self.core_arr = core.reshape(1).astype(jnp.int32)
        self.place_arr = jnp.stack([chip, core]).astype(jnp.int32)
        self.shape = {n: (shape, axis) for n, shape, axis in BIG}
        self.whole, self.grads, self.reduced, self.pending, self.tails = {}, {}, {}, {}, 0
        hook, done = self._gather(GATHER_FIRST)
        done(_alone(hook, "gather_first"))
        for host, names in GATHER_IN.items():
            self._arm(host, *self._gather(names))

    def _arm(self, host, hook, done):
        self.pending.setdefault(host, []).append((hook, done))

    def _gather(self, names):
        shards = [self.shards[n].astype(BF16) for n in names]

        def done(outs):
            for n, s, g in zip(names, shards, outs):
                self.whole[n] = _unstack(lax.dynamic_update_slice(g, s[None], (self.chip, 0, 0)), *self.shape[n])

        return _gather_hook(shards), done

    def __getitem__(self, name):
        return self.whole[name]

    def __setitem__(self, name, grad):
        self.grads[name] = grad
        for names, swap_host, scatter_host, join_host in REDUCE:
            if name in names and all(n in self.grads for n in names):
                self._reduce(names, swap_host, scatter_host, join_host)

    def _reduce(self, names, swap_host, scatter_host, join_host):
        stacked = [_stack(self.grads[n], *self.shape[n]) for n in names]

        def joined(outs):
            for n, r in zip(names, outs):
                self.reduced[n] = r.reshape(1, *self.shape[n][0])

        def swapped(outs):
            parts = [_add_half(g, o, self.core_arr, "add_half_" + n) for n, g, o in zip(names, stacked, outs)]

            def scattered(gots):
                wholes = [_add_chips(q, o, self.place_arr, "add_chips_" + n) for n, q, o in zip(names, parts, gots)]
                self._arm(join_host, _join_hook(wholes), joined)

            self._arm(scatter_host, _scatter_hook(parts), scattered)

        self._arm(swap_host, _swap_hook(stacked), swapped)

    def _run(self, todo, call):
        hook = _merge_hooks([h for h, _ in todo])
        result, outs = call(hook)
        off = 0
        for h, done in todo:
            done(outs[off:off + len(h.out_shapes)])
            off += len(h.out_shapes)
        return result

    def mm(self, a, b, *, name, **kw):
        todo = self.pending.pop(name, None)
        if not todo:
            return _mm(a, b, name=name, **kw)
        return self._run(todo, lambda hook: _mm(a, b, name=name, hook=hook, **kw))

    def finish(self):
        while self.pending:
            todo = self.pending.pop(next(iter(self.pending)))
            self.tails += 1
            self._run(todo, lambda hook: (None, _alone(hook, "exchange_tail_%d" % self.tails)))
        return self.reduced


def kernel(x, p, positions, mix_norm_pre, mix_norm_post, w_in, q_norm, w_uq, kv_norm, w_ukv, conv_w, conv_b, dt_bias, a_log, d_skip, ssm_norm, w_attn_o, w_ssm_o, w_out, ffn_norm_pre, ffn_norm_post, w_gate, w_up, w_down, ple_norm_pre, ple_norm_post, w_ple_gate, w_ple, loss_target, m_mix_norm_pre, m_mix_norm_post, m_w_in, m_q_norm, m_w_uq, m_kv_norm, m_w_ukv, m_conv_w, m_conv_b, m_dt_bias, m_a_log, m_d_skip, m_ssm_norm, m_w_attn_o, m_w_ssm_o, m_w_out, m_ffn_norm_pre, m_ffn_norm_post, m_w_gate, m_w_up, m_w_down, m_ple_norm_pre, m_ple_norm_post, m_w_ple_gate, m_w_ple, v_mix_norm_pre, v_mix_norm_post, v_w_in, v_q_norm, v_w_uq, v_kv_norm, v_w_ukv, v_conv_w, v_conv_b, v_dt_bias, v_a_log, v_d_skip, v_ssm_norm, v_w_attn_o, v_w_ssm_o, v_w_out, v_ffn_norm_pre, v_ffn_norm_post, v_w_gate, v_w_up, v_w_down, v_ple_norm_pre, v_ple_norm_post, v_w_ple_gate, v_w_ple):
    given = dict(locals())
    names = [n for n, _, _ in BIG] + [n for n, _ in SMALL] + ["conv_w"]
    order = ["mix_norm_pre", "mix_norm_post", "w_in", "q_norm", "w_uq", "kv_norm", "w_ukv", "conv_w", "conv_b", "dt_bias", "a_log",
             "d_skip", "ssm_norm", "w_attn_o", "w_ssm_o", "w_out", "ffn_norm_pre", "ffn_norm_post", "w_gate", "w_up", "w_down",
             "ple_norm_pre", "ple_norm_post", "w_ple_gate", "w_ple"]
    assert sorted(names) == sorted(order)
    cx, cy, cc = _place()
    chip = 2 * cx + cy
    conv_cols = CONV_DIM // N_CHIPS

    ex = _Exchange({n: given[n][0] for n, _, _ in BIG}, chip, cc)
    own = jnp.where(cc == 0, conv_w[0], 0.0)
    conv_vec = lax.dynamic_update_slice(jnp.zeros((CONV_WIDTH, CONV_DIM), F32), own, (0, chip * conv_cols))
    conv_full = _allreduce_small(conv_vec.reshape(CONV_W_LEN // LANE, LANE), "gather_conv_w").reshape(CONV_WIDTH, CONV_DIM)
    sp = {n: given[n] for n, _ in SMALL}
    sp["conv_w"] = conv_full

    loss_part, grad_x, gs = _local_step(x[0], p[0, 0], positions[0], ex, sp, loss_target[0])

    g_big = ex.finish()

    small_parts = [gs[n] for n, _ in SMALL] + [gs["conv_w"], loss_part[:, :1]]
    small_vec = jnp.concatenate([t.reshape(-1) for t in small_parts])
    small_vec = jnp.pad(small_vec, (0, SMALL_ROWS * LANE - small_vec.shape[0])).reshape(SMALL_ROWS, LANE)
    small_sum = _allreduce_small(small_vec, "allreduce_small").reshape(-1)
    g_small, off = {}, 0
    for n, length in SMALL:
        g_small[n] = small_sum[off:off + length].reshape(1, length)
        off += length
    g_conv = small_sum[off:off + CONV_W_LEN].reshape(CONV_WIDTH, CONV_DIM)
    g_small["conv_w"] = lax.dynamic_slice(g_conv, (0, chip * conv_cols), (CONV_WIDTH, conv_cols)).reshape(1, CONV_WIDTH, conv_cols)
    loss = small_sum[off + CONV_W_LEN]

    grads, deltas, new_m, new_v = [], [], [], []
    for n in order:
        g = g_big[n] if n in g_big else g_small[n]
        d, m_, v_ = _adamw(given[n], g, given["m_" + n], given["v_" + n], "adamw_" + n)
        grads.append(g)
        deltas.append(d)
        new_m.append(m_)
        new_v.append(v_)
    return (loss, grad_x.reshape(x.shape), *grads, *deltas, *new_m, *new_v)
```

```python
import numpy as np
import jax
import jax.numpy as jnp
from jax import lax
from jax.experimental import pallas as pl
from jax.experimental.pallas import tpu as pltpu

F32 = jnp.float32
BF16 = jnp.bfloat16

D_MODEL = 2048
N_HEADS_MLA = 16
Q_LORA = 512
KV_LORA = 512
QK_NOPE = 128
QK_ROPE = 64
V_DIM = 128
QK_DIM = QK_NOPE + QK_ROPE
ROPE_THETA = 10000.0
D_INNER = 4096
SSM_HEADDIM = 64
N_HEADS_SSM = 64
SSM_GROUPS = 8
HEADS_PER_GROUP = 8
D_STATE = 128
CONV_WIDTH = 4
CHUNK = 256
CONV_DIM = D_INNER + 2 * SSM_GROUPS * D_STATE
D_FF = 5632
PLE_DIM = 256
EPS = 1e-6
IN_SPLITS = (Q_LORA, KV_LORA, QK_ROPE, D_INNER, CONV_DIM, N_HEADS_SSM, D_MODEL, D_MODEL)

ADAM_LR = 0.001
ADAM_B1 = 0.9
ADAM_B2 = 0.999
ADAM_EPS = 1e-08
ADAM_WD = 0.01
ADAM_STEP = 10

LANE = 128
SUBLANE = 8
HEAD_PAD = 256
VMEM_LIMIT = 56 * 1024 * 1024
ATTN_TILE = 1024
NEG = -1e30

MESH_ID = pl.DeviceIdType.MESH
N_CHIPS = 4
N_DEV = 8


def _tile(n, pref, mult=LANE):
    if n <= pref:
        return n
    t = (pref // mult) * mult
    while t >= mult:
        if n % t == 0:
            return t
        t -= mult
    return n


def _params(sem, vmem=VMEM_LIMIT, **kw):
    return pltpu.CompilerParams(dimension_semantics=sem, vmem_limit_bytes=vmem, **kw)


class _Hook:
    def __init__(self, ins, out_shapes, sems, make, aliases=None):
        self.ins, self.out_shapes, self.sems, self.make, self.aliases = list(ins), list(out_shapes), tuple(sems), make, dict(aliases or {})


def _merge_hooks(hooks):
    hooks = [h for h in hooks if h is not None]
    if not hooks:
        return None
    ins, outs, sems, aliases, cuts = [], [], [], {}, []
    for h in hooks:
        cuts.append((len(ins), len(outs), len(sems)))
        aliases.update({len(ins) + i: len(outs) + o for i, o in h.aliases.items()})
        ins += h.ins
        outs += h.out_shapes
        sems += h.sems

    def make(in_refs, out_refs, sem_refs):
        pairs = []
        for h, (i0, o0, s0) in zip(hooks, cuts):
            pairs.append(h.make(in_refs[i0:i0 + len(h.ins)], out_refs[o0:o0 + len(h.out_shapes)], sem_refs[s0:s0 + len(h.sems)]))

        def start():
            for st, _ in pairs:
                st()

        def finish():
            for _, fin in pairs:
                fin()

        return start, finish

    return _Hook(ins, outs, sems, make, aliases)


def _mm(a, b, *, ta=False, tb=False, add=None, out_dtype=F32, name, tm=1024, tn=1536, tk=2048, hook=None, epi=None):
    if ta:
        K, M = a.shape
    else:
        M, K = a.shape
    N = b.shape[0] if tb else b.shape[1]
    assert (b.shape[1] if tb else b.shape[0]) == K, (a.shape, b.shape, ta, tb)
    tm, tn, tk = _tile(M, tm), _tile(N, tn), _tile(K, tk)
    nk = K // tk
    dn = (((0 if ta else 1,), (1 if tb else 0,)), ((), ()))
    has_add = add is not None
    epi_fn, extras, out_dtypes = epi if epi else (None, [], [out_dtype])
    assert not epi or (nk == 1 and not has_add)
    n_own = 2 + has_add + len(extras)
    n_main = len(out_dtypes)
    n_hin = len(hook.ins) if hook else 0
    n_hout = len(hook.out_shapes) if hook else 0
    grid = (M // tm, N // tn, nk)

    def body(*refs):
        a_ref, b_ref = refs[:2]
        c_ref = refs[2] if has_add else None
        main = refs[n_own + n_hin:n_own + n_hin + n_main]
        o_ref = main[0]
        scratch = refs[n_own + n_hin + n_main + n_hout:]
        if hook:
            start, finish = hook.make(refs[n_own:n_own + n_hin], refs[n_own + n_hin + n_main:n_own + n_hin + n_main + n_hout],
                                      scratch[len(scratch) - len(hook.sems):])
            ids = [pl.program_id(d) for d in range(3)]
            pl.when((ids[0] == 0) & (ids[1] == 0) & (ids[2] == 0))(start)
        prod = lax.dot_general(a_ref[...].astype(BF16), b_ref[...].astype(BF16), dn, preferred_element_type=F32)
        if epi:
            for ref, val in zip(main, epi_fn(prod, *[r[...] for r in refs[2 + has_add:n_own]])):
                ref[...] = val.astype(ref.dtype)
        elif nk == 1:
            o_ref[...] = ((c_ref[...] + prod) if has_add else prod).astype(out_dtype)
        else:
            acc = scratch[0]
            k = pl.program_id(2)

            @pl.when(k == 0)
            def _():
                acc[...] = (c_ref[...] + prod) if has_add else prod

            @pl.when(k > 0)
            def _():
                acc[...] += prod

            @pl.when(k == nk - 1)
            def _():
                o_ref[...] = acc[...].astype(out_dtype)
        if hook:
            pl.when((ids[0] == grid[0] - 1) & (ids[1] == grid[1] - 1) & (ids[2] == grid[2] - 1))(finish)

    a_spec = pl.BlockSpec((tk, tm), lambda i, j, k: (k, i)) if ta else pl.BlockSpec((tm, tk), lambda i, j, k: (i, k))
    b_spec = pl.BlockSpec((tn, tk), lambda i, j, k: (j, k)) if tb else pl.BlockSpec((tk, tn), lambda i, j, k: (k, j))
    in_specs = [a_spec, b_spec]
    args = [a, b]
    tile_spec = pl.BlockSpec((tm, tn), lambda i, j, k: (i, j))
    for t in ([add] if has_add else []) + list(extras):
        in_specs.append(tile_spec)
        args.append(t)
    hbm = pl.BlockSpec(memory_space=pl.ANY)
    scratch_shapes = [pltpu.VMEM((tm, tn), F32)] if nk > 1 else []
    out_shapes = [jax.ShapeDtypeStruct((M, N), dt) for dt in out_dtypes]
    if not hook:
        outs = pl.pallas_call(
            body, name=name, out_shape=tuple(out_shapes), grid=grid, in_specs=in_specs, out_specs=(tile_spec,) * n_main,
            scratch_shapes=scratch_shapes,
            compiler_params=_params(("parallel", "parallel", "arbitrary")),
        )(*args)
        return tuple(outs) if epi else outs[0]
    outs = pl.pallas_call(
        body, name=name, out_shape=(*out_shapes, *hook.out_shapes), grid=grid,
        in_specs=in_specs + [hbm] * n_hin, out_specs=(*[tile_spec] * n_main, *[hbm] * n_hout),
        scratch_shapes=scratch_shapes + [pltpu.SemaphoreType.DMA((s,)) for s in hook.sems],
        input_output_aliases={n_own + i: n_main + o for i, o in hook.aliases.items()},
        compiler_params=_params(("arbitrary", "arbitrary", "arbitrary")),
    )(*args, *hook.ins)
    return (tuple(outs[:n_main]) if epi else outs[0]), list(outs[n_main:])


def _mm_swiglu(a, w_gate, w_up, tm=1024, tn=512):
    M, K = a.shape
    N = w_gate.shape[1]
    tm, tn = _tile(M, tm), _tile(N, tn)

    def body(a_ref, g_ref, u_ref, go_ref, uo_ref, s_ref):
        av = a_ref[...].astype(BF16)
        gate = jnp.dot(av, g_ref[...].astype(BF16), preferred_element_type=F32)
        up = jnp.dot(av, u_ref[...].astype(BF16), preferred_element_type=F32)
        go_ref[...] = gate
        uo_ref[...] = up
        s_ref[...] = (gate * _sigmoid(gate) * up).astype(s_ref.dtype)

    w_spec = pl.BlockSpec((K, tn), lambda i, j: (0, j))
    o_spec = pl.BlockSpec((tm, tn), lambda i, j: (i, j))
    return pl.pallas_call(
        body, name="mm_swiglu",
        out_shape=(jax.ShapeDtypeStruct((M, N), F32), jax.ShapeDtypeStruct((M, N), F32), jax.ShapeDtypeStruct((M, N), BF16)),
        grid=(M // tm, N // tn),
        in_specs=[pl.BlockSpec((tm, K), lambda i, j: (i, 0)), w_spec, w_spec],
        out_specs=(o_spec, o_spec, o_spec),
        compiler_params=_params(("parallel", "parallel")),
    )(a, w_gate, w_up)


def _row(arr, width=None, cblk=0):
    return ("row", arr, arr.shape[1] if width is None else width, cblk)


def _full(arr):
    return ("full", arr)


def _prev8(arr):
    return ("prev8", arr)


def _next8(arr):
    return ("next8", arr)


def _rows(fn, n_rows, tm, ins, outs, name):
    tm = min(tm, n_rows)
    assert n_rows % tm == 0 and tm % SUBLANE == 0
    n = n_rows // tm
    in_specs, args = [], []
    for spec in ins:
        kind, arr = spec[0], spec[1]
        if kind == "row":
            _, _, w, cb = spec
            in_specs.append(pl.BlockSpec((tm, w), lambda i, cb=cb: (i, cb)))
        elif kind == "full":
            in_specs.append(pl.BlockSpec(arr.shape, lambda i, nd=arr.ndim: (0,) * nd))
        elif kind == "prev8":
            in_specs.append(pl.BlockSpec((SUBLANE, arr.shape[1]),
                                         lambda i: (jnp.maximum(i * (tm // SUBLANE) - 1, 0), 0)))
        elif kind == "next8":
            last = n_rows // SUBLANE - 1
            in_specs.append(pl.BlockSpec((SUBLANE, arr.shape[1]),
                                         lambda i: (jnp.minimum((i + 1) * (tm // SUBLANE), last), 0)))
        else:
            raise ValueError(kind)
        args.append(arr)
    out_shapes, out_specs = [], []
    any_acc = False
    for spec in outs:
        if spec[0] == "row":
            _, w, dt = spec
            out_shapes.append(jax.ShapeDtypeStruct((n_rows, w), dt))
            out_specs.append(pl.BlockSpec((tm, w), lambda i: (i, 0)))
        else:
            _, shp, dt = spec
            any_acc = True
            out_shapes.append(jax.ShapeDtypeStruct(shp, dt))
            out_specs.append(pl.BlockSpec(shp, lambda i, nd=len(shp): (0,) * nd))
    nin = len(ins)

    def body(*refs):
        i = pl.program_id(0)
        vals = fn(i, n, *[r[...] for r in refs[:nin]])
        for o_ref, spec, v in zip(refs[nin:], outs, vals):
            if spec[0] == "acc":
                @pl.when(i == 0)
                def _(o_ref=o_ref):
                    o_ref[...] = jnp.zeros_like(o_ref)

                o_ref[...] += v.astype(o_ref.dtype)
            else:
                o_ref[...] = v.astype(o_ref.dtype)

    res = pl.pallas_call(
        body, name=name,
        out_shape=tuple(out_shapes),
        grid=(n,),
        in_specs=in_specs,
        out_specs=tuple(out_specs),
        compiler_params=_params(("arbitrary",) if any_acc else ("parallel",)),
    )(*args)
    return res


def _rstd(x):
    return lax.rsqrt(jnp.mean(x * x, axis=-1, keepdims=True) + EPS)


def _norm_bwd(x, r, g, dy):
    xh = x * r
    dyg = dy * g
    dx = r * (dyg - xh * jnp.mean(dyg * xh, axis=-1, keepdims=True))
    return dx, dy * xh


def _sigmoid(x):
    return 0.5 * jnp.tanh(0.5 * x) + 0.5


def _colsum(v):
    return jnp.sum(v, axis=0, keepdims=True)


def _rope_tables(pos, invf):
    ang = pos.astype(F32) * invf
    lane = lax.broadcasted_iota(jnp.int32, ang.shape, 1)
    cos, sin = jnp.cos(ang), jnp.sin(ang)
    ct = jnp.where(lane < QK_ROPE, cos, 0.0)
    sa = jnp.where(lane < QK_ROPE // 2, -sin, 0.0)
    sb = jnp.where((lane >= QK_ROPE // 2) & (lane < QK_ROPE), sin, 0.0)
    return ct, sa, sb


def _rope(b, ct, sa, sb):
    return ct * b + sa * pltpu.roll(b, LANE - QK_ROPE // 2, 1) + sb * pltpu.roll(b, QK_ROPE // 2, 1)


def _rope_t(d, ct, sa, sb):
    return ct * d + pltpu.roll(sa * d, QK_ROPE // 2, 1) + pltpu.roll(sb * d, LANE - QK_ROPE // 2, 1)


def _rope_fwd(q_raw, kr_pad, pos_col, invf):
    S = q_raw.shape[0]

    def fn(i, n, q, kr, pos, invf):
        ct, sa, sb = _rope_tables(pos, invf)
        parts = []
        for h in range(N_HEADS_MLA):
            parts.append(q[:, h * HEAD_PAD:h * HEAD_PAD + LANE])
            parts.append(_rope(q[:, h * HEAD_PAD + LANE:(h + 1) * HEAD_PAD], ct, sa, sb))
        return jnp.concatenate(parts, axis=1), _rope(kr, ct, sa, sb)

    return _rows(fn, S, 256, [_row(q_raw), _row(kr_pad), _row(pos_col), _full(invf)],
                 [("row", N_HEADS_MLA * HEAD_PAD, BF16), ("row", LANE, BF16)], "rope_fwd")


def _rope_bwd(dq, dkp, pos_col, invf):
    S = dq.shape[0]
    tm = 256

    def body(dq_ref, dkp_ref, pos_ref, invf_ref, dqo_ref, dkr_ref):
        ct, sa, sb = _rope_tables(pos_ref[...], invf_ref[...])
        for h in range(N_HEADS_MLA):
            dqo_ref[:, h * HEAD_PAD:h * HEAD_PAD + LANE] = dq_ref[:, h * HEAD_PAD:h * HEAD_PAD + LANE].astype(BF16)
            dqo_ref[:, h * HEAD_PAD + LANE:(h + 1) * HEAD_PAD] = _rope_t(
                dq_ref[:, h * HEAD_PAD + LANE:(h + 1) * HEAD_PAD], ct, sa, sb).astype(BF16)
        tot = dkp_ref[0]
        for h in range(1, N_HEADS_MLA):
            tot = tot + dkp_ref[h]
        dkr_ref[...] = _rope_t(tot, ct, sa, sb).astype(BF16)

    return pl.pallas_call(
        body, name="rope_bwd",
        out_shape=(jax.ShapeDtypeStruct(dq.shape, BF16), jax.ShapeDtypeStruct((S, LANE), BF16)),
        grid=(S // tm,),
        in_specs=[pl.BlockSpec((tm, dq.shape[1]), lambda i: (i, 0)),
                  pl.BlockSpec((N_HEADS_MLA, tm, LANE), lambda i: (0, i, 0)),
                  pl.BlockSpec((tm, 1), lambda i: (i, 0)),
                  pl.BlockSpec((1, LANE), lambda i: (0, 0))],
        out_specs=(pl.BlockSpec((tm, dq.shape[1]), lambda i: (i, 0)), pl.BlockSpec((tm, LANE), lambda i: (i, 0))),
        compiler_params=_params(("parallel",)),
    )(dq, dkp, pos_col, invf)


def _row_of(col, n):
    eye = lax.broadcasted_iota(jnp.int32, (LANE, LANE), 0) == lax.broadcasted_iota(jnp.int32, (LANE, LANE), 1)
    parts = [jnp.sum(jnp.where(eye, col[i:i + LANE], 0.0), axis=0, keepdims=True) for i in range(0, n, LANE)]
    return parts[0] if len(parts) == 1 else jnp.concatenate(parts, axis=1)


def _attn_fwd(q, kv, kp, tile):
    S = q.shape[0]
    nq = S // tile
    scale = QK_DIM ** -0.5
    nt = (((1,), (1,)), ((), ()))

    def body(q_ref, kv_ref, kp_ref, o_ref, lse_ref, m_s, l_s, acc_s, s_buf):
        qi = pl.program_id(1)
        qv = q_ref[...]
        m_s[...] = jnp.full_like(m_s, NEG)
        l_s[...] = jnp.zeros_like(l_s)
        acc_s[...] = jnp.zeros_like(acc_s)

        def scores(j):
            start = pl.multiple_of(j * tile, tile)
            k = jnp.concatenate([kv_ref[pl.ds(start, tile), 0:LANE], kp_ref[pl.ds(start, tile), :]], axis=1)
            return lax.dot_general(qv, k, nt, preferred_element_type=F32) * scale

        def update(s, j):
            v = kv_ref[pl.ds(pl.multiple_of(j * tile, tile), tile), LANE:2 * LANE]
            m_old = m_s[...]
            m_new = jnp.maximum(m_old, jnp.max(s, axis=1, keepdims=True))
            alpha = jnp.exp(m_old - m_new)
            p = jnp.exp(s - m_new)
            l_s[...] = alpha * l_s[...] + jnp.sum(p, axis=1, keepdims=True)
            acc_s[...] = alpha * acc_s[...] + jnp.dot(p.astype(BF16), v, preferred_element_type=F32)
            m_s[...] = m_new

        s_buf[0] = scores(0)

        def loop_body(j, carry):
            nxt = scores(j + 1)
            update(s_buf[lax.rem(j, 2)], j)
            s_buf[lax.rem(j + 1, 2)] = nxt
            return carry

        lax.fori_loop(0, qi, loop_body, 0)
        s = s_buf[lax.rem(qi, 2)]
        row = lax.broadcasted_iota(jnp.int32, s.shape, 0)
        col = lax.broadcasted_iota(jnp.int32, s.shape, 1)
        update(jnp.where(row >= col, s, NEG), qi)
        l = l_s[...]
        o_ref[...] = (acc_s[...] / l).astype(o_ref.dtype)
        lse_ref[0, 0] = _row_of(m_s[...] + jnp.log(l), tile)

    return pl.pallas_call(
        body, name="attn_fwd",
        out_shape=(jax.ShapeDtypeStruct((S, N_HEADS_MLA * V_DIM), BF16),
                   jax.ShapeDtypeStruct((N_HEADS_MLA, nq, 1, tile), F32)),
        grid=(N_HEADS_MLA, nq),
        in_specs=[pl.BlockSpec((tile, HEAD_PAD), lambda h, i: (i, h)),
                  pl.BlockSpec((S, HEAD_PAD), lambda h, i: (0, h)),
                  pl.BlockSpec((S, LANE), lambda h, i: (0, 0))],
        out_specs=(pl.BlockSpec((tile, V_DIM), lambda h, i: (i, h)),
                   pl.BlockSpec((1, 1, 1, tile), lambda h, i: (h, i, 0, 0))),
        scratch_shapes=[pltpu.VMEM((tile, 1), F32), pltpu.VMEM((tile, 1), F32), pltpu.VMEM((tile, V_DIM), F32),
                        pltpu.VMEM((2, tile, tile), F32)],
        compiler_params=_params(("parallel", "arbitrary")),
    )(q, kv, kp)


def _attn_delta(o, do, tile):
    S = o.shape[0]
    nq = S // tile

    def body(o_ref, do_ref, d_ref):
        prod = o_ref[...].astype(F32) * do_ref[...].astype(F32)
        for h in range(N_HEADS_MLA):
            col = jnp.sum(prod[:, h * V_DIM:(h + 1) * V_DIM], axis=1, keepdims=True)
            d_ref[h, 0] = _row_of(col, tile)

    return pl.pallas_call(
        body, name="attn_delta",
        out_shape=jax.ShapeDtypeStruct((N_HEADS_MLA, nq, 1, tile), F32),
        grid=(nq,),
        in_specs=[pl.BlockSpec((tile, o.shape[1]), lambda i: (i, 0)), pl.BlockSpec((tile, o.shape[1]), lambda i: (i, 0))],
        out_specs=pl.BlockSpec((N_HEADS_MLA, 1, 1, tile), lambda i: (0, i, 0, 0)),
        compiler_params=_params(("parallel",)),
    )(o, do)


def _attn_bwd(q, kv, kp, do, lse, delta, tile):
    S = q.shape[0]
    nq = S // tile
    scale = QK_DIM ** -0.5
    nt = (((1,), (1,)), ((), ()))
    tn = (((0,), (0,)), ((), ()))

    def body(kv_ref, kp_ref, q_ref, do_ref, lse_ref, d_ref, dq_ref, dkv_ref, dkp_ref, dk_s, dv_s):
        ki = pl.program_id(1)
        k = jnp.concatenate([kv_ref[:, 0:LANE], kp_ref[...]], axis=1)
        v = kv_ref[:, LANE:2 * LANE]

        @pl.when(ki == 0)
        def _():
            dq_ref[...] = jnp.zeros_like(dq_ref)

        dk_s[...] = jnp.zeros_like(dk_s)
        dv_s[...] = jnp.zeros_like(dv_s)

        def step(qi, masked):
            start = pl.multiple_of(qi * tile, tile)
            qv = q_ref[pl.ds(start, tile), :]
            dov = do_ref[pl.ds(start, tile), :]
            st = lax.dot_general(k, qv, nt, preferred_element_type=F32) * scale
            pt = jnp.exp(st - lse_ref[0, qi])
            if masked:
                krow = lax.broadcasted_iota(jnp.int32, pt.shape, 0)
                qcol = lax.broadcasted_iota(jnp.int32, pt.shape, 1)
                pt = jnp.where(krow <= qcol, pt, 0.0)
            dv_s[...] += jnp.dot(pt.astype(BF16), dov, preferred_element_type=F32)
            dpt = lax.dot_general(v, dov, nt, preferred_element_type=F32)
            dst = (pt * (dpt - d_ref[0, qi]) * scale).astype(BF16)
            dk_s[...] += jnp.dot(dst, qv, preferred_element_type=F32)
            dq_ref[pl.ds(start, tile), :] += lax.dot_general(dst, k, tn, preferred_element_type=F32)

        step(ki, True)

        def loop_body(qi, carry):
            step(qi, False)
            return carry

        lax.fori_loop(ki + 1, nq, loop_body, 0)
        dkv_ref[...] = jnp.concatenate([dk_s[:, 0:LANE], dv_s[...]], axis=1).astype(dkv_ref.dtype)
        dkp_ref[0] = dk_s[:, LANE:2 * LANE]

    return pl.pallas_call(
        body, name="attn_bwd",
        out_shape=(jax.ShapeDtypeStruct((S, N_HEADS_MLA * HEAD_PAD), F32),
                   jax.ShapeDtypeStruct((S, N_HEADS_MLA * HEAD_PAD), BF16),
                   jax.ShapeDtypeStruct((N_HEADS_MLA, S, LANE), F32)),
        grid=(N_HEADS_MLA, nq),
        in_specs=[pl.BlockSpec((tile, HEAD_PAD), lambda h, i: (i, h)),
                  pl.BlockSpec((tile, LANE), lambda h, i: (i, 0)),
                  pl.BlockSpec((S, HEAD_PAD), lambda h, i: (0, h)),
                  pl.BlockSpec((S, V_DIM), lambda h, i: (0, h)),
                  pl.BlockSpec((1, nq, 1, tile), lambda h, i: (h, 0, 0, 0)),
                  pl.BlockSpec((1, nq, 1, tile), lambda h, i: (h, 0, 0, 0))],
        out_specs=(pl.BlockSpec((S, HEAD_PAD), lambda h, i: (0, h)),
                   pl.BlockSpec((tile, HEAD_PAD), lambda h, i: (i, h)),
                   pl.BlockSpec((1, tile, LANE), lambda h, i: (h, i, 0))),
        scratch_shapes=[pltpu.VMEM((tile, HEAD_PAD), F32), pltpu.VMEM((tile, V_DIM), F32)],
        compiler_params=_params(("parallel", "arbitrary")),
    )(kv, kp, q, do, lse, delta)


def _shift_down(cur, halo, k):
    sh = pltpu.roll(cur, k, 0)
    hs = pltpu.roll(halo, k, 0)
    rows = lax.broadcasted_iota(jnp.int32, hs.shape, 0)
    first = jnp.where(rows < k, hs, sh[0:SUBLANE])
    if cur.shape[0] == SUBLANE:
        return first
    return jnp.concatenate([first, sh[SUBLANE:]], axis=0)


def _shift_up(cur, nxt, k):
    n = cur.shape[0]
    sh = pltpu.roll(cur, n - k, 0)
    ns = pltpu.roll(nxt, SUBLANE - k, 0)
    rows = lax.broadcasted_iota(jnp.int32, ns.shape, 0)
    last = jnp.where(rows >= SUBLANE - k, ns, sh[n - SUBLANE:])
    if n == SUBLANE:
        return last
    return jnp.concatenate([sh[:n - SUBLANE], last], axis=0)


def _conv_pre(cur, halo, w, b):
    shifted = [_shift_down(cur, halo, k) for k in range(1, CONV_WIDTH)]
    out = b + w[3:4] * cur
    for k in range(1, CONV_WIDTH):
        out = out + w[3 - k:4 - k] * shifted[k - 1]
    return out, shifted


def _conv_fwd(xbc, w, b):
    S = xbc.shape[0]

    def fn(i, n, cur, prev, w, b):
        halo = jnp.where(i > 0, prev, 0.0)
        pre, _ = _conv_pre(cur, halo, w, b)
        return (pre * _sigmoid(pre),)

    return _rows(fn, S, 256, [_row(xbc), _prev8(xbc), _full(w), _full(b)], [("row", xbc.shape[1], F32)], "conv_fwd")[0]


def _conv_bwd(xbc, dacts, w, b):
    S, C = xbc.shape

    def dsilu(pre):
        s = _sigmoid(pre)
        return s * (1.0 + pre * (1.0 - s))

    def fn(i, n, cur, prev, nxt, *rest):
        k3 = len(dacts)
        dcur = jnp.concatenate(rest[:k3], axis=1)
        dnxt = jnp.concatenate(rest[k3:2 * k3], axis=1)
        w, b = rest[2 * k3:]
        halo = jnp.where(i > 0, prev, 0.0)
        pre, shifted = _conv_pre(cur, halo, w, b)
        dpre = dcur * dsilu(pre)
        pre_n, _ = _conv_pre(nxt, cur[cur.shape[0] - SUBLANE:], w, b)
        dpre_n = jnp.where(i < n - 1, dnxt * dsilu(pre_n), 0.0)
        dx = w[3:4] * dpre
        rows = lax.broadcasted_iota(jnp.int32, (SUBLANE, C), 0)
        dw = jnp.where(rows == 3, _colsum(dpre * cur), 0.0)
        for k in range(1, CONV_WIDTH):
            dx = dx + w[3 - k:4 - k] * _shift_up(dpre, dpre_n, k)
            dw = dw + jnp.where(rows == 3 - k, _colsum(dpre * shifted[k - 1]), 0.0)
        return dx, dw, _colsum(dpre)

    return _rows(fn, S, 256, [_row(xbc), _prev8(xbc), _next8(xbc), *[_row(d) for d in dacts], *[_next8(d) for d in dacts],
                              _full(w), _full(b)],
                 [("row", C, BF16), ("acc", (SUBLANE, C), F32), ("acc", (1, C), F32)], "conv_bwd")


def _softplus(x):
    return jnp.maximum(x, 0.0) + jnp.log1p(jnp.exp(-jnp.abs(x)))


def _cumsum_rows(x):
    rows = lax.broadcasted_iota(jnp.int32, x.shape, 0)
    s = 1
    while s < x.shape[0]:
        x = x + jnp.where(rows >= s, pltpu.roll(x, s, 0), 0.0)
        s *= 2
    return x


def _revcumsum_rows(x):
    n = x.shape[0]
    rows = lax.broadcasted_iota(jnp.int32, x.shape, 0)
    s = 1
    while s < n:
        x = x + jnp.where(rows < n - s, pltpu.roll(x, n - s, 0), 0.0)
        s *= 2
    return x


def _dt_prep(dt_raw, dt_bias, a_log):
    S = dt_raw.shape[0]

    def body(raw_ref, bias_ref, alog_ref, dt_ref, cum_ref, cumt_ref):
        dt = _softplus(raw_ref[...] + bias_ref[...])
        cum = _cumsum_rows(dt * (-jnp.exp(alog_ref[...])))
        dt_ref[...] = dt
        cum_ref[...] = cum
        cumt_ref[...] = cum.T

    return pl.pallas_call(
        body, name="dt_prep",
        out_shape=(jax.ShapeDtypeStruct((S, LANE), F32), jax.ShapeDtypeStruct((S, LANE), F32),
                   jax.ShapeDtypeStruct((LANE, S), F32)),
        grid=(S // CHUNK,),
        in_specs=[pl.BlockSpec((CHUNK, LANE), lambda i: (i, 0)), pl.BlockSpec((1, LANE), lambda i: (0, 0)),
                  pl.BlockSpec((1, LANE), lambda i: (0, 0))],
        out_specs=(pl.BlockSpec((CHUNK, LANE), lambda i: (i, 0)), pl.BlockSpec((CHUNK, LANE), lambda i: (i, 0)),
                   pl.BlockSpec((LANE, CHUNK), lambda i: (0, i))),
        compiler_params=_params(("parallel",)),
    )(dt_raw, dt_bias, a_log)


_NT = (((1,), (1,)), ((), ()))
_TN = (((0,), (0,)), ((), ()))
P = SSM_HEADDIM
GW = HEADS_PER_GROUP * SSM_HEADDIM


PAIRS = HEADS_PER_GROUP // 2
SPREAD_W = HEADS_PER_GROUP * LANE


def _spread_matrix():
    e = np.zeros((SSM_GROUPS, LANE, SPREAD_W), np.float32)
    for g in range(SSM_GROUPS):
        for r in range(HEADS_PER_GROUP):
            e[g, g * HEADS_PER_GROUP + r, r * LANE:(r + 1) * LANE] = 1.0
    return jnp.asarray(e, BF16)


def _pieces(v, n):
    out = []
    for _ in range(n):
        p = v.astype(BF16)
        out.append(p)
        v = v - p.astype(F32)
    return out


def _spread(v, e, n):
    tot = None
    for p in _pieces(v, n):
        t = jnp.dot(p, e, preferred_element_type=F32)
        tot = t if tot is None else tot + t
    return tot


def _gather_rows(z, e):
    hi, lo = _pieces(z, 2)
    return lax.dot_general(hi, e, _NT, preferred_element_type=F32) + lax.dot_general(lo, e, _NT, preferred_element_type=F32)


def _decay_pair(cc, cr, transposed):
    L = cc.shape[0]
    halves = []
    for h in range(L // LANE):
        i = lax.broadcasted_iota(jnp.int32, (L, LANE), 0)
        j = lax.broadcasted_iota(jnp.int32, (L, LANE), 1) + h * LANE
        crh = cr[:, h * LANE:(h + 1) * LANE]
        if transposed:
            halves.append(jnp.exp(jnp.where(j >= i, crh - cc, NEG)))
        else:
            halves.append(jnp.exp(jnp.where(i >= j, cc - crh, NEG)))
    return jnp.concatenate(halves, axis=1)


def _ssd_fwd(xbc_c, dt, cum, cumt_g, spread):
    S = xbc_c.shape[0]
    nc = S // CHUNK
    L = CHUNK
    boff = D_INNER // D_STATE

    def body(x_ref, b_ref, c_ref, dt_ref, cum_ref, cumt_ref, e_ref, y_ref, st_ref, state):
        c = pl.program_id(1)

        @pl.when(c == 0)
        def _():
            state[...] = jnp.zeros_like(state)

        e = e_ref[0]
        bm = b_ref[...].astype(BF16)
        cm = c_ref[...].astype(BF16)
        cb = lax.dot_general(cm, bm, _NT, preferred_element_type=F32)
        rep_cum = _spread(cum_ref[...], e, 3)
        rep_dt = _spread(dt_ref[...], e, 2)
        lo = lax.broadcasted_iota(jnp.int32, (L, LANE), 1) < P
        lo1 = lax.broadcasted_iota(jnp.int32, (1, LANE), 1) < P
        top = lax.broadcasted_iota(jnp.int32, (LANE, LANE), 0) < P
        for p in range(PAIRS):
            t0, t1 = 2 * p * LANE, (2 * p + 1) * LANE
            cc0, cc1 = rep_cum[:, t0:t0 + LANE], rep_cum[:, t1:t1 + LANE]
            ccp = jnp.where(lo, cc0, cc1)
            cl0, cl1 = cc0[L - 1:L, :], cc1[L - 1:L, :]
            clp = jnp.where(lo1, cl0, cl1)
            xdt = x_ref[:, p * LANE:(p + 1) * LANE] * jnp.where(lo, rep_dt[:, t0:t0 + LANE], rep_dt[:, t1:t1 + LANE])
            xdb = xdt.astype(BF16)
            ys = []
            for r, cc in ((2 * p, cc0), (2 * p + 1, cc1)):
                m = (cb * _decay_pair(cc, cumt_ref[0, r:r + 1, :], False)).astype(BF16)
                ys.append(jnp.dot(m, xdb, preferred_element_type=F32))
            st = state[p * LANE:(p + 1) * LANE, :]
            st_ref[0, 0, p * LANE:(p + 1) * LANE, :] = st
            yoff = lax.dot_general(cm, st.astype(BF16), _NT, preferred_element_type=F32) * jnp.exp(ccp)
            y_ref[:, p * LANE:(p + 1) * LANE] = jnp.where(lo, ys[0], ys[1]) + yoff
            wend = jnp.exp(clp - ccp)
            ecl = jnp.where(top, jnp.exp(cl0), jnp.exp(cl1))
            state[p * LANE:(p + 1) * LANE, :] = st * ecl + lax.dot_general(
                (xdt * wend).astype(BF16), bm, _TN, preferred_element_type=F32)

    return pl.pallas_call(
        body, name="ssd_fwd",
        out_shape=(jax.ShapeDtypeStruct((S, D_INNER), F32), jax.ShapeDtypeStruct((SSM_GROUPS, nc, GW, D_STATE), F32)),
        grid=(SSM_GROUPS, nc),
        in_specs=[pl.BlockSpec((L, GW), lambda g, c: (c, g)),
                  pl.BlockSpec((L, D_STATE), lambda g, c: (c, boff + g)),
                  pl.BlockSpec((L, D_STATE), lambda g, c: (c, boff + SSM_GROUPS + g)),
                  pl.BlockSpec((L, LANE), lambda g, c: (c, 0)),
                  pl.BlockSpec((L, LANE), lambda g, c: (c, 0)),
                  pl.BlockSpec((1, HEADS_PER_GROUP, L), lambda g, c: (g, 0, c)),
                  pl.BlockSpec((1, LANE, SPREAD_W), lambda g, c: (g, 0, 0))],
        out_specs=(pl.BlockSpec((L, GW), lambda g, c: (c, g)),
                   pl.BlockSpec((1, 1, GW, D_STATE), lambda g, c: (g, c, 0, 0))),
        scratch_shapes=[pltpu.VMEM((GW, D_STATE), F32)],
        compiler_params=_params(("parallel", "arbitrary")),
    )(xbc_c, xbc_c, xbc_c, dt, cum, cumt_g, spread)


def _ssd_bwd(xbc_c, dt, cum, cumt_g, spread, states, dy, d_skip):
    S = xbc_c.shape[0]
    nc = S // CHUNK
    L = CHUNK
    boff = D_INNER // D_STATE
    rev = lambda c: nc - 1 - c

    def body(x_ref, b_ref, c_ref, dt_ref, cum_ref, cumt_ref, e_ref, st_ref, dy_ref, skip_ref,
             dx_ref, db_ref, dc_ref, ddt_ref, dcum_ref, dstate):
        c = pl.program_id(1)

        @pl.when(c == 0)
        def _():
            dstate[...] = jnp.zeros_like(dstate)

        e = e_ref[0]
        bf = b_ref[...]
        bm = bf.astype(BF16)
        cm = c_ref[...].astype(BF16)
        cb = lax.dot_general(cm, bm, _NT, preferred_element_type=F32)
        cbt = lax.dot_general(bm, cm, _NT, preferred_element_type=F32)
        rep_cum = _spread(cum_ref[...], e, 3)
        rep_dt = _spread(dt_ref[...], e, 2)
        lane = lax.broadcasted_iota(jnp.int32, (L, LANE), 1)
        lo = lane < P
        lo1 = lax.broadcasted_iota(jnp.int32, (1, LANE), 1) < P
        top = lax.broadcasted_iota(jnp.int32, (LANE, LANE), 0) < P
        last = lax.broadcasted_iota(jnp.int32, (L, LANE), 0) == L - 1
        dcb = jnp.zeros((L, L), F32)
        dcbt = jnp.zeros((L, L), F32)
        dbs = jnp.zeros((L, D_STATE), F32)
        dcs = jnp.zeros((L, D_STATE), F32)
        zs, zds = [], []
        for p in range(PAIRS):
            sl = slice(p * LANE, (p + 1) * LANE)
            t0, t1 = 2 * p * LANE, (2 * p + 1) * LANE
            cc0, cc1 = rep_cum[:, t0:t0 + LANE], rep_cum[:, t1:t1 + LANE]
            ccp = jnp.where(lo, cc0, cc1)
            cl0, cl1 = cc0[L - 1:L, :], cc1[L - 1:L, :]
            w0, w1 = jnp.exp(cl0 - cc0), jnp.exp(cl1 - cc1)
            wend = jnp.where(lo, w0, w1)
            ecc = jnp.exp(ccp)
            ecl0, ecl1 = jnp.exp(cl0), jnp.exp(cl1)
            dtp = jnp.where(lo, rep_dt[:, t0:t0 + LANE], rep_dt[:, t1:t1 + LANE])
            xp = x_ref[:, sl]
            xdt = xp * dtp
            xdb = xdt.astype(BF16)
            dyp = dy_ref[:, sl]
            st = st_ref[0, 0, sl, :]
            stb = st.astype(BF16)
            ds = dstate[sl, :]
            dsb = ds.astype(BF16)
            yoff = lax.dot_general(cm, stb, _NT, preferred_element_type=F32) * ecc
            dye = (dyp * ecc).astype(BF16)
            dcs = dcs + jnp.dot(dye, stb, preferred_element_type=F32)
            dstate[sl, :] = jnp.where(top, ecl0, ecl1) * ds + lax.dot_general(dye, cm, _TN, preferred_element_type=F32)
            dxd = lax.dot_general(bm, dsb, _NT, preferred_element_type=F32) * wend
            sst = ds * st
            dyo = dyp * yoff
            mts = []
            for r, cc, w, ecl, keep, keep_rows in ((2 * p, cc0, w0, ecl0, lo, top), (2 * p + 1, cc1, w1, ecl1, ~lo, ~top)):
                cr = cumt_ref[0, r:r + 1, :]
                decay = _decay_pair(cc, cr, False)
                decay_t = _decay_pair(cc, cr, True)
                m = cb * decay
                mt = cbt * decay_t
                dyr = jnp.where(keep, dyp, 0.0).astype(BF16)
                g = lax.dot_general(dyr, xdb, _NT, preferred_element_type=F32)
                gt = lax.dot_general(xdb, dyr, _NT, preferred_element_type=F32)
                q = g * m
                qt = gt * mt
                dcb = dcb + g * decay
                dcbt = dcbt + gt * decay_t
                mts.append(jnp.dot(mt.astype(BF16), dyr, preferred_element_type=F32))
                t = jnp.dot(jnp.where(keep, xdt, 0.0).astype(BF16), dsb, preferred_element_type=F32)
                dbs = dbs + t * w
                tb = t * bf * w
                end_row = _colsum(tb) + ecl * _colsum(jnp.where(keep_rows, sst, 0.0))
                z = (q[:, 0:LANE] + q[:, LANE:2 * LANE]) - (qt[:, 0:LANE] + qt[:, LANE:2 * LANE])
                z = z + jnp.where(keep, dyo, 0.0) - tb + jnp.where(last, end_row, 0.0)
                zs.append(z)
            dxd = dxd + mts[0] + mts[1]
            dx_ref[:, sl] = dxd * dtp + dyp * skip_ref[:, sl]
            zd = dxd * xp
            zds.append(jnp.where(lo, zd, 0.0))
            zds.append(jnp.where(lo, 0.0, zd))
        dc_ref[...] = dcs + jnp.dot(dcb.astype(BF16), bm, preferred_element_type=F32)
        db_ref[...] = dbs + jnp.dot(dcbt.astype(BF16), cm, preferred_element_type=F32)
        dcum_ref[0] = _gather_rows(jnp.concatenate(zs, axis=1), e)
        ddt_ref[0] = _gather_rows(jnp.concatenate(zds, axis=1), e)

    return pl.pallas_call(
        body, name="ssd_bwd",
        out_shape=(jax.ShapeDtypeStruct((S, D_INNER), F32),
                   jax.ShapeDtypeStruct((S, SSM_GROUPS * D_STATE), F32),
                   jax.ShapeDtypeStruct((S, SSM_GROUPS * D_STATE), F32),
                   jax.ShapeDtypeStruct((SSM_GROUPS, S, LANE), F32),
                   jax.ShapeDtypeStruct((SSM_GROUPS, S, LANE), F32)),
        grid=(SSM_GROUPS, nc),
        in_specs=[pl.BlockSpec((L, GW), lambda g, c: (rev(c), g)),
                  pl.BlockSpec((L, D_STATE), lambda g, c: (rev(c), boff + g)),
                  pl.BlockSpec((L, D_STATE), lambda g, c: (rev(c), boff + SSM_GROUPS + g)),
                  pl.BlockSpec((L, LANE), lambda g, c: (rev(c), 0)),
                  pl.BlockSpec((L, LANE), lambda g, c: (rev(c), 0)),
                  pl.BlockSpec((1, HEADS_PER_GROUP, L), lambda g, c: (g, 0, rev(c))),
                  pl.BlockSpec((1, LANE, SPREAD_W), lambda g, c: (g, 0, 0)),
                  pl.BlockSpec((1, 1, GW, D_STATE), lambda g, c: (g, rev(c), 0, 0)),
                  pl.BlockSpec((L, GW), lambda g, c: (rev(c), g)),
                  pl.BlockSpec((1, GW), lambda g, c: (0, g))],
        out_specs=(pl.BlockSpec((L, GW), lambda g, c: (rev(c), g)),
                   pl.BlockSpec((L, D_STATE), lambda g, c: (rev(c), g)),
                   pl.BlockSpec((L, D_STATE), lambda g, c: (rev(c), g)),
                   pl.BlockSpec((1, L, LANE), lambda g, c: (g, rev(c), 0)),
                   pl.BlockSpec((1, L, LANE), lambda g, c: (g, rev(c), 0))),
        scratch_shapes=[pltpu.VMEM((GW, D_STATE), F32)],
        compiler_params=_params(("parallel", "arbitrary")),
    )(xbc_c, xbc_c, xbc_c, dt, cum, cumt_g, spread, states, dy, d_skip)


def _dt_bwd(dt_raw, dt_bias, a_log, ddt_x, dcum):
    S = dt_raw.shape[0]
    n = S // CHUNK

    def body(raw_ref, ddx_ref, dcu_ref, bias_ref, alog_ref, draw_ref, gb_ref, ga_ref):
        i = pl.program_id(0)

        @pl.when(i == 0)
        def _():
            gb_ref[...] = jnp.zeros_like(gb_ref)
            ga_ref[...] = jnp.zeros_like(ga_ref)

        ddx, dcu = ddx_ref[0], dcu_ref[0]
        for g in range(1, SSM_GROUPS):
            ddx = ddx + ddx_ref[g]
            dcu = dcu + dcu_ref[g]
        xx = raw_ref[...] + bias_ref[...]
        dt = _softplus(xx)
        a = -jnp.exp(alog_ref[...])
        dda = _revcumsum_rows(dcu)
        lane = lax.broadcasted_iota(jnp.int32, xx.shape, 1)
        draw = jnp.where(lane < N_HEADS_SSM, (ddx + dda * a) * _sigmoid(xx), 0.0)
        draw_ref[...] = draw.astype(draw_ref.dtype)
        gb_ref[...] += _colsum(draw)
        ga_ref[...] += _colsum(dda * dt) * a

    row = pl.BlockSpec((CHUNK, LANE), lambda i: (i, 0))
    grp = pl.BlockSpec((SSM_GROUPS, CHUNK, LANE), lambda i: (0, i, 0))
    one = pl.BlockSpec((1, LANE), lambda i: (0, 0))
    return pl.pallas_call(
        body, name="dt_bwd",
        out_shape=(jax.ShapeDtypeStruct((S, LANE), BF16), jax.ShapeDtypeStruct((1, LANE), F32), jax.ShapeDtypeStruct((1, LANE), F32)),
        grid=(n,),
        in_specs=[row, grp, grp, one, one],
        out_specs=(row, one, one),
        compiler_params=_params(("arbitrary",)),
    )(dt_raw, ddt_x, dcum, dt_bias, a_log)


def _adamw(w, g, m, v, name):
    shape = w.shape
    cols = shape[-1]
    rows = int(np.prod(shape[:-1]))
    w2, g2, m2, v2 = (t.reshape(1, rows, cols) for t in (w, g, m, v))
    tr = rows if rows * cols <= 512 * 1024 else _tile(rows, max(SUBLANE, (512 * 1024 // cols) // SUBLANE * SUBLANE), SUBLANE)
    c1 = 1.0 - ADAM_B1 ** ADAM_STEP
    c2 = 1.0 - ADAM_B2 ** ADAM_STEP

    def body(w_ref, g_ref, m_ref, v_ref, d_ref, mo_ref, vo_ref):
        gv = g_ref[...]
        mn = ADAM_B1 * m_ref[...] + (1.0 - ADAM_B1) * gv
        vn = ADAM_B2 * v_ref[...] + (1.0 - ADAM_B2) * (gv * gv)
        d_ref[...] = -ADAM_LR * ((mn / c1) / (jnp.sqrt(vn / c2) + ADAM_EPS) + ADAM_WD * w_ref[...])
        mo_ref[...] = mn
        vo_ref[...] = vn

    spec = pl.BlockSpec((1, tr, cols), lambda i: (0, i, 0))
    outs = pl.pallas_call(
        body, name=name,
        out_shape=tuple(jax.ShapeDtypeStruct((1, rows, cols), F32) for _ in range(3)),
        grid=(rows // tr,),
        in_specs=[spec] * 4, out_specs=(spec,) * 3,
        compiler_params=_params(("parallel",)),
    )(w2, g2, m2, v2)
    return tuple(o.reshape(shape) for o in outs)


def _prep_weights(w_in, w_uq):
    offs = np.cumsum((0,) + IN_SPLITS)
    pad = lambda t: jnp.pad(t, ((0, 0), (0, LANE - t.shape[1])))
    pieces = dict(
        qkv=w_in[:, offs[0]:offs[2]],
        kr=pad(w_in[:, offs[2]:offs[3]]),
        z=w_in[:, offs[3]:offs[4]],
        xbc=w_in[:, offs[4]:offs[5]],
        dt=pad(w_in[:, offs[5]:offs[6]]),
        g=w_in[:, offs[6]:offs[8]],
    )
    uq = w_uq.reshape(Q_LORA, N_HEADS_MLA, QK_DIM)
    uq = jnp.pad(uq, ((0, 0), (0, 0), (0, HEAD_PAD - QK_DIM))).reshape(Q_LORA, N_HEADS_MLA * HEAD_PAD)
    return pieces, uq


def _local_step(x, p, positions, ex, sp, target):
    W = gw = ex
    S = x.shape[0]
    tile = min(ATTN_TILE, S)
    pos_col = positions.reshape(S, 1)
    invf = ROPE_THETA ** (-jnp.arange(0, QK_ROPE, 2, dtype=F32) / QK_ROPE)
    invf = jnp.pad(jnp.concatenate([invf, invf]), (0, LANE - QK_ROPE)).reshape(1, LANE)
    wp, w_uq_p = _prep_weights(W["w_in"], W["w_uq"])
    padl = lambda t: jnp.pad(t, ((0, 0), (0, LANE - t.shape[1])))
    dt_bias_p, a_log_p = padl(sp["dt_bias"]), padl(sp["a_log"])
    dskip_ch = jnp.repeat(sp["d_skip"], SSM_HEADDIM, axis=1)
    p_bf = p.astype(BF16)
    RW = 256

    (u_bf,) = _rows(lambda i, n, x, g: (x * _rstd(x) * g,), S, RW, [_row(x), _full(sp["mix_norm_pre"])],
                    [("row", D_MODEL, BF16)], "norm_pre")
    cqkv = ex.mm(u_bf, wp["qkv"], name="mm_qkv")
    z = ex.mm(u_bf, wp["z"], name="mm_z")
    xbc = ex.mm(u_bf, wp["xbc"], name="mm_xbc")
    gates = ex.mm(u_bf, wp["g"], name="mm_gates")
    kr_pad = ex.mm(u_bf, wp["kr"], name="mm_kr")
    dt_raw = ex.mm(u_bf, wp["dt"], name="mm_dt")

    def qkv_norm(i, n, cq, ckv, gq, gkv):
        return cq * _rstd(cq) * gq, ckv * _rstd(ckv) * gkv

    cqn, ckvn = _rows(qkv_norm, S, 512, [_row(cqkv, Q_LORA, 0), _row(cqkv, KV_LORA, 1), _full(sp["q_norm"]), _full(sp["kv_norm"])],
                      [("row", Q_LORA, BF16), ("row", KV_LORA, BF16)], "qkv_norm")
    q_raw = ex.mm(cqn, w_uq_p, name="mm_uq")
    kv = ex.mm(ckvn, W["w_ukv"], out_dtype=BF16, name="mm_ukv")
    q_bf, kp_bf = _rope_fwd(q_raw, kr_pad, pos_col, invf)
    attn, lse = _attn_fwd(q_bf, kv, kp_bf, tile)

    xbc_c = _conv_fwd(xbc, sp["conv_w"], sp["conv_b"])
    dt, cum, cumt = _dt_prep(dt_raw, dt_bias_p, a_log_p)
    cumt_g = cumt[:N_HEADS_SSM].reshape(SSM_GROUPS, HEADS_PER_GROUP, S)
    spread = _spread_matrix()
    y, states = _ssd_fwd(xbc_c, dt, cum, cumt_g, spread)

    GN = D_INNER // SSM_GROUPS

    def gated(y, xs, z, dsk):
        yt = y + dsk * xs
        sz = _sigmoid(z)
        return yt, sz, yt * (z * sz)

    def gated_norm(i, n, y, xs, z, dsk, gn):
        _, _, yg = gated(y, xs, z, dsk)
        parts = []
        for g in range(SSM_GROUPS):
            blk = yg[:, g * GN:(g + 1) * GN]
            parts.append(blk * _rstd(blk) * gn[:, g * GN:(g + 1) * GN])
        return (jnp.concatenate(parts, axis=1),)

    (ssm,) = _rows(gated_norm, S, 128, [_row(y), _row(xbc_c, D_INNER, 0), _row(z), _full(dskip_ch), _full(sp["ssm_norm"])],
                   [("row", D_INNER, BF16)], "gated_norm")

    a_o = ex.mm(attn, W["w_attn_o"], name="mm_attn_o")
    b_o = ex.mm(ssm, W["w_ssm_o"], name="mm_ssm_o")

    def mix(i, n, ga, gs, a, b):
        return (_sigmoid(ga) * a + _sigmoid(gs) * b,)

    (mixed,) = _rows(mix, S, RW, [_row(gates, D_MODEL, 0), _row(gates, D_MODEL, 1), _row(a_o), _row(b_o)],
                     [("row", D_MODEL, BF16)], "mix")
    m2 = ex.mm(mixed, W["w_out"], name="mm_out")

    def post(i, n, h, m, gpost, gpre):
        hn = h + m * _rstd(m) * gpost
        return hn, hn * _rstd(hn) * gpre

    h1, f_bf = _rows(post, S, RW, [_row(x), _row(m2), _full(sp["mix_norm_post"]), _full(sp["ffn_norm_pre"])],
                     [("row", D_MODEL, F32), ("row", D_MODEL, BF16)], "post_mix")
    ga, up, s_bf = _mm_swiglu(f_bf, W["w_gate"], W["w_up"])
    f2 = ex.mm(s_bf, W["w_down"], name="mm_down")
    h2, n3_bf = _rows(post, S, RW, [_row(h1), _row(f2), _full(sp["ffn_norm_post"]), _full(sp["ple_norm_pre"])],
                      [("row", D_MODEL, F32), ("row", D_MODEL, BF16)], "post_ffn")
    gpre = ex.mm(n3_bf, W["w_ple_gate"], name="mm_ple_gate")
    pe = ex.mm(p_bf, W["w_ple"], name="mm_ple")

    def ple_loss(i, n, h2, gpre, pe, tgt, gpost):
        gate = _sigmoid(gpre)
        e = pe * gate
        r = _rstd(e)
        diff = h2 + e * r * gpost - tgt
        loss = 0.5 * jnp.sum(jnp.mean(diff * diff, axis=1, keepdims=True))
        dh3 = diff * (1.0 / D_MODEL)
        de, dg_rows = _norm_bwd(e, r, gpost, dh3)
        return (jnp.full((1, LANE), loss, F32), dh3, de * gate, de * pe * gate * (1.0 - gate), _colsum(dg_rows))

    loss, dh3, dpe, dgpre, g_ple_post = _rows(
        ple_loss, S, 128, [_row(h2), _row(gpre), _row(pe), _row(target), _full(sp["ple_norm_post"])],
        [("acc", (1, LANE), F32), ("row", D_MODEL, F32), ("row", D_MODEL, BF16), ("row", D_MODEL, BF16),
         ("acc", (1, D_MODEL), F32)], "ple_loss")

    gs = {"ple_norm_post": g_ple_post}
    gw["w_ple"] = ex.mm(p_bf, dpe, ta=True, name="mmg_ple")
    gw["w_ple_gate"] = ex.mm(n3_bf, dgpre, ta=True, name="mmg_ple_gate")
    dn3 = ex.mm(dgpre, W["w_ple_gate"], tb=True, name="mmb_ple_gate")

    def post_bwd(i, n, h, m, dhn, dn, gpost, gpre):
        rm = _rstd(m)
        hn = h + m * rm * gpost
        dx, dgpre_rows = _norm_bwd(hn, _rstd(hn), gpre, dn)
        dhn_t = dhn + dx
        dm, dgpost_rows = _norm_bwd(m, rm, gpost, dhn_t)
        return dhn_t, dm, _colsum(dgpre_rows), _colsum(dgpost_rows)

    def run_post_bwd(h, m, dhn, dn, gpost, gpre, name):
        return _rows(post_bwd, S, 128, [_row(h), _row(m), _row(dhn), _row(dn), _full(gpost), _full(gpre)],
                     [("row", D_MODEL, F32), ("row", D_MODEL, BF16), ("acc", (1, D_MODEL), F32), ("acc", (1, D_MODEL), F32)], name)

    dh2, df2, gs["ple_norm_pre"], gs["ffn_norm_post"] = run_post_bwd(
        h1, f2, dh3, dn3, sp["ffn_norm_post"], sp["ple_norm_pre"], "post_ffn_bwd")
    gw["w_down"] = ex.mm(s_bf, df2, ta=True, name="mmg_down")
    def swiglu_bwd(ds, a, b):
        sa = _sigmoid(a)
        return ds * b * (sa * (1.0 + a * (1.0 - sa))), ds * (a * sa)

    dga, dup = ex.mm(df2, W["w_down"], tb=True, tn=512, epi=(swiglu_bwd, [ga, up], [BF16, BF16]), name="mmb_down")
    gw["w_gate"] = ex.mm(f_bf, dga, ta=True, name="mmg_gate")
    gw["w_up"] = ex.mm(f_bf, dup, ta=True, name="mmg_up")
    df = ex.mm(dga, W["w_gate"], tb=True, name="mmb_gate")
    df = ex.mm(dup, W["w_up"], tb=True, add=df, name="mmb_up")
    dh1, dm2, gs["ffn_norm_pre"], gs["mix_norm_post"] = run_post_bwd(
        x, m2, dh2, df, sp["mix_norm_post"], sp["ffn_norm_pre"], "post_mix_bwd")
    gw["w_out"] = ex.mm(mixed, dm2, ta=True, name="mmg_out")
    dmixed = ex.mm(dm2, W["w_out"], tb=True, out_dtype=BF16, name="mmb_out")

    def mix_bwd(i, n, ga, gs_, a, b, dm):
        sa, ss = _sigmoid(ga), _sigmoid(gs_)
        return dm * sa, dm * ss, jnp.concatenate([dm * a * sa * (1.0 - sa), dm * b * ss * (1.0 - ss)], axis=1)

    da_o, db_o, dgates = _rows(mix_bwd, S, RW, [_row(gates, D_MODEL, 0), _row(gates, D_MODEL, 1), _row(a_o), _row(b_o), _row(dmixed)],
                               [("row", D_MODEL, BF16), ("row", D_MODEL, BF16), ("row", 2 * D_MODEL, BF16)], "mix_bwd")
    gw["w_attn_o"] = ex.mm(attn, da_o, ta=True, name="mmg_attn_o")
    dattn = ex.mm(da_o, W["w_attn_o"], tb=True, out_dtype=BF16, name="mmb_attn_o")
    gw["w_ssm_o"] = ex.mm(ssm, db_o, ta=True, name="mmg_ssm_o")
    dssm = ex.mm(db_o, W["w_ssm_o"], tb=True, out_dtype=BF16, name="mmb_ssm_o")

    delta = _attn_delta(attn, dattn, tile)
    dq, dkv, dkp = _attn_bwd(q_bf, kv, kp_bf, dattn, lse, delta, tile)
    dq_raw, dkr = _rope_bwd(dq, dkp, pos_col, invf)
    g_uq_p = ex.mm(cqn, dq_raw, ta=True, name="mmg_uq")
    gw["w_uq"] = g_uq_p.reshape(Q_LORA, N_HEADS_MLA, HEAD_PAD)[:, :, :QK_DIM].reshape(Q_LORA, N_HEADS_MLA * QK_DIM)
    dcqn = ex.mm(dq_raw, w_uq_p, tb=True, name="mmb_uq")
    gw["w_ukv"] = ex.mm(ckvn, dkv, ta=True, name="mmg_ukv")
    dckvn = ex.mm(dkv, W["w_ukv"], tb=True, name="mmb_ukv")

    def qkv_norm_bwd(i, n, cq, ckv, dq_, dkv_, gq, gkv):
        dcq, gq_rows = _norm_bwd(cq, _rstd(cq), gq, dq_)
        dckv, gkv_rows = _norm_bwd(ckv, _rstd(ckv), gkv, dkv_)
        return jnp.concatenate([dcq, dckv], axis=1), _colsum(gq_rows), _colsum(gkv_rows)

    dcqkv, gs["q_norm"], gs["kv_norm"] = _rows(
        qkv_norm_bwd, S, 512, [_row(cqkv, Q_LORA, 0), _row(cqkv, KV_LORA, 1), _row(dcqn), _row(dckvn), _full(sp["q_norm"]), _full(sp["kv_norm"])],
        [("row", Q_LORA + KV_LORA, BF16), ("acc", (1, Q_LORA), F32), ("acc", (1, KV_LORA), F32)], "qkv_norm_bwd")

    def gated_norm_bwd(i, n, y, xs, z, dssm, dsk, gn):
        yt, sz, yg = gated(y, xs, z, dsk)
        dyg_parts, gn_parts = [], []
        for g in range(SSM_GROUPS):
            sl = slice(g * GN, (g + 1) * GN)
            blk = yg[:, sl]
            dblk, rows = _norm_bwd(blk, _rstd(blk), gn[:, sl], dssm[:, sl])
            dyg_parts.append(dblk)
            gn_parts.append(_colsum(rows))
        dyg = jnp.concatenate(dyg_parts, axis=1)
        dyt = dyg * (z * sz)
        dz = dyg * yt * (sz * (1.0 + z * (1.0 - sz)))
        return dyt, dz, jnp.concatenate(gn_parts, axis=1), _colsum(dyt * xs)

    dy, dz, gs["ssm_norm"], g_dskip_ch = _rows(
        gated_norm_bwd, S, 128, [_row(y), _row(xbc_c, D_INNER, 0), _row(z), _row(dssm), _full(dskip_ch), _full(sp["ssm_norm"])],
        [("row", D_INNER, F32), ("row", D_INNER, BF16), ("acc", (1, D_INNER), F32), ("acc", (1, D_INNER), F32)],
        "gated_norm_bwd")
    gs["d_skip"] = jnp.sum(g_dskip_ch.reshape(N_HEADS_SSM, SSM_HEADDIM), axis=1).reshape(1, N_HEADS_SSM)
    dxs, dbm, dcm, ddt_x, dcum = _ssd_bwd(xbc_c, dt, cum, cumt_g, spread, states, dy, dskip_ch)
    ddt_raw, g_dtb, g_alog = _dt_bwd(dt_raw, dt_bias_p, a_log_p, ddt_x, dcum)
    gs["dt_bias"] = g_dtb[:, :N_HEADS_SSM]
    gs["a_log"] = g_alog[:, :N_HEADS_SSM]
    dxbc, g_conv_w8, gs["conv_b"] = _conv_bwd(xbc, [dxs, dbm, dcm], sp["conv_w"], sp["conv_b"])
    gs["conv_w"] = g_conv_w8[:CONV_WIDTH]

    g_qkv = ex.mm(u_bf, dcqkv, ta=True, name="mmg_qkv")
    g_kr = ex.mm(u_bf, dkr, ta=True, name="mmg_kr")
    g_z = ex.mm(u_bf, dz, ta=True, name="mmg_z")
    g_xbc = ex.mm(u_bf, dxbc, ta=True, name="mmg_xbc")
    g_dt = ex.mm(u_bf, ddt_raw, ta=True, name="mmg_dt")
    g_g = ex.mm(u_bf, dgates, ta=True, name="mmg_gates")
    gw["w_in"] = [g_qkv, g_kr[:, :QK_ROPE], g_z, g_xbc, g_dt[:, :N_HEADS_SSM], g_g]
    du = ex.mm(dcqkv, wp["qkv"], tb=True, name="mmb_qkv")
    du = ex.mm(dkr, wp["kr"], tb=True, add=du, name="mmb_kr")
    du = ex.mm(ddt_raw, wp["dt"], tb=True, add=du, name="mmb_dt")
    du = ex.mm(dz, wp["z"], tb=True, add=du, name="mmb_z")
    du = ex.mm(dxbc, wp["xbc"], tb=True, add=du, name="mmb_xbc")
    du = ex.mm(dgates, wp["g"], tb=True, add=du, name="mmb_gates")

    def pre_bwd(i, n, x, du, dh, g):
        dx, rows = _norm_bwd(x, _rstd(x), g, du)
        return dh + dx, _colsum(rows)

    grad_x, gs["mix_norm_pre"] = _rows(pre_bwd, S, RW, [_row(x), _row(du), _row(dh1), _full(sp["mix_norm_pre"])],
                                       [("row", D_MODEL, F32), ("acc", (1, D_MODEL), F32)], "norm_pre_bwd")
    return loss, grad_x, gs


BIG = (
    ("w_in", (2048, 3872), 1), ("w_uq", (512, 768), 1), ("w_ukv", (512, 1024), 1), ("w_attn_o", (512, 2048), 0),
    ("w_ssm_o", (1024, 2048), 0), ("w_out", (512, 2048), 0), ("w_gate", (2048, 1408), 1), ("w_up", (2048, 1408), 1),
    ("w_down", (1408, 2048), 0), ("w_ple_gate", (512, 2048), 0), ("w_ple", (256, 512), 1),
)
SMALL = (
    ("mix_norm_pre", 2048), ("mix_norm_post", 2048), ("q_norm", 512), ("kv_norm", 512), ("conv_b", 6144), ("dt_bias", 64),
    ("a_log", 64), ("d_skip", 64), ("ssm_norm", 4096), ("ffn_norm_pre", 2048), ("ffn_norm_post", 2048),
    ("ple_norm_pre", 2048), ("ple_norm_post", 2048),
)
CONV_W_LEN = CONV_WIDTH * CONV_DIM
SMALL_ROWS = 384


def _place():
    return lax.axis_index("x"), lax.axis_index("y"), lax.axis_index("c")


def _flip(v, bit):
    return 1 - v if bit else v


def _alone(hook, name):
    n_in, n_out = len(hook.ins), len(hook.out_shapes)

    def body(*refs):
        start, finish = hook.make(refs[:n_in], refs[n_in:n_in + n_out], refs[n_in + n_out:])
        start()
        finish()

    return list(pl.pallas_call(
        body, name=name, out_shape=tuple(hook.out_shapes),
        in_specs=[pl.BlockSpec(memory_space=pl.ANY)] * n_in,
        out_specs=tuple(pl.BlockSpec(memory_space=pl.ANY) for _ in range(n_out)),
        scratch_shapes=[pltpu.SemaphoreType.DMA((s,)) for s in hook.sems],
        input_output_aliases=hook.aliases,
    )(*hook.ins))


def _simple(copies):
    def start():
        for cp in copies:
            cp.start()

    def finish():
        for cp in copies:
            cp.wait()

    return start, finish


def _gather_hook(shards):
    n = len(shards)

    def make(ins, outs, sems):
        send_sems, recv_sems, fwd_send_sems, fwd_recv_sems = sems
        x, y, c = _place()
        me = 2 * x + y
        far, near = [], []
        for a in range(n):
            half = shards[a].shape[0] // 2
            lo = pl.multiple_of(c * half, SUBLANE)
            for k in (1, 2, 3):
                px, py = _flip(x, k >> 1), _flip(y, k & 1)
                far.append(pltpu.make_async_remote_copy(
                    src_ref=ins[a].at[pl.ds(lo, half), :], dst_ref=outs[a].at[me, pl.ds(lo, half), :],
                    send_sem=send_sems.at[3 * a + k - 1], recv_sem=recv_sems.at[3 * a + k - 1],
                    device_id=(px, py, c), device_id_type=MESH_ID))
                got = outs[a].at[2 * px + py, pl.ds(lo, half), :]
                near.append(pltpu.make_async_remote_copy(
                    src_ref=got, dst_ref=got, send_sem=fwd_send_sems.at[3 * a + k - 1], recv_sem=fwd_recv_sems.at[3 * a + k - 1],
                    device_id=(x, y, 1 - c), device_id_type=MESH_ID))

        def start():
            for cp in far:
                cp.start()

        def finish():
            for cp, fwd in zip(far, near):
                cp.wait_recv()
                fwd.start()
            for cp, fwd in zip(far, near):
                cp.wait_send()
                fwd.wait()

        return start, finish

    return _Hook(shards, [jax.ShapeDtypeStruct((N_CHIPS, *s.shape), s.dtype) for s in shards], (3 * n,) * 4, make)


def _swap_hook(gs):
    n = len(gs)

    def make(ins, outs, sems):
        send_sems, recv_sems = sems
        x, y, c = _place()
        copies = []
        for a in range(n):
            half = gs[a].shape[1] // 2
            src = ins[a].at[:, pl.ds(pl.multiple_of((1 - c) * half, SUBLANE), half), :]
            copies.append(pltpu.make_async_remote_copy(
                src_ref=src, dst_ref=outs[a], send_sem=send_sems.at[a], recv_sem=recv_sems.at[a],
                device_id=(x, y, 1 - c), device_id_type=MESH_ID))
        return _simple(copies)

    return _Hook(gs, [jax.ShapeDtypeStruct((g.shape[0], g.shape[1] // 2, g.shape[2]), g.dtype) for g in gs], (n, n), make)


def _sum_rows_tile(rows, cols):
    return _tile(rows, max(2 * SUBLANE, (512 * 1024 // cols) // (2 * SUBLANE) * (2 * SUBLANE)), 2 * SUBLANE)


def _add_half(g, other, c, name):
    n, R, C = g.shape
    half = R // 2
    tr = _sum_rows_tile(half, C)
    nb = half // tr

    def body(c_ref, g_ref, o_ref, out_ref):
        out_ref[...] = (g_ref[...] + o_ref[...]).astype(out_ref.dtype)

    return pl.pallas_call(
        body, name=name,
        out_shape=jax.ShapeDtypeStruct((n, half, C), BF16),
        grid_spec=pltpu.PrefetchScalarGridSpec(
            num_scalar_prefetch=1, grid=(n, nb),
            in_specs=[pl.BlockSpec((1, tr, C), lambda j, i, c_ref: (j, c_ref[0] * nb + i, 0)),
                      pl.BlockSpec((1, tr, C), lambda j, i, c_ref: (j, i, 0))],
            out_specs=pl.BlockSpec((1, tr, C), lambda j, i, c_ref: (j, i, 0))),
        compiler_params=_params(("parallel", "parallel")),
    )(c, g, other)


def _scatter_hook(parts):
    n = len(parts)

    def make(ins, outs, sems):
        send_sems, recv_sems = sems
        x, y, c = _place()
        copies = []
        for a in range(n):
            for k in (1, 2, 3):
                px, py = _flip(x, k >> 1), _flip(y, k & 1)
                copies.append(pltpu.make_async_remote_copy(
                    src_ref=ins[a].at[2 * px + py], dst_ref=outs[a].at[k - 1], send_sem=send_sems.at[3 * a + k - 1],
                    recv_sem=recv_sems.at[3 * a + k - 1], device_id=(px, py, c), device_id_type=MESH_ID))
        return _simple(copies)

    return _Hook(parts, [jax.ShapeDtypeStruct((3, *p.shape[1:]), p.dtype) for p in parts], (3 * n, 3 * n), make)


def _add_chips(part, got, place, name):
    n, R, C = part.shape
    tr = _sum_rows_tile(R, C)
    nb = R // tr

    def body(place_ref, p_ref, g_ref, out_ref):
        out_ref[...] = ((p_ref[0].astype(F32) + g_ref[0].astype(F32)) + g_ref[1].astype(F32)) + g_ref[2].astype(F32)

    return pl.pallas_call(
        body, name=name,
        out_shape=jax.ShapeDtypeStruct((2 * R, C), F32),
        grid_spec=pltpu.PrefetchScalarGridSpec(
            num_scalar_prefetch=1, grid=(nb,),
            in_specs=[pl.BlockSpec((1, tr, C), lambda i, place_ref: (place_ref[0], i, 0)),
                      pl.BlockSpec((3, tr, C), lambda i, place_ref: (0, i, 0))],
            out_specs=pl.BlockSpec((tr, C), lambda i, place_ref: (place_ref[1] * nb + i, 0))),
        compiler_params=_params(("parallel",)),
    )(place, part, got)


def _join_hook(wholes):
    n = len(wholes)

    def make(ins, outs, sems):
        send_sems, recv_sems = sems
        x, y, c = _place()
        copies = []
        for a in range(n):
            half = wholes[a].shape[0] // 2
            rows = outs[a].at[pl.ds(pl.multiple_of(c * half, SUBLANE), half), :]
            copies.append(pltpu.make_async_remote_copy(
                src_ref=rows, dst_ref=rows, send_sem=send_sems.at[a], recv_sem=recv_sems.at[a],
                device_id=(x, y, 1 - c), device_id_type=MESH_ID))
        return _simple(copies)

    return _Hook(wholes, [jax.ShapeDtypeStruct(w.shape, w.dtype) for w in wholes], (n, n), make, aliases={a: a for a in range(n)})


def _allreduce_small(vec, name):
    R, C = vec.shape

    def body(v_ref, o_ref, buf, send_sems, recv_sems):
        x, y, c = _place()
        me = 4 * x + 2 * y + c
        buf[me] = v_ref[...]
        copies = []
        for k in range(1, N_DEV):
            peer = (_flip(x, (k >> 2) & 1), _flip(y, (k >> 1) & 1), _flip(c, k & 1))
            copies.append(pltpu.make_async_remote_copy(
                src_ref=v_ref, dst_ref=buf.at[me], send_sem=send_sems.at[k - 1], recv_sem=recv_sems.at[k - 1],
                device_id=peer, device_id_type=MESH_ID))
        for cp in copies:
            cp.start()
        for cp in copies:
            cp.wait()
        tot = buf[0]
        for d in range(1, N_DEV):
            tot = tot + buf[d]
        o_ref[...] = tot

    return pl.pallas_call(
        body, name=name,
        out_shape=jax.ShapeDtypeStruct((R, C), F32),
        in_specs=[pl.BlockSpec(memory_space=pltpu.VMEM)],
        out_specs=pl.BlockSpec(memory_space=pltpu.VMEM),
        scratch_shapes=[pltpu.VMEM((N_DEV, R, C), F32), pltpu.SemaphoreType.DMA((N_DEV - 1,)), pltpu.SemaphoreType.DMA((N_DEV - 1,))],
    )(vec)


def _unstack(gathered, shape, axis):
    if axis == 0:
        return gathered.reshape(N_CHIPS * shape[0], shape[1])
    return jnp.concatenate([gathered[j] for j in range(N_CHIPS)], axis=1)


def _stack(whole, shape, axis):
    if axis == 0:
        return whole.reshape(N_CHIPS, shape[0], shape[1])
    pieces = whole if isinstance(whole, (list, tuple)) else [whole]
    shards = []
    for j in range(N_CHIPS):
        lo, hi, off, cols = j * shape[1], (j + 1) * shape[1], 0, []
        for p in pieces:
            a, b = max(lo, off), min(hi, off + p.shape[1])
            if a < b:
                cols.append(p[:, a - off:b - off])
            off += p.shape[1]
        shards.append(cols[0] if len(cols) == 1 else jnp.concatenate(cols, axis=1))
    return jnp.stack(shards)


GATHER_FIRST = ("w_in", "w_uq", "w_ukv")
GATHER_IN = {"mm_z": ("w_attn_o", "w_ssm_o", "w_out"), "mm_xbc": ("w_gate", "w_up"), "mm_gates": ("w_down", "w_ple_gate", "w_ple")}
REDUCE = (
    (("w_ple", "w_ple_gate", "w_down"), "mmb_down", "mmg_gate", "mmg_up"),
    (("w_gate", "w_up"), "mmb_gate", "mmb_up", "mmg_out"),
    (("w_out", "w_attn_o", "w_ssm_o"), "mmb_ssm_o", "mmg_z", "mmg_xbc"),
    (("w_uq", "w_ukv"), "mmb_ukv", "mmg_gates", "mmb_z"),
    (("w_in",), "mmb_z", "mmb_xbc", "mmb_gates"),
)


class _Exchange:
    def __init__(self, shards, chip, core):
        self.shards, self.chip = shards, chip
        self.core_arr = core.reshape(1).astype(jnp.int32)
        self.place_arr = jnp.stack([chip, core]).astype(jnp.int32)
        self.shape = {n: (shape, axis) for n, shape, axis in BIG}
        self.whole, self.grads, self.reduced, self.pending, self.tails = {}, {}, {}, {}, 0
        hook, done = self._gather(GATHER_FIRST)
        done(_alone(hook, "gather_first"))
        for host, names in GATHER_IN.items():
            self._arm(host, *self._gather(names))

    def _arm(self, host, hook, done):
        self.pending.setdefault(host, []).append((hook, done))

    def _gather(self, names):
        shards = [self.shards[n].astype(BF16) for n in names]

        def done(outs):
            for n, s, g in zip(names, shards, outs):
                self.whole[n] = _unstack(lax.dynamic_update_slice(g, s[None], (self.chip, 0, 0)), *self.shape[n])

        return _gather_hook(shards), done

    def __getitem__(self, name):
        return self.whole[name]

    def __setitem__(self, name, grad):
        self.grads[name] = grad
        for names, swap_host, scatter_host, join_host in REDUCE:
            if name in names and all(n in self.grads for n in names):
                self._reduce(names, swap_host, scatter_host, join_host)

    def _reduce(self, names, swap_host, scatter_host, join_host):
        stacked = [_stack(self.grads[n], *self.shape[n]) for n in names]

        def joined(outs):
            for n, r in zip(names, outs):
                self.reduced[n] = r.reshape(1, *self.shape[n][0])

        def swapped(outs):
            parts = [_add_half(g, o, self.core_arr, "add_half_" + n) for n, g, o in zip(names, stacked, outs)]

            def scattered(gots):
                wholes = [_add_chips(q, o, self.place_arr, "add_chips_" + n) for n, q, o in zip(names, parts, gots)]
                self._arm(join_host, _join_hook(wholes), joined)

            self._arm(scatter_host, _scatter_hook(parts), scattered)

        self._arm(swap_host, _swap_hook(stacked), swapped)

    def _run(self, todo, call):
        hook = _merge_hooks([h for h, _ in todo])
        result, outs = call(hook)
        off = 0
        for h, done in todo:
            done(outs[off:off + len(h.out_shapes)])
            off += len(h.out_shapes)
        return result

    def mm(self, a, b, *, name, **kw):
        todo = self.pending.pop(name, None)
        if not todo:
            return _mm(a, b, name=name, **kw)
        return self._run(todo, lambda hook: _mm(a, b, name=name, hook=hook, **kw))

    def finish(self):
        while self.pending:
            todo = self.pending.pop(next(iter(self.pending)))
            self.tails += 1
            self._run(todo, lambda hook: (None, _alone(hook, "exchange_tail_%d" % self.tails)))
        return self.reduced


def kernel(x, p, positions, mix_norm_pre, mix_norm_post, w_in, q_norm, w_uq, kv_norm, w_ukv, conv_w, conv_b, dt_bias, a_log, d_skip, ssm_norm, w_attn_o, w_ssm_o, w_out, ffn_norm_pre, ffn_norm_post, w_gate, w_up, w_down, ple_norm_pre, ple_norm_post, w_ple_gate, w_ple, loss_target, m_mix_norm_pre, m_mix_norm_post, m_w_in, m_q_norm, m_w_uq, m_kv_norm, m_w_ukv, m_conv_w, m_conv_b, m_dt_bias, m_a_log, m_d_skip, m_ssm_norm, m_w_attn_o, m_w_ssm_o, m_w_out, m_ffn_norm_pre, m_ffn_norm_post, m_w_gate, m_w_up, m_w_down, m_ple_norm_pre, m_ple_norm_post, m_w_ple_gate, m_w_ple, v_mix_norm_pre, v_mix_norm_post, v_w_in, v_q_norm, v_w_uq, v_kv_norm, v_w_ukv, v_conv_w, v_conv_b, v_dt_bias, v_a_log, v_d_skip, v_ssm_norm, v_w_attn_o, v_w_ssm_o, v_w_out, v_ffn_norm_pre, v_ffn_norm_post, v_w_gate, v_w_up, v_w_down, v_ple_norm_pre, v_ple_norm_post, v_w_ple_gate, v_w_ple):
    given = dict(locals())
    names = [n for n, _, _ in BIG] + [n for n, _ in SMALL] + ["conv_w"]
    order = ["mix_norm_pre", "mix_norm_post", "w_in", "q_norm", "w_uq", "kv_norm", "w_ukv", "conv_w", "conv_b", "dt_bias", "a_log",
             "d_skip", "ssm_norm", "w_attn_o", "w_ssm_o", "w_out", "ffn_norm_pre", "ffn_norm_post", "w_gate", "w_up", "w_down",
             "ple_norm_pre", "ple_norm_post", "w_ple_gate", "w_ple"]
    assert sorted(names) == sorted(order)
    cx, cy, cc = _place()
    chip = 2 * cx + cy
    conv_cols = CONV_DIM // N_CHIPS

    ex = _Exchange({n: given[n][0] for n, _, _ in BIG}, chip, cc)
    own = jnp.where(cc == 0, conv_w[0], 0.0)
    conv_vec = lax.dynamic_update_slice(jnp.zeros((CONV_WIDTH, CONV_DIM), F32), own, (0, chip * conv_cols))
    conv_full = _allreduce_small(conv_vec.reshape(CONV_W_LEN // LANE, LANE), "gather_conv_w").reshape(CONV_WIDTH, CONV_DIM)
    sp = {n: given[n] for n, _ in SMALL}
    sp["conv_w"] = conv_full

    loss_part, grad_x, gs = _local_step(x[0], p[0, 0], positions[0], ex, sp, loss_target[0])

    g_big = ex.finish()

    small_parts = [gs[n] for n, _ in SMALL] + [gs["conv_w"], loss_part[:, :1]]
    small_vec = jnp.concatenate([t.reshape(-1) for t in small_parts])
    small_vec = jnp.pad(small_vec, (0, SMALL_ROWS * LANE - small_vec.shape[0])).reshape(SMALL_ROWS, LANE)
    small_sum = _allreduce_small(small_vec, "allreduce_small").reshape(-1)
    g_small, off = {}, 0
    for n, length in SMALL:
        g_small[n] = small_sum[off:off + length].reshape(1, length)
        off += length
    g_conv = small_sum[off:off + CONV_W_LEN].reshape(CONV_WIDTH, CONV_DIM)
    g_small["conv_w"] = lax.dynamic_slice(g_conv, (0, chip * conv_cols), (CONV_WIDTH, conv_cols)).reshape(1, CONV_WIDTH, conv_cols)
    loss = small_sum[off + CONV_W_LEN]

    grads, deltas, new_m, new_v = [], [], [], []
    for n in order:
        g = g_big[n] if n in g_big else g_small[n]
        d, m_, v_ = _adamw(given[n], g, given["m_" + n], given["v_" + n], "adamw_" + n)
        grads.append(g)
        deltas.append(d)
        new_m.append(m_)
        new_v.append(v_)
    return (loss, grad_x.reshape(x.shape), *grads, *deltas, *new_m, *new_v)
```

```python
import numpy as np
import jax
import jax.numpy as jnp
from jax import lax
from jax.experimental import pallas as pl
from jax.experimental.pallas import tpu as pltpu

F32 = jnp.float32
BF16 = jnp.bfloat16

D_MODEL = 2048
N_HEADS_MLA = 16
Q_LORA = 512
KV_LORA = 512
QK_NOPE = 128
QK_ROPE = 64
V_DIM = 128
QK_DIM = QK_NOPE + QK_ROPE
ROPE_THETA = 10000.0
D_INNER = 4096
SSM_HEADDIM = 64
N_HEADS_SSM = 64
SSM_GROUPS = 8
HEADS_PER_GROUP = 8
D_STATE = 128
CONV_WIDTH = 4
CHUNK = 256
CONV_DIM = D_INNER + 2 * SSM_GROUPS * D_STATE
D_FF = 5632
PLE_DIM = 256
EPS = 1e-6
IN_SPLITS = (Q_LORA, KV_LORA, QK_ROPE, D_INNER, CONV_DIM, N_HEADS_SSM, D_MODEL, D_MODEL)

ADAM_LR = 0.001
ADAM_B1 = 0.9
ADAM_B2 = 0.999
ADAM_EPS = 1e-08
ADAM_WD = 0.01
ADAM_STEP = 10

LANE = 128
SUBLANE = 8
HEAD_PAD = 256
VMEM_LIMIT = 56 * 1024 * 1024
ATTN_TILE = 1024
NEG = -1e30

MESH_ID = pl.DeviceIdType.MESH
N_CHIPS = 4
N_DEV = 8


def _tile(n, pref, mult=LANE):
    if n <= pref:
        return n
    t = (pref // mult) * mult
    while t >= mult:
        if n % t == 0:
            return t
        t -= mult
    return n


def _params(sem, vmem=VMEM_LIMIT, **kw):
    return pltpu.CompilerParams(dimension_semantics=sem, vmem_limit_bytes=vmem, **kw)


class _Hook:
    def __init__(self, ins, out_shapes, sems, make, aliases=None):
        self.ins, self.out_shapes, self.sems, self.make, self.aliases = list(ins), list(out_shapes), tuple(sems), make, dict(aliases or {})


def _merge_hooks(hooks):
    hooks = [h for h in hooks if h is not None]
    if not hooks:
        return None
    ins, outs, sems, aliases, cuts = [], [], [], {}, []
    for h in hooks:
        cuts.append((len(ins), len(outs), len(sems)))
        aliases.update({len(ins) + i: len(outs) + o for i, o in h.aliases.items()})
        ins += h.ins
        outs += h.out_shapes
        sems += h.sems

    def make(in_refs, out_refs, sem_refs):
        pairs = []
        for h, (i0, o0, s0) in zip(hooks, cuts):
            pairs.append(h.make(in_refs[i0:i0 + len(h.ins)], out_refs[o0:o0 + len(h.out_shapes)], sem_refs[s0:s0 + len(h.sems)]))

        def start():
            for st, _ in pairs:
                st()

        def finish():
            for _, fin in pairs:
                fin()

        return start, finish

    return _Hook(ins, outs, sems, make, aliases)


def _mm(a, b, *, ta=False, tb=False, add=None, out_dtype=F32, name, tm=1024, tn=1536, tk=2048, hook=None, epi=None):
    if ta:
        K, M = a.shape
    else:
        M, K = a.shape
    N = b.shape[0] if tb else b.shape[1]
    assert (b.shape[1] if tb else b.shape[0]) == K, (a.shape, b.shape, ta, tb)
    tm, tn, tk = _tile(M, tm), _tile(N, tn), _tile(K, tk)
    nk = K // tk
    dn = (((0 if ta else 1,), (1 if tb else 0,)), ((), ()))
    has_add = add is not None
    epi_fn, extras, out_dtypes = epi if epi else (None, [], [out_dtype])
    assert not epi or (nk == 1 and not has_add)
    n_own = 2 + has_add + len(extras)
    n_main = len(out_dtypes)
    n_hin = len(hook.ins) if hook else 0
    n_hout = len(hook.out_shapes) if hook else 0
    grid = (M // tm, N // tn, nk)

    def body(*refs):
        a_ref, b_ref = refs[:2]
        c_ref = refs[2] if has_add else None
        main = refs[n_own + n_hin:n_own + n_hin + n_main]
        o_ref = main[0]
        scratch = refs[n_own + n_hin + n_main + n_hout:]
        if hook:
            start, finish = hook.make(refs[n_own:n_own + n_hin], refs[n_own + n_hin + n_main:n_own + n_hin + n_main + n_hout],
                                      scratch[len(scratch) - len(hook.sems):])
            ids = [pl.program_id(d) for d in range(3)]
            pl.when((ids[0] == 0) & (ids[1] == 0) & (ids[2] == 0))(start)
        prod = lax.dot_general(a_ref[...].astype(BF16), b_ref[...].astype(BF16), dn, preferred_element_type=F32)
        if epi:
            for ref, val in zip(main, epi_fn(prod, *[r[...] for r in refs[2 + has_add:n_own]])):
                ref[...] = val.astype(ref.dtype)
        elif nk == 1:
            o_ref[...] = ((c_ref[...] + prod) if has_add else prod).astype(out_dtype)
        else:
            acc = scratch[0]
            k = pl.program_id(2)

            @pl.when(k == 0)
            def _():
                acc[...] = (c_ref[...] + prod) if has_add else prod

            @pl.when(k > 0)
            def _():
                acc[...] += prod

            @pl.when(k == nk - 1)
            def _():
                o_ref[...] = acc[...].astype(out_dtype)
        if hook:
            pl.when((ids[0] == grid[0] - 1) & (ids[1] == grid[1] - 1) & (ids[2] == grid[2] - 1))(finish)

    a_spec = pl.BlockSpec((tk, tm), lambda i, j, k: (k, i)) if ta else pl.BlockSpec((tm, tk), lambda i, j, k: (i, k))
    b_spec = pl.BlockSpec((tn, tk), lambda i, j, k: (j, k)) if tb else pl.BlockSpec((tk, tn), lambda i, j, k: (k, j))
    in_specs = [a_spec, b_spec]
    args = [a, b]
    tile_spec = pl.BlockSpec((tm, tn), lambda i, j, k: (i, j))
    for t in ([add] if has_add else []) + list(extras):
        in_specs.append(tile_spec)
        args.append(t)
    hbm = pl.BlockSpec(memory_space=pl.ANY)
    scratch_shapes = [pltpu.VMEM((tm, tn), F32)] if nk > 1 else []
    out_shapes = [jax.ShapeDtypeStruct((M, N), dt) for dt in out_dtypes]
    if not hook:
        outs = pl.pallas_call(
            body, name=name, out_shape=tuple(out_shapes), grid=grid, in_specs=in_specs, out_specs=(tile_spec,) * n_main,
            scratch_shapes=scratch_shapes,
            compiler_params=_params(("parallel", "parallel", "arbitrary")),
        )(*args)
        return tuple(outs) if epi else outs[0]
    outs = pl.pallas_call(
        body, name=name, out_shape=(*out_shapes, *hook.out_shapes), grid=grid,
        in_specs=in_specs + [hbm] * n_hin, out_specs=(*[tile_spec] * n_main, *[hbm] * n_hout),
        scratch_shapes=scratch_shapes + [pltpu.SemaphoreType.DMA((s,)) for s in hook.sems],
        input_output_aliases={n_own + i: n_main + o for i, o in hook.aliases.items()},
        compiler_params=_params(("arbitrary", "arbitrary", "arbitrary")),
    )(*args, *hook.ins)
    return (tuple(outs[:n_main]) if epi else outs[0]), list(outs[n_main:])


def _mm_sum_nt(pairs, name, tm=1024, tn=1024):
    M, N, n = pairs[0][0].shape[0], pairs[0][1].shape[0], len(pairs)
    tm, tn = _tile(M, tm), _tile(N, tn)

    def body(*refs):
        tot = None
        for p in range(n):
            t = lax.dot_general(refs[2 * p][...].astype(BF16), refs[2 * p + 1][...].astype(BF16), (((1,), (1,)), ((), ())),
                                preferred_element_type=F32)
            tot = t if tot is None else tot + t
        refs[2 * n][...] = tot

    in_specs, args = [], []
    for a, b in pairs:
        in_specs += [pl.BlockSpec((tm, a.shape[1]), lambda i, j: (i, 0)), pl.BlockSpec((tn, b.shape[1]), lambda i, j: (j, 0))]
        args += [a, b]
    return pl.pallas_call(
        body, name=name, out_shape=jax.ShapeDtypeStruct((M, N), F32), grid=(M // tm, N // tn),
        in_specs=in_specs, out_specs=pl.BlockSpec((tm, tn), lambda i, j: (i, j)),
        compiler_params=_params(("parallel", "parallel")),
    )(*args)


def _mm_swiglu(a, w_gate, w_up, tm=1024, tn=512):
    M, K = a.shape
    N = w_gate.shape[1]
    tm, tn = _tile(M, tm), _tile(N, tn)

    def body(a_ref, g_ref, u_ref, go_ref, uo_ref, s_ref):
        av = a_ref[...].astype(BF16)
        gate = jnp.dot(av, g_ref[...].astype(BF16), preferred_element_type=F32)
        up = jnp.dot(av, u_ref[...].astype(BF16), preferred_element_type=F32)
        go_ref[...] = gate
        uo_ref[...] = up
        s_ref[...] = (gate * _sigmoid(gate) * up).astype(s_ref.dtype)

    w_spec = pl.BlockSpec((K, tn), lambda i, j: (0, j))
    o_spec = pl.BlockSpec((tm, tn), lambda i, j: (i, j))
    return pl.pallas_call(
        body, name="mm_swiglu",
        out_shape=(jax.ShapeDtypeStruct((M, N), F32), jax.ShapeDtypeStruct((M, N), F32), jax.ShapeDtypeStruct((M, N), BF16)),
        grid=(M // tm, N // tn),
        in_specs=[pl.BlockSpec((tm, K), lambda i, j: (i, 0)), w_spec, w_spec],
        out_specs=(o_spec, o_spec, o_spec),
        compiler_params=_params(("parallel", "parallel")),
    )(a, w_gate, w_up)


def _row(arr, width=None, cblk=0):
    return ("row", arr, arr.shape[1] if width is None else width, cblk)


def _full(arr):
    return ("full", arr)


def _prev8(arr):
    return ("prev8", arr)


def _next8(arr):
    return ("next8", arr)


def _rows(fn, n_rows, tm, ins, outs, name):
    tm = min(tm, n_rows)
    assert n_rows % tm == 0 and tm % SUBLANE == 0
    n = n_rows // tm
    in_specs, args = [], []
    for spec in ins:
        kind, arr = spec[0], spec[1]
        if kind == "row":
            _, _, w, cb = spec
            in_specs.append(pl.BlockSpec((tm, w), lambda i, cb=cb: (i, cb)))
        elif kind == "full":
            in_specs.append(pl.BlockSpec(arr.shape, lambda i, nd=arr.ndim: (0,) * nd))
        elif kind == "prev8":
            in_specs.append(pl.BlockSpec((SUBLANE, arr.shape[1]),
                                         lambda i: (jnp.maximum(i * (tm // SUBLANE) - 1, 0), 0)))
        elif kind == "next8":
            last = n_rows // SUBLANE - 1
            in_specs.append(pl.BlockSpec((SUBLANE, arr.shape[1]),
                                         lambda i: (jnp.minimum((i + 1) * (tm // SUBLANE), last), 0)))
        else:
            raise ValueError(kind)
        args.append(arr)
    out_shapes, out_specs = [], []
    any_acc = False
    for spec in outs:
        if spec[0] == "row":
            _, w, dt = spec
            out_shapes.append(jax.ShapeDtypeStruct((n_rows, w), dt))
            out_specs.append(pl.BlockSpec((tm, w), lambda i: (i, 0)))
        else:
            _, shp, dt = spec
            any_acc = True
            out_shapes.append(jax.ShapeDtypeStruct(shp, dt))
            out_specs.append(pl.BlockSpec(shp, lambda i, nd=len(shp): (0,) * nd))
    nin = len(ins)

    def body(*refs):
        i = pl.program_id(0)
        vals = fn(i, n, *[r[...] for r in refs[:nin]])
        for o_ref, spec, v in zip(refs[nin:], outs, vals):
            if spec[0] == "acc":
                @pl.when(i == 0)
                def _(o_ref=o_ref):
                    o_ref[...] = jnp.zeros_like(o_ref)

                o_ref[...] += v.astype(o_ref.dtype)
            else:
                o_ref[...] = v.astype(o_ref.dtype)

    res = pl.pallas_call(
        body, name=name,
        out_shape=tuple(out_shapes),
        grid=(n,),
        in_specs=in_specs,
        out_specs=tuple(out_specs),
        compiler_params=_params(("arbitrary",) if any_acc else ("parallel",)),
    )(*args)
    return res


def _rstd(x):
    return lax.rsqrt(jnp.mean(x * x, axis=-1, keepdims=True) + EPS)


def _norm_bwd(x, r, g, dy):
    xh = x * r
    dyg = dy * g
    dx = r * (dyg - xh * jnp.mean(dyg * xh, axis=-1, keepdims=True))
    return dx, dy * xh


def _sigmoid(x):
    return 0.5 * jnp.tanh(0.5 * x) + 0.5


def _colsum(v):
    return jnp.sum(v, axis=0, keepdims=True)


def _rope_tables(pos, invf):
    ang = pos.astype(F32) * invf
    lane = lax.broadcasted_iota(jnp.int32, ang.shape, 1)
    cos, sin = jnp.cos(ang), jnp.sin(ang)
    ct = jnp.where(lane < QK_ROPE, cos, 0.0)
    sa = jnp.where(lane < QK_ROPE // 2, -sin, 0.0)
    sb = jnp.where((lane >= QK_ROPE // 2) & (lane < QK_ROPE), sin, 0.0)
    return ct, sa, sb


def _rope(b, ct, sa, sb):
    return ct * b + sa * pltpu.roll(b, LANE - QK_ROPE // 2, 1) + sb * pltpu.roll(b, QK_ROPE // 2, 1)


def _rope_t(d, ct, sa, sb):
    return ct * d + pltpu.roll(sa * d, QK_ROPE // 2, 1) + pltpu.roll(sb * d, LANE - QK_ROPE // 2, 1)


def _rope_fwd(q_raw, kr_pad, pos_col, invf):
    S = q_raw.shape[0]

    def fn(i, n, q, kr, pos, invf):
        ct, sa, sb = _rope_tables(pos, invf)
        parts = []
        for h in range(N_HEADS_MLA):
            parts.append(q[:, h * HEAD_PAD:h * HEAD_PAD + LANE])
            parts.append(_rope(q[:, h * HEAD_PAD + LANE:(h + 1) * HEAD_PAD], ct, sa, sb))
        return jnp.concatenate(parts, axis=1), _rope(kr, ct, sa, sb)

    return _rows(fn, S, 256, [_row(q_raw), _row(kr_pad), _row(pos_col), _full(invf)],
                 [("row", N_HEADS_MLA * HEAD_PAD, BF16), ("row", LANE, BF16)], "rope_fwd")


def _rope_bwd(dq, dkp, pos_col, invf):
    S = dq.shape[0]
    tm = 256

    def body(dq_ref, dkp_ref, pos_ref, invf_ref, dqo_ref, dkr_ref):
        ct, sa, sb = _rope_tables(pos_ref[...], invf_ref[...])
        for h in range(N_HEADS_MLA):
            dqo_ref[:, h * HEAD_PAD:h * HEAD_PAD + LANE] = dq_ref[:, h * HEAD_PAD:h * HEAD_PAD + LANE].astype(BF16)
            dqo_ref[:, h * HEAD_PAD + LANE:(h + 1) * HEAD_PAD] = _rope_t(
                dq_ref[:, h * HEAD_PAD + LANE:(h + 1) * HEAD_PAD], ct, sa, sb).astype(BF16)
        tot = dkp_ref[0]
        for h in range(1, N_HEADS_MLA):
            tot = tot + dkp_ref[h]
        dkr_ref[...] = _rope_t(tot, ct, sa, sb).astype(BF16)

    return pl.pallas_call(
        body, name="rope_bwd",
        out_shape=(jax.ShapeDtypeStruct(dq.shape, BF16), jax.ShapeDtypeStruct((S, LANE), BF16)),
        grid=(S // tm,),
        in_specs=[pl.BlockSpec((tm, dq.shape[1]), lambda i: (i, 0)),
                  pl.BlockSpec((N_HEADS_MLA, tm, LANE), lambda i: (0, i, 0)),
                  pl.BlockSpec((tm, 1), lambda i: (i, 0)),
                  pl.BlockSpec((1, LANE), lambda i: (0, 0))],
        out_specs=(pl.BlockSpec((tm, dq.shape[1]), lambda i: (i, 0)), pl.BlockSpec((tm, LANE), lambda i: (i, 0))),
        compiler_params=_params(("parallel",)),
    )(dq, dkp, pos_col, invf)


def _row_of(col, n):
    eye = lax.broadcasted_iota(jnp.int32, (LANE, LANE), 0) == lax.broadcasted_iota(jnp.int32, (LANE, LANE), 1)
    parts = [jnp.sum(jnp.where(eye, col[i:i + LANE], 0.0), axis=0, keepdims=True) for i in range(0, n, LANE)]
    return parts[0] if len(parts) == 1 else jnp.concatenate(parts, axis=1)


def _attn_fwd(q, kv, kp, tile):
    S = q.shape[0]
    nq = S // tile
    scale = QK_DIM ** -0.5
    nt = (((1,), (1,)), ((), ()))

    def body(q_ref, kv_ref, kp_ref, o_ref, lse_ref, m_s, l_s, acc_s, s_buf):
        qi = pl.program_id(1)
        qv = q_ref[...]
        m_s[...] = jnp.full_like(m_s, NEG)
        l_s[...] = jnp.zeros_like(l_s)
        acc_s[...] = jnp.zeros_like(acc_s)

        def scores(j):
            start = pl.multiple_of(j * tile, tile)
            k = jnp.concatenate([kv_ref[pl.ds(start, tile), 0:LANE], kp_ref[pl.ds(start, tile), :]], axis=1)
            return lax.dot_general(qv, k, nt, preferred_element_type=F32) * scale

        def update(s, j):
            v = kv_ref[pl.ds(pl.multiple_of(j * tile, tile), tile), LANE:2 * LANE]
            m_old = m_s[...]
            m_new = jnp.maximum(m_old, jnp.max(s, axis=1, keepdims=True))
            alpha = jnp.exp(m_old - m_new)
            p = jnp.exp(s - m_new)
            l_s[...] = alpha * l_s[...] + jnp.sum(p, axis=1, keepdims=True)
            acc_s[...] = alpha * acc_s[...] + jnp.dot(p.astype(BF16), v, preferred_element_type=F32)
            m_s[...] = m_new

        s_buf[0] = scores(0)

        def loop_body(j, carry):
            nxt = scores(j + 1)
            update(s_buf[lax.rem(j, 2)], j)
            s_buf[lax.rem(j + 1, 2)] = nxt
            return carry

        lax.fori_loop(0, qi, loop_body, 0)
        s = s_buf[lax.rem(qi, 2)]
        row = lax.broadcasted_iota(jnp.int32, s.shape, 0)
        col = lax.broadcasted_iota(jnp.int32, s.shape, 1)
        update(jnp.where(row >= col, s, NEG), qi)
        l = l_s[...]
        o_ref[...] = (acc_s[...] / l).astype(o_ref.dtype)
        lse_ref[0, 0] = _row_of(m_s[...] + jnp.log(l), tile)

    return pl.pallas_call(
        body, name="attn_fwd",
        out_shape=(jax.ShapeDtypeStruct((S, N_HEADS_MLA * V_DIM), BF16),
                   jax.ShapeDtypeStruct((N_HEADS_MLA, nq, 1, tile), F32)),
        grid=(N_HEADS_MLA, nq),
        in_specs=[pl.BlockSpec((tile, HEAD_PAD), lambda h, i: (i, h)),
                  pl.BlockSpec((S, HEAD_PAD), lambda h, i: (0, h)),
                  pl.BlockSpec((S, LANE), lambda h, i: (0, 0))],
        out_specs=(pl.BlockSpec((tile, V_DIM), lambda h, i: (i, h)),
                   pl.BlockSpec((1, 1, 1, tile), lambda h, i: (h, i, 0, 0))),
        scratch_shapes=[pltpu.VMEM((tile, 1), F32), pltpu.VMEM((tile, 1), F32), pltpu.VMEM((tile, V_DIM), F32),
                        pltpu.VMEM((2, tile, tile), F32)],
        compiler_params=_params(("parallel", "arbitrary")),
    )(q, kv, kp)


def _attn_delta(o, do, tile):
    S = o.shape[0]
    nq = S // tile

    def body(o_ref, do_ref, d_ref):
        prod = o_ref[...].astype(F32) * do_ref[...].astype(F32)
        for h in range(N_HEADS_MLA):
            col = jnp.sum(prod[:, h * V_DIM:(h + 1) * V_DIM], axis=1, keepdims=True)
            d_ref[h, 0] = _row_of(col, tile)

    return pl.pallas_call(
        body, name="attn_delta",
        out_shape=jax.ShapeDtypeStruct((N_HEADS_MLA, nq, 1, tile), F32),
        grid=(nq,),
        in_specs=[pl.BlockSpec((tile, o.shape[1]), lambda i: (i, 0)), pl.BlockSpec((tile, o.shape[1]), lambda i: (i, 0))],
        out_specs=pl.BlockSpec((N_HEADS_MLA, 1, 1, tile), lambda i: (0, i, 0, 0)),
        compiler_params=_params(("parallel",)),
    )(o, do)


def _attn_bwd(q, kv, kp, do, lse, delta, tile):
    S = q.shape[0]
    nq = S // tile
    scale = QK_DIM ** -0.5
    nt = (((1,), (1,)), ((), ()))
    tn = (((0,), (0,)), ((), ()))

    def body(kv_ref, kp_ref, q_ref, do_ref, lse_ref, d_ref, dq_ref, dkv_ref, dkp_ref, dk_s, dv_s):
        ki = pl.program_id(1)
        k = jnp.concatenate([kv_ref[:, 0:LANE], kp_ref[...]], axis=1)
        v = kv_ref[:, LANE:2 * LANE]

        @pl.when(ki == 0)
        def _():
            dq_ref[...] = jnp.zeros_like(dq_ref)

        dk_s[...] = jnp.zeros_like(dk_s)
        dv_s[...] = jnp.zeros_like(dv_s)

        def step(qi, masked):
            start = pl.multiple_of(qi * tile, tile)
            qv = q_ref[pl.ds(start, tile), :]
            dov = do_ref[pl.ds(start, tile), :]
            st = lax.dot_general(k, qv, nt, preferred_element_type=F32) * scale
            pt = jnp.exp(st - lse_ref[0, qi])
            if masked:
                krow = lax.broadcasted_iota(jnp.int32, pt.shape, 0)
                qcol = lax.broadcasted_iota(jnp.int32, pt.shape, 1)
                pt = jnp.where(krow <= qcol, pt, 0.0)
            dv_s[...] += jnp.dot(pt.astype(BF16), dov, preferred_element_type=F32)
            dpt = lax.dot_general(v, dov, nt, preferred_element_type=F32)
            dst = (pt * (dpt - d_ref[0, qi]) * scale).astype(BF16)
            dk_s[...] += jnp.dot(dst, qv, preferred_element_type=F32)
            dq_ref[pl.ds(start, tile), :] += lax.dot_general(dst, k, tn, preferred_element_type=F32)

        step(ki, True)

        def loop_body(qi, carry):
            step(qi, False)
            return carry

        lax.fori_loop(ki + 1, nq, loop_body, 0)
        dkv_ref[...] = jnp.concatenate([dk_s[:, 0:LANE], dv_s[...]], axis=1).astype(dkv_ref.dtype)
        dkp_ref[0] = dk_s[:, LANE:2 * LANE]

    return pl.pallas_call(
        body, name="attn_bwd",
        out_shape=(jax.ShapeDtypeStruct((S, N_HEADS_MLA * HEAD_PAD), F32),
                   jax.ShapeDtypeStruct((S, N_HEADS_MLA * HEAD_PAD), BF16),
                   jax.ShapeDtypeStruct((N_HEADS_MLA, S, LANE), F32)),
        grid=(N_HEADS_MLA, nq),
        in_specs=[pl.BlockSpec((tile, HEAD_PAD), lambda h, i: (i, h)),
                  pl.BlockSpec((tile, LANE), lambda h, i: (i, 0)),
                  pl.BlockSpec((S, HEAD_PAD), lambda h, i: (0, h)),
                  pl.BlockSpec((S, V_DIM), lambda h, i: (0, h)),
                  pl.BlockSpec((1, nq, 1, tile), lambda h, i: (h, 0, 0, 0)),
                  pl.BlockSpec((1, nq, 1, tile), lambda h, i: (h, 0, 0, 0))],
        out_specs=(pl.BlockSpec((S, HEAD_PAD), lambda h, i: (0, h)),
                   pl.BlockSpec((tile, HEAD_PAD), lambda h, i: (i, h)),
                   pl.BlockSpec((1, tile, LANE), lambda h, i: (h, i, 0))),
        scratch_shapes=[pltpu.VMEM((tile, HEAD_PAD), F32), pltpu.VMEM((tile, V_DIM), F32)],
        compiler_params=_params(("parallel", "arbitrary")),
    )(kv, kp, q, do, lse, delta)


def _shift_down(cur, halo, k):
    sh = pltpu.roll(cur, k, 0)
    hs = pltpu.roll(halo, k, 0)
    rows = lax.broadcasted_iota(jnp.int32, hs.shape, 0)
    first = jnp.where(rows < k, hs, sh[0:SUBLANE])
    if cur.shape[0] == SUBLANE:
        return first
    return jnp.concatenate([first, sh[SUBLANE:]], axis=0)


def _shift_up(cur, nxt, k):
    n = cur.shape[0]
    sh = pltpu.roll(cur, n - k, 0)
    ns = pltpu.roll(nxt, SUBLANE - k, 0)
    rows = lax.broadcasted_iota(jnp.int32, ns.shape, 0)
    last = jnp.where(rows >= SUBLANE - k, ns, sh[n - SUBLANE:])
    if n == SUBLANE:
        return last
    return jnp.concatenate([sh[:n - SUBLANE], last], axis=0)


def _conv_pre(cur, halo, w, b):
    shifted = [_shift_down(cur, halo, k) for k in range(1, CONV_WIDTH)]
    out = b + w[3:4] * cur
    for k in range(1, CONV_WIDTH):
        out = out + w[3 - k:4 - k] * shifted[k - 1]
    return out, shifted


def _conv_fwd(xbc, w, b):
    S = xbc.shape[0]

    def fn(i, n, cur, prev, w, b):
        halo = jnp.where(i > 0, prev, 0.0)
        pre, _ = _conv_pre(cur, halo, w, b)
        return (pre * _sigmoid(pre),)

    return _rows(fn, S, 256, [_row(xbc), _prev8(xbc), _full(w), _full(b)], [("row", xbc.shape[1], F32)], "conv_fwd")[0]


def _conv_bwd(xbc, dacts, w, b):
    S, C = xbc.shape

    def dsilu(pre):
        s = _sigmoid(pre)
        return s * (1.0 + pre * (1.0 - s))

    def fn(i, n, cur, prev, nxt, *rest):
        k3 = len(dacts)
        dcur = jnp.concatenate(rest[:k3], axis=1)
        dnxt = jnp.concatenate(rest[k3:2 * k3], axis=1)
        w, b = rest[2 * k3:]
        halo = jnp.where(i > 0, prev, 0.0)
        pre, shifted = _conv_pre(cur, halo, w, b)
        dpre = dcur * dsilu(pre)
        pre_n, _ = _conv_pre(nxt, cur[cur.shape[0] - SUBLANE:], w, b)
        dpre_n = jnp.where(i < n - 1, dnxt * dsilu(pre_n), 0.0)
        dx = w[3:4] * dpre
        rows = lax.broadcasted_iota(jnp.int32, (SUBLANE, C), 0)
        dw = jnp.where(rows == 3, _colsum(dpre * cur), 0.0)
        for k in range(1, CONV_WIDTH):
            dx = dx + w[3 - k:4 - k] * _shift_up(dpre, dpre_n, k)
            dw = dw + jnp.where(rows == 3 - k, _colsum(dpre * shifted[k - 1]), 0.0)
        return dx, dw, _colsum(dpre)

    return _rows(fn, S, 256, [_row(xbc), _prev8(xbc), _next8(xbc), *[_row(d) for d in dacts], *[_next8(d) for d in dacts],
                              _full(w), _full(b)],
                 [("row", C, BF16), ("acc", (SUBLANE, C), F32), ("acc", (1, C), F32)], "conv_bwd")


def _softplus(x):
    return jnp.maximum(x, 0.0) + jnp.log1p(jnp.exp(-jnp.abs(x)))


def _cumsum_rows(x):
    rows = lax.broadcasted_iota(jnp.int32, x.shape, 0)
    s = 1
    while s < x.shape[0]:
        x = x + jnp.where(rows >= s, pltpu.roll(x, s, 0), 0.0)
        s *= 2
    return x


def _revcumsum_rows(x):
    n = x.shape[0]
    rows = lax.broadcasted_iota(jnp.int32, x.shape, 0)
    s = 1
    while s < n:
        x = x + jnp.where(rows < n - s, pltpu.roll(x, n - s, 0), 0.0)
        s *= 2
    return x


def _dt_prep(dt_raw, dt_bias, a_log):
    S = dt_raw.shape[0]

    def body(raw_ref, bias_ref, alog_ref, dt_ref, cum_ref, cumt_ref):
        dt = _softplus(raw_ref[...] + bias_ref[...])
        cum = _cumsum_rows(dt * (-jnp.exp(alog_ref[...])))
        dt_ref[...] = dt
        cum_ref[...] = cum
        cumt_ref[...] = cum.T

    return pl.pallas_call(
        body, name="dt_prep",
        out_shape=(jax.ShapeDtypeStruct((S, LANE), F32), jax.ShapeDtypeStruct((S, LANE), F32),
                   jax.ShapeDtypeStruct((LANE, S), F32)),
        grid=(S // CHUNK,),
        in_specs=[pl.BlockSpec((CHUNK, LANE), lambda i: (i, 0)), pl.BlockSpec((1, LANE), lambda i: (0, 0)),
                  pl.BlockSpec((1, LANE), lambda i: (0, 0))],
        out_specs=(pl.BlockSpec((CHUNK, LANE), lambda i: (i, 0)), pl.BlockSpec((CHUNK, LANE), lambda i: (i, 0)),
                   pl.BlockSpec((LANE, CHUNK), lambda i: (0, i))),
        compiler_params=_params(("parallel",)),
    )(dt_raw, dt_bias, a_log)


_NT = (((1,), (1,)), ((), ()))
_TN = (((0,), (0,)), ((), ()))
P = SSM_HEADDIM
GW = HEADS_PER_GROUP * SSM_HEADDIM


PAIRS = HEADS_PER_GROUP // 2
SPREAD_W = HEADS_PER_GROUP * LANE


def _spread_matrix():
    e = np.zeros((SSM_GROUPS, LANE, SPREAD_W), np.float32)
    for g in range(SSM_GROUPS):
        for r in range(HEADS_PER_GROUP):
            e[g, g * HEADS_PER_GROUP + r, r * LANE:(r + 1) * LANE] = 1.0
    return jnp.asarray(e, BF16)


def _pieces(v, n):
    out = []
    for _ in range(n):
        p = v.astype(BF16)
        out.append(p)
        v = v - p.astype(F32)
    return out


def _spread(v, e, n):
    tot = None
    for p in _pieces(v, n):
        t = jnp.dot(p, e, preferred_element_type=F32)
        tot = t if tot is None else tot + t
    return tot


def _gather_rows(z, e):
    hi, lo = _pieces(z, 2)
    return lax.dot_general(hi, e, _NT, preferred_element_type=F32) + lax.dot_general(lo, e, _NT, preferred_element_type=F32)


def _decay_pair(cc, cr, transposed):
    L = cc.shape[0]
    halves = []
    for h in range(L // LANE):
        i = lax.broadcasted_iota(jnp.int32, (L, LANE), 0)
        j = lax.broadcasted_iota(jnp.int32, (L, LANE), 1) + h * LANE
        crh = cr[:, h * LANE:(h + 1) * LANE]
        if transposed:
            halves.append(jnp.exp(jnp.where(j >= i, crh - cc, NEG)))
        else:
            halves.append(jnp.exp(jnp.where(i >= j, cc - crh, NEG)))
    return jnp.concatenate(halves, axis=1)


def _ssd_fwd(xbc_c, dt, cum, cumt_g, spread):
    S = xbc_c.shape[0]
    nc = S // CHUNK
    L = CHUNK
    boff = D_INNER // D_STATE

    def body(x_ref, b_ref, c_ref, dt_ref, cum_ref, cumt_ref, e_ref, y_ref, st_ref, state):
        c = pl.program_id(1)

        @pl.when(c == 0)
        def _():
            state[...] = jnp.zeros_like(state)

        e = e_ref[0]
        bm = b_ref[...].astype(BF16)
        cm = c_ref[...].astype(BF16)
        cb = lax.dot_general(cm, bm, _NT, preferred_element_type=F32)
        rep_cum = _spread(cum_ref[...], e, 3)
        rep_dt = _spread(dt_ref[...], e, 2)
        lo = lax.broadcasted_iota(jnp.int32, (L, LANE), 1) < P
        lo1 = lax.broadcasted_iota(jnp.int32, (1, LANE), 1) < P
        top = lax.broadcasted_iota(jnp.int32, (LANE, LANE), 0) < P
        for p in range(PAIRS):
            t0, t1 = 2 * p * LANE, (2 * p + 1) * LANE
            cc0, cc1 = rep_cum[:, t0:t0 + LANE], rep_cum[:, t1:t1 + LANE]
            ccp = jnp.where(lo, cc0, cc1)
            cl0, cl1 = cc0[L - 1:L, :], cc1[L - 1:L, :]
            clp = jnp.where(lo1, cl0, cl1)
            xdt = x_ref[:, p * LANE:(p + 1) * LANE] * jnp.where(lo, rep_dt[:, t0:t0 + LANE], rep_dt[:, t1:t1 + LANE])
            xdb = xdt.astype(BF16)
            ys = []
            for r, cc in ((2 * p, cc0), (2 * p + 1, cc1)):
                m = (cb * _decay_pair(cc, cumt_ref[0, r:r + 1, :], False)).astype(BF16)
                ys.append(jnp.dot(m, xdb, preferred_element_type=F32))
            st = state[p * LANE:(p + 1) * LANE, :]
            st_ref[0, 0, p * LANE:(p + 1) * LANE, :] = st
            yoff = lax.dot_general(cm, st.astype(BF16), _NT, preferred_element_type=F32) * jnp.exp(ccp)
            y_ref[:, p * LANE:(p + 1) * LANE] = jnp.where(lo, ys[0], ys[1]) + yoff
            wend = jnp.exp(clp - ccp)
            ecl = jnp.where(top, jnp.exp(cl0), jnp.exp(cl1))
            state[p * LANE:(p + 1) * LANE, :] = st * ecl + lax.dot_general(
                (xdt * wend).astype(BF16), bm, _TN, preferred_element_type=F32)

    return pl.pallas_call(
        body, name="ssd_fwd",
        out_shape=(jax.ShapeDtypeStruct((S, D_INNER), F32), jax.ShapeDtypeStruct((SSM_GROUPS, nc, GW, D_STATE), F32)),
        grid=(SSM_GROUPS, nc),
        in_specs=[pl.BlockSpec((L, GW), lambda g, c: (c, g)),
                  pl.BlockSpec((L, D_STATE), lambda g, c: (c, boff + g)),
                  pl.BlockSpec((L, D_STATE), lambda g, c: (c, boff + SSM_GROUPS + g)),
                  pl.BlockSpec((L, LANE), lambda g, c: (c, 0)),
                  pl.BlockSpec((L, LANE), lambda g, c: (c, 0)),
                  pl.BlockSpec((1, HEADS_PER_GROUP, L), lambda g, c: (g, 0, c)),
                  pl.BlockSpec((1, LANE, SPREAD_W), lambda g, c: (g, 0, 0))],
        out_specs=(pl.BlockSpec((L, GW), lambda g, c: (c, g)),
                   pl.BlockSpec((1, 1, GW, D_STATE), lambda g, c: (g, c, 0, 0))),
        scratch_shapes=[pltpu.VMEM((GW, D_STATE), F32)],
        compiler_params=_params(("parallel", "arbitrary")),
    )(xbc_c, xbc_c, xbc_c, dt, cum, cumt_g, spread)


def _ssd_bwd(xbc_c, dt, cum, cumt_g, spread, states, dy, d_skip):
    S = xbc_c.shape[0]
    nc = S // CHUNK
    L = CHUNK
    boff = D_INNER // D_STATE
    rev = lambda c: nc - 1 - c

    def body(x_ref, b_ref, c_ref, dt_ref, cum_ref, cumt_ref, e_ref, st_ref, dy_ref, skip_ref,
             dx_ref, db_ref, dc_ref, ddt_ref, dcum_ref, dstate):
        c = pl.program_id(1)

        @pl.when(c == 0)
        def _():
            dstate[...] = jnp.zeros_like(dstate)

        e = e_ref[0]
        bf = b_ref[...]
        bm = bf.astype(BF16)
        cm = c_ref[...].astype(BF16)
        cb = lax.dot_general(cm, bm, _NT, preferred_element_type=F32)
        cbt = lax.dot_general(bm, cm, _NT, preferred_element_type=F32)
        rep_cum = _spread(cum_ref[...], e, 3)
        rep_dt = _spread(dt_ref[...], e, 2)
        lane = lax.broadcasted_iota(jnp.int32, (L, LANE), 1)
        lo = lane < P
        lo1 = lax.broadcasted_iota(jnp.int32, (1, LANE), 1) < P
        top = lax.broadcasted_iota(jnp.int32, (LANE, LANE), 0) < P
        last = lax.broadcasted_iota(jnp.int32, (L, LANE), 0) == L - 1
        dcb = jnp.zeros((L, L), F32)
        dcbt = jnp.zeros((L, L), F32)
        dbs = jnp.zeros((L, D_STATE), F32)
        dcs = jnp.zeros((L, D_STATE), F32)
        zs, zds = [], []
        for p in range(PAIRS):
            sl = slice(p * LANE, (p + 1) * LANE)
            t0, t1 = 2 * p * LANE, (2 * p + 1) * LANE
            cc0, cc1 = rep_cum[:, t0:t0 + LANE], rep_cum[:, t1:t1 + LANE]
            ccp = jnp.where(lo, cc0, cc1)
            cl0, cl1 = cc0[L - 1:L, :], cc1[L - 1:L, :]
            w0, w1 = jnp.exp(cl0 - cc0), jnp.exp(cl1 - cc1)
            wend = jnp.where(lo, w0, w1)
            ecc = jnp.exp(ccp)
            ecl0, ecl1 = jnp.exp(cl0), jnp.exp(cl1)
            dtp = jnp.where(lo, rep_dt[:, t0:t0 + LANE], rep_dt[:, t1:t1 + LANE])
            xp = x_ref[:, sl]
            xdt = xp * dtp
            xdb = xdt.astype(BF16)
            dyp = dy_ref[:, sl]
            st = st_ref[0, 0, sl, :]
            stb = st.astype(BF16)
            ds = dstate[sl, :]
            dsb = ds.astype(BF16)
            yoff = lax.dot_general(cm, stb, _NT, preferred_element_type=F32) * ecc
            dye = (dyp * ecc).astype(BF16)
            dcs = dcs + jnp.dot(dye, stb, preferred_element_type=F32)
            dstate[sl, :] = jnp.where(top, ecl0, ecl1) * ds + lax.dot_general(dye, cm, _TN, preferred_element_type=F32)
            dxd = lax.dot_general(bm, dsb, _NT, preferred_element_type=F32) * wend
            sst = ds * st
            dyo = dyp * yoff
            mts = []
            for r, cc, w, ecl, keep, keep_rows in ((2 * p, cc0, w0, ecl0, lo, top), (2 * p + 1, cc1, w1, ecl1, ~lo, ~top)):
                cr = cumt_ref[0, r:r + 1, :]
                decay = _decay_pair(cc, cr, False)
                decay_t = _decay_pair(cc, cr, True)
                m = cb * decay
                mt = cbt * decay_t
                dyr = jnp.where(keep, dyp, 0.0).astype(BF16)
                g = lax.dot_general(dyr, xdb, _NT, preferred_element_type=F32)
                gt = lax.dot_general(xdb, dyr, _NT, preferred_element_type=F32)
                q = g * m
                qt = gt * mt
                dcb = dcb + g * decay
                dcbt = dcbt + gt * decay_t
                mts.append(jnp.dot(mt.astype(BF16), dyr, preferred_element_type=F32))
                t = jnp.dot(jnp.where(keep, xdt, 0.0).astype(BF16), dsb, preferred_element_type=F32)
                dbs = dbs + t * w
                tb = t * bf * w
                end_row = _colsum(tb) + ecl * _colsum(jnp.where(keep_rows, sst, 0.0))
                z = (q[:, 0:LANE] + q[:, LANE:2 * LANE]) - (qt[:, 0:LANE] + qt[:, LANE:2 * LANE])
                z = z + jnp.where(keep, dyo, 0.0) - tb + jnp.where(last, end_row, 0.0)
                zs.append(z)
            dxd = dxd + mts[0] + mts[1]
            dx_ref[:, sl] = dxd * dtp + dyp * skip_ref[:, sl]
            zd = dxd * xp
            zds.append(jnp.where(lo, zd, 0.0))
            zds.append(jnp.where(lo, 0.0, zd))
        dc_ref[...] = dcs + jnp.dot(dcb.astype(BF16), bm, preferred_element_type=F32)
        db_ref[...] = dbs + jnp.dot(dcbt.astype(BF16), cm, preferred_element_type=F32)
        dcum_ref[0] = _gather_rows(jnp.concatenate(zs, axis=1), e)
        ddt_ref[0] = _gather_rows(jnp.concatenate(zds, axis=1), e)

    return pl.pallas_call(
        body, name="ssd_bwd",
        out_shape=(jax.ShapeDtypeStruct((S, D_INNER), F32),
                   jax.ShapeDtypeStruct((S, SSM_GROUPS * D_STATE), F32),
                   jax.ShapeDtypeStruct((S, SSM_GROUPS * D_STATE), F32),
                   jax.ShapeDtypeStruct((SSM_GROUPS, S, LANE), F32),
                   jax.ShapeDtypeStruct((SSM_GROUPS, S, LANE), F32)),
        grid=(SSM_GROUPS, nc),
        in_specs=[pl.BlockSpec((L, GW), lambda g, c: (rev(c), g)),
                  pl.BlockSpec((L, D_STATE), lambda g, c: (rev(c), boff + g)),
                  pl.BlockSpec((L, D_STATE), lambda g, c: (rev(c), boff + SSM_GROUPS + g)),
                  pl.BlockSpec((L, LANE), lambda g, c: (rev(c), 0)),
                  pl.BlockSpec((L, LANE), lambda g, c: (rev(c), 0)),
                  pl.BlockSpec((1, HEADS_PER_GROUP, L), lambda g, c: (g, 0, rev(c))),
                  pl.BlockSpec((1, LANE, SPREAD_W), lambda g, c: (g, 0, 0)),
                  pl.BlockSpec((1, 1, GW, D_STATE), lambda g, c: (g, rev(c), 0, 0)),
                  pl.BlockSpec((L, GW), lambda g, c: (rev(c), g)),
                  pl.BlockSpec((1, GW), lambda g, c: (0, g))],
        out_specs=(pl.BlockSpec((L, GW), lambda g, c: (rev(c), g)),
                   pl.BlockSpec((L, D_STATE), lambda g, c: (rev(c), g)),
                   pl.BlockSpec((L, D_STATE), lambda g, c: (rev(c), g)),
                   pl.BlockSpec((1, L, LANE), lambda g, c: (g, rev(c), 0)),
                   pl.BlockSpec((1, L, LANE), lambda g, c: (g, rev(c), 0))),
        scratch_shapes=[pltpu.VMEM((GW, D_STATE), F32)],
        compiler_params=_params(("parallel", "arbitrary")),
    )(xbc_c, xbc_c, xbc_c, dt, cum, cumt_g, spread, states, dy, d_skip)


def _dt_bwd(dt_raw, dt_bias, a_log, ddt_x, dcum):
    S = dt_raw.shape[0]
    n = S // CHUNK

    def body(raw_ref, ddx_ref, dcu_ref, bias_ref, alog_ref, draw_ref, gb_ref, ga_ref):
        i = pl.program_id(0)

        @pl.when(i == 0)
        def _():
            gb_ref[...] = jnp.zeros_like(gb_ref)
            ga_ref[...] = jnp.zeros_like(ga_ref)

        ddx, dcu = ddx_ref[0], dcu_ref[0]
        for g in range(1, SSM_GROUPS):
            ddx = ddx + ddx_ref[g]
            dcu = dcu + dcu_ref[g]
        xx = raw_ref[...] + bias_ref[...]
        dt = _softplus(xx)
        a = -jnp.exp(alog_ref[...])
        dda = _revcumsum_rows(dcu)
        lane = lax.broadcasted_iota(jnp.int32, xx.shape, 1)
        draw = jnp.where(lane < N_HEADS_SSM, (ddx + dda * a) * _sigmoid(xx), 0.0)
        draw_ref[...] = draw.astype(draw_ref.dtype)
        gb_ref[...] += _colsum(draw)
        ga_ref[...] += _colsum(dda * dt) * a

    row = pl.BlockSpec((CHUNK, LANE), lambda i: (i, 0))
    grp = pl.BlockSpec((SSM_GROUPS, CHUNK, LANE), lambda i: (0, i, 0))
    one = pl.BlockSpec((1, LANE), lambda i: (0, 0))
    return pl.pallas_call(
        body, name="dt_bwd",
        out_shape=(jax.ShapeDtypeStruct((S, LANE), BF16), jax.ShapeDtypeStruct((1, LANE), F32), jax.ShapeDtypeStruct((1, LANE), F32)),
        grid=(n,),
        in_specs=[row, grp, grp, one, one],
        out_specs=(row, one, one),
        compiler_params=_params(("arbitrary",)),
    )(dt_raw, ddt_x, dcum, dt_bias, a_log)


def _adamw(w, g, m, v, name):
    shape = w.shape
    cols = shape[-1]
    rows = int(np.prod(shape[:-1]))
    w2, g2, m2, v2 = (t.reshape(1, rows, cols) for t in (w, g, m, v))
    tr = rows if rows * cols <= 512 * 1024 else _tile(rows, max(SUBLANE, (512 * 1024 // cols) // SUBLANE * SUBLANE), SUBLANE)
    c1 = 1.0 - ADAM_B1 ** ADAM_STEP
    c2 = 1.0 - ADAM_B2 ** ADAM_STEP

    def body(w_ref, g_ref, m_ref, v_ref, d_ref, mo_ref, vo_ref):
        gv = g_ref[...]
        mn = ADAM_B1 * m_ref[...] + (1.0 - ADAM_B1) * gv
        vn = ADAM_B2 * v_ref[...] + (1.0 - ADAM_B2) * (gv * gv)
        d_ref[...] = -ADAM_LR * ((mn / c1) / (jnp.sqrt(vn / c2) + ADAM_EPS) + ADAM_WD * w_ref[...])
        mo_ref[...] = mn
        vo_ref[...] = vn

    spec = pl.BlockSpec((1, tr, cols), lambda i: (0, i, 0))
    outs = pl.pallas_call(
        body, name=name,
        out_shape=tuple(jax.ShapeDtypeStruct((1, rows, cols), F32) for _ in range(3)),
        grid=(rows // tr,),
        in_specs=[spec] * 4, out_specs=(spec,) * 3,
        compiler_params=_params(("parallel",)),
    )(w2, g2, m2, v2)
    return tuple(o.reshape(shape) for o in outs)


def _prep_weights(w_in, w_uq):
    offs = np.cumsum((0,) + IN_SPLITS)
    pad = lambda t: jnp.pad(t, ((0, 0), (0, LANE - t.shape[1])))
    pieces = dict(
        qkv=w_in[:, offs[0]:offs[2]],
        kr=pad(w_in[:, offs[2]:offs[3]]),
        z=w_in[:, offs[3]:offs[4]],
        xbc=w_in[:, offs[4]:offs[5]],
        dt=pad(w_in[:, offs[5]:offs[6]]),
        g=w_in[:, offs[6]:offs[8]],
    )
    uq = w_uq.reshape(Q_LORA, N_HEADS_MLA, QK_DIM)
    uq = jnp.pad(uq, ((0, 0), (0, 0), (0, HEAD_PAD - QK_DIM))).reshape(Q_LORA, N_HEADS_MLA * HEAD_PAD)
    return pieces, uq


def _local_step(x, p, positions, ex, sp, target):
    W = gw = ex
    S = x.shape[0]
    tile = min(ATTN_TILE, S)
    pos_col = positions.reshape(S, 1)
    invf = ROPE_THETA ** (-jnp.arange(0, QK_ROPE, 2, dtype=F32) / QK_ROPE)
    invf = jnp.pad(jnp.concatenate([invf, invf]), (0, LANE - QK_ROPE)).reshape(1, LANE)
    wp, w_uq_p = _prep_weights(W["w_in"], W["w_uq"])
    padl = lambda t: jnp.pad(t, ((0, 0), (0, LANE - t.shape[1])))
    dt_bias_p, a_log_p = padl(sp["dt_bias"]), padl(sp["a_log"])
    dskip_ch = jnp.repeat(sp["d_skip"], SSM_HEADDIM, axis=1)
    p_bf = p.astype(BF16)
    RW = 256

    (u_bf,) = _rows(lambda i, n, x, g: (x * _rstd(x) * g,), S, RW, [_row(x), _full(sp["mix_norm_pre"])],
                    [("row", D_MODEL, BF16)], "norm_pre")
    cqkv = ex.mm(u_bf, wp["qkv"], name="mm_qkv")
    z = ex.mm(u_bf, wp["z"], name="mm_z")
    xbc = ex.mm(u_bf, wp["xbc"], name="mm_xbc")
    gates = ex.mm(u_bf, wp["g"], name="mm_gates")
    kr_pad = ex.mm(u_bf, wp["kr"], name="mm_kr")
    dt_raw = ex.mm(u_bf, wp["dt"], name="mm_dt")

    def qkv_norm(i, n, cq, ckv, gq, gkv):
        return cq * _rstd(cq) * gq, ckv * _rstd(ckv) * gkv

    cqn, ckvn = _rows(qkv_norm, S, 512, [_row(cqkv, Q_LORA, 0), _row(cqkv, KV_LORA, 1), _full(sp["q_norm"]), _full(sp["kv_norm"])],
                      [("row", Q_LORA, BF16), ("row", KV_LORA, BF16)], "qkv_norm")
    q_raw = ex.mm(cqn, w_uq_p, name="mm_uq")
    kv = ex.mm(ckvn, W["w_ukv"], out_dtype=BF16, name="mm_ukv")
    q_bf, kp_bf = _rope_fwd(q_raw, kr_pad, pos_col, invf)
    attn, lse = _attn_fwd(q_bf, kv, kp_bf, tile)

    xbc_c = _conv_fwd(xbc, sp["conv_w"], sp["conv_b"])
    dt, cum, cumt = _dt_prep(dt_raw, dt_bias_p, a_log_p)
    cumt_g = cumt[:N_HEADS_SSM].reshape(SSM_GROUPS, HEADS_PER_GROUP, S)
    spread = _spread_matrix()
    y, states = _ssd_fwd(xbc_c, dt, cum, cumt_g, spread)

    GN = D_INNER // SSM_GROUPS

    def gated(y, xs, z, dsk):
        yt = y + dsk * xs
        sz = _sigmoid(z)
        return yt, sz, yt * (z * sz)

    def gated_norm(i, n, y, xs, z, dsk, gn):
        _, _, yg = gated(y, xs, z, dsk)
        parts = []
        for g in range(SSM_GROUPS):
            blk = yg[:, g * GN:(g + 1) * GN]
            parts.append(blk * _rstd(blk) * gn[:, g * GN:(g + 1) * GN])
        return (jnp.concatenate(parts, axis=1),)

    (ssm,) = _rows(gated_norm, S, 128, [_row(y), _row(xbc_c, D_INNER, 0), _row(z), _full(dskip_ch), _full(sp["ssm_norm"])],
                   [("row", D_INNER, BF16)], "gated_norm")

    a_o = ex.mm(attn, W["w_attn_o"], name="mm_attn_o")
    b_o = ex.mm(ssm, W["w_ssm_o"], name="mm_ssm_o")

    def mix(i, n, ga, gs, a, b):
        return (_sigmoid(ga) * a + _sigmoid(gs) * b,)

    (mixed,) = _rows(mix, S, RW, [_row(gates, D_MODEL, 0), _row(gates, D_MODEL, 1), _row(a_o), _row(b_o)],
                     [("row", D_MODEL, BF16)], "mix")
    m2 = ex.mm(mixed, W["w_out"], name="mm_out")

    def post(i, n, h, m, gpost, gpre):
        hn = h + m * _rstd(m) * gpost
        return hn, hn * _rstd(hn) * gpre

    h1, f_bf = _rows(post, S, RW, [_row(x), _row(m2), _full(sp["mix_norm_post"]), _full(sp["ffn_norm_pre"])],
                     [("row", D_MODEL, F32), ("row", D_MODEL, BF16)], "post_mix")
    ga, up, s_bf = _mm_swiglu(f_bf, W["w_gate"], W["w_up"])
    f2 = ex.mm(s_bf, W["w_down"], name="mm_down")
    h2, n3_bf = _rows(post, S, RW, [_row(h1), _row(f2), _full(sp["ffn_norm_post"]), _full(sp["ple_norm_pre"])],
                      [("row", D_MODEL, F32), ("row", D_MODEL, BF16)], "post_ffn")
    gpre = ex.mm(n3_bf, W["w_ple_gate"], name="mm_ple_gate")
    pe = ex.mm(p_bf, W["w_ple"], name="mm_ple")

    def ple_loss(i, n, h2, gpre, pe, tgt, gpost):
        gate = _sigmoid(gpre)
        e = pe * gate
        r = _rstd(e)
        diff = h2 + e * r * gpost - tgt
        loss = 0.5 * jnp.sum(jnp.mean(diff * diff, axis=1, keepdims=True))
        dh3 = diff * (1.0 / D_MODEL)
        de, dg_rows = _norm_bwd(e, r, gpost, dh3)
        return (jnp.full((1, LANE), loss, F32), dh3, de * gate, de * pe * gate * (1.0 - gate), _colsum(dg_rows))

    loss, dh3, dpe, dgpre, g_ple_post = _rows(
        ple_loss, S, 128, [_row(h2), _row(gpre), _row(pe), _row(target), _full(sp["ple_norm_post"])],
        [("acc", (1, LANE), F32), ("row", D_MODEL, F32), ("row", D_MODEL, BF16), ("row", D_MODEL, BF16),
         ("acc", (1, D_MODEL), F32)], "ple_loss")

    gs = {"ple_norm_post": g_ple_post}
    gw["w_ple"] = ex.mm(p_bf, dpe, ta=True, name="mmg_ple")
    gw["w_ple_gate"] = ex.mm(n3_bf, dgpre, ta=True, name="mmg_ple_gate")
    dn3 = ex.mm(dgpre, W["w_ple_gate"], tb=True, name="mmb_ple_gate")

    def post_bwd(i, n, h, m, dhn, dn, gpost, gpre):
        rm = _rstd(m)
        hn = h + m * rm * gpost
        dx, dgpre_rows = _norm_bwd(hn, _rstd(hn), gpre, dn)
        dhn_t = dhn + dx
        dm, dgpost_rows = _norm_bwd(m, rm, gpost, dhn_t)
        return dhn_t, dm, _colsum(dgpre_rows), _colsum(dgpost_rows)

    def run_post_bwd(h, m, dhn, dn, gpost, gpre, name):
        return _rows(post_bwd, S, 128, [_row(h), _row(m), _row(dhn), _row(dn), _full(gpost), _full(gpre)],
                     [("row", D_MODEL, F32), ("row", D_MODEL, BF16), ("acc", (1, D_MODEL), F32), ("acc", (1, D_MODEL), F32)], name)

    dh2, df2, gs["ple_norm_pre"], gs["ffn_norm_post"] = run_post_bwd(
        h1, f2, dh3, dn3, sp["ffn_norm_post"], sp["ple_norm_pre"], "post_ffn_bwd")
    gw["w_down"] = ex.mm(s_bf, df2, ta=True, name="mmg_down")
    def swiglu_bwd(ds, a, b):
        sa = _sigmoid(a)
        return ds * b * (sa * (1.0 + a * (1.0 - sa))), ds * (a * sa)

    dga, dup = ex.mm(df2, W["w_down"], tb=True, tn=512, epi=(swiglu_bwd, [ga, up], [BF16, BF16]), name="mmb_down")
    gw["w_gate"] = ex.mm(f_bf, dga, ta=True, name="mmg_gate")
    gw["w_up"] = ex.mm(f_bf, dup, ta=True, name="mmg_up")
    df = ex.mm(dga, W["w_gate"], tb=True, name="mmb_gate")
    df = ex.mm(dup, W["w_up"], tb=True, add=df, name="mmb_up")
    dh1, dm2, gs["ffn_norm_pre"], gs["mix_norm_post"] = run_post_bwd(
        x, m2, dh2, df, sp["mix_norm_post"], sp["ffn_norm_pre"], "post_mix_bwd")
    gw["w_out"] = ex.mm(mixed, dm2, ta=True, name="mmg_out")
    dmixed = ex.mm(dm2, W["w_out"], tb=True, out_dtype=BF16, name="mmb_out")

    def mix_bwd(i, n, ga, gs_, a, b, dm):
        sa, ss = _sigmoid(ga), _sigmoid(gs_)
        return dm * sa, dm * ss, jnp.concatenate([dm * a * sa * (1.0 - sa), dm * b * ss * (1.0 - ss)], axis=1)

    da_o, db_o, dgates = _rows(mix_bwd, S, RW, [_row(gates, D_MODEL, 0), _row(gates, D_MODEL, 1), _row(a_o), _row(b_o), _row(dmixed)],
                               [("row", D_MODEL, BF16), ("row", D_MODEL, BF16), ("row", 2 * D_MODEL, BF16)], "mix_bwd")
    gw["w_attn_o"] = ex.mm(attn, da_o, ta=True, name="mmg_attn_o")
    dattn = ex.mm(da_o, W["w_attn_o"], tb=True, out_dtype=BF16, name="mmb_attn_o")
    gw["w_ssm_o"] = ex.mm(ssm, db_o, ta=True, name="mmg_ssm_o")
    dssm = ex.mm(db_o, W["w_ssm_o"], tb=True, out_dtype=BF16, name="mmb_ssm_o")

    delta = _attn_delta(attn, dattn, tile)
    dq, dkv, dkp = _attn_bwd(q_bf, kv, kp_bf, dattn, lse, delta, tile)
    dq_raw, dkr = _rope_bwd(dq, dkp, pos_col, invf)
    g_uq_p = ex.mm(cqn, dq_raw, ta=True, name="mmg_uq")
    gw["w_uq"] = g_uq_p.reshape(Q_LORA, N_HEADS_MLA, HEAD_PAD)[:, :, :QK_DIM].reshape(Q_LORA, N_HEADS_MLA * QK_DIM)
    dcqn = ex.mm(dq_raw, w_uq_p, tb=True, name="mmb_uq")
    gw["w_ukv"] = ex.mm(ckvn, dkv, ta=True, name="mmg_ukv")
    dckvn = ex.mm(dkv, W["w_ukv"], tb=True, name="mmb_ukv")

    def qkv_norm_bwd(i, n, cq, ckv, dq_, dkv_, gq, gkv):
        dcq, gq_rows = _norm_bwd(cq, _rstd(cq), gq, dq_)
        dckv, gkv_rows = _norm_bwd(ckv, _rstd(ckv), gkv, dkv_)
        return jnp.concatenate([dcq, dckv], axis=1), _colsum(gq_rows), _colsum(gkv_rows)

    dcqkv, gs["q_norm"], gs["kv_norm"] = _rows(
        qkv_norm_bwd, S, 512, [_row(cqkv, Q_LORA, 0), _row(cqkv, KV_LORA, 1), _row(dcqn), _row(dckvn), _full(sp["q_norm"]), _full(sp["kv_norm"])],
        [("row", Q_LORA + KV_LORA, BF16), ("acc", (1, Q_LORA), F32), ("acc", (1, KV_LORA), F32)], "qkv_norm_bwd")

    def gated_norm_bwd(i, n, y, xs, z, dssm, dsk, gn):
        yt, sz, yg = gated(y, xs, z, dsk)
        dyg_parts, gn_parts = [], []
        for g in range(SSM_GROUPS):
            sl = slice(g * GN, (g + 1) * GN)
            blk = yg[:, sl]
            dblk, rows = _norm_bwd(blk, _rstd(blk), gn[:, sl], dssm[:, sl])
            dyg_parts.append(dblk)
            gn_parts.append(_colsum(rows))
        dyg = jnp.concatenate(dyg_parts, axis=1)
        dyt = dyg * (z * sz)
        dz = dyg * yt * (sz * (1.0 + z * (1.0 - sz)))
        return dyt, dz, jnp.concatenate(gn_parts, axis=1), _colsum(dyt * xs)

    dy, dz, gs["ssm_norm"], g_dskip_ch = _rows(
        gated_norm_bwd, S, 128, [_row(y), _row(xbc_c, D_INNER, 0), _row(z), _row(dssm), _full(dskip_ch), _full(sp["ssm_norm"])],
        [("row", D_INNER, F32), ("row", D_INNER, BF16), ("acc", (1, D_INNER), F32), ("acc", (1, D_INNER), F32)],
        "gated_norm_bwd")
    gs["d_skip"] = jnp.sum(g_dskip_ch.reshape(N_HEADS_SSM, SSM_HEADDIM), axis=1).reshape(1, N_HEADS_SSM)
    dxs, dbm, dcm, ddt_x, dcum = _ssd_bwd(xbc_c, dt, cum, cumt_g, spread, states, dy, dskip_ch)
    ddt_raw, g_dtb, g_alog = _dt_bwd(dt_raw, dt_bias_p, a_log_p, ddt_x, dcum)
    gs["dt_bias"] = g_dtb[:, :N_HEADS_SSM]
    gs["a_log"] = g_alog[:, :N_HEADS_SSM]
    dxbc, g_conv_w8, gs["conv_b"] = _conv_bwd(xbc, [dxs, dbm, dcm], sp["conv_w"], sp["conv_b"])
    gs["conv_w"] = g_conv_w8[:CONV_WIDTH]

    g_qkv = ex.mm(u_bf, dcqkv, ta=True, name="mmg_qkv")
    g_kr = ex.mm(u_bf, dkr, ta=True, name="mmg_kr")
    g_z = ex.mm(u_bf, dz, ta=True, name="mmg_z")
    g_xbc = ex.mm(u_bf, dxbc, ta=True, name="mmg_xbc")
    g_dt = ex.mm(u_bf, ddt_raw, ta=True, name="mmg_dt")
    g_g = ex.mm(u_bf, dgates, ta=True, name="mmg_gates")
    gw["w_in"] = [g_qkv, g_kr[:, :QK_ROPE], g_z, g_xbc, g_dt[:, :N_HEADS_SSM], g_g]
    du = _mm_sum_nt([(dcqkv, wp["qkv"]), (dkr, wp["kr"]), (ddt_raw, wp["dt"])], "mmb_small")
    du = ex.mm(dz, wp["z"], tb=True, add=du, name="mmb_z")
    du = ex.mm(dxbc, wp["xbc"], tb=True, add=du, name="mmb_xbc")
    du = ex.mm(dgates, wp["g"], tb=True, add=du, name="mmb_gates")

    def pre_bwd(i, n, x, du, dh, g):
        dx, rows = _norm_bwd(x, _rstd(x), g, du)
        return dh + dx, _colsum(rows)

    grad_x, gs["mix_norm_pre"] = _rows(pre_bwd, S, RW, [_row(x), _row(du), _row(dh1), _full(sp["mix_norm_pre"])],
                                       [("row", D_MODEL, F32), ("acc", (1, D_MODEL), F32)], "norm_pre_bwd")
    return loss, grad_x, gs


BIG = (
    ("w_in", (2048, 3872), 1), ("w_uq", (512, 768), 1), ("w_ukv", (512, 1024), 1), ("w_attn_o", (512, 2048), 0),
    ("w_ssm_o", (1024, 2048), 0), ("w_out", (512, 2048), 0), ("w_gate", (2048, 1408), 1), ("w_up", (2048, 1408), 1),
    ("w_down", (1408, 2048), 0), ("w_ple_gate", (512, 2048), 0), ("w_ple", (256, 512), 1),
)
SMALL = (
    ("mix_norm_pre", 2048), ("mix_norm_post", 2048), ("q_norm", 512), ("kv_norm", 512), ("conv_b", 6144), ("dt_bias", 64),
    ("a_log", 64), ("d_skip", 64), ("ssm_norm", 4096), ("ffn_norm_pre", 2048), ("ffn_norm_post", 2048),
    ("ple_norm_pre", 2048), ("ple_norm_post", 2048),
)
CONV_W_LEN = CONV_WIDTH * CONV_DIM
SMALL_ROWS = 384


def _place():
    return lax.axis_index("x"), lax.axis_index("y"), lax.axis_index("c")


def _flip(v, bit):
    return 1 - v if bit else v


def _alone(hook, name):
    n_in, n_out = len(hook.ins), len(hook.out_shapes)

    def body(*refs):
        start, finish = hook.make(refs[:n_in], refs[n_in:n_in + n_out], refs[n_in + n_out:])
        start()
        finish()

    return list(pl.pallas_call(
        body, name=name, out_shape=tuple(hook.out_shapes),
        in_specs=[pl.BlockSpec(memory_space=pl.ANY)] * n_in,
        out_specs=tuple(pl.BlockSpec(memory_space=pl.ANY) for _ in range(n_out)),
        scratch_shapes=[pltpu.SemaphoreType.DMA((s,)) for s in hook.sems],
        input_output_aliases=hook.aliases,
    )(*hook.ins))


def _simple(copies):
    def start():
        for cp in copies:
            cp.start()

    def finish():
        for cp in copies:
            cp.wait()

    return start, finish


def _gather_hook(shards):
    n = len(shards)

    def make(ins, outs, sems):
        send_sems, recv_sems, fwd_send_sems, fwd_recv_sems = sems
        x, y, c = _place()
        me = 2 * x + y
        far, near = [], []
        for a in range(n):
            half = shards[a].shape[0] // 2
            lo = pl.multiple_of(c * half, SUBLANE)
            for k in (1, 2, 3):
                px, py = _flip(x, k >> 1), _flip(y, k & 1)
                far.append(pltpu.make_async_remote_copy(
                    src_ref=ins[a].at[pl.ds(lo, half), :], dst_ref=outs[a].at[me, pl.ds(lo, half), :],
                    send_sem=send_sems.at[3 * a + k - 1], recv_sem=recv_sems.at[3 * a + k - 1],
                    device_id=(px, py, c), device_id_type=MESH_ID))
                got = outs[a].at[2 * px + py, pl.ds(lo, half), :]
                near.append(pltpu.make_async_remote_copy(
                    src_ref=got, dst_ref=got, send_sem=fwd_send_sems.at[3 * a + k - 1], recv_sem=fwd_recv_sems.at[3 * a + k - 1],
                    device_id=(x, y, 1 - c), device_id_type=MESH_ID))

        def start():
            for cp in far:
                cp.start()

        def finish():
            for cp, fwd in zip(far, near):
                cp.wait_recv()
                fwd.start()
            for cp, fwd in zip(far, near):
                cp.wait_send()
                fwd.wait()

        return start, finish

    return _Hook(shards, [jax.ShapeDtypeStruct((N_CHIPS, *s.shape), s.dtype) for s in shards], (3 * n,) * 4, make)


def _swap_hook(gs):
    n = len(gs)

    def make(ins, outs, sems):
        send_sems, recv_sems = sems
        x, y, c = _place()
        copies = []
        for a in range(n):
            half = gs[a].shape[1] // 2
            src = ins[a].at[:, pl.ds(pl.multiple_of((1 - c) * half, SUBLANE), half), :]
            copies.append(pltpu.make_async_remote_copy(
                src_ref=src, dst_ref=outs[a], send_sem=send_sems.at[a], recv_sem=recv_sems.at[a],
                device_id=(x, y, 1 - c), device_id_type=MESH_ID))
        return _simple(copies)

    return _Hook(gs, [jax.ShapeDtypeStruct((g.shape[0], g.shape[1] // 2, g.shape[2]), g.dtype) for g in gs], (n, n), make)


def _sum_rows_tile(rows, cols):
    return _tile(rows, max(2 * SUBLANE, (512 * 1024 // cols) // (2 * SUBLANE) * (2 * SUBLANE)), 2 * SUBLANE)


def _add_half(g, other, c, name):
    n, R, C = g.shape
    half = R // 2
    tr = _sum_rows_tile(half, C)
    nb = half // tr

    def body(c_ref, g_ref, o_ref, out_ref):
        out_ref[...] = (g_ref[...] + o_ref[...]).astype(out_ref.dtype)

    return pl.pallas_call(
        body, name=name,
        out_shape=jax.ShapeDtypeStruct((n, half, C), BF16),
        grid_spec=pltpu.PrefetchScalarGridSpec(
            num_scalar_prefetch=1, grid=(n, nb),
            in_specs=[pl.BlockSpec((1, tr, C), lambda j, i, c_ref: (j, c_ref[0] * nb + i, 0)),
                      pl.BlockSpec((1, tr, C), lambda j, i, c_ref: (j, i, 0))],
            out_specs=pl.BlockSpec((1, tr, C), lambda j, i, c_ref: (j, i, 0))),
        compiler_params=_params(("parallel", "parallel")),
    )(c, g, other)


def _scatter_hook(parts):
    n = len(parts)

    def make(ins, outs, sems):
        send_sems, recv_sems = sems
        x, y, c = _place()
        copies = []
        for a in range(n):
            for k in (1, 2, 3):
                px, py = _flip(x, k >> 1), _flip(y, k & 1)
                copies.append(pltpu.make_async_remote_copy(
                    src_ref=ins[a].at[2 * px + py], dst_ref=outs[a].at[k - 1], send_sem=send_sems.at[3 * a + k - 1],
                    recv_sem=recv_sems.at[3 * a + k - 1], device_id=(px, py, c), device_id_type=MESH_ID))
        return _simple(copies)

    return _Hook(parts, [jax.ShapeDtypeStruct((3, *p.shape[1:]), p.dtype) for p in parts], (3 * n, 3 * n), make)


def _add_chips(part, got, place, name):
    n, R, C = part.shape
    tr = _sum_rows_tile(R, C)
    nb = R // tr

    def body(place_ref, p_ref, g_ref, out_ref):
        out_ref[...] = ((p_ref[0].astype(F32) + g_ref[0].astype(F32)) + g_ref[1].astype(F32)) + g_ref[2].astype(F32)

    return pl.pallas_call(
        body, name=name,
        out_shape=jax.ShapeDtypeStruct((2 * R, C), F32),
        grid_spec=pltpu.PrefetchScalarGridSpec(
            num_scalar_prefetch=1, grid=(nb,),
            in_specs=[pl.BlockSpec((1, tr, C), lambda i, place_ref: (place_ref[0], i, 0)),
                      pl.BlockSpec((3, tr, C), lambda i, place_ref: (0, i, 0))],
            out_specs=pl.BlockSpec((tr, C), lambda i, place_ref: (place_ref[1] * nb + i, 0))),
        compiler_params=_params(("parallel",)),
    )(place, part, got)


def _join_hook(wholes):
    n = len(wholes)

    def make(ins, outs, sems):
        send_sems, recv_sems = sems
        x, y, c = _place()
        copies = []
        for a in range(n):
            half = wholes[a].shape[0] // 2
            rows = outs[a].at[pl.ds(pl.multiple_of(c * half, SUBLANE), half), :]
            copies.append(pltpu.make_async_remote_copy(
                src_ref=rows, dst_ref=rows, send_sem=send_sems.at[a], recv_sem=recv_sems.at[a],
                device_id=(x, y, 1 - c), device_id_type=MESH_ID))
        return _simple(copies)

    return _Hook(wholes, [jax.ShapeDtypeStruct(w.shape, w.dtype) for w in wholes], (n, n), make, aliases={a: a for a in range(n)})


def _allreduce_small(vec, name):
    R, C = vec.shape

    def body(v_ref, o_ref, buf, send_sems, recv_sems):
        x, y, c = _place()
        me = 4 * x + 2 * y + c
        buf[me] = v_ref[...]
        copies = []
        for k in range(1, N_DEV):
            peer = (_flip(x, (k >> 2) & 1), _flip(y, (k >> 1) & 1), _flip(c, k & 1))
            copies.append(pltpu.make_async_remote_copy(
                src_ref=v_ref, dst_ref=buf.at[me], send_sem=send_sems.at[k - 1], recv_sem=recv_sems.at[k - 1],
                device_id=peer, device_id_type=MESH_ID))
        for cp in copies:
            cp.start()
        for cp in copies:
            cp.wait()
        tot = buf[0]
        for d in range(1, N_DEV):
            tot = tot + buf[d]
        o_ref[...] = tot

    return pl.pallas_call(
        body, name=name,
        out_shape=jax.ShapeDtypeStruct((R, C), F32),
        in_specs=[pl.BlockSpec(memory_space=pltpu.VMEM)],
        out_specs=pl.BlockSpec(memory_space=pltpu.VMEM),
        scratch_shapes=[pltpu.VMEM((N_DEV, R, C), F32), pltpu.SemaphoreType.DMA((N_DEV - 1,)), pltpu.SemaphoreType.DMA((N_DEV - 1,))],
    )(vec)


def _unstack(gathered, shape, axis):
    if axis == 0:
        return gathered.reshape(N_CHIPS * shape[0], shape[1])
    return jnp.concatenate([gathered[j] for j in range(N_CHIPS)], axis=1)


def _stack(whole, shape, axis):
    if axis == 0:
        return whole.reshape(N_CHIPS, shape[0], shape[1])
    pieces = whole if isinstance(whole, (list, tuple)) else [whole]
    shards = []
    for j in range(N_CHIPS):
        lo, hi, off, cols = j * shape[1], (j + 1) * shape[1], 0, []
        for p in pieces:
            a, b = max(lo, off), min(hi, off + p.shape[1])
            if a < b:
                cols.append(p[:, a - off:b - off])
            off += p.shape[1]
        shards.append(cols[0] if len(cols) == 1 else jnp.concatenate(cols, axis=1))
    return jnp.stack(shards)


GATHER_FIRST = ("w_in", "w_uq", "w_ukv")
GATHER_IN = {"mm_z": ("w_attn_o", "w_ssm_o", "w_out"), "mm_xbc": ("w_gate", "w_up"), "mm_gates": ("w_down", "w_ple_gate", "w_ple")}
REDUCE = (
    (("w_ple", "w_ple_gate", "w_down"), "mmb_down", "mmg_gate", "mmg_up"),
    (("w_gate", "w_up"), "mmb_gate", "mmb_up", "mmg_out"),
    (("w_out", "w_attn_o", "w_ssm_o"), "mmb_ssm_o", "mmg_z", "mmg_xbc"),
    (("w_uq", "w_ukv"), "mmb_ukv", "mmg_gates", "mmb_z"),
    (("w_in",), "mmb_z", "mmb_xbc", "mmb_gates"),
)


class _Exchange:
    def __init__(self, shards, chip, core):
        self.shards, self.chip = shards, chip
        self.core_arr = core.reshape(1).astype(jnp.int32)
        self.place_arr = jnp.stack([chip, core]).astype(jnp.int32)
        self.shape = {n: (shape, axis) for n, shape, axis in BIG}
        self.whole, self.grads, self.reduced, self.pending, self.tails = {}, {}, {}, {}, 0
        hook, done = self._gather(GATHER_FIRST)
        done(_alone(hook, "gather_first"))
        for host, names in GATHER_IN.items():
            self._arm(host, *self._gather(names))

    def _arm(self, host, hook, done):
        self.pending.setdefault(host, []).append((hook, done))

    def _gather(self, names):
        shards = [self.shards[n].astype(BF16) for n in names]

        def done(outs):
            for n, s, g in zip(names, shards, outs):
                self.whole[n] = _unstack(lax.dynamic_update_slice(g, s[None], (self.chip, 0, 0)), *self.shape[n])

        return _gather_hook(shards), done

    def __getitem__(self, name):
        return self.whole[name]

    def __setitem__(self, name, grad):
        self.grads[name] = grad
        for names, swap_host, scatter_host, join_host in REDUCE:
            if name in names and all(n in self.grads for n in names):
                self._reduce(names, swap_host, scatter_host, join_host)

    def _reduce(self, names, swap_host, scatter_host, join_host):
        stacked = [_stack(self.grads[n], *self.shape[n]) for n in names]

        def joined(outs):
            for n, r in zip(names, outs):
                self.reduced[n] = r.reshape(1, *self.shape[n][0])

        def swapped(outs):
            parts = [_add_half(g, o, self.core_arr, "add_half_" + n) for n, g, o in zip(names, stacked, outs)]

            def scattered(gots):
                wholes = [_add_chips(q, o, self.place_arr, "add_chips_" + n) for n, q, o in zip(names, parts, gots)]
                self._arm(join_host, _join_hook(wholes), joined)

            self._arm(scatter_host, _scatter_hook(parts), scattered)

        self._arm(swap_host, _swap_hook(stacked), swapped)

    def _run(self, todo, call):
        hook = _merge_hooks([h for h, _ in todo])
        result, outs = call(hook)
        off = 0
        for h, done in todo:
            done(outs[off:off + len(h.out_shapes)])
            off += len(h.out_shapes)
        return result

    def mm(self, a, b, *, name, **kw):
        todo = self.pending.pop(name, None)
        if not todo:
            return _mm(a, b, name=name, **kw)
        return self._run(todo, lambda hook: _mm(a, b, name=name, hook=hook, **kw))

    def finish(self):
        while self.pending:
            todo = self.pending.pop(next(iter(self.pending)))
            self.tails += 1
            self._run(todo, lambda hook: (None, _alone(hook, "exchange_tail_%d" % self.tails)))
        return self.reduced


def kernel(x, p, positions, mix_norm_pre, mix_norm_post, w_in, q_norm, w_uq, kv_norm, w_ukv, conv_w, conv_b, dt_bias, a_log, d_skip, ssm_norm, w_attn_o, w_ssm_o, w_out, ffn_norm_pre, ffn_norm_post, w_gate, w_up, w_down, ple_norm_pre, ple_norm_post, w_ple_gate, w_ple, loss_target, m_mix_norm_pre, m_mix_norm_post, m_w_in, m_q_norm, m_w_uq, m_kv_norm, m_w_ukv, m_conv_w, m_conv_b, m_dt_bias, m_a_log, m_d_skip, m_ssm_norm, m_w_attn_o, m_w_ssm_o, m_w_out, m_ffn_norm_pre, m_ffn_norm_post, m_w_gate, m_w_up, m_w_down, m_ple_norm_pre, m_ple_norm_post, m_w_ple_gate, m_w_ple, v_mix_norm_pre, v_mix_norm_post, v_w_in, v_q_norm, v_w_uq, v_kv_norm, v_w_ukv, v_conv_w, v_conv_b, v_dt_bias, v_a_log, v_d_skip, v_ssm_norm, v_w_attn_o, v_w_ssm_o, v_w_out, v_ffn_norm_pre, v_ffn_norm_post, v_w_gate, v_w_up, v_w_down, v_ple_norm_pre, v_ple_norm_post, v_w_ple_gate, v_w_ple):
    given = dict(locals())
    names = [n for n, _, _ in BIG] + [n for n, _ in SMALL] + ["conv_w"]
    order = ["mix_norm_pre", "mix_norm_post", "w_in", "q_norm", "w_uq", "kv_norm", "w_ukv", "conv_w", "conv_b", "dt_bias", "a_log",
             "d_skip", "ssm_norm", "w_attn_o", "w_ssm_o", "w_out", "ffn_norm_pre", "ffn_norm_post", "w_gate", "w_up", "w_down",
             "ple_norm_pre", "ple_norm_post", "w_ple_gate", "w_ple"]
    assert sorted(names) == sorted(order)
    cx, cy, cc = _place()
    chip = 2 * cx + cy
    conv_cols = CONV_DIM // N_CHIPS

    ex = _Exchange({n: given[n][0] for n, _, _ in BIG}, chip, cc)
    own = jnp.where(cc == 0, conv_w[0], 0.0)
    conv_vec = lax.dynamic_update_slice(jnp.zeros((CONV_WIDTH, CONV_DIM), F32), own, (0, chip * conv_cols))
    conv_full = _allreduce_small(conv_vec.reshape(CONV_W_LEN // LANE, LANE), "gather_conv_w").reshape(CONV_WIDTH, CONV_DIM)
    sp = {n: given[n] for n, _ in SMALL}
    sp["conv_w"] = conv_full

    loss_part, grad_x, gs = _local_step(x[0], p[0, 0], positions[0], ex, sp, loss_target[0])

    g_big = ex.finish()

    small_parts = [gs[n] for n, _ in SMALL] + [gs["conv_w"], loss_part[:, :1]]
    small_vec = jnp.concatenate([t.reshape(-1) for t in small_parts])
    small_vec = jnp.pad(small_vec, (0, SMALL_ROWS * LANE - small_vec.shape[0])).reshape(SMALL_ROWS, LANE)
    small_sum = _allreduce_small(small_vec, "allreduce_small").reshape(-1)
    g_small, off = {}, 0
    for n, length in SMALL:
        g_small[n] = small_sum[off:off + length].reshape(1, length)
        off += length
    g_conv = small_sum[off:off + CONV_W_LEN].reshape(CONV_WIDTH, CONV_DIM)
    g_small["conv_w"] = lax.dynamic_slice(g_conv, (0, chip * conv_cols), (CONV_WIDTH, conv_cols)).reshape(1, CONV_WIDTH, conv_cols)
    loss = small_sum[off + CONV_W_LEN]

    grads, deltas, new_m, new_v = [], [], [], []
    for n in order:
        g = g_big[n] if n in g_big else g_small[n]
        d, m_, v_ = _adamw(given[n], g, given["m_" + n], given["v_" + n], "adamw_" + n)
        grads.append(g)
        deltas.append(d)
        new_m.append(m_)
        new_v.append(v_)
    return (loss, grad_x.reshape(x.shape), *grads, *deltas, *new_m, *new_v)
```
